```python
import math
import jax, jax.numpy as jnp
from jax import lax
import numpy as np

D_MODEL = 1024
BATCH = 16
SEQ = 4096
DEPTH = 1

CHUNK = 64
D_MIX = D_MODEL
LRU_WIDTH = D_MIX // 2
LRU_BLOCKS = 8
LRU_BLOCK = LRU_WIDTH // LRU_BLOCKS
CONV_WIDTH = 4
LRU_C = 8.0
SB_WIDTH = D_MIX - LRU_WIDTH
SB_HEADS = 8
SB_HEAD_DIM = SB_WIDTH // SB_HEADS
Q_BLOCK = 128
D_FF = 4 * D_MODEL
IN_COLS = 2 * LRU_WIDTH + 3 * SB_WIDTH
EPS = 1e-6

kernel_name = "hymba_rglru_stickbreaking_block"


def _rmsnorm(x, g):
    xf = x.astype(jnp.float32)
    y = xf * lax.rsqrt(jnp.mean(xf * xf, axis=-1, keepdims=True) + EPS)
    return (y * g.astype(jnp.float32)).astype(x.dtype)


def _causal_depthwise_conv(u, w, b):
    rhs = w[:, None, :].astype(u.dtype)
    y = lax.conv_general_dilated(
        u, rhs, window_strides=(1,), padding=[(CONV_WIDTH - 1, 0)],
        dimension_numbers=("NWC", "WIO", "NWC"), feature_group_count=u.shape[-1])
    return y + b.astype(u.dtype)


def _rg_lru(u, w_a, b_a, w_x, b_x, lam):
    B, S, _ = u.shape
    uf = u.astype(jnp.float32)
    ub = uf.reshape(B, S, LRU_BLOCKS, LRU_BLOCK)
    r = jax.nn.sigmoid(jnp.einsum('bsgi,gij->bsgj', ub, w_a.astype(jnp.float32))
                       + b_a.astype(jnp.float32)).reshape(B, S, LRU_WIDTH)
    i = jax.nn.sigmoid(jnp.einsum('bsgi,gij->bsgj', ub, w_x.astype(jnp.float32))
                       + b_x.astype(jnp.float32)).reshape(B, S, LRU_WIDTH)
    log_a = -LRU_C * r * jax.nn.softplus(-lam.astype(jnp.float32))
    a = jnp.exp(log_a)
    bterm = jnp.sqrt(-jnp.expm1(2.0 * log_a)) * (i * uf)

    def combine(left, right):
        a1, b1 = left
        a2, b2 = right
        return a1 * a2, a2 * b1 + b2

    _, h = lax.associative_scan(combine, (a, bterm), axis=1)
    return h.astype(u.dtype)


def _stick_breaking(q, k, v):
    B, S, H, Dh = q.shape
    scale = 1.0 / math.sqrt(Dh)
    qh = jnp.transpose(q, (0, 2, 1, 3)).astype(jnp.float32)
    kh = jnp.transpose(k, (0, 2, 1, 3)).astype(jnp.float32)
    vh = jnp.transpose(v, (0, 2, 1, 3)).astype(jnp.float32)
    outs = []
    for blk in range(S // Q_BLOCK):
        start = blk * Q_BLOCK
        end = start + Q_BLOCK
        qb = qh[:, :, start:end]
        kb = kh[:, :, :end]
        vb = vh[:, :, :end]
        z = jnp.einsum('bhqd,bhkd->bhqk', qb, kb) * scale
        t_idx = jnp.arange(start, end)[:, None]
        s_idx = jnp.arange(end)[None, :]
        mask = s_idx < t_idx
        log_1m_beta = jnp.where(mask, jax.nn.log_sigmoid(-z), 0.0)
        surv = lax.cumsum(log_1m_beta, axis=3, reverse=True) - log_1m_beta
        att = jnp.where(mask, jnp.exp(jax.nn.log_sigmoid(z) + surv), 0.0)
        outs.append(jnp.einsum('bhqk,bhkd->bhqd', att, vb))
    o = jnp.concatenate(outs, axis=2)
    return jnp.transpose(o, (0, 2, 1, 3)).reshape(B, S, H * Dh).astype(q.dtype)


def _fwd_setup_inputs(seed: int = 0) -> dict:
    key = jax.random.key(seed)
    ks = jax.random.split(key, 20)
    f32 = jnp.float32
    x = jax.random.normal(ks[0], (BATCH, SEQ, D_MODEL), f32)
    norm1_g = 1.0 + 0.01 * jax.random.normal(ks[1], (DEPTH, D_MODEL), f32)
    w_in = jax.random.normal(ks[2], (DEPTH, D_MODEL, IN_COLS), f32) * D_MODEL ** -0.5
    conv_w = jax.random.normal(ks[3], (DEPTH, CONV_WIDTH, LRU_WIDTH), f32) * CONV_WIDTH ** -0.5
    conv_b = 0.01 * jax.random.normal(ks[4], (DEPTH, LRU_WIDTH), f32)
    lru_w_a = jax.random.normal(ks[5], (DEPTH, LRU_BLOCKS, LRU_BLOCK, LRU_BLOCK), f32) * LRU_BLOCK ** -0.5
    lru_b_a = 0.01 * jax.random.normal(ks[6], (DEPTH, LRU_BLOCKS, LRU_BLOCK), f32)
    lru_w_x = jax.random.normal(ks[7], (DEPTH, LRU_BLOCKS, LRU_BLOCK, LRU_BLOCK), f32) * LRU_BLOCK ** -0.5
    lru_b_x = 0.01 * jax.random.normal(ks[8], (DEPTH, LRU_BLOCKS, LRU_BLOCK), f32)
    u = jax.random.uniform(ks[9], (DEPTH, LRU_WIDTH), f32, 0.9, 0.999)
    a0 = u ** (1.0 / LRU_C)
    lru_lambda = jnp.log(a0) - jnp.log1p(-a0)
    lru_out_g = 1.0 + 0.01 * jax.random.normal(ks[10], (DEPTH, LRU_WIDTH), f32)
    sb_out_g = 1.0 + 0.01 * jax.random.normal(ks[11], (DEPTH, SB_WIDTH), f32)
    w_out = jax.random.normal(ks[12], (DEPTH, D_MIX, D_MODEL), f32) * D_MIX ** -0.5
    norm2_g = 1.0 + 0.01 * jax.random.normal(ks[13], (DEPTH, D_MODEL), f32)
    w_up = jax.random.normal(ks[14], (DEPTH, D_MODEL, D_FF), f32) * D_MODEL ** -0.5
    w_down = jax.random.normal(ks[15], (DEPTH, D_FF, D_MODEL), f32) * D_FF ** -0.5
    final_g = 1.0 + 0.01 * jax.random.normal(ks[16], (D_MODEL,), f32)
    return {"x": x, "norm1_g": norm1_g, "w_in": w_in, "conv_w": conv_w, "conv_b": conv_b,
            "lru_w_a": lru_w_a, "lru_b_a": lru_b_a, "lru_w_x": lru_w_x, "lru_b_x": lru_b_x,
            "lru_lambda": lru_lambda, "lru_out_g": lru_out_g, "sb_out_g": sb_out_g,
            "w_out": w_out, "norm2_g": norm2_g, "w_up": w_up, "w_down": w_down,
            "final_g": final_g}


def _fwd_reference(x, norm1_g, w_in, conv_w, conv_b, lru_w_a, lru_b_a, lru_w_x, lru_b_x,
              lru_lambda, lru_out_g, sb_out_g, w_out, norm2_g, w_up, w_down, final_g):
    B, S, _ = x.shape
    split_pts = [LRU_WIDTH, 2 * LRU_WIDTH, 2 * LRU_WIDTH + SB_WIDTH, 2 * LRU_WIDTH + 2 * SB_WIDTH]
    h = x
    for layer in range(DEPTH):
        xn = _rmsnorm(h, norm1_g[layer])
        proj = xn @ w_in[layer].astype(xn.dtype)
        x_lru, g_lru, q, k, v = jnp.split(proj, split_pts, axis=-1)
        c = _causal_depthwise_conv(x_lru, conv_w[layer], conv_b[layer])
        y_lru = _rg_lru(c, lru_w_a[layer], lru_b_a[layer], lru_w_x[layer], lru_b_x[layer],
                        lru_lambda[layer]) * jax.nn.gelu(g_lru, approximate=True)
        y_sb = _stick_breaking(q.reshape(B, S, SB_HEADS, SB_HEAD_DIM),
                               k.reshape(B, S, SB_HEADS, SB_HEAD_DIM),
                               v.reshape(B, S, SB_HEADS, SB_HEAD_DIM))
        mix = jnp.concatenate([_rmsnorm(y_lru, lru_out_g[layer]),
                               _rmsnorm(y_sb, sb_out_g[layer])], axis=-1)
        h = h + mix @ w_out[layer].astype(mix.dtype)
        hn = _rmsnorm(h, norm2_g[layer])
        up = jax.nn.relu(hn @ w_up[layer].astype(hn.dtype))
        h = h + (up * up) @ w_down[layer].astype(up.dtype)
    return _rmsnorm(h, final_g)


import jax as _jax
import jax.numpy as _jnp

TWIN_FORMAT = 'train_step'
FWD_PARAMS = ['x', 'norm1_g', 'w_in', 'conv_w', 'conv_b', 'lru_w_a', 'lru_b_a', 'lru_w_x', 'lru_b_x', 'lru_lambda', 'lru_out_g', 'sb_out_g', 'w_out', 'norm2_g', 'w_up', 'w_down', 'final_g']
TWIN_WEIGHTS = ['norm1_g', 'w_in', 'conv_w', 'conv_b', 'lru_w_a', 'lru_b_a', 'lru_w_x', 'lru_b_x', 'lru_lambda', 'lru_out_g', 'sb_out_g', 'w_out', 'norm2_g', 'w_up', 'w_down', 'final_g']
TWIN_DIFF_INPUT = 'x'
TWIN_INPUTS = ['x', 'norm1_g', 'w_in', 'conv_w', 'conv_b', 'lru_w_a', 'lru_b_a', 'lru_w_x', 'lru_b_x', 'lru_lambda', 'lru_out_g', 'sb_out_g', 'w_out', 'norm2_g', 'w_up', 'w_down', 'final_g', 'loss_target', 'm_norm1_g', 'm_w_in', 'm_conv_w', 'm_conv_b', 'm_lru_w_a', 'm_lru_b_a', 'm_lru_w_x', 'm_lru_b_x', 'm_lru_lambda', 'm_lru_out_g', 'm_sb_out_g', 'm_w_out', 'm_norm2_g', 'm_w_up', 'm_w_down', 'm_final_g', 'v_norm1_g', 'v_w_in', 'v_conv_w', 'v_conv_b', 'v_lru_w_a', 'v_lru_b_a', 'v_lru_w_x', 'v_lru_b_x', 'v_lru_lambda', 'v_lru_out_g', 'v_sb_out_g', 'v_w_out', 'v_norm2_g', 'v_w_up', 'v_w_down', 'v_final_g']
TWIN_OUTPUTS = ['loss', 'grad_x', 'grad_norm1_g', 'grad_w_in', 'grad_conv_w', 'grad_conv_b', 'grad_lru_w_a', 'grad_lru_b_a', 'grad_lru_w_x', 'grad_lru_b_x', 'grad_lru_lambda', 'grad_lru_out_g', 'grad_sb_out_g', 'grad_w_out', 'grad_norm2_g', 'grad_w_up', 'grad_w_down', 'grad_final_g', 'delta_norm1_g', 'delta_w_in', 'delta_conv_w', 'delta_conv_b', 'delta_lru_w_a', 'delta_lru_b_a', 'delta_lru_w_x', 'delta_lru_b_x', 'delta_lru_lambda', 'delta_lru_out_g', 'delta_sb_out_g', 'delta_w_out', 'delta_norm2_g', 'delta_w_up', 'delta_w_down', 'delta_final_g', 'new_m_norm1_g', 'new_m_w_in', 'new_m_conv_w', 'new_m_conv_b', 'new_m_lru_w_a', 'new_m_lru_b_a', 'new_m_lru_w_x', 'new_m_lru_b_x', 'new_m_lru_lambda', 'new_m_lru_out_g', 'new_m_sb_out_g', 'new_m_w_out', 'new_m_norm2_g', 'new_m_w_up', 'new_m_w_down', 'new_m_final_g', 'new_v_norm1_g', 'new_v_w_in', 'new_v_conv_w', 'new_v_conv_b', 'new_v_lru_w_a', 'new_v_lru_b_a', 'new_v_lru_w_x', 'new_v_lru_b_x', 'new_v_lru_lambda', 'new_v_lru_out_g', 'new_v_sb_out_g', 'new_v_w_out', 'new_v_norm2_g', 'new_v_w_up', 'new_v_w_down', 'new_v_final_g']
TWIN_LEAF_KINDS = {'loss': 'loss', 'grad_x': 'grad_x', 'grad_norm1_g': 'grad_w', 'grad_w_in': 'grad_w', 'grad_conv_w': 'grad_w', 'grad_conv_b': 'grad_w', 'grad_lru_w_a': 'grad_w', 'grad_lru_b_a': 'grad_w', 'grad_lru_w_x': 'grad_w', 'grad_lru_b_x': 'grad_w', 'grad_lru_lambda': 'grad_w', 'grad_lru_out_g': 'grad_w', 'grad_sb_out_g': 'grad_w', 'grad_w_out': 'grad_w', 'grad_norm2_g': 'grad_w', 'grad_w_up': 'grad_w', 'grad_w_down': 'grad_w', 'grad_final_g': 'grad_w', 'delta_norm1_g': 'delta_w', 'delta_w_in': 'delta_w', 'delta_conv_w': 'delta_w', 'delta_conv_b': 'delta_w', 'delta_lru_w_a': 'delta_w', 'delta_lru_b_a': 'delta_w', 'delta_lru_w_x': 'delta_w', 'delta_lru_b_x': 'delta_w', 'delta_lru_lambda': 'delta_w', 'delta_lru_out_g': 'delta_w', 'delta_sb_out_g': 'delta_w', 'delta_w_out': 'delta_w', 'delta_norm2_g': 'delta_w', 'delta_w_up': 'delta_w', 'delta_w_down': 'delta_w', 'delta_final_g': 'delta_w', 'new_m_norm1_g': 'new_m', 'new_m_w_in': 'new_m', 'new_m_conv_w': 'new_m', 'new_m_conv_b': 'new_m', 'new_m_lru_w_a': 'new_m', 'new_m_lru_b_a': 'new_m', 'new_m_lru_w_x': 'new_m', 'new_m_lru_b_x': 'new_m', 'new_m_lru_lambda': 'new_m', 'new_m_lru_out_g': 'new_m', 'new_m_sb_out_g': 'new_m', 'new_m_w_out': 'new_m', 'new_m_norm2_g': 'new_m', 'new_m_w_up': 'new_m', 'new_m_w_down': 'new_m', 'new_m_final_g': 'new_m', 'new_v_norm1_g': 'new_v', 'new_v_w_in': 'new_v', 'new_v_conv_w': 'new_v', 'new_v_conv_b': 'new_v', 'new_v_lru_w_a': 'new_v', 'new_v_lru_b_a': 'new_v', 'new_v_lru_w_x': 'new_v', 'new_v_lru_b_x': 'new_v', 'new_v_lru_lambda': 'new_v', 'new_v_lru_out_g': 'new_v', 'new_v_sb_out_g': 'new_v', 'new_v_w_out': 'new_v', 'new_v_norm2_g': 'new_v', 'new_v_w_up': 'new_v', 'new_v_w_down': 'new_v', 'new_v_final_g': 'new_v'}


def _forward(args):
    return _fwd_reference(*[args[k] for k in FWD_PARAMS])


def _output_shape():
    out = _jax.eval_shape(lambda: _forward(_fwd_setup_inputs(0)))
    return out.shape, out.dtype

N_MICROBATCH = 1
ADAM_LR = 0.001
ADAM_B1 = 0.9
ADAM_B2 = 0.999
ADAM_EPS = 1e-08
ADAM_WD = 0.01
ADAM_STEP = 10
PER_EXAMPLE_BATCH_AXIS = {'x': 0, 'loss_target': 0}
SHARED_INPUTS = []
_WEIGHT_DTYPES = {'norm1_g': _jnp.float32, 'w_in': _jnp.float32, 'conv_w': _jnp.float32, 'conv_b': _jnp.float32, 'lru_w_a': _jnp.float32, 'lru_b_a': _jnp.float32, 'lru_w_x': _jnp.float32, 'lru_b_x': _jnp.float32, 'lru_lambda': _jnp.float32, 'lru_out_g': _jnp.float32, 'sb_out_g': _jnp.float32, 'w_out': _jnp.float32, 'norm2_g': _jnp.float32, 'w_up': _jnp.float32, 'w_down': _jnp.float32, 'final_g': _jnp.float32}
MOMENT_SCALE = {'norm1_g': 2.536153e-01, 'w_in': 1.585342e-01, 'conv_w': 2.082862e-01, 'conv_b': 2.560967e+00, 'lru_w_a': 7.214561e-02, 'lru_b_a': 6.923199e-02, 'lru_w_x': 1.316532e-01, 'lru_b_x': 7.237876e-02, 'lru_lambda': 1.025711e-01, 'lru_out_g': 1.913923e-01, 'sb_out_g': 2.047469e-01, 'w_out': 1.898430e-01, 'norm2_g': 1.888694e-01, 'w_up': 9.497285e-02, 'w_down': 1.668075e-01, 'final_g': 6.454777e+01}


def _to_microbatches(a, axis):
    t = _jnp.moveaxis(a, axis, 0)
    t = t.reshape((N_MICROBATCH, t.shape[0] // N_MICROBATCH) + t.shape[1:])
    return _jnp.moveaxis(t, 1, axis + 1)


def setup_inputs(seed: int = 0) -> dict:
    inp = _fwd_setup_inputs(seed)
    key = _jax.random.fold_in(_jax.random.key(seed), 7919)
    shape, _ = _output_shape()
    out = dict(inp)
    out["loss_target"] = _jax.random.normal(_jax.random.fold_in(key, 0), shape, _jnp.float32)
    for i, name in enumerate(TWIN_WEIGHTS):
        w = inp[name].astype(_jnp.float32)
        if MOMENT_SCALE is None:
            s = _jnp.sqrt(_jnp.mean(_jnp.square(w)) + 1e-30)
        else:
            s = MOMENT_SCALE[name]
        km, kv = _jax.random.split(_jax.random.fold_in(key, i + 1))
        out[name] = w
        out["m_" + name] = s * _jax.random.normal(km, w.shape, _jnp.float32)
        out["v_" + name] = (s * s) * _jax.random.uniform(kv, w.shape, _jnp.float32, 0.5, 1.5)
    if N_MICROBATCH > 1:
        for name, axis in PER_EXAMPLE_BATCH_AXIS.items():
            out[name] = _to_microbatches(out[name], axis)
    return {'x': out['x'], 'norm1_g': out['norm1_g'], 'w_in': out['w_in'], 'conv_w': out['conv_w'], 'conv_b': out['conv_b'], 'lru_w_a': out['lru_w_a'], 'lru_b_a': out['lru_b_a'], 'lru_w_x': out['lru_w_x'], 'lru_b_x': out['lru_b_x'], 'lru_lambda': out['lru_lambda'], 'lru_out_g': out['lru_out_g'], 'sb_out_g': out['sb_out_g'], 'w_out': out['w_out'], 'norm2_g': out['norm2_g'], 'w_up': out['w_up'], 'w_down': out['w_down'], 'final_g': out['final_g'], 'loss_target': out['loss_target'], 'm_norm1_g': out['m_norm1_g'], 'm_w_in': out['m_w_in'], 'm_conv_w': out['m_conv_w'], 'm_conv_b': out['m_conv_b'], 'm_lru_w_a': out['m_lru_w_a'], 'm_lru_b_a': out['m_lru_b_a'], 'm_lru_w_x': out['m_lru_w_x'], 'm_lru_b_x': out['m_lru_b_x'], 'm_lru_lambda': out['m_lru_lambda'], 'm_lru_out_g': out['m_lru_out_g'], 'm_sb_out_g': out['m_sb_out_g'], 'm_w_out': out['m_w_out'], 'm_norm2_g': out['m_norm2_g'], 'm_w_up': out['m_w_up'], 'm_w_down': out['m_w_down'], 'm_final_g': out['m_final_g'], 'v_norm1_g': out['v_norm1_g'], 'v_w_in': out['v_w_in'], 'v_conv_w': out['v_conv_w'], 'v_conv_b': out['v_conv_b'], 'v_lru_w_a': out['v_lru_w_a'], 'v_lru_b_a': out['v_lru_b_a'], 'v_lru_w_x': out['v_lru_w_x'], 'v_lru_b_x': out['v_lru_b_x'], 'v_lru_lambda': out['v_lru_lambda'], 'v_lru_out_g': out['v_lru_out_g'], 'v_sb_out_g': out['v_sb_out_g'], 'v_w_out': out['v_w_out'], 'v_norm2_g': out['v_norm2_g'], 'v_w_up': out['v_w_up'], 'v_w_down': out['v_w_down'], 'v_final_g': out['v_final_g']}


def _loss(weights, diff, rest, loss_target):
    with _jax.named_scope("forward"):
        args = {**rest, TWIN_DIFF_INPUT: diff, **{k: w.astype(_WEIGHT_DTYPES[k]) for k, w in weights.items()}}
        y = _forward(args)
    with _jax.named_scope("loss_head"):
        err = _jnp.square(y.astype(_jnp.float32) - loss_target)
        return 0.5 * _jnp.sum(_jnp.mean(err, axis=-1)) if err.ndim else 0.5 * err


def _adamw(w, g, m, v):
    m = ADAM_B1 * m + (1.0 - ADAM_B1) * g
    v = ADAM_B2 * v + (1.0 - ADAM_B2) * _jnp.square(g)
    m_hat = m / (1.0 - ADAM_B1 ** ADAM_STEP)
    v_hat = v / (1.0 - ADAM_B2 ** ADAM_STEP)
    delta = -ADAM_LR * (m_hat / (_jnp.sqrt(v_hat) + ADAM_EPS) + ADAM_WD * w)
    return delta, m, v


def reference(x, norm1_g, w_in, conv_w, conv_b, lru_w_a, lru_b_a, lru_w_x, lru_b_x, lru_lambda, lru_out_g, sb_out_g, w_out, norm2_g, w_up, w_down, final_g, loss_target, m_norm1_g, m_w_in, m_conv_w, m_conv_b, m_lru_w_a, m_lru_b_a, m_lru_w_x, m_lru_b_x, m_lru_lambda, m_lru_out_g, m_sb_out_g, m_w_out, m_norm2_g, m_w_up, m_w_down, m_final_g, v_norm1_g, v_w_in, v_conv_w, v_conv_b, v_lru_w_a, v_lru_b_a, v_lru_w_x, v_lru_b_x, v_lru_lambda, v_lru_out_g, v_sb_out_g, v_w_out, v_norm2_g, v_w_up, v_w_down, v_final_g):
    given = dict(x=x, norm1_g=norm1_g, w_in=w_in, conv_w=conv_w, conv_b=conv_b, lru_w_a=lru_w_a, lru_b_a=lru_b_a, lru_w_x=lru_w_x, lru_b_x=lru_b_x, lru_lambda=lru_lambda, lru_out_g=lru_out_g, sb_out_g=sb_out_g, w_out=w_out, norm2_g=norm2_g, w_up=w_up, w_down=w_down, final_g=final_g, loss_target=loss_target, m_norm1_g=m_norm1_g, m_w_in=m_w_in, m_conv_w=m_conv_w, m_conv_b=m_conv_b, m_lru_w_a=m_lru_w_a, m_lru_b_a=m_lru_b_a, m_lru_w_x=m_lru_w_x, m_lru_b_x=m_lru_b_x, m_lru_lambda=m_lru_lambda, m_lru_out_g=m_lru_out_g, m_sb_out_g=m_sb_out_g, m_w_out=m_w_out, m_norm2_g=m_norm2_g, m_w_up=m_w_up, m_w_down=m_w_down, m_final_g=m_final_g, v_norm1_g=v_norm1_g, v_w_in=v_w_in, v_conv_w=v_conv_w, v_conv_b=v_conv_b, v_lru_w_a=v_lru_w_a, v_lru_b_a=v_lru_b_a, v_lru_w_x=v_lru_w_x, v_lru_b_x=v_lru_b_x, v_lru_lambda=v_lru_lambda, v_lru_out_g=v_lru_out_g, v_sb_out_g=v_sb_out_g, v_w_out=v_w_out, v_norm2_g=v_norm2_g, v_w_up=v_w_up, v_w_down=v_w_down, v_final_g=v_final_g)
    weights = {n: given[n] for n in TWIN_WEIGHTS}
    shared = {n: given[n] for n in SHARED_INPUTS}
    per_example = {n: given[n] for n in ['x']}
    grad_fn = _jax.value_and_grad(_loss, argnums=(0, 1))

    def one_microbatch(ex, loss_target):
        ex = dict(ex)
        diff = ex.pop(TWIN_DIFF_INPUT)
        return grad_fn(weights, diff, {**shared, **ex}, loss_target)

    if N_MICROBATCH == 1:
        loss, (grad_w, grad_x) = one_microbatch(per_example, given["loss_target"])
    else:
        def body(carry, xs):
            loss_sum, grad_sum = carry
            l_k, (gw_k, gx_k) = one_microbatch(xs[0], xs[1])
            with _jax.named_scope("update"):
                return (loss_sum + l_k, _jax.tree.map(_jnp.add, grad_sum, gw_k)), gx_k

        init = (_jnp.zeros((), _jnp.float32), _jax.tree.map(_jnp.zeros_like, weights))
        (loss, grad_w), grad_x = _jax.lax.scan(body, init, (per_example, given["loss_target"]))
    with _jax.named_scope("update"):
        delta_w, new_m, new_v = {}, {}, {}
        for n in TWIN_WEIGHTS:
            delta_w[n], new_m[n], new_v[n] = _adamw(weights[n], grad_w[n], given["m_" + n], given["v_" + n])
    return (loss, grad_x, *[grad_w[n] for n in TWIN_WEIGHTS], *[delta_w[n] for n in TWIN_WEIGHTS],
            *[new_m[n] for n in TWIN_WEIGHTS], *[new_v[n] for n in TWIN_WEIGHTS])
```

```python
import functools
import math

import jax
import jax.numpy as jnp
from jax import lax
from jax.experimental import pallas as pl
from jax.experimental.pallas import tpu as pltpu

F32, BF16 = jnp.float32, jnp.bfloat16
MESH = pl.DeviceIdType.MESH

D_MODEL = 1024
LRU_WIDTH = 512
SB_WIDTH = 512
DH = 64
IN_COLS = 2 * LRU_WIDTH + 3 * SB_WIDTH
D_FF = 4 * D_MODEL
CONV_WIDTH = 4
LRU_C = 8.0
EPS = 1e-6
N_CHIPS = 4
N_DEV = 8
LANES = 128
SUBLANES = 8
TQ = 128
ATT_SCALE = 1.0 / math.sqrt(DH)
VMEM_LIMIT = 52 * 1024 * 1024

ADAM_LR, ADAM_B1, ADAM_B2, ADAM_EPS, ADAM_WD, ADAM_STEP = 0.001, 0.9, 0.999, 1e-08, 0.01, 10

_GELU_K = math.sqrt(2.0 / math.pi)
_GELU_C = 0.044715


def _cp(sem):
    return pltpu.CompilerParams(dimension_semantics=sem, vmem_limit_bytes=VMEM_LIMIT)


def _dot(a, b):
    return jnp.dot(a, b, preferred_element_type=F32)


def _dot_nt(a, b):
    return lax.dot_general(a, b, (((1,), (1,)), ((), ())), preferred_element_type=F32)


def _dot_tn(a, b):
    return lax.dot_general(a, b, (((0,), (0,)), ((), ())), preferred_element_type=F32)


def _rstd(x):
    return lax.rsqrt(jnp.mean(x * x, axis=-1, keepdims=True) + EPS)


def _rms_bwd(x, g, dy):
    r = _rstd(x)
    gd = g * dy
    dx = r * gd - x * (r * r * r) * jnp.mean(x * gd, axis=-1, keepdims=True)
    return dx, jnp.sum(dy * x * r, axis=0, keepdims=True)


def _sigmoid(x):
    return 1.0 / (1.0 + jnp.exp(-x))


def _softplus(x):
    return jnp.maximum(x, 0.0) + jnp.log(1.0 + jnp.exp(-jnp.abs(x)))


def _neg_expm1(x):
    series = -x * (1.0 + x * (0.5 + x * (1.0 / 6.0 + x * (1.0 / 24.0))))
    return jnp.where(x > -0.01, series, 1.0 - jnp.exp(x))


def _gelu(g):
    t = jnp.tanh(_GELU_K * (g + _GELU_C * g * g * g))
    return 0.5 * g * (1.0 + t), t


def _gelu_grad(g, t):
    return 0.5 * (1.0 + t) + 0.5 * g * (1.0 - t * t) * _GELU_K * (1.0 + 3.0 * _GELU_C * g * g)


def _rows(shape):
    return lax.broadcasted_iota(jnp.int32, shape, 0)


def _shift_down(x, s, fill):
    return jnp.where(_rows(x.shape) >= s, pltpu.roll(x, s, 0), fill)


def _shift_up(x, s, fill):
    n = x.shape[0]
    return jnp.where(_rows(x.shape) < n - s, pltpu.roll(x, n - s, 0), fill)


def _row_of(x, idx):
    return jnp.sum(jnp.where(_rows(x.shape) == idx, x, 0.0), axis=0, keepdims=True)


def _matmul(a, b, dims, out_dtype, name, tm, tn, tk, split_cols=False):
    if dims == "nn":
        (m, kk), n, dot = a.shape, b.shape[1], _dot
    elif dims == "nt":
        (m, kk), n, dot = a.shape, b.shape[0], _dot_nt
    else:
        (kk, m), n, dot = a.shape, b.shape[1], _dot_tn
    tm, tn, tk = min(tm, m), min(tn, n), min(tk, kk)
    assert m % tm == 0 and n % tn == 0 and kk % tk == 0, (name, m, n, kk)
    nk = kk // tk

    def kern(a_ref, b_ref, o_ref, acc_ref):
        k = pl.program_id(2)

        @pl.when(k == 0)
        def _():
            acc_ref[...] = jnp.zeros_like(acc_ref)

        acc_ref[...] += dot(a_ref[...].astype(BF16), b_ref[...].astype(BF16))

        @pl.when(k == nk - 1)
        def _():
            o_ref[...] = acc_ref[...].astype(o_ref.dtype)

    if split_cols:
        out_shape = jax.ShapeDtypeStruct((n // tn, m, tn), out_dtype)
        o_spec = pl.BlockSpec((None, tm, tn), lambda i, j, k: (j, i, 0))
    else:
        out_shape = jax.ShapeDtypeStruct((m, n), out_dtype)
        o_spec = pl.BlockSpec((tm, tn), lambda i, j, k: (i, j))
    if dims == "nn":
        a_spec = pl.BlockSpec((tm, tk), lambda i, j, k: (i, k))
        b_spec = pl.BlockSpec((tk, tn), lambda i, j, k: (k, j))
    elif dims == "nt":
        a_spec = pl.BlockSpec((tm, tk), lambda i, j, k: (i, k))
        b_spec = pl.BlockSpec((tn, tk), lambda i, j, k: (j, k))
    else:
        a_spec = pl.BlockSpec((tk, tm), lambda i, j, k: (k, i))
        b_spec = pl.BlockSpec((tk, tn), lambda i, j, k: (k, j))
    return pl.pallas_call(
        kern, grid=(m // tm, n // tn, nk), in_specs=[a_spec, b_spec], out_specs=o_spec, out_shape=out_shape,
        scratch_shapes=[pltpu.VMEM((tm, tn), F32)], compiler_params=_cp(("parallel", "parallel", "arbitrary")), name=name,
    )(a, b)


def _inproj(x, g1, w_in):
    t = x.shape[0]
    tm = min(512, t)

    def kern(x_ref, g_ref, w_ref, xl_ref, qkv_ref, xn_ref):
        xv = x_ref[...]
        xn = (xv * _rstd(xv) * g_ref[...]).astype(BF16)
        xn_ref[...] = xn
        xl_ref[...] = _dot(xn, w_ref[:, : 2 * LRU_WIDTH])
        qkv_ref[...] = _dot(xn, w_ref[:, 2 * LRU_WIDTH:]).astype(BF16)

    row = lambda c: pl.BlockSpec((tm, c), lambda i: (i, 0))
    return pl.pallas_call(
        kern, grid=(t // tm,),
        in_specs=[row(D_MODEL), pl.BlockSpec((1, D_MODEL), lambda i: (0, 0)), pl.BlockSpec((D_MODEL, IN_COLS), lambda i: (0, 0))],
        out_specs=[row(2 * LRU_WIDTH), row(3 * SB_WIDTH), row(D_MODEL)],
        out_shape=[jax.ShapeDtypeStruct((t, 2 * LRU_WIDTH), F32), jax.ShapeDtypeStruct((t, 3 * SB_WIDTH), BF16),
                   jax.ShapeDtypeStruct((t, D_MODEL), BF16)],
        compiler_params=_cp(("parallel",)), name="inproj",
    )(x, g1, w_in)


def _conv_taps(hist, u):
    cat = jnp.concatenate([hist, u], axis=0)
    return [pltpu.roll(cat, CONV_WIDTH - 1 - k, 0)[SUBLANES:] for k in range(CONV_WIDTH - 1)] + [u]


def _lru_gates(c, wbd_ref, ba, bx, sp):
    gas, gxs = [], []
    for p in range(LRU_WIDTH // LANES):
        gax = _dot(c[:, LANES * p: LANES * (p + 1)].astype(BF16), wbd_ref[p])
        gas.append(gax[:, :LANES])
        gxs.append(gax[:, LANES:])
    r = _sigmoid(jnp.concatenate(gas, axis=1) + ba)
    i = _sigmoid(jnp.concatenate(gxs, axis=1) + bx)
    la = (-LRU_C) * r * sp
    a = jnp.exp(la)
    mult = jnp.sqrt(_neg_expm1(2.0 * la))
    return r, i, a, mult


def _scan_fwd(a, b):
    s = 1
    while s < a.shape[0]:
        b = b + a * _shift_down(b, s, 0.0)
        a = a * _shift_down(a, s, 1.0)
        s *= 2
    return a, b


def _scan_rev(a, b):
    s = 1
    while s < a.shape[0]:
        b = b + a * _shift_up(b, s, 0.0)
        a = a * _shift_up(a, s, 1.0)
        s *= 2
    return a, b


def _lru_param_specs(grid_rank):
    z2 = (lambda e, c: (0, 0)) if grid_rank == 2 else None
    return [
        pl.BlockSpec((CONV_WIDTH, LRU_WIDTH), z2), pl.BlockSpec((1, LRU_WIDTH), z2),
        pl.BlockSpec((LRU_WIDTH // LANES, LANES, 2 * LANES), lambda e, c: (0, 0, 0)),
        pl.BlockSpec((1, LRU_WIDTH), z2), pl.BlockSpec((1, LRU_WIDTH), z2), pl.BlockSpec((1, LRU_WIDTH), z2),
    ]


def _lru_fwd(xl, conv_w, conv_b, wbd, ba, bx, lam, seq):
    t = xl.shape[0]
    tc = min(512, seq)
    nc = seq // tc

    def kern(u_ref, g_ref, cw_ref, cb_ref, wbd_ref, ba_ref, bx_ref, lam_ref, h_ref, y_ref, hist_ref, hcar_ref):
        @pl.when(pl.program_id(1) == 0)
        def _():
            hist_ref[...] = jnp.zeros_like(hist_ref)
            hcar_ref[...] = jnp.zeros_like(hcar_ref)

        u = u_ref[...]
        taps = _conv_taps(hist_ref[...], u)
        hist_ref[...] = u_ref[tc - SUBLANES:, :]
        c = cb_ref[...]
        for k in range(CONV_WIDTH):
            c = c + taps[k] * cw_ref[k:k + 1, :]
        sp = _softplus(-lam_ref[...])
        _, i, a, mult = _lru_gates(c, wbd_ref, ba_ref[...], bx_ref[...], sp)
        aa, bb = _scan_fwd(a, mult * i * c)
        h = bb + aa * hcar_ref[0:1, :]
        h_ref[...] = h
        hcar_ref[0:1, :] = h_ref[tc - 1:tc, :]
        y_ref[...] = h * _gelu(g_ref[...])[0]

    chunk = lambda col: pl.BlockSpec((tc, LRU_WIDTH), lambda e, c: (e * nc + c, col))
    out = jax.ShapeDtypeStruct((t, LRU_WIDTH), F32)
    return pl.pallas_call(
        kern, grid=(t // seq, nc), in_specs=[chunk(0), chunk(1)] + _lru_param_specs(2),
        out_specs=[chunk(0), chunk(0)], out_shape=[out, out],
        scratch_shapes=[pltpu.VMEM((SUBLANES, LRU_WIDTH), F32), pltpu.VMEM((SUBLANES, LRU_WIDTH), F32)],
        compiler_params=_cp(("arbitrary", "arbitrary")), name="lru_fwd",
    )(xl, xl, conv_w, conv_b, wbd, ba, bx, lam)


def _att_masks():
    r0 = lax.broadcasted_iota(jnp.int32, (TQ, TQ), 0)
    r1 = lax.broadcasted_iota(jnp.int32, (TQ, TQ), 1)
    return r0, r1


def _att_logits(kb, qb):
    z = _dot_nt(kb, qb) * ATT_SCALE
    lg = jnp.log(1.0 + jnp.exp(-jnp.abs(z)))
    lb = jnp.minimum(z, 0.0) - lg
    return lb, lb - z


def _split_dot(m, x):
    hi = x.astype(BF16)
    lo = (x - hi.astype(F32)).astype(BF16)
    return _dot(m, hi) + _dot(m, lo)


def _qkv_specs(seq):
    n = SB_WIDTH // LANES
    return [pl.BlockSpec((seq, LANES), lambda e, p, off=off: (e, off * n + p)) for off in range(3)]


def _attn_fwd(qkv, seq):
    t = qkv.shape[0]
    ne, nq = t // seq, seq // TQ

    def kern(q_ref, k_ref, v_ref, o_ref, tot_ref):
        r0, r1 = _att_masks()
        upper = (r1 > r0).astype(BF16)
        causal = r0 < r1

        for hh in range(LANES // DH):
            ls = slice(hh * DH, (hh + 1) * DH)

            def block(k0, qb, carry, oacc, masked):
                kb = k_ref[pl.ds(k0, TQ), ls]
                vb = v_ref[pl.ds(k0, TQ), ls]
                lb, l1 = _att_logits(kb, qb)
                if masked:
                    l1 = jnp.where(causal, l1, 0.0)
                att = jnp.exp(lb + _split_dot(upper, l1) + carry)
                if masked:
                    att = jnp.where(causal, att, 0.0)
                oacc = oacc + _dot_tn(att.astype(BF16), vb)
                return carry + jnp.sum(l1, axis=0, keepdims=True), oacc

            def qloop(qi, _):
                q0 = pl.multiple_of(qi * TQ, TQ)
                qb = q_ref[pl.ds(q0, TQ), ls]
                carry, oacc = block(q0, qb, jnp.zeros((1, TQ), F32), jnp.zeros((TQ, DH), F32), True)

                def kloop(it, co):
                    return block(pl.multiple_of((qi - 1 - it) * TQ, TQ), qb, co[0], co[1], False)

                carry, oacc = lax.fori_loop(0, qi, kloop, (carry, oacc))
                o_ref[pl.ds(q0, TQ), ls] = oacc
                tot_ref[hh, qi] = jnp.broadcast_to(carry, (SUBLANES, TQ))
                return 0

            lax.fori_loop(0, nq, qloop, 0)

    return pl.pallas_call(
        kern, grid=(ne, SB_WIDTH // LANES), in_specs=_qkv_specs(seq),
        out_specs=[pl.BlockSpec((seq, LANES), lambda e, p: (e, p)),
                   pl.BlockSpec((None, LANES // DH, nq, SUBLANES, TQ), lambda e, p: (e, p, 0, 0, 0))],
        out_shape=[jax.ShapeDtypeStruct((t, SB_WIDTH), F32),
                   jax.ShapeDtypeStruct((ne, SB_WIDTH // DH, nq, SUBLANES, TQ), F32)],
        compiler_params=_cp(("parallel", "parallel")), name="attn_fwd",
    )(qkv, qkv, qkv)


def _outproj(y_lru, o, x, ga, gb, w_out):
    t = x.shape[0]
    tm = min(512, t)

    def kern(y_ref, o_ref, x_ref, ga_ref, gb_ref, w_ref, h1_ref, mix_ref):
        yv, ov = y_ref[...], o_ref[...]
        mix = jnp.concatenate([yv * _rstd(yv) * ga_ref[...], ov * _rstd(ov) * gb_ref[...]], axis=1).astype(BF16)
        mix_ref[...] = mix
        h1_ref[...] = x_ref[...] + _dot(mix, w_ref[...])

    row = lambda c: pl.BlockSpec((tm, c), lambda i: (i, 0))
    vec = lambda c: pl.BlockSpec((1, c), lambda i: (0, 0))
    return pl.pallas_call(
        kern, grid=(t // tm,),
        in_specs=[row(LRU_WIDTH), row(SB_WIDTH), row(D_MODEL), vec(LRU_WIDTH), vec(SB_WIDTH),
                  pl.BlockSpec((D_MODEL, D_MODEL), lambda i: (0, 0))],
        out_specs=[row(D_MODEL), row(D_MODEL)],
        out_shape=[jax.ShapeDtypeStruct((t, D_MODEL), F32), jax.ShapeDtypeStruct((t, D_MODEL), BF16)],
        compiler_params=_cp(("parallel",)), name="outproj",
    )(y_lru, o, x, ga, gb, w_out)


def _mlp_fwd(h1, g2, w_up, w_down):
    t = h1.shape[0]
    tm, tf = min(512, t), 1024
    nf = D_FF // tf

    def kern(h1_ref, g_ref, wu_ref, wd_ref, h2_ref, hn_ref, up_ref, u2_ref, acc_ref):
        f = pl.program_id(1)

        @pl.when(f == 0)
        def _():
            hv = h1_ref[...]
            hn_ref[...] = (hv * _rstd(hv) * g_ref[...]).astype(BF16)
            acc_ref[...] = hv

        up = jnp.maximum(_dot(hn_ref[...], wu_ref[...]), 0.0)
        u2 = (up * up).astype(BF16)
        up_ref[...] = up.astype(BF16)
        u2_ref[...] = u2
        acc_ref[...] += _dot(u2, wd_ref[...])

        @pl.when(f == nf - 1)
        def _():
            h2_ref[...] = acc_ref[...]

    return pl.pallas_call(
        kern, grid=(t // tm, nf),
        in_specs=[pl.BlockSpec((tm, D_MODEL), lambda i, f: (i, 0)), pl.BlockSpec((1, D_MODEL), lambda i, f: (0, 0)),
                  pl.BlockSpec((D_MODEL, tf), lambda i, f: (0, f)), pl.BlockSpec((tf, D_MODEL), lambda i, f: (f, 0))],
        out_specs=[pl.BlockSpec((tm, D_MODEL), lambda i, f: (i, 0)), pl.BlockSpec((tm, D_MODEL), lambda i, f: (i, 0)),
                   pl.BlockSpec((tm, tf), lambda i, f: (i, f)), pl.BlockSpec((tm, tf), lambda i, f: (i, f))],
        out_shape=[jax.ShapeDtypeStruct((t, D_MODEL), F32), jax.ShapeDtypeStruct((t, D_MODEL), BF16),
                   jax.ShapeDtypeStruct((t, D_FF), BF16), jax.ShapeDtypeStruct((t, D_FF), BF16)],
        scratch_shapes=[pltpu.VMEM((tm, D_MODEL), F32)],
        compiler_params=_cp(("parallel", "arbitrary")), name="mlp_fwd",
    )(h1, g2, w_up, w_down)


def _loss_head(h2, target, gf):
    t = h2.shape[0]
    tm = min(512, t)

    def kern(h_ref, t_ref, g_ref, dh_ref, loss_ref, dg_ref):
        @pl.when(pl.program_id(0) == 0)
        def _():
            loss_ref[...] = jnp.zeros_like(loss_ref)
            dg_ref[...] = jnp.zeros_like(dg_ref)

        hv, g = h_ref[...], g_ref[...]
        err = hv * _rstd(hv) * g - t_ref[...]
        lane = lax.broadcasted_iota(jnp.int32, (1, LANES), 1)
        loss_ref[...] += jnp.where(lane == 0, 0.5 * jnp.sum(err * err) / D_MODEL, 0.0)
        dx, dg = _rms_bwd(hv, g, err * (1.0 / D_MODEL))
        dh_ref[...] = dx
        dg_ref[...] += dg

    row = pl.BlockSpec((tm, D_MODEL), lambda i: (i, 0))
    vec = pl.BlockSpec((1, D_MODEL), lambda i: (0, 0))
    return pl.pallas_call(
        kern, grid=(t // tm,), in_specs=[row, row, vec],
        out_specs=[row, pl.BlockSpec((1, LANES), lambda i: (0, 0)), vec],
        out_shape=[jax.ShapeDtypeStruct((t, D_MODEL), F32), jax.ShapeDtypeStruct((1, LANES), F32),
                   jax.ShapeDtypeStruct((1, D_MODEL), F32)],
        compiler_params=_cp(("arbitrary",)), name="loss_head",
    )(h2, target, gf)


def _mlp_bwd_pre(dh2, w_down, up):
    t = dh2.shape[0]
    tm, tf = min(512, t), 1024

    def kern(d_ref, w_ref, up_ref, o_ref):
        du2 = _dot_nt(d_ref[...].astype(BF16), w_ref[...])
        o_ref[...] = (du2 * (2.0 * up_ref[...].astype(F32))).astype(BF16)

    return pl.pallas_call(
        kern, grid=(t // tm, D_FF // tf),
        in_specs=[pl.BlockSpec((tm, D_MODEL), lambda i, f: (i, 0)), pl.BlockSpec((tf, D_MODEL), lambda i, f: (f, 0)),
                  pl.BlockSpec((tm, tf), lambda i, f: (i, f))],
        out_specs=pl.BlockSpec((tm, tf), lambda i, f: (i, f)), out_shape=jax.ShapeDtypeStruct((t, D_FF), BF16),
        compiler_params=_cp(("parallel", "parallel")), name="mlp_bwd_pre",
    )(dh2, w_down, up)


def _proj_bwd_norm(dys, w, x, g, resid, name):
    t = x.shape[0]
    tm = min(512, t)
    widths = [dy.shape[1] for dy in dys]
    n = len(dys)

    def kern(*refs):
        dy_refs, (w_ref, x_ref, g_ref, r_ref, dx_ref, dg_ref) = refs[:n], refs[n:]

        @pl.when(pl.program_id(0) == 0)
        def _():
            dg_ref[...] = jnp.zeros_like(dg_ref)

        off, dxn = 0, None
        for dy_ref, wd in zip(dy_refs, widths):
            part = _dot_nt(dy_ref[...], w_ref[:, off:off + wd])
            dxn = part if dxn is None else dxn + part
            off += wd
        dx, dg = _rms_bwd(x_ref[...], g_ref[...], dxn)
        dx_ref[...] = r_ref[...] + dx
        dg_ref[...] += dg

    row = lambda c: pl.BlockSpec((tm, c), lambda i: (i, 0))
    vec = pl.BlockSpec((1, D_MODEL), lambda i: (0, 0))
    return pl.pallas_call(
        kern, grid=(t // tm,),
        in_specs=[row(wd) for wd in widths] + [pl.BlockSpec(w.shape, lambda i: (0, 0)), row(D_MODEL), vec, row(D_MODEL)],
        out_specs=[row(D_MODEL), vec],
        out_shape=[jax.ShapeDtypeStruct((t, D_MODEL), F32), jax.ShapeDtypeStruct((1, D_MODEL), F32)],
        compiler_params=_cp(("arbitrary",)), name=name,
    )(*dys, w, x, g, resid)


def _outproj_bwd(dh1, w_out, y_lru, o, ga, gb):
    t = dh1.shape[0]
    tm = min(512, t)

    def kern(d_ref, w_ref, y_ref, o_ref, ga_ref, gb_ref, dy_ref, do_ref, dga_ref, dgb_ref):
        @pl.when(pl.program_id(0) == 0)
        def _():
            dga_ref[...] = jnp.zeros_like(dga_ref)
            dgb_ref[...] = jnp.zeros_like(dgb_ref)

        dmix = _dot_nt(d_ref[...].astype(BF16), w_ref[...])
        dy, dga = _rms_bwd(y_ref[...], ga_ref[...], dmix[:, :LRU_WIDTH])
        do, dgb = _rms_bwd(o_ref[...], gb_ref[...], dmix[:, LRU_WIDTH:])
        dy_ref[...] = dy
        do_ref[...] = do
        dga_ref[...] += dga
        dgb_ref[...] += dgb

    row = lambda c: pl.BlockSpec((tm, c), lambda i: (i, 0))
    vec = pl.BlockSpec((1, LRU_WIDTH), lambda i: (0, 0))
    half = jax.ShapeDtypeStruct((t, LRU_WIDTH), F32)
    gsum = jax.ShapeDtypeStruct((1, LRU_WIDTH), F32)
    return pl.pallas_call(
        kern, grid=(t // tm,),
        in_specs=[row(D_MODEL), pl.BlockSpec((D_MODEL, D_MODEL), lambda i: (0, 0)), row(LRU_WIDTH), row(SB_WIDTH), vec, vec],
        out_specs=[row(LRU_WIDTH), row(SB_WIDTH), vec, vec], out_shape=[half, half, gsum, gsum],
        compiler_params=_cp(("arbitrary",)), name="outproj_bwd",
    )(dh1, w_out, y_lru, o, ga, gb)


def _attn_bwd(qkv, do, tot, seq):
    t = qkv.shape[0]
    ne, nq = t // seq, seq // TQ

    def kern(q_ref, k_ref, v_ref, do_ref, tot_ref, dq_ref, dk_ref, dv_ref, dq_scr, dk_scr, dv_scr):
        r0, r1 = _att_masks()
        lower_inc = (r1 <= r0).astype(BF16)
        lower_exc = (r1 < r0).astype(BF16)
        causal = r0 < r1
        dk_scr[...] = jnp.zeros_like(dk_scr)
        dv_scr[...] = jnp.zeros_like(dv_scr)

        for hh in range(LANES // DH):
            ls = slice(hh * DH, (hh + 1) * DH)

            def block(k0, qb, dob, totr, cf, cp, dqacc, masked):
                kb = k_ref[pl.ds(k0, TQ), ls]
                vb = v_ref[pl.ds(k0, TQ), ls]
                lb, l1 = _att_logits(kb, qb)
                if masked:
                    l1 = jnp.where(causal, l1, 0.0)
                att = jnp.exp(lb + (totr - (_split_dot(lower_inc, l1) + cf)))
                if masked:
                    att = jnp.where(causal, att, 0.0)
                pw = att * _dot_nt(vb, dob)
                dz = (pw - jnp.exp(lb) * (pw + _split_dot(lower_exc, pw) + cp)) * ATT_SCALE
                if masked:
                    dz = jnp.where(causal, dz, 0.0)
                dzb = dz.astype(BF16)
                dv_scr[pl.ds(k0, TQ), ls] += _dot(att.astype(BF16), dob)
                dk_scr[pl.ds(k0, TQ), ls] += _dot(dzb, qb)
                return (cf + jnp.sum(l1, axis=0, keepdims=True), cp + jnp.sum(pw, axis=0, keepdims=True),
                        dqacc + _dot_tn(dzb, kb))

            def qloop(qi, _):
                q0 = pl.multiple_of(qi * TQ, TQ)
                qb = q_ref[pl.ds(q0, TQ), ls]
                dob = do_ref[pl.ds(q0, TQ), ls].astype(BF16)
                totr = tot_ref[hh, qi][0:1, :]

                def kloop(kj, st):
                    return block(pl.multiple_of(kj * TQ, TQ), qb, dob, totr, st[0], st[1], st[2], False)

                zrow = jnp.zeros((1, TQ), F32)
                st = lax.fori_loop(0, qi, kloop, (zrow, zrow, jnp.zeros((TQ, DH), F32)))
                _, _, dqacc = block(q0, qb, dob, totr, st[0], st[1], st[2], True)
                dq_scr[pl.ds(q0, TQ), ls] = dqacc
                return 0

            lax.fori_loop(0, nq, qloop, 0)

        dq_ref[...] = dq_scr[...].astype(BF16)
        dk_ref[...] = dk_scr[...].astype(BF16)
        dv_ref[...] = dv_scr[...].astype(BF16)

    blk = pl.BlockSpec((seq, LANES), lambda e, p: (e, p))
    out = jax.ShapeDtypeStruct((t, SB_WIDTH), BF16)
    return pl.pallas_call(
        kern, grid=(ne, SB_WIDTH // LANES),
        in_specs=_qkv_specs(seq) + [blk, pl.BlockSpec((None, LANES // DH, nq, SUBLANES, TQ), lambda e, p: (e, p, 0, 0, 0))],
        out_specs=[blk, blk, blk], out_shape=[out, out, out],
        scratch_shapes=[pltpu.VMEM((seq, LANES), F32)] * 3,
        compiler_params=_cp(("parallel", "parallel")), name="attn_bwd",
    )(qkv, qkv, qkv, do, tot)


def _lru_bwd(xl, h, dy, conv_w, conv_b, wbd, ba, bx, lam, seq):
    t = xl.shape[0]
    tc = min(512, seq)
    nc = seq // tc
    nb = tc // SUBLANES

    def kern(u_ref, g_ref, up_ref, h_ref, hp_ref, dy_ref, cw_ref, cb_ref, wbd_ref, ba_ref, bx_ref, lam_ref,
             dxl_ref, small_ref, dwbd_ref, lnext_ref, anext_ref, dcnext_ref):
        e, ci = pl.program_id(0), pl.program_id(1)
        first = ci == nc - 1

        @pl.when((e == 0) & (ci == 0))
        def _():
            small_ref[...] = jnp.zeros_like(small_ref)
            dwbd_ref[...] = jnp.zeros_like(dwbd_ref)

        @pl.when(ci == 0)
        def _():
            lnext_ref[...] = jnp.zeros_like(lnext_ref)
            anext_ref[...] = jnp.zeros_like(anext_ref)
            dcnext_ref[...] = jnp.zeros_like(dcnext_ref)

        u, g = u_ref[...], g_ref[...]
        keep = jnp.where(first, 0.0, 1.0)
        taps = _conv_taps(keep * up_ref[...], u)
        c = cb_ref[...]
        for k in range(CONV_WIDTH):
            c = c + taps[k] * cw_ref[k:k + 1, :]
        lam = lam_ref[...]
        sp = _softplus(-lam)
        r, i, a, mult = _lru_gates(c, wbd_ref, ba_ref[...], bx_ref[...], sp)
        gel, th = _gelu(g)
        dyv, hv = dy_ref[...], h_ref[...]
        dg = dyv * hv * _gelu_grad(g, th)

        aa, bb = _scan_rev(_shift_up(a, 1, anext_ref[0:1, :]), dyv * gel)
        lt = bb + aa * lnext_ref[0:1, :]
        lnext_ref[0:1, :] = _row_of(lt, 0)
        anext_ref[0:1, :] = _row_of(a, 0)

        hprev = _shift_down(hv, 1, keep * hp_ref[SUBLANES - 1:SUBLANES, :])
        da = lt * hprev
        dmult = lt * i * c
        di = lt * mult * c
        dc = lt * mult * i
        dla = da * a - dmult * (a * a) / mult
        dga = dla * ((-LRU_C) * sp) * r * (1.0 - r)
        dgx = di * i * (1.0 - i)
        small_ref[7:8, :] += jnp.sum(dla * r, axis=0, keepdims=True) * (LRU_C * _sigmoid(-lam))
        small_ref[5:6, :] += jnp.sum(dga, axis=0, keepdims=True)
        small_ref[6:7, :] += jnp.sum(dgx, axis=0, keepdims=True)

        dcs = []
        for p in range(LRU_WIDTH // LANES):
            cols = slice(LANES * p, LANES * (p + 1))
            dgax = jnp.concatenate([dga[:, cols], dgx[:, cols]], axis=1).astype(BF16)
            dcs.append(_dot_nt(dgax, wbd_ref[p]))
            dwbd_ref[p] += _dot_tn(c[:, cols].astype(BF16), dgax)
        dc = dc + jnp.concatenate(dcs, axis=1)
        small_ref[4:5, :] += jnp.sum(dc, axis=0, keepdims=True)

        catd = jnp.concatenate([dc, dcnext_ref[...]], axis=0)
        du = dc * cw_ref[CONV_WIDTH - 1:CONV_WIDTH, :]
        for j in range(1, CONV_WIDTH):
            du = du + pltpu.roll(catd, tc + SUBLANES - j, 0)[:tc] * cw_ref[CONV_WIDTH - 1 - j:CONV_WIDTH - j, :]
        dcnext_ref[...] = dc[:SUBLANES]
        for k in range(CONV_WIDTH):
            small_ref[k:k + 1, :] += jnp.sum(dc * taps[k], axis=0, keepdims=True)
        dxl_ref[:, :LRU_WIDTH] = du.astype(BF16)
        dxl_ref[:, LRU_WIDTH:] = dg.astype(BF16)

    rev = lambda e, c: e * nc + (nc - 1 - c)
    chunk = lambda col: pl.BlockSpec((tc, LRU_WIDTH), lambda e, c: (rev(e, c), col))
    prev8 = pl.BlockSpec((SUBLANES, LRU_WIDTH), lambda e, c: (jnp.maximum(rev(e, c) * nb - 1, 0), 0))
    return pl.pallas_call(
        kern, grid=(t // seq, nc),
        in_specs=[chunk(0), chunk(1), prev8, chunk(0), prev8, chunk(0)] + _lru_param_specs(2),
        out_specs=[pl.BlockSpec((tc, 2 * LRU_WIDTH), lambda e, c: (rev(e, c), 0)),
                   pl.BlockSpec((SUBLANES, LRU_WIDTH), lambda e, c: (0, 0)),
                   pl.BlockSpec((LRU_WIDTH // LANES, LANES, 2 * LANES), lambda e, c: (0, 0, 0))],
        out_shape=[jax.ShapeDtypeStruct((t, 2 * LRU_WIDTH), BF16), jax.ShapeDtypeStruct((SUBLANES, LRU_WIDTH), F32),
                   jax.ShapeDtypeStruct((LRU_WIDTH // LANES, LANES, 2 * LANES), F32)],
        scratch_shapes=[pltpu.VMEM((SUBLANES, LRU_WIDTH), F32)] * 3,
        compiler_params=_cp(("arbitrary", "arbitrary")), name="lru_bwd",
    )(xl, xl, xl, h, h, dy, conv_w, conv_b, wbd, ba, bx, lam)


def _adamw(w, g, m, v, name):
    rows, cols = w.shape
    tr = 256 if rows % 256 == 0 else rows

    def kern(w_ref, g_ref, m_ref, v_ref, d_ref, m2_ref, v2_ref):
        gv = g_ref[...]
        m2 = ADAM_B1 * m_ref[...] + (1.0 - ADAM_B1) * gv
        v2 = ADAM_B2 * v_ref[...] + (1.0 - ADAM_B2) * (gv * gv)
        m_hat = m2 / (1.0 - ADAM_B1 ** ADAM_STEP)
        v_hat = v2 / (1.0 - ADAM_B2 ** ADAM_STEP)
        d_ref[...] = -ADAM_LR * (m_hat / (jnp.sqrt(v_hat) + ADAM_EPS) + ADAM_WD * w_ref[...])
        m2_ref[...] = m2
        v2_ref[...] = v2

    blk = pl.BlockSpec((tr, cols), lambda i: (i, 0))
    out = jax.ShapeDtypeStruct((rows, cols), F32)
    return pl.pallas_call(kern, grid=(rows // tr,), in_specs=[blk] * 4, out_specs=[blk] * 3, out_shape=[out] * 3,
                          compiler_params=_cp(("parallel",)), name=name)(w, g, m, v)


ANY = pl.BlockSpec(memory_space=pl.ANY)


def _place():
    return lax.axis_index("x"), lax.axis_index("y"), lax.axis_index("c")


def _other_chips(x, y):
    return [(1 - x, y), (x, 1 - y), (1 - x, 1 - y)]


def _gather_weights(shards):
    n = len(shards)

    def kern(*refs):
        ins, outs, (send_sems, recv_sems, loc_sems) = refs[:n], refs[n:2 * n], refs[2 * n:]
        x, y, c = _place()
        mine = 2 * x + y
        started = []
        for w in range(n):
            lc = pltpu.make_async_copy(ins[w], outs[w].at[mine], loc_sems.at[w])
            lc.start()
            started.append(lc)
            for k, chip in enumerate(_other_chips(x, y)):
                cp = pltpu.make_async_remote_copy(
                    src_ref=ins[w], dst_ref=outs[w].at[mine], send_sem=send_sems.at[3 * w + k],
                    recv_sem=recv_sems.at[3 * w + k], device_id=(*chip, c), device_id_type=MESH)
                cp.start()
                started.append(cp)
        for cp in started:
            cp.wait()

    return pl.pallas_call(
        kern, in_specs=[ANY] * n, out_specs=[ANY] * n,
        out_shape=[jax.ShapeDtypeStruct((N_CHIPS,) + s.shape, s.dtype) for s in shards],
        scratch_shapes=[pltpu.SemaphoreType.DMA((3 * n,)), pltpu.SemaphoreType.DMA((3 * n,)), pltpu.SemaphoreType.DMA((n,))],
        name="gather_weights",
    )(*shards)


def _pair_swap(grads):
    n = len(grads)

    def kern(*refs):
        ins, outs, (send_sems, recv_sems) = refs[:n], refs[n:2 * n], refs[2 * n:]
        x, y, c = _place()
        started = []
        for w in range(n):
            half = ins[w].shape[1] // 2
            cp = pltpu.make_async_remote_copy(
                src_ref=ins[w].at[:, pl.ds((1 - c) * half, half), :], dst_ref=outs[w], send_sem=send_sems.at[w],
                recv_sem=recv_sems.at[w], device_id=(x, y, 1 - c), device_id_type=MESH)
            cp.start()
            started.append(cp)
        for cp in started:
            cp.wait()

    return pl.pallas_call(
        kern, in_specs=[ANY] * n, out_specs=[ANY] * n,
        out_shape=[jax.ShapeDtypeStruct((g.shape[0], g.shape[1] // 2, g.shape[2]), g.dtype) for g in grads],
        scratch_shapes=[pltpu.SemaphoreType.DMA((n,)), pltpu.SemaphoreType.DMA((n,))], name="pair_swap",
    )(*grads)


def _pair_add(g, got, core):
    _, rows, cols = g.shape
    half = rows // 2
    tr = min(256, half)
    nt = half // tr

    def kern(c_ref, g_ref, o_ref, out_ref):
        out_ref[...] = g_ref[...] + o_ref[...]

    return pl.pallas_call(
        kern, grid_spec=pltpu.PrefetchScalarGridSpec(
            num_scalar_prefetch=1, grid=(N_CHIPS, nt),
            in_specs=[pl.BlockSpec((None, tr, cols), lambda j, i, c_ref: (j, c_ref[0] * nt + i, 0)),
                      pl.BlockSpec((None, tr, cols), lambda j, i, c_ref: (j, i, 0))],
            out_specs=pl.BlockSpec((None, tr, cols), lambda j, i, c_ref: (j, i, 0))),
        out_shape=jax.ShapeDtypeStruct((N_CHIPS, half, cols), F32),
        compiler_params=_cp(("parallel", "parallel")), name="pair_add",
    )(core, g, got)


def _chip_exchange(parts):
    n = len(parts)

    def kern(*refs):
        ins, outs, (send_sems, recv_sems) = refs[:n], refs[n:2 * n], refs[2 * n:]
        x, y, c = _place()
        started = []
        for w in range(n):
            for k, chip in enumerate(_other_chips(x, y)):
                cp = pltpu.make_async_remote_copy(
                    src_ref=ins[w].at[2 * chip[0] + chip[1]], dst_ref=outs[w].at[k], send_sem=send_sems.at[3 * w + k],
                    recv_sem=recv_sems.at[3 * w + k], device_id=(*chip, c), device_id_type=MESH)
                cp.start()
                started.append(cp)
        for cp in started:
            cp.wait()

    return pl.pallas_call(
        kern, in_specs=[ANY] * n, out_specs=[ANY] * n,
        out_shape=[jax.ShapeDtypeStruct((3,) + p.shape[1:], p.dtype) for p in parts],
        scratch_shapes=[pltpu.SemaphoreType.DMA((3 * n,)), pltpu.SemaphoreType.DMA((3 * n,))], name="chip_exchange",
    )(*parts)


def _chip_add(part, got, chip):
    _, half, cols = part.shape
    tr = min(256, half)

    def kern(j_ref, p_ref, g_ref, out_ref):
        out_ref[...] = p_ref[...] + g_ref[0] + g_ref[1] + g_ref[2]

    return pl.pallas_call(
        kern, grid_spec=pltpu.PrefetchScalarGridSpec(
            num_scalar_prefetch=1, grid=(half // tr,),
            in_specs=[pl.BlockSpec((None, tr, cols), lambda i, j_ref: (j_ref[0], i, 0)),
                      pl.BlockSpec((3, tr, cols), lambda i, j_ref: (0, i, 0))],
            out_specs=pl.BlockSpec((tr, cols), lambda i, j_ref: (i, 0))),
        out_shape=jax.ShapeDtypeStruct((half, cols), F32),
        compiler_params=_cp(("parallel",)), name="chip_add",
    )(chip, part, got)


def _pair_join(halves):
    n = len(halves)

    def kern(*refs):
        ins, outs, (send_sems, recv_sems, loc_sems) = refs[:n], refs[n:2 * n], refs[2 * n:]
        x, y, c = _place()
        started = []
        for w in range(n):
            half = ins[w].shape[0]
            dst = outs[w].at[pl.ds(c * half, half), :]
            lc = pltpu.make_async_copy(ins[w], dst, loc_sems.at[w])
            lc.start()
            cp = pltpu.make_async_remote_copy(
                src_ref=ins[w], dst_ref=dst, send_sem=send_sems.at[w], recv_sem=recv_sems.at[w],
                device_id=(x, y, 1 - c), device_id_type=MESH)
            cp.start()
            started += [lc, cp]
        for cp in started:
            cp.wait()

    return pl.pallas_call(
        kern, in_specs=[ANY] * n, out_specs=[ANY] * n,
        out_shape=[jax.ShapeDtypeStruct((2 * h.shape[0], h.shape[1]), h.dtype) for h in halves],
        scratch_shapes=[pltpu.SemaphoreType.DMA((n,)), pltpu.SemaphoreType.DMA((n,)), pltpu.SemaphoreType.DMA((n,))],
        name="pair_join",
    )(*halves)


def _allreduce_small(packed):
    rows = packed.shape[0]

    def kern(in_ref, out_ref, slots, send_sems, recv_sems):
        x, y, c = _place()
        mine = 4 * x + 2 * y + c
        slots[mine] = in_ref[...]
        started = []
        for k in range(1, N_DEV):
            peer = (x ^ (k >> 2), y ^ ((k >> 1) & 1), c ^ (k & 1))
            cp = pltpu.make_async_remote_copy(
                src_ref=in_ref, dst_ref=slots.at[mine], send_sem=send_sems.at[k - 1], recv_sem=recv_sems.at[k - 1],
                device_id=peer, device_id_type=MESH)
            cp.start()
            started.append(cp)
        for cp in started:
            cp.wait()
        acc = slots[0]
        for s in range(1, N_DEV):
            acc = acc + slots[s]
        out_ref[...] = acc

    vm = pl.BlockSpec(memory_space=pltpu.VMEM)
    return pl.pallas_call(
        kern, in_specs=[vm], out_specs=vm, out_shape=jax.ShapeDtypeStruct((rows, LANES), F32),
        scratch_shapes=[pltpu.VMEM((N_DEV, rows, LANES), F32), pltpu.SemaphoreType.DMA((N_DEV - 1,)),
                        pltpu.SemaphoreType.DMA((N_DEV - 1,))],
        name="allreduce_small",
    )(packed)


SMALL = ["norm1_g", "conv_w", "conv_b", "lru_w_a", "lru_b_a", "lru_w_x", "lru_b_x", "lru_lambda", "lru_out_g", "sb_out_g",
         "norm2_g", "final_g"]
BIG = ["w_in", "w_out", "w_up", "w_down"]
WEIGHTS = ["norm1_g", "w_in", "conv_w", "conv_b", "lru_w_a", "lru_b_a", "lru_w_x", "lru_b_x", "lru_lambda", "lru_out_g",
           "sb_out_g", "w_out", "norm2_g", "w_up", "w_down", "final_g"]


def _pack(arrays):
    flat = []
    for a in arrays:
        a = a.reshape(-1).astype(F32)
        flat.append(jnp.pad(a, (0, (-a.shape[0]) % LANES)))
    v = jnp.concatenate(flat)
    v = jnp.pad(v, (0, (-v.shape[0]) % (LANES * SUBLANES)))
    return v.reshape(-1, LANES)


def _unpack(packed, shapes):
    v, out, off = packed.reshape(-1), [], 0
    for shp in shapes:
        size = math.prod(shp)
        out.append(v[off:off + size].reshape(shp))
        off += size + (-size) % LANES
    return out


def _blockdiag_pairs(w):
    w = w.reshape(4, 2, DH, DH)
    z = jnp.zeros((4, DH, DH), w.dtype)
    return jnp.concatenate([jnp.concatenate([w[:, 0], z], axis=2), jnp.concatenate([z, w[:, 1]], axis=2)], axis=1)


def _blockdiag_unpairs(wbd):
    return jnp.stack([wbd[:, :DH, :DH], wbd[:, DH:, DH:]], axis=1).reshape(8, DH, DH)


def _local_step(x2, tgt, seq, norm1_g, w_in_f, conv_w_f, conv_b, w_a, b_a, w_x, b_x, lru_lambda, lru_out_g, sb_out_g, w_out_f,
                norm2_g, w_up_f, w_down_f, final_g):
    wbd = jnp.concatenate([_blockdiag_pairs(w_a), _blockdiag_pairs(w_x)], axis=2).astype(BF16)
    ba, bx = b_a.reshape(1, LRU_WIDTH), b_x.reshape(1, LRU_WIDTH)
    gf = final_g.reshape(1, D_MODEL)

    xl, qkv, xn = _inproj(x2, norm1_g, w_in_f)
    h, y_lru = _lru_fwd(xl, conv_w_f, conv_b, wbd, ba, bx, lru_lambda, seq)
    o, tot = _attn_fwd(qkv, seq)
    h1, mix = _outproj(y_lru, o, x2, lru_out_g, sb_out_g, w_out_f)
    h2, hn, up, u2 = _mlp_fwd(h1, norm2_g, w_up_f, w_down_f)
    dh2, loss_part, d_final = _loss_head(h2, tgt, gf)

    dpre = _mlp_bwd_pre(dh2, w_down_f, up)
    g_w_down = _matmul(u2, dh2, "tn", F32, "dw_down", 1024, 1024, 512).reshape(N_CHIPS, D_FF // N_CHIPS, D_MODEL)
    g_w_up = _matmul(hn, dpre, "tn", F32, "dw_up", 1024, D_FF // N_CHIPS, 512, split_cols=True)
    dh1, d_norm2 = _proj_bwd_norm([dpre], w_up_f, h1, norm2_g, dh2, "mlp_bwd_in")
    g_w_out = _matmul(mix, dh1, "tn", F32, "dw_out", 1024, 1024, 512).reshape(N_CHIPS, D_MODEL // N_CHIPS, D_MODEL)
    dy_lru, do, d_ga, d_gb = _outproj_bwd(dh1, w_out_f, y_lru, o, lru_out_g, sb_out_g)
    dq, dk, dv = _attn_bwd(qkv, do, tot, seq)
    dxl, lru_small, d_wbd = _lru_bwd(xl, h, dy_lru, conv_w_f, conv_b, wbd, ba, bx, lru_lambda, seq)
    dx, d_norm1 = _proj_bwd_norm([dxl, dq, dk, dv], w_in_f, x2, norm1_g, dh1, "inproj_bwd")
    dproj = jnp.concatenate([dxl, dq, dk, dv], axis=1)
    g_w_in = _matmul(xn, dproj, "tn", F32, "dw_in", 1024, IN_COLS // N_CHIPS, 512, split_cols=True)
    small_parts = {
        "norm1_g": d_norm1, "conv_w": lru_small[:CONV_WIDTH], "conv_b": lru_small[4:5],
        "lru_w_a": _blockdiag_unpairs(d_wbd[:, :, :LANES]), "lru_b_a": lru_small[5:6],
        "lru_w_x": _blockdiag_unpairs(d_wbd[:, :, LANES:]), "lru_b_x": lru_small[6:7], "lru_lambda": lru_small[7:8],
        "lru_out_g": d_ga, "sb_out_g": d_gb, "norm2_g": d_norm2, "final_g": d_final,
    }
    return loss_part, dx, [g_w_in, g_w_out, g_w_up, g_w_down], small_parts


def kernel(x, norm1_g, w_in, conv_w, conv_b, lru_w_a, lru_b_a, lru_w_x, lru_b_x, lru_lambda, lru_out_g, sb_out_g, w_out, norm2_g, w_up, w_down, final_g, loss_target, m_norm1_g, m_w_in, m_conv_w, m_conv_b, m_lru_w_a, m_lru_b_a, m_lru_w_x, m_lru_b_x, m_lru_lambda, m_lru_out_g, m_sb_out_g, m_w_out, m_norm2_g, m_w_up, m_w_down, m_final_g, v_norm1_g, v_w_in, v_conv_w, v_conv_b, v_lru_w_a, v_lru_b_a, v_lru_w_x, v_lru_b_x, v_lru_lambda, v_lru_out_g, v_sb_out_g, v_w_out, v_norm2_g, v_w_up, v_w_down, v_final_g):
    given = dict(locals())
    ne, seq, _ = x.shape
    t = ne * seq
    xi, yi, ci = _place()
    chip = (2 * xi + yi).astype(jnp.int32)

    gw_in, gw_out, gw_up, gw_down, gconv = _gather_weights(
        [w_in[0].astype(BF16), w_out[0].astype(BF16), w_up[0].astype(BF16), w_down[0].astype(BF16), conv_w[0]])
    w_in_f = jnp.transpose(gw_in, (1, 0, 2)).reshape(D_MODEL, IN_COLS)
    w_up_f = jnp.transpose(gw_up, (1, 0, 2)).reshape(D_MODEL, D_FF)
    w_out_f = gw_out.reshape(D_MODEL, D_MODEL)
    w_down_f = gw_down.reshape(D_FF, D_MODEL)
    conv_w_f = jnp.transpose(gconv, (1, 0, 2)).reshape(CONV_WIDTH, LRU_WIDTH)

    loss_part, dx, big, small_parts = _local_step(
        x.reshape(t, D_MODEL), loss_target.reshape(t, D_MODEL), seq, norm1_g, w_in_f, conv_w_f, conv_b, lru_w_a[0], lru_b_a,
        lru_w_x[0], lru_b_x, lru_lambda, lru_out_g, sb_out_g, w_out_f, norm2_g, w_up_f, w_down_f, final_g)

    full_shapes = {n: ((CONV_WIDTH, LRU_WIDTH) if n == "conv_w" else given[n].shape) for n in SMALL}
    red = _allreduce_small(_pack([small_parts[n] for n in SMALL] + [loss_part]))
    red_list = _unpack(red, [full_shapes[n] for n in SMALL] + [(1, LANES)])
    grads = dict(zip(SMALL, red_list[:-1]))
    loss = red_list[-1][0, 0]
    grads["conv_w"] = lax.dynamic_slice_in_dim(grads["conv_w"], chip * (LRU_WIDTH // N_CHIPS), LRU_WIDTH // N_CHIPS,
                                               axis=1).reshape(conv_w.shape)

    core = ci.astype(jnp.int32).reshape(1)
    swapped = _pair_swap(big)
    parts = [_pair_add(g, s, core) for g, s in zip(big, swapped)]
    got = _chip_exchange(parts)
    halves = [_chip_add(p, r, chip.reshape(1)) for p, r in zip(parts, got)]
    for n, full in zip(BIG, _pair_join(halves)):
        grads[n] = full.reshape(given[n].shape)

    delta, new_m, new_v = {}, {}, {}
    for n in BIG:
        shp = given[n].shape
        d, m2, v2 = _adamw(given[n][0], grads[n][0], given["m_" + n][0], given["v_" + n][0], "adamw_" + n)
        delta[n], new_m[n], new_v[n] = d.reshape(shp), m2.reshape(shp), v2.reshape(shp)
    shapes = [given[n].shape for n in SMALL]
    d, m2, v2 = _adamw(_pack([given[n] for n in SMALL]), _pack([grads[n] for n in SMALL]),
                       _pack([given["m_" + n] for n in SMALL]), _pack([given["v_" + n] for n in SMALL]), "adamw_small")
    for n, dd, mm, vv in zip(SMALL, _unpack(d, shapes), _unpack(m2, shapes), _unpack(v2, shapes)):
        delta[n], new_m[n], new_v[n] = dd, mm, vv

    return (loss, dx.reshape(x.shape), *[grads[n] for n in WEIGHTS], *[delta[n] for n in WEIGHTS],
            *[new_m[n] for n in WEIGHTS], *[new_v[n] for n in WEIGHTS])
```

```python
import functools
import math

import jax
import jax.numpy as jnp
from jax import lax
from jax.experimental import pallas as pl
from jax.experimental.pallas import tpu as pltpu

F32, BF16 = jnp.float32, jnp.bfloat16
MESH = pl.DeviceIdType.MESH

D_MODEL = 1024
LRU_WIDTH = 512
SB_WIDTH = 512
DH = 64
IN_COLS = 2 * LRU_WIDTH + 3 * SB_WIDTH
D_FF = 4 * D_MODEL
CONV_WIDTH = 4
LRU_C = 8.0
EPS = 1e-6
N_CHIPS = 4
N_DEV = 8
LANES = 128
SUBLANES = 8
TQ = 256
TK = 128
ATT_SCALE = 1.0 / math.sqrt(DH)
VMEM_LIMIT = 52 * 1024 * 1024

ADAM_LR, ADAM_B1, ADAM_B2, ADAM_EPS, ADAM_WD, ADAM_STEP = 0.001, 0.9, 0.999, 1e-08, 0.01, 10

_GELU_K = math.sqrt(2.0 / math.pi)
_GELU_C = 0.044715


def _cp(sem):
    return pltpu.CompilerParams(dimension_semantics=sem, vmem_limit_bytes=VMEM_LIMIT)


def _dot(a, b):
    return jnp.dot(a, b, preferred_element_type=F32)


def _dot_nt(a, b):
    return lax.dot_general(a, b, (((1,), (1,)), ((), ())), preferred_element_type=F32)


def _dot_tn(a, b):
    return lax.dot_general(a, b, (((0,), (0,)), ((), ())), preferred_element_type=F32)


def _rstd(x):
    return lax.rsqrt(jnp.mean(x * x, axis=-1, keepdims=True) + EPS)


def _rms_bwd(x, g, dy):
    r = _rstd(x)
    gd = g * dy
    dx = r * gd - x * (r * r * r) * jnp.mean(x * gd, axis=-1, keepdims=True)
    return dx, jnp.sum(dy * x * r, axis=0, keepdims=True)


def _sigmoid(x):
    return 1.0 / (1.0 + jnp.exp(-x))


def _softplus(x):
    return jnp.maximum(x, 0.0) + jnp.log(1.0 + jnp.exp(-jnp.abs(x)))


def _neg_expm1(x):
    series = -x * (1.0 + x * (0.5 + x * (1.0 / 6.0 + x * (1.0 / 24.0))))
    return jnp.where(x > -0.01, series, 1.0 - jnp.exp(x))


def _gelu(g):
    t = jnp.tanh(_GELU_K * (g + _GELU_C * g * g * g))
    return 0.5 * g * (1.0 + t), t


def _gelu_grad(g, t):
    return 0.5 * (1.0 + t) + 0.5 * g * (1.0 - t * t) * _GELU_K * (1.0 + 3.0 * _GELU_C * g * g)


def _rows(shape):
    return lax.broadcasted_iota(jnp.int32, shape, 0)


def _shift_down(x, s, fill):
    return jnp.where(_rows(x.shape) >= s, pltpu.roll(x, s, 0), fill)


def _shift_up(x, s, fill):
    n = x.shape[0]
    return jnp.where(_rows(x.shape) < n - s, pltpu.roll(x, n - s, 0), fill)


def _row_of(x, idx):
    return jnp.sum(jnp.where(_rows(x.shape) == idx, x, 0.0), axis=0, keepdims=True)


def _matmul(a, b, dims, out_dtype, name, tm, tn, tk, split_cols=False):
    if dims == "nn":
        (m, kk), n, dot = a.shape, b.shape[1], _dot
    elif dims == "nt":
        (m, kk), n, dot = a.shape, b.shape[0], _dot_nt
    else:
        (kk, m), n, dot = a.shape, b.shape[1], _dot_tn
    tm, tn, tk = min(tm, m), min(tn, n), min(tk, kk)
    assert m % tm == 0 and n % tn == 0 and kk % tk == 0, (name, m, n, kk)
    nk = kk // tk

    def kern(a_ref, b_ref, o_ref, acc_ref):
        k = pl.program_id(2)

        @pl.when(k == 0)
        def _():
            acc_ref[...] = jnp.zeros_like(acc_ref)

        acc_ref[...] += dot(a_ref[...].astype(BF16), b_ref[...].astype(BF16))

        @pl.when(k == nk - 1)
        def _():
            o_ref[...] = acc_ref[...].astype(o_ref.dtype)

    if split_cols:
        out_shape = jax.ShapeDtypeStruct((n // tn, m, tn), out_dtype)
        o_spec = pl.BlockSpec((None, tm, tn), lambda i, j, k: (j, i, 0))
    else:
        out_shape = jax.ShapeDtypeStruct((m, n), out_dtype)
        o_spec = pl.BlockSpec((tm, tn), lambda i, j, k: (i, j))
    if dims == "nn":
        a_spec = pl.BlockSpec((tm, tk), lambda i, j, k: (i, k))
        b_spec = pl.BlockSpec((tk, tn), lambda i, j, k: (k, j))
    elif dims == "nt":
        a_spec = pl.BlockSpec((tm, tk), lambda i, j, k: (i, k))
        b_spec = pl.BlockSpec((tn, tk), lambda i, j, k: (j, k))
    else:
        a_spec = pl.BlockSpec((tk, tm), lambda i, j, k: (k, i))
        b_spec = pl.BlockSpec((tk, tn), lambda i, j, k: (k, j))
    return pl.pallas_call(
        kern, grid=(m // tm, n // tn, nk), in_specs=[a_spec, b_spec], out_specs=o_spec, out_shape=out_shape,
        scratch_shapes=[pltpu.VMEM((tm, tn), F32)], compiler_params=_cp(("parallel", "parallel", "arbitrary")), name=name,
    )(a, b)


def _inproj(x, g1, w_in):
    t = x.shape[0]
    tm = min(512, t)

    def kern(x_ref, g_ref, w_ref, xl_ref, qkv_ref, xn_ref):
        xv = x_ref[...]
        xn = (xv * _rstd(xv) * g_ref[...]).astype(BF16)
        xn_ref[...] = xn
        xl_ref[...] = _dot(xn, w_ref[:, : 2 * LRU_WIDTH])
        qkv_ref[...] = _dot(xn, w_ref[:, 2 * LRU_WIDTH:]).astype(BF16)

    row = lambda c: pl.BlockSpec((tm, c), lambda i: (i, 0))
    return pl.pallas_call(
        kern, grid=(t // tm,),
        in_specs=[row(D_MODEL), pl.BlockSpec((1, D_MODEL), lambda i: (0, 0)), pl.BlockSpec((D_MODEL, IN_COLS), lambda i: (0, 0))],
        out_specs=[row(2 * LRU_WIDTH), row(3 * SB_WIDTH), row(D_MODEL)],
        out_shape=[jax.ShapeDtypeStruct((t, 2 * LRU_WIDTH), F32), jax.ShapeDtypeStruct((t, 3 * SB_WIDTH), BF16),
                   jax.ShapeDtypeStruct((t, D_MODEL), BF16)],
        compiler_params=_cp(("parallel",)), name="inproj",
    )(x, g1, w_in)


def _conv_taps(hist, u):
    cat = jnp.concatenate([hist, u], axis=0)
    return [pltpu.roll(cat, CONV_WIDTH - 1 - k, 0)[SUBLANES:] for k in range(CONV_WIDTH - 1)] + [u]


def _lru_gates(c, wbd_ref, ba, bx, sp):
    gas, gxs = [], []
    for p in range(LRU_WIDTH // LANES):
        gax = _dot(c[:, LANES * p: LANES * (p + 1)].astype(BF16), wbd_ref[p])
        gas.append(gax[:, :LANES])
        gxs.append(gax[:, LANES:])
    r = _sigmoid(jnp.concatenate(gas, axis=1) + ba)
    i = _sigmoid(jnp.concatenate(gxs, axis=1) + bx)
    la = (-LRU_C) * r * sp
    a = jnp.exp(la)
    mult = jnp.sqrt(_neg_expm1(2.0 * la))
    return r, i, a, mult


def _scan_fwd(a, b):
    s = 1
    while s < a.shape[0]:
        b = b + a * _shift_down(b, s, 0.0)
        a = a * _shift_down(a, s, 1.0)
        s *= 2
    return a, b


def _scan_rev(a, b):
    s = 1
    while s < a.shape[0]:
        b = b + a * _shift_up(b, s, 0.0)
        a = a * _shift_up(a, s, 1.0)
        s *= 2
    return a, b


def _lru_param_specs(grid_rank):
    z2 = (lambda e, c: (0, 0)) if grid_rank == 2 else None
    return [
        pl.BlockSpec((CONV_WIDTH, LRU_WIDTH), z2), pl.BlockSpec((1, LRU_WIDTH), z2),
        pl.BlockSpec((LRU_WIDTH // LANES, LANES, 2 * LANES), lambda e, c: (0, 0, 0)),
        pl.BlockSpec((1, LRU_WIDTH), z2), pl.BlockSpec((1, LRU_WIDTH), z2), pl.BlockSpec((1, LRU_WIDTH), z2),
    ]


def _lru_fwd(xl, conv_w, conv_b, wbd, ba, bx, lam, seq):
    t = xl.shape[0]
    tc = min(512, seq)
    nc = seq // tc

    def kern(u_ref, g_ref, cw_ref, cb_ref, wbd_ref, ba_ref, bx_ref, lam_ref, h_ref, y_ref, hist_ref, hcar_ref):
        @pl.when(pl.program_id(1) == 0)
        def _():
            hist_ref[...] = jnp.zeros_like(hist_ref)
            hcar_ref[...] = jnp.zeros_like(hcar_ref)

        u = u_ref[...]
        taps = _conv_taps(hist_ref[...], u)
        hist_ref[...] = u_ref[tc - SUBLANES:, :]
        c = cb_ref[...]
        for k in range(CONV_WIDTH):
            c = c + taps[k] * cw_ref[k:k + 1, :]
        sp = _softplus(-lam_ref[...])
        _, i, a, mult = _lru_gates(c, wbd_ref, ba_ref[...], bx_ref[...], sp)
        aa, bb = _scan_fwd(a, mult * i * c)
        h = bb + aa * hcar_ref[0:1, :]
        h_ref[...] = h
        hcar_ref[0:1, :] = h_ref[tc - 1:tc, :]
        y_ref[...] = h * _gelu(g_ref[...])[0]

    chunk = lambda col: pl.BlockSpec((tc, LRU_WIDTH), lambda e, c: (e * nc + c, col))
    out = jax.ShapeDtypeStruct((t, LRU_WIDTH), F32)
    return pl.pallas_call(
        kern, grid=(t // seq, nc), in_specs=[chunk(0), chunk(1)] + _lru_param_specs(2),
        out_specs=[chunk(0), chunk(0)], out_shape=[out, out],
        scratch_shapes=[pltpu.VMEM((SUBLANES, LRU_WIDTH), F32), pltpu.VMEM((SUBLANES, LRU_WIDTH), F32)],
        compiler_params=_cp(("arbitrary", "arbitrary")), name="lru_fwd",
    )(xl, xl, conv_w, conv_b, wbd, ba, bx, lam)


def _att_consts():
    row = lax.broadcasted_iota(jnp.int32, (TQ, 2 * TK), 0)
    key = lax.broadcasted_iota(jnp.int32, (TQ, 2 * TK), 1) & (TK - 1)
    return [(TK * jj + key) < row for jj in range(TQ // TK)]


def _sum_matrix(kind):
    j = lax.broadcasted_iota(jnp.int32, (2 * TK, 2 * TK), 0) & (TK - 1)
    s = lax.broadcasted_iota(jnp.int32, (2 * TK, 2 * TK), 1)
    pick = {"after": j > s, "upto": j <= s, "before": j < s}[kind]
    return jnp.where((s >= TK) | pick, 1.0, 0.0).astype(BF16)


def _hi_lo(x):
    hi = x.astype(BF16)
    return hi, (x - hi.astype(F32)).astype(BF16)


def _pair_sums(x, m):
    hi, lo = _hi_lo(x)
    out = []
    for hd in range(2):
        cols = slice(hd * TK, (hd + 1) * TK)
        out.append(_dot(jnp.concatenate([hi[:, cols], lo[:, cols]], axis=1), m))
    return [o[:, :TK] for o in out], [o[:, TK:] for o in out]


def _att_logits(qb, kbd):
    z = _dot(qb, kbd)
    lg = jnp.log(1.0 + jnp.exp(-jnp.abs(z)))
    lb = jnp.minimum(z, 0.0) - lg
    return lb, lb - z


def _head_diag(x, rows_first):
    n = x.shape[0] if rows_first else x.shape[1]
    idx = lax.broadcasted_iota(jnp.int32, x.shape, 0 if rows_first else 1)
    return jnp.where(idx < n // 2, x, 0), jnp.where(idx >= n // 2, x, 0)


def _scaled_q(q_ref, q0):
    return (q_ref[pl.ds(q0, TQ), :].astype(F32) * ATT_SCALE).astype(BF16)


def _qkv_specs(seq):
    n = SB_WIDTH // LANES
    return [pl.BlockSpec((seq, LANES), lambda e, p, off=off: (e, off * n + p)) for off in range(3)]


def _attn_fwd(qkv, seq):
    t = qkv.shape[0]
    ne, nq, nk = t // seq, seq // TQ, seq // TK

    def kern(q_ref, k_ref, v_ref, o_ref, tot_ref, kbd_scr, vbd_scr):
        causal = _att_consts()
        after = _sum_matrix("after")

        def prep(j, _):
            k0 = pl.multiple_of(j * TK, TK)
            top, bot = _head_diag(k_ref[pl.ds(k0, TK), :].astype(F32).T, True)
            kbd_scr[j] = jnp.concatenate([top, bot], axis=1).astype(BF16)
            left, right = _head_diag(v_ref[pl.ds(k0, TK), :], False)
            vbd_scr[j] = jnp.concatenate([left, right], axis=0)
            return 0

        lax.fori_loop(0, nk, prep, 0)

        def block(j, qb, st, mask):
            c0, c1, oacc = st
            lb, l1 = _att_logits(qb, kbd_scr[j])
            if mask is not None:
                l1 = jnp.where(mask, l1, 0.0)
            (s0, s1), (r0, r1) = _pair_sums(l1, after)
            att = jnp.exp(lb + jnp.concatenate([s0 + c0, s1 + c1], axis=1))
            if mask is not None:
                att = jnp.where(mask, att, 0.0)
            return c0 + r0, c1 + r1, oacc + _dot(att.astype(BF16), vbd_scr[j])

        def qloop(qi, _):
            q0 = pl.multiple_of(qi * TQ, TQ)
            qb = _scaled_q(q_ref, q0)
            zero = jnp.zeros((TQ, TK), F32)
            st = (zero, zero, jnp.zeros((TQ, LANES), F32))
            for jj in reversed(range(TQ // TK)):
                st = block((TQ // TK) * qi + jj, qb, st, causal[jj])

            def kloop(it, st):
                j = (TQ // TK) * (qi - it) - 1
                return block(j - 1, qb, block(j, qb, st, None), None)

            c0, c1, oacc = lax.fori_loop(0, qi, kloop, st)
            o_ref[pl.ds(q0, TQ), :] = oacc
            tot_ref[pl.ds(q0, TQ), :] = jnp.concatenate([c0, c1], axis=1)
            return 0

        lax.fori_loop(0, nq, qloop, 0)

    return pl.pallas_call(
        kern, grid=(ne, SB_WIDTH // LANES), in_specs=_qkv_specs(seq),
        out_specs=[pl.BlockSpec((seq, LANES), lambda e, p: (e, p)), pl.BlockSpec((seq, 2 * TK), lambda e, p: (e, p))],
        out_shape=[jax.ShapeDtypeStruct((t, SB_WIDTH), F32), jax.ShapeDtypeStruct((t, 2 * TK * SB_WIDTH // LANES), F32)],
        scratch_shapes=[pltpu.VMEM((nk, LANES, 2 * TK), BF16), pltpu.VMEM((nk, 2 * TK, LANES), BF16)],
        compiler_params=_cp(("parallel", "parallel")), name="attn_fwd",
    )(qkv, qkv, qkv)


def _outproj(y_lru, o, x, ga, gb, w_out):
    t = x.shape[0]
    tm = min(512, t)

    def kern(y_ref, o_ref, x_ref, ga_ref, gb_ref, w_ref, h1_ref, mix_ref):
        yv, ov = y_ref[...], o_ref[...]
        mix = jnp.concatenate([yv * _rstd(yv) * ga_ref[...], ov * _rstd(ov) * gb_ref[...]], axis=1).astype(BF16)
        mix_ref[...] = mix
        h1_ref[...] = x_ref[...] + _dot(mix, w_ref[...])

    row = lambda c: pl.BlockSpec((tm, c), lambda i: (i, 0))
    vec = lambda c: pl.BlockSpec((1, c), lambda i: (0, 0))
    return pl.pallas_call(
        kern, grid=(t // tm,),
        in_specs=[row(LRU_WIDTH), row(SB_WIDTH), row(D_MODEL), vec(LRU_WIDTH), vec(SB_WIDTH),
                  pl.BlockSpec((D_MODEL, D_MODEL), lambda i: (0, 0))],
        out_specs=[row(D_MODEL), row(D_MODEL)],
        out_shape=[jax.ShapeDtypeStruct((t, D_MODEL), F32), jax.ShapeDtypeStruct((t, D_MODEL), BF16)],
        compiler_params=_cp(("parallel",)), name="outproj",
    )(y_lru, o, x, ga, gb, w_out)


def _mlp_fwd(h1, g2, w_up, w_down):
    t = h1.shape[0]
    tm, tf = min(512, t), 1024
    nf = D_FF // tf

    def kern(h1_ref, g_ref, wu_ref, wd_ref, h2_ref, hn_ref, up_ref, u2_ref, acc_ref):
        f = pl.program_id(1)

        @pl.when(f == 0)
        def _():
            hv = h1_ref[...]
            hn_ref[...] = (hv * _rstd(hv) * g_ref[...]).astype(BF16)
            acc_ref[...] = hv

        up = jnp.maximum(_dot(hn_ref[...], wu_ref[...]), 0.0)
        u2 = (up * up).astype(BF16)
        up_ref[...] = up.astype(BF16)
        u2_ref[...] = u2
        acc_ref[...] += _dot(u2, wd_ref[...])

        @pl.when(f == nf - 1)
        def _():
            h2_ref[...] = acc_ref[...]

    return pl.pallas_call(
        kern, grid=(t // tm, nf),
        in_specs=[pl.BlockSpec((tm, D_MODEL), lambda i, f: (i, 0)), pl.BlockSpec((1, D_MODEL), lambda i, f: (0, 0)),
                  pl.BlockSpec((D_MODEL, tf), lambda i, f: (0, f)), pl.BlockSpec((tf, D_MODEL), lambda i, f: (f, 0))],
        out_specs=[pl.BlockSpec((tm, D_MODEL), lambda i, f: (i, 0)), pl.BlockSpec((tm, D_MODEL), lambda i, f: (i, 0)),
                   pl.BlockSpec((tm, tf), lambda i, f: (i, f)), pl.BlockSpec((tm, tf), lambda i, f: (i, f))],
        out_shape=[jax.ShapeDtypeStruct((t, D_MODEL), F32), jax.ShapeDtypeStruct((t, D_MODEL), BF16),
                   jax.ShapeDtypeStruct((t, D_FF), BF16), jax.ShapeDtypeStruct((t, D_FF), BF16)],
        scratch_shapes=[pltpu.VMEM((tm, D_MODEL), F32)],
        compiler_params=_cp(("parallel", "arbitrary")), name="mlp_fwd",
    )(h1, g2, w_up, w_down)


def _loss_head(h2, target, gf):
    t = h2.shape[0]
    tm = min(512, t)

    def kern(h_ref, t_ref, g_ref, dh_ref, loss_ref, dg_ref):
        @pl.when(pl.program_id(0) == 0)
        def _():
            loss_ref[...] = jnp.zeros_like(loss_ref)
            dg_ref[...] = jnp.zeros_like(dg_ref)

        hv, g = h_ref[...], g_ref[...]
        err = hv * _rstd(hv) * g - t_ref[...]
        lane = lax.broadcasted_iota(jnp.int32, (1, LANES), 1)
        loss_ref[...] += jnp.where(lane == 0, 0.5 * jnp.sum(err * err) / D_MODEL, 0.0)
        dx, dg = _rms_bwd(hv, g, err * (1.0 / D_MODEL))
        dh_ref[...] = dx
        dg_ref[...] += dg

    row = pl.BlockSpec((tm, D_MODEL), lambda i: (i, 0))
    vec = pl.BlockSpec((1, D_MODEL), lambda i: (0, 0))
    return pl.pallas_call(
        kern, grid=(t // tm,), in_specs=[row, row, vec],
        out_specs=[row, pl.BlockSpec((1, LANES), lambda i: (0, 0)), vec],
        out_shape=[jax.ShapeDtypeStruct((t, D_MODEL), F32), jax.ShapeDtypeStruct((1, LANES), F32),
                   jax.ShapeDtypeStruct((1, D_MODEL), F32)],
        compiler_params=_cp(("arbitrary",)), name="loss_head",
    )(h2, target, gf)


def _mlp_bwd_pre(dh2, w_down, up):
    t = dh2.shape[0]
    tm, tf = min(512, t), 1024

    def kern(d_ref, w_ref, up_ref, o_ref):
        du2 = _dot_nt(d_ref[...].astype(BF16), w_ref[...])
        o_ref[...] = (du2 * (2.0 * up_ref[...].astype(F32))).astype(BF16)

    return pl.pallas_call(
        kern, grid=(t // tm, D_FF // tf),
        in_specs=[pl.BlockSpec((tm, D_MODEL), lambda i, f: (i, 0)), pl.BlockSpec((tf, D_MODEL), lambda i, f: (f, 0)),
                  pl.BlockSpec((tm, tf), lambda i, f: (i, f))],
        out_specs=pl.BlockSpec((tm, tf), lambda i, f: (i, f)), out_shape=jax.ShapeDtypeStruct((t, D_FF), BF16),
        compiler_params=_cp(("parallel", "parallel")), name="mlp_bwd_pre",
    )(dh2, w_down, up)


def _proj_bwd_norm(dys, w, x, g, resid, name):
    t = x.shape[0]
    tm = min(512, t)
    widths = [dy.shape[1] for dy in dys]
    n = len(dys)

    def kern(*refs):
        dy_refs, (w_ref, x_ref, g_ref, r_ref, dx_ref, dg_ref) = refs[:n], refs[n:]

        @pl.when(pl.program_id(0) == 0)
        def _():
            dg_ref[...] = jnp.zeros_like(dg_ref)

        off, dxn = 0, None
        for dy_ref, wd in zip(dy_refs, widths):
            part = _dot_nt(dy_ref[...], w_ref[:, off:off + wd])
            dxn = part if dxn is None else dxn + part
            off += wd
        dx, dg = _rms_bwd(x_ref[...], g_ref[...], dxn)
        dx_ref[...] = r_ref[...] + dx
        dg_ref[...] += dg

    row = lambda c: pl.BlockSpec((tm, c), lambda i: (i, 0))
    vec = pl.BlockSpec((1, D_MODEL), lambda i: (0, 0))
    return pl.pallas_call(
        kern, grid=(t // tm,),
        in_specs=[row(wd) for wd in widths] + [pl.BlockSpec(w.shape, lambda i: (0, 0)), row(D_MODEL), vec, row(D_MODEL)],
        out_specs=[row(D_MODEL), vec],
        out_shape=[jax.ShapeDtypeStruct((t, D_MODEL), F32), jax.ShapeDtypeStruct((1, D_MODEL), F32)],
        compiler_params=_cp(("arbitrary",)), name=name,
    )(*dys, w, x, g, resid)


def _outproj_bwd(dh1, w_out, y_lru, o, ga, gb):
    t = dh1.shape[0]
    tm = min(512, t)

    def kern(d_ref, w_ref, y_ref, o_ref, ga_ref, gb_ref, dy_ref, do_ref, dga_ref, dgb_ref):
        @pl.when(pl.program_id(0) == 0)
        def _():
            dga_ref[...] = jnp.zeros_like(dga_ref)
            dgb_ref[...] = jnp.zeros_like(dgb_ref)

        dmix = _dot_nt(d_ref[...].astype(BF16), w_ref[...])
        dy, dga = _rms_bwd(y_ref[...], ga_ref[...], dmix[:, :LRU_WIDTH])
        do, dgb = _rms_bwd(o_ref[...], gb_ref[...], dmix[:, LRU_WIDTH:])
        dy_ref[...] = dy
        do_ref[...] = do
        dga_ref[...] += dga
        dgb_ref[...] += dgb

    row = lambda c: pl.BlockSpec((tm, c), lambda i: (i, 0))
    vec = pl.BlockSpec((1, LRU_WIDTH), lambda i: (0, 0))
    half = jax.ShapeDtypeStruct((t, LRU_WIDTH), F32)
    gsum = jax.ShapeDtypeStruct((1, LRU_WIDTH), F32)
    return pl.pallas_call(
        kern, grid=(t // tm,),
        in_specs=[row(D_MODEL), pl.BlockSpec((D_MODEL, D_MODEL), lambda i: (0, 0)), row(LRU_WIDTH), row(SB_WIDTH), vec, vec],
        out_specs=[row(LRU_WIDTH), row(SB_WIDTH), vec, vec], out_shape=[half, half, gsum, gsum],
        compiler_params=_cp(("arbitrary",)), name="outproj_bwd",
    )(dh1, w_out, y_lru, o, ga, gb)


def _attn_bwd(qkv, do, tot, seq):
    t = qkv.shape[0]
    ne, nq, nk = t // seq, seq // TQ, seq // TK

    def kern(q_ref, k_ref, v_ref, do_ref, tot_ref, dq_ref, dk_ref, dv_ref, kbd_scr, vtbd_scr, kbd2_scr, dkt_scr, dvt_scr):
        causal = _att_consts()
        upto, before = _sum_matrix("upto"), _sum_matrix("before")

        def prep(j, _):
            k0 = pl.multiple_of(j * TK, TK)
            kb = k_ref[pl.ds(k0, TK), :]
            top, bot = _head_diag(kb.astype(F32).T, True)
            kbd_scr[j] = jnp.concatenate([top, bot], axis=1).astype(BF16)
            top, bot = _head_diag(v_ref[pl.ds(k0, TK), :].astype(F32).T, True)
            vtbd_scr[j] = jnp.concatenate([top, bot], axis=1).astype(BF16)
            left, right = _head_diag(kb, False)
            kbd2_scr[j] = jnp.concatenate([left, right], axis=0)
            dkt_scr[j] = jnp.zeros((LANES, 2 * TK), F32)
            dvt_scr[j] = jnp.zeros((LANES, 2 * TK), F32)
            return 0

        lax.fori_loop(0, nk, prep, 0)

        def block(j, qb, qt, dob, dot_, totb, st, mask):
            f0, f1, p0, p1, dqacc = st
            lb, l1 = _att_logits(qb, kbd_scr[j])
            if mask is not None:
                l1 = jnp.where(mask, l1, 0.0)
            (s0, s1), (r0, r1) = _pair_sums(l1, upto)
            att = jnp.exp(lb + (totb - jnp.concatenate([s0 + f0, s1 + f1], axis=1)))
            if mask is not None:
                att = jnp.where(mask, att, 0.0)
            pw = att * _dot(dob, vtbd_scr[j])
            (e0, e1), (t0, t1) = _pair_sums(pw, before)
            dz = pw - jnp.exp(lb) * (pw + jnp.concatenate([e0 + p0, e1 + p1], axis=1))
            if mask is not None:
                dz = jnp.where(mask, dz, 0.0)
            dzb = dz.astype(BF16)
            dkt_scr[j] += _dot(qt, dzb)
            dvt_scr[j] += _dot(dot_, att.astype(BF16))
            return f0 + r0, f1 + r1, p0 + t0, p1 + t1, dqacc + _dot(dzb, kbd2_scr[j])

        def qloop(qi, _):
            q0 = pl.multiple_of(qi * TQ, TQ)
            qb = _scaled_q(q_ref, q0)
            qt = qb.astype(F32).T.astype(BF16)
            do32 = do_ref[pl.ds(q0, TQ), :]
            dob, dot_ = do32.astype(BF16), do32.T.astype(BF16)
            totb = tot_ref[pl.ds(q0, TQ), :]
            zero = jnp.zeros((TQ, TK), F32)
            st = (zero, zero, zero, zero, jnp.zeros((TQ, LANES), F32))

            def kloop(it, st):
                j = (TQ // TK) * it
                return block(j + 1, qb, qt, dob, dot_, totb, block(j, qb, qt, dob, dot_, totb, st, None), None)

            st = lax.fori_loop(0, qi, kloop, st)
            for jj in range(TQ // TK):
                st = block((TQ // TK) * qi + jj, qb, qt, dob, dot_, totb, st, causal[jj])
            dq_ref[pl.ds(q0, TQ), :] = (st[4] * ATT_SCALE).astype(BF16)
            return 0

        lax.fori_loop(0, nq, qloop, 0)

        def finish(j, _):
            k0 = pl.multiple_of(j * TK, TK)
            head0 = lax.broadcasted_iota(jnp.int32, (LANES, TK), 0) < DH
            for src, dst in ((dkt_scr, dk_ref), (dvt_scr, dv_ref)):
                acc = src[j]
                dst[pl.ds(k0, TK), :] = jnp.where(head0, acc[:, :TK], acc[:, TK:]).T.astype(BF16)
            return 0

        lax.fori_loop(0, nk, finish, 0)

    blk = pl.BlockSpec((seq, LANES), lambda e, p: (e, p))
    out = jax.ShapeDtypeStruct((t, SB_WIDTH), BF16)
    return pl.pallas_call(
        kern, grid=(ne, SB_WIDTH // LANES),
        in_specs=_qkv_specs(seq) + [blk, pl.BlockSpec((seq, 2 * TK), lambda e, p: (e, p))],
        out_specs=[blk, blk, blk], out_shape=[out, out, out],
        scratch_shapes=[pltpu.VMEM((nk, LANES, 2 * TK), BF16), pltpu.VMEM((nk, LANES, 2 * TK), BF16),
                        pltpu.VMEM((nk, 2 * TK, LANES), BF16), pltpu.VMEM((nk, LANES, 2 * TK), F32),
                        pltpu.VMEM((nk, LANES, 2 * TK), F32)],
        compiler_params=_cp(("parallel", "parallel")), name="attn_bwd",
    )(qkv, qkv, qkv, do, tot)


def _lru_bwd(xl, h, dy, conv_w, conv_b, wbd, ba, bx, lam, seq):
    t = xl.shape[0]
    tc = min(512, seq)
    nc = seq // tc
    nb = tc // SUBLANES

    def kern(u_ref, g_ref, up_ref, h_ref, hp_ref, dy_ref, cw_ref, cb_ref, wbd_ref, ba_ref, bx_ref, lam_ref,
             dxl_ref, small_ref, dwbd_ref, lnext_ref, anext_ref, dcnext_ref):
        e, ci = pl.program_id(0), pl.program_id(1)
        first = ci == nc - 1

        @pl.when((e == 0) & (ci == 0))
        def _():
            small_ref[...] = jnp.zeros_like(small_ref)
            dwbd_ref[...] = jnp.zeros_like(dwbd_ref)

        @pl.when(ci == 0)
        def _():
            lnext_ref[...] = jnp.zeros_like(lnext_ref)
            anext_ref[...] = jnp.zeros_like(anext_ref)
            dcnext_ref[...] = jnp.zeros_like(dcnext_ref)

        u, g = u_ref[...], g_ref[...]
        keep = jnp.where(first, 0.0, 1.0)
        taps = _conv_taps(keep * up_ref[...], u)
        c = cb_ref[...]
        for k in range(CONV_WIDTH):
            c = c + taps[k] * cw_ref[k:k + 1, :]
        lam = lam_ref[...]
        sp = _softplus(-lam)
        r, i, a, mult = _lru_gates(c, wbd_ref, ba_ref[...], bx_ref[...], sp)
        gel, th = _gelu(g)
        dyv, hv = dy_ref[...], h_ref[...]
        dg = dyv * hv * _gelu_grad(g, th)

        aa, bb = _scan_rev(_shift_up(a, 1, anext_ref[0:1, :]), dyv * gel)
        lt = bb + aa * lnext_ref[0:1, :]
        lnext_ref[0:1, :] = _row_of(lt, 0)
        anext_ref[0:1, :] = _row_of(a, 0)

        hprev = _shift_down(hv, 1, keep * hp_ref[SUBLANES - 1:SUBLANES, :])
        da = lt * hprev
        dmult = lt * i * c
        di = lt * mult * c
        dc = lt * mult * i
        dla = da * a - dmult * (a * a) / mult
        dga = dla * ((-LRU_C) * sp) * r * (1.0 - r)
        dgx = di * i * (1.0 - i)
        small_ref[7:8, :] += jnp.sum(dla * r, axis=0, keepdims=True) * (LRU_C * _sigmoid(-lam))
        small_ref[5:6, :] += jnp.sum(dga, axis=0, keepdims=True)
        small_ref[6:7, :] += jnp.sum(dgx, axis=0, keepdims=True)

        dcs = []
        for p in range(LRU_WIDTH // LANES):
            cols = slice(LANES * p, LANES * (p + 1))
            dgax = jnp.concatenate([dga[:, cols], dgx[:, cols]], axis=1).astype(BF16)
            dcs.append(_dot_nt(dgax, wbd_ref[p]))
            dwbd_ref[p] += _dot_tn(c[:, cols].astype(BF16), dgax)
        dc = dc + jnp.concatenate(dcs, axis=1)
        small_ref[4:5, :] += jnp.sum(dc, axis=0, keepdims=True)

        catd = jnp.concatenate([dc, dcnext_ref[...]], axis=0)
        du = dc * cw_ref[CONV_WIDTH - 1:CONV_WIDTH, :]
        for j in range(1, CONV_WIDTH):
            du = du + pltpu.roll(catd, tc + SUBLANES - j, 0)[:tc] * cw_ref[CONV_WIDTH - 1 - j:CONV_WIDTH - j, :]
        dcnext_ref[...] = dc[:SUBLANES]
        for k in range(CONV_WIDTH):
            small_ref[k:k + 1, :] += jnp.sum(dc * taps[k], axis=0, keepdims=True)
        dxl_ref[:, :LRU_WIDTH] = du.astype(BF16)
        dxl_ref[:, LRU_WIDTH:] = dg.astype(BF16)

    rev = lambda e, c: e * nc + (nc - 1 - c)
    chunk = lambda col: pl.BlockSpec((tc, LRU_WIDTH), lambda e, c: (rev(e, c), col))
    prev8 = pl.BlockSpec((SUBLANES, LRU_WIDTH), lambda e, c: (jnp.maximum(rev(e, c) * nb - 1, 0), 0))
    return pl.pallas_call(
        kern, grid=(t // seq, nc),
        in_specs=[chunk(0), chunk(1), prev8, chunk(0), prev8, chunk(0)] + _lru_param_specs(2),
        out_specs=[pl.BlockSpec((tc, 2 * LRU_WIDTH), lambda e, c: (rev(e, c), 0)),
                   pl.BlockSpec((SUBLANES, LRU_WIDTH), lambda e, c: (0, 0)),
                   pl.BlockSpec((LRU_WIDTH // LANES, LANES, 2 * LANES), lambda e, c: (0, 0, 0))],
        out_shape=[jax.ShapeDtypeStruct((t, 2 * LRU_WIDTH), BF16), jax.ShapeDtypeStruct((SUBLANES, LRU_WIDTH), F32),
                   jax.ShapeDtypeStruct((LRU_WIDTH // LANES, LANES, 2 * LANES), F32)],
        scratch_shapes=[pltpu.VMEM((SUBLANES, LRU_WIDTH), F32)] * 3,
        compiler_params=_cp(("arbitrary", "arbitrary")), name="lru_bwd",
    )(xl, xl, xl, h, h, dy, conv_w, conv_b, wbd, ba, bx, lam)


def _adamw(w, g, m, v, name):
    rows, cols = w.shape
    tr = 256 if rows % 256 == 0 else rows

    def kern(w_ref, g_ref, m_ref, v_ref, d_ref, m2_ref, v2_ref):
        gv = g_ref[...]
        m2 = ADAM_B1 * m_ref[...] + (1.0 - ADAM_B1) * gv
        v2 = ADAM_B2 * v_ref[...] + (1.0 - ADAM_B2) * (gv * gv)
        m_hat = m2 / (1.0 - ADAM_B1 ** ADAM_STEP)
        v_hat = v2 / (1.0 - ADAM_B2 ** ADAM_STEP)
        d_ref[...] = -ADAM_LR * (m_hat / (jnp.sqrt(v_hat) + ADAM_EPS) + ADAM_WD * w_ref[...])
        m2_ref[...] = m2
        v2_ref[...] = v2

    blk = pl.BlockSpec((tr, cols), lambda i: (i, 0))
    out = jax.ShapeDtypeStruct((rows, cols), F32)
    return pl.pallas_call(kern, grid=(rows // tr,), in_specs=[blk] * 4, out_specs=[blk] * 3, out_shape=[out] * 3,
                          compiler_params=_cp(("parallel",)), name=name)(w, g, m, v)


ANY = pl.BlockSpec(memory_space=pl.ANY)


def _place():
    return lax.axis_index("x"), lax.axis_index("y"), lax.axis_index("c")


def _other_chips(x, y):
    return [(1 - x, y), (x, 1 - y), (1 - x, 1 - y)]


def _gather_weights(shards):
    n = len(shards)

    def kern(*refs):
        ins, outs, (send_sems, recv_sems, loc_sems) = refs[:n], refs[n:2 * n], refs[2 * n:]
        x, y, c = _place()
        mine = 2 * x + y
        started = []
        for w in range(n):
            lc = pltpu.make_async_copy(ins[w], outs[w].at[mine], loc_sems.at[w])
            lc.start()
            started.append(lc)
            for k, chip in enumerate(_other_chips(x, y)):
                cp = pltpu.make_async_remote_copy(
                    src_ref=ins[w], dst_ref=outs[w].at[mine], send_sem=send_sems.at[3 * w + k],
                    recv_sem=recv_sems.at[3 * w + k], device_id=(*chip, c), device_id_type=MESH)
                cp.start()
                started.append(cp)
        for cp in started:
            cp.wait()

    return pl.pallas_call(
        kern, in_specs=[ANY] * n, out_specs=[ANY] * n,
        out_shape=[jax.ShapeDtypeStruct((N_CHIPS,) + s.shape, s.dtype) for s in shards],
        scratch_shapes=[pltpu.SemaphoreType.DMA((3 * n,)), pltpu.SemaphoreType.DMA((3 * n,)), pltpu.SemaphoreType.DMA((n,))],
        name="gather_weights",
    )(*shards)


def _pair_swap(grads):
    n = len(grads)

    def kern(*refs):
        ins, outs, (send_sems, recv_sems) = refs[:n], refs[n:2 * n], refs[2 * n:]
        x, y, c = _place()
        started = []
        for w in range(n):
            half = ins[w].shape[1] // 2
            cp = pltpu.make_async_remote_copy(
                src_ref=ins[w].at[:, pl.ds((1 - c) * half, half), :], dst_ref=outs[w], send_sem=send_sems.at[w],
                recv_sem=recv_sems.at[w], device_id=(x, y, 1 - c), device_id_type=MESH)
            cp.start()
            started.append(cp)
        for cp in started:
            cp.wait()

    return pl.pallas_call(
        kern, in_specs=[ANY] * n, out_specs=[ANY] * n,
        out_shape=[jax.ShapeDtypeStruct((g.shape[0], g.shape[1] // 2, g.shape[2]), g.dtype) for g in grads],
        scratch_shapes=[pltpu.SemaphoreType.DMA((n,)), pltpu.SemaphoreType.DMA((n,))], name="pair_swap",
    )(*grads)


def _pair_add(g, got, core):
    _, rows, cols = g.shape
    half = rows // 2
    tr = min(256, half)
    nt = half // tr

    def kern(c_ref, g_ref, o_ref, out_ref):
        out_ref[...] = g_ref[...] + o_ref[...]

    return pl.pallas_call(
        kern, grid_spec=pltpu.PrefetchScalarGridSpec(
            num_scalar_prefetch=1, grid=(N_CHIPS, nt),
            in_specs=[pl.BlockSpec((None, tr, cols), lambda j, i, c_ref: (j, c_ref[0] * nt + i, 0)),
                      pl.BlockSpec((None, tr, cols), lambda j, i, c_ref: (j, i, 0))],
            out_specs=pl.BlockSpec((None, tr, cols), lambda j, i, c_ref: (j, i, 0))),
        out_shape=jax.ShapeDtypeStruct((N_CHIPS, half, cols), F32),
        compiler_params=_cp(("parallel", "parallel")), name="pair_add",
    )(core, g, got)


def _chip_exchange(parts):
    n = len(parts)

    def kern(*refs):
        ins, outs, (send_sems, recv_sems) = refs[:n], refs[n:2 * n], refs[2 * n:]
        x, y, c = _place()
        started = []
        for w in range(n):
            for k, chip in enumerate(_other_chips(x, y)):
                cp = pltpu.make_async_remote_copy(
                    src_ref=ins[w].at[2 * chip[0] + chip[1]], dst_ref=outs[w].at[k], send_sem=send_sems.at[3 * w + k],
                    recv_sem=recv_sems.at[3 * w + k], device_id=(*chip, c), device_id_type=MESH)
                cp.start()
                started.append(cp)
        for cp in started:
            cp.wait()

    return pl.pallas_call(
        kern, in_specs=[ANY] * n, out_specs=[ANY] * n,
        out_shape=[jax.ShapeDtypeStruct((3,) + p.shape[1:], p.dtype) for p in parts],
        scratch_shapes=[pltpu.SemaphoreType.DMA((3 * n,)), pltpu.SemaphoreType.DMA((3 * n,))], name="chip_exchange",
    )(*parts)


def _chip_add(part, got, chip):
    _, half, cols = part.shape
    tr = min(256, half)

    def kern(j_ref, p_ref, g_ref, out_ref):
        out_ref[...] = p_ref[...] + g_ref[0] + g_ref[1] + g_ref[2]

    return pl.pallas_call(
        kern, grid_spec=pltpu.PrefetchScalarGridSpec(
            num_scalar_prefetch=1, grid=(half // tr,),
            in_specs=[pl.BlockSpec((None, tr, cols), lambda i, j_ref: (j_ref[0], i, 0)),
                      pl.BlockSpec((3, tr, cols), lambda i, j_ref: (0, i, 0))],
            out_specs=pl.BlockSpec((tr, cols), lambda i, j_ref: (i, 0))),
        out_shape=jax.ShapeDtypeStruct((half, cols), F32),
        compiler_params=_cp(("parallel",)), name="chip_add",
    )(chip, part, got)


def _pair_join(halves):
    n = len(halves)

    def kern(*refs):
        ins, outs, (send_sems, recv_sems, loc_sems) = refs[:n], refs[n:2 * n], refs[2 * n:]
        x, y, c = _place()
        started = []
        for w in range(n):
            half = ins[w].shape[0]
            dst = outs[w].at[pl.ds(c * half, half), :]
            lc = pltpu.make_async_copy(ins[w], dst, loc_sems.at[w])
            lc.start()
            cp = pltpu.make_async_remote_copy(
                src_ref=ins[w], dst_ref=dst, send_sem=send_sems.at[w], recv_sem=recv_sems.at[w],
                device_id=(x, y, 1 - c), device_id_type=MESH)
            cp.start()
            started += [lc, cp]
        for cp in started:
            cp.wait()

    return pl.pallas_call(
        kern, in_specs=[ANY] * n, out_specs=[ANY] * n,
        out_shape=[jax.ShapeDtypeStruct((2 * h.shape[0], h.shape[1]), h.dtype) for h in halves],
        scratch_shapes=[pltpu.SemaphoreType.DMA((n,)), pltpu.SemaphoreType.DMA((n,)), pltpu.SemaphoreType.DMA((n,))],
        name="pair_join",
    )(*halves)


def _allreduce_small(packed):
    rows = packed.shape[0]

    def kern(in_ref, out_ref, slots, send_sems, recv_sems):
        x, y, c = _place()
        mine = 4 * x + 2 * y + c
        slots[mine] = in_ref[...]
        started = []
        for k in range(1, N_DEV):
            peer = (x ^ (k >> 2), y ^ ((k >> 1) & 1), c ^ (k & 1))
            cp = pltpu.make_async_remote_copy(
                src_ref=in_ref, dst_ref=slots.at[mine], send_sem=send_sems.at[k - 1], recv_sem=recv_sems.at[k - 1],
                device_id=peer, device_id_type=MESH)
            cp.start()
            started.append(cp)
        for cp in started:
            cp.wait()
        acc = slots[0]
        for s in range(1, N_DEV):
            acc = acc + slots[s]
        out_ref[...] = acc

    vm = pl.BlockSpec(memory_space=pltpu.VMEM)
    return pl.pallas_call(
        kern, in_specs=[vm], out_specs=vm, out_shape=jax.ShapeDtypeStruct((rows, LANES), F32),
        scratch_shapes=[pltpu.VMEM((N_DEV, rows, LANES), F32), pltpu.SemaphoreType.DMA((N_DEV - 1,)),
                        pltpu.SemaphoreType.DMA((N_DEV - 1,))],
        name="allreduce_small",
    )(packed)


SMALL = ["norm1_g", "conv_w", "conv_b", "lru_w_a", "lru_b_a", "lru_w_x", "lru_b_x", "lru_lambda", "lru_out_g", "sb_out_g",
         "norm2_g", "final_g"]
BIG = ["w_in", "w_out", "w_up", "w_down"]
WEIGHTS = ["norm1_g", "w_in", "conv_w", "conv_b", "lru_w_a", "lru_b_a", "lru_w_x", "lru_b_x", "lru_lambda", "lru_out_g",
           "sb_out_g", "w_out", "norm2_g", "w_up", "w_down", "final_g"]


def _pack(arrays):
    flat = []
    for a in arrays:
        a = a.reshape(-1).astype(F32)
        flat.append(jnp.pad(a, (0, (-a.shape[0]) % LANES)))
    v = jnp.concatenate(flat)
    v = jnp.pad(v, (0, (-v.shape[0]) % (LANES * SUBLANES)))
    return v.reshape(-1, LANES)


def _unpack(packed, shapes):
    v, out, off = packed.reshape(-1), [], 0
    for shp in shapes:
        size = math.prod(shp)
        out.append(v[off:off + size].reshape(shp))
        off += size + (-size) % LANES
    return out


def _blockdiag_pairs(w):
    w = w.reshape(4, 2, DH, DH)
    z = jnp.zeros((4, DH, DH), w.dtype)
    return jnp.concatenate([jnp.concatenate([w[:, 0], z], axis=2), jnp.concatenate([z, w[:, 1]], axis=2)], axis=1)


def _blockdiag_unpairs(wbd):
    return jnp.stack([wbd[:, :DH, :DH], wbd[:, DH:, DH:]], axis=1).reshape(8, DH, DH)


def _local_step(x2, tgt, seq, norm1_g, w_in_f, conv_w_f, conv_b, w_a, b_a, w_x, b_x, lru_lambda, lru_out_g, sb_out_g, w_out_f,
                norm2_g, w_up_f, w_down_f, final_g):
    wbd = jnp.concatenate([_blockdiag_pairs(w_a), _blockdiag_pairs(w_x)], axis=2).astype(BF16)
    ba, bx = b_a.reshape(1, LRU_WIDTH), b_x.reshape(1, LRU_WIDTH)
    gf = final_g.reshape(1, D_MODEL)

    xl, qkv, xn = _inproj(x2, norm1_g, w_in_f)
    h, y_lru = _lru_fwd(xl, conv_w_f, conv_b, wbd, ba, bx, lru_lambda, seq)
    o, tot = _attn_fwd(qkv, seq)
    h1, mix = _outproj(y_lru, o, x2, lru_out_g, sb_out_g, w_out_f)
    h2, hn, up, u2 = _mlp_fwd(h1, norm2_g, w_up_f, w_down_f)
    dh2, loss_part, d_final = _loss_head(h2, tgt, gf)

    dpre = _mlp_bwd_pre(dh2, w_down_f, up)
    g_w_down = _matmul(u2, dh2, "tn", F32, "dw_down", 1024, 1024, 512).reshape(N_CHIPS, D_FF // N_CHIPS, D_MODEL)
    g_w_up = _matmul(hn, dpre, "tn", F32, "dw_up", 1024, D_FF // N_CHIPS, 512, split_cols=True)
    dh1, d_norm2 = _proj_bwd_norm([dpre], w_up_f, h1, norm2_g, dh2, "mlp_bwd_in")
    g_w_out = _matmul(mix, dh1, "tn", F32, "dw_out", 1024, 1024, 512).reshape(N_CHIPS, D_MODEL // N_CHIPS, D_MODEL)
    dy_lru, do, d_ga, d_gb = _outproj_bwd(dh1, w_out_f, y_lru, o, lru_out_g, sb_out_g)
    dq, dk, dv = _attn_bwd(qkv, do, tot, seq)
    dxl, lru_small, d_wbd = _lru_bwd(xl, h, dy_lru, conv_w_f, conv_b, wbd, ba, bx, lru_lambda, seq)
    dx, d_norm1 = _proj_bwd_norm([dxl, dq, dk, dv], w_in_f, x2, norm1_g, dh1, "inproj_bwd")
    dproj = jnp.concatenate([dxl, dq, dk, dv], axis=1)
    g_w_in = _matmul(xn, dproj, "tn", F32, "dw_in", 1024, IN_COLS // N_CHIPS, 512, split_cols=True)
    small_parts = {
        "norm1_g": d_norm1, "conv_w": lru_small[:CONV_WIDTH], "conv_b": lru_small[4:5],
        "lru_w_a": _blockdiag_unpairs(d_wbd[:, :, :LANES]), "lru_b_a": lru_small[5:6],
        "lru_w_x": _blockdiag_unpairs(d_wbd[:, :, LANES:]), "lru_b_x": lru_small[6:7], "lru_lambda": lru_small[7:8],
        "lru_out_g": d_ga, "sb_out_g": d_gb, "norm2_g": d_norm2, "final_g": d_final,
    }
    return loss_part, dx, [g_w_in, g_w_out, g_w_up, g_w_down], small_parts


def kernel(x, norm1_g, w_in, conv_w, conv_b, lru_w_a, lru_b_a, lru_w_x, lru_b_x, lru_lambda, lru_out_g, sb_out_g, w_out, norm2_g, w_up, w_down, final_g, loss_target, m_norm1_g, m_w_in, m_conv_w, m_conv_b, m_lru_w_a, m_lru_b_a, m_lru_w_x, m_lru_b_x, m_lru_lambda, m_lru_out_g, m_sb_out_g, m_w_out, m_norm2_g, m_w_up, m_w_down, m_final_g, v_norm1_g, v_w_in, v_conv_w, v_conv_b, v_lru_w_a, v_lru_b_a, v_lru_w_x, v_lru_b_x, v_lru_lambda, v_lru_out_g, v_sb_out_g, v_w_out, v_norm2_g, v_w_up, v_w_down, v_final_g):
    given = dict(locals())
    ne, seq, _ = x.shape
    t = ne * seq
    xi, yi, ci = _place()
    chip = (2 * xi + yi).astype(jnp.int32)

    gw_in, gw_out, gw_up, gw_down, gconv = _gather_weights(
        [w_in[0].astype(BF16), w_out[0].astype(BF16), w_up[0].astype(BF16), w_down[0].astype(BF16), conv_w[0]])
    w_in_f = jnp.transpose(gw_in, (1, 0, 2)).reshape(D_MODEL, IN_COLS)
    w_up_f = jnp.transpose(gw_up, (1, 0, 2)).reshape(D_MODEL, D_FF)
    w_out_f = gw_out.reshape(D_MODEL, D_MODEL)
    w_down_f = gw_down.reshape(D_FF, D_MODEL)
    conv_w_f = jnp.transpose(gconv, (1, 0, 2)).reshape(CONV_WIDTH, LRU_WIDTH)

    loss_part, dx, big, small_parts = _local_step(
        x.reshape(t, D_MODEL), loss_target.reshape(t, D_MODEL), seq, norm1_g, w_in_f, conv_w_f, conv_b, lru_w_a[0], lru_b_a,
        lru_w_x[0], lru_b_x, lru_lambda, lru_out_g, sb_out_g, w_out_f, norm2_g, w_up_f, w_down_f, final_g)

    full_shapes = {n: ((CONV_WIDTH, LRU_WIDTH) if n == "conv_w" else given[n].shape) for n in SMALL}
    red = _allreduce_small(_pack([small_parts[n] for n in SMALL] + [loss_part]))
    red_list = _unpack(red, [full_shapes[n] for n in SMALL] + [(1, LANES)])
    grads = dict(zip(SMALL, red_list[:-1]))
    loss = red_list[-1][0, 0]
    grads["conv_w"] = lax.dynamic_slice_in_dim(grads["conv_w"], chip * (LRU_WIDTH // N_CHIPS), LRU_WIDTH // N_CHIPS,
                                               axis=1).reshape(conv_w.shape)

    core = ci.astype(jnp.int32).reshape(1)
    swapped = _pair_swap(big)
    parts = [_pair_add(g, s, core) for g, s in zip(big, swapped)]
    got = _chip_exchange(parts)
    halves = [_chip_add(p, r, chip.reshape(1)) for p, r in zip(parts, got)]
    for n, full in zip(BIG, _pair_join(halves)):
        grads[n] = full.reshape(given[n].shape)

    delta, new_m, new_v = {}, {}, {}
    for n in BIG:
        shp = given[n].shape
        d, m2, v2 = _adamw(given[n][0], grads[n][0], given["m_" + n][0], given["v_" + n][0], "adamw_" + n)
        delta[n], new_m[n], new_v[n] = d.reshape(shp), m2.reshape(shp), v2.reshape(shp)
    shapes = [given[n].shape for n in SMALL]
    d, m2, v2 = _adamw(_pack([given[n] for n in SMALL]), _pack([grads[n] for n in SMALL]),
                       _pack([given["m_" + n] for n in SMALL]), _pack([given["v_" + n] for n in SMALL]), "adamw_small")
    for n, dd, mm, vv in zip(SMALL, _unpack(d, shapes), _unpack(m2, shapes), _unpack(v2, shapes)):
        delta[n], new_m[n], new_v[n] = dd, mm, vv

    return (loss, dx.reshape(x.shape), *[grads[n] for n in WEIGHTS], *[delta[n] for n in WEIGHTS],
            *[new_m[n] for n in WEIGHTS], *[new_v[n] for n in WEIGHTS])
```

```python
import functools
import math

import jax
import jax.numpy as jnp
from jax import lax
from jax.experimental import pallas as pl
from jax.experimental.pallas import tpu as pltpu

F32, BF16 = jnp.float32, jnp.bfloat16
MESH = pl.DeviceIdType.MESH

D_MODEL = 1024
LRU_WIDTH = 512
SB_WIDTH = 512
DH = 64
IN_COLS = 2 * LRU_WIDTH + 3 * SB_WIDTH
D_FF = 4 * D_MODEL
CONV_WIDTH = 4
LRU_C = 8.0
EPS = 1e-6
N_CHIPS = 4
N_DEV = 8
LANES = 128
SUBLANES = 8
TQ = 512
TK = 128
ATT_SCALE = 1.0 / math.sqrt(DH)
VMEM_LIMIT = 52 * 1024 * 1024

ADAM_LR, ADAM_B1, ADAM_B2, ADAM_EPS, ADAM_WD, ADAM_STEP = 0.001, 0.9, 0.999, 1e-08, 0.01, 10

_GELU_K = math.sqrt(2.0 / math.pi)
_GELU_C = 0.044715


def _cp(sem):
    return pltpu.CompilerParams(dimension_semantics=sem, vmem_limit_bytes=VMEM_LIMIT)


def _dot(a, b):
    return jnp.dot(a, b, preferred_element_type=F32)


def _dot_nt(a, b):
    return lax.dot_general(a, b, (((1,), (1,)), ((), ())), preferred_element_type=F32)


def _dot_tn(a, b):
    return lax.dot_general(a, b, (((0,), (0,)), ((), ())), preferred_element_type=F32)


def _rstd(x):
    return lax.rsqrt(jnp.mean(x * x, axis=-1, keepdims=True) + EPS)


def _rms_bwd(x, g, dy):
    r = _rstd(x)
    gd = g * dy
    dx = r * gd - x * (r * r * r) * jnp.mean(x * gd, axis=-1, keepdims=True)
    return dx, jnp.sum(dy * x * r, axis=0, keepdims=True)


def _sigmoid(x):
    return 1.0 / (1.0 + jnp.exp(-x))


def _softplus(x):
    return jnp.maximum(x, 0.0) + jnp.log(1.0 + jnp.exp(-jnp.abs(x)))


def _neg_expm1(x):
    series = -x * (1.0 + x * (0.5 + x * (1.0 / 6.0 + x * (1.0 / 24.0))))
    return jnp.where(x > -0.01, series, 1.0 - jnp.exp(x))


def _gelu(g):
    t = jnp.tanh(_GELU_K * (g + _GELU_C * g * g * g))
    return 0.5 * g * (1.0 + t), t


def _gelu_grad(g, t):
    return 0.5 * (1.0 + t) + 0.5 * g * (1.0 - t * t) * _GELU_K * (1.0 + 3.0 * _GELU_C * g * g)


def _rows(shape):
    return lax.broadcasted_iota(jnp.int32, shape, 0)


def _shift_down(x, s, fill):
    return jnp.where(_rows(x.shape) >= s, pltpu.roll(x, s, 0), fill)


def _shift_up(x, s, fill):
    n = x.shape[0]
    return jnp.where(_rows(x.shape) < n - s, pltpu.roll(x, n - s, 0), fill)


def _row_of(x, idx):
    return jnp.sum(jnp.where(_rows(x.shape) == idx, x, 0.0), axis=0, keepdims=True)


def _matmul(a, b, dims, out_dtype, name, tm, tn, tk, split_cols=False):
    if dims == "nn":
        (m, kk), n, dot = a.shape, b.shape[1], _dot
    elif dims == "nt":
        (m, kk), n, dot = a.shape, b.shape[0], _dot_nt
    else:
        (kk, m), n, dot = a.shape, b.shape[1], _dot_tn
    tm, tn, tk = min(tm, m), min(tn, n), min(tk, kk)
    assert m % tm == 0 and n % tn == 0 and kk % tk == 0, (name, m, n, kk)
    nk = kk // tk

    def kern(a_ref, b_ref, o_ref, acc_ref):
        k = pl.program_id(2)

        @pl.when(k == 0)
        def _():
            acc_ref[...] = jnp.zeros_like(acc_ref)

        acc_ref[...] += dot(a_ref[...].astype(BF16), b_ref[...].astype(BF16))

        @pl.when(k == nk - 1)
        def _():
            o_ref[...] = acc_ref[...].astype(o_ref.dtype)

    if split_cols:
        out_shape = jax.ShapeDtypeStruct((n // tn, m, tn), out_dtype)
        o_spec = pl.BlockSpec((None, tm, tn), lambda i, j, k: (j, i, 0))
    else:
        out_shape = jax.ShapeDtypeStruct((m, n), out_dtype)
        o_spec = pl.BlockSpec((tm, tn), lambda i, j, k: (i, j))
    if dims == "nn":
        a_spec = pl.BlockSpec((tm, tk), lambda i, j, k: (i, k))
        b_spec = pl.BlockSpec((tk, tn), lambda i, j, k: (k, j))
    elif dims == "nt":
        a_spec = pl.BlockSpec((tm, tk), lambda i, j, k: (i, k))
        b_spec = pl.BlockSpec((tn, tk), lambda i, j, k: (j, k))
    else:
        a_spec = pl.BlockSpec((tk, tm), lambda i, j, k: (k, i))
        b_spec = pl.BlockSpec((tk, tn), lambda i, j, k: (k, j))
    return pl.pallas_call(
        kern, grid=(m // tm, n // tn, nk), in_specs=[a_spec, b_spec], out_specs=o_spec, out_shape=out_shape,
        scratch_shapes=[pltpu.VMEM((tm, tn), F32)], compiler_params=_cp(("parallel", "parallel", "arbitrary")), name=name,
    )(a, b)


ANY = pl.BlockSpec(memory_space=pl.ANY)


def _place():
    return lax.axis_index("x"), lax.axis_index("y"), lax.axis_index("c")


def _other_chips(x, y):
    return [(1 - x, y), (x, 1 - y), (1 - x, 1 - y)]


def _gather_copies(ins, outs, send_sems, recv_sems, loc_sems):
    x, y, c = _place()
    mine = 2 * x + y
    copies = []
    for w in range(len(ins)):
        copies.append(pltpu.make_async_copy(ins[w], outs[w].at[mine], loc_sems.at[w]))
        for k, chip in enumerate(_other_chips(x, y)):
            copies.append(pltpu.make_async_remote_copy(
                src_ref=ins[w], dst_ref=outs[w].at[mine], send_sem=send_sems.at[3 * w + k],
                recv_sem=recv_sems.at[3 * w + k], device_id=(*chip, c), device_id_type=MESH))
    return copies


def _gather_shapes(shards):
    return ([jax.ShapeDtypeStruct((N_CHIPS,) + s.shape, s.dtype) for s in shards],
            [pltpu.SemaphoreType.DMA((3 * len(shards),)), pltpu.SemaphoreType.DMA((3 * len(shards),)),
             pltpu.SemaphoreType.DMA((len(shards),))])


def _exchange_copies(ins, outs, send_sems, recv_sems):
    x, y, c = _place()
    copies = []
    for w in range(len(ins)):
        for k, chip in enumerate(_other_chips(x, y)):
            copies.append(pltpu.make_async_remote_copy(
                src_ref=ins[w].at[2 * chip[0] + chip[1]], dst_ref=outs[w].at[k], send_sem=send_sems.at[3 * w + k],
                recv_sem=recv_sems.at[3 * w + k], device_id=(*chip, c), device_id_type=MESH))
    return copies


def _exchange_shapes(parts):
    return ([jax.ShapeDtypeStruct((3,) + p.shape[1:], p.dtype) for p in parts],
            [pltpu.SemaphoreType.DMA((3 * len(parts),)), pltpu.SemaphoreType.DMA((3 * len(parts),))])


def _grid_ends(grid):
    i, j = pl.program_id(0), pl.program_id(1)
    return (i == 0) & (j == 0), (i == grid[0] - 1) & (j == grid[1] - 1)


def _inproj(x, g1, w_in):
    t = x.shape[0]
    tm = min(512, t)

    def kern(x_ref, g_ref, w_ref, xl_ref, qkv_ref, xn_ref):
        xv = x_ref[...]
        xn = (xv * _rstd(xv) * g_ref[...]).astype(BF16)
        xn_ref[...] = xn
        xl_ref[...] = _dot(xn, w_ref[:, : 2 * LRU_WIDTH])
        qkv_ref[...] = _dot(xn, w_ref[:, 2 * LRU_WIDTH:]).astype(BF16)

    row = lambda c: pl.BlockSpec((tm, c), lambda i: (i, 0))
    return pl.pallas_call(
        kern, grid=(t // tm,),
        in_specs=[row(D_MODEL), pl.BlockSpec((1, D_MODEL), lambda i: (0, 0)), pl.BlockSpec((D_MODEL, IN_COLS), lambda i: (0, 0))],
        out_specs=[row(2 * LRU_WIDTH), row(3 * SB_WIDTH), row(D_MODEL)],
        out_shape=[jax.ShapeDtypeStruct((t, 2 * LRU_WIDTH), F32), jax.ShapeDtypeStruct((t, 3 * SB_WIDTH), BF16),
                   jax.ShapeDtypeStruct((t, D_MODEL), BF16)],
        compiler_params=_cp(("parallel",)), name="inproj",
    )(x, g1, w_in)


def _conv_taps(hist, u):
    cat = jnp.concatenate([hist, u], axis=0)
    return [pltpu.roll(cat, CONV_WIDTH - 1 - k, 0)[SUBLANES:] for k in range(CONV_WIDTH - 1)] + [u]


def _lru_gates(c, wbd_ref, ba, bx, sp):
    gas, gxs = [], []
    for p in range(LRU_WIDTH // LANES):
        gax = _dot(c[:, LANES * p: LANES * (p + 1)].astype(BF16), wbd_ref[p])
        gas.append(gax[:, :LANES])
        gxs.append(gax[:, LANES:])
    r = _sigmoid(jnp.concatenate(gas, axis=1) + ba)
    i = _sigmoid(jnp.concatenate(gxs, axis=1) + bx)
    la = (-LRU_C) * r * sp
    a = jnp.exp(la)
    mult = jnp.sqrt(_neg_expm1(2.0 * la))
    return r, i, a, mult


def _scan_fwd(a, b):
    s = 1
    while s < a.shape[0]:
        b = b + a * _shift_down(b, s, 0.0)
        a = a * _shift_down(a, s, 1.0)
        s *= 2
    return a, b


def _scan_rev(a, b):
    s = 1
    while s < a.shape[0]:
        b = b + a * _shift_up(b, s, 0.0)
        a = a * _shift_up(a, s, 1.0)
        s *= 2
    return a, b


def _lru_param_specs(grid_rank):
    z2 = (lambda e, c: (0, 0)) if grid_rank == 2 else None
    return [
        pl.BlockSpec((CONV_WIDTH, LRU_WIDTH), z2), pl.BlockSpec((1, LRU_WIDTH), z2),
        pl.BlockSpec((LRU_WIDTH // LANES, LANES, 2 * LANES), lambda e, c: (0, 0, 0)),
        pl.BlockSpec((1, LRU_WIDTH), z2), pl.BlockSpec((1, LRU_WIDTH), z2), pl.BlockSpec((1, LRU_WIDTH), z2),
    ]


def _lru_fwd(xl, conv_w, conv_b, wbd, ba, bx, lam, seq):
    t = xl.shape[0]
    tc = min(512, seq)
    nc = seq // tc

    def kern(u_ref, g_ref, cw_ref, cb_ref, wbd_ref, ba_ref, bx_ref, lam_ref, h_ref, y_ref, hist_ref, hcar_ref):
        @pl.when(pl.program_id(1) == 0)
        def _():
            hist_ref[...] = jnp.zeros_like(hist_ref)
            hcar_ref[...] = jnp.zeros_like(hcar_ref)

        u = u_ref[...]
        taps = _conv_taps(hist_ref[...], u)
        hist_ref[...] = u_ref[tc - SUBLANES:, :]
        c = cb_ref[...]
        for k in range(CONV_WIDTH):
            c = c + taps[k] * cw_ref[k:k + 1, :]
        sp = _softplus(-lam_ref[...])
        _, i, a, mult = _lru_gates(c, wbd_ref, ba_ref[...], bx_ref[...], sp)
        aa, bb = _scan_fwd(a, mult * i * c)
        h = bb + aa * hcar_ref[0:1, :]
        h_ref[...] = h
        hcar_ref[0:1, :] = h_ref[tc - 1:tc, :]
        y_ref[...] = h * _gelu(g_ref[...])[0]

    chunk = lambda col: pl.BlockSpec((tc, LRU_WIDTH), lambda e, c: (e * nc + c, col))
    out = jax.ShapeDtypeStruct((t, LRU_WIDTH), F32)
    return pl.pallas_call(
        kern, grid=(t // seq, nc), in_specs=[chunk(0), chunk(1)] + _lru_param_specs(2),
        out_specs=[chunk(0), chunk(0)], out_shape=[out, out],
        scratch_shapes=[pltpu.VMEM((SUBLANES, LRU_WIDTH), F32), pltpu.VMEM((SUBLANES, LRU_WIDTH), F32)],
        compiler_params=_cp(("arbitrary", "arbitrary")), name="lru_fwd",
    )(xl, xl, conv_w, conv_b, wbd, ba, bx, lam)


def _att_consts():
    row = lax.broadcasted_iota(jnp.int32, (TQ, 2 * TK), 0)
    key = lax.broadcasted_iota(jnp.int32, (TQ, 2 * TK), 1) & (TK - 1)
    return [(TK * jj + key) < row for jj in range(TQ // TK)]


def _sum_matrix(kind):
    j = lax.broadcasted_iota(jnp.int32, (2 * TK, 2 * TK), 0) & (TK - 1)
    s = lax.broadcasted_iota(jnp.int32, (2 * TK, 2 * TK), 1)
    pick = {"after": j > s, "upto": j <= s, "before": j < s}[kind]
    return jnp.where((s >= TK) | pick, 1.0, 0.0).astype(BF16)


def _hi_lo(x):
    hi = x.astype(BF16)
    return hi, (x - hi.astype(F32)).astype(BF16)


def _pair_sums(x, m):
    hi, lo = _hi_lo(x)
    out = []
    for hd in range(2):
        cols = slice(hd * TK, (hd + 1) * TK)
        out.append(_dot(jnp.concatenate([hi[:, cols], lo[:, cols]], axis=1), m))
    return [o[:, :TK] for o in out], [o[:, TK:] for o in out]


def _att_logits(qb, kbd):
    z = _dot(qb, kbd)
    lg = jnp.log(1.0 + jnp.exp(-jnp.abs(z)))
    lb = jnp.minimum(z, 0.0) - lg
    return lb, lb - z


def _head_diag(x, rows_first):
    n = x.shape[0] if rows_first else x.shape[1]
    idx = lax.broadcasted_iota(jnp.int32, x.shape, 0 if rows_first else 1)
    return jnp.where(idx < n // 2, x, 0), jnp.where(idx >= n // 2, x, 0)


def _scaled_q(q_ref, q0):
    return (q_ref[pl.ds(q0, TQ), :].astype(F32) * ATT_SCALE).astype(BF16)


def _qkv_specs(seq):
    n = SB_WIDTH // LANES
    return [pl.BlockSpec((seq, LANES), lambda e, p, off=off: (e, off * n + p)) for off in range(3)]


def _attn_fwd(qkv, seq, shards=()):
    t = qkv.shape[0]
    ne, nq, nk = t // seq, seq // TQ, seq // TK
    grid = (ne, SB_WIDTH // LANES)
    n = len(shards)

    def kern(q_ref, k_ref, v_ref, *rest):
        (o_ref, tot_ref), (kbd_scr, vbd_scr) = rest[n:n + 2], rest[2 * n + 2:2 * n + 4]
        first, last = _grid_ends(grid)
        if n:
            @pl.when(first)
            def _():
                for cp in _gather_copies(rest[:n], rest[n + 2:2 * n + 2], *rest[2 * n + 4:]):
                    cp.start()

        causal = _att_consts()
        after = _sum_matrix("after")

        def prep(j, _):
            k0 = pl.multiple_of(j * TK, TK)
            top, bot = _head_diag(k_ref[pl.ds(k0, TK), :].astype(F32).T, True)
            kbd_scr[j] = jnp.concatenate([top, bot], axis=1).astype(BF16)
            left, right = _head_diag(v_ref[pl.ds(k0, TK), :], False)
            vbd_scr[j] = jnp.concatenate([left, right], axis=0)
            return 0

        lax.fori_loop(0, nk, prep, 0)

        def block(j, qb, st, mask):
            c0, c1, oacc = st
            lb, l1 = _att_logits(qb, kbd_scr[j])
            if mask is not None:
                l1 = jnp.where(mask, l1, 0.0)
            (s0, s1), (r0, r1) = _pair_sums(l1, after)
            att = jnp.exp(lb + jnp.concatenate([s0 + c0, s1 + c1], axis=1))
            if mask is not None:
                att = jnp.where(mask, att, 0.0)
            return c0 + r0, c1 + r1, oacc + _dot(att.astype(BF16), vbd_scr[j])

        def qloop(qi, _):
            q0 = pl.multiple_of(qi * TQ, TQ)
            qb = _scaled_q(q_ref, q0)
            zero = jnp.zeros((TQ, TK), F32)
            st = (zero, zero, jnp.zeros((TQ, LANES), F32))
            for jj in reversed(range(TQ // TK)):
                st = block((TQ // TK) * qi + jj, qb, st, causal[jj])

            def kloop(it, st):
                j = (TQ // TK) * qi - 1 - 2 * it
                return block(j - 1, qb, block(j, qb, st, None), None)

            c0, c1, oacc = lax.fori_loop(0, (TQ // TK // 2) * qi, kloop, st)
            o_ref[pl.ds(q0, TQ), :] = oacc
            tot_ref[pl.ds(q0, TQ), :] = jnp.concatenate([c0, c1], axis=1)
            return 0

        lax.fori_loop(0, nq, qloop, 0)

        if n:
            @pl.when(last)
            def _():
                for cp in _gather_copies(rest[:n], rest[n + 2:2 * n + 2], *rest[2 * n + 4:]):
                    cp.wait()

    g_shapes, g_sems = _gather_shapes(shards) if n else ([], [])
    out = pl.pallas_call(
        kern, grid=grid, in_specs=_qkv_specs(seq) + [ANY] * n,
        out_specs=[pl.BlockSpec((seq, LANES), lambda e, p: (e, p)), pl.BlockSpec((seq, 2 * TK), lambda e, p: (e, p))] + [ANY] * n,
        out_shape=[jax.ShapeDtypeStruct((t, SB_WIDTH), F32), jax.ShapeDtypeStruct((t, 2 * TK * SB_WIDTH // LANES), F32)] + g_shapes,
        scratch_shapes=[pltpu.VMEM((nk, LANES, 2 * TK), BF16), pltpu.VMEM((nk, 2 * TK, LANES), BF16)] + g_sems,
        compiler_params=_cp(("arbitrary", "arbitrary")), name="attn_fwd",
    )(qkv, qkv, qkv, *shards)
    return out[0], out[1], list(out[2:])


def _outproj(y_lru, o, x, ga, gb, w_out):
    t = x.shape[0]
    tm = min(512, t)

    def kern(y_ref, o_ref, x_ref, ga_ref, gb_ref, w_ref, h1_ref, mix_ref):
        yv, ov = y_ref[...], o_ref[...]
        mix = jnp.concatenate([yv * _rstd(yv) * ga_ref[...], ov * _rstd(ov) * gb_ref[...]], axis=1).astype(BF16)
        mix_ref[...] = mix
        h1_ref[...] = x_ref[...] + _dot(mix, w_ref[...])

    row = lambda c: pl.BlockSpec((tm, c), lambda i: (i, 0))
    vec = lambda c: pl.BlockSpec((1, c), lambda i: (0, 0))
    return pl.pallas_call(
        kern, grid=(t // tm,),
        in_specs=[row(LRU_WIDTH), row(SB_WIDTH), row(D_MODEL), vec(LRU_WIDTH), vec(SB_WIDTH),
                  pl.BlockSpec((D_MODEL, D_MODEL), lambda i: (0, 0))],
        out_specs=[row(D_MODEL), row(D_MODEL)],
        out_shape=[jax.ShapeDtypeStruct((t, D_MODEL), F32), jax.ShapeDtypeStruct((t, D_MODEL), BF16)],
        compiler_params=_cp(("parallel",)), name="outproj",
    )(y_lru, o, x, ga, gb, w_out)


def _mlp_fwd(h1, g2, w_up, w_down):
    t = h1.shape[0]
    tm, tf = min(512, t), 1024
    nf = D_FF // tf

    def kern(h1_ref, g_ref, wu_ref, wd_ref, h2_ref, hn_ref, up_ref, u2_ref, acc_ref):
        f = pl.program_id(1)

        @pl.when(f == 0)
        def _():
            hv = h1_ref[...]
            hn_ref[...] = (hv * _rstd(hv) * g_ref[...]).astype(BF16)
            acc_ref[...] = hv

        up = jnp.maximum(_dot(hn_ref[...], wu_ref[...]), 0.0)
        u2 = (up * up).astype(BF16)
        up_ref[...] = up.astype(BF16)
        u2_ref[...] = u2
        acc_ref[...] += _dot(u2, wd_ref[...])

        @pl.when(f == nf - 1)
        def _():
            h2_ref[...] = acc_ref[...]

    return pl.pallas_call(
        kern, grid=(t // tm, nf),
        in_specs=[pl.BlockSpec((tm, D_MODEL), lambda i, f: (i, 0)), pl.BlockSpec((1, D_MODEL), lambda i, f: (0, 0)),
                  pl.BlockSpec((D_MODEL, tf), lambda i, f: (0, f)), pl.BlockSpec((tf, D_MODEL), lambda i, f: (f, 0))],
        out_specs=[pl.BlockSpec((tm, D_MODEL), lambda i, f: (i, 0)), pl.BlockSpec((tm, D_MODEL), lambda i, f: (i, 0)),
                   pl.BlockSpec((tm, tf), lambda i, f: (i, f)), pl.BlockSpec((tm, tf), lambda i, f: (i, f))],
        out_shape=[jax.ShapeDtypeStruct((t, D_MODEL), F32), jax.ShapeDtypeStruct((t, D_MODEL), BF16),
                   jax.ShapeDtypeStruct((t, D_FF), BF16), jax.ShapeDtypeStruct((t, D_FF), BF16)],
        scratch_shapes=[pltpu.VMEM((tm, D_MODEL), F32)],
        compiler_params=_cp(("parallel", "arbitrary")), name="mlp_fwd",
    )(h1, g2, w_up, w_down)


def _loss_head(h2, target, gf):
    t = h2.shape[0]
    tm = min(512, t)

    def kern(h_ref, t_ref, g_ref, dh_ref, loss_ref, dg_ref):
        @pl.when(pl.program_id(0) == 0)
        def _():
            loss_ref[...] = jnp.zeros_like(loss_ref)
            dg_ref[...] = jnp.zeros_like(dg_ref)

        hv, g = h_ref[...], g_ref[...]
        err = hv * _rstd(hv) * g - t_ref[...]
        lane = lax.broadcasted_iota(jnp.int32, (1, LANES), 1)
        loss_ref[...] += jnp.where(lane == 0, 0.5 * jnp.sum(err * err) / D_MODEL, 0.0)
        dx, dg = _rms_bwd(hv, g, err * (1.0 / D_MODEL))
        dh_ref[...] = dx
        dg_ref[...] += dg

    row = pl.BlockSpec((tm, D_MODEL), lambda i: (i, 0))
    vec = pl.BlockSpec((1, D_MODEL), lambda i: (0, 0))
    return pl.pallas_call(
        kern, grid=(t // tm,), in_specs=[row, row, vec],
        out_specs=[row, pl.BlockSpec((1, LANES), lambda i: (0, 0)), vec],
        out_shape=[jax.ShapeDtypeStruct((t, D_MODEL), F32), jax.ShapeDtypeStruct((1, LANES), F32),
                   jax.ShapeDtypeStruct((1, D_MODEL), F32)],
        compiler_params=_cp(("arbitrary",)), name="loss_head",
    )(h2, target, gf)


def _mlp_bwd_pre(dh2, w_down, up):
    t = dh2.shape[0]
    tm, tf = min(512, t), 1024

    def kern(d_ref, w_ref, up_ref, o_ref):
        du2 = _dot_nt(d_ref[...].astype(BF16), w_ref[...])
        o_ref[...] = (du2 * (2.0 * up_ref[...].astype(F32))).astype(BF16)

    return pl.pallas_call(
        kern, grid=(t // tm, D_FF // tf),
        in_specs=[pl.BlockSpec((tm, D_MODEL), lambda i, f: (i, 0)), pl.BlockSpec((tf, D_MODEL), lambda i, f: (f, 0)),
                  pl.BlockSpec((tm, tf), lambda i, f: (i, f))],
        out_specs=pl.BlockSpec((tm, tf), lambda i, f: (i, f)), out_shape=jax.ShapeDtypeStruct((t, D_FF), BF16),
        compiler_params=_cp(("parallel", "parallel")), name="mlp_bwd_pre",
    )(dh2, w_down, up)


def _proj_bwd_norm(dys, w, x, g, resid, name):
    t = x.shape[0]
    tm = min(512, t)
    widths = [dy.shape[1] for dy in dys]
    n = len(dys)

    def kern(*refs):
        dy_refs, (w_ref, x_ref, g_ref, r_ref, dx_ref, dg_ref) = refs[:n], refs[n:]

        @pl.when(pl.program_id(0) == 0)
        def _():
            dg_ref[...] = jnp.zeros_like(dg_ref)

        off, dxn = 0, None
        for dy_ref, wd in zip(dy_refs, widths):
            part = _dot_nt(dy_ref[...], w_ref[:, off:off + wd])
            dxn = part if dxn is None else dxn + part
            off += wd
        dx, dg = _rms_bwd(x_ref[...], g_ref[...], dxn)
        dx_ref[...] = r_ref[...] + dx
        dg_ref[...] += dg

    row = lambda c: pl.BlockSpec((tm, c), lambda i: (i, 0))
    vec = pl.BlockSpec((1, D_MODEL), lambda i: (0, 0))
    return pl.pallas_call(
        kern, grid=(t // tm,),
        in_specs=[row(wd) for wd in widths] + [pl.BlockSpec(w.shape, lambda i: (0, 0)), row(D_MODEL), vec, row(D_MODEL)],
        out_specs=[row(D_MODEL), vec],
        out_shape=[jax.ShapeDtypeStruct((t, D_MODEL), F32), jax.ShapeDtypeStruct((1, D_MODEL), F32)],
        compiler_params=_cp(("arbitrary",)), name=name,
    )(*dys, w, x, g, resid)


def _outproj_bwd(dh1, w_out, y_lru, o, ga, gb):
    t = dh1.shape[0]
    tm = min(512, t)

    def kern(d_ref, w_ref, y_ref, o_ref, ga_ref, gb_ref, dy_ref, do_ref, dga_ref, dgb_ref):
        @pl.when(pl.program_id(0) == 0)
        def _():
            dga_ref[...] = jnp.zeros_like(dga_ref)
            dgb_ref[...] = jnp.zeros_like(dgb_ref)

        dmix = _dot_nt(d_ref[...].astype(BF16), w_ref[...])
        dy, dga = _rms_bwd(y_ref[...], ga_ref[...], dmix[:, :LRU_WIDTH])
        do, dgb = _rms_bwd(o_ref[...], gb_ref[...], dmix[:, LRU_WIDTH:])
        dy_ref[...] = dy
        do_ref[...] = do
        dga_ref[...] += dga
        dgb_ref[...] += dgb

    row = lambda c: pl.BlockSpec((tm, c), lambda i: (i, 0))
    vec = pl.BlockSpec((1, LRU_WIDTH), lambda i: (0, 0))
    half = jax.ShapeDtypeStruct((t, LRU_WIDTH), F32)
    gsum = jax.ShapeDtypeStruct((1, LRU_WIDTH), F32)
    return pl.pallas_call(
        kern, grid=(t // tm,),
        in_specs=[row(D_MODEL), pl.BlockSpec((D_MODEL, D_MODEL), lambda i: (0, 0)), row(LRU_WIDTH), row(SB_WIDTH), vec, vec],
        out_specs=[row(LRU_WIDTH), row(SB_WIDTH), vec, vec], out_shape=[half, half, gsum, gsum],
        compiler_params=_cp(("arbitrary",)), name="outproj_bwd",
    )(dh1, w_out, y_lru, o, ga, gb)


def _attn_bwd(qkv, do, tot, seq, parts=()):
    t = qkv.shape[0]
    ne, nq, nk = t // seq, seq // TQ, seq // TK
    grid = (ne, SB_WIDTH // LANES)
    n = len(parts)

    def kern(q_ref, k_ref, v_ref, do_ref, tot_ref, *rest):
        dq_ref, dk_ref, dv_ref = rest[n:n + 3]
        kbd_scr, vtbd_scr, kbd2_scr, dkt_scr, dvt_scr = rest[2 * n + 3:2 * n + 8]
        first, last = _grid_ends(grid)
        if n:
            @pl.when(first)
            def _():
                for cp in _exchange_copies(rest[:n], rest[n + 3:2 * n + 3], *rest[2 * n + 8:]):
                    cp.start()

        causal = _att_consts()
        upto, before = _sum_matrix("upto"), _sum_matrix("before")

        def prep(j, _):
            k0 = pl.multiple_of(j * TK, TK)
            kb = k_ref[pl.ds(k0, TK), :]
            top, bot = _head_diag(kb.astype(F32).T, True)
            kbd_scr[j] = jnp.concatenate([top, bot], axis=1).astype(BF16)
            top, bot = _head_diag(v_ref[pl.ds(k0, TK), :].astype(F32).T, True)
            vtbd_scr[j] = jnp.concatenate([top, bot], axis=1).astype(BF16)
            left, right = _head_diag(kb, False)
            kbd2_scr[j] = jnp.concatenate([left, right], axis=0)
            dkt_scr[j] = jnp.zeros((LANES, 2 * TK), F32)
            dvt_scr[j] = jnp.zeros((LANES, 2 * TK), F32)
            return 0

        lax.fori_loop(0, nk, prep, 0)

        def block(j, qb, qt, dob, dot_, totb, st, mask):
            f0, f1, p0, p1, dqacc = st
            lb, l1 = _att_logits(qb, kbd_scr[j])
            if mask is not None:
                l1 = jnp.where(mask, l1, 0.0)
            (s0, s1), (r0, r1) = _pair_sums(l1, upto)
            att = jnp.exp(lb + (totb - jnp.concatenate([s0 + f0, s1 + f1], axis=1)))
            if mask is not None:
                att = jnp.where(mask, att, 0.0)
            pw = att * _dot(dob, vtbd_scr[j])
            (e0, e1), (t0, t1) = _pair_sums(pw, before)
            dz = pw - jnp.exp(lb) * (pw + jnp.concatenate([e0 + p0, e1 + p1], axis=1))
            if mask is not None:
                dz = jnp.where(mask, dz, 0.0)
            dzb = dz.astype(BF16)
            dkt_scr[j] += _dot(qt, dzb)
            dvt_scr[j] += _dot(dot_, att.astype(BF16))
            return f0 + r0, f1 + r1, p0 + t0, p1 + t1, dqacc + _dot(dzb, kbd2_scr[j])

        def qloop(qi, _):
            q0 = pl.multiple_of(qi * TQ, TQ)
            qb = _scaled_q(q_ref, q0)
            qt = qb.astype(F32).T.astype(BF16)
            do32 = do_ref[pl.ds(q0, TQ), :]
            dob, dot_ = do32.astype(BF16), do32.T.astype(BF16)
            totb = tot_ref[pl.ds(q0, TQ), :]
            zero = jnp.zeros((TQ, TK), F32)
            st = (zero, zero, zero, zero, jnp.zeros((TQ, LANES), F32))

            def kloop(it, st):
                j = 2 * it
                return block(j + 1, qb, qt, dob, dot_, totb, block(j, qb, qt, dob, dot_, totb, st, None), None)

            st = lax.fori_loop(0, (TQ // TK // 2) * qi, kloop, st)
            for jj in range(TQ // TK):
                st = block((TQ // TK) * qi + jj, qb, qt, dob, dot_, totb, st, causal[jj])
            dq_ref[pl.ds(q0, TQ), :] = (st[4] * ATT_SCALE).astype(BF16)
            return 0

        lax.fori_loop(0, nq, qloop, 0)

        def finish(j, _):
            k0 = pl.multiple_of(j * TK, TK)
            head0 = lax.broadcasted_iota(jnp.int32, (LANES, TK), 0) < DH
            for src, dst in ((dkt_scr, dk_ref), (dvt_scr, dv_ref)):
                acc = src[j]
                dst[pl.ds(k0, TK), :] = jnp.where(head0, acc[:, :TK], acc[:, TK:]).T.astype(BF16)
            return 0

        lax.fori_loop(0, nk, finish, 0)

        if n:
            @pl.when(last)
            def _():
                for cp in _exchange_copies(rest[:n], rest[n + 3:2 * n + 3], *rest[2 * n + 8:]):
                    cp.wait()

    blk = pl.BlockSpec((seq, LANES), lambda e, p: (e, p))
    grad = jax.ShapeDtypeStruct((t, SB_WIDTH), BF16)
    x_shapes, x_sems = _exchange_shapes(parts) if n else ([], [])
    out = pl.pallas_call(
        kern, grid=grid,
        in_specs=_qkv_specs(seq) + [blk, pl.BlockSpec((seq, 2 * TK), lambda e, p: (e, p))] + [ANY] * n,
        out_specs=[blk, blk, blk] + [ANY] * n, out_shape=[grad, grad, grad] + x_shapes,
        scratch_shapes=[pltpu.VMEM((nk, LANES, 2 * TK), BF16), pltpu.VMEM((nk, LANES, 2 * TK), BF16),
                        pltpu.VMEM((nk, 2 * TK, LANES), BF16), pltpu.VMEM((nk, LANES, 2 * TK), F32),
                        pltpu.VMEM((nk, LANES, 2 * TK), F32)] + x_sems,
        compiler_params=_cp(("arbitrary", "arbitrary")), name="attn_bwd",
    )(qkv, qkv, qkv, do, tot, *parts)
    return out[0], out[1], out[2], list(out[3:])


def _lru_bwd(xl, h, dy, conv_w, conv_b, wbd, ba, bx, lam, seq):
    t = xl.shape[0]
    tc = min(512, seq)
    nc = seq // tc
    nb = tc // SUBLANES

    def kern(u_ref, g_ref, up_ref, h_ref, hp_ref, dy_ref, cw_ref, cb_ref, wbd_ref, ba_ref, bx_ref, lam_ref,
             dxl_ref, small_ref, dwbd_ref, lnext_ref, anext_ref, dcnext_ref):
        e, ci = pl.program_id(0), pl.program_id(1)
        first = ci == nc - 1

        @pl.when((e == 0) & (ci == 0))
        def _():
            small_ref[...] = jnp.zeros_like(small_ref)
            dwbd_ref[...] = jnp.zeros_like(dwbd_ref)

        @pl.when(ci == 0)
        def _():
            lnext_ref[...] = jnp.zeros_like(lnext_ref)
            anext_ref[...] = jnp.zeros_like(anext_ref)
            dcnext_ref[...] = jnp.zeros_like(dcnext_ref)

        u, g = u_ref[...], g_ref[...]
        keep = jnp.where(first, 0.0, 1.0)
        taps = _conv_taps(keep * up_ref[...], u)
        c = cb_ref[...]
        for k in range(CONV_WIDTH):
            c = c + taps[k] * cw_ref[k:k + 1, :]
        lam = lam_ref[...]
        sp = _softplus(-lam)
        r, i, a, mult = _lru_gates(c, wbd_ref, ba_ref[...], bx_ref[...], sp)
        gel, th = _gelu(g)
        dyv, hv = dy_ref[...], h_ref[...]
        dg = dyv * hv * _gelu_grad(g, th)

        aa, bb = _scan_rev(_shift_up(a, 1, anext_ref[0:1, :]), dyv * gel)
        lt = bb + aa * lnext_ref[0:1, :]
        lnext_ref[0:1, :] = _row_of(lt, 0)
        anext_ref[0:1, :] = _row_of(a, 0)

        hprev = _shift_down(hv, 1, keep * hp_ref[SUBLANES - 1:SUBLANES, :])
        da = lt * hprev
        dmult = lt * i * c
        di = lt * mult * c
        dc = lt * mult * i
        dla = da * a - dmult * (a * a) / mult
        dga = dla * ((-LRU_C) * sp) * r * (1.0 - r)
        dgx = di * i * (1.0 - i)
        small_ref[7:8, :] += jnp.sum(dla * r, axis=0, keepdims=True) * (LRU_C * _sigmoid(-lam))
        small_ref[5:6, :] += jnp.sum(dga, axis=0, keepdims=True)
        small_ref[6:7, :] += jnp.sum(dgx, axis=0, keepdims=True)

        dcs = []
        for p in range(LRU_WIDTH // LANES):
            cols = slice(LANES * p, LANES * (p + 1))
            dgax = jnp.concatenate([dga[:, cols], dgx[:, cols]], axis=1).astype(BF16)
            dcs.append(_dot_nt(dgax, wbd_ref[p]))
            dwbd_ref[p] += _dot_tn(c[:, cols].astype(BF16), dgax)
        dc = dc + jnp.concatenate(dcs, axis=1)
        small_ref[4:5, :] += jnp.sum(dc, axis=0, keepdims=True)

        catd = jnp.concatenate([dc, dcnext_ref[...]], axis=0)
        du = dc * cw_ref[CONV_WIDTH - 1:CONV_WIDTH, :]
        for j in range(1, CONV_WIDTH):
            du = du + pltpu.roll(catd, tc + SUBLANES - j, 0)[:tc] * cw_ref[CONV_WIDTH - 1 - j:CONV_WIDTH - j, :]
        dcnext_ref[...] = dc[:SUBLANES]
        for k in range(CONV_WIDTH):
            small_ref[k:k + 1, :] += jnp.sum(dc * taps[k], axis=0, keepdims=True)
        dxl_ref[:, :LRU_WIDTH] = du.astype(BF16)
        dxl_ref[:, LRU_WIDTH:] = dg.astype(BF16)

    rev = lambda e, c: e * nc + (nc - 1 - c)
    chunk = lambda col: pl.BlockSpec((tc, LRU_WIDTH), lambda e, c: (rev(e, c), col))
    prev8 = pl.BlockSpec((SUBLANES, LRU_WIDTH), lambda e, c: (jnp.maximum(rev(e, c) * nb - 1, 0), 0))
    return pl.pallas_call(
        kern, grid=(t // seq, nc),
        in_specs=[chunk(0), chunk(1), prev8, chunk(0), prev8, chunk(0)] + _lru_param_specs(2),
        out_specs=[pl.BlockSpec((tc, 2 * LRU_WIDTH), lambda e, c: (rev(e, c), 0)),
                   pl.BlockSpec((SUBLANES, LRU_WIDTH), lambda e, c: (0, 0)),
                   pl.BlockSpec((LRU_WIDTH // LANES, LANES, 2 * LANES), lambda e, c: (0, 0, 0))],
        out_shape=[jax.ShapeDtypeStruct((t, 2 * LRU_WIDTH), BF16), jax.ShapeDtypeStruct((SUBLANES, LRU_WIDTH), F32),
                   jax.ShapeDtypeStruct((LRU_WIDTH // LANES, LANES, 2 * LANES), F32)],
        scratch_shapes=[pltpu.VMEM((SUBLANES, LRU_WIDTH), F32)] * 3,
        compiler_params=_cp(("arbitrary", "arbitrary")), name="lru_bwd",
    )(xl, xl, xl, h, h, dy, conv_w, conv_b, wbd, ba, bx, lam)


def _adamw(w, g, m, v, name):
    rows, cols = w.shape
    tr = 256 if rows % 256 == 0 else rows

    def kern(w_ref, g_ref, m_ref, v_ref, d_ref, m2_ref, v2_ref):
        gv = g_ref[...]
        m2 = ADAM_B1 * m_ref[...] + (1.0 - ADAM_B1) * gv
        v2 = ADAM_B2 * v_ref[...] + (1.0 - ADAM_B2) * (gv * gv)
        m_hat = m2 / (1.0 - ADAM_B1 ** ADAM_STEP)
        v_hat = v2 / (1.0 - ADAM_B2 ** ADAM_STEP)
        d_ref[...] = -ADAM_LR * (m_hat / (jnp.sqrt(v_hat) + ADAM_EPS) + ADAM_WD * w_ref[...])
        m2_ref[...] = m2
        v2_ref[...] = v2

    blk = pl.BlockSpec((tr, cols), lambda i: (i, 0))
    out = jax.ShapeDtypeStruct((rows, cols), F32)
    return pl.pallas_call(kern, grid=(rows // tr,), in_specs=[blk] * 4, out_specs=[blk] * 3, out_shape=[out] * 3,
                          compiler_params=_cp(("parallel",)), name=name)(w, g, m, v)


def _gather_weights(shards):
    n = len(shards)

    def kern(*refs):
        copies = _gather_copies(refs[:n], refs[n:2 * n], *refs[2 * n:])
        for cp in copies:
            cp.start()
        for cp in copies:
            cp.wait()

    out_shape, sems = _gather_shapes(shards)
    return pl.pallas_call(kern, in_specs=[ANY] * n, out_specs=[ANY] * n, out_shape=out_shape, scratch_shapes=sems,
                          name="gather_weights")(*shards)


def _pair_swap(grads):
    n = len(grads)

    def kern(*refs):
        ins, outs, (send_sems, recv_sems) = refs[:n], refs[n:2 * n], refs[2 * n:]
        x, y, c = _place()
        started = []
        for w in range(n):
            half = ins[w].shape[1] // 2
            cp = pltpu.make_async_remote_copy(
                src_ref=ins[w].at[:, pl.ds((1 - c) * half, half), :], dst_ref=outs[w], send_sem=send_sems.at[w],
                recv_sem=recv_sems.at[w], device_id=(x, y, 1 - c), device_id_type=MESH)
            cp.start()
            started.append(cp)
        for cp in started:
            cp.wait()

    return pl.pallas_call(
        kern, in_specs=[ANY] * n, out_specs=[ANY] * n,
        out_shape=[jax.ShapeDtypeStruct((g.shape[0], g.shape[1] // 2, g.shape[2]), g.dtype) for g in grads],
        scratch_shapes=[pltpu.SemaphoreType.DMA((n,)), pltpu.SemaphoreType.DMA((n,))], name="pair_swap",
    )(*grads)


def _pair_add(g, got, core):
    _, rows, cols = g.shape
    half = rows // 2
    tr = min(256, half)
    nt = half // tr

    def kern(c_ref, g_ref, o_ref, out_ref):
        out_ref[...] = g_ref[...] + o_ref[...]

    return pl.pallas_call(
        kern, grid_spec=pltpu.PrefetchScalarGridSpec(
            num_scalar_prefetch=1, grid=(N_CHIPS, nt),
            in_specs=[pl.BlockSpec((None, tr, cols), lambda j, i, c_ref: (j, c_ref[0] * nt + i, 0)),
                      pl.BlockSpec((None, tr, cols), lambda j, i, c_ref: (j, i, 0))],
            out_specs=pl.BlockSpec((None, tr, cols), lambda j, i, c_ref: (j, i, 0))),
        out_shape=jax.ShapeDtypeStruct((N_CHIPS, half, cols), F32),
        compiler_params=_cp(("parallel", "parallel")), name="pair_add",
    )(core, g, got)


def _chip_exchange(parts):
    n = len(parts)

    def kern(*refs):
        copies = _exchange_copies(refs[:n], refs[n:2 * n], *refs[2 * n:])
        for cp in copies:
            cp.start()
        for cp in copies:
            cp.wait()

    out_shape, sems = _exchange_shapes(parts)
    return pl.pallas_call(kern, in_specs=[ANY] * n, out_specs=[ANY] * n, out_shape=out_shape, scratch_shapes=sems,
                          name="chip_exchange")(*parts)


def _chip_add(part, got, place):
    _, half, cols = part.shape
    tr = min(256, half)
    nt = half // tr

    def kern(p_ref, part_ref, got_ref, out_ref):
        out_ref[...] = part_ref[...] + got_ref[0] + got_ref[1] + got_ref[2]

    return pl.pallas_call(
        kern, grid_spec=pltpu.PrefetchScalarGridSpec(
            num_scalar_prefetch=1, grid=(nt,),
            in_specs=[pl.BlockSpec((None, tr, cols), lambda i, p_ref: (p_ref[0], i, 0)),
                      pl.BlockSpec((3, tr, cols), lambda i, p_ref: (0, i, 0))],
            out_specs=pl.BlockSpec((tr, cols), lambda i, p_ref: (p_ref[1] * nt + i, 0))),
        out_shape=jax.ShapeDtypeStruct((2 * half, cols), F32),
        compiler_params=_cp(("parallel",)), name="chip_add",
    )(place, part, got)


def _pair_join(fulls):
    n = len(fulls)

    def kern(*refs):
        ins, outs, (send_sems, recv_sems) = refs[:n], refs[n:2 * n], refs[2 * n:]
        x, y, c = _place()
        copies = []
        for w in range(n):
            half = ins[w].shape[0] // 2
            rows = pl.ds(c * half, half)
            copies.append(pltpu.make_async_remote_copy(
                src_ref=ins[w].at[rows, :], dst_ref=outs[w].at[rows, :], send_sem=send_sems.at[w],
                recv_sem=recv_sems.at[w], device_id=(x, y, 1 - c), device_id_type=MESH))
        for cp in copies:
            cp.start()
        for cp in copies:
            cp.wait()

    return pl.pallas_call(
        kern, in_specs=[ANY] * n, out_specs=[ANY] * n, out_shape=[jax.ShapeDtypeStruct(f.shape, f.dtype) for f in fulls],
        input_output_aliases={w: w for w in range(n)},
        scratch_shapes=[pltpu.SemaphoreType.DMA((n,)), pltpu.SemaphoreType.DMA((n,))], name="pair_join",
    )(*fulls)


def _allreduce_small(packed):
    rows = packed.shape[0]

    def kern(in_ref, out_ref, slots, send_sems, recv_sems):
        x, y, c = _place()
        mine = 4 * x + 2 * y + c
        slots[mine] = in_ref[...]
        started = []
        for k in range(1, N_DEV):
            peer = (x ^ (k >> 2), y ^ ((k >> 1) & 1), c ^ (k & 1))
            cp = pltpu.make_async_remote_copy(
                src_ref=in_ref, dst_ref=slots.at[mine], send_sem=send_sems.at[k - 1], recv_sem=recv_sems.at[k - 1],
                device_id=peer, device_id_type=MESH)
            cp.start()
            started.append(cp)
        for cp in started:
            cp.wait()
        acc = slots[0]
        for s in range(1, N_DEV):
            acc = acc + slots[s]
        out_ref[...] = acc

    vm = pl.BlockSpec(memory_space=pltpu.VMEM)
    return pl.pallas_call(
        kern, in_specs=[vm], out_specs=vm, out_shape=jax.ShapeDtypeStruct((rows, LANES), F32),
        scratch_shapes=[pltpu.VMEM((N_DEV, rows, LANES), F32), pltpu.SemaphoreType.DMA((N_DEV - 1,)),
                        pltpu.SemaphoreType.DMA((N_DEV - 1,))],
        name="allreduce_small",
    )(packed)


SMALL = ["norm1_g", "conv_w", "conv_b", "lru_w_a", "lru_b_a", "lru_w_x", "lru_b_x", "lru_lambda", "lru_out_g", "sb_out_g",
         "norm2_g", "final_g"]
BIG = ["w_in", "w_out", "w_up", "w_down"]
WEIGHTS = ["norm1_g", "w_in", "conv_w", "conv_b", "lru_w_a", "lru_b_a", "lru_w_x", "lru_b_x", "lru_lambda", "lru_out_g",
           "sb_out_g", "w_out", "norm2_g", "w_up", "w_down", "final_g"]


def _pack(arrays):
    flat = []
    for a in arrays:
        a = a.reshape(-1).astype(F32)
        flat.append(jnp.pad(a, (0, (-a.shape[0]) % LANES)))
    v = jnp.concatenate(flat)
    v = jnp.pad(v, (0, (-v.shape[0]) % (LANES * SUBLANES)))
    return v.reshape(-1, LANES)


def _unpack(packed, shapes):
    v, out, off = packed.reshape(-1), [], 0
    for shp in shapes:
        size = math.prod(shp)
        out.append(v[off:off + size].reshape(shp))
        off += size + (-size) % LANES
    return out


def _blockdiag_pairs(w):
    w = w.reshape(4, 2, DH, DH)
    z = jnp.zeros((4, DH, DH), w.dtype)
    return jnp.concatenate([jnp.concatenate([w[:, 0], z], axis=2), jnp.concatenate([z, w[:, 1]], axis=2)], axis=1)


def _blockdiag_unpairs(wbd):
    return jnp.stack([wbd[:, :DH, :DH], wbd[:, DH:, DH:]], axis=1).reshape(8, DH, DH)


def _full_cols(g):
    return jnp.transpose(g, (1, 0, 2)).reshape(g.shape[1], N_CHIPS * g.shape[2])


def _local_step(x2, tgt, seq, norm1_g, w_in_f, conv_w_f, conv_b, w_a, b_a, w_x, b_x, lru_lambda, lru_out_g, sb_out_g, rest,
                norm2_g, final_g, place=None):
    wbd = jnp.concatenate([_blockdiag_pairs(w_a), _blockdiag_pairs(w_x)], axis=2).astype(BF16)
    ba, bx = b_a.reshape(1, LRU_WIDTH), b_x.reshape(1, LRU_WIDTH)
    gf = final_g.reshape(1, D_MODEL)

    xl, qkv, xn = _inproj(x2, norm1_g, w_in_f)
    h, y_lru = _lru_fwd(xl, conv_w_f, conv_b, wbd, ba, bx, lru_lambda, seq)
    if place is None:
        o, tot, _ = _attn_fwd(qkv, seq)
        w_out_f, w_up_f, w_down_f = rest
    else:
        o, tot, (gw_out, gw_up, gw_down) = _attn_fwd(qkv, seq, rest)
        w_out_f, w_up_f, w_down_f = gw_out.reshape(D_MODEL, D_MODEL), _full_cols(gw_up), gw_down.reshape(D_FF, D_MODEL)
    h1, mix = _outproj(y_lru, o, x2, lru_out_g, sb_out_g, w_out_f)
    h2, hn, up, u2 = _mlp_fwd(h1, norm2_g, w_up_f, w_down_f)
    dh2, loss_part, d_final = _loss_head(h2, tgt, gf)

    dpre = _mlp_bwd_pre(dh2, w_down_f, up)
    g_w_down = _matmul(u2, dh2, "tn", F32, "dw_down", 1024, 1024, 512).reshape(N_CHIPS, D_FF // N_CHIPS, D_MODEL)
    g_w_up = _matmul(hn, dpre, "tn", F32, "dw_up", 1024, D_FF // N_CHIPS, 512, split_cols=True)
    dh1, d_norm2 = _proj_bwd_norm([dpre], w_up_f, h1, norm2_g, dh2, "mlp_bwd_in")
    g_w_out = _matmul(mix, dh1, "tn", F32, "dw_out", 1024, 1024, 512).reshape(N_CHIPS, D_MODEL // N_CHIPS, D_MODEL)
    dy_lru, do, d_ga, d_gb = _outproj_bwd(dh1, w_out_f, y_lru, o, lru_out_g, sb_out_g)
    late = [g_w_out, g_w_up, g_w_down]
    if place is None:
        dq, dk, dv, _ = _attn_bwd(qkv, do, tot, seq)
    else:
        parts = [_pair_add(g, r, place[1:]) for g, r in zip(late, _pair_swap(late))]
        dq, dk, dv, got = _attn_bwd(qkv, do, tot, seq, parts)
        late = [_chip_add(p, r, place) for p, r in zip(parts, got)]
    dxl, lru_small, d_wbd = _lru_bwd(xl, h, dy_lru, conv_w_f, conv_b, wbd, ba, bx, lru_lambda, seq)
    dx, d_norm1 = _proj_bwd_norm([dxl, dq, dk, dv], w_in_f, x2, norm1_g, dh1, "inproj_bwd")
    dproj = jnp.concatenate([dxl, dq, dk, dv], axis=1)
    g_w_in = _matmul(xn, dproj, "tn", F32, "dw_in", 1024, IN_COLS // N_CHIPS, 512, split_cols=True)
    small_parts = {
        "norm1_g": d_norm1, "conv_w": lru_small[:CONV_WIDTH], "conv_b": lru_small[4:5],
        "lru_w_a": _blockdiag_unpairs(d_wbd[:, :, :LANES]), "lru_b_a": lru_small[5:6],
        "lru_w_x": _blockdiag_unpairs(d_wbd[:, :, LANES:]), "lru_b_x": lru_small[6:7], "lru_lambda": lru_small[7:8],
        "lru_out_g": d_ga, "sb_out_g": d_gb, "norm2_g": d_norm2, "final_g": d_final,
    }
    return loss_part, dx, [g_w_in] + late, small_parts


def kernel(x, norm1_g, w_in, conv_w, conv_b, lru_w_a, lru_b_a, lru_w_x, lru_b_x, lru_lambda, lru_out_g, sb_out_g, w_out, norm2_g, w_up, w_down, final_g, loss_target, m_norm1_g, m_w_in, m_conv_w, m_conv_b, m_lru_w_a, m_lru_b_a, m_lru_w_x, m_lru_b_x, m_lru_lambda, m_lru_out_g, m_sb_out_g, m_w_out, m_norm2_g, m_w_up, m_w_down, m_final_g, v_norm1_g, v_w_in, v_conv_w, v_conv_b, v_lru_w_a, v_lru_b_a, v_lru_w_x, v_lru_b_x, v_lru_lambda, v_lru_out_g, v_sb_out_g, v_w_out, v_norm2_g, v_w_up, v_w_down, v_final_g):
    given = dict(locals())
    ne, seq, _ = x.shape
    t = ne * seq
    xi, yi, ci = _place()
    place = jnp.stack([2 * xi + yi, ci]).astype(jnp.int32)

    gw_in, gconv = _gather_weights([w_in[0].astype(BF16), conv_w[0]])
    rest = [w_out[0].astype(BF16), w_up[0].astype(BF16), w_down[0].astype(BF16)]
    loss_part, dx, (g_w_in, *late), small_parts = _local_step(
        x.reshape(t, D_MODEL), loss_target.reshape(t, D_MODEL), seq, norm1_g, _full_cols(gw_in), _full_cols(gconv), conv_b,
        lru_w_a[0], lru_b_a, lru_w_x[0], lru_b_x, lru_lambda, lru_out_g, sb_out_g, rest, norm2_g, final_g, place)

    full_shapes = {n: ((CONV_WIDTH, LRU_WIDTH) if n == "conv_w" else given[n].shape) for n in SMALL}
    red = _allreduce_small(_pack([small_parts[n] for n in SMALL] + [loss_part]))
    red_list = _unpack(red, [full_shapes[n] for n in SMALL] + [(1, LANES)])
    grads = dict(zip(SMALL, red_list[:-1]))
    loss = red_list[-1][0, 0]
    grads["conv_w"] = lax.dynamic_slice_in_dim(grads["conv_w"], place[0] * (LRU_WIDTH // N_CHIPS), LRU_WIDTH // N_CHIPS,
                                               axis=1).reshape(conv_w.shape)

    part = _pair_add(g_w_in, _pair_swap([g_w_in])[0], place[1:])
    g_w_in = _chip_add(part, _chip_exchange([part])[0], place)
    for n, full in zip(BIG, _pair_join([g_w_in] + late)):
        grads[n] = full.reshape(given[n].shape)

    delta, new_m, new_v = {}, {}, {}
    for n in BIG:
        shp = given[n].shape
        d, m2, v2 = _adamw(given[n][0], grads[n][0], given["m_" + n][0], given["v_" + n][0], "adamw_" + n)
        delta[n], new_m[n], new_v[n] = d.reshape(shp), m2.reshape(shp), v2.reshape(shp)
    shapes = [given[n].shape for n in SMALL]
    d, m2, v2 = _adamw(_pack([given[n] for n in SMALL]), _pack([grads[n] for n in SMALL]),
                       _pack([given["m_" + n] for n in SMALL]), _pack([given["v_" + n] for n in SMALL]), "adamw_small")
    for n, dd, mm, vv in zip(SMALL, _unpack(d, shapes), _unpack(m2, shapes), _unpack(v2, shapes)):
        delta[n], new_m[n], new_v[n] = dd, mm, vv

    return (loss, dx.reshape(x.shape), *[grads[n] for n in WEIGHTS], *[delta[n] for n in WEIGHTS],
            *[new_m[n] for n in WEIGHTS], *[new_v[n] for n in WEIGHTS])
```

```python
import functools
import math

import jax
import jax.numpy as jnp
from jax import lax
from jax.experimental import pallas as pl
from jax.experimental.pallas import tpu as pltpu

F32, BF16 = jnp.float32, jnp.bfloat16
MESH = pl.DeviceIdType.MESH

D_MODEL = 1024
LRU_WIDTH = 512
SB_WIDTH = 512
DH = 64
IN_COLS = 2 * LRU_WIDTH + 3 * SB_WIDTH
D_FF = 4 * D_MODEL
CONV_WIDTH = 4
LRU_C = 8.0
EPS = 1e-6
N_CHIPS = 4
N_DEV = 8
LANES = 128
SUBLANES = 8
TQ = 512
TK = 128
ATT_SCALE = 1.0 / math.sqrt(DH)
SKIP_LOG = -105.0
VMEM_LIMIT = 52 * 1024 * 1024

ADAM_LR, ADAM_B1, ADAM_B2, ADAM_EPS, ADAM_WD, ADAM_STEP = 0.001, 0.9, 0.999, 1e-08, 0.01, 10

_GELU_K = math.sqrt(2.0 / math.pi)
_GELU_C = 0.044715


def _cp(sem):
    return pltpu.CompilerParams(dimension_semantics=sem, vmem_limit_bytes=VMEM_LIMIT)


def _dot(a, b):
    return jnp.dot(a, b, preferred_element_type=F32)


def _dot_nt(a, b):
    return lax.dot_general(a, b, (((1,), (1,)), ((), ())), preferred_element_type=F32)


def _dot_tn(a, b):
    return lax.dot_general(a, b, (((0,), (0,)), ((), ())), preferred_element_type=F32)


def _rstd(x):
    return lax.rsqrt(jnp.mean(x * x, axis=-1, keepdims=True) + EPS)


def _rms_bwd(x, g, dy):
    r = _rstd(x)
    gd = g * dy
    dx = r * gd - x * (r * r * r) * jnp.mean(x * gd, axis=-1, keepdims=True)
    return dx, jnp.sum(dy * x * r, axis=0, keepdims=True)


def _sigmoid(x):
    return 1.0 / (1.0 + jnp.exp(-x))


def _softplus(x):
    return jnp.maximum(x, 0.0) + jnp.log(1.0 + jnp.exp(-jnp.abs(x)))


def _neg_expm1(x):
    series = -x * (1.0 + x * (0.5 + x * (1.0 / 6.0 + x * (1.0 / 24.0))))
    return jnp.where(x > -0.01, series, 1.0 - jnp.exp(x))


def _gelu(g):
    t = jnp.tanh(_GELU_K * (g + _GELU_C * g * g * g))
    return 0.5 * g * (1.0 + t), t


def _gelu_grad(g, t):
    return 0.5 * (1.0 + t) + 0.5 * g * (1.0 - t * t) * _GELU_K * (1.0 + 3.0 * _GELU_C * g * g)


def _rows(shape):
    return lax.broadcasted_iota(jnp.int32, shape, 0)


def _shift_down(x, s, fill):
    return jnp.where(_rows(x.shape) >= s, pltpu.roll(x, s, 0), fill)


def _shift_up(x, s, fill):
    n = x.shape[0]
    return jnp.where(_rows(x.shape) < n - s, pltpu.roll(x, n - s, 0), fill)


def _row_of(x, idx):
    return jnp.sum(jnp.where(_rows(x.shape) == idx, x, 0.0), axis=0, keepdims=True)


def _matmul(a, b, dims, out_dtype, name, tm, tn, tk, split_cols=False):
    if dims == "nn":
        (m, kk), n, dot = a.shape, b.shape[1], _dot
    elif dims == "nt":
        (m, kk), n, dot = a.shape, b.shape[0], _dot_nt
    else:
        (kk, m), n, dot = a.shape, b.shape[1], _dot_tn
    tm, tn, tk = min(tm, m), min(tn, n), min(tk, kk)
    assert m % tm == 0 and n % tn == 0 and kk % tk == 0, (name, m, n, kk)
    nk = kk // tk

    def kern(a_ref, b_ref, o_ref, acc_ref):
        k = pl.program_id(2)

        @pl.when(k == 0)
        def _():
            acc_ref[...] = jnp.zeros_like(acc_ref)

        acc_ref[...] += dot(a_ref[...].astype(BF16), b_ref[...].astype(BF16))

        @pl.when(k == nk - 1)
        def _():
            o_ref[...] = acc_ref[...].astype(o_ref.dtype)

    if split_cols:
        out_shape = jax.ShapeDtypeStruct((n // tn, m, tn), out_dtype)
        o_spec = pl.BlockSpec((None, tm, tn), lambda i, j, k: (j, i, 0))
    else:
        out_shape = jax.ShapeDtypeStruct((m, n), out_dtype)
        o_spec = pl.BlockSpec((tm, tn), lambda i, j, k: (i, j))
    if dims == "nn":
        a_spec = pl.BlockSpec((tm, tk), lambda i, j, k: (i, k))
        b_spec = pl.BlockSpec((tk, tn), lambda i, j, k: (k, j))
    elif dims == "nt":
        a_spec = pl.BlockSpec((tm, tk), lambda i, j, k: (i, k))
        b_spec = pl.BlockSpec((tn, tk), lambda i, j, k: (j, k))
    else:
        a_spec = pl.BlockSpec((tk, tm), lambda i, j, k: (k, i))
        b_spec = pl.BlockSpec((tk, tn), lambda i, j, k: (k, j))
    return pl.pallas_call(
        kern, grid=(m // tm, n // tn, nk), in_specs=[a_spec, b_spec], out_specs=o_spec, out_shape=out_shape,
        scratch_shapes=[pltpu.VMEM((tm, tn), F32)], compiler_params=_cp(("parallel", "parallel", "arbitrary")), name=name,
    )(a, b)


ANY = pl.BlockSpec(memory_space=pl.ANY)


def _place():
    return lax.axis_index("x"), lax.axis_index("y"), lax.axis_index("c")


def _other_chips(x, y):
    return [(1 - x, y), (x, 1 - y), (1 - x, 1 - y)]


def _gather_copies(ins, outs, send_sems, recv_sems, loc_sems):
    x, y, c = _place()
    mine = 2 * x + y
    copies = []
    for w in range(len(ins)):
        copies.append(pltpu.make_async_copy(ins[w], outs[w].at[mine], loc_sems.at[w]))
        for k, chip in enumerate(_other_chips(x, y)):
            copies.append(pltpu.make_async_remote_copy(
                src_ref=ins[w], dst_ref=outs[w].at[mine], send_sem=send_sems.at[3 * w + k],
                recv_sem=recv_sems.at[3 * w + k], device_id=(*chip, c), device_id_type=MESH))
    return copies


def _gather_shapes(shards):
    return ([jax.ShapeDtypeStruct((N_CHIPS,) + s.shape, s.dtype) for s in shards],
            [pltpu.SemaphoreType.DMA((3 * len(shards),)), pltpu.SemaphoreType.DMA((3 * len(shards),)),
             pltpu.SemaphoreType.DMA((len(shards),))])


def _exchange_copies(ins, outs, send_sems, recv_sems):
    x, y, c = _place()
    copies = []
    for w in range(len(ins)):
        for k, chip in enumerate(_other_chips(x, y)):
            copies.append(pltpu.make_async_remote_copy(
                src_ref=ins[w].at[2 * chip[0] + chip[1]], dst_ref=outs[w].at[k], send_sem=send_sems.at[3 * w + k],
                recv_sem=recv_sems.at[3 * w + k], device_id=(*chip, c), device_id_type=MESH))
    return copies


def _exchange_shapes(parts):
    return ([jax.ShapeDtypeStruct((3,) + p.shape[1:], p.dtype) for p in parts],
            [pltpu.SemaphoreType.DMA((3 * len(parts),)), pltpu.SemaphoreType.DMA((3 * len(parts),))])


def _grid_ends(grid):
    i, j = pl.program_id(0), pl.program_id(1)
    return (i == 0) & (j == 0), (i == grid[0] - 1) & (j == grid[1] - 1)


def _inproj(x, g1, w_in):
    t = x.shape[0]
    tm = min(512, t)

    def kern(x_ref, g_ref, w_ref, xl_ref, qkv_ref, xn_ref):
        xv = x_ref[...]
        xn = (xv * _rstd(xv) * g_ref[...]).astype(BF16)
        xn_ref[...] = xn
        xl_ref[...] = _dot(xn, w_ref[:, : 2 * LRU_WIDTH])
        qkv_ref[...] = _dot(xn, w_ref[:, 2 * LRU_WIDTH:]).astype(BF16)

    row = lambda c: pl.BlockSpec((tm, c), lambda i: (i, 0))
    return pl.pallas_call(
        kern, grid=(t // tm,),
        in_specs=[row(D_MODEL), pl.BlockSpec((1, D_MODEL), lambda i: (0, 0)), pl.BlockSpec((D_MODEL, IN_COLS), lambda i: (0, 0))],
        out_specs=[row(2 * LRU_WIDTH), row(3 * SB_WIDTH), row(D_MODEL)],
        out_shape=[jax.ShapeDtypeStruct((t, 2 * LRU_WIDTH), F32), jax.ShapeDtypeStruct((t, 3 * SB_WIDTH), BF16),
                   jax.ShapeDtypeStruct((t, D_MODEL), BF16)],
        compiler_params=_cp(("parallel",)), name="inproj",
    )(x, g1, w_in)


def _conv_taps(hist, u):
    cat = jnp.concatenate([hist, u], axis=0)
    return [pltpu.roll(cat, CONV_WIDTH - 1 - k, 0)[SUBLANES:] for k in range(CONV_WIDTH - 1)] + [u]


def _lru_gates(c, wbd_ref, ba, bx, sp):
    gas, gxs = [], []
    for p in range(LRU_WIDTH // LANES):
        gax = _dot(c[:, LANES * p: LANES * (p + 1)].astype(BF16), wbd_ref[p])
        gas.append(gax[:, :LANES])
        gxs.append(gax[:, LANES:])
    r = _sigmoid(jnp.concatenate(gas, axis=1) + ba)
    i = _sigmoid(jnp.concatenate(gxs, axis=1) + bx)
    la = (-LRU_C) * r * sp
    a = jnp.exp(la)
    mult = jnp.sqrt(_neg_expm1(2.0 * la))
    return r, i, a, mult


def _scan_fwd(a, b):
    s = 1
    while s < a.shape[0]:
        b = b + a * _shift_down(b, s, 0.0)
        a = a * _shift_down(a, s, 1.0)
        s *= 2
    return a, b


def _scan_rev(a, b):
    s = 1
    while s < a.shape[0]:
        b = b + a * _shift_up(b, s, 0.0)
        a = a * _shift_up(a, s, 1.0)
        s *= 2
    return a, b


def _lru_param_specs(grid_rank):
    z2 = (lambda e, c: (0, 0)) if grid_rank == 2 else None
    return [
        pl.BlockSpec((CONV_WIDTH, LRU_WIDTH), z2), pl.BlockSpec((1, LRU_WIDTH), z2),
        pl.BlockSpec((LRU_WIDTH // LANES, LANES, 2 * LANES), lambda e, c: (0, 0, 0)),
        pl.BlockSpec((1, LRU_WIDTH), z2), pl.BlockSpec((1, LRU_WIDTH), z2), pl.BlockSpec((1, LRU_WIDTH), z2),
    ]


def _lru_fwd(xl, conv_w, conv_b, wbd, ba, bx, lam, seq):
    t = xl.shape[0]
    tc = min(512, seq)
    nc = seq // tc

    def kern(u_ref, g_ref, cw_ref, cb_ref, wbd_ref, ba_ref, bx_ref, lam_ref, h_ref, y_ref, hist_ref, hcar_ref):
        @pl.when(pl.program_id(1) == 0)
        def _():
            hist_ref[...] = jnp.zeros_like(hist_ref)
            hcar_ref[...] = jnp.zeros_like(hcar_ref)

        u = u_ref[...]
        taps = _conv_taps(hist_ref[...], u)
        hist_ref[...] = u_ref[tc - SUBLANES:, :]
        c = cb_ref[...]
        for k in range(CONV_WIDTH):
            c = c + taps[k] * cw_ref[k:k + 1, :]
        sp = _softplus(-lam_ref[...])
        _, i, a, mult = _lru_gates(c, wbd_ref, ba_ref[...], bx_ref[...], sp)
        aa, bb = _scan_fwd(a, mult * i * c)
        h = bb + aa * hcar_ref[0:1, :]
        h_ref[...] = h
        hcar_ref[0:1, :] = h_ref[tc - 1:tc, :]
        y_ref[...] = h * _gelu(g_ref[...])[0]

    chunk = lambda col: pl.BlockSpec((tc, LRU_WIDTH), lambda e, c: (e * nc + c, col))
    out = jax.ShapeDtypeStruct((t, LRU_WIDTH), F32)
    return pl.pallas_call(
        kern, grid=(t // seq, nc), in_specs=[chunk(0), chunk(1)] + _lru_param_specs(2),
        out_specs=[chunk(0), chunk(0)], out_shape=[out, out],
        scratch_shapes=[pltpu.VMEM((SUBLANES, LRU_WIDTH), F32), pltpu.VMEM((SUBLANES, LRU_WIDTH), F32)],
        compiler_params=_cp(("arbitrary", "arbitrary")), name="lru_fwd",
    )(xl, xl, conv_w, conv_b, wbd, ba, bx, lam)


def _att_consts():
    row = lax.broadcasted_iota(jnp.int32, (TQ, 2 * TK), 0)
    key = lax.broadcasted_iota(jnp.int32, (TQ, 2 * TK), 1) & (TK - 1)
    return [(TK * jj + key) < row for jj in range(TQ // TK)]


def _sum_matrix(kind):
    j = lax.broadcasted_iota(jnp.int32, (2 * TK, 2 * TK), 0) & (TK - 1)
    s = lax.broadcasted_iota(jnp.int32, (2 * TK, 2 * TK), 1)
    pick = {"after": j > s, "upto": j <= s, "before": j < s}[kind]
    return jnp.where((s >= TK) | pick, 1.0, 0.0).astype(BF16)


def _hi_lo(x):
    hi = x.astype(BF16)
    return hi, (x - hi.astype(F32)).astype(BF16)


def _pair_sums(x, m):
    hi, lo = _hi_lo(x)
    out = []
    for hd in range(2):
        cols = slice(hd * TK, (hd + 1) * TK)
        out.append(_dot(jnp.concatenate([hi[:, cols], lo[:, cols]], axis=1), m))
    return [o[:, :TK] for o in out], [o[:, TK:] for o in out]


def _att_logits(qb, kbd):
    z = _dot(qb, kbd)
    lg = jnp.log(1.0 + jnp.exp(-jnp.abs(z)))
    lb = jnp.minimum(z, 0.0) - lg
    return lb, lb - z


def _head_diag(x, rows_first):
    n = x.shape[0] if rows_first else x.shape[1]
    idx = lax.broadcasted_iota(jnp.int32, x.shape, 0 if rows_first else 1)
    return jnp.where(idx < n // 2, x, 0), jnp.where(idx >= n // 2, x, 0)


def _scaled_q(q_ref, q0):
    return (q_ref[pl.ds(q0, TQ), :].astype(F32) * ATT_SCALE).astype(BF16)


def _qkv_specs(seq):
    n = SB_WIDTH // LANES
    return [pl.BlockSpec((seq, LANES), lambda e, p, off=off: (e, off * n + p)) for off in range(3)]


def _attn_fwd(qkv, seq, shards=()):
    t = qkv.shape[0]
    ne, nq, nk = t // seq, seq // TQ, seq // TK
    grid = (ne, SB_WIDTH // LANES)
    n = len(shards)

    def kern(q_ref, k_ref, v_ref, *rest):
        (o_ref, tot_ref, kmin_ref), (kbd_scr, vbd_scr) = rest[n:n + 3], rest[2 * n + 3:2 * n + 5]
        first, last = _grid_ends(grid)
        if n:
            @pl.when(first)
            def _():
                for cp in _gather_copies(rest[:n], rest[n + 3:2 * n + 3], *rest[2 * n + 5:]):
                    cp.start()

        causal = _att_consts()
        after = _sum_matrix("after")

        def prep(j, _):
            k0 = pl.multiple_of(j * TK, TK)
            top, bot = _head_diag(k_ref[pl.ds(k0, TK), :].astype(F32).T, True)
            kbd_scr[j] = jnp.concatenate([top, bot], axis=1).astype(BF16)
            left, right = _head_diag(v_ref[pl.ds(k0, TK), :], False)
            vbd_scr[j] = jnp.concatenate([left, right], axis=0)
            return 0

        lax.fori_loop(0, nk, prep, 0)

        def block(j, qb, st, mask):
            c0, c1, oacc = st
            lb, l1 = _att_logits(qb, kbd_scr[j])
            if mask is not None:
                l1 = jnp.where(mask, l1, 0.0)
            (s0, s1), (r0, r1) = _pair_sums(l1, after)
            att = jnp.exp(lb + jnp.concatenate([s0 + c0, s1 + c1], axis=1))
            if mask is not None:
                att = jnp.where(mask, att, 0.0)
            return c0 + r0, c1 + r1, oacc + _dot(att.astype(BF16), vbd_scr[j])

        def qloop(qi, _):
            q0 = pl.multiple_of(qi * TQ, TQ)
            qb = _scaled_q(q_ref, q0)
            zero = jnp.zeros((TQ, TK), F32)
            st = (zero, zero, jnp.zeros((TQ, LANES), F32))
            for jj in reversed(range(TQ // TK)):
                st = block((TQ // TK) * qi + jj, qb, st, causal[jj])

            npair = (TQ // TK // 2) * qi

            def more(its):
                return (its[0] < npair) & (jnp.max(jnp.maximum(its[1], its[2])) > SKIP_LOG)

            def kloop(its):
                j = 2 * (npair - its[0]) - 1
                return (its[0] + 1,) + block(j - 1, qb, block(j, qb, its[1:], None), None)

            done, c0, c1, oacc = lax.while_loop(more, kloop, (jnp.int32(0),) + st)
            o_ref[pl.ds(q0, TQ), :] = oacc
            tot_ref[pl.ds(q0, TQ), :] = jnp.concatenate([c0, c1], axis=1)
            kmin_ref[pl.program_id(0), pl.program_id(1), qi] = 2 * (npair - done)
            return 0

        lax.fori_loop(0, nq, qloop, 0)

        if n:
            @pl.when(last)
            def _():
                for cp in _gather_copies(rest[:n], rest[n + 3:2 * n + 3], *rest[2 * n + 5:]):
                    cp.wait()

    g_shapes, g_sems = _gather_shapes(shards) if n else ([], [])
    out = pl.pallas_call(
        kern, grid=grid, in_specs=_qkv_specs(seq) + [ANY] * n,
        out_specs=[pl.BlockSpec((seq, LANES), lambda e, p: (e, p)), pl.BlockSpec((seq, 2 * TK), lambda e, p: (e, p)),
                   pl.BlockSpec(memory_space=pltpu.SMEM)] + [ANY] * n,
        out_shape=[jax.ShapeDtypeStruct((t, SB_WIDTH), F32), jax.ShapeDtypeStruct((t, 2 * TK * SB_WIDTH // LANES), F32),
                   jax.ShapeDtypeStruct((ne, SB_WIDTH // LANES, nq), jnp.int32)] + g_shapes,
        scratch_shapes=[pltpu.VMEM((nk, LANES, 2 * TK), BF16), pltpu.VMEM((nk, 2 * TK, LANES), BF16)] + g_sems,
        compiler_params=_cp(("arbitrary", "arbitrary")), name="attn_fwd",
    )(qkv, qkv, qkv, *shards)
    return out[0], out[1], out[2], list(out[3:])


def _outproj(y_lru, o, x, ga, gb, w_out):
    t = x.shape[0]
    tm = min(512, t)

    def kern(y_ref, o_ref, x_ref, ga_ref, gb_ref, w_ref, h1_ref, mix_ref):
        yv, ov = y_ref[...], o_ref[...]
        mix = jnp.concatenate([yv * _rstd(yv) * ga_ref[...], ov * _rstd(ov) * gb_ref[...]], axis=1).astype(BF16)
        mix_ref[...] = mix
        h1_ref[...] = x_ref[...] + _dot(mix, w_ref[...])

    row = lambda c: pl.BlockSpec((tm, c), lambda i: (i, 0))
    vec = lambda c: pl.BlockSpec((1, c), lambda i: (0, 0))
    return pl.pallas_call(
        kern, grid=(t // tm,),
        in_specs=[row(LRU_WIDTH), row(SB_WIDTH), row(D_MODEL), vec(LRU_WIDTH), vec(SB_WIDTH),
                  pl.BlockSpec((D_MODEL, D_MODEL), lambda i: (0, 0))],
        out_specs=[row(D_MODEL), row(D_MODEL)],
        out_shape=[jax.ShapeDtypeStruct((t, D_MODEL), F32), jax.ShapeDtypeStruct((t, D_MODEL), BF16)],
        compiler_params=_cp(("parallel",)), name="outproj",
    )(y_lru, o, x, ga, gb, w_out)


def _mlp_fwd(h1, g2, w_up, w_down):
    t = h1.shape[0]
    tm, tf = min(512, t), 1024
    nf = D_FF // tf

    def kern(h1_ref, g_ref, wu_ref, wd_ref, h2_ref, hn_ref, up_ref, u2_ref, acc_ref):
        f = pl.program_id(1)

        @pl.when(f == 0)
        def _():
            hv = h1_ref[...]
            hn_ref[...] = (hv * _rstd(hv) * g_ref[...]).astype(BF16)
            acc_ref[...] = hv

        up = jnp.maximum(_dot(hn_ref[...], wu_ref[...]), 0.0)
        u2 = (up * up).astype(BF16)
        up_ref[...] = up.astype(BF16)
        u2_ref[...] = u2
        acc_ref[...] += _dot(u2, wd_ref[...])

        @pl.when(f == nf - 1)
        def _():
            h2_ref[...] = acc_ref[...]

    return pl.pallas_call(
        kern, grid=(t // tm, nf),
        in_specs=[pl.BlockSpec((tm, D_MODEL), lambda i, f: (i, 0)), pl.BlockSpec((1, D_MODEL), lambda i, f: (0, 0)),
                  pl.BlockSpec((D_MODEL, tf), lambda i, f: (0, f)), pl.BlockSpec((tf, D_MODEL), lambda i, f: (f, 0))],
        out_specs=[pl.BlockSpec((tm, D_MODEL), lambda i, f: (i, 0)), pl.BlockSpec((tm, D_MODEL), lambda i, f: (i, 0)),
                   pl.BlockSpec((tm, tf), lambda i, f: (i, f)), pl.BlockSpec((tm, tf), lambda i, f: (i, f))],
        out_shape=[jax.ShapeDtypeStruct((t, D_MODEL), F32), jax.ShapeDtypeStruct((t, D_MODEL), BF16),
                   jax.ShapeDtypeStruct((t, D_FF), BF16), jax.ShapeDtypeStruct((t, D_FF), BF16)],
        scratch_shapes=[pltpu.VMEM((tm, D_MODEL), F32)],
        compiler_params=_cp(("parallel", "arbitrary")), name="mlp_fwd",
    )(h1, g2, w_up, w_down)


def _loss_head(h2, target, gf):
    t = h2.shape[0]
    tm = min(512, t)

    def kern(h_ref, t_ref, g_ref, dh_ref, loss_ref, dg_ref):
        @pl.when(pl.program_id(0) == 0)
        def _():
            loss_ref[...] = jnp.zeros_like(loss_ref)
            dg_ref[...] = jnp.zeros_like(dg_ref)

        hv, g = h_ref[...], g_ref[...]
        err = hv * _rstd(hv) * g - t_ref[...]
        lane = lax.broadcasted_iota(jnp.int32, (1, LANES), 1)
        loss_ref[...] += jnp.where(lane == 0, 0.5 * jnp.sum(err * err) / D_MODEL, 0.0)
        dx, dg = _rms_bwd(hv, g, err * (1.0 / D_MODEL))
        dh_ref[...] = dx
        dg_ref[...] += dg

    row = pl.BlockSpec((tm, D_MODEL), lambda i: (i, 0))
    vec = pl.BlockSpec((1, D_MODEL), lambda i: (0, 0))
    return pl.pallas_call(
        kern, grid=(t // tm,), in_specs=[row, row, vec],
        out_specs=[row, pl.BlockSpec((1, LANES), lambda i: (0, 0)), vec],
        out_shape=[jax.ShapeDtypeStruct((t, D_MODEL), F32), jax.ShapeDtypeStruct((1, LANES), F32),
                   jax.ShapeDtypeStruct((1, D_MODEL), F32)],
        compiler_params=_cp(("arbitrary",)), name="loss_head",
    )(h2, target, gf)


def _mlp_bwd_pre(dh2, w_down, up):
    t = dh2.shape[0]
    tm, tf = min(512, t), 1024

    def kern(d_ref, w_ref, up_ref, o_ref):
        du2 = _dot_nt(d_ref[...].astype(BF16), w_ref[...])
        o_ref[...] = (du2 * (2.0 * up_ref[...].astype(F32))).astype(BF16)

    return pl.pallas_call(
        kern, grid=(t // tm, D_FF // tf),
        in_specs=[pl.BlockSpec((tm, D_MODEL), lambda i, f: (i, 0)), pl.BlockSpec((tf, D_MODEL), lambda i, f: (f, 0)),
                  pl.BlockSpec((tm, tf), lambda i, f: (i, f))],
        out_specs=pl.BlockSpec((tm, tf), lambda i, f: (i, f)), out_shape=jax.ShapeDtypeStruct((t, D_FF), BF16),
        compiler_params=_cp(("parallel", "parallel")), name="mlp_bwd_pre",
    )(dh2, w_down, up)


def _proj_bwd_norm(dys, w, x, g, resid, name):
    t = x.shape[0]
    tm = min(512, t)
    widths = [dy.shape[1] for dy in dys]
    n = len(dys)

    def kern(*refs):
        dy_refs, (w_ref, x_ref, g_ref, r_ref, dx_ref, dg_ref) = refs[:n], refs[n:]

        @pl.when(pl.program_id(0) == 0)
        def _():
            dg_ref[...] = jnp.zeros_like(dg_ref)

        off, dxn = 0, None
        for dy_ref, wd in zip(dy_refs, widths):
            part = _dot_nt(dy_ref[...], w_ref[:, off:off + wd])
            dxn = part if dxn is None else dxn + part
            off += wd
        dx, dg = _rms_bwd(x_ref[...], g_ref[...], dxn)
        dx_ref[...] = r_ref[...] + dx
        dg_ref[...] += dg

    row = lambda c: pl.BlockSpec((tm, c), lambda i: (i, 0))
    vec = pl.BlockSpec((1, D_MODEL), lambda i: (0, 0))
    return pl.pallas_call(
        kern, grid=(t // tm,),
        in_specs=[row(wd) for wd in widths] + [pl.BlockSpec(w.shape, lambda i: (0, 0)), row(D_MODEL), vec, row(D_MODEL)],
        out_specs=[row(D_MODEL), vec],
        out_shape=[jax.ShapeDtypeStruct((t, D_MODEL), F32), jax.ShapeDtypeStruct((1, D_MODEL), F32)],
        compiler_params=_cp(("arbitrary",)), name=name,
    )(*dys, w, x, g, resid)


def _outproj_bwd(dh1, w_out, y_lru, o, ga, gb):
    t = dh1.shape[0]
    tm = min(512, t)

    def kern(d_ref, w_ref, y_ref, o_ref, ga_ref, gb_ref, dy_ref, do_ref, dga_ref, dgb_ref):
        @pl.when(pl.program_id(0) == 0)
        def _():
            dga_ref[...] = jnp.zeros_like(dga_ref)
            dgb_ref[...] = jnp.zeros_like(dgb_ref)

        dmix = _dot_nt(d_ref[...].astype(BF16), w_ref[...])
        dy, dga = _rms_bwd(y_ref[...], ga_ref[...], dmix[:, :LRU_WIDTH])
        do, dgb = _rms_bwd(o_ref[...], gb_ref[...], dmix[:, LRU_WIDTH:])
        dy_ref[...] = dy
        do_ref[...] = do
        dga_ref[...] += dga
        dgb_ref[...] += dgb

    row = lambda c: pl.BlockSpec((tm, c), lambda i: (i, 0))
    vec = pl.BlockSpec((1, LRU_WIDTH), lambda i: (0, 0))
    half = jax.ShapeDtypeStruct((t, LRU_WIDTH), F32)
    gsum = jax.ShapeDtypeStruct((1, LRU_WIDTH), F32)
    return pl.pallas_call(
        kern, grid=(t // tm,),
        in_specs=[row(D_MODEL), pl.BlockSpec((D_MODEL, D_MODEL), lambda i: (0, 0)), row(LRU_WIDTH), row(SB_WIDTH), vec, vec],
        out_specs=[row(LRU_WIDTH), row(SB_WIDTH), vec, vec], out_shape=[half, half, gsum, gsum],
        compiler_params=_cp(("arbitrary",)), name="outproj_bwd",
    )(dh1, w_out, y_lru, o, ga, gb)


def _attn_bwd(qkv, do, tot, kmin, seq):
    t = qkv.shape[0]
    ne, nq, nk = t // seq, seq // TQ, seq // TK

    def kern(q_ref, k_ref, v_ref, do_ref, tot_ref, kmin_ref, dq_ref, dk_ref, dv_ref,
             kbd_scr, vtbd_scr, kbd2_scr, dkt_scr, dvt_scr):
        causal = _att_consts()
        upto, before = _sum_matrix("upto"), _sum_matrix("before")

        def prep(j, _):
            k0 = pl.multiple_of(j * TK, TK)
            kb = k_ref[pl.ds(k0, TK), :]
            top, bot = _head_diag(kb.astype(F32).T, True)
            kbd_scr[j] = jnp.concatenate([top, bot], axis=1).astype(BF16)
            top, bot = _head_diag(v_ref[pl.ds(k0, TK), :].astype(F32).T, True)
            vtbd_scr[j] = jnp.concatenate([top, bot], axis=1).astype(BF16)
            left, right = _head_diag(kb, False)
            kbd2_scr[j] = jnp.concatenate([left, right], axis=0)
            dkt_scr[j] = jnp.zeros((LANES, 2 * TK), F32)
            dvt_scr[j] = jnp.zeros((LANES, 2 * TK), F32)
            return 0

        lax.fori_loop(0, nk, prep, 0)

        def block(j, qb, qt, dob, dot_, totb, st, mask):
            f0, f1, p0, p1, dqacc = st
            lb, l1 = _att_logits(qb, kbd_scr[j])
            if mask is not None:
                l1 = jnp.where(mask, l1, 0.0)
            (s0, s1), (r0, r1) = _pair_sums(l1, upto)
            att = jnp.exp(lb + (totb - jnp.concatenate([s0 + f0, s1 + f1], axis=1)))
            if mask is not None:
                att = jnp.where(mask, att, 0.0)
            pw = att * _dot(dob, vtbd_scr[j])
            (e0, e1), (t0, t1) = _pair_sums(pw, before)
            dz = pw - jnp.exp(lb) * (pw + jnp.concatenate([e0 + p0, e1 + p1], axis=1))
            if mask is not None:
                dz = jnp.where(mask, dz, 0.0)
            dzb = dz.astype(BF16)
            dkt_scr[j] += _dot(qt, dzb)
            dvt_scr[j] += _dot(dot_, att.astype(BF16))
            return f0 + r0, f1 + r1, p0 + t0, p1 + t1, dqacc + _dot(dzb, kbd2_scr[j])

        def qloop(qi, _):
            q0 = pl.multiple_of(qi * TQ, TQ)
            qb = _scaled_q(q_ref, q0)
            qt = qb.astype(F32).T.astype(BF16)
            do32 = do_ref[pl.ds(q0, TQ), :]
            dob, dot_ = do32.astype(BF16), do32.T.astype(BF16)
            totb = tot_ref[pl.ds(q0, TQ), :]
            zero = jnp.zeros((TQ, TK), F32)
            st = (zero, zero, zero, zero, jnp.zeros((TQ, LANES), F32))

            k0 = kmin_ref[pl.program_id(0), pl.program_id(1), qi]

            def kloop(it, st):
                j = k0 + 2 * it
                return block(j + 1, qb, qt, dob, dot_, totb, block(j, qb, qt, dob, dot_, totb, st, None), None)

            st = lax.fori_loop(0, ((TQ // TK) * qi - k0) // 2, kloop, st)
            for jj in range(TQ // TK):
                st = block((TQ // TK) * qi + jj, qb, qt, dob, dot_, totb, st, causal[jj])
            dq_ref[pl.ds(q0, TQ), :] = (st[4] * ATT_SCALE).astype(BF16)
            return 0

        lax.fori_loop(0, nq, qloop, 0)

        def finish(j, _):
            k0 = pl.multiple_of(j * TK, TK)
            head0 = lax.broadcasted_iota(jnp.int32, (LANES, TK), 0) < DH
            for src, dst in ((dkt_scr, dk_ref), (dvt_scr, dv_ref)):
                acc = src[j]
                dst[pl.ds(k0, TK), :] = jnp.where(head0, acc[:, :TK], acc[:, TK:]).T.astype(BF16)
            return 0

        lax.fori_loop(0, nk, finish, 0)

    blk = pl.BlockSpec((seq, LANES), lambda e, p: (e, p))
    grad = jax.ShapeDtypeStruct((t, SB_WIDTH), BF16)
    return pl.pallas_call(
        kern, grid=(ne, SB_WIDTH // LANES),
        in_specs=_qkv_specs(seq) + [blk, pl.BlockSpec((seq, 2 * TK), lambda e, p: (e, p)),
                                    pl.BlockSpec(memory_space=pltpu.SMEM)],
        out_specs=[blk, blk, blk], out_shape=[grad, grad, grad],
        scratch_shapes=[pltpu.VMEM((nk, LANES, 2 * TK), BF16), pltpu.VMEM((nk, LANES, 2 * TK), BF16),
                        pltpu.VMEM((nk, 2 * TK, LANES), BF16), pltpu.VMEM((nk, LANES, 2 * TK), F32),
                        pltpu.VMEM((nk, LANES, 2 * TK), F32)],
        compiler_params=_cp(("parallel", "parallel")), name="attn_bwd",
    )(qkv, qkv, qkv, do, tot, kmin)


def _lru_bwd(xl, h, dy, conv_w, conv_b, wbd, ba, bx, lam, seq, parts=()):
    t = xl.shape[0]
    tc = min(512, seq)
    nc = seq // tc
    nb = tc // SUBLANES

    grid = (t // seq, nc)
    n = len(parts)

    def kern(u_ref, g_ref, up_ref, h_ref, hp_ref, dy_ref, cw_ref, cb_ref, wbd_ref, ba_ref, bx_ref, lam_ref, *rest):
        (dxl_ref, small_ref, dwbd_ref), (lnext_ref, anext_ref, dcnext_ref) = rest[n:n + 3], rest[2 * n + 3:2 * n + 6]
        e, ci = pl.program_id(0), pl.program_id(1)
        first = ci == nc - 1
        grid_first, grid_last = _grid_ends(grid)
        if n:
            @pl.when(grid_first)
            def _():
                for cp in _exchange_copies(rest[:n], rest[n + 3:2 * n + 3], *rest[2 * n + 6:]):
                    cp.start()

        @pl.when((e == 0) & (ci == 0))
        def _():
            small_ref[...] = jnp.zeros_like(small_ref)
            dwbd_ref[...] = jnp.zeros_like(dwbd_ref)

        @pl.when(ci == 0)
        def _():
            lnext_ref[...] = jnp.zeros_like(lnext_ref)
            anext_ref[...] = jnp.zeros_like(anext_ref)
            dcnext_ref[...] = jnp.zeros_like(dcnext_ref)

        u, g = u_ref[...], g_ref[...]
        keep = jnp.where(first, 0.0, 1.0)
        taps = _conv_taps(keep * up_ref[...], u)
        c = cb_ref[...]
        for k in range(CONV_WIDTH):
            c = c + taps[k] * cw_ref[k:k + 1, :]
        lam = lam_ref[...]
        sp = _softplus(-lam)
        r, i, a, mult = _lru_gates(c, wbd_ref, ba_ref[...], bx_ref[...], sp)
        gel, th = _gelu(g)
        dyv, hv = dy_ref[...], h_ref[...]
        dg = dyv * hv * _gelu_grad(g, th)

        aa, bb = _scan_rev(_shift_up(a, 1, anext_ref[0:1, :]), dyv * gel)
        lt = bb + aa * lnext_ref[0:1, :]
        lnext_ref[0:1, :] = _row_of(lt, 0)
        anext_ref[0:1, :] = _row_of(a, 0)

        hprev = _shift_down(hv, 1, keep * hp_ref[SUBLANES - 1:SUBLANES, :])
        da = lt * hprev
        dmult = lt * i * c
        di = lt * mult * c
        dc = lt * mult * i
        dla = da * a - dmult * (a * a) / mult
        dga = dla * ((-LRU_C) * sp) * r * (1.0 - r)
        dgx = di * i * (1.0 - i)
        small_ref[7:8, :] += jnp.sum(dla * r, axis=0, keepdims=True) * (LRU_C * _sigmoid(-lam))
        small_ref[5:6, :] += jnp.sum(dga, axis=0, keepdims=True)
        small_ref[6:7, :] += jnp.sum(dgx, axis=0, keepdims=True)

        dcs = []
        for p in range(LRU_WIDTH // LANES):
            cols = slice(LANES * p, LANES * (p + 1))
            dgax = jnp.concatenate([dga[:, cols], dgx[:, cols]], axis=1).astype(BF16)
            dcs.append(_dot_nt(dgax, wbd_ref[p]))
            dwbd_ref[p] += _dot_tn(c[:, cols].astype(BF16), dgax)
        dc = dc + jnp.concatenate(dcs, axis=1)
        small_ref[4:5, :] += jnp.sum(dc, axis=0, keepdims=True)

        catd = jnp.concatenate([dc, dcnext_ref[...]], axis=0)
        du = dc * cw_ref[CONV_WIDTH - 1:CONV_WIDTH, :]
        for j in range(1, CONV_WIDTH):
            du = du + pltpu.roll(catd, tc + SUBLANES - j, 0)[:tc] * cw_ref[CONV_WIDTH - 1 - j:CONV_WIDTH - j, :]
        dcnext_ref[...] = dc[:SUBLANES]
        for k in range(CONV_WIDTH):
            small_ref[k:k + 1, :] += jnp.sum(dc * taps[k], axis=0, keepdims=True)
        dxl_ref[:, :LRU_WIDTH] = du.astype(BF16)
        dxl_ref[:, LRU_WIDTH:] = dg.astype(BF16)

        if n:
            @pl.when(grid_last)
            def _():
                for cp in _exchange_copies(rest[:n], rest[n + 3:2 * n + 3], *rest[2 * n + 6:]):
                    cp.wait()

    rev = lambda e, c: e * nc + (nc - 1 - c)
    chunk = lambda col: pl.BlockSpec((tc, LRU_WIDTH), lambda e, c: (rev(e, c), col))
    prev8 = pl.BlockSpec((SUBLANES, LRU_WIDTH), lambda e, c: (jnp.maximum(rev(e, c) * nb - 1, 0), 0))
    x_shapes, x_sems = _exchange_shapes(parts) if n else ([], [])
    out = pl.pallas_call(
        kern, grid=grid,
        in_specs=[chunk(0), chunk(1), prev8, chunk(0), prev8, chunk(0)] + _lru_param_specs(2) + [ANY] * n,
        out_specs=[pl.BlockSpec((tc, 2 * LRU_WIDTH), lambda e, c: (rev(e, c), 0)),
                   pl.BlockSpec((SUBLANES, LRU_WIDTH), lambda e, c: (0, 0)),
                   pl.BlockSpec((LRU_WIDTH // LANES, LANES, 2 * LANES), lambda e, c: (0, 0, 0))] + [ANY] * n,
        out_shape=[jax.ShapeDtypeStruct((t, 2 * LRU_WIDTH), BF16), jax.ShapeDtypeStruct((SUBLANES, LRU_WIDTH), F32),
                   jax.ShapeDtypeStruct((LRU_WIDTH // LANES, LANES, 2 * LANES), F32)] + x_shapes,
        scratch_shapes=[pltpu.VMEM((SUBLANES, LRU_WIDTH), F32)] * 3 + x_sems,
        compiler_params=_cp(("arbitrary", "arbitrary")), name="lru_bwd",
    )(xl, xl, xl, h, h, dy, conv_w, conv_b, wbd, ba, bx, lam, *parts)
    return out[0], out[1], out[2], list(out[3:])


def _adamw(w, g, m, v, name):
    rows, cols = w.shape
    tr = 256 if rows % 256 == 0 else rows

    def kern(w_ref, g_ref, m_ref, v_ref, d_ref, m2_ref, v2_ref):
        gv = g_ref[...]
        m2 = ADAM_B1 * m_ref[...] + (1.0 - ADAM_B1) * gv
        v2 = ADAM_B2 * v_ref[...] + (1.0 - ADAM_B2) * (gv * gv)
        m_hat = m2 / (1.0 - ADAM_B1 ** ADAM_STEP)
        v_hat = v2 / (1.0 - ADAM_B2 ** ADAM_STEP)
        d_ref[...] = -ADAM_LR * (m_hat / (jnp.sqrt(v_hat) + ADAM_EPS) + ADAM_WD * w_ref[...])
        m2_ref[...] = m2
        v2_ref[...] = v2

    blk = pl.BlockSpec((tr, cols), lambda i: (i, 0))
    out = jax.ShapeDtypeStruct((rows, cols), F32)
    return pl.pallas_call(kern, grid=(rows // tr,), in_specs=[blk] * 4, out_specs=[blk] * 3, out_shape=[out] * 3,
                          compiler_params=_cp(("parallel",)), name=name)(w, g, m, v)


def _gather_weights(shards):
    n = len(shards)

    def kern(*refs):
        copies = _gather_copies(refs[:n], refs[n:2 * n], *refs[2 * n:])
        for cp in copies:
            cp.start()
        for cp in copies:
            cp.wait()

    out_shape, sems = _gather_shapes(shards)
    return pl.pallas_call(kern, in_specs=[ANY] * n, out_specs=[ANY] * n, out_shape=out_shape, scratch_shapes=sems,
                          name="gather_weights")(*shards)


def _pair_swap(grads):
    n = len(grads)

    def kern(*refs):
        ins, outs, (send_sems, recv_sems) = refs[:n], refs[n:2 * n], refs[2 * n:]
        x, y, c = _place()
        started = []
        for w in range(n):
            half = ins[w].shape[1] // 2
            cp = pltpu.make_async_remote_copy(
                src_ref=ins[w].at[:, pl.ds((1 - c) * half, half), :], dst_ref=outs[w], send_sem=send_sems.at[w],
                recv_sem=recv_sems.at[w], device_id=(x, y, 1 - c), device_id_type=MESH)
            cp.start()
            started.append(cp)
        for cp in started:
            cp.wait()

    return pl.pallas_call(
        kern, in_specs=[ANY] * n, out_specs=[ANY] * n,
        out_shape=[jax.ShapeDtypeStruct((g.shape[0], g.shape[1] // 2, g.shape[2]), g.dtype) for g in grads],
        scratch_shapes=[pltpu.SemaphoreType.DMA((n,)), pltpu.SemaphoreType.DMA((n,))], name="pair_swap",
    )(*grads)


def _pair_add(g, got, core):
    _, rows, cols = g.shape
    half = rows // 2
    tr = min(256, half)
    nt = half // tr

    def kern(c_ref, g_ref, o_ref, out_ref):
        out_ref[...] = (g_ref[...] + o_ref[...]).astype(BF16)

    return pl.pallas_call(
        kern, grid_spec=pltpu.PrefetchScalarGridSpec(
            num_scalar_prefetch=1, grid=(N_CHIPS, nt),
            in_specs=[pl.BlockSpec((None, tr, cols), lambda j, i, c_ref: (j, c_ref[0] * nt + i, 0)),
                      pl.BlockSpec((None, tr, cols), lambda j, i, c_ref: (j, i, 0))],
            out_specs=pl.BlockSpec((None, tr, cols), lambda j, i, c_ref: (j, i, 0))),
        out_shape=jax.ShapeDtypeStruct((N_CHIPS, half, cols), BF16),
        compiler_params=_cp(("parallel", "parallel")), name="pair_add",
    )(core, g, got)


def _chip_exchange(parts):
    n = len(parts)

    def kern(*refs):
        copies = _exchange_copies(refs[:n], refs[n:2 * n], *refs[2 * n:])
        for cp in copies:
            cp.start()
        for cp in copies:
            cp.wait()

    out_shape, sems = _exchange_shapes(parts)
    return pl.pallas_call(kern, in_specs=[ANY] * n, out_specs=[ANY] * n, out_shape=out_shape, scratch_shapes=sems,
                          name="chip_exchange")(*parts)


def _chip_add(part, got, place):
    _, half, cols = part.shape
    tr = min(256, half)
    nt = half // tr

    def kern(p_ref, part_ref, got_ref, out_ref):
        out_ref[...] = (part_ref[...].astype(F32) + got_ref[0].astype(F32) + got_ref[1].astype(F32)
                        + got_ref[2].astype(F32))

    return pl.pallas_call(
        kern, grid_spec=pltpu.PrefetchScalarGridSpec(
            num_scalar_prefetch=1, grid=(nt,),
            in_specs=[pl.BlockSpec((None, tr, cols), lambda i, p_ref: (p_ref[0], i, 0)),
                      pl.BlockSpec((3, tr, cols), lambda i, p_ref: (0, i, 0))],
            out_specs=pl.BlockSpec((tr, cols), lambda i, p_ref: (p_ref[1] * nt + i, 0))),
        out_shape=jax.ShapeDtypeStruct((2 * half, cols), F32),
        compiler_params=_cp(("parallel",)), name="chip_add",
    )(place, part, got)


def _pair_join(fulls):
    n = len(fulls)

    def kern(*refs):
        ins, outs, (send_sems, recv_sems) = refs[:n], refs[n:2 * n], refs[2 * n:]
        x, y, c = _place()
        copies = []
        for w in range(n):
            half = ins[w].shape[0] // 2
            rows = pl.ds(c * half, half)
            copies.append(pltpu.make_async_remote_copy(
                src_ref=ins[w].at[rows, :], dst_ref=outs[w].at[rows, :], send_sem=send_sems.at[w],
                recv_sem=recv_sems.at[w], device_id=(x, y, 1 - c), device_id_type=MESH))
        for cp in copies:
            cp.start()
        for cp in copies:
            cp.wait()

    return pl.pallas_call(
        kern, in_specs=[ANY] * n, out_specs=[ANY] * n, out_shape=[jax.ShapeDtypeStruct(f.shape, f.dtype) for f in fulls],
        input_output_aliases={w: w for w in range(n)},
        scratch_shapes=[pltpu.SemaphoreType.DMA((n,)), pltpu.SemaphoreType.DMA((n,))], name="pair_join",
    )(*fulls)


def _allreduce_small(packed):
    rows = packed.shape[0]

    def kern(in_ref, out_ref, slots, send_sems, recv_sems):
        x, y, c = _place()
        mine = 4 * x + 2 * y + c
        slots[mine] = in_ref[...]
        started = []
        for k in range(1, N_DEV):
            peer = (x ^ (k >> 2), y ^ ((k >> 1) & 1), c ^ (k & 1))
            cp = pltpu.make_async_remote_copy(
                src_ref=in_ref, dst_ref=slots.at[mine], send_sem=send_sems.at[k - 1], recv_sem=recv_sems.at[k - 1],
                device_id=peer, device_id_type=MESH)
            cp.start()
            started.append(cp)
        for cp in started:
            cp.wait()
        acc = slots[0]
        for s in range(1, N_DEV):
            acc = acc + slots[s]
        out_ref[...] = acc

    vm = pl.BlockSpec(memory_space=pltpu.VMEM)
    return pl.pallas_call(
        kern, in_specs=[vm], out_specs=vm, out_shape=jax.ShapeDtypeStruct((rows, LANES), F32),
        scratch_shapes=[pltpu.VMEM((N_DEV, rows, LANES), F32), pltpu.SemaphoreType.DMA((N_DEV - 1,)),
                        pltpu.SemaphoreType.DMA((N_DEV - 1,))],
        name="allreduce_small",
    )(packed)


SMALL = ["norm1_g", "conv_w", "conv_b", "lru_w_a", "lru_b_a", "lru_w_x", "lru_b_x", "lru_lambda", "lru_out_g", "sb_out_g",
         "norm2_g", "final_g"]
BIG = ["w_in", "w_out", "w_up", "w_down"]
WEIGHTS = ["norm1_g", "w_in", "conv_w", "conv_b", "lru_w_a", "lru_b_a", "lru_w_x", "lru_b_x", "lru_lambda", "lru_out_g",
           "sb_out_g", "w_out", "norm2_g", "w_up", "w_down", "final_g"]


def _pack(arrays):
    flat = []
    for a in arrays:
        a = a.reshape(-1).astype(F32)
        flat.append(jnp.pad(a, (0, (-a.shape[0]) % LANES)))
    v = jnp.concatenate(flat)
    v = jnp.pad(v, (0, (-v.shape[0]) % (LANES * SUBLANES)))
    return v.reshape(-1, LANES)


def _unpack(packed, shapes):
    v, out, off = packed.reshape(-1), [], 0
    for shp in shapes:
        size = math.prod(shp)
        out.append(v[off:off + size].reshape(shp))
        off += size + (-size) % LANES
    return out


def _blockdiag_pairs(w):
    w = w.reshape(4, 2, DH, DH)
    z = jnp.zeros((4, DH, DH), w.dtype)
    return jnp.concatenate([jnp.concatenate([w[:, 0], z], axis=2), jnp.concatenate([z, w[:, 1]], axis=2)], axis=1)


def _blockdiag_unpairs(wbd):
    return jnp.stack([wbd[:, :DH, :DH], wbd[:, DH:, DH:]], axis=1).reshape(8, DH, DH)


def _full_cols(g):
    return jnp.transpose(g, (1, 0, 2)).reshape(g.shape[1], N_CHIPS * g.shape[2])


def _local_step(x2, tgt, seq, norm1_g, w_in_f, conv_w_f, conv_b, w_a, b_a, w_x, b_x, lru_lambda, lru_out_g, sb_out_g, rest,
                norm2_g, final_g, place=None):
    wbd = jnp.concatenate([_blockdiag_pairs(w_a), _blockdiag_pairs(w_x)], axis=2).astype(BF16)
    ba, bx = b_a.reshape(1, LRU_WIDTH), b_x.reshape(1, LRU_WIDTH)
    gf = final_g.reshape(1, D_MODEL)

    xl, qkv, xn = _inproj(x2, norm1_g, w_in_f)
    h, y_lru = _lru_fwd(xl, conv_w_f, conv_b, wbd, ba, bx, lru_lambda, seq)
    if place is None:
        o, tot, kmin, _ = _attn_fwd(qkv, seq)
        w_out_f, w_up_f, w_down_f = rest
    else:
        o, tot, kmin, (gw_out, gw_up, gw_down) = _attn_fwd(qkv, seq, rest)
        w_out_f, w_up_f, w_down_f = gw_out.reshape(D_MODEL, D_MODEL), _full_cols(gw_up), gw_down.reshape(D_FF, D_MODEL)
    h1, mix = _outproj(y_lru, o, x2, lru_out_g, sb_out_g, w_out_f)
    h2, hn, up, u2 = _mlp_fwd(h1, norm2_g, w_up_f, w_down_f)
    dh2, loss_part, d_final = _loss_head(h2, tgt, gf)

    dpre = _mlp_bwd_pre(dh2, w_down_f, up)
    g_w_down = _matmul(u2, dh2, "tn", F32, "dw_down", 1024, 1024, 512).reshape(N_CHIPS, D_FF // N_CHIPS, D_MODEL)
    g_w_up = _matmul(hn, dpre, "tn", F32, "dw_up", 1024, D_FF // N_CHIPS, 512, split_cols=True)
    dh1, d_norm2 = _proj_bwd_norm([dpre], w_up_f, h1, norm2_g, dh2, "mlp_bwd_in")
    g_w_out = _matmul(mix, dh1, "tn", F32, "dw_out", 1024, 1024, 512).reshape(N_CHIPS, D_MODEL // N_CHIPS, D_MODEL)
    dy_lru, do, d_ga, d_gb = _outproj_bwd(dh1, w_out_f, y_lru, o, lru_out_g, sb_out_g)
    late = [g_w_out, g_w_up, g_w_down]
    dq, dk, dv = _attn_bwd(qkv, do, tot, kmin, seq)
    if place is None:
        dxl, lru_small, d_wbd, _ = _lru_bwd(xl, h, dy_lru, conv_w_f, conv_b, wbd, ba, bx, lru_lambda, seq)
    else:
        parts = [_pair_add(g, r, place[1:]) for g, r in zip(late, _pair_swap(late))]
        dxl, lru_small, d_wbd, got = _lru_bwd(xl, h, dy_lru, conv_w_f, conv_b, wbd, ba, bx, lru_lambda, seq, parts)
        late = [_chip_add(p, r, place) for p, r in zip(parts, got)]
    dx, d_norm1 = _proj_bwd_norm([dxl, dq, dk, dv], w_in_f, x2, norm1_g, dh1, "inproj_bwd")
    dproj = jnp.concatenate([dxl, dq, dk, dv], axis=1)
    g_w_in = _matmul(xn, dproj, "tn", F32, "dw_in", 1024, IN_COLS // N_CHIPS, 512, split_cols=True)
    small_parts = {
        "norm1_g": d_norm1, "conv_w": lru_small[:CONV_WIDTH], "conv_b": lru_small[4:5],
        "lru_w_a": _blockdiag_unpairs(d_wbd[:, :, :LANES]), "lru_b_a": lru_small[5:6],
        "lru_w_x": _blockdiag_unpairs(d_wbd[:, :, LANES:]), "lru_b_x": lru_small[6:7], "lru_lambda": lru_small[7:8],
        "lru_out_g": d_ga, "sb_out_g": d_gb, "norm2_g": d_norm2, "final_g": d_final,
    }
    return loss_part, dx, [g_w_in] + late, small_parts


def kernel(x, norm1_g, w_in, conv_w, conv_b, lru_w_a, lru_b_a, lru_w_x, lru_b_x, lru_lambda, lru_out_g, sb_out_g, w_out, norm2_g, w_up, w_down, final_g, loss_target, m_norm1_g, m_w_in, m_conv_w, m_conv_b, m_lru_w_a, m_lru_b_a, m_lru_w_x, m_lru_b_x, m_lru_lambda, m_lru_out_g, m_sb_out_g, m_w_out, m_norm2_g, m_w_up, m_w_down, m_final_g, v_norm1_g, v_w_in, v_conv_w, v_conv_b, v_lru_w_a, v_lru_b_a, v_lru_w_x, v_lru_b_x, v_lru_lambda, v_lru_out_g, v_sb_out_g, v_w_out, v_norm2_g, v_w_up, v_w_down, v_final_g):
    given = dict(locals())
    ne, seq, _ = x.shape
    t = ne * seq
    xi, yi, ci = _place()
    place = jnp.stack([2 * xi + yi, ci]).astype(jnp.int32)

    gw_in, gconv = _gather_weights([w_in[0].astype(BF16), conv_w[0]])
    rest = [w_out[0].astype(BF16), w_up[0].astype(BF16), w_down[0].astype(BF16)]
    loss_part, dx, (g_w_in, *late), small_parts = _local_step(
        x.reshape(t, D_MODEL), loss_target.reshape(t, D_MODEL), seq, norm1_g, _full_cols(gw_in), _full_cols(gconv), conv_b,
        lru_w_a[0], lru_b_a, lru_w_x[0], lru_b_x, lru_lambda, lru_out_g, sb_out_g, rest, norm2_g, final_g, place)

    full_shapes = {n: ((CONV_WIDTH, LRU_WIDTH) if n == "conv_w" else given[n].shape) for n in SMALL}
    red = _allreduce_small(_pack([small_parts[n] for n in SMALL] + [loss_part]))
    red_list = _unpack(red, [full_shapes[n] for n in SMALL] + [(1, LANES)])
    grads = dict(zip(SMALL, red_list[:-1]))
    loss = red_list[-1][0, 0]
    grads["conv_w"] = lax.dynamic_slice_in_dim(grads["conv_w"], place[0] * (LRU_WIDTH // N_CHIPS), LRU_WIDTH // N_CHIPS,
                                               axis=1).reshape(conv_w.shape)

    part = _pair_add(g_w_in, _pair_swap([g_w_in])[0], place[1:])
    g_w_in = _chip_add(part, _chip_exchange([part])[0], place)
    for n, full in zip(BIG, _pair_join([g_w_in] + late)):
        grads[n] = full.reshape(given[n].shape)

    delta, new_m, new_v = {}, {}, {}
    for n in BIG:
        shp = given[n].shape
        d, m2, v2 = _adamw(given[n][0], grads[n][0], given["m_" + n][0], given["v_" + n][0], "adamw_" + n)
        delta[n], new_m[n], new_v[n] = d.reshape(shp), m2.reshape(shp), v2.reshape(shp)
    shapes = [given[n].shape for n in SMALL]
    d, m2, v2 = _adamw(_pack([given[n] for n in SMALL]), _pack([grads[n] for n in SMALL]),
                       _pack([given["m_" + n] for n in SMALL]), _pack([given["v_" + n] for n in SMALL]), "adamw_small")
    for n, dd, mm, vv in zip(SMALL, _unpack(d, shapes), _unpack(m2, shapes), _unpack(v2, shapes)):
        delta[n], new_m[n], new_v[n] = dd, mm, vv

    return (loss, dx.reshape(x.shape), *[grads[n] for n in WEIGHTS], *[delta[n] for n in WEIGHTS],
            *[new_m[n] for n in WEIGHTS], *[new_v[n] for n in WEIGHTS])
```

```python
import functools
import math

import jax
import jax.numpy as jnp
from jax import lax
from jax.experimental import pallas as pl
from jax.experimental.pallas import tpu as pltpu

F32, BF16 = jnp.float32, jnp.bfloat16
MESH = pl.DeviceIdType.MESH

D_MODEL = 1024
LRU_WIDTH = 512
SB_WIDTH = 512
DH = 64
IN_COLS = 2 * LRU_WIDTH + 3 * SB_WIDTH
D_FF = 4 * D_MODEL
CONV_WIDTH = 4
LRU_C = 8.0
EPS = 1e-6
N_CHIPS = 4
N_DEV = 8
LANES = 128
SUBLANES = 8
TQ = 512
TK = 128
ATT_SCALE = 1.0 / math.sqrt(DH)
SKIP_LOG = -105.0
VMEM_LIMIT = 52 * 1024 * 1024

ADAM_LR, ADAM_B1, ADAM_B2, ADAM_EPS, ADAM_WD, ADAM_STEP = 0.001, 0.9, 0.999, 1e-08, 0.01, 10

_GELU_K = math.sqrt(2.0 / math.pi)
_GELU_C = 0.044715


def _cp(sem):
    return pltpu.CompilerParams(dimension_semantics=sem, vmem_limit_bytes=VMEM_LIMIT)


def _dot(a, b):
    return jnp.dot(a, b, preferred_element_type=F32)


def _dot_nt(a, b):
    return lax.dot_general(a, b, (((1,), (1,)), ((), ())), preferred_element_type=F32)


def _dot_tn(a, b):
    return lax.dot_general(a, b, (((0,), (0,)), ((), ())), preferred_element_type=F32)


def _rstd(x):
    return lax.rsqrt(jnp.mean(x * x, axis=-1, keepdims=True) + EPS)


def _rms_bwd(x, g, dy):
    r = _rstd(x)
    gd = g * dy
    dx = r * gd - x * (r * r * r) * jnp.mean(x * gd, axis=-1, keepdims=True)
    return dx, jnp.sum(dy * x * r, axis=0, keepdims=True)


def _sigmoid(x):
    return 1.0 / (1.0 + jnp.exp(-x))


def _softplus(x):
    return jnp.maximum(x, 0.0) + jnp.log(1.0 + jnp.exp(-jnp.abs(x)))


def _neg_expm1(x):
    series = -x * (1.0 + x * (0.5 + x * (1.0 / 6.0 + x * (1.0 / 24.0))))
    return jnp.where(x > -0.01, series, 1.0 - jnp.exp(x))


def _gelu(g):
    t = jnp.tanh(_GELU_K * (g + _GELU_C * g * g * g))
    return 0.5 * g * (1.0 + t), t


def _gelu_grad(g, t):
    return 0.5 * (1.0 + t) + 0.5 * g * (1.0 - t * t) * _GELU_K * (1.0 + 3.0 * _GELU_C * g * g)


def _rows(shape):
    return lax.broadcasted_iota(jnp.int32, shape, 0)


def _shift_down(x, s, fill):
    return jnp.where(_rows(x.shape) >= s, pltpu.roll(x, s, 0), fill)


def _shift_up(x, s, fill):
    n = x.shape[0]
    return jnp.where(_rows(x.shape) < n - s, pltpu.roll(x, n - s, 0), fill)


def _row_of(x, idx):
    return jnp.sum(jnp.where(_rows(x.shape) == idx, x, 0.0), axis=0, keepdims=True)


def _matmul(a, b, dims, out_dtype, name, tm, tn, tk, split_cols=False):
    if dims == "nn":
        (m, kk), n, dot = a.shape, b.shape[1], _dot
    elif dims == "nt":
        (m, kk), n, dot = a.shape, b.shape[0], _dot_nt
    else:
        (kk, m), n, dot = a.shape, b.shape[1], _dot_tn
    tm, tn, tk = min(tm, m), min(tn, n), min(tk, kk)
    assert m % tm == 0 and n % tn == 0 and kk % tk == 0, (name, m, n, kk)
    nk = kk // tk

    def kern(a_ref, b_ref, o_ref, acc_ref):
        k = pl.program_id(2)

        @pl.when(k == 0)
        def _():
            acc_ref[...] = jnp.zeros_like(acc_ref)

        acc_ref[...] += dot(a_ref[...].astype(BF16), b_ref[...].astype(BF16))

        @pl.when(k == nk - 1)
        def _():
            o_ref[...] = acc_ref[...].astype(o_ref.dtype)

    if split_cols:
        out_shape = jax.ShapeDtypeStruct((n // tn, m, tn), out_dtype)
        o_spec = pl.BlockSpec((None, tm, tn), lambda i, j, k: (j, i, 0))
    else:
        out_shape = jax.ShapeDtypeStruct((m, n), out_dtype)
        o_spec = pl.BlockSpec((tm, tn), lambda i, j, k: (i, j))
    if dims == "nn":
        a_spec = pl.BlockSpec((tm, tk), lambda i, j, k: (i, k))
        b_spec = pl.BlockSpec((tk, tn), lambda i, j, k: (k, j))
    elif dims == "nt":
        a_spec = pl.BlockSpec((tm, tk), lambda i, j, k: (i, k))
        b_spec = pl.BlockSpec((tn, tk), lambda i, j, k: (j, k))
    else:
        a_spec = pl.BlockSpec((tk, tm), lambda i, j, k: (k, i))
        b_spec = pl.BlockSpec((tk, tn), lambda i, j, k: (k, j))
    return pl.pallas_call(
        kern, grid=(m // tm, n // tn, nk), in_specs=[a_spec, b_spec], out_specs=o_spec, out_shape=out_shape,
        scratch_shapes=[pltpu.VMEM((tm, tn), F32)], compiler_params=_cp(("parallel", "parallel", "arbitrary")), name=name,
    )(a, b)


ANY = pl.BlockSpec(memory_space=pl.ANY)


def _place():
    return lax.axis_index("x"), lax.axis_index("y"), lax.axis_index("c")


def _other_chips(x, y):
    return [(1 - x, y), (x, 1 - y), (1 - x, 1 - y)]


def _gather_copies(ins, outs, send_sems, recv_sems, loc_sems):
    x, y, c = _place()
    mine = 2 * x + y
    copies = []
    for w in range(len(ins)):
        copies.append(pltpu.make_async_copy(ins[w], outs[w].at[mine], loc_sems.at[w]))
        for k, chip in enumerate(_other_chips(x, y)):
            copies.append(pltpu.make_async_remote_copy(
                src_ref=ins[w], dst_ref=outs[w].at[mine], send_sem=send_sems.at[3 * w + k],
                recv_sem=recv_sems.at[3 * w + k], device_id=(*chip, c), device_id_type=MESH))
    return copies


def _gather_shapes(shards):
    return ([jax.ShapeDtypeStruct((N_CHIPS,) + s.shape, s.dtype) for s in shards],
            [pltpu.SemaphoreType.DMA((3 * len(shards),)), pltpu.SemaphoreType.DMA((3 * len(shards),)),
             pltpu.SemaphoreType.DMA((len(shards),))])


def _exchange_copies(ins, outs, send_sems, recv_sems):
    x, y, c = _place()
    copies = []
    for w in range(len(ins)):
        for k, chip in enumerate(_other_chips(x, y)):
            copies.append(pltpu.make_async_remote_copy(
                src_ref=ins[w].at[2 * chip[0] + chip[1]], dst_ref=outs[w].at[k], send_sem=send_sems.at[3 * w + k],
                recv_sem=recv_sems.at[3 * w + k], device_id=(*chip, c), device_id_type=MESH))
    return copies


def _exchange_shapes(parts):
    return ([jax.ShapeDtypeStruct((3,) + p.shape[1:], p.dtype) for p in parts],
            [pltpu.SemaphoreType.DMA((3 * len(parts),)), pltpu.SemaphoreType.DMA((3 * len(parts),))])


def _grid_ends(grid):
    i, j = pl.program_id(0), pl.program_id(1)
    return (i == 0) & (j == 0), (i == grid[0] - 1) & (j == grid[1] - 1)


def _inproj(x, g1, w_in):
    t = x.shape[0]
    tm = min(512, t)

    def kern(x_ref, g_ref, w_ref, xl_ref, qkv_ref, xn_ref):
        xv = x_ref[...]
        xn = (xv * _rstd(xv) * g_ref[...]).astype(BF16)
        xn_ref[...] = xn
        xl_ref[...] = _dot(xn, w_ref[:, : 2 * LRU_WIDTH])
        qkv_ref[...] = _dot(xn, w_ref[:, 2 * LRU_WIDTH:]).astype(BF16)

    row = lambda c: pl.BlockSpec((tm, c), lambda i: (i, 0))
    return pl.pallas_call(
        kern, grid=(t // tm,),
        in_specs=[row(D_MODEL), pl.BlockSpec((1, D_MODEL), lambda i: (0, 0)), pl.BlockSpec((D_MODEL, IN_COLS), lambda i: (0, 0))],
        out_specs=[row(2 * LRU_WIDTH), row(3 * SB_WIDTH), row(D_MODEL)],
        out_shape=[jax.ShapeDtypeStruct((t, 2 * LRU_WIDTH), F32), jax.ShapeDtypeStruct((t, 3 * SB_WIDTH), BF16),
                   jax.ShapeDtypeStruct((t, D_MODEL), BF16)],
        compiler_params=_cp(("parallel",)), name="inproj",
    )(x, g1, w_in)


def _conv_taps(hist, u):
    cat = jnp.concatenate([hist, u], axis=0)
    return [pltpu.roll(cat, CONV_WIDTH - 1 - k, 0)[SUBLANES:] for k in range(CONV_WIDTH - 1)] + [u]


def _lru_gates(c, wbd_ref, ba, bx, sp):
    gas, gxs = [], []
    for p in range(LRU_WIDTH // LANES):
        gax = _dot(c[:, LANES * p: LANES * (p + 1)].astype(BF16), wbd_ref[p])
        gas.append(gax[:, :LANES])
        gxs.append(gax[:, LANES:])
    r = _sigmoid(jnp.concatenate(gas, axis=1) + ba)
    i = _sigmoid(jnp.concatenate(gxs, axis=1) + bx)
    la = (-LRU_C) * r * sp
    a = jnp.exp(la)
    mult = jnp.sqrt(_neg_expm1(2.0 * la))
    return r, i, a, mult


def _scan_fwd(a, b):
    s = 1
    while s < a.shape[0]:
        b = b + a * _shift_down(b, s, 0.0)
        a = a * _shift_down(a, s, 1.0)
        s *= 2
    return a, b


def _scan_rev(a, b):
    s = 1
    while s < a.shape[0]:
        b = b + a * _shift_up(b, s, 0.0)
        a = a * _shift_up(a, s, 1.0)
        s *= 2
    return a, b


def _lru_param_specs(grid_rank):
    z2 = (lambda e, c: (0, 0)) if grid_rank == 2 else None
    return [
        pl.BlockSpec((CONV_WIDTH, LRU_WIDTH), z2), pl.BlockSpec((1, LRU_WIDTH), z2),
        pl.BlockSpec((LRU_WIDTH // LANES, LANES, 2 * LANES), lambda e, c: (0, 0, 0)),
        pl.BlockSpec((1, LRU_WIDTH), z2), pl.BlockSpec((1, LRU_WIDTH), z2), pl.BlockSpec((1, LRU_WIDTH), z2),
    ]


def _lru_fwd(xl, conv_w, conv_b, wbd, ba, bx, lam, seq):
    t = xl.shape[0]
    tc = min(512, seq)
    nc = seq // tc

    def kern(u_ref, g_ref, cw_ref, cb_ref, wbd_ref, ba_ref, bx_ref, lam_ref, h_ref, y_ref, hist_ref, hcar_ref):
        @pl.when(pl.program_id(1) == 0)
        def _():
            hist_ref[...] = jnp.zeros_like(hist_ref)
            hcar_ref[...] = jnp.zeros_like(hcar_ref)

        u = u_ref[...]
        taps = _conv_taps(hist_ref[...], u)
        hist_ref[...] = u_ref[tc - SUBLANES:, :]
        c = cb_ref[...]
        for k in range(CONV_WIDTH):
            c = c + taps[k] * cw_ref[k:k + 1, :]
        sp = _softplus(-lam_ref[...])
        _, i, a, mult = _lru_gates(c, wbd_ref, ba_ref[...], bx_ref[...], sp)
        aa, bb = _scan_fwd(a, mult * i * c)
        h = bb + aa * hcar_ref[0:1, :]
        h_ref[...] = h
        hcar_ref[0:1, :] = h_ref[tc - 1:tc, :]
        y_ref[...] = h * _gelu(g_ref[...])[0]

    chunk = lambda col: pl.BlockSpec((tc, LRU_WIDTH), lambda e, c: (e * nc + c, col))
    out = jax.ShapeDtypeStruct((t, LRU_WIDTH), F32)
    return pl.pallas_call(
        kern, grid=(t // seq, nc), in_specs=[chunk(0), chunk(1)] + _lru_param_specs(2),
        out_specs=[chunk(0), chunk(0)], out_shape=[out, out],
        scratch_shapes=[pltpu.VMEM((SUBLANES, LRU_WIDTH), F32), pltpu.VMEM((SUBLANES, LRU_WIDTH), F32)],
        compiler_params=_cp(("arbitrary", "arbitrary")), name="lru_fwd",
    )(xl, xl, conv_w, conv_b, wbd, ba, bx, lam)


def _att_consts():
    row = lax.broadcasted_iota(jnp.int32, (TQ, 2 * TK), 0)
    key = lax.broadcasted_iota(jnp.int32, (TQ, 2 * TK), 1) & (TK - 1)
    return key < row


def _sum_matrix(kind):
    j = lax.broadcasted_iota(jnp.int32, (2 * TK, 2 * TK), 0) & (TK - 1)
    s = lax.broadcasted_iota(jnp.int32, (2 * TK, 2 * TK), 1)
    pick = {"after": j > s, "upto": j <= s, "before": j < s}[kind]
    return jnp.where((s >= TK) | pick, 1.0, 0.0).astype(BF16)


def _hi_lo(x):
    hi = x.astype(BF16)
    return hi, (x - hi.astype(F32)).astype(BF16)


def _pair_sums(x, m):
    hi, lo = _hi_lo(x)
    out = []
    for hd in range(2):
        cols = slice(hd * TK, (hd + 1) * TK)
        out.append(_dot(jnp.concatenate([hi[:, cols], lo[:, cols]], axis=1), m))
    return [o[:, :TK] for o in out], [o[:, TK:] for o in out]


def _att_logits(qb, kbd):
    z = _dot(qb, kbd)
    lg = jnp.log(1.0 + jnp.exp(-jnp.abs(z)))
    lb = jnp.minimum(z, 0.0) - lg
    return lb, lb - z


def _head_diag(x, rows_first):
    n = x.shape[0] if rows_first else x.shape[1]
    idx = lax.broadcasted_iota(jnp.int32, x.shape, 0 if rows_first else 1)
    return jnp.where(idx < n // 2, x, 0), jnp.where(idx >= n // 2, x, 0)


def _scaled_q(q_ref, q0):
    return (q_ref[pl.ds(q0, TQ), :].astype(F32) * ATT_SCALE).astype(BF16)


def _qkv_specs(seq):
    n = SB_WIDTH // LANES
    return [pl.BlockSpec((seq, LANES), lambda e, p, off=off: (e, off * n + p)) for off in range(3)]


def _attn_fwd(qkv, seq, shards=()):
    t = qkv.shape[0]
    ne, nq, nk = t // seq, seq // TQ, seq // TK
    grid = (ne, SB_WIDTH // LANES)
    n = len(shards)

    def kern(q_ref, k_ref, v_ref, *rest):
        (o_ref, tot_ref, kmin_ref), (kbd_scr, vbd_scr) = rest[n:n + 3], rest[2 * n + 3:2 * n + 5]
        first, last = _grid_ends(grid)
        if n:
            @pl.when(first)
            def _():
                for cp in _gather_copies(rest[:n], rest[n + 3:2 * n + 3], *rest[2 * n + 5:]):
                    cp.start()

        causal = _att_consts()
        after = _sum_matrix("after")

        def prep(j, _):
            k0 = pl.multiple_of(j * TK, TK)
            top, bot = _head_diag(k_ref[pl.ds(k0, TK), :].astype(F32).T, True)
            kbd_scr[j] = jnp.concatenate([top, bot], axis=1).astype(BF16)
            left, right = _head_diag(v_ref[pl.ds(k0, TK), :], False)
            vbd_scr[j] = jnp.concatenate([left, right], axis=0)
            return 0

        lax.fori_loop(0, nk, prep, 0)

        def block(j, qb, st, mask):
            c0, c1, oacc = st
            lb, l1 = _att_logits(qb, kbd_scr[j])
            if mask is not None:
                l1 = jnp.where(mask, l1, 0.0)
            (s0, s1), (r0, r1) = _pair_sums(l1, after)
            att = jnp.exp(lb + jnp.concatenate([s0 + c0, s1 + c1], axis=1))
            if mask is not None:
                att = jnp.where(mask, att, 0.0)
            return c0 + r0, c1 + r1, oacc + _dot(att.astype(BF16), vbd_scr[j])

        def qloop(qi, _):
            q0 = pl.multiple_of(qi * TQ, TQ)
            qb = _scaled_q(q_ref, q0)
            zero = jnp.zeros((TQ, TK), F32)
            st = (zero, zero, jnp.zeros((TQ, LANES), F32))
            for jj in reversed(range(TQ // TK)):
                lo = TK * jj
                new = block((TQ // TK) * qi + jj, qb[lo:], tuple(x[lo:] for x in st), causal[:TQ - lo])
                st = tuple(jnp.concatenate([x[:lo], y], axis=0) if lo else y for x, y in zip(st, new))

            npair = (TQ // TK // 2) * qi

            def more(its):
                return (its[0] < npair) & (jnp.max(jnp.maximum(its[1], its[2])) > SKIP_LOG)

            def kloop(its):
                j = 2 * (npair - its[0]) - 1
                return (its[0] + 1,) + block(j - 1, qb, block(j, qb, its[1:], None), None)

            done, c0, c1, oacc = lax.while_loop(more, kloop, (jnp.int32(0),) + st)
            o_ref[pl.ds(q0, TQ), :] = oacc
            tot_ref[pl.ds(q0, TQ), :] = jnp.concatenate([c0, c1], axis=1)
            kmin_ref[pl.program_id(0), pl.program_id(1), qi] = 2 * (npair - done)
            return 0

        lax.fori_loop(0, nq, qloop, 0)

        if n:
            @pl.when(last)
            def _():
                for cp in _gather_copies(rest[:n], rest[n + 3:2 * n + 3], *rest[2 * n + 5:]):
                    cp.wait()

    g_shapes, g_sems = _gather_shapes(shards) if n else ([], [])
    out = pl.pallas_call(
        kern, grid=grid, in_specs=_qkv_specs(seq) + [ANY] * n,
        out_specs=[pl.BlockSpec((seq, LANES), lambda e, p: (e, p)), pl.BlockSpec((seq, 2 * TK), lambda e, p: (e, p)),
                   pl.BlockSpec(memory_space=pltpu.SMEM)] + [ANY] * n,
        out_shape=[jax.ShapeDtypeStruct((t, SB_WIDTH), F32), jax.ShapeDtypeStruct((t, 2 * TK * SB_WIDTH // LANES), F32),
                   jax.ShapeDtypeStruct((ne, SB_WIDTH // LANES, nq), jnp.int32)] + g_shapes,
        scratch_shapes=[pltpu.VMEM((nk, LANES, 2 * TK), BF16), pltpu.VMEM((nk, 2 * TK, LANES), BF16)] + g_sems,
        compiler_params=_cp(("arbitrary", "arbitrary")), name="attn_fwd",
    )(qkv, qkv, qkv, *shards)
    return out[0], out[1], out[2], list(out[3:])


def _outproj(y_lru, o, x, ga, gb, w_out):
    t = x.shape[0]
    tm = min(512, t)

    def kern(y_ref, o_ref, x_ref, ga_ref, gb_ref, w_ref, h1_ref, mix_ref):
        yv, ov = y_ref[...], o_ref[...]
        mix = jnp.concatenate([yv * _rstd(yv) * ga_ref[...], ov * _rstd(ov) * gb_ref[...]], axis=1).astype(BF16)
        mix_ref[...] = mix
        h1_ref[...] = x_ref[...] + _dot(mix, w_ref[...])

    row = lambda c: pl.BlockSpec((tm, c), lambda i: (i, 0))
    vec = lambda c: pl.BlockSpec((1, c), lambda i: (0, 0))
    return pl.pallas_call(
        kern, grid=(t // tm,),
        in_specs=[row(LRU_WIDTH), row(SB_WIDTH), row(D_MODEL), vec(LRU_WIDTH), vec(SB_WIDTH),
                  pl.BlockSpec((D_MODEL, D_MODEL), lambda i: (0, 0))],
        out_specs=[row(D_MODEL), row(D_MODEL)],
        out_shape=[jax.ShapeDtypeStruct((t, D_MODEL), F32), jax.ShapeDtypeStruct((t, D_MODEL), BF16)],
        compiler_params=_cp(("parallel",)), name="outproj",
    )(y_lru, o, x, ga, gb, w_out)


def _resident(shape):
    return pl.BlockSpec(shape, lambda *_: (0,) * len(shape), pipeline_mode=pl.Buffered(1))


def _mlp_fwd(h1, g2, w_up, w_down):
    t = h1.shape[0]
    tm, tf = min(512, t), 1024

    def kern(h1_ref, g_ref, wu_ref, wd_ref, h2_ref, hn_ref, up_ref, u2_ref):
        hv = h1_ref[...]
        hn = (hv * _rstd(hv) * g_ref[...]).astype(BF16)
        hn_ref[...] = hn
        acc = hv
        for f in range(D_FF // tf):
            cols = slice(f * tf, (f + 1) * tf)
            up = jnp.maximum(_dot(hn, wu_ref[:, cols]), 0.0)
            u2 = (up * up).astype(BF16)
            up_ref[:, cols] = up.astype(BF16)
            u2_ref[:, cols] = u2
            acc = acc + _dot(u2, wd_ref[cols, :])
        h2_ref[...] = acc

    row = lambda c: pl.BlockSpec((tm, c), lambda i: (i, 0))
    return pl.pallas_call(
        kern, grid=(t // tm,),
        in_specs=[row(D_MODEL), pl.BlockSpec((1, D_MODEL), lambda i: (0, 0)), _resident((D_MODEL, D_FF)),
                  _resident((D_FF, D_MODEL))],
        out_specs=[row(D_MODEL), row(D_MODEL), row(D_FF), row(D_FF)],
        out_shape=[jax.ShapeDtypeStruct((t, D_MODEL), F32), jax.ShapeDtypeStruct((t, D_MODEL), BF16),
                   jax.ShapeDtypeStruct((t, D_FF), BF16), jax.ShapeDtypeStruct((t, D_FF), BF16)],
        compiler_params=_cp(("parallel",)), name="mlp_fwd",
    )(h1, g2, w_up, w_down)


def _loss_head(h2, target, gf):
    t = h2.shape[0]
    tm = min(512, t)

    def kern(h_ref, t_ref, g_ref, dh_ref, dhb_ref, loss_ref, dg_ref):
        @pl.when(pl.program_id(0) == 0)
        def _():
            loss_ref[...] = jnp.zeros_like(loss_ref)
            dg_ref[...] = jnp.zeros_like(dg_ref)

        hv, g = h_ref[...], g_ref[...]
        err = hv * _rstd(hv) * g - t_ref[...]
        lane = lax.broadcasted_iota(jnp.int32, (1, LANES), 1)
        loss_ref[...] += jnp.where(lane == 0, 0.5 * jnp.sum(err * err) / D_MODEL, 0.0)
        dx, dg = _rms_bwd(hv, g, err * (1.0 / D_MODEL))
        dh_ref[...] = dx
        dhb_ref[...] = dx.astype(BF16)
        dg_ref[...] += dg

    row = pl.BlockSpec((tm, D_MODEL), lambda i: (i, 0))
    vec = pl.BlockSpec((1, D_MODEL), lambda i: (0, 0))
    return pl.pallas_call(
        kern, grid=(t // tm,), in_specs=[row, row, vec],
        out_specs=[row, row, pl.BlockSpec((1, LANES), lambda i: (0, 0)), vec],
        out_shape=[jax.ShapeDtypeStruct((t, D_MODEL), F32), jax.ShapeDtypeStruct((t, D_MODEL), BF16),
                   jax.ShapeDtypeStruct((1, LANES), F32), jax.ShapeDtypeStruct((1, D_MODEL), F32)],
        compiler_params=_cp(("arbitrary",)), name="loss_head",
    )(h2, target, gf)


def _mlp_bwd_pre(dh2, w_down, up):
    t = dh2.shape[0]
    tm, tf = min(512, t), 1024

    def kern(d_ref, w_ref, up_ref, o_ref):
        dv = d_ref[...]
        for f in range(D_FF // tf):
            cols = slice(f * tf, (f + 1) * tf)
            o_ref[:, cols] = (_dot_nt(dv, w_ref[cols, :]) * (2.0 * up_ref[:, cols].astype(F32))).astype(BF16)

    row = lambda c: pl.BlockSpec((tm, c), lambda i: (i, 0))
    return pl.pallas_call(
        kern, grid=(t // tm,), in_specs=[row(D_MODEL), _resident((D_FF, D_MODEL)), row(D_FF)],
        out_specs=row(D_FF), out_shape=jax.ShapeDtypeStruct((t, D_FF), BF16),
        compiler_params=_cp(("parallel",)), name="mlp_bwd_pre",
    )(dh2, w_down, up)


def _proj_bwd_norm(dys, w, x, g, resid, name):
    t = x.shape[0]
    tm = min(512, t)
    widths = [dy.shape[1] for dy in dys]
    n = len(dys)

    def kern(*refs):
        dy_refs, (w_ref, x_ref, g_ref, r_ref, dx_ref, dg_ref) = refs[:n], refs[n:]

        @pl.when(pl.program_id(0) == 0)
        def _():
            dg_ref[...] = jnp.zeros_like(dg_ref)

        off, dxn = 0, None
        for dy_ref, wd in zip(dy_refs, widths):
            part = _dot_nt(dy_ref[...], w_ref[:, off:off + wd])
            dxn = part if dxn is None else dxn + part
            off += wd
        dx, dg = _rms_bwd(x_ref[...], g_ref[...], dxn)
        dx_ref[...] = r_ref[...] + dx
        dg_ref[...] += dg

    row = lambda c: pl.BlockSpec((tm, c), lambda i: (i, 0))
    vec = pl.BlockSpec((1, D_MODEL), lambda i: (0, 0))
    return pl.pallas_call(
        kern, grid=(t // tm,),
        in_specs=[row(wd) for wd in widths] + [pl.BlockSpec(w.shape, lambda i: (0, 0)), row(D_MODEL), vec, row(D_MODEL)],
        out_specs=[row(D_MODEL), vec],
        out_shape=[jax.ShapeDtypeStruct((t, D_MODEL), F32), jax.ShapeDtypeStruct((1, D_MODEL), F32)],
        compiler_params=_cp(("arbitrary",)), name=name,
    )(*dys, w, x, g, resid)


def _outproj_bwd(dh1, w_out, y_lru, o, ga, gb):
    t = dh1.shape[0]
    tm = min(512, t)

    def kern(d_ref, w_ref, y_ref, o_ref, ga_ref, gb_ref, dy_ref, do_ref, dga_ref, dgb_ref):
        @pl.when(pl.program_id(0) == 0)
        def _():
            dga_ref[...] = jnp.zeros_like(dga_ref)
            dgb_ref[...] = jnp.zeros_like(dgb_ref)

        dmix = _dot_nt(d_ref[...].astype(BF16), w_ref[...])
        dy, dga = _rms_bwd(y_ref[...], ga_ref[...], dmix[:, :LRU_WIDTH])
        do, dgb = _rms_bwd(o_ref[...], gb_ref[...], dmix[:, LRU_WIDTH:])
        dy_ref[...] = dy
        do_ref[...] = do
        dga_ref[...] += dga
        dgb_ref[...] += dgb

    row = lambda c: pl.BlockSpec((tm, c), lambda i: (i, 0))
    vec = pl.BlockSpec((1, LRU_WIDTH), lambda i: (0, 0))
    half = jax.ShapeDtypeStruct((t, LRU_WIDTH), F32)
    gsum = jax.ShapeDtypeStruct((1, LRU_WIDTH), F32)
    return pl.pallas_call(
        kern, grid=(t // tm,),
        in_specs=[row(D_MODEL), pl.BlockSpec((D_MODEL, D_MODEL), lambda i: (0, 0)), row(LRU_WIDTH), row(SB_WIDTH), vec, vec],
        out_specs=[row(LRU_WIDTH), row(SB_WIDTH), vec, vec], out_shape=[half, half, gsum, gsum],
        compiler_params=_cp(("arbitrary",)), name="outproj_bwd",
    )(dh1, w_out, y_lru, o, ga, gb)


def _attn_bwd(qkv, do, tot, kmin, seq):
    t = qkv.shape[0]
    ne, nq, nk = t // seq, seq // TQ, seq // TK

    def kern(q_ref, k_ref, v_ref, do_ref, tot_ref, kmin_ref, dq_ref, dk_ref, dv_ref,
             kbd_scr, vtbd_scr, kbd2_scr, dkt_scr, dvt_scr):
        causal = _att_consts()
        upto, before = _sum_matrix("upto"), _sum_matrix("before")

        def prep(j, _):
            k0 = pl.multiple_of(j * TK, TK)
            kb = k_ref[pl.ds(k0, TK), :]
            top, bot = _head_diag(kb.astype(F32).T, True)
            kbd_scr[j] = jnp.concatenate([top, bot], axis=1).astype(BF16)
            top, bot = _head_diag(v_ref[pl.ds(k0, TK), :].astype(F32).T, True)
            vtbd_scr[j] = jnp.concatenate([top, bot], axis=1).astype(BF16)
            left, right = _head_diag(kb, False)
            kbd2_scr[j] = jnp.concatenate([left, right], axis=0)
            dkt_scr[j] = jnp.zeros((LANES, 2 * TK), F32)
            dvt_scr[j] = jnp.zeros((LANES, 2 * TK), F32)
            return 0

        lax.fori_loop(0, nk, prep, 0)

        def block(j, qb, qt, dob, dot_, totb, st, mask):
            f0, f1, p0, p1, dqacc = st
            lb, l1 = _att_logits(qb, kbd_scr[j])
            if mask is not None:
                l1 = jnp.where(mask, l1, 0.0)
            (s0, s1), (r0, r1) = _pair_sums(l1, upto)
            att = jnp.exp(lb + (totb - jnp.concatenate([s0 + f0, s1 + f1], axis=1)))
            if mask is not None:
                att = jnp.where(mask, att, 0.0)
            pw = att * _dot(dob, vtbd_scr[j])
            (e0, e1), (t0, t1) = _pair_sums(pw, before)
            dz = pw - jnp.exp(lb) * (pw + jnp.concatenate([e0 + p0, e1 + p1], axis=1))
            if mask is not None:
                dz = jnp.where(mask, dz, 0.0)
            dzb = dz.astype(BF16)
            dkt_scr[j] += _dot(qt, dzb)
            dvt_scr[j] += _dot(dot_, att.astype(BF16))
            return f0 + r0, f1 + r1, p0 + t0, p1 + t1, dqacc + _dot(dzb, kbd2_scr[j])

        def qloop(qi, _):
            q0 = pl.multiple_of(qi * TQ, TQ)
            qb = _scaled_q(q_ref, q0)
            qt = qb.astype(F32).T.astype(BF16)
            do32 = do_ref[pl.ds(q0, TQ), :]
            dob, dot_ = do32.astype(BF16), do32.T.astype(BF16)
            totb = tot_ref[pl.ds(q0, TQ), :]
            zero = jnp.zeros((TQ, TK), F32)
            st = (zero, zero, zero, zero, jnp.zeros((TQ, LANES), F32))

            k0 = kmin_ref[pl.program_id(0), pl.program_id(1), qi]

            def kloop(it, st):
                j = k0 + 2 * it
                return block(j + 1, qb, qt, dob, dot_, totb, block(j, qb, qt, dob, dot_, totb, st, None), None)

            st = lax.fori_loop(0, ((TQ // TK) * qi - k0) // 2, kloop, st)
            for jj in range(TQ // TK):
                lo = TK * jj
                new = block((TQ // TK) * qi + jj, qb[lo:], qt[:, lo:], dob[lo:], dot_[:, lo:], totb[lo:],
                            tuple(x[lo:] for x in st), causal[:TQ - lo])
                st = tuple(jnp.concatenate([x[:lo], y], axis=0) if lo else y for x, y in zip(st, new))
            dq_ref[pl.ds(q0, TQ), :] = (st[4] * ATT_SCALE).astype(BF16)
            return 0

        lax.fori_loop(0, nq, qloop, 0)

        def finish(j, _):
            k0 = pl.multiple_of(j * TK, TK)
            head0 = lax.broadcasted_iota(jnp.int32, (LANES, TK), 0) < DH
            for src, dst in ((dkt_scr, dk_ref), (dvt_scr, dv_ref)):
                acc = src[j]
                dst[pl.ds(k0, TK), :] = jnp.where(head0, acc[:, :TK], acc[:, TK:]).T.astype(BF16)
            return 0

        lax.fori_loop(0, nk, finish, 0)

    blk = pl.BlockSpec((seq, LANES), lambda e, p: (e, p))
    grad = jax.ShapeDtypeStruct((t, SB_WIDTH), BF16)
    return pl.pallas_call(
        kern, grid=(ne, SB_WIDTH // LANES),
        in_specs=_qkv_specs(seq) + [blk, pl.BlockSpec((seq, 2 * TK), lambda e, p: (e, p)),
                                    pl.BlockSpec(memory_space=pltpu.SMEM)],
        out_specs=[blk, blk, blk], out_shape=[grad, grad, grad],
        scratch_shapes=[pltpu.VMEM((nk, LANES, 2 * TK), BF16), pltpu.VMEM((nk, LANES, 2 * TK), BF16),
                        pltpu.VMEM((nk, 2 * TK, LANES), BF16), pltpu.VMEM((nk, LANES, 2 * TK), F32),
                        pltpu.VMEM((nk, LANES, 2 * TK), F32)],
        compiler_params=_cp(("parallel", "parallel")), name="attn_bwd",
    )(qkv, qkv, qkv, do, tot, kmin)


def _lru_bwd(xl, h, dy, conv_w, conv_b, wbd, ba, bx, lam, seq, parts=()):
    t = xl.shape[0]
    tc = min(512, seq)
    nc = seq // tc
    nb = tc // SUBLANES

    grid = (t // seq, nc)
    n = len(parts)

    def kern(u_ref, g_ref, up_ref, h_ref, hp_ref, dy_ref, cw_ref, cb_ref, wbd_ref, ba_ref, bx_ref, lam_ref, *rest):
        (dxl_ref, small_ref, dwbd_ref), (lnext_ref, anext_ref, dcnext_ref) = rest[n:n + 3], rest[2 * n + 3:2 * n + 6]
        e, ci = pl.program_id(0), pl.program_id(1)
        first = ci == nc - 1
        grid_first, grid_last = _grid_ends(grid)
        if n:
            @pl.when(grid_first)
            def _():
                for cp in _exchange_copies(rest[:n], rest[n + 3:2 * n + 3], *rest[2 * n + 6:]):
                    cp.start()

        @pl.when((e == 0) & (ci == 0))
        def _():
            small_ref[...] = jnp.zeros_like(small_ref)
            dwbd_ref[...] = jnp.zeros_like(dwbd_ref)

        @pl.when(ci == 0)
        def _():
            lnext_ref[...] = jnp.zeros_like(lnext_ref)
            anext_ref[...] = jnp.zeros_like(anext_ref)
            dcnext_ref[...] = jnp.zeros_like(dcnext_ref)

        u, g = u_ref[...], g_ref[...]
        keep = jnp.where(first, 0.0, 1.0)
        taps = _conv_taps(keep * up_ref[...], u)
        c = cb_ref[...]
        for k in range(CONV_WIDTH):
            c = c + taps[k] * cw_ref[k:k + 1, :]
        lam = lam_ref[...]
        sp = _softplus(-lam)
        r, i, a, mult = _lru_gates(c, wbd_ref, ba_ref[...], bx_ref[...], sp)
        gel, th = _gelu(g)
        dyv, hv = dy_ref[...], h_ref[...]
        dg = dyv * hv * _gelu_grad(g, th)

        aa, bb = _scan_rev(_shift_up(a, 1, anext_ref[0:1, :]), dyv * gel)
        lt = bb + aa * lnext_ref[0:1, :]
        lnext_ref[0:1, :] = _row_of(lt, 0)
        anext_ref[0:1, :] = _row_of(a, 0)

        hprev = _shift_down(hv, 1, keep * hp_ref[SUBLANES - 1:SUBLANES, :])
        da = lt * hprev
        dmult = lt * i * c
        di = lt * mult * c
        dc = lt * mult * i
        dla = da * a - dmult * (a * a) / mult
        dga = dla * ((-LRU_C) * sp) * r * (1.0 - r)
        dgx = di * i * (1.0 - i)
        small_ref[7:8, :] += jnp.sum(dla * r, axis=0, keepdims=True) * (LRU_C * _sigmoid(-lam))
        small_ref[5:6, :] += jnp.sum(dga, axis=0, keepdims=True)
        small_ref[6:7, :] += jnp.sum(dgx, axis=0, keepdims=True)

        dcs = []
        for p in range(LRU_WIDTH // LANES):
            cols = slice(LANES * p, LANES * (p + 1))
            dgax = jnp.concatenate([dga[:, cols], dgx[:, cols]], axis=1).astype(BF16)
            dcs.append(_dot_nt(dgax, wbd_ref[p]))
            dwbd_ref[p] += _dot_tn(c[:, cols].astype(BF16), dgax)
        dc = dc + jnp.concatenate(dcs, axis=1)
        small_ref[4:5, :] += jnp.sum(dc, axis=0, keepdims=True)

        catd = jnp.concatenate([dc, dcnext_ref[...]], axis=0)
        du = dc * cw_ref[CONV_WIDTH - 1:CONV_WIDTH, :]
        for j in range(1, CONV_WIDTH):
            du = du + pltpu.roll(catd, tc + SUBLANES - j, 0)[:tc] * cw_ref[CONV_WIDTH - 1 - j:CONV_WIDTH - j, :]
        dcnext_ref[...] = dc[:SUBLANES]
        for k in range(CONV_WIDTH):
            small_ref[k:k + 1, :] += jnp.sum(dc * taps[k], axis=0, keepdims=True)
        dxl_ref[:, :LRU_WIDTH] = du.astype(BF16)
        dxl_ref[:, LRU_WIDTH:] = dg.astype(BF16)

        if n:
            @pl.when(grid_last)
            def _():
                for cp in _exchange_copies(rest[:n], rest[n + 3:2 * n + 3], *rest[2 * n + 6:]):
                    cp.wait()

    rev = lambda e, c: e * nc + (nc - 1 - c)
    chunk = lambda col: pl.BlockSpec((tc, LRU_WIDTH), lambda e, c: (rev(e, c), col))
    prev8 = pl.BlockSpec((SUBLANES, LRU_WIDTH), lambda e, c: (jnp.maximum(rev(e, c) * nb - 1, 0), 0))
    x_shapes, x_sems = _exchange_shapes(parts) if n else ([], [])
    out = pl.pallas_call(
        kern, grid=grid,
        in_specs=[chunk(0), chunk(1), prev8, chunk(0), prev8, chunk(0)] + _lru_param_specs(2) + [ANY] * n,
        out_specs=[pl.BlockSpec((tc, 2 * LRU_WIDTH), lambda e, c: (rev(e, c), 0)),
                   pl.BlockSpec((SUBLANES, LRU_WIDTH), lambda e, c: (0, 0)),
                   pl.BlockSpec((LRU_WIDTH // LANES, LANES, 2 * LANES), lambda e, c: (0, 0, 0))] + [ANY] * n,
        out_shape=[jax.ShapeDtypeStruct((t, 2 * LRU_WIDTH), BF16), jax.ShapeDtypeStruct((SUBLANES, LRU_WIDTH), F32),
                   jax.ShapeDtypeStruct((LRU_WIDTH // LANES, LANES, 2 * LANES), F32)] + x_shapes,
        scratch_shapes=[pltpu.VMEM((SUBLANES, LRU_WIDTH), F32)] * 3 + x_sems,
        compiler_params=_cp(("arbitrary", "arbitrary")), name="lru_bwd",
    )(xl, xl, xl, h, h, dy, conv_w, conv_b, wbd, ba, bx, lam, *parts)
    return out[0], out[1], out[2], list(out[3:])


def _adamw(w, g, m, v, name):
    rows, cols = w.shape
    tr = 256 if rows % 256 == 0 else rows

    def kern(w_ref, g_ref, m_ref, v_ref, d_ref, m2_ref, v2_ref):
        gv = g_ref[...]
        m2 = ADAM_B1 * m_ref[...] + (1.0 - ADAM_B1) * gv
        v2 = ADAM_B2 * v_ref[...] + (1.0 - ADAM_B2) * (gv * gv)
        m_hat = m2 / (1.0 - ADAM_B1 ** ADAM_STEP)
        v_hat = v2 / (1.0 - ADAM_B2 ** ADAM_STEP)
        d_ref[...] = -ADAM_LR * (m_hat / (jnp.sqrt(v_hat) + ADAM_EPS) + ADAM_WD * w_ref[...])
        m2_ref[...] = m2
        v2_ref[...] = v2

    blk = pl.BlockSpec((tr, cols), lambda i: (i, 0))
    out = jax.ShapeDtypeStruct((rows, cols), F32)
    return pl.pallas_call(kern, grid=(rows // tr,), in_specs=[blk] * 4, out_specs=[blk] * 3, out_shape=[out] * 3,
                          compiler_params=_cp(("parallel",)), name=name)(w, g, m, v)


def _gather_weights(shards):
    n = len(shards)

    def kern(*refs):
        copies = _gather_copies(refs[:n], refs[n:2 * n], *refs[2 * n:])
        for cp in copies:
            cp.start()
        for cp in copies:
            cp.wait()

    out_shape, sems = _gather_shapes(shards)
    return pl.pallas_call(kern, in_specs=[ANY] * n, out_specs=[ANY] * n, out_shape=out_shape, scratch_shapes=sems,
                          name="gather_weights")(*shards)


def _pair_swap(grads):
    n = len(grads)

    def kern(*refs):
        ins, outs, (send_sems, recv_sems) = refs[:n], refs[n:2 * n], refs[2 * n:]
        x, y, c = _place()
        started = []
        for w in range(n):
            half = ins[w].shape[1] // 2
            cp = pltpu.make_async_remote_copy(
                src_ref=ins[w].at[:, pl.ds((1 - c) * half, half), :], dst_ref=outs[w], send_sem=send_sems.at[w],
                recv_sem=recv_sems.at[w], device_id=(x, y, 1 - c), device_id_type=MESH)
            cp.start()
            started.append(cp)
        for cp in started:
            cp.wait()

    return pl.pallas_call(
        kern, in_specs=[ANY] * n, out_specs=[ANY] * n,
        out_shape=[jax.ShapeDtypeStruct((g.shape[0], g.shape[1] // 2, g.shape[2]), g.dtype) for g in grads],
        scratch_shapes=[pltpu.SemaphoreType.DMA((n,)), pltpu.SemaphoreType.DMA((n,))], name="pair_swap",
    )(*grads)


def _pair_add(g, got, core):
    _, rows, cols = g.shape
    half = rows // 2
    tr = min(256, half)
    nt = half // tr

    def kern(c_ref, g_ref, o_ref, out_ref):
        out_ref[...] = (g_ref[...] + o_ref[...]).astype(BF16)

    return pl.pallas_call(
        kern, grid_spec=pltpu.PrefetchScalarGridSpec(
            num_scalar_prefetch=1, grid=(N_CHIPS, nt),
            in_specs=[pl.BlockSpec((None, tr, cols), lambda j, i, c_ref: (j, c_ref[0] * nt + i, 0)),
                      pl.BlockSpec((None, tr, cols), lambda j, i, c_ref: (j, i, 0))],
            out_specs=pl.BlockSpec((None, tr, cols), lambda j, i, c_ref: (j, i, 0))),
        out_shape=jax.ShapeDtypeStruct((N_CHIPS, half, cols), BF16),
        compiler_params=_cp(("parallel", "parallel")), name="pair_add",
    )(core, g, got)


def _chip_exchange(parts):
    n = len(parts)

    def kern(*refs):
        copies = _exchange_copies(refs[:n], refs[n:2 * n], *refs[2 * n:])
        for cp in copies:
            cp.start()
        for cp in copies:
            cp.wait()

    out_shape, sems = _exchange_shapes(parts)
    return pl.pallas_call(kern, in_specs=[ANY] * n, out_specs=[ANY] * n, out_shape=out_shape, scratch_shapes=sems,
                          name="chip_exchange")(*parts)


def _chip_add(part, got, place):
    _, half, cols = part.shape
    tr = min(256, half)
    nt = half // tr

    def kern(p_ref, part_ref, got_ref, out_ref):
        out_ref[...] = (part_ref[...].astype(F32) + got_ref[0].astype(F32) + got_ref[1].astype(F32)
                        + got_ref[2].astype(F32))

    return pl.pallas_call(
        kern, grid_spec=pltpu.PrefetchScalarGridSpec(
            num_scalar_prefetch=1, grid=(nt,),
            in_specs=[pl.BlockSpec((None, tr, cols), lambda i, p_ref: (p_ref[0], i, 0)),
                      pl.BlockSpec((3, tr, cols), lambda i, p_ref: (0, i, 0))],
            out_specs=pl.BlockSpec((tr, cols), lambda i, p_ref: (p_ref[1] * nt + i, 0))),
        out_shape=jax.ShapeDtypeStruct((2 * half, cols), F32),
        compiler_params=_cp(("parallel",)), name="chip_add",
    )(place, part, got)


def _pair_join(fulls):
    n = len(fulls)

    def kern(*refs):
        ins, outs, (send_sems, recv_sems) = refs[:n], refs[n:2 * n], refs[2 * n:]
        x, y, c = _place()
        copies = []
        for w in range(n):
            half = ins[w].shape[0] // 2
            rows = pl.ds(c * half, half)
            copies.append(pltpu.make_async_remote_copy(
                src_ref=ins[w].at[rows, :], dst_ref=outs[w].at[rows, :], send_sem=send_sems.at[w],
                recv_sem=recv_sems.at[w], device_id=(x, y, 1 - c), device_id_type=MESH))
        for cp in copies:
            cp.start()
        for cp in copies:
            cp.wait()

    return pl.pallas_call(
        kern, in_specs=[ANY] * n, out_specs=[ANY] * n, out_shape=[jax.ShapeDtypeStruct(f.shape, f.dtype) for f in fulls],
        input_output_aliases={w: w for w in range(n)},
        scratch_shapes=[pltpu.SemaphoreType.DMA((n,)), pltpu.SemaphoreType.DMA((n,))], name="pair_join",
    )(*fulls)


def _allreduce_small(packed):
    rows = packed.shape[0]

    def kern(in_ref, out_ref, slots, send_sems, recv_sems):
        x, y, c = _place()
        mine = 4 * x + 2 * y + c
        slots[mine] = in_ref[...]
        started = []
        for k in range(1, N_DEV):
            peer = (x ^ (k >> 2), y ^ ((k >> 1) & 1), c ^ (k & 1))
            cp = pltpu.make_async_remote_copy(
                src_ref=in_ref, dst_ref=slots.at[mine], send_sem=send_sems.at[k - 1], recv_sem=recv_sems.at[k - 1],
                device_id=peer, device_id_type=MESH)
            cp.start()
            started.append(cp)
        for cp in started:
            cp.wait()
        acc = slots[0]
        for s in range(1, N_DEV):
            acc = acc + slots[s]
        out_ref[...] = acc

    vm = pl.BlockSpec(memory_space=pltpu.VMEM)
    return pl.pallas_call(
        kern, in_specs=[vm], out_specs=vm, out_shape=jax.ShapeDtypeStruct((rows, LANES), F32),
        scratch_shapes=[pltpu.VMEM((N_DEV, rows, LANES), F32), pltpu.SemaphoreType.DMA((N_DEV - 1,)),
                        pltpu.SemaphoreType.DMA((N_DEV - 1,))],
        name="allreduce_small",
    )(packed)


SMALL = ["norm1_g", "conv_w", "conv_b", "lru_w_a", "lru_b_a", "lru_w_x", "lru_b_x", "lru_lambda", "lru_out_g", "sb_out_g",
         "norm2_g", "final_g"]
BIG = ["w_in", "w_out", "w_up", "w_down"]
WEIGHTS = ["norm1_g", "w_in", "conv_w", "conv_b", "lru_w_a", "lru_b_a", "lru_w_x", "lru_b_x", "lru_lambda", "lru_out_g",
           "sb_out_g", "w_out", "norm2_g", "w_up", "w_down", "final_g"]


def _pack(arrays):
    flat = []
    for a in arrays:
        a = a.reshape(-1).astype(F32)
        flat.append(jnp.pad(a, (0, (-a.shape[0]) % LANES)))
    v = jnp.concatenate(flat)
    v = jnp.pad(v, (0, (-v.shape[0]) % (LANES * SUBLANES)))
    return v.reshape(-1, LANES)


def _unpack(packed, shapes):
    v, out, off = packed.reshape(-1), [], 0
    for shp in shapes:
        size = math.prod(shp)
        out.append(v[off:off + size].reshape(shp))
        off += size + (-size) % LANES
    return out


def _blockdiag_pairs(w):
    w = w.reshape(4, 2, DH, DH)
    z = jnp.zeros((4, DH, DH), w.dtype)
    return jnp.concatenate([jnp.concatenate([w[:, 0], z], axis=2), jnp.concatenate([z, w[:, 1]], axis=2)], axis=1)


def _blockdiag_unpairs(wbd):
    return jnp.stack([wbd[:, :DH, :DH], wbd[:, DH:, DH:]], axis=1).reshape(8, DH, DH)


def _full_cols(g):
    return jnp.transpose(g, (1, 0, 2)).reshape(g.shape[1], N_CHIPS * g.shape[2])


def _local_step(x2, tgt, seq, norm1_g, w_in_f, conv_w_f, conv_b, w_a, b_a, w_x, b_x, lru_lambda, lru_out_g, sb_out_g, rest,
                norm2_g, final_g, place=None):
    wbd = jnp.concatenate([_blockdiag_pairs(w_a), _blockdiag_pairs(w_x)], axis=2).astype(BF16)
    ba, bx = b_a.reshape(1, LRU_WIDTH), b_x.reshape(1, LRU_WIDTH)
    gf = final_g.reshape(1, D_MODEL)

    xl, qkv, xn = _inproj(x2, norm1_g, w_in_f)
    h, y_lru = _lru_fwd(xl, conv_w_f, conv_b, wbd, ba, bx, lru_lambda, seq)
    if place is None:
        o, tot, kmin, _ = _attn_fwd(qkv, seq)
        w_out_f, w_up_f, w_down_f = rest
    else:
        o, tot, kmin, (gw_out, gw_up, gw_down) = _attn_fwd(qkv, seq, rest)
        w_out_f, w_up_f, w_down_f = gw_out.reshape(D_MODEL, D_MODEL), _full_cols(gw_up), gw_down.reshape(D_FF, D_MODEL)
    h1, mix = _outproj(y_lru, o, x2, lru_out_g, sb_out_g, w_out_f)
    h2, hn, up, u2 = _mlp_fwd(h1, norm2_g, w_up_f, w_down_f)
    dh2, dh2b, loss_part, d_final = _loss_head(h2, tgt, gf)

    dpre = _mlp_bwd_pre(dh2b, w_down_f, up)
    g_w_down = _matmul(u2, dh2b, "tn", F32, "dw_down", 1024, 1024, 1024).reshape(N_CHIPS, D_FF // N_CHIPS, D_MODEL)
    g_w_up = _matmul(hn, dpre, "tn", F32, "dw_up", 1024, D_FF // N_CHIPS, 1024, split_cols=True)
    dh1, d_norm2 = _proj_bwd_norm([dpre], w_up_f, h1, norm2_g, dh2, "mlp_bwd_in")
    g_w_out = _matmul(mix, dh1, "tn", F32, "dw_out", 1024, 1024, 2048).reshape(N_CHIPS, D_MODEL // N_CHIPS, D_MODEL)
    dy_lru, do, d_ga, d_gb = _outproj_bwd(dh1, w_out_f, y_lru, o, lru_out_g, sb_out_g)
    late = [g_w_out, g_w_up, g_w_down]
    dq, dk, dv = _attn_bwd(qkv, do, tot, kmin, seq)
    if place is None:
        dxl, lru_small, d_wbd, _ = _lru_bwd(xl, h, dy_lru, conv_w_f, conv_b, wbd, ba, bx, lru_lambda, seq)
    else:
        parts = [_pair_add(g, r, place[1:]) for g, r in zip(late, _pair_swap(late))]
        dxl, lru_small, d_wbd, got = _lru_bwd(xl, h, dy_lru, conv_w_f, conv_b, wbd, ba, bx, lru_lambda, seq, parts)
        late = [_chip_add(p, r, place) for p, r in zip(parts, got)]
    dx, d_norm1 = _proj_bwd_norm([dxl, dq, dk, dv], w_in_f, x2, norm1_g, dh1, "inproj_bwd")
    dproj = jnp.concatenate([dxl, dq, dk, dv], axis=1)
    g_w_in = _matmul(xn, dproj, "tn", F32, "dw_in", 1024, IN_COLS // N_CHIPS, 2048, split_cols=True)
    small_parts = {
        "norm1_g": d_norm1, "conv_w": lru_small[:CONV_WIDTH], "conv_b": lru_small[4:5],
        "lru_w_a": _blockdiag_unpairs(d_wbd[:, :, :LANES]), "lru_b_a": lru_small[5:6],
        "lru_w_x": _blockdiag_unpairs(d_wbd[:, :, LANES:]), "lru_b_x": lru_small[6:7], "lru_lambda": lru_small[7:8],
        "lru_out_g": d_ga, "sb_out_g": d_gb, "norm2_g": d_norm2, "final_g": d_final,
    }
    return loss_part, dx, [g_w_in] + late, small_parts


def kernel(x, norm1_g, w_in, conv_w, conv_b, lru_w_a, lru_b_a, lru_w_x, lru_b_x, lru_lambda, lru_out_g, sb_out_g, w_out, norm2_g, w_up, w_down, final_g, loss_target, m_norm1_g, m_w_in, m_conv_w, m_conv_b, m_lru_w_a, m_lru_b_a, m_lru_w_x, m_lru_b_x, m_lru_lambda, m_lru_out_g, m_sb_out_g, m_w_out, m_norm2_g, m_w_up, m_w_down, m_final_g, v_norm1_g, v_w_in, v_conv_w, v_conv_b, v_lru_w_a, v_lru_b_a, v_lru_w_x, v_lru_b_x, v_lru_lambda, v_lru_out_g, v_sb_out_g, v_w_out, v_norm2_g, v_w_up, v_w_down, v_final_g):
    given = dict(locals())
    ne, seq, _ = x.shape
    t = ne * seq
    xi, yi, ci = _place()
    place = jnp.stack([2 * xi + yi, ci]).astype(jnp.int32)

    gw_in, gconv = _gather_weights([w_in[0].astype(BF16), conv_w[0]])
    rest = [w_out[0].astype(BF16), w_up[0].astype(BF16), w_down[0].astype(BF16)]
    loss_part, dx, (g_w_in, *late), small_parts = _local_step(
        x.reshape(t, D_MODEL), loss_target.reshape(t, D_MODEL), seq, norm1_g, _full_cols(gw_in), _full_cols(gconv), conv_b,
        lru_w_a[0], lru_b_a, lru_w_x[0], lru_b_x, lru_lambda, lru_out_g, sb_out_g, rest, norm2_g, final_g, place)

    full_shapes = {n: ((CONV_WIDTH, LRU_WIDTH) if n == "conv_w" else given[n].shape) for n in SMALL}
    red = _allreduce_small(_pack([small_parts[n] for n in SMALL] + [loss_part]))
    red_list = _unpack(red, [full_shapes[n] for n in SMALL] + [(1, LANES)])
    grads = dict(zip(SMALL, red_list[:-1]))
    loss = red_list[-1][0, 0]
    grads["conv_w"] = lax.dynamic_slice_in_dim(grads["conv_w"], place[0] * (LRU_WIDTH // N_CHIPS), LRU_WIDTH // N_CHIPS,
                                               axis=1).reshape(conv_w.shape)

    part = _pair_add(g_w_in, _pair_swap([g_w_in])[0], place[1:])
    g_w_in = _chip_add(part, _chip_exchange([part])[0], place)
    for n, full in zip(BIG, _pair_join([g_w_in] + late)):
        grads[n] = full.reshape(given[n].shape)

    delta, new_m, new_v = {}, {}, {}
    for n in BIG:
        shp = given[n].shape
        d, m2, v2 = _adamw(given[n][0], grads[n][0], given["m_" + n][0], given["v_" + n][0], "adamw_" + n)
        delta[n], new_m[n], new_v[n] = d.reshape(shp), m2.reshape(shp), v2.reshape(shp)
    shapes = [given[n].shape for n in SMALL]
    d, m2, v2 = _adamw(_pack([given[n] for n in SMALL]), _pack([grads[n] for n in SMALL]),
                       _pack([given["m_" + n] for n in SMALL]), _pack([given["v_" + n] for n in SMALL]), "adamw_small")
    for n, dd, mm, vv in zip(SMALL, _unpack(d, shapes), _unpack(m2, shapes), _unpack(v2, shapes)):
        delta[n], new_m[n], new_v[n] = dd, mm, vv

    return (loss, dx.reshape(x.shape), *[grads[n] for n in WEIGHTS], *[delta[n] for n in WEIGHTS],
            *[new_m[n] for n in WEIGHTS], *[new_v[n] for n in WEIGHTS])
```

```python
import functools
import math

import jax
import jax.numpy as jnp
from jax import lax
from jax.experimental import pallas as pl
from jax.experimental.pallas import tpu as pltpu

F32, BF16 = jnp.float32, jnp.bfloat16
MESH = pl.DeviceIdType.MESH

D_MODEL = 1024
LRU_WIDTH = 512
SB_WIDTH = 512
DH = 64
IN_COLS = 2 * LRU_WIDTH + 3 * SB_WIDTH
D_FF = 4 * D_MODEL
CONV_WIDTH = 4
LRU_C = 8.0
EPS = 1e-6
N_CHIPS = 4
N_DEV = 8
LANES = 128
SUBLANES = 8
TQ = 512
TK = 128
ATT_SCALE = 1.0 / math.sqrt(DH)
SKIP_LOG = -105.0
VMEM_LIMIT = 52 * 1024 * 1024

ADAM_LR, ADAM_B1, ADAM_B2, ADAM_EPS, ADAM_WD, ADAM_STEP = 0.001, 0.9, 0.999, 1e-08, 0.01, 10

_GELU_K = math.sqrt(2.0 / math.pi)
_GELU_C = 0.044715


def _cp(sem):
    return pltpu.CompilerParams(dimension_semantics=sem, vmem_limit_bytes=VMEM_LIMIT)


def _dot(a, b):
    return jnp.dot(a, b, preferred_element_type=F32)


def _dot_nt(a, b):
    return lax.dot_general(a, b, (((1,), (1,)), ((), ())), preferred_element_type=F32)


def _dot_tn(a, b):
    return lax.dot_general(a, b, (((0,), (0,)), ((), ())), preferred_element_type=F32)


def _rstd(x):
    return lax.rsqrt(jnp.mean(x * x, axis=-1, keepdims=True) + EPS)


def _rms_bwd(x, g, dy):
    r = _rstd(x)
    gd = g * dy
    dx = r * gd - x * (r * r * r) * jnp.mean(x * gd, axis=-1, keepdims=True)
    return dx, jnp.sum(dy * x * r, axis=0, keepdims=True)


def _sigmoid(x):
    return 1.0 / (1.0 + jnp.exp(-x))


def _softplus(x):
    return jnp.maximum(x, 0.0) + jnp.log(1.0 + jnp.exp(-jnp.abs(x)))


def _neg_expm1(x):
    series = -x * (1.0 + x * (0.5 + x * (1.0 / 6.0 + x * (1.0 / 24.0))))
    return jnp.where(x > -0.01, series, 1.0 - jnp.exp(x))


def _gelu(g):
    t = jnp.tanh(_GELU_K * (g + _GELU_C * g * g * g))
    return 0.5 * g * (1.0 + t), t


def _gelu_grad(g, t):
    return 0.5 * (1.0 + t) + 0.5 * g * (1.0 - t * t) * _GELU_K * (1.0 + 3.0 * _GELU_C * g * g)


def _rows(shape):
    return lax.broadcasted_iota(jnp.int32, shape, 0)


def _shift_down(x, s, fill):
    return jnp.where(_rows(x.shape) >= s, pltpu.roll(x, s, 0), fill)


def _shift_up(x, s, fill):
    n = x.shape[0]
    return jnp.where(_rows(x.shape) < n - s, pltpu.roll(x, n - s, 0), fill)


def _row_of(x, idx):
    return jnp.sum(jnp.where(_rows(x.shape) == idx, x, 0.0), axis=0, keepdims=True)


def _matmul(a, b, dims, out_dtype, name, tm, tn, tk, split_cols=False):
    if dims == "nn":
        (m, kk), n, dot = a.shape, b.shape[1], _dot
    elif dims == "nt":
        (m, kk), n, dot = a.shape, b.shape[0], _dot_nt
    else:
        (kk, m), n, dot = a.shape, b.shape[1], _dot_tn
    tm, tn, tk = min(tm, m), min(tn, n), min(tk, kk)
    assert m % tm == 0 and n % tn == 0 and kk % tk == 0, (name, m, n, kk)
    nk = kk // tk

    def kern(a_ref, b_ref, o_ref, acc_ref):
        k = pl.program_id(2)

        @pl.when(k == 0)
        def _():
            acc_ref[...] = jnp.zeros_like(acc_ref)

        acc_ref[...] += dot(a_ref[...].astype(BF16), b_ref[...].astype(BF16))

        @pl.when(k == nk - 1)
        def _():
            o_ref[...] = acc_ref[...].astype(o_ref.dtype)

    if split_cols:
        out_shape = jax.ShapeDtypeStruct((n // tn, m, tn), out_dtype)
        o_spec = pl.BlockSpec((None, tm, tn), lambda i, j, k: (j, i, 0))
    else:
        out_shape = jax.ShapeDtypeStruct((m, n), out_dtype)
        o_spec = pl.BlockSpec((tm, tn), lambda i, j, k: (i, j))
    if dims == "nn":
        a_spec = pl.BlockSpec((tm, tk), lambda i, j, k: (i, k))
        b_spec = pl.BlockSpec((tk, tn), lambda i, j, k: (k, j))
    elif dims == "nt":
        a_spec = pl.BlockSpec((tm, tk), lambda i, j, k: (i, k))
        b_spec = pl.BlockSpec((tn, tk), lambda i, j, k: (j, k))
    else:
        a_spec = pl.BlockSpec((tk, tm), lambda i, j, k: (k, i))
        b_spec = pl.BlockSpec((tk, tn), lambda i, j, k: (k, j))
    return pl.pallas_call(
        kern, grid=(m // tm, n // tn, nk), in_specs=[a_spec, b_spec], out_specs=o_spec, out_shape=out_shape,
        scratch_shapes=[pltpu.VMEM((tm, tn), F32)], compiler_params=_cp(("parallel", "parallel", "arbitrary")), name=name,
    )(a, b)


ANY = pl.BlockSpec(memory_space=pl.ANY)


def _place():
    return lax.axis_index("x"), lax.axis_index("y"), lax.axis_index("c")


def _other_chips(x, y):
    return [(1 - x, y), (x, 1 - y), (1 - x, 1 - y)]


def _gather_copies(ins, outs, send_sems, recv_sems, loc_sems):
    x, y, c = _place()
    mine = 2 * x + y
    copies = []
    for w in range(len(ins)):
        copies.append(pltpu.make_async_copy(ins[w], outs[w].at[mine], loc_sems.at[w]))
        for k, chip in enumerate(_other_chips(x, y)):
            copies.append(pltpu.make_async_remote_copy(
                src_ref=ins[w], dst_ref=outs[w].at[mine], send_sem=send_sems.at[3 * w + k],
                recv_sem=recv_sems.at[3 * w + k], device_id=(*chip, c), device_id_type=MESH))
    return copies


def _gather_shapes(shards):
    return ([jax.ShapeDtypeStruct((N_CHIPS,) + s.shape, s.dtype) for s in shards],
            [pltpu.SemaphoreType.DMA((3 * len(shards),)), pltpu.SemaphoreType.DMA((3 * len(shards),)),
             pltpu.SemaphoreType.DMA((len(shards),))])


def _exchange_copies(ins, outs, send_sems, recv_sems):
    x, y, c = _place()
    copies = []
    for w in range(len(ins)):
        for k, chip in enumerate(_other_chips(x, y)):
            copies.append(pltpu.make_async_remote_copy(
                src_ref=ins[w].at[2 * chip[0] + chip[1]], dst_ref=outs[w].at[k], send_sem=send_sems.at[3 * w + k],
                recv_sem=recv_sems.at[3 * w + k], device_id=(*chip, c), device_id_type=MESH))
    return copies


def _exchange_shapes(parts):
    return ([jax.ShapeDtypeStruct((3,) + p.shape[1:], p.dtype) for p in parts],
            [pltpu.SemaphoreType.DMA((3 * len(parts),)), pltpu.SemaphoreType.DMA((3 * len(parts),))])


def _swap_copies(ins, outs, send_sems, recv_sems):
    x, y, c = _place()
    copies = []
    for w in range(len(ins)):
        half = ins[w].shape[1] // 2
        copies.append(pltpu.make_async_remote_copy(
            src_ref=ins[w].at[:, pl.ds((1 - c) * half, half), :], dst_ref=outs[w], send_sem=send_sems.at[w],
            recv_sem=recv_sems.at[w], device_id=(x, y, 1 - c), device_id_type=MESH))
    return copies


def _swap_shapes(grads):
    return ([jax.ShapeDtypeStruct((g.shape[0], g.shape[1] // 2, g.shape[2]), g.dtype) for g in grads],
            [pltpu.SemaphoreType.DMA((len(grads),)), pltpu.SemaphoreType.DMA((len(grads),))])


COMM = {"gather": (_gather_copies, _gather_shapes), "exchange": (_exchange_copies, _exchange_shapes),
        "swap": (_swap_copies, _swap_shapes)}


def _call(body, comm, *, grid, in_specs, out_specs, out_shape, scratch_shapes, args, name):
    ni, no, ns = len(in_specs), len(out_specs), len(scratch_shapes)
    arrays = list(comm[1]) if comm else []
    nc = len(arrays)
    c_shapes, c_sems = COMM[comm[0]][1](arrays) if comm else ([], [])

    def kern(*refs):
        ins, cin, outs = refs[:ni], refs[ni:ni + nc], refs[ni + nc:ni + nc + no]
        cout, scr, sems = refs[ni + nc + no:ni + 2 * nc + no], refs[ni + 2 * nc + no:ni + 2 * nc + no + ns], refs[ni + 2 * nc + no + ns:]
        ids = [pl.program_id(d) for d in range(len(grid))]
        if nc:
            @pl.when(functools.reduce(lambda a, b: a & b, [i == 0 for i in ids]))
            def _():
                for cp in COMM[comm[0]][0](cin, cout, *sems):
                    cp.start()

        body(ins, outs, scr)
        if nc:
            @pl.when(functools.reduce(lambda a, b: a & b, [i == g - 1 for i, g in zip(ids, grid)]))
            def _():
                for cp in COMM[comm[0]][0](cin, cout, *sems):
                    cp.wait()

    out = pl.pallas_call(
        kern, grid=grid, in_specs=list(in_specs) + [ANY] * nc, out_specs=list(out_specs) + [ANY] * nc,
        out_shape=list(out_shape) + c_shapes, scratch_shapes=list(scratch_shapes) + c_sems,
        compiler_params=_cp(("arbitrary",) * len(grid)), name=name,
    )(*args, *arrays)
    return list(out[:no]), list(out[no:])


def _resident(shape):
    return pl.BlockSpec(shape, lambda *_: (0,) * len(shape), pipeline_mode=pl.Buffered(1))


def _norm1(x, g1, comm):
    t = x.shape[0]
    tm = min(1024, t)

    def body(ins, outs, _):
        xv = ins[0][...]
        outs[0][...] = (xv * _rstd(xv) * ins[1][...]).astype(BF16)

    row = pl.BlockSpec((tm, D_MODEL), lambda i: (i, 0))
    (xn,), got = _call(body, comm, grid=(t // tm,), in_specs=[row, pl.BlockSpec((1, D_MODEL), lambda i: (0, 0))],
                       out_specs=[row], out_shape=[jax.ShapeDtypeStruct((t, D_MODEL), BF16)], scratch_shapes=[],
                       args=(x, g1), name="norm1")
    return xn, got


def _inproj(xn, w_in):
    t = xn.shape[0]
    tm = min(512, t)

    def kern(x_ref, w_ref, xl_ref, qkv_ref):
        xn_v = x_ref[...]
        xl_ref[...] = _dot(xn_v, w_ref[:, : 2 * LRU_WIDTH])
        qkv_ref[...] = _dot(xn_v, w_ref[:, 2 * LRU_WIDTH:]).astype(BF16)

    row = lambda c: pl.BlockSpec((tm, c), lambda i: (i, 0))
    return pl.pallas_call(
        kern, grid=(t // tm,), in_specs=[row(D_MODEL), _resident((D_MODEL, IN_COLS))],
        out_specs=[row(2 * LRU_WIDTH), row(3 * SB_WIDTH)],
        out_shape=[jax.ShapeDtypeStruct((t, 2 * LRU_WIDTH), F32), jax.ShapeDtypeStruct((t, 3 * SB_WIDTH), BF16)],
        compiler_params=_cp(("parallel",)), name="inproj",
    )(xn, w_in)


def _conv_taps(hist, u):
    cat = jnp.concatenate([hist, u], axis=0)
    return [pltpu.roll(cat, CONV_WIDTH - 1 - k, 0)[SUBLANES:] for k in range(CONV_WIDTH - 1)] + [u]


def _lru_gates(c, wbd_ref, ba, bx, sp):
    gas, gxs = [], []
    for p in range(LRU_WIDTH // LANES):
        gax = _dot(c[:, LANES * p: LANES * (p + 1)].astype(BF16), wbd_ref[p])
        gas.append(gax[:, :LANES])
        gxs.append(gax[:, LANES:])
    r = _sigmoid(jnp.concatenate(gas, axis=1) + ba)
    i = _sigmoid(jnp.concatenate(gxs, axis=1) + bx)
    la = (-LRU_C) * r * sp
    a = jnp.exp(la)
    mult = jnp.sqrt(_neg_expm1(2.0 * la))
    return r, i, a, mult


def _scan_fwd(a, b):
    s = 1
    while s < a.shape[0]:
        b = b + a * _shift_down(b, s, 0.0)
        a = a * _shift_down(a, s, 1.0)
        s *= 2
    return a, b


def _scan_rev(a, b):
    s = 1
    while s < a.shape[0]:
        b = b + a * _shift_up(b, s, 0.0)
        a = a * _shift_up(a, s, 1.0)
        s *= 2
    return a, b


def _lru_param_specs(grid_rank):
    z2 = (lambda e, c: (0, 0)) if grid_rank == 2 else None
    return [
        pl.BlockSpec((CONV_WIDTH, LRU_WIDTH), z2), pl.BlockSpec((1, LRU_WIDTH), z2),
        pl.BlockSpec((LRU_WIDTH // LANES, LANES, 2 * LANES), lambda e, c: (0, 0, 0)),
        pl.BlockSpec((1, LRU_WIDTH), z2), pl.BlockSpec((1, LRU_WIDTH), z2), pl.BlockSpec((1, LRU_WIDTH), z2),
    ]


def _lru_fwd(xl, conv_w, conv_b, wbd, ba, bx, lam, seq):
    t = xl.shape[0]
    tc = min(512, seq)
    nc = seq // tc

    def kern(u_ref, g_ref, cw_ref, cb_ref, wbd_ref, ba_ref, bx_ref, lam_ref, h_ref, y_ref, hist_ref, hcar_ref):
        @pl.when(pl.program_id(1) == 0)
        def _():
            hist_ref[...] = jnp.zeros_like(hist_ref)
            hcar_ref[...] = jnp.zeros_like(hcar_ref)

        u = u_ref[...]
        taps = _conv_taps(hist_ref[...], u)
        hist_ref[...] = u_ref[tc - SUBLANES:, :]
        c = cb_ref[...]
        for k in range(CONV_WIDTH):
            c = c + taps[k] * cw_ref[k:k + 1, :]
        sp = _softplus(-lam_ref[...])
        _, i, a, mult = _lru_gates(c, wbd_ref, ba_ref[...], bx_ref[...], sp)
        aa, bb = _scan_fwd(a, mult * i * c)
        h = bb + aa * hcar_ref[0:1, :]
        h_ref[...] = h
        hcar_ref[0:1, :] = h_ref[tc - 1:tc, :]
        y_ref[...] = h * _gelu(g_ref[...])[0]

    chunk = lambda col: pl.BlockSpec((tc, LRU_WIDTH), lambda e, c: (e * nc + c, col))
    out = jax.ShapeDtypeStruct((t, LRU_WIDTH), F32)
    return pl.pallas_call(
        kern, grid=(t // seq, nc), in_specs=[chunk(0), chunk(1)] + _lru_param_specs(2),
        out_specs=[chunk(0), chunk(0)], out_shape=[out, out],
        scratch_shapes=[pltpu.VMEM((SUBLANES, LRU_WIDTH), F32), pltpu.VMEM((SUBLANES, LRU_WIDTH), F32)],
        compiler_params=_cp(("arbitrary", "arbitrary")), name="lru_fwd",
    )(xl, xl, conv_w, conv_b, wbd, ba, bx, lam)


def _att_consts():
    row = lax.broadcasted_iota(jnp.int32, (TQ, 2 * TK), 0)
    key = lax.broadcasted_iota(jnp.int32, (TQ, 2 * TK), 1) & (TK - 1)
    return key < row


def _sum_matrix(kind):
    j = lax.broadcasted_iota(jnp.int32, (2 * TK, 2 * TK), 0) & (TK - 1)
    s = lax.broadcasted_iota(jnp.int32, (2 * TK, 2 * TK), 1)
    pick = {"after": j > s, "upto": j <= s, "before": j < s}[kind]
    return jnp.where((s >= TK) | pick, 1.0, 0.0).astype(BF16)


def _hi_lo(x):
    hi = x.astype(BF16)
    return hi, (x - hi.astype(F32)).astype(BF16)


def _pair_sums(x, m):
    hi, lo = _hi_lo(x)
    out = []
    for hd in range(2):
        cols = slice(hd * TK, (hd + 1) * TK)
        out.append(_dot(jnp.concatenate([hi[:, cols], lo[:, cols]], axis=1), m))
    return [o[:, :TK] for o in out], [o[:, TK:] for o in out]


def _att_logits(qb, kbd):
    z = _dot(qb, kbd)
    lg = jnp.log(1.0 + jnp.exp(-jnp.abs(z)))
    lb = jnp.minimum(z, 0.0) - lg
    return lb, lb - z


def _head_diag(x, rows_first):
    n = x.shape[0] if rows_first else x.shape[1]
    idx = lax.broadcasted_iota(jnp.int32, x.shape, 0 if rows_first else 1)
    return jnp.where(idx < n // 2, x, 0), jnp.where(idx >= n // 2, x, 0)


def _scaled_q(q_ref, q0):
    return (q_ref[pl.ds(q0, TQ), :].astype(F32) * ATT_SCALE).astype(BF16)


def _qkv_specs(seq):
    n = SB_WIDTH // LANES
    return [pl.BlockSpec((seq, LANES), lambda e, p, off=off: (e, off * n + p)) for off in range(3)]


def _attn_fwd(qkv, seq, comm=None):
    t = qkv.shape[0]
    ne, nq, nk = t // seq, seq // TQ, seq // TK

    def body(ins, outs, scr):
        (q_ref, k_ref, v_ref), (o_ref, tot_ref, kmin_ref), (kbd_scr, vbd_scr) = ins, outs, scr
        causal = _att_consts()
        after = _sum_matrix("after")

        def prep(j, _):
            k0 = pl.multiple_of(j * TK, TK)
            top, bot = _head_diag(k_ref[pl.ds(k0, TK), :].astype(F32).T, True)
            kbd_scr[j] = jnp.concatenate([top, bot], axis=1).astype(BF16)
            left, right = _head_diag(v_ref[pl.ds(k0, TK), :], False)
            vbd_scr[j] = jnp.concatenate([left, right], axis=0)
            return 0

        lax.fori_loop(0, nk, prep, 0)

        def block(j, qb, st, mask):
            c0, c1, oacc = st
            lb, l1 = _att_logits(qb, kbd_scr[j])
            if mask is not None:
                l1 = jnp.where(mask, l1, 0.0)
            (s0, s1), (r0, r1) = _pair_sums(l1, after)
            att = jnp.exp(lb + jnp.concatenate([s0 + c0, s1 + c1], axis=1))
            if mask is not None:
                att = jnp.where(mask, att, 0.0)
            return c0 + r0, c1 + r1, oacc + _dot(att.astype(BF16), vbd_scr[j])

        def qloop(qi, _):
            q0 = pl.multiple_of(qi * TQ, TQ)
            qb = _scaled_q(q_ref, q0)
            zero = jnp.zeros((TQ, TK), F32)
            st = (zero, zero, jnp.zeros((TQ, LANES), F32))
            for jj in reversed(range(TQ // TK)):
                lo = TK * jj
                new = block((TQ // TK) * qi + jj, qb[lo:], tuple(x[lo:] for x in st), causal[:TQ - lo])
                st = tuple(jnp.concatenate([x[:lo], y], axis=0) if lo else y for x, y in zip(st, new))

            npair = (TQ // TK // 2) * qi

            def more(its):
                return (its[0] < npair) & (jnp.max(jnp.maximum(its[1], its[2])) > SKIP_LOG)

            def kloop(its):
                j = 2 * (npair - its[0]) - 1
                return (its[0] + 1,) + block(j - 1, qb, block(j, qb, its[1:], None), None)

            done, c0, c1, oacc = lax.while_loop(more, kloop, (jnp.int32(0),) + st)
            o_ref[pl.ds(q0, TQ), :] = oacc
            tot_ref[pl.ds(q0, TQ), :] = jnp.concatenate([c0, c1], axis=1)
            kmin_ref[pl.program_id(0), pl.program_id(1), qi] = 2 * (npair - done)
            return 0

        lax.fori_loop(0, nq, qloop, 0)

    (o, tot, kmin), got = _call(
        body, comm, grid=(ne, SB_WIDTH // LANES), in_specs=_qkv_specs(seq),
        out_specs=[pl.BlockSpec((seq, LANES), lambda e, p: (e, p)), pl.BlockSpec((seq, 2 * TK), lambda e, p: (e, p)),
                   pl.BlockSpec(memory_space=pltpu.SMEM)],
        out_shape=[jax.ShapeDtypeStruct((t, SB_WIDTH), F32), jax.ShapeDtypeStruct((t, 2 * TK * SB_WIDTH // LANES), F32),
                   jax.ShapeDtypeStruct((ne, SB_WIDTH // LANES, nq), jnp.int32)],
        scratch_shapes=[pltpu.VMEM((nk, LANES, 2 * TK), BF16), pltpu.VMEM((nk, 2 * TK, LANES), BF16)],
        args=(qkv, qkv, qkv), name="attn_fwd")
    return o, tot, kmin, got


def _outproj(y_lru, o, x, ga, gb, w_out):
    t = x.shape[0]
    tm = min(512, t)

    def kern(y_ref, o_ref, x_ref, ga_ref, gb_ref, w_ref, h1_ref, mix_ref):
        yv, ov = y_ref[...], o_ref[...]
        mix = jnp.concatenate([yv * _rstd(yv) * ga_ref[...], ov * _rstd(ov) * gb_ref[...]], axis=1).astype(BF16)
        mix_ref[...] = mix
        h1_ref[...] = x_ref[...] + _dot(mix, w_ref[...])

    row = lambda c: pl.BlockSpec((tm, c), lambda i: (i, 0))
    vec = lambda c: pl.BlockSpec((1, c), lambda i: (0, 0))
    return pl.pallas_call(
        kern, grid=(t // tm,),
        in_specs=[row(LRU_WIDTH), row(SB_WIDTH), row(D_MODEL), vec(LRU_WIDTH), vec(SB_WIDTH),
                  pl.BlockSpec((D_MODEL, D_MODEL), lambda i: (0, 0))],
        out_specs=[row(D_MODEL), row(D_MODEL)],
        out_shape=[jax.ShapeDtypeStruct((t, D_MODEL), F32), jax.ShapeDtypeStruct((t, D_MODEL), BF16)],
        compiler_params=_cp(("parallel",)), name="outproj",
    )(y_lru, o, x, ga, gb, w_out)


def _mlp_fwd(h1, g2, w_up, w_down):
    t = h1.shape[0]
    tm, tf = min(512, t), 1024

    def kern(h1_ref, g_ref, wu_ref, wd_ref, h2_ref, hn_ref, up_ref, u2_ref):
        hv = h1_ref[...]
        hn = (hv * _rstd(hv) * g_ref[...]).astype(BF16)
        hn_ref[...] = hn
        acc = hv
        for f in range(D_FF // tf):
            cols = slice(f * tf, (f + 1) * tf)
            up = jnp.maximum(_dot(hn, wu_ref[:, cols]), 0.0)
            u2 = (up * up).astype(BF16)
            up_ref[:, cols] = up.astype(BF16)
            u2_ref[:, cols] = u2
            acc = acc + _dot(u2, wd_ref[cols, :])
        h2_ref[...] = acc

    row = lambda c: pl.BlockSpec((tm, c), lambda i: (i, 0))
    return pl.pallas_call(
        kern, grid=(t // tm,),
        in_specs=[row(D_MODEL), pl.BlockSpec((1, D_MODEL), lambda i: (0, 0)), _resident((D_MODEL, D_FF)),
                  _resident((D_FF, D_MODEL))],
        out_specs=[row(D_MODEL), row(D_MODEL), row(D_FF), row(D_FF)],
        out_shape=[jax.ShapeDtypeStruct((t, D_MODEL), F32), jax.ShapeDtypeStruct((t, D_MODEL), BF16),
                   jax.ShapeDtypeStruct((t, D_FF), BF16), jax.ShapeDtypeStruct((t, D_FF), BF16)],
        compiler_params=_cp(("parallel",)), name="mlp_fwd",
    )(h1, g2, w_up, w_down)


def _loss_head(h2, target, gf):
    t = h2.shape[0]
    tm = min(512, t)

    def kern(h_ref, t_ref, g_ref, dh_ref, dhb_ref, loss_ref, dg_ref):
        @pl.when(pl.program_id(0) == 0)
        def _():
            loss_ref[...] = jnp.zeros_like(loss_ref)
            dg_ref[...] = jnp.zeros_like(dg_ref)

        hv, g = h_ref[...], g_ref[...]
        err = hv * _rstd(hv) * g - t_ref[...]
        lane = lax.broadcasted_iota(jnp.int32, (1, LANES), 1)
        loss_ref[...] += jnp.where(lane == 0, 0.5 * jnp.sum(err * err) / D_MODEL, 0.0)
        dx, dg = _rms_bwd(hv, g, err * (1.0 / D_MODEL))
        dh_ref[...] = dx
        dhb_ref[...] = dx.astype(BF16)
        dg_ref[...] += dg

    row = pl.BlockSpec((tm, D_MODEL), lambda i: (i, 0))
    vec = pl.BlockSpec((1, D_MODEL), lambda i: (0, 0))
    return pl.pallas_call(
        kern, grid=(t // tm,), in_specs=[row, row, vec],
        out_specs=[row, row, pl.BlockSpec((1, LANES), lambda i: (0, 0)), vec],
        out_shape=[jax.ShapeDtypeStruct((t, D_MODEL), F32), jax.ShapeDtypeStruct((t, D_MODEL), BF16),
                   jax.ShapeDtypeStruct((1, LANES), F32), jax.ShapeDtypeStruct((1, D_MODEL), F32)],
        compiler_params=_cp(("arbitrary",)), name="loss_head",
    )(h2, target, gf)


def _mlp_bwd_pre(dh2, w_down, up):
    t = dh2.shape[0]
    tm, tf = min(512, t), 1024

    def kern(d_ref, w_ref, up_ref, o_ref):
        dv = d_ref[...]
        for f in range(D_FF // tf):
            cols = slice(f * tf, (f + 1) * tf)
            o_ref[:, cols] = (_dot_nt(dv, w_ref[cols, :]) * (2.0 * up_ref[:, cols].astype(F32))).astype(BF16)

    row = lambda c: pl.BlockSpec((tm, c), lambda i: (i, 0))
    return pl.pallas_call(
        kern, grid=(t // tm,), in_specs=[row(D_MODEL), _resident((D_FF, D_MODEL)), row(D_FF)],
        out_specs=row(D_FF), out_shape=jax.ShapeDtypeStruct((t, D_FF), BF16),
        compiler_params=_cp(("parallel",)), name="mlp_bwd_pre",
    )(dh2, w_down, up)


def _proj_bwd_norm(dys, w, x, g, resid, name, comm=None):
    t = x.shape[0]
    tm = min(512, t)
    widths = [dy.shape[1] for dy in dys]
    n = len(dys)

    def body(ins, outs, _):
        dy_refs, (w_ref, x_ref, g_ref, r_ref), (dx_ref, dg_ref) = ins[:n], ins[n:], outs

        @pl.when(pl.program_id(0) == 0)
        def _():
            dg_ref[...] = jnp.zeros_like(dg_ref)

        off, dxn = 0, None
        for dy_ref, wd in zip(dy_refs, widths):
            part = _dot_nt(dy_ref[...], w_ref[:, off:off + wd])
            dxn = part if dxn is None else dxn + part
            off += wd
        dx, dg = _rms_bwd(x_ref[...], g_ref[...], dxn)
        dx_ref[...] = r_ref[...] + dx
        dg_ref[...] += dg

    row = lambda c: pl.BlockSpec((tm, c), lambda i: (i, 0))
    vec = pl.BlockSpec((1, D_MODEL), lambda i: (0, 0))
    (dx, dg), got = _call(
        body, comm, grid=(t // tm,), in_specs=[row(wd) for wd in widths] + [_resident(w.shape), row(D_MODEL), vec, row(D_MODEL)],
        out_specs=[row(D_MODEL), vec],
        out_shape=[jax.ShapeDtypeStruct((t, D_MODEL), F32), jax.ShapeDtypeStruct((1, D_MODEL), F32)],
        scratch_shapes=[], args=(*dys, w, x, g, resid), name=name)
    return dx, dg, got


def _outproj_bwd(dh1, w_out, y_lru, o, ga, gb, comm=None):
    t = dh1.shape[0]
    tm = min(512, t)

    def body(ins, outs, _):
        (d_ref, w_ref, y_ref, o_ref, ga_ref, gb_ref), (dy_ref, do_ref, dga_ref, dgb_ref) = ins, outs

        @pl.when(pl.program_id(0) == 0)
        def _():
            dga_ref[...] = jnp.zeros_like(dga_ref)
            dgb_ref[...] = jnp.zeros_like(dgb_ref)

        dmix = _dot_nt(d_ref[...].astype(BF16), w_ref[...])
        dy, dga = _rms_bwd(y_ref[...], ga_ref[...], dmix[:, :LRU_WIDTH])
        do, dgb = _rms_bwd(o_ref[...], gb_ref[...], dmix[:, LRU_WIDTH:])
        dy_ref[...] = dy
        do_ref[...] = do
        dga_ref[...] += dga
        dgb_ref[...] += dgb

    row = lambda c: pl.BlockSpec((tm, c), lambda i: (i, 0))
    vec = pl.BlockSpec((1, LRU_WIDTH), lambda i: (0, 0))
    half = jax.ShapeDtypeStruct((t, LRU_WIDTH), F32)
    gsum = jax.ShapeDtypeStruct((1, LRU_WIDTH), F32)
    outs, got = _call(body, comm, grid=(t // tm,),
                      in_specs=[row(D_MODEL), _resident((D_MODEL, D_MODEL)), row(LRU_WIDTH), row(SB_WIDTH), vec, vec],
                      out_specs=[row(LRU_WIDTH), row(SB_WIDTH), vec, vec], out_shape=[half, half, gsum, gsum],
                      scratch_shapes=[], args=(dh1, w_out, y_lru, o, ga, gb), name="outproj_bwd")
    return (*outs, got)


def _attn_bwd(qkv, do, tot, kmin, seq):
    t = qkv.shape[0]
    ne, nq, nk = t // seq, seq // TQ, seq // TK

    def kern(q_ref, k_ref, v_ref, do_ref, tot_ref, kmin_ref, dq_ref, dk_ref, dv_ref,
             kbd_scr, vtbd_scr, kbd2_scr, dkt_scr, dvt_scr):
        causal = _att_consts()
        upto, before = _sum_matrix("upto"), _sum_matrix("before")

        def prep(j, _):
            k0 = pl.multiple_of(j * TK, TK)
            kb = k_ref[pl.ds(k0, TK), :]
            top, bot = _head_diag(kb.astype(F32).T, True)
            kbd_scr[j] = jnp.concatenate([top, bot], axis=1).astype(BF16)
            top, bot = _head_diag(v_ref[pl.ds(k0, TK), :].astype(F32).T, True)
            vtbd_scr[j] = jnp.concatenate([top, bot], axis=1).astype(BF16)
            left, right = _head_diag(kb, False)
            kbd2_scr[j] = jnp.concatenate([left, right], axis=0)
            dkt_scr[j] = jnp.zeros((LANES, 2 * TK), F32)
            dvt_scr[j] = jnp.zeros((LANES, 2 * TK), F32)
            return 0

        lax.fori_loop(0, nk, prep, 0)

        def block(j, qb, qt, dob, dot_, totb, st, mask):
            f0, f1, p0, p1, dqacc = st
            lb, l1 = _att_logits(qb, kbd_scr[j])
            if mask is not None:
                l1 = jnp.where(mask, l1, 0.0)
            (s0, s1), (r0, r1) = _pair_sums(l1, upto)
            att = jnp.exp(lb + (totb - jnp.concatenate([s0 + f0, s1 + f1], axis=1)))
            if mask is not None:
                att = jnp.where(mask, att, 0.0)
            pw = att * _dot(dob, vtbd_scr[j])
            (e0, e1), (t0, t1) = _pair_sums(pw, before)
            dz = pw - jnp.exp(lb) * (pw + jnp.concatenate([e0 + p0, e1 + p1], axis=1))
            if mask is not None:
                dz = jnp.where(mask, dz, 0.0)
            dzb = dz.astype(BF16)
            dkt_scr[j] += _dot(qt, dzb)
            dvt_scr[j] += _dot(dot_, att.astype(BF16))
            return f0 + r0, f1 + r1, p0 + t0, p1 + t1, dqacc + _dot(dzb, kbd2_scr[j])

        def qloop(qi, _):
            q0 = pl.multiple_of(qi * TQ, TQ)
            qb = _scaled_q(q_ref, q0)
            qt = qb.astype(F32).T.astype(BF16)
            do32 = do_ref[pl.ds(q0, TQ), :]
            dob, dot_ = do32.astype(BF16), do32.T.astype(BF16)
            totb = tot_ref[pl.ds(q0, TQ), :]
            zero = jnp.zeros((TQ, TK), F32)
            st = (zero, zero, zero, zero, jnp.zeros((TQ, LANES), F32))

            k0 = kmin_ref[pl.program_id(0), pl.program_id(1), qi]

            def kloop(it, st):
                j = k0 + 2 * it
                return block(j + 1, qb, qt, dob, dot_, totb, block(j, qb, qt, dob, dot_, totb, st, None), None)

            st = lax.fori_loop(0, ((TQ // TK) * qi - k0) // 2, kloop, st)
            for jj in range(TQ // TK):
                lo = TK * jj
                new = block((TQ // TK) * qi + jj, qb[lo:], qt[:, lo:], dob[lo:], dot_[:, lo:], totb[lo:],
                            tuple(x[lo:] for x in st), causal[:TQ - lo])
                st = tuple(jnp.concatenate([x[:lo], y], axis=0) if lo else y for x, y in zip(st, new))
            dq_ref[pl.ds(q0, TQ), :] = (st[4] * ATT_SCALE).astype(BF16)
            return 0

        lax.fori_loop(0, nq, qloop, 0)

        def finish(j, _):
            k0 = pl.multiple_of(j * TK, TK)
            head0 = lax.broadcasted_iota(jnp.int32, (LANES, TK), 0) < DH
            for src, dst in ((dkt_scr, dk_ref), (dvt_scr, dv_ref)):
                acc = src[j]
                dst[pl.ds(k0, TK), :] = jnp.where(head0, acc[:, :TK], acc[:, TK:]).T.astype(BF16)
            return 0

        lax.fori_loop(0, nk, finish, 0)

    blk = pl.BlockSpec((seq, LANES), lambda e, p: (e, p))
    grad = jax.ShapeDtypeStruct((t, SB_WIDTH), BF16)
    return pl.pallas_call(
        kern, grid=(ne, SB_WIDTH // LANES),
        in_specs=_qkv_specs(seq) + [blk, pl.BlockSpec((seq, 2 * TK), lambda e, p: (e, p)),
                                    pl.BlockSpec(memory_space=pltpu.SMEM)],
        out_specs=[blk, blk, blk], out_shape=[grad, grad, grad],
        scratch_shapes=[pltpu.VMEM((nk, LANES, 2 * TK), BF16), pltpu.VMEM((nk, LANES, 2 * TK), BF16),
                        pltpu.VMEM((nk, 2 * TK, LANES), BF16), pltpu.VMEM((nk, LANES, 2 * TK), F32),
                        pltpu.VMEM((nk, LANES, 2 * TK), F32)],
        compiler_params=_cp(("parallel", "parallel")), name="attn_bwd",
    )(qkv, qkv, qkv, do, tot, kmin)


def _lru_bwd(xl, h, dy, conv_w, conv_b, wbd, ba, bx, lam, seq, comm=None):
    t = xl.shape[0]
    tc = min(512, seq)
    nc = seq // tc
    nb = tc // SUBLANES


    def body(ins, outs, scr):
        u_ref, g_ref, up_ref, h_ref, hp_ref, dy_ref, cw_ref, cb_ref, wbd_ref, ba_ref, bx_ref, lam_ref = ins
        (dxl_ref, small_ref, dwbd_ref), (lnext_ref, anext_ref, dcnext_ref) = outs, scr
        e, ci = pl.program_id(0), pl.program_id(1)
        first = ci == nc - 1

        @pl.when((e == 0) & (ci == 0))
        def _():
            small_ref[...] = jnp.zeros_like(small_ref)
            dwbd_ref[...] = jnp.zeros_like(dwbd_ref)

        @pl.when(ci == 0)
        def _():
            lnext_ref[...] = jnp.zeros_like(lnext_ref)
            anext_ref[...] = jnp.zeros_like(anext_ref)
            dcnext_ref[...] = jnp.zeros_like(dcnext_ref)

        u, g = u_ref[...], g_ref[...]
        keep = jnp.where(first, 0.0, 1.0)
        taps = _conv_taps(keep * up_ref[...], u)
        c = cb_ref[...]
        for k in range(CONV_WIDTH):
            c = c + taps[k] * cw_ref[k:k + 1, :]
        lam = lam_ref[...]
        sp = _softplus(-lam)
        r, i, a, mult = _lru_gates(c, wbd_ref, ba_ref[...], bx_ref[...], sp)
        gel, th = _gelu(g)
        dyv, hv = dy_ref[...], h_ref[...]
        dg = dyv * hv * _gelu_grad(g, th)

        aa, bb = _scan_rev(_shift_up(a, 1, anext_ref[0:1, :]), dyv * gel)
        lt = bb + aa * lnext_ref[0:1, :]
        lnext_ref[0:1, :] = _row_of(lt, 0)
        anext_ref[0:1, :] = _row_of(a, 0)

        hprev = _shift_down(hv, 1, keep * hp_ref[SUBLANES - 1:SUBLANES, :])
        da = lt * hprev
        dmult = lt * i * c
        di = lt * mult * c
        dc = lt * mult * i
        dla = da * a - dmult * (a * a) / mult
        dga = dla * ((-LRU_C) * sp) * r * (1.0 - r)
        dgx = di * i * (1.0 - i)
        small_ref[7:8, :] += jnp.sum(dla * r, axis=0, keepdims=True) * (LRU_C * _sigmoid(-lam))
        small_ref[5:6, :] += jnp.sum(dga, axis=0, keepdims=True)
        small_ref[6:7, :] += jnp.sum(dgx, axis=0, keepdims=True)

        dcs = []
        for p in range(LRU_WIDTH // LANES):
            cols = slice(LANES * p, LANES * (p + 1))
            dgax = jnp.concatenate([dga[:, cols], dgx[:, cols]], axis=1).astype(BF16)
            dcs.append(_dot_nt(dgax, wbd_ref[p]))
            dwbd_ref[p] += _dot_tn(c[:, cols].astype(BF16), dgax)
        dc = dc + jnp.concatenate(dcs, axis=1)
        small_ref[4:5, :] += jnp.sum(dc, axis=0, keepdims=True)

        catd = jnp.concatenate([dc, dcnext_ref[...]], axis=0)
        du = dc * cw_ref[CONV_WIDTH - 1:CONV_WIDTH, :]
        for j in range(1, CONV_WIDTH):
            du = du + pltpu.roll(catd, tc + SUBLANES - j, 0)[:tc] * cw_ref[CONV_WIDTH - 1 - j:CONV_WIDTH - j, :]
        dcnext_ref[...] = dc[:SUBLANES]
        for k in range(CONV_WIDTH):
            small_ref[k:k + 1, :] += jnp.sum(dc * taps[k], axis=0, keepdims=True)
        dxl_ref[:, :LRU_WIDTH] = du.astype(BF16)
        dxl_ref[:, LRU_WIDTH:] = dg.astype(BF16)

    rev = lambda e, c: e * nc + (nc - 1 - c)
    chunk = lambda col: pl.BlockSpec((tc, LRU_WIDTH), lambda e, c: (rev(e, c), col))
    prev8 = pl.BlockSpec((SUBLANES, LRU_WIDTH), lambda e, c: (jnp.maximum(rev(e, c) * nb - 1, 0), 0))
    outs, got = _call(
        body, comm, grid=(t // seq, nc),
        in_specs=[chunk(0), chunk(1), prev8, chunk(0), prev8, chunk(0)] + _lru_param_specs(2),
        out_specs=[pl.BlockSpec((tc, 2 * LRU_WIDTH), lambda e, c: (rev(e, c), 0)),
                   pl.BlockSpec((SUBLANES, LRU_WIDTH), lambda e, c: (0, 0)),
                   pl.BlockSpec((LRU_WIDTH // LANES, LANES, 2 * LANES), lambda e, c: (0, 0, 0))],
        out_shape=[jax.ShapeDtypeStruct((t, 2 * LRU_WIDTH), BF16), jax.ShapeDtypeStruct((SUBLANES, LRU_WIDTH), F32),
                   jax.ShapeDtypeStruct((LRU_WIDTH // LANES, LANES, 2 * LANES), F32)],
        scratch_shapes=[pltpu.VMEM((SUBLANES, LRU_WIDTH), F32)] * 3,
        args=(xl, xl, xl, h, h, dy, conv_w, conv_b, wbd, ba, bx, lam), name="lru_bwd")
    return (*outs, got)


def _adamw(w, g, m, v, name):
    rows, cols = w.shape
    tr = 256 if rows % 256 == 0 else rows

    def kern(w_ref, g_ref, m_ref, v_ref, d_ref, m2_ref, v2_ref):
        gv = g_ref[...]
        m2 = ADAM_B1 * m_ref[...] + (1.0 - ADAM_B1) * gv
        v2 = ADAM_B2 * v_ref[...] + (1.0 - ADAM_B2) * (gv * gv)
        m_hat = m2 / (1.0 - ADAM_B1 ** ADAM_STEP)
        v_hat = v2 / (1.0 - ADAM_B2 ** ADAM_STEP)
        d_ref[...] = -ADAM_LR * (m_hat / (jnp.sqrt(v_hat) + ADAM_EPS) + ADAM_WD * w_ref[...])
        m2_ref[...] = m2
        v2_ref[...] = v2

    blk = pl.BlockSpec((tr, cols), lambda i: (i, 0))
    out = jax.ShapeDtypeStruct((rows, cols), F32)
    return pl.pallas_call(kern, grid=(rows // tr,), in_specs=[blk] * 4, out_specs=[blk] * 3, out_shape=[out] * 3,
                          compiler_params=_cp(("parallel",)), name=name)(w, g, m, v)


def _comm_only(kind, arrays, name):
    return _call(lambda ins, outs, scr: None, (kind, arrays), grid=(1,), in_specs=[], out_specs=[], out_shape=[],
                 scratch_shapes=[], args=(), name=name)[1]


def _pair_add(g, got, core):
    _, rows, cols = g.shape
    half = rows // 2
    tr = min(256, half)
    nt = half // tr

    def kern(c_ref, g_ref, o_ref, out_ref):
        out_ref[...] = (g_ref[...] + o_ref[...]).astype(BF16)

    return pl.pallas_call(
        kern, grid_spec=pltpu.PrefetchScalarGridSpec(
            num_scalar_prefetch=1, grid=(N_CHIPS, nt),
            in_specs=[pl.BlockSpec((None, tr, cols), lambda j, i, c_ref: (j, c_ref[0] * nt + i, 0)),
                      pl.BlockSpec((None, tr, cols), lambda j, i, c_ref: (j, i, 0))],
            out_specs=pl.BlockSpec((None, tr, cols), lambda j, i, c_ref: (j, i, 0))),
        out_shape=jax.ShapeDtypeStruct((N_CHIPS, half, cols), BF16),
        compiler_params=_cp(("parallel", "parallel")), name="pair_add",
    )(core, g, got)


def _chip_add(part, got, place):
    _, half, cols = part.shape
    tr = min(256, half)
    nt = half // tr

    def kern(p_ref, part_ref, got_ref, out_ref):
        out_ref[...] = (part_ref[...].astype(F32) + got_ref[0].astype(F32) + got_ref[1].astype(F32)
                        + got_ref[2].astype(F32))

    return pl.pallas_call(
        kern, grid_spec=pltpu.PrefetchScalarGridSpec(
            num_scalar_prefetch=1, grid=(nt,),
            in_specs=[pl.BlockSpec((None, tr, cols), lambda i, p_ref: (p_ref[0], i, 0)),
                      pl.BlockSpec((3, tr, cols), lambda i, p_ref: (0, i, 0))],
            out_specs=pl.BlockSpec((tr, cols), lambda i, p_ref: (p_ref[1] * nt + i, 0))),
        out_shape=jax.ShapeDtypeStruct((2 * half, cols), F32),
        compiler_params=_cp(("parallel",)), name="chip_add",
    )(place, part, got)


def _finale(packed, fulls):
    rows, n = packed.shape[0], len(fulls)

    def kern(in_ref, *refs):
        ins, out_ref, outs = refs[:n], refs[n], refs[n + 1:2 * n + 1]
        slots, send_sems, recv_sems, join_send, join_recv = refs[2 * n + 1:]
        x, y, c = _place()
        mine = 4 * x + 2 * y + c
        copies = []
        for w in range(n):
            half = ins[w].shape[0] // 2
            rws = pl.ds(c * half, half)
            copies.append(pltpu.make_async_remote_copy(
                src_ref=ins[w].at[rws, :], dst_ref=outs[w].at[rws, :], send_sem=join_send.at[w],
                recv_sem=join_recv.at[w], device_id=(x, y, 1 - c), device_id_type=MESH))
        for k in range(1, N_DEV):
            peer = (x ^ (k >> 2), y ^ ((k >> 1) & 1), c ^ (k & 1))
            copies.append(pltpu.make_async_remote_copy(
                src_ref=in_ref, dst_ref=slots.at[mine], send_sem=send_sems.at[k - 1], recv_sem=recv_sems.at[k - 1],
                device_id=peer, device_id_type=MESH))
        for cp in copies:
            cp.start()
        slots[mine] = in_ref[...]
        for cp in copies:
            cp.wait()
        acc = slots[0]
        for sl in range(1, N_DEV):
            acc = acc + slots[sl]
        out_ref[...] = acc

    vm = pl.BlockSpec(memory_space=pltpu.VMEM)
    out = pl.pallas_call(
        kern, in_specs=[vm] + [ANY] * n, out_specs=[vm] + [ANY] * n,
        out_shape=[jax.ShapeDtypeStruct((rows, LANES), F32)] + [jax.ShapeDtypeStruct(f.shape, f.dtype) for f in fulls],
        input_output_aliases={w + 1: w + 1 for w in range(n)},
        scratch_shapes=[pltpu.VMEM((N_DEV, rows, LANES), F32), pltpu.SemaphoreType.DMA((N_DEV - 1,)),
                        pltpu.SemaphoreType.DMA((N_DEV - 1,)), pltpu.SemaphoreType.DMA((n,)), pltpu.SemaphoreType.DMA((n,))],
        name="finale",
    )(packed, *fulls)
    return out[0], list(out[1:])


SMALL = ["norm1_g", "conv_w", "conv_b", "lru_w_a", "lru_b_a", "lru_w_x", "lru_b_x", "lru_lambda", "lru_out_g", "sb_out_g",
         "norm2_g", "final_g"]
BIG = ["w_in", "w_out", "w_up", "w_down"]
WEIGHTS = ["norm1_g", "w_in", "conv_w", "conv_b", "lru_w_a", "lru_b_a", "lru_w_x", "lru_b_x", "lru_lambda", "lru_out_g",
           "sb_out_g", "w_out", "norm2_g", "w_up", "w_down", "final_g"]


def _pack(arrays):
    flat = []
    for a in arrays:
        a = a.reshape(-1).astype(F32)
        flat.append(jnp.pad(a, (0, (-a.shape[0]) % LANES)))
    v = jnp.concatenate(flat)
    v = jnp.pad(v, (0, (-v.shape[0]) % (LANES * SUBLANES)))
    return v.reshape(-1, LANES)


def _unpack(packed, shapes):
    v, out, off = packed.reshape(-1), [], 0
    for shp in shapes:
        size = math.prod(shp)
        out.append(v[off:off + size].reshape(shp))
        off += size + (-size) % LANES
    return out


def _blockdiag_pairs(w):
    w = w.reshape(4, 2, DH, DH)
    z = jnp.zeros((4, DH, DH), w.dtype)
    return jnp.concatenate([jnp.concatenate([w[:, 0], z], axis=2), jnp.concatenate([z, w[:, 1]], axis=2)], axis=1)


def _blockdiag_unpairs(wbd):
    return jnp.stack([wbd[:, :DH, :DH], wbd[:, DH:, DH:]], axis=1).reshape(8, DH, DH)


def _full_cols(g):
    return jnp.transpose(g, (1, 0, 2)).reshape(g.shape[1], N_CHIPS * g.shape[2])


def _local_step(x2, tgt, seq, norm1_g, w_in, conv_w, conv_b, w_a, b_a, w_x, b_x, lru_lambda, lru_out_g, sb_out_g, rest,
                norm2_g, final_g, place=None):
    alone = place is None
    wbd = jnp.concatenate([_blockdiag_pairs(w_a), _blockdiag_pairs(w_x)], axis=2).astype(BF16)
    ba, bx = b_a.reshape(1, LRU_WIDTH), b_x.reshape(1, LRU_WIDTH)
    gf = final_g.reshape(1, D_MODEL)

    xn, got = _norm1(x2, norm1_g, None if alone else ("gather", [w_in, conv_w]))
    w_in_f, conv_w_f = (w_in, conv_w) if alone else (_full_cols(got[0]), _full_cols(got[1]))
    xl, qkv = _inproj(xn, w_in_f)
    h, y_lru = _lru_fwd(xl, conv_w_f, conv_b, wbd, ba, bx, lru_lambda, seq)
    o, tot, kmin, got = _attn_fwd(qkv, seq, None if alone else ("gather", rest))
    w_out_f, w_up_f, w_down_f = rest if alone else (
        got[0].reshape(D_MODEL, D_MODEL), _full_cols(got[1]), got[2].reshape(D_FF, D_MODEL))
    h1, mix = _outproj(y_lru, o, x2, lru_out_g, sb_out_g, w_out_f)
    h2, hn, up, u2 = _mlp_fwd(h1, norm2_g, w_up_f, w_down_f)
    dh2, dh2b, loss_part, d_final = _loss_head(h2, tgt, gf)

    dpre = _mlp_bwd_pre(dh2b, w_down_f, up)
    g_w_down = _matmul(u2, dh2b, "tn", F32, "dw_down", 1024, 1024, 1024).reshape(N_CHIPS, D_FF // N_CHIPS, D_MODEL)
    g_w_up = _matmul(hn, dpre, "tn", F32, "dw_up", 1024, D_FF // N_CHIPS, 1024, split_cols=True)
    dh1, d_norm2, _ = _proj_bwd_norm([dpre], w_up_f, h1, norm2_g, dh2, "mlp_bwd_in")
    g_w_out = _matmul(mix, dh1, "tn", F32, "dw_out", 1024, 1024, 2048).reshape(N_CHIPS, D_MODEL // N_CHIPS, D_MODEL)
    late = [g_w_out, g_w_up, g_w_down]
    dy_lru, do, d_ga, d_gb, swapped = _outproj_bwd(dh1, w_out_f, y_lru, o, lru_out_g, sb_out_g,
                                                   None if alone else ("swap", late))
    parts = None if alone else [_pair_add(g, r, place[1:]) for g, r in zip(late, swapped)]
    dq, dk, dv = _attn_bwd(qkv, do, tot, kmin, seq)
    dxl, lru_small, d_wbd, got = _lru_bwd(xl, h, dy_lru, conv_w_f, conv_b, wbd, ba, bx, lru_lambda, seq,
                                          None if alone else ("exchange", parts))
    if not alone:
        late = [_chip_add(p, r, place) for p, r in zip(parts, got)]
    dproj = jnp.concatenate([dxl, dq, dk, dv], axis=1)
    g_w_in = _matmul(xn, dproj, "tn", F32, "dw_in", 1024, IN_COLS // N_CHIPS, 2048, split_cols=True)
    part = None if alone else _pair_add(g_w_in, _comm_only("swap", [g_w_in], "pair_swap")[0], place[1:])
    dx, d_norm1, got = _proj_bwd_norm([dxl, dq, dk, dv], w_in_f, x2, norm1_g, dh1, "inproj_bwd",
                                      None if alone else ("exchange", [part]))
    if not alone:
        g_w_in = _chip_add(part, got[0], place)
    small_parts = {
        "norm1_g": d_norm1, "conv_w": lru_small[:CONV_WIDTH], "conv_b": lru_small[4:5],
        "lru_w_a": _blockdiag_unpairs(d_wbd[:, :, :LANES]), "lru_b_a": lru_small[5:6],
        "lru_w_x": _blockdiag_unpairs(d_wbd[:, :, LANES:]), "lru_b_x": lru_small[6:7], "lru_lambda": lru_small[7:8],
        "lru_out_g": d_ga, "sb_out_g": d_gb, "norm2_g": d_norm2, "final_g": d_final,
    }
    return loss_part, dx, [g_w_in] + late, small_parts


def kernel(x, norm1_g, w_in, conv_w, conv_b, lru_w_a, lru_b_a, lru_w_x, lru_b_x, lru_lambda, lru_out_g, sb_out_g, w_out, norm2_g, w_up, w_down, final_g, loss_target, m_norm1_g, m_w_in, m_conv_w, m_conv_b, m_lru_w_a, m_lru_b_a, m_lru_w_x, m_lru_b_x, m_lru_lambda, m_lru_out_g, m_sb_out_g, m_w_out, m_norm2_g, m_w_up, m_w_down, m_final_g, v_norm1_g, v_w_in, v_conv_w, v_conv_b, v_lru_w_a, v_lru_b_a, v_lru_w_x, v_lru_b_x, v_lru_lambda, v_lru_out_g, v_sb_out_g, v_w_out, v_norm2_g, v_w_up, v_w_down, v_final_g):
    given = dict(locals())
    ne, seq, _ = x.shape
    t = ne * seq
    xi, yi, ci = _place()
    place = jnp.stack([2 * xi + yi, ci]).astype(jnp.int32)

    loss_part, dx, halves, small_parts = _local_step(
        x.reshape(t, D_MODEL), loss_target.reshape(t, D_MODEL), seq, norm1_g, w_in[0].astype(BF16), conv_w[0], conv_b,
        lru_w_a[0], lru_b_a, lru_w_x[0], lru_b_x, lru_lambda, lru_out_g, sb_out_g,
        [w_out[0].astype(BF16), w_up[0].astype(BF16), w_down[0].astype(BF16)], norm2_g, final_g, place)

    full_shapes = {n: ((CONV_WIDTH, LRU_WIDTH) if n == "conv_w" else given[n].shape) for n in SMALL}
    red, fulls = _finale(_pack([small_parts[n] for n in SMALL] + [loss_part]), halves)
    red_list = _unpack(red, [full_shapes[n] for n in SMALL] + [(1, LANES)])
    grads = dict(zip(SMALL, red_list[:-1]))
    loss = red_list[-1][0, 0]
    grads["conv_w"] = lax.dynamic_slice_in_dim(grads["conv_w"], place[0] * (LRU_WIDTH // N_CHIPS), LRU_WIDTH // N_CHIPS,
                                               axis=1).reshape(conv_w.shape)
    for n, full in zip(BIG, fulls):
        grads[n] = full.reshape(given[n].shape)

    delta, new_m, new_v = {}, {}, {}
    for n in BIG:
        shp = given[n].shape
        d, m2, v2 = _adamw(given[n][0], grads[n][0], given["m_" + n][0], given["v_" + n][0], "adamw_" + n)
        delta[n], new_m[n], new_v[n] = d.reshape(shp), m2.reshape(shp), v2.reshape(shp)
    shapes = [given[n].shape for n in SMALL]
    d, m2, v2 = _adamw(_pack([given[n] for n in SMALL]), _pack([grads[n] for n in SMALL]),
                       _pack([given["m_" + n] for n in SMALL]), _pack([given["v_" + n] for n in SMALL]), "adamw_small")
    for n, dd, mm, vv in zip(SMALL, _unpack(d, shapes), _unpack(m2, shapes), _unpack(v2, shapes)):
        delta[n], new_m[n], new_v[n] = dd, mm, vv

    return (loss, dx.reshape(x.shape), *[grads[n] for n in WEIGHTS], *[delta[n] for n in WEIGHTS],
            *[new_m[n] for n in WEIGHTS], *[new_v[n] for n in WEIGHTS])
```

```python
import functools
import math

import jax
import jax.numpy as jnp
from jax import lax
from jax.experimental import pallas as pl
from jax.experimental.pallas import tpu as pltpu

F32, BF16 = jnp.float32, jnp.bfloat16
MESH = pl.DeviceIdType.MESH

D_MODEL = 1024
LRU_WIDTH = 512
SB_WIDTH = 512
DH = 64
IN_COLS = 2 * LRU_WIDTH + 3 * SB_WIDTH
D_FF = 4 * D_MODEL
CONV_WIDTH = 4
LRU_C = 8.0
EPS = 1e-6
N_CHIPS = 4
N_DEV = 8
LANES = 128
SUBLANES = 8
TQ = 512
TK = 128
ATT_SCALE = 1.0 / math.sqrt(DH)
SKIP_LOG = -105.0
VMEM_LIMIT = 52 * 1024 * 1024
VMEM_LIMIT_BIG = 62 * 1024 * 1024

ADAM_LR, ADAM_B1, ADAM_B2, ADAM_EPS, ADAM_WD, ADAM_STEP = 0.001, 0.9, 0.999, 1e-08, 0.01, 10

_GELU_K = math.sqrt(2.0 / math.pi)
_GELU_C = 0.044715


def _cp(sem, vmem=VMEM_LIMIT):
    return pltpu.CompilerParams(dimension_semantics=sem, vmem_limit_bytes=vmem)


def _dot(a, b):
    return jnp.dot(a, b, preferred_element_type=F32)


def _dot_nt(a, b):
    return lax.dot_general(a, b, (((1,), (1,)), ((), ())), preferred_element_type=F32)


def _dot_tn(a, b):
    return lax.dot_general(a, b, (((0,), (0,)), ((), ())), preferred_element_type=F32)


def _rstd(x):
    return lax.rsqrt(jnp.mean(x * x, axis=-1, keepdims=True) + EPS)


def _rms_bwd(x, g, dy):
    r = _rstd(x)
    gd = g * dy
    dx = r * gd - x * (r * r * r) * jnp.mean(x * gd, axis=-1, keepdims=True)
    return dx, jnp.sum(dy * x * r, axis=0, keepdims=True)


def _sigmoid(x):
    return 0.5 * jnp.tanh(0.5 * x) + 0.5


def _softplus(x):
    return jnp.maximum(x, 0.0) + jnp.log(1.0 + jnp.exp(-jnp.abs(x)))


def _neg_expm1(x, ex):
    series = -x * (1.0 + x * (0.5 + x * (1.0 / 6.0)))
    return jnp.where(x > -2.0 ** -7, series, 1.0 - ex)


def _gelu(g):
    t = jnp.tanh(_GELU_K * (g + _GELU_C * g * g * g))
    return 0.5 * g * (1.0 + t), t


def _gelu_grad(g, t):
    return 0.5 * (1.0 + t) + 0.5 * g * (1.0 - t * t) * _GELU_K * (1.0 + 3.0 * _GELU_C * g * g)


def _rows(shape):
    return lax.broadcasted_iota(jnp.int32, shape, 0)


def _shift_down(x, s, fill):
    n, c = x.shape
    if s % SUBLANES == 0:
        return jnp.concatenate([jnp.broadcast_to(jnp.asarray(fill, x.dtype), (s, c)), x[:n - s]], axis=0)
    return jnp.where(_rows(x.shape) >= s, pltpu.roll(x, s, 0), fill)


def _shift_up(x, s, fill):
    n, c = x.shape
    if s % SUBLANES == 0:
        return jnp.concatenate([x[s:], jnp.broadcast_to(jnp.asarray(fill, x.dtype), (s, c))], axis=0)
    return jnp.where(_rows(x.shape) < n - s, pltpu.roll(x, n - s, 0), fill)


def _row_of(x, idx):
    return jnp.sum(jnp.where(_rows(x.shape) == idx, x, 0.0), axis=0, keepdims=True)


def _matmul(a, b, dims, out_dtype, name, tm, tn, tk, split_cols=False):
    if dims == "nn":
        (m, kk), n, dot = a.shape, b.shape[1], _dot
    elif dims == "nt":
        (m, kk), n, dot = a.shape, b.shape[0], _dot_nt
    else:
        (kk, m), n, dot = a.shape, b.shape[1], _dot_tn
    tm, tn, tk = min(tm, m), min(tn, n), min(tk, kk)
    assert m % tm == 0 and n % tn == 0 and kk % tk == 0, (name, m, n, kk)
    nk = kk // tk

    def kern(a_ref, b_ref, o_ref, acc_ref):
        k = pl.program_id(2)

        @pl.when(k == 0)
        def _():
            acc_ref[...] = jnp.zeros_like(acc_ref)

        acc_ref[...] += dot(a_ref[...].astype(BF16), b_ref[...].astype(BF16))

        @pl.when(k == nk - 1)
        def _():
            o_ref[...] = acc_ref[...].astype(o_ref.dtype)

    if split_cols:
        out_shape = jax.ShapeDtypeStruct((n // tn, m, tn), out_dtype)
        o_spec = pl.BlockSpec((None, tm, tn), lambda i, j, k: (j, i, 0))
    else:
        out_shape = jax.ShapeDtypeStruct((m, n), out_dtype)
        o_spec = pl.BlockSpec((tm, tn), lambda i, j, k: (i, j))
    if dims == "nn":
        a_spec = pl.BlockSpec((tm, tk), lambda i, j, k: (i, k))
        b_spec = pl.BlockSpec((tk, tn), lambda i, j, k: (k, j))
    elif dims == "nt":
        a_spec = pl.BlockSpec((tm, tk), lambda i, j, k: (i, k))
        b_spec = pl.BlockSpec((tn, tk), lambda i, j, k: (j, k))
    else:
        a_spec = pl.BlockSpec((tk, tm), lambda i, j, k: (k, i))
        b_spec = pl.BlockSpec((tk, tn), lambda i, j, k: (k, j))
    return pl.pallas_call(
        kern, grid=(m // tm, n // tn, nk), in_specs=[a_spec, b_spec], out_specs=o_spec, out_shape=out_shape,
        scratch_shapes=[pltpu.VMEM((tm, tn), F32)], compiler_params=_cp(("parallel", "parallel", "arbitrary")), name=name,
    )(a, b)


ANY = pl.BlockSpec(memory_space=pl.ANY)


def _place():
    return lax.axis_index("x"), lax.axis_index("y"), lax.axis_index("c")


def _other_chips(x, y):
    return [(1 - x, y), (x, 1 - y), (1 - x, 1 - y)]


def _gather_copies(ins, outs, send_sems, recv_sems, loc_sems):
    x, y, c = _place()
    mine = 2 * x + y
    copies = []
    for w in range(len(ins)):
        copies.append(pltpu.make_async_copy(ins[w], outs[w].at[mine], loc_sems.at[w]))
        for k, chip in enumerate(_other_chips(x, y)):
            copies.append(pltpu.make_async_remote_copy(
                src_ref=ins[w], dst_ref=outs[w].at[mine], send_sem=send_sems.at[3 * w + k],
                recv_sem=recv_sems.at[3 * w + k], device_id=(*chip, c), device_id_type=MESH))
    return copies


def _gather_shapes(shards):
    return ([jax.ShapeDtypeStruct((N_CHIPS,) + s.shape, s.dtype) for s in shards],
            [pltpu.SemaphoreType.DMA((3 * len(shards),)), pltpu.SemaphoreType.DMA((3 * len(shards),)),
             pltpu.SemaphoreType.DMA((len(shards),))])


def _exchange_copies(ins, outs, send_sems, recv_sems):
    x, y, c = _place()
    copies = []
    for w in range(len(ins)):
        for k, chip in enumerate(_other_chips(x, y)):
            copies.append(pltpu.make_async_remote_copy(
                src_ref=ins[w].at[2 * chip[0] + chip[1]], dst_ref=outs[w].at[k], send_sem=send_sems.at[3 * w + k],
                recv_sem=recv_sems.at[3 * w + k], device_id=(*chip, c), device_id_type=MESH))
    return copies


def _exchange_shapes(parts):
    return ([jax.ShapeDtypeStruct((3,) + p.shape[1:], p.dtype) for p in parts],
            [pltpu.SemaphoreType.DMA((3 * len(parts),)), pltpu.SemaphoreType.DMA((3 * len(parts),))])


def _swap_copies(ins, outs, send_sems, recv_sems):
    x, y, c = _place()
    copies = []
    for w in range(len(ins)):
        half = ins[w].shape[1] // 2
        copies.append(pltpu.make_async_remote_copy(
            src_ref=ins[w].at[:, pl.ds((1 - c) * half, half), :], dst_ref=outs[w], send_sem=send_sems.at[w],
            recv_sem=recv_sems.at[w], device_id=(x, y, 1 - c), device_id_type=MESH))
    return copies


def _swap_shapes(grads):
    return ([jax.ShapeDtypeStruct((g.shape[0], g.shape[1] // 2, g.shape[2]), g.dtype) for g in grads],
            [pltpu.SemaphoreType.DMA((len(grads),)), pltpu.SemaphoreType.DMA((len(grads),))])


COMM = {"gather": (_gather_copies, _gather_shapes), "exchange": (_exchange_copies, _exchange_shapes),
        "swap": (_swap_copies, _swap_shapes)}


def _call(body, comm, *, grid, in_specs, out_specs, out_shape, scratch_shapes, args, name):
    ni, no, ns = len(in_specs), len(out_specs), len(scratch_shapes)
    arrays = list(comm[1]) if comm else []
    nc = len(arrays)
    c_shapes, c_sems = COMM[comm[0]][1](arrays) if comm else ([], [])

    def kern(*refs):
        ins, cin, outs = refs[:ni], refs[ni:ni + nc], refs[ni + nc:ni + nc + no]
        cout, scr, sems = refs[ni + nc + no:ni + 2 * nc + no], refs[ni + 2 * nc + no:ni + 2 * nc + no + ns], refs[ni + 2 * nc + no + ns:]
        ids = [pl.program_id(d) for d in range(len(grid))]
        if nc:
            @pl.when(functools.reduce(lambda a, b: a & b, [i == 0 for i in ids]))
            def _():
                for cp in COMM[comm[0]][0](cin, cout, *sems):
                    cp.start()

        body(ins, outs, scr)
        if nc:
            @pl.when(functools.reduce(lambda a, b: a & b, [i == g - 1 for i, g in zip(ids, grid)]))
            def _():
                for cp in COMM[comm[0]][0](cin, cout, *sems):
                    cp.wait()

    out = pl.pallas_call(
        kern, grid=grid, in_specs=list(in_specs) + [ANY] * nc, out_specs=list(out_specs) + [ANY] * nc,
        out_shape=list(out_shape) + c_shapes, scratch_shapes=list(scratch_shapes) + c_sems,
        compiler_params=_cp(("arbitrary",) * len(grid)), name=name,
    )(*args, *arrays)
    return list(out[:no]), list(out[no:])


def _resident(shape):
    return pl.BlockSpec(shape, lambda *_: (0,) * len(shape), pipeline_mode=pl.Buffered(1))


def _norm1(x, g1, comm):
    t = x.shape[0]
    tm = min(1024, t)

    def body(ins, outs, _):
        xv = ins[0][...]
        outs[0][...] = (xv * _rstd(xv) * ins[1][...]).astype(BF16)

    row = pl.BlockSpec((tm, D_MODEL), lambda i: (i, 0))
    (xn,), got = _call(body, comm, grid=(t // tm,), in_specs=[row, pl.BlockSpec((1, D_MODEL), lambda i: (0, 0))],
                       out_specs=[row], out_shape=[jax.ShapeDtypeStruct((t, D_MODEL), BF16)], scratch_shapes=[],
                       args=(x, g1), name="norm1")
    return xn, got


def _inproj(xn, w_in):
    t = xn.shape[0]
    tm = min(512, t)

    def kern(x_ref, w_ref, xl_ref, qkv_ref):
        xn_v = x_ref[...]
        xl_ref[...] = _dot(xn_v, w_ref[:, : 2 * LRU_WIDTH])
        qkv_ref[...] = _dot(xn_v, w_ref[:, 2 * LRU_WIDTH:]).astype(BF16)

    row = lambda c: pl.BlockSpec((tm, c), lambda i: (i, 0))
    return pl.pallas_call(
        kern, grid=(t // tm,), in_specs=[row(D_MODEL), _resident((D_MODEL, IN_COLS))],
        out_specs=[row(2 * LRU_WIDTH), row(3 * SB_WIDTH)],
        out_shape=[jax.ShapeDtypeStruct((t, 2 * LRU_WIDTH), F32), jax.ShapeDtypeStruct((t, 3 * SB_WIDTH), BF16)],
        compiler_params=_cp(("parallel",)), name="inproj",
    )(xn, w_in)


def _conv_taps(hist, u):
    cat = jnp.concatenate([hist, u], axis=0)
    return [pltpu.roll(cat, CONV_WIDTH - 1 - k, 0)[SUBLANES:] for k in range(CONV_WIDTH - 1)] + [u]


def _lru_gates(c, wbd_ref, ba, bx, sp):
    gas, gxs = [], []
    for p in range(LRU_WIDTH // LANES):
        gax = _dot(c[:, LANES * p: LANES * (p + 1)].astype(BF16), wbd_ref[p])
        gas.append(gax[:, :LANES])
        gxs.append(gax[:, LANES:])
    r = _sigmoid(jnp.concatenate(gas, axis=1) + ba)
    i = _sigmoid(jnp.concatenate(gxs, axis=1) + bx)
    la = (-LRU_C) * r * sp
    a = jnp.exp(la)
    e2 = _neg_expm1(2.0 * la, a * a)
    inv_mult = lax.rsqrt(jnp.maximum(e2, 1e-30))
    return r, i, a, e2 * inv_mult, inv_mult


def _scan_fwd(a, b):
    s = 1
    while s < a.shape[0]:
        b = b + a * _shift_down(b, s, 0.0)
        a = a * _shift_down(a, s, 1.0)
        s *= 2
    return a, b


def _scan_rev(a, b):
    s = 1
    while s < a.shape[0]:
        b = b + a * _shift_up(b, s, 0.0)
        a = a * _shift_up(a, s, 1.0)
        s *= 2
    return a, b


def _lru_param_specs(grid_rank):
    z2 = (lambda e, c: (0, 0)) if grid_rank == 2 else None
    return [
        pl.BlockSpec((CONV_WIDTH, LRU_WIDTH), z2), pl.BlockSpec((1, LRU_WIDTH), z2),
        pl.BlockSpec((LRU_WIDTH // LANES, LANES, 2 * LANES), lambda e, c: (0, 0, 0)),
        pl.BlockSpec((1, LRU_WIDTH), z2), pl.BlockSpec((1, LRU_WIDTH), z2), pl.BlockSpec((1, LRU_WIDTH), z2),
    ]


def _lru_fwd(xl, conv_w, conv_b, wbd, ba, bx, lam, seq):
    t = xl.shape[0]
    tc = min(512, seq)
    nc = seq // tc

    def kern(u_ref, g_ref, cw_ref, cb_ref, wbd_ref, ba_ref, bx_ref, lam_ref, h_ref, y_ref, hist_ref, hcar_ref):
        @pl.when(pl.program_id(1) == 0)
        def _():
            hist_ref[...] = jnp.zeros_like(hist_ref)
            hcar_ref[...] = jnp.zeros_like(hcar_ref)

        u = u_ref[...]
        taps = _conv_taps(hist_ref[...], u)
        hist_ref[...] = u_ref[tc - SUBLANES:, :]
        c = cb_ref[...]
        for k in range(CONV_WIDTH):
            c = c + taps[k] * cw_ref[k:k + 1, :]
        sp = _softplus(-lam_ref[...])
        _, i, a, mult, _ = _lru_gates(c, wbd_ref, ba_ref[...], bx_ref[...], sp)
        aa, bb = _scan_fwd(a, mult * i * c)
        h = bb + aa * hcar_ref[0:1, :]
        h_ref[...] = h
        hcar_ref[0:1, :] = h_ref[tc - 1:tc, :]
        y_ref[...] = h * _gelu(g_ref[...])[0]

    chunk = lambda col: pl.BlockSpec((tc, LRU_WIDTH), lambda e, c: (e * nc + c, col))
    out = jax.ShapeDtypeStruct((t, LRU_WIDTH), F32)
    return pl.pallas_call(
        kern, grid=(t // seq, nc), in_specs=[chunk(0), chunk(1)] + _lru_param_specs(2),
        out_specs=[chunk(0), chunk(0)], out_shape=[out, out],
        scratch_shapes=[pltpu.VMEM((SUBLANES, LRU_WIDTH), F32), pltpu.VMEM((SUBLANES, LRU_WIDTH), F32)],
        compiler_params=_cp(("arbitrary", "arbitrary")), name="lru_fwd",
    )(xl, xl, conv_w, conv_b, wbd, ba, bx, lam)


def _att_consts():
    row = lax.broadcasted_iota(jnp.int32, (TQ, 2 * TK), 0)
    key = lax.broadcasted_iota(jnp.int32, (TQ, 2 * TK), 1) & (TK - 1)
    return key < row


def _sum_matrix(kind):
    j = lax.broadcasted_iota(jnp.int32, (2 * TK, 2 * TK), 0) & (TK - 1)
    s = lax.broadcasted_iota(jnp.int32, (2 * TK, 2 * TK), 1)
    pick = {"after": j > s, "upto": j <= s, "before": j < s}[kind]
    return jnp.where((s >= TK) | pick, 1.0, 0.0).astype(BF16)


def _hi_lo(x):
    hi = x.astype(BF16)
    return hi, (x - hi.astype(F32)).astype(BF16)


def _pair_sums(x, m):
    hi, lo = _hi_lo(x)
    out = []
    for hd in range(2):
        cols = slice(hd * TK, (hd + 1) * TK)
        out.append(_dot(jnp.concatenate([hi[:, cols], lo[:, cols]], axis=1), m))
    return [o[:, :TK] for o in out], [o[:, TK:] for o in out]


def _att_logits(qb, kbd):
    z = _dot(qb, kbd)
    lg = jnp.log(1.0 + jnp.exp(-jnp.abs(z)))
    lb = jnp.minimum(z, 0.0) - lg
    return lb, lb - z


def _head_diag(x, rows_first):
    n = x.shape[0] if rows_first else x.shape[1]
    idx = lax.broadcasted_iota(jnp.int32, x.shape, 0 if rows_first else 1)
    return jnp.where(idx < n // 2, x, 0), jnp.where(idx >= n // 2, x, 0)


def _scaled_q(q_ref, q0):
    return (q_ref[pl.ds(q0, TQ), :].astype(F32) * ATT_SCALE).astype(BF16)


def _qkv_specs(seq):
    n = SB_WIDTH // LANES
    return [pl.BlockSpec((seq, LANES), lambda e, p, off=off: (e, off * n + p)) for off in range(3)]


def _attn_fwd(qkv, seq, comm=None):
    t = qkv.shape[0]
    ne, nq, nk = t // seq, seq // TQ, seq // TK

    def body(ins, outs, scr):
        (q_ref, k_ref, v_ref), (o_ref, tot_ref, kmin_ref), (kbd_scr, vbd_scr) = ins, outs, scr
        causal = _att_consts()
        after = _sum_matrix("after")

        def prep(j, _):
            k0 = pl.multiple_of(j * TK, TK)
            top, bot = _head_diag(k_ref[pl.ds(k0, TK), :].astype(F32).T, True)
            kbd_scr[j] = jnp.concatenate([top, bot], axis=1).astype(BF16)
            left, right = _head_diag(v_ref[pl.ds(k0, TK), :], False)
            vbd_scr[j] = jnp.concatenate([left, right], axis=0)
            return 0

        lax.fori_loop(0, nk, prep, 0)

        def block(j, qb, st, mask):
            c0, c1, oacc = st
            lb, l1 = _att_logits(qb, kbd_scr[j])
            if mask is not None:
                l1 = jnp.where(mask, l1, 0.0)
            (s0, s1), (r0, r1) = _pair_sums(l1, after)
            att = jnp.exp(lb + jnp.concatenate([s0 + c0, s1 + c1], axis=1))
            if mask is not None:
                att = jnp.where(mask, att, 0.0)
            return c0 + r0, c1 + r1, oacc + _dot(att.astype(BF16), vbd_scr[j])

        def qloop(qi, _):
            q0 = pl.multiple_of(qi * TQ, TQ)
            qb = _scaled_q(q_ref, q0)
            zero = jnp.zeros((TQ, TK), F32)
            st = (zero, zero, jnp.zeros((TQ, LANES), F32))
            for jj in reversed(range(TQ // TK)):
                lo = TK * jj
                new = block((TQ // TK) * qi + jj, qb[lo:], tuple(x[lo:] for x in st), causal[:TQ - lo])
                st = tuple(jnp.concatenate([x[:lo], y], axis=0) if lo else y for x, y in zip(st, new))

            npair = (TQ // TK // 2) * qi

            def more(its):
                return (its[0] < npair) & (jnp.max(jnp.maximum(its[1], its[2])) > SKIP_LOG)

            def kloop(its):
                j = 2 * (npair - its[0]) - 1
                return (its[0] + 1,) + block(j - 1, qb, block(j, qb, its[1:], None), None)

            done, c0, c1, oacc = lax.while_loop(more, kloop, (jnp.int32(0),) + st)
            o_ref[pl.ds(q0, TQ), :] = oacc
            tot_ref[pl.ds(q0, TQ), :] = jnp.concatenate([c0, c1], axis=1)
            kmin_ref[pl.program_id(0), pl.program_id(1), qi] = 2 * (npair - done)
            return 0

        lax.fori_loop(0, nq, qloop, 0)

    (o, tot, kmin), got = _call(
        body, comm, grid=(ne, SB_WIDTH // LANES), in_specs=_qkv_specs(seq),
        out_specs=[pl.BlockSpec((seq, LANES), lambda e, p: (e, p)), pl.BlockSpec((seq, 2 * TK), lambda e, p: (e, p)),
                   pl.BlockSpec(memory_space=pltpu.SMEM)],
        out_shape=[jax.ShapeDtypeStruct((t, SB_WIDTH), F32), jax.ShapeDtypeStruct((t, 2 * TK * SB_WIDTH // LANES), F32),
                   jax.ShapeDtypeStruct((ne, SB_WIDTH // LANES, nq), jnp.int32)],
        scratch_shapes=[pltpu.VMEM((nk, LANES, 2 * TK), BF16), pltpu.VMEM((nk, 2 * TK, LANES), BF16)],
        args=(qkv, qkv, qkv), name="attn_fwd")
    return o, tot, kmin, got


def _outproj(y_lru, o, x, ga, gb, w_out):
    t = x.shape[0]
    tm = min(512, t)

    def kern(y_ref, o_ref, x_ref, ga_ref, gb_ref, w_ref, h1_ref, mix_ref):
        yv, ov = y_ref[...], o_ref[...]
        mix = jnp.concatenate([yv * _rstd(yv) * ga_ref[...], ov * _rstd(ov) * gb_ref[...]], axis=1).astype(BF16)
        mix_ref[...] = mix
        h1_ref[...] = x_ref[...] + _dot(mix, w_ref[...])

    row = lambda c: pl.BlockSpec((tm, c), lambda i: (i, 0))
    vec = lambda c: pl.BlockSpec((1, c), lambda i: (0, 0))
    return pl.pallas_call(
        kern, grid=(t // tm,),
        in_specs=[row(LRU_WIDTH), row(SB_WIDTH), row(D_MODEL), vec(LRU_WIDTH), vec(SB_WIDTH),
                  pl.BlockSpec((D_MODEL, D_MODEL), lambda i: (0, 0))],
        out_specs=[row(D_MODEL), row(D_MODEL)],
        out_shape=[jax.ShapeDtypeStruct((t, D_MODEL), F32), jax.ShapeDtypeStruct((t, D_MODEL), BF16)],
        compiler_params=_cp(("parallel",)), name="outproj",
    )(y_lru, o, x, ga, gb, w_out)


def _mlp_loss(h1, g2, w_up, w_down, target, gf):
    t = h1.shape[0]
    tm, tf = min(512, t), 1024

    def kern(h1_ref, g_ref, wu_ref, wd_ref, t_ref, gf_ref, hn_ref, up_ref, u2_ref, dh_ref, dhb_ref, loss_ref, dg_ref):
        @pl.when(pl.program_id(0) == 0)
        def _():
            loss_ref[...] = jnp.zeros_like(loss_ref)
            dg_ref[...] = jnp.zeros_like(dg_ref)

        hv = h1_ref[...]
        hn = (hv * _rstd(hv) * g_ref[...]).astype(BF16)
        hn_ref[...] = hn
        h2 = hv
        for f in range(D_FF // tf):
            cols = slice(f * tf, (f + 1) * tf)
            up = jnp.maximum(_dot(hn, wu_ref[:, cols]), 0.0)
            u2 = (up * up).astype(BF16)
            up_ref[:, cols] = up.astype(BF16)
            u2_ref[:, cols] = u2
            h2 = h2 + _dot(u2, wd_ref[cols, :])

        g = gf_ref[...]
        err = h2 * _rstd(h2) * g - t_ref[...]
        lane = lax.broadcasted_iota(jnp.int32, (1, LANES), 1)
        loss_ref[...] += jnp.where(lane == 0, 0.5 * jnp.sum(err * err) / D_MODEL, 0.0)
        dx, dg = _rms_bwd(h2, g, err * (1.0 / D_MODEL))
        dh_ref[...] = dx
        dhb_ref[...] = dx.astype(BF16)
        dg_ref[...] += dg

    row = lambda c: pl.BlockSpec((tm, c), lambda i: (i, 0))
    vec = pl.BlockSpec((1, D_MODEL), lambda i: (0, 0))
    return pl.pallas_call(
        kern, grid=(t // tm,),
        in_specs=[row(D_MODEL), vec, _resident((D_MODEL, D_FF)), _resident((D_FF, D_MODEL)), row(D_MODEL), vec],
        out_specs=[row(D_MODEL), row(D_FF), row(D_FF), row(D_MODEL), row(D_MODEL), pl.BlockSpec((1, LANES), lambda i: (0, 0)), vec],
        out_shape=[jax.ShapeDtypeStruct((t, D_MODEL), BF16), jax.ShapeDtypeStruct((t, D_FF), BF16),
                   jax.ShapeDtypeStruct((t, D_FF), BF16), jax.ShapeDtypeStruct((t, D_MODEL), F32),
                   jax.ShapeDtypeStruct((t, D_MODEL), BF16), jax.ShapeDtypeStruct((1, LANES), F32),
                   jax.ShapeDtypeStruct((1, D_MODEL), F32)],
        compiler_params=_cp(("arbitrary",), VMEM_LIMIT_BIG), name="mlp_loss",
    )(h1, g2, w_up, w_down, target, gf)


def _mlp_bwd_pre(dh2, w_down, up):
    t = dh2.shape[0]
    tm, tf = min(512, t), 1024

    def kern(d_ref, w_ref, up_ref, o_ref):
        dv = d_ref[...]
        for f in range(D_FF // tf):
            cols = slice(f * tf, (f + 1) * tf)
            o_ref[:, cols] = (_dot_nt(dv, w_ref[cols, :]) * (2.0 * up_ref[:, cols].astype(F32))).astype(BF16)

    row = lambda c: pl.BlockSpec((tm, c), lambda i: (i, 0))
    return pl.pallas_call(
        kern, grid=(t // tm,), in_specs=[row(D_MODEL), _resident((D_FF, D_MODEL)), row(D_FF)],
        out_specs=row(D_FF), out_shape=jax.ShapeDtypeStruct((t, D_FF), BF16),
        compiler_params=_cp(("parallel",)), name="mlp_bwd_pre",
    )(dh2, w_down, up)


def _proj_bwd_norm(dys, w, x, g, resid, name, comm=None):
    t = x.shape[0]
    tm = min(512, t)
    widths = [dy.shape[1] for dy in dys]
    n = len(dys)

    def body(ins, outs, _):
        dy_refs, (w_ref, x_ref, g_ref, r_ref), (dx_ref, dg_ref) = ins[:n], ins[n:], outs

        @pl.when(pl.program_id(0) == 0)
        def _():
            dg_ref[...] = jnp.zeros_like(dg_ref)

        off, dxn = 0, None
        for dy_ref, wd in zip(dy_refs, widths):
            part = _dot_nt(dy_ref[...], w_ref[:, off:off + wd])
            dxn = part if dxn is None else dxn + part
            off += wd
        dx, dg = _rms_bwd(x_ref[...], g_ref[...], dxn)
        dx_ref[...] = r_ref[...] + dx
        dg_ref[...] += dg

    row = lambda c: pl.BlockSpec((tm, c), lambda i: (i, 0))
    vec = pl.BlockSpec((1, D_MODEL), lambda i: (0, 0))
    (dx, dg), got = _call(
        body, comm, grid=(t // tm,), in_specs=[row(wd) for wd in widths] + [_resident(w.shape), row(D_MODEL), vec, row(D_MODEL)],
        out_specs=[row(D_MODEL), vec],
        out_shape=[jax.ShapeDtypeStruct((t, D_MODEL), F32), jax.ShapeDtypeStruct((1, D_MODEL), F32)],
        scratch_shapes=[], args=(*dys, w, x, g, resid), name=name)
    return dx, dg, got


def _outproj_bwd(dh1, w_out, y_lru, o, ga, gb, comm=None):
    t = dh1.shape[0]
    tm = min(512, t)

    def body(ins, outs, _):
        (d_ref, w_ref, y_ref, o_ref, ga_ref, gb_ref), (dy_ref, do_ref, dga_ref, dgb_ref) = ins, outs

        @pl.when(pl.program_id(0) == 0)
        def _():
            dga_ref[...] = jnp.zeros_like(dga_ref)
            dgb_ref[...] = jnp.zeros_like(dgb_ref)

        dmix = _dot_nt(d_ref[...].astype(BF16), w_ref[...])
        dy, dga = _rms_bwd(y_ref[...], ga_ref[...], dmix[:, :LRU_WIDTH])
        do, dgb = _rms_bwd(o_ref[...], gb_ref[...], dmix[:, LRU_WIDTH:])
        dy_ref[...] = dy
        do_ref[...] = do
        dga_ref[...] += dga
        dgb_ref[...] += dgb

    row = lambda c: pl.BlockSpec((tm, c), lambda i: (i, 0))
    vec = pl.BlockSpec((1, LRU_WIDTH), lambda i: (0, 0))
    half = jax.ShapeDtypeStruct((t, LRU_WIDTH), F32)
    gsum = jax.ShapeDtypeStruct((1, LRU_WIDTH), F32)
    outs, got = _call(body, comm, grid=(t // tm,),
                      in_specs=[row(D_MODEL), _resident((D_MODEL, D_MODEL)), row(LRU_WIDTH), row(SB_WIDTH), vec, vec],
                      out_specs=[row(LRU_WIDTH), row(SB_WIDTH), vec, vec], out_shape=[half, half, gsum, gsum],
                      scratch_shapes=[], args=(dh1, w_out, y_lru, o, ga, gb), name="outproj_bwd")
    return (*outs, got)


def _attn_bwd(qkv, do, tot, kmin, seq):
    t = qkv.shape[0]
    ne, nq, nk = t // seq, seq // TQ, seq // TK

    def kern(q_ref, k_ref, v_ref, do_ref, tot_ref, kmin_ref, dq_ref, dk_ref, dv_ref,
             kbd_scr, vtbd_scr, kbd2_scr, dkt_scr, dvt_scr):
        causal = _att_consts()
        upto, before = _sum_matrix("upto"), _sum_matrix("before")

        def prep(j, _):
            k0 = pl.multiple_of(j * TK, TK)
            kb = k_ref[pl.ds(k0, TK), :]
            top, bot = _head_diag(kb.astype(F32).T, True)
            kbd_scr[j] = jnp.concatenate([top, bot], axis=1).astype(BF16)
            top, bot = _head_diag(v_ref[pl.ds(k0, TK), :].astype(F32).T, True)
            vtbd_scr[j] = jnp.concatenate([top, bot], axis=1).astype(BF16)
            left, right = _head_diag(kb, False)
            kbd2_scr[j] = jnp.concatenate([left, right], axis=0)
            dkt_scr[j] = jnp.zeros((LANES, 2 * TK), F32)
            dvt_scr[j] = jnp.zeros((LANES, 2 * TK), F32)
            return 0

        lax.fori_loop(0, nk, prep, 0)

        def block(j, qb, qt, dob, dot_, totb, st, mask):
            f0, f1, p0, p1, dqacc = st
            lb, l1 = _att_logits(qb, kbd_scr[j])
            if mask is not None:
                l1 = jnp.where(mask, l1, 0.0)
            (s0, s1), (r0, r1) = _pair_sums(l1, upto)
            att = jnp.exp(lb + (totb - jnp.concatenate([s0 + f0, s1 + f1], axis=1)))
            if mask is not None:
                att = jnp.where(mask, att, 0.0)
            pw = att * _dot(dob, vtbd_scr[j])
            (e0, e1), (t0, t1) = _pair_sums(pw, before)
            dz = pw - jnp.exp(lb) * (pw + jnp.concatenate([e0 + p0, e1 + p1], axis=1))
            if mask is not None:
                dz = jnp.where(mask, dz, 0.0)
            dzb = dz.astype(BF16)
            dkt_scr[j] += _dot(qt, dzb)
            dvt_scr[j] += _dot(dot_, att.astype(BF16))
            return f0 + r0, f1 + r1, p0 + t0, p1 + t1, dqacc + _dot(dzb, kbd2_scr[j])

        def qloop(qi, _):
            q0 = pl.multiple_of(qi * TQ, TQ)
            qb = _scaled_q(q_ref, q0)
            qt = qb.astype(F32).T.astype(BF16)
            do32 = do_ref[pl.ds(q0, TQ), :]
            dob, dot_ = do32.astype(BF16), do32.T.astype(BF16)
            totb = tot_ref[pl.ds(q0, TQ), :]
            zero = jnp.zeros((TQ, TK), F32)
            st = (zero, zero, zero, zero, jnp.zeros((TQ, LANES), F32))

            k0 = kmin_ref[pl.program_id(0), pl.program_id(1), qi]

            def kloop(it, st):
                j = k0 + 2 * it
                return block(j + 1, qb, qt, dob, dot_, totb, block(j, qb, qt, dob, dot_, totb, st, None), None)

            st = lax.fori_loop(0, ((TQ // TK) * qi - k0) // 2, kloop, st)
            for jj in range(TQ // TK):
                lo = TK * jj
                new = block((TQ // TK) * qi + jj, qb[lo:], qt[:, lo:], dob[lo:], dot_[:, lo:], totb[lo:],
                            tuple(x[lo:] for x in st), causal[:TQ - lo])
                st = tuple(jnp.concatenate([x[:lo], y], axis=0) if lo else y for x, y in zip(st, new))
            dq_ref[pl.ds(q0, TQ), :] = (st[4] * ATT_SCALE).astype(BF16)
            return 0

        lax.fori_loop(0, nq, qloop, 0)

        def finish(j, _):
            k0 = pl.multiple_of(j * TK, TK)
            head0 = lax.broadcasted_iota(jnp.int32, (LANES, TK), 0) < DH
            for src, dst in ((dkt_scr, dk_ref), (dvt_scr, dv_ref)):
                acc = src[j]
                dst[pl.ds(k0, TK), :] = jnp.where(head0, acc[:, :TK], acc[:, TK:]).T.astype(BF16)
            return 0

        lax.fori_loop(0, nk, finish, 0)

    blk = pl.BlockSpec((seq, LANES), lambda e, p: (e, p))
    grad = jax.ShapeDtypeStruct((t, SB_WIDTH), BF16)
    return pl.pallas_call(
        kern, grid=(ne, SB_WIDTH // LANES),
        in_specs=_qkv_specs(seq) + [blk, pl.BlockSpec((seq, 2 * TK), lambda e, p: (e, p)),
                                    pl.BlockSpec(memory_space=pltpu.SMEM)],
        out_specs=[blk, blk, blk], out_shape=[grad, grad, grad],
        scratch_shapes=[pltpu.VMEM((nk, LANES, 2 * TK), BF16), pltpu.VMEM((nk, LANES, 2 * TK), BF16),
                        pltpu.VMEM((nk, 2 * TK, LANES), BF16), pltpu.VMEM((nk, LANES, 2 * TK), F32),
                        pltpu.VMEM((nk, LANES, 2 * TK), F32)],
        compiler_params=_cp(("parallel", "parallel")), name="attn_bwd",
    )(qkv, qkv, qkv, do, tot, kmin)


def _lru_bwd(xl, h, dy, conv_w, conv_b, wbd, ba, bx, lam, seq, comm=None):
    t = xl.shape[0]
    tc = min(512, seq)
    nc = seq // tc
    nb = tc // SUBLANES


    def body(ins, outs, scr):
        u_ref, g_ref, up_ref, h_ref, hp_ref, dy_ref, cw_ref, cb_ref, wbd_ref, ba_ref, bx_ref, lam_ref = ins
        (dxl_ref, small_ref, dwbd_ref), (lnext_ref, anext_ref, dcnext_ref) = outs, scr
        e, ci = pl.program_id(0), pl.program_id(1)
        first = ci == nc - 1

        @pl.when((e == 0) & (ci == 0))
        def _():
            small_ref[...] = jnp.zeros_like(small_ref)
            dwbd_ref[...] = jnp.zeros_like(dwbd_ref)

        @pl.when(ci == 0)
        def _():
            lnext_ref[...] = jnp.zeros_like(lnext_ref)
            anext_ref[...] = jnp.zeros_like(anext_ref)
            dcnext_ref[...] = jnp.zeros_like(dcnext_ref)

        u, g = u_ref[...], g_ref[...]
        keep = jnp.where(first, 0.0, 1.0)
        taps = _conv_taps(keep * up_ref[...], u)
        c = cb_ref[...]
        for k in range(CONV_WIDTH):
            c = c + taps[k] * cw_ref[k:k + 1, :]
        lam = lam_ref[...]
        sp = _softplus(-lam)
        r, i, a, mult, inv_mult = _lru_gates(c, wbd_ref, ba_ref[...], bx_ref[...], sp)
        gel, th = _gelu(g)
        dyv, hv = dy_ref[...], h_ref[...]
        dg = dyv * hv * _gelu_grad(g, th)

        aa, bb = _scan_rev(_shift_up(a, 1, anext_ref[0:1, :]), dyv * gel)
        lt = bb + aa * lnext_ref[0:1, :]
        lnext_ref[0:1, :] = _row_of(lt, 0)
        anext_ref[0:1, :] = _row_of(a, 0)

        hprev = _shift_down(hv, 1, keep * hp_ref[SUBLANES - 1:SUBLANES, :])
        da = lt * hprev
        dmult = lt * i * c
        di = lt * mult * c
        dc = lt * mult * i
        dla = da * a - dmult * (a * a) * inv_mult
        dga = dla * ((-LRU_C) * sp) * r * (1.0 - r)
        dgx = di * i * (1.0 - i)
        small_ref[7:8, :] += jnp.sum(dla * r, axis=0, keepdims=True) * (LRU_C * _sigmoid(-lam))
        small_ref[5:6, :] += jnp.sum(dga, axis=0, keepdims=True)
        small_ref[6:7, :] += jnp.sum(dgx, axis=0, keepdims=True)

        dcs = []
        for p in range(LRU_WIDTH // LANES):
            cols = slice(LANES * p, LANES * (p + 1))
            dgax = jnp.concatenate([dga[:, cols], dgx[:, cols]], axis=1).astype(BF16)
            dcs.append(_dot_nt(dgax, wbd_ref[p]))
            dwbd_ref[p] += _dot_tn(c[:, cols].astype(BF16), dgax)
        dc = dc + jnp.concatenate(dcs, axis=1)
        small_ref[4:5, :] += jnp.sum(dc, axis=0, keepdims=True)

        catd = jnp.concatenate([dc, dcnext_ref[...]], axis=0)
        du = dc * cw_ref[CONV_WIDTH - 1:CONV_WIDTH, :]
        for j in range(1, CONV_WIDTH):
            du = du + pltpu.roll(catd, tc + SUBLANES - j, 0)[:tc] * cw_ref[CONV_WIDTH - 1 - j:CONV_WIDTH - j, :]
        dcnext_ref[...] = dc[:SUBLANES]
        for k in range(CONV_WIDTH):
            small_ref[k:k + 1, :] += jnp.sum(dc * taps[k], axis=0, keepdims=True)
        dxl_ref[:, :LRU_WIDTH] = du.astype(BF16)
        dxl_ref[:, LRU_WIDTH:] = dg.astype(BF16)

    rev = lambda e, c: e * nc + (nc - 1 - c)
    chunk = lambda col: pl.BlockSpec((tc, LRU_WIDTH), lambda e, c: (rev(e, c), col))
    prev8 = pl.BlockSpec((SUBLANES, LRU_WIDTH), lambda e, c: (jnp.maximum(rev(e, c) * nb - 1, 0), 0))
    outs, got = _call(
        body, comm, grid=(t // seq, nc),
        in_specs=[chunk(0), chunk(1), prev8, chunk(0), prev8, chunk(0)] + _lru_param_specs(2),
        out_specs=[pl.BlockSpec((tc, 2 * LRU_WIDTH), lambda e, c: (rev(e, c), 0)),
                   pl.BlockSpec((SUBLANES, LRU_WIDTH), lambda e, c: (0, 0)),
                   pl.BlockSpec((LRU_WIDTH // LANES, LANES, 2 * LANES), lambda e, c: (0, 0, 0))],
        out_shape=[jax.ShapeDtypeStruct((t, 2 * LRU_WIDTH), BF16), jax.ShapeDtypeStruct((SUBLANES, LRU_WIDTH), F32),
                   jax.ShapeDtypeStruct((LRU_WIDTH // LANES, LANES, 2 * LANES), F32)],
        scratch_shapes=[pltpu.VMEM((SUBLANES, LRU_WIDTH), F32)] * 3,
        args=(xl, xl, xl, h, h, dy, conv_w, conv_b, wbd, ba, bx, lam), name="lru_bwd")
    return (*outs, got)


def _adam_math(w, g, m, v):
    m2 = ADAM_B1 * m + (1.0 - ADAM_B1) * g
    v2 = ADAM_B2 * v + (1.0 - ADAM_B2) * (g * g)
    m_hat = m2 / (1.0 - ADAM_B1 ** ADAM_STEP)
    v_hat = v2 / (1.0 - ADAM_B2 ** ADAM_STEP)
    return -ADAM_LR * (m_hat / (jnp.sqrt(v_hat) + ADAM_EPS) + ADAM_WD * w), m2, v2


def _adamw(w, g, m, v, name):
    rows, cols = w.shape
    tr = 256 if rows % 256 == 0 else rows

    def kern(w_ref, g_ref, m_ref, v_ref, d_ref, m2_ref, v2_ref):
        d_ref[...], m2_ref[...], v2_ref[...] = _adam_math(w_ref[...], g_ref[...], m_ref[...], v_ref[...])

    blk = pl.BlockSpec((tr, cols), lambda i: (i, 0))
    out = jax.ShapeDtypeStruct((rows, cols), F32)
    return pl.pallas_call(kern, grid=(rows // tr,), in_specs=[blk] * 4, out_specs=[blk] * 3, out_shape=[out] * 3,
                          compiler_params=_cp(("parallel",)), name=name)(w, g, m, v)


def _adamw_small(ws, gs, ms, vs):
    n = len(ws)

    def kern(*refs):
        for k in range(n):
            outs = _adam_math(refs[k][...], refs[n + k][...], refs[2 * n + k][...], refs[3 * n + k][...])
            for j in range(3):
                refs[(4 + j) * n + k][...] = outs[j]

    vm = pl.BlockSpec(memory_space=pltpu.VMEM)
    out = pl.pallas_call(kern, in_specs=[vm] * (4 * n), out_specs=[vm] * (3 * n),
                         out_shape=[jax.ShapeDtypeStruct(w.shape, F32) for w in ws] * 3, name="adamw_small")(*ws, *gs, *ms, *vs)
    return out[:n], out[n:2 * n], out[2 * n:]


def _comm_only(kind, arrays, name):
    return _call(lambda ins, outs, scr: None, (kind, arrays), grid=(1,), in_specs=[], out_specs=[], out_shape=[],
                 scratch_shapes=[], args=(), name=name)[1]


def _pair_add(g, got, core):
    _, rows, cols = g.shape
    half = rows // 2
    tr = min(256, half)
    nt = half // tr

    def kern(c_ref, g_ref, o_ref, out_ref):
        out_ref[...] = (g_ref[...] + o_ref[...]).astype(BF16)

    return pl.pallas_call(
        kern, grid_spec=pltpu.PrefetchScalarGridSpec(
            num_scalar_prefetch=1, grid=(N_CHIPS, nt),
            in_specs=[pl.BlockSpec((None, tr, cols), lambda j, i, c_ref: (j, c_ref[0] * nt + i, 0)),
                      pl.BlockSpec((None, tr, cols), lambda j, i, c_ref: (j, i, 0))],
            out_specs=pl.BlockSpec((None, tr, cols), lambda j, i, c_ref: (j, i, 0))),
        out_shape=jax.ShapeDtypeStruct((N_CHIPS, half, cols), BF16),
        compiler_params=_cp(("parallel", "parallel")), name="pair_add",
    )(core, g, got)


def _chip_add(part, got, place):
    _, half, cols = part.shape
    tr = min(256, half)
    nt = half // tr

    def kern(p_ref, part_ref, got_ref, out_ref):
        out_ref[...] = (part_ref[...].astype(F32) + got_ref[0].astype(F32) + got_ref[1].astype(F32)
                        + got_ref[2].astype(F32))

    return pl.pallas_call(
        kern, grid_spec=pltpu.PrefetchScalarGridSpec(
            num_scalar_prefetch=1, grid=(nt,),
            in_specs=[pl.BlockSpec((None, tr, cols), lambda i, p_ref: (p_ref[0], i, 0)),
                      pl.BlockSpec((3, tr, cols), lambda i, p_ref: (0, i, 0))],
            out_specs=pl.BlockSpec((tr, cols), lambda i, p_ref: (p_ref[1] * nt + i, 0))),
        out_shape=jax.ShapeDtypeStruct((2 * half, cols), F32),
        compiler_params=_cp(("parallel",)), name="chip_add",
    )(place, part, got)


def _finale(packed, fulls):
    rows, n = packed.shape[0], len(fulls)

    def kern(in_ref, *refs):
        ins, out_ref, outs = refs[:n], refs[n], refs[n + 1:2 * n + 1]
        slots, send_sems, recv_sems, join_send, join_recv = refs[2 * n + 1:]
        x, y, c = _place()
        mine = 4 * x + 2 * y + c
        copies = []
        for w in range(n):
            half = ins[w].shape[0] // 2
            rws = pl.ds(c * half, half)
            copies.append(pltpu.make_async_remote_copy(
                src_ref=ins[w].at[rws, :], dst_ref=outs[w].at[rws, :], send_sem=join_send.at[w],
                recv_sem=join_recv.at[w], device_id=(x, y, 1 - c), device_id_type=MESH))
        for k in range(1, N_DEV):
            peer = (x ^ (k >> 2), y ^ ((k >> 1) & 1), c ^ (k & 1))
            copies.append(pltpu.make_async_remote_copy(
                src_ref=in_ref, dst_ref=slots.at[mine], send_sem=send_sems.at[k - 1], recv_sem=recv_sems.at[k - 1],
                device_id=peer, device_id_type=MESH))
        for cp in copies:
            cp.start()
        slots[mine] = in_ref[...]
        for cp in copies:
            cp.wait()
        acc = slots[0]
        for sl in range(1, N_DEV):
            acc = acc + slots[sl]
        out_ref[...] = acc

    vm = pl.BlockSpec(memory_space=pltpu.VMEM)
    out = pl.pallas_call(
        kern, in_specs=[vm] + [ANY] * n, out_specs=[vm] + [ANY] * n,
        out_shape=[jax.ShapeDtypeStruct((rows, LANES), F32)] + [jax.ShapeDtypeStruct(f.shape, f.dtype) for f in fulls],
        input_output_aliases={w + 1: w + 1 for w in range(n)},
        scratch_shapes=[pltpu.VMEM((N_DEV, rows, LANES), F32), pltpu.SemaphoreType.DMA((N_DEV - 1,)),
                        pltpu.SemaphoreType.DMA((N_DEV - 1,)), pltpu.SemaphoreType.DMA((n,)), pltpu.SemaphoreType.DMA((n,))],
        name="finale",
    )(packed, *fulls)
    return out[0], list(out[1:])


SMALL = ["norm1_g", "conv_w", "conv_b", "lru_w_a", "lru_b_a", "lru_w_x", "lru_b_x", "lru_lambda", "lru_out_g", "sb_out_g",
         "norm2_g", "final_g"]
BIG = ["w_in", "w_out", "w_up", "w_down"]
WEIGHTS = ["norm1_g", "w_in", "conv_w", "conv_b", "lru_w_a", "lru_b_a", "lru_w_x", "lru_b_x", "lru_lambda", "lru_out_g",
           "sb_out_g", "w_out", "norm2_g", "w_up", "w_down", "final_g"]


def _pack(arrays):
    flat = []
    for a in arrays:
        a = a.reshape(-1).astype(F32)
        flat.append(jnp.pad(a, (0, (-a.shape[0]) % LANES)))
    v = jnp.concatenate(flat)
    v = jnp.pad(v, (0, (-v.shape[0]) % (LANES * SUBLANES)))
    return v.reshape(-1, LANES)


def _unpack(packed, shapes):
    v, out, off = packed.reshape(-1), [], 0
    for shp in shapes:
        size = math.prod(shp)
        out.append(v[off:off + size].reshape(shp))
        off += size + (-size) % LANES
    return out


def _blockdiag_pairs(w):
    w = w.reshape(4, 2, DH, DH)
    z = jnp.zeros((4, DH, DH), w.dtype)
    return jnp.concatenate([jnp.concatenate([w[:, 0], z], axis=2), jnp.concatenate([z, w[:, 1]], axis=2)], axis=1)


def _blockdiag_unpairs(wbd):
    return jnp.stack([wbd[:, :DH, :DH], wbd[:, DH:, DH:]], axis=1).reshape(8, DH, DH)


def _full_cols(g):
    return jnp.transpose(g, (1, 0, 2)).reshape(g.shape[1], N_CHIPS * g.shape[2])


def _local_step(x2, tgt, seq, norm1_g, w_in, conv_w, conv_b, w_a, b_a, w_x, b_x, lru_lambda, lru_out_g, sb_out_g, rest,
                norm2_g, final_g, place=None):
    alone = place is None
    wbd = jnp.concatenate([_blockdiag_pairs(w_a), _blockdiag_pairs(w_x)], axis=2).astype(BF16)
    ba, bx = b_a.reshape(1, LRU_WIDTH), b_x.reshape(1, LRU_WIDTH)
    gf = final_g.reshape(1, D_MODEL)

    xn, got = _norm1(x2, norm1_g, None if alone else ("gather", [w_in, conv_w]))
    w_in_f, conv_w_f = (w_in, conv_w) if alone else (_full_cols(got[0]), _full_cols(got[1]))
    xl, qkv = _inproj(xn, w_in_f)
    h, y_lru = _lru_fwd(xl, conv_w_f, conv_b, wbd, ba, bx, lru_lambda, seq)
    o, tot, kmin, got = _attn_fwd(qkv, seq, None if alone else ("gather", rest))
    w_out_f, w_up_f, w_down_f = rest if alone else (
        got[0].reshape(D_MODEL, D_MODEL), _full_cols(got[1]), got[2].reshape(D_FF, D_MODEL))
    h1, mix = _outproj(y_lru, o, x2, lru_out_g, sb_out_g, w_out_f)
    hn, up, u2, dh2, dh2b, loss_part, d_final = _mlp_loss(h1, norm2_g, w_up_f, w_down_f, tgt, gf)

    dpre = _mlp_bwd_pre(dh2b, w_down_f, up)
    g_w_down = _matmul(u2, dh2b, "tn", F32, "dw_down", 1024, 1024, 1024).reshape(N_CHIPS, D_FF // N_CHIPS, D_MODEL)
    g_w_up = _matmul(hn, dpre, "tn", F32, "dw_up", 1024, D_FF // N_CHIPS, 1024, split_cols=True)
    dh1, d_norm2, _ = _proj_bwd_norm([dpre], w_up_f, h1, norm2_g, dh2, "mlp_bwd_in")
    g_w_out = _matmul(mix, dh1, "tn", F32, "dw_out", 1024, 1024, 2048).reshape(N_CHIPS, D_MODEL // N_CHIPS, D_MODEL)
    late = [g_w_out, g_w_up, g_w_down]
    dy_lru, do, d_ga, d_gb, swapped = _outproj_bwd(dh1, w_out_f, y_lru, o, lru_out_g, sb_out_g,
                                                   None if alone else ("swap", late))
    parts = None if alone else [_pair_add(g, r, place[1:]) for g, r in zip(late, swapped)]
    dq, dk, dv = _attn_bwd(qkv, do, tot, kmin, seq)
    dxl, lru_small, d_wbd, got = _lru_bwd(xl, h, dy_lru, conv_w_f, conv_b, wbd, ba, bx, lru_lambda, seq,
                                          None if alone else ("exchange", parts))
    if not alone:
        late = [_chip_add(p, r, place) for p, r in zip(parts, got)]
    dproj = jnp.concatenate([dxl, dq, dk, dv], axis=1)
    g_w_in = _matmul(xn, dproj, "tn", F32, "dw_in", 1024, IN_COLS // N_CHIPS, 2048, split_cols=True)
    part = None if alone else _pair_add(g_w_in, _comm_only("swap", [g_w_in], "pair_swap")[0], place[1:])
    dx, d_norm1, got = _proj_bwd_norm([dxl, dq, dk, dv], w_in_f, x2, norm1_g, dh1, "inproj_bwd",
                                      None if alone else ("exchange", [part]))
    if not alone:
        g_w_in = _chip_add(part, got[0], place)
    small_parts = {
        "norm1_g": d_norm1, "conv_w": lru_small[:CONV_WIDTH], "conv_b": lru_small[4:5],
        "lru_w_a": _blockdiag_unpairs(d_wbd[:, :, :LANES]), "lru_b_a": lru_small[5:6],
        "lru_w_x": _blockdiag_unpairs(d_wbd[:, :, LANES:]), "lru_b_x": lru_small[6:7], "lru_lambda": lru_small[7:8],
        "lru_out_g": d_ga, "sb_out_g": d_gb, "norm2_g": d_norm2, "final_g": d_final,
    }
    return loss_part, dx, [g_w_in] + late, small_parts


def kernel(x, norm1_g, w_in, conv_w, conv_b, lru_w_a, lru_b_a, lru_w_x, lru_b_x, lru_lambda, lru_out_g, sb_out_g, w_out, norm2_g, w_up, w_down, final_g, loss_target, m_norm1_g, m_w_in, m_conv_w, m_conv_b, m_lru_w_a, m_lru_b_a, m_lru_w_x, m_lru_b_x, m_lru_lambda, m_lru_out_g, m_sb_out_g, m_w_out, m_norm2_g, m_w_up, m_w_down, m_final_g, v_norm1_g, v_w_in, v_conv_w, v_conv_b, v_lru_w_a, v_lru_b_a, v_lru_w_x, v_lru_b_x, v_lru_lambda, v_lru_out_g, v_sb_out_g, v_w_out, v_norm2_g, v_w_up, v_w_down, v_final_g):
    given = dict(locals())
    ne, seq, _ = x.shape
    t = ne * seq
    xi, yi, ci = _place()
    place = jnp.stack([2 * xi + yi, ci]).astype(jnp.int32)

    loss_part, dx, halves, small_parts = _local_step(
        x.reshape(t, D_MODEL), loss_target.reshape(t, D_MODEL), seq, norm1_g, w_in[0].astype(BF16), conv_w[0], conv_b,
        lru_w_a[0], lru_b_a, lru_w_x[0], lru_b_x, lru_lambda, lru_out_g, sb_out_g,
        [w_out[0].astype(BF16), w_up[0].astype(BF16), w_down[0].astype(BF16)], norm2_g, final_g, place)

    full_shapes = {n: ((CONV_WIDTH, LRU_WIDTH) if n == "conv_w" else given[n].shape) for n in SMALL}
    red, fulls = _finale(_pack([small_parts[n] for n in SMALL] + [loss_part]), halves)
    red_list = _unpack(red, [full_shapes[n] for n in SMALL] + [(1, LANES)])
    grads = dict(zip(SMALL, red_list[:-1]))
    loss = red_list[-1][0, 0]
    grads["conv_w"] = lax.dynamic_slice_in_dim(grads["conv_w"], place[0] * (LRU_WIDTH // N_CHIPS), LRU_WIDTH // N_CHIPS,
                                               axis=1).reshape(conv_w.shape)
    for n, full in zip(BIG, fulls):
        grads[n] = full.reshape(given[n].shape)

    delta, new_m, new_v = {}, {}, {}
    for n in BIG:
        shp = given[n].shape
        d, m2, v2 = _adamw(given[n][0], grads[n][0], given["m_" + n][0], given["v_" + n][0], "adamw_" + n)
        delta[n], new_m[n], new_v[n] = d.reshape(shp), m2.reshape(shp), v2.reshape(shp)
    as2d = lambda a: a.reshape(-1, a.shape[-1])
    ds, m2s, v2s = _adamw_small([as2d(given[n]) for n in SMALL], [as2d(grads[n]) for n in SMALL],
                                [as2d(given["m_" + n]) for n in SMALL], [as2d(given["v_" + n]) for n in SMALL])
    for n, dd, mm, vv in zip(SMALL, ds, m2s, v2s):
        shp = given[n].shape
        delta[n], new_m[n], new_v[n] = dd.reshape(shp), mm.reshape(shp), vv.reshape(shp)

    return (loss, dx.reshape(x.shape), *[grads[n] for n in WEIGHTS], *[delta[n] for n in WEIGHTS],
            *[new_m[n] for n in WEIGHTS], *[new_v[n] for n in WEIGHTS])
```

```python
import functools
import math

import jax
import jax.numpy as jnp
from jax import lax
from jax.experimental import pallas as pl
from jax.experimental.pallas import tpu as pltpu

F32, BF16 = jnp.float32, jnp.bfloat16
MESH = pl.DeviceIdType.MESH

D_MODEL = 1024
LRU_WIDTH = 512
SB_WIDTH = 512
DH = 64
IN_COLS = 2 * LRU_WIDTH + 3 * SB_WIDTH
D_FF = 4 * D_MODEL
CONV_WIDTH = 4
LRU_C = 8.0
EPS = 1e-6
N_CHIPS = 4
N_DEV = 8
LANES = 128
SUBLANES = 8
TQ = 512
TK = 128
ATT_SCALE = 1.0 / math.sqrt(DH)
SKIP_LOG = -105.0
BAND = 2
VMEM_LIMIT = 52 * 1024 * 1024
VMEM_LIMIT_BIG = 62 * 1024 * 1024

ADAM_LR, ADAM_B1, ADAM_B2, ADAM_EPS, ADAM_WD, ADAM_STEP = 0.001, 0.9, 0.999, 1e-08, 0.01, 10

_GELU_K = math.sqrt(2.0 / math.pi)
_GELU_C = 0.044715


def _cp(sem, vmem=VMEM_LIMIT):
    return pltpu.CompilerParams(dimension_semantics=sem, vmem_limit_bytes=vmem)


def _dot(a, b):
    return jnp.dot(a, b, preferred_element_type=F32)


def _dot_nt(a, b):
    return lax.dot_general(a, b, (((1,), (1,)), ((), ())), preferred_element_type=F32)


def _dot_tn(a, b):
    return lax.dot_general(a, b, (((0,), (0,)), ((), ())), preferred_element_type=F32)


def _rstd(x):
    return lax.rsqrt(jnp.mean(x * x, axis=-1, keepdims=True) + EPS)


def _rms_bwd(x, g, dy):
    r = _rstd(x)
    gd = g * dy
    dx = r * gd - x * (r * r * r) * jnp.mean(x * gd, axis=-1, keepdims=True)
    return dx, jnp.sum(dy * x * r, axis=0, keepdims=True)


def _sigmoid(x):
    return 0.5 * jnp.tanh(0.5 * x) + 0.5


def _softplus(x):
    return jnp.maximum(x, 0.0) + jnp.log(1.0 + jnp.exp(-jnp.abs(x)))


def _neg_expm1(x, ex):
    series = -x * (1.0 + x * (0.5 + x * (1.0 / 6.0)))
    return jnp.where(x > -2.0 ** -7, series, 1.0 - ex)


def _gelu(g):
    t = jnp.tanh(_GELU_K * (g + _GELU_C * g * g * g))
    return 0.5 * g * (1.0 + t), t


def _gelu_grad(g, t):
    return 0.5 * (1.0 + t) + 0.5 * g * (1.0 - t * t) * _GELU_K * (1.0 + 3.0 * _GELU_C * g * g)


def _rows(shape):
    return lax.broadcasted_iota(jnp.int32, shape, 0)


def _shift_down(x, s, fill):
    n, c = x.shape
    if s % SUBLANES == 0:
        return jnp.concatenate([jnp.broadcast_to(jnp.asarray(fill, x.dtype), (s, c)), x[:n - s]], axis=0)
    return jnp.where(_rows(x.shape) >= s, pltpu.roll(x, s, 0), fill)


def _shift_up(x, s, fill):
    n, c = x.shape
    if s % SUBLANES == 0:
        return jnp.concatenate([x[s:], jnp.broadcast_to(jnp.asarray(fill, x.dtype), (s, c))], axis=0)
    return jnp.where(_rows(x.shape) < n - s, pltpu.roll(x, n - s, 0), fill)


def _row_of(x, idx):
    return jnp.sum(jnp.where(_rows(x.shape) == idx, x, 0.0), axis=0, keepdims=True)


def _matmul(a, b, dims, out_dtype, name, tm, tn, tk, split_cols=False):
    if dims == "nn":
        (m, kk), n, dot = a.shape, b.shape[1], _dot
    elif dims == "nt":
        (m, kk), n, dot = a.shape, b.shape[0], _dot_nt
    else:
        (kk, m), n, dot = a.shape, b.shape[1], _dot_tn
    tm, tn, tk = min(tm, m), min(tn, n), min(tk, kk)
    assert m % tm == 0 and n % tn == 0 and kk % tk == 0, (name, m, n, kk)
    nk = kk // tk

    def kern(a_ref, b_ref, o_ref, acc_ref):
        k = pl.program_id(2)

        @pl.when(k == 0)
        def _():
            acc_ref[...] = jnp.zeros_like(acc_ref)

        acc_ref[...] += dot(a_ref[...].astype(BF16), b_ref[...].astype(BF16))

        @pl.when(k == nk - 1)
        def _():
            o_ref[...] = acc_ref[...].astype(o_ref.dtype)

    if split_cols:
        out_shape = jax.ShapeDtypeStruct((n // tn, m, tn), out_dtype)
        o_spec = pl.BlockSpec((None, tm, tn), lambda i, j, k: (j, i, 0))
    else:
        out_shape = jax.ShapeDtypeStruct((m, n), out_dtype)
        o_spec = pl.BlockSpec((tm, tn), lambda i, j, k: (i, j))
    if dims == "nn":
        a_spec = pl.BlockSpec((tm, tk), lambda i, j, k: (i, k))
        b_spec = pl.BlockSpec((tk, tn), lambda i, j, k: (k, j))
    elif dims == "nt":
        a_spec = pl.BlockSpec((tm, tk), lambda i, j, k: (i, k))
        b_spec = pl.BlockSpec((tn, tk), lambda i, j, k: (j, k))
    else:
        a_spec = pl.BlockSpec((tk, tm), lambda i, j, k: (k, i))
        b_spec = pl.BlockSpec((tk, tn), lambda i, j, k: (k, j))
    return pl.pallas_call(
        kern, grid=(m // tm, n // tn, nk), in_specs=[a_spec, b_spec], out_specs=o_spec, out_shape=out_shape,
        scratch_shapes=[pltpu.VMEM((tm, tn), F32)], compiler_params=_cp(("parallel", "parallel", "arbitrary")), name=name,
    )(a, b)


ANY = pl.BlockSpec(memory_space=pl.ANY)


def _place():
    return lax.axis_index("x"), lax.axis_index("y"), lax.axis_index("c")


def _other_chips(x, y):
    return [(1 - x, y), (x, 1 - y), (1 - x, 1 - y)]


def _gather_copies(ins, outs, send_sems, recv_sems, loc_sems):
    x, y, c = _place()
    mine = 2 * x + y
    copies = []
    for w in range(len(ins)):
        copies.append(pltpu.make_async_copy(ins[w], outs[w].at[mine], loc_sems.at[w]))
        for k, chip in enumerate(_other_chips(x, y)):
            copies.append(pltpu.make_async_remote_copy(
                src_ref=ins[w], dst_ref=outs[w].at[mine], send_sem=send_sems.at[3 * w + k],
                recv_sem=recv_sems.at[3 * w + k], device_id=(*chip, c), device_id_type=MESH))
    return copies


def _gather_shapes(shards):
    return ([jax.ShapeDtypeStruct((N_CHIPS,) + s.shape, s.dtype) for s in shards],
            [pltpu.SemaphoreType.DMA((3 * len(shards),)), pltpu.SemaphoreType.DMA((3 * len(shards),)),
             pltpu.SemaphoreType.DMA((len(shards),))])


def _exchange_copies(ins, outs, send_sems, recv_sems):
    x, y, c = _place()
    copies = []
    for w in range(len(ins)):
        for k, chip in enumerate(_other_chips(x, y)):
            copies.append(pltpu.make_async_remote_copy(
                src_ref=ins[w].at[2 * chip[0] + chip[1]], dst_ref=outs[w].at[k], send_sem=send_sems.at[3 * w + k],
                recv_sem=recv_sems.at[3 * w + k], device_id=(*chip, c), device_id_type=MESH))
    return copies


def _exchange_shapes(parts):
    return ([jax.ShapeDtypeStruct((3,) + p.shape[1:], p.dtype) for p in parts],
            [pltpu.SemaphoreType.DMA((3 * len(parts),)), pltpu.SemaphoreType.DMA((3 * len(parts),))])


def _swap_copies(ins, outs, send_sems, recv_sems):
    x, y, c = _place()
    copies = []
    for w in range(len(ins)):
        half = ins[w].shape[1] // 2
        copies.append(pltpu.make_async_remote_copy(
            src_ref=ins[w].at[:, pl.ds((1 - c) * half, half), :], dst_ref=outs[w], send_sem=send_sems.at[w],
            recv_sem=recv_sems.at[w], device_id=(x, y, 1 - c), device_id_type=MESH))
    return copies


def _swap_shapes(grads):
    return ([jax.ShapeDtypeStruct((g.shape[0], g.shape[1] // 2, g.shape[2]), g.dtype) for g in grads],
            [pltpu.SemaphoreType.DMA((len(grads),)), pltpu.SemaphoreType.DMA((len(grads),))])


COMM = {"gather": (_gather_copies, _gather_shapes), "exchange": (_exchange_copies, _exchange_shapes),
        "swap": (_swap_copies, _swap_shapes)}


def _call(body, comm, *, grid, in_specs, out_specs, out_shape, scratch_shapes, args, name):
    ni, no, ns = len(in_specs), len(out_specs), len(scratch_shapes)
    arrays = list(comm[1]) if comm else []
    nc = len(arrays)
    c_shapes, c_sems = COMM[comm[0]][1](arrays) if comm else ([], [])

    def kern(*refs):
        ins, cin, outs = refs[:ni], refs[ni:ni + nc], refs[ni + nc:ni + nc + no]
        cout, scr, sems = refs[ni + nc + no:ni + 2 * nc + no], refs[ni + 2 * nc + no:ni + 2 * nc + no + ns], refs[ni + 2 * nc + no + ns:]
        ids = [pl.program_id(d) for d in range(len(grid))]
        if nc:
            @pl.when(functools.reduce(lambda a, b: a & b, [i == 0 for i in ids]))
            def _():
                for cp in COMM[comm[0]][0](cin, cout, *sems):
                    cp.start()

        body(ins, outs, scr)
        if nc:
            @pl.when(functools.reduce(lambda a, b: a & b, [i == g - 1 for i, g in zip(ids, grid)]))
            def _():
                for cp in COMM[comm[0]][0](cin, cout, *sems):
                    cp.wait()

    out = pl.pallas_call(
        kern, grid=grid, in_specs=list(in_specs) + [ANY] * nc, out_specs=list(out_specs) + [ANY] * nc,
        out_shape=list(out_shape) + c_shapes, scratch_shapes=list(scratch_shapes) + c_sems,
        compiler_params=_cp(("arbitrary",) * len(grid)), name=name,
    )(*args, *arrays)
    return list(out[:no]), list(out[no:])


def _resident(shape):
    return pl.BlockSpec(shape, lambda *_: (0,) * len(shape), pipeline_mode=pl.Buffered(1))


def _norm1(x, g1, comm):
    t = x.shape[0]
    tm = min(1024, t)

    def body(ins, outs, _):
        xv = ins[0][...]
        outs[0][...] = (xv * _rstd(xv) * ins[1][...]).astype(BF16)

    row = pl.BlockSpec((tm, D_MODEL), lambda i: (i, 0))
    (xn,), got = _call(body, comm, grid=(t // tm,), in_specs=[row, pl.BlockSpec((1, D_MODEL), lambda i: (0, 0))],
                       out_specs=[row], out_shape=[jax.ShapeDtypeStruct((t, D_MODEL), BF16)], scratch_shapes=[],
                       args=(x, g1), name="norm1")
    return xn, got


def _inproj(xn, w_in):
    t = xn.shape[0]
    tm = min(512, t)

    def kern(x_ref, w_ref, xl_ref, qkv_ref):
        xn_v = x_ref[...]
        xl_ref[...] = _dot(xn_v, w_ref[:, : 2 * LRU_WIDTH])
        qkv_ref[...] = _dot(xn_v, w_ref[:, 2 * LRU_WIDTH:]).astype(BF16)

    row = lambda c: pl.BlockSpec((tm, c), lambda i: (i, 0))
    return pl.pallas_call(
        kern, grid=(t // tm,), in_specs=[row(D_MODEL), _resident((D_MODEL, IN_COLS))],
        out_specs=[row(2 * LRU_WIDTH), row(3 * SB_WIDTH)],
        out_shape=[jax.ShapeDtypeStruct((t, 2 * LRU_WIDTH), F32), jax.ShapeDtypeStruct((t, 3 * SB_WIDTH), BF16)],
        compiler_params=_cp(("parallel",)), name="inproj",
    )(xn, w_in)


def _conv_taps(hist, u):
    cat = jnp.concatenate([hist, u], axis=0)
    return [pltpu.roll(cat, CONV_WIDTH - 1 - k, 0)[SUBLANES:] for k in range(CONV_WIDTH - 1)] + [u]


def _lru_gates(c, wbd_ref, ba, bx, sp):
    gas, gxs = [], []
    for p in range(LRU_WIDTH // LANES):
        gax = _dot(c[:, LANES * p: LANES * (p + 1)].astype(BF16), wbd_ref[p])
        gas.append(gax[:, :LANES])
        gxs.append(gax[:, LANES:])
    r = _sigmoid(jnp.concatenate(gas, axis=1) + ba)
    i = _sigmoid(jnp.concatenate(gxs, axis=1) + bx)
    la = (-LRU_C) * r * sp
    a = jnp.exp(la)
    e2 = _neg_expm1(2.0 * la, a * a)
    inv_mult = lax.rsqrt(jnp.maximum(e2, 1e-30))
    return r, i, a, e2 * inv_mult, inv_mult


def _scan_fwd(a, b):
    s = 1
    while s < a.shape[0]:
        b = b + a * _shift_down(b, s, 0.0)
        a = a * _shift_down(a, s, 1.0)
        s *= 2
    return a, b


def _scan_rev(a, b):
    s = 1
    while s < a.shape[0]:
        b = b + a * _shift_up(b, s, 0.0)
        a = a * _shift_up(a, s, 1.0)
        s *= 2
    return a, b


def _lru_param_specs(grid_rank):
    z2 = (lambda e, c: (0, 0)) if grid_rank == 2 else None
    return [
        pl.BlockSpec((CONV_WIDTH, LRU_WIDTH), z2), pl.BlockSpec((1, LRU_WIDTH), z2),
        pl.BlockSpec((LRU_WIDTH // LANES, LANES, 2 * LANES), lambda e, c: (0, 0, 0)),
        pl.BlockSpec((1, LRU_WIDTH), z2), pl.BlockSpec((1, LRU_WIDTH), z2), pl.BlockSpec((1, LRU_WIDTH), z2),
    ]


def _lru_fwd(xl, conv_w, conv_b, wbd, ba, bx, lam, seq):
    t = xl.shape[0]
    tc = min(512, seq)
    nc = seq // tc

    def kern(u_ref, g_ref, cw_ref, cb_ref, wbd_ref, ba_ref, bx_ref, lam_ref, h_ref, y_ref, hist_ref, hcar_ref):
        @pl.when(pl.program_id(1) == 0)
        def _():
            hist_ref[...] = jnp.zeros_like(hist_ref)
            hcar_ref[...] = jnp.zeros_like(hcar_ref)

        u = u_ref[...]
        taps = _conv_taps(hist_ref[...], u)
        hist_ref[...] = u_ref[tc - SUBLANES:, :]
        c = cb_ref[...]
        for k in range(CONV_WIDTH):
            c = c + taps[k] * cw_ref[k:k + 1, :]
        sp = _softplus(-lam_ref[...])
        _, i, a, mult, _ = _lru_gates(c, wbd_ref, ba_ref[...], bx_ref[...], sp)
        aa, bb = _scan_fwd(a, mult * i * c)
        h = bb + aa * hcar_ref[0:1, :]
        h_ref[...] = h
        hcar_ref[0:1, :] = h_ref[tc - 1:tc, :]
        y_ref[...] = h * _gelu(g_ref[...])[0]

    chunk = lambda col: pl.BlockSpec((tc, LRU_WIDTH), lambda e, c: (e * nc + c, col))
    out = jax.ShapeDtypeStruct((t, LRU_WIDTH), F32)
    return pl.pallas_call(
        kern, grid=(t // seq, nc), in_specs=[chunk(0), chunk(1)] + _lru_param_specs(2),
        out_specs=[chunk(0), chunk(0)], out_shape=[out, out],
        scratch_shapes=[pltpu.VMEM((SUBLANES, LRU_WIDTH), F32), pltpu.VMEM((SUBLANES, LRU_WIDTH), F32)],
        compiler_params=_cp(("arbitrary", "arbitrary")), name="lru_fwd",
    )(xl, xl, conv_w, conv_b, wbd, ba, bx, lam)


def _att_consts():
    row = lax.broadcasted_iota(jnp.int32, (TQ, 2 * TK), 0)
    key = lax.broadcasted_iota(jnp.int32, (TQ, 2 * TK), 1) & (TK - 1)
    return key < row


def _sum_matrix(kind):
    j = lax.broadcasted_iota(jnp.int32, (2 * TK, 2 * TK), 0) & (TK - 1)
    s = lax.broadcasted_iota(jnp.int32, (2 * TK, 2 * TK), 1)
    pick = {"after": j > s, "upto": j <= s, "before": j < s}[kind]
    return jnp.where((s >= TK) | pick, 1.0, 0.0).astype(BF16)


def _hi_lo(x):
    hi = x.astype(BF16)
    return hi, (x - hi.astype(F32)).astype(BF16)


def _pair_sums(x, m):
    hi, lo = _hi_lo(x)
    out = []
    for hd in range(2):
        cols = slice(hd * TK, (hd + 1) * TK)
        out.append(_dot(jnp.concatenate([hi[:, cols], lo[:, cols]], axis=1), m))
    return [o[:, :TK] for o in out], [o[:, TK:] for o in out]


def _att_logits(qb, kbd):
    z = _dot(qb, kbd)
    lg = jnp.log(1.0 + jnp.exp(-jnp.abs(z)))
    lb = jnp.minimum(z, 0.0) - lg
    return lb, lb - z


def _head_diag(x, rows_first):
    n = x.shape[0] if rows_first else x.shape[1]
    idx = lax.broadcasted_iota(jnp.int32, x.shape, 0 if rows_first else 1)
    return jnp.where(idx < n // 2, x, 0), jnp.where(idx >= n // 2, x, 0)


def _band_tiles():
    nd = TQ // TK
    return [(jr, TK * max(jr, 0), TK * min(jr + BAND + 1, nd)) for jr in range(nd - 1, -BAND - 1, -1)]


def _rows_update(st, new, lo, hi):
    def one(x, y):
        pieces = ([x[:lo]] if lo else []) + [y] + ([x[hi:]] if hi < x.shape[0] else [])
        return pieces[0] if len(pieces) == 1 else jnp.concatenate(pieces, axis=0)
    return tuple(one(x, y) for x, y in zip(st, new))


def _scaled_q(q_ref, q0):
    return (q_ref[pl.ds(q0, TQ), :].astype(F32) * ATT_SCALE).astype(BF16)


def _qkv_specs(seq):
    n = SB_WIDTH // LANES
    return [pl.BlockSpec((seq, LANES), lambda e, p, off=off: (e, off * n + p)) for off in range(3)]


def _attn_fwd(qkv, seq, comm=None):
    t = qkv.shape[0]
    ne, nq, nk = t // seq, seq // TQ, seq // TK

    def body(ins, outs, scr):
        (q_ref, k_ref, v_ref), (o_ref, tot_ref, kmin_ref), (kbd_scr, vbd_scr) = ins, outs, scr
        causal = _att_consts()
        after = _sum_matrix("after")

        def prep(j, _):
            k0 = pl.multiple_of(j * TK, TK)
            top, bot = _head_diag(k_ref[pl.ds(k0, TK), :].astype(F32).T, True)
            kbd_scr[j] = jnp.concatenate([top, bot], axis=1).astype(BF16)
            left, right = _head_diag(v_ref[pl.ds(k0, TK), :], False)
            vbd_scr[j] = jnp.concatenate([left, right], axis=0)
            return 0

        lax.fori_loop(0, nk, prep, 0)

        def block(j, qb, st, mask):
            c0, c1, oacc = st
            lb, l1 = _att_logits(qb, kbd_scr[j])
            if mask is not None:
                l1 = jnp.where(mask, l1, 0.0)
            (s0, s1), (r0, r1) = _pair_sums(l1, after)
            att = jnp.exp(lb + jnp.concatenate([s0 + c0, s1 + c1], axis=1))
            if mask is not None:
                att = jnp.where(mask, att, 0.0)
            return c0 + r0, c1 + r1, oacc + _dot(att.astype(BF16), vbd_scr[j])

        def general(qi, qb, st):
            for jj in reversed(range(TQ // TK)):
                lo = TK * jj
                new = block((TQ // TK) * qi + jj, qb[lo:], tuple(x[lo:] for x in st), causal[:TQ - lo])
                st = _rows_update(st, new, lo, TQ)

            npair = (TQ // TK // 2) * qi

            def more(its):
                return (its[0] < npair) & (jnp.max(jnp.maximum(its[1], its[2])) > SKIP_LOG)

            def kloop(its):
                j = 2 * (npair - its[0]) - 1
                return (its[0] + 1,) + block(j - 1, qb, block(j, qb, its[1:], None), None)

            done, c0, c1, oacc = lax.while_loop(more, kloop, (jnp.int32(0),) + st)
            return c0, c1, oacc, 2 * (npair - done)

        def short(qi, qb, st):
            for jr, lo, hi in _band_tiles():
                new = block((TQ // TK) * qi + jr, qb[lo:hi], tuple(x[lo:hi] for x in st),
                            causal[:hi - lo] if jr >= 0 else None)
                st = _rows_update(st, new, lo, hi)
            return st

        def qloop(qi, _):
            q0 = pl.multiple_of(qi * TQ, TQ)
            qb = _scaled_q(q_ref, q0)
            zero = jnp.zeros((TQ, TK), F32)
            st = (zero, zero, jnp.zeros((TQ, LANES), F32))

            def try_short():
                c0, c1, oacc = short(qi, qb, st)
                return lax.cond(jnp.max(jnp.maximum(c0, c1)) <= SKIP_LOG, lambda: (c0, c1, oacc, jnp.int32(-1)),
                                lambda: general(qi, qb, st))

            c0, c1, oacc, first = lax.cond(qi > 0, try_short, lambda: general(qi, qb, st))
            o_ref[pl.ds(q0, TQ), :] = oacc
            tot_ref[pl.ds(q0, TQ), :] = jnp.concatenate([c0, c1], axis=1)
            kmin_ref[pl.program_id(0), pl.program_id(1), qi] = first
            return 0

        lax.fori_loop(0, nq, qloop, 0)

    (o, tot, kmin), got = _call(
        body, comm, grid=(ne, SB_WIDTH // LANES), in_specs=_qkv_specs(seq),
        out_specs=[pl.BlockSpec((seq, LANES), lambda e, p: (e, p)), pl.BlockSpec((seq, 2 * TK), lambda e, p: (e, p)),
                   pl.BlockSpec(memory_space=pltpu.SMEM)],
        out_shape=[jax.ShapeDtypeStruct((t, SB_WIDTH), F32), jax.ShapeDtypeStruct((t, 2 * TK * SB_WIDTH // LANES), F32),
                   jax.ShapeDtypeStruct((ne, SB_WIDTH // LANES, nq), jnp.int32)],
        scratch_shapes=[pltpu.VMEM((nk, LANES, 2 * TK), BF16), pltpu.VMEM((nk, 2 * TK, LANES), BF16)],
        args=(qkv, qkv, qkv), name="attn_fwd")
    return o, tot, kmin, got


def _outproj(y_lru, o, x, ga, gb, w_out):
    t = x.shape[0]
    tm = min(512, t)

    def kern(y_ref, o_ref, x_ref, ga_ref, gb_ref, w_ref, h1_ref, mix_ref):
        yv, ov = y_ref[...], o_ref[...]
        mix = jnp.concatenate([yv * _rstd(yv) * ga_ref[...], ov * _rstd(ov) * gb_ref[...]], axis=1).astype(BF16)
        mix_ref[...] = mix
        h1_ref[...] = x_ref[...] + _dot(mix, w_ref[...])

    row = lambda c: pl.BlockSpec((tm, c), lambda i: (i, 0))
    vec = lambda c: pl.BlockSpec((1, c), lambda i: (0, 0))
    return pl.pallas_call(
        kern, grid=(t // tm,),
        in_specs=[row(LRU_WIDTH), row(SB_WIDTH), row(D_MODEL), vec(LRU_WIDTH), vec(SB_WIDTH),
                  pl.BlockSpec((D_MODEL, D_MODEL), lambda i: (0, 0))],
        out_specs=[row(D_MODEL), row(D_MODEL)],
        out_shape=[jax.ShapeDtypeStruct((t, D_MODEL), F32), jax.ShapeDtypeStruct((t, D_MODEL), BF16)],
        compiler_params=_cp(("parallel",)), name="outproj",
    )(y_lru, o, x, ga, gb, w_out)


def _mlp_loss(h1, g2, w_up, w_down, target, gf):
    t = h1.shape[0]
    tm, tf = min(512, t), 1024

    def kern(h1_ref, g_ref, wu_ref, wd_ref, t_ref, gf_ref, hn_ref, up_ref, u2_ref, dh_ref, dhb_ref, loss_ref, dg_ref):
        @pl.when(pl.program_id(0) == 0)
        def _():
            loss_ref[...] = jnp.zeros_like(loss_ref)
            dg_ref[...] = jnp.zeros_like(dg_ref)

        hv = h1_ref[...]
        hn = (hv * _rstd(hv) * g_ref[...]).astype(BF16)
        hn_ref[...] = hn
        h2 = hv
        for f in range(D_FF // tf):
            cols = slice(f * tf, (f + 1) * tf)
            up = jnp.maximum(_dot(hn, wu_ref[:, cols]), 0.0)
            u2 = (up * up).astype(BF16)
            up_ref[:, cols] = up.astype(BF16)
            u2_ref[:, cols] = u2
            h2 = h2 + _dot(u2, wd_ref[cols, :])

        g = gf_ref[...]
        err = h2 * _rstd(h2) * g - t_ref[...]
        lane = lax.broadcasted_iota(jnp.int32, (1, LANES), 1)
        loss_ref[...] += jnp.where(lane == 0, 0.5 * jnp.sum(err * err) / D_MODEL, 0.0)
        dx, dg = _rms_bwd(h2, g, err * (1.0 / D_MODEL))
        dh_ref[...] = dx
        dhb_ref[...] = dx.astype(BF16)
        dg_ref[...] += dg

    row = lambda c: pl.BlockSpec((tm, c), lambda i: (i, 0))
    vec = pl.BlockSpec((1, D_MODEL), lambda i: (0, 0))
    return pl.pallas_call(
        kern, grid=(t // tm,),
        in_specs=[row(D_MODEL), vec, _resident((D_MODEL, D_FF)), _resident((D_FF, D_MODEL)), row(D_MODEL), vec],
        out_specs=[row(D_MODEL), row(D_FF), row(D_FF), row(D_MODEL), row(D_MODEL), pl.BlockSpec((1, LANES), lambda i: (0, 0)), vec],
        out_shape=[jax.ShapeDtypeStruct((t, D_MODEL), BF16), jax.ShapeDtypeStruct((t, D_FF), BF16),
                   jax.ShapeDtypeStruct((t, D_FF), BF16), jax.ShapeDtypeStruct((t, D_MODEL), F32),
                   jax.ShapeDtypeStruct((t, D_MODEL), BF16), jax.ShapeDtypeStruct((1, LANES), F32),
                   jax.ShapeDtypeStruct((1, D_MODEL), F32)],
        compiler_params=_cp(("arbitrary",), VMEM_LIMIT_BIG), name="mlp_loss",
    )(h1, g2, w_up, w_down, target, gf)


def _mlp_bwd_pre(dh2, w_down, up):
    t = dh2.shape[0]
    tm, tf = min(512, t), 1024

    def kern(d_ref, w_ref, up_ref, o_ref):
        dv = d_ref[...]
        for f in range(D_FF // tf):
            cols = slice(f * tf, (f + 1) * tf)
            o_ref[:, cols] = (_dot_nt(dv, w_ref[cols, :]) * (2.0 * up_ref[:, cols].astype(F32))).astype(BF16)

    row = lambda c: pl.BlockSpec((tm, c), lambda i: (i, 0))
    return pl.pallas_call(
        kern, grid=(t // tm,), in_specs=[row(D_MODEL), _resident((D_FF, D_MODEL)), row(D_FF)],
        out_specs=row(D_FF), out_shape=jax.ShapeDtypeStruct((t, D_FF), BF16),
        compiler_params=_cp(("parallel",)), name="mlp_bwd_pre",
    )(dh2, w_down, up)


def _proj_bwd_norm(dys, w, x, g, resid, name, comm=None):
    t = x.shape[0]
    tm = min(512, t)
    widths = [dy.shape[1] for dy in dys]
    n = len(dys)

    def body(ins, outs, _):
        dy_refs, (w_ref, x_ref, g_ref, r_ref), (dx_ref, dg_ref) = ins[:n], ins[n:], outs

        @pl.when(pl.program_id(0) == 0)
        def _():
            dg_ref[...] = jnp.zeros_like(dg_ref)

        off, dxn = 0, None
        for dy_ref, wd in zip(dy_refs, widths):
            part = _dot_nt(dy_ref[...], w_ref[:, off:off + wd])
            dxn = part if dxn is None else dxn + part
            off += wd
        dx, dg = _rms_bwd(x_ref[...], g_ref[...], dxn)
        dx_ref[...] = r_ref[...] + dx
        dg_ref[...] += dg

    row = lambda c: pl.BlockSpec((tm, c), lambda i: (i, 0))
    vec = pl.BlockSpec((1, D_MODEL), lambda i: (0, 0))
    (dx, dg), got = _call(
        body, comm, grid=(t // tm,), in_specs=[row(wd) for wd in widths] + [_resident(w.shape), row(D_MODEL), vec, row(D_MODEL)],
        out_specs=[row(D_MODEL), vec],
        out_shape=[jax.ShapeDtypeStruct((t, D_MODEL), F32), jax.ShapeDtypeStruct((1, D_MODEL), F32)],
        scratch_shapes=[], args=(*dys, w, x, g, resid), name=name)
    return dx, dg, got


def _outproj_bwd(dh1, w_out, y_lru, o, ga, gb, comm=None):
    t = dh1.shape[0]
    tm = min(512, t)

    def body(ins, outs, _):
        (d_ref, w_ref, y_ref, o_ref, ga_ref, gb_ref), (dy_ref, do_ref, dga_ref, dgb_ref) = ins, outs

        @pl.when(pl.program_id(0) == 0)
        def _():
            dga_ref[...] = jnp.zeros_like(dga_ref)
            dgb_ref[...] = jnp.zeros_like(dgb_ref)

        dmix = _dot_nt(d_ref[...].astype(BF16), w_ref[...])
        dy, dga = _rms_bwd(y_ref[...], ga_ref[...], dmix[:, :LRU_WIDTH])
        do, dgb = _rms_bwd(o_ref[...], gb_ref[...], dmix[:, LRU_WIDTH:])
        dy_ref[...] = dy
        do_ref[...] = do
        dga_ref[...] += dga
        dgb_ref[...] += dgb

    row = lambda c: pl.BlockSpec((tm, c), lambda i: (i, 0))
    vec = pl.BlockSpec((1, LRU_WIDTH), lambda i: (0, 0))
    half = jax.ShapeDtypeStruct((t, LRU_WIDTH), F32)
    gsum = jax.ShapeDtypeStruct((1, LRU_WIDTH), F32)
    outs, got = _call(body, comm, grid=(t // tm,),
                      in_specs=[row(D_MODEL), _resident((D_MODEL, D_MODEL)), row(LRU_WIDTH), row(SB_WIDTH), vec, vec],
                      out_specs=[row(LRU_WIDTH), row(SB_WIDTH), vec, vec], out_shape=[half, half, gsum, gsum],
                      scratch_shapes=[], args=(dh1, w_out, y_lru, o, ga, gb), name="outproj_bwd")
    return (*outs, got)


def _attn_bwd(qkv, do, tot, kmin, seq):
    t = qkv.shape[0]
    ne, nq, nk = t // seq, seq // TQ, seq // TK

    def kern(q_ref, k_ref, v_ref, do_ref, tot_ref, kmin_ref, dq_ref, dk_ref, dv_ref,
             kbd_scr, vtbd_scr, kbd2_scr, dkt_scr, dvt_scr):
        causal = _att_consts()
        upto, before = _sum_matrix("upto"), _sum_matrix("before")

        def prep(j, _):
            k0 = pl.multiple_of(j * TK, TK)
            kb = k_ref[pl.ds(k0, TK), :]
            top, bot = _head_diag(kb.astype(F32).T, True)
            kbd_scr[j] = jnp.concatenate([top, bot], axis=1).astype(BF16)
            top, bot = _head_diag(v_ref[pl.ds(k0, TK), :].astype(F32).T, True)
            vtbd_scr[j] = jnp.concatenate([top, bot], axis=1).astype(BF16)
            left, right = _head_diag(kb, False)
            kbd2_scr[j] = jnp.concatenate([left, right], axis=0)
            dkt_scr[j] = jnp.zeros((LANES, 2 * TK), F32)
            dvt_scr[j] = jnp.zeros((LANES, 2 * TK), F32)
            return 0

        lax.fori_loop(0, nk, prep, 0)

        def block(j, qb, qt, dob, dot_, totb, st, mask):
            f0, f1, p0, p1, dqacc = st
            lb, l1 = _att_logits(qb, kbd_scr[j])
            if mask is not None:
                l1 = jnp.where(mask, l1, 0.0)
            (s0, s1), (r0, r1) = _pair_sums(l1, upto)
            att = jnp.exp(lb + (totb - jnp.concatenate([s0 + f0, s1 + f1], axis=1)))
            if mask is not None:
                att = jnp.where(mask, att, 0.0)
            pw = att * _dot(dob, vtbd_scr[j])
            (e0, e1), (t0, t1) = _pair_sums(pw, before)
            dz = pw - jnp.exp(lb) * (pw + jnp.concatenate([e0 + p0, e1 + p1], axis=1))
            if mask is not None:
                dz = jnp.where(mask, dz, 0.0)
            dzb = dz.astype(BF16)
            dkt_scr[j] += _dot(qt, dzb)
            dvt_scr[j] += _dot(dot_, att.astype(BF16))
            return f0 + r0, f1 + r1, p0 + t0, p1 + t1, dqacc + _dot(dzb, kbd2_scr[j])

        def qloop(qi, _):
            q0 = pl.multiple_of(qi * TQ, TQ)
            qb = _scaled_q(q_ref, q0)
            qt = qb.astype(F32).T.astype(BF16)
            do32 = do_ref[pl.ds(q0, TQ), :]
            dob, dot_ = do32.astype(BF16), do32.T.astype(BF16)
            totb = tot_ref[pl.ds(q0, TQ), :]
            zero = jnp.zeros((TQ, TK), F32)
            st = (zero, zero, zero, zero, jnp.zeros((TQ, LANES), F32))

            k0 = kmin_ref[pl.program_id(0), pl.program_id(1), qi]

            def tile(j, lo, hi, st, masked):
                new = block(j, qb[lo:hi], qt[:, lo:hi], dob[lo:hi], dot_[:, lo:hi], totb[lo:hi],
                            tuple(x[lo:hi] for x in st), causal[:hi - lo] if masked else None)
                return _rows_update(st, new, lo, hi)

            def general():
                def kloop(it, st):
                    j = k0 + 2 * it
                    return block(j + 1, qb, qt, dob, dot_, totb, block(j, qb, qt, dob, dot_, totb, st, None), None)

                out = lax.fori_loop(0, ((TQ // TK) * qi - k0) // 2, kloop, st)
                for jj in range(TQ // TK):
                    out = tile((TQ // TK) * qi + jj, TK * jj, TQ, out, True)
                return out

            def short():
                out = st
                for jr, lo, hi in reversed(_band_tiles()):
                    out = tile((TQ // TK) * qi + jr, lo, hi, out, jr >= 0)
                return out

            st = lax.cond(k0 < 0, short, general)
            dq_ref[pl.ds(q0, TQ), :] = (st[4] * ATT_SCALE).astype(BF16)
            return 0

        lax.fori_loop(0, nq, qloop, 0)

        def finish(j, _):
            k0 = pl.multiple_of(j * TK, TK)
            head0 = lax.broadcasted_iota(jnp.int32, (LANES, TK), 0) < DH
            for src, dst in ((dkt_scr, dk_ref), (dvt_scr, dv_ref)):
                acc = src[j]
                dst[pl.ds(k0, TK), :] = jnp.where(head0, acc[:, :TK], acc[:, TK:]).T.astype(BF16)
            return 0

        lax.fori_loop(0, nk, finish, 0)

    blk = pl.BlockSpec((seq, LANES), lambda e, p: (e, p))
    grad = jax.ShapeDtypeStruct((t, SB_WIDTH), BF16)
    return pl.pallas_call(
        kern, grid=(ne, SB_WIDTH // LANES),
        in_specs=_qkv_specs(seq) + [blk, pl.BlockSpec((seq, 2 * TK), lambda e, p: (e, p)),
                                    pl.BlockSpec(memory_space=pltpu.SMEM)],
        out_specs=[blk, blk, blk], out_shape=[grad, grad, grad],
        scratch_shapes=[pltpu.VMEM((nk, LANES, 2 * TK), BF16), pltpu.VMEM((nk, LANES, 2 * TK), BF16),
                        pltpu.VMEM((nk, 2 * TK, LANES), BF16), pltpu.VMEM((nk, LANES, 2 * TK), F32),
                        pltpu.VMEM((nk, LANES, 2 * TK), F32)],
        compiler_params=_cp(("parallel", "parallel")), name="attn_bwd",
    )(qkv, qkv, qkv, do, tot, kmin)


def _lru_bwd(xl, h, dy, conv_w, conv_b, wbd, ba, bx, lam, seq, comm=None):
    t = xl.shape[0]
    tc = min(512, seq)
    nc = seq // tc
    nb = tc // SUBLANES


    def body(ins, outs, scr):
        u_ref, g_ref, up_ref, h_ref, hp_ref, dy_ref, cw_ref, cb_ref, wbd_ref, ba_ref, bx_ref, lam_ref = ins
        (dxl_ref, small_ref, dwbd_ref), (lnext_ref, anext_ref, dcnext_ref) = outs, scr
        e, ci = pl.program_id(0), pl.program_id(1)
        first = ci == nc - 1

        @pl.when((e == 0) & (ci == 0))
        def _():
            small_ref[...] = jnp.zeros_like(small_ref)
            dwbd_ref[...] = jnp.zeros_like(dwbd_ref)

        @pl.when(ci == 0)
        def _():
            lnext_ref[...] = jnp.zeros_like(lnext_ref)
            anext_ref[...] = jnp.zeros_like(anext_ref)
            dcnext_ref[...] = jnp.zeros_like(dcnext_ref)

        u, g = u_ref[...], g_ref[...]
        keep = jnp.where(first, 0.0, 1.0)
        taps = _conv_taps(keep * up_ref[...], u)
        c = cb_ref[...]
        for k in range(CONV_WIDTH):
            c = c + taps[k] * cw_ref[k:k + 1, :]
        lam = lam_ref[...]
        sp = _softplus(-lam)
        r, i, a, mult, inv_mult = _lru_gates(c, wbd_ref, ba_ref[...], bx_ref[...], sp)
        gel, th = _gelu(g)
        dyv, hv = dy_ref[...], h_ref[...]
        dg = dyv * hv * _gelu_grad(g, th)

        aa, bb = _scan_rev(_shift_up(a, 1, anext_ref[0:1, :]), dyv * gel)
        lt = bb + aa * lnext_ref[0:1, :]
        lnext_ref[0:1, :] = _row_of(lt, 0)
        anext_ref[0:1, :] = _row_of(a, 0)

        hprev = _shift_down(hv, 1, keep * hp_ref[SUBLANES - 1:SUBLANES, :])
        da = lt * hprev
        dmult = lt * i * c
        di = lt * mult * c
        dc = lt * mult * i
        dla = da * a - dmult * (a * a) * inv_mult
        dga = dla * ((-LRU_C) * sp) * r * (1.0 - r)
        dgx = di * i * (1.0 - i)
        small_ref[7:8, :] += jnp.sum(dla * r, axis=0, keepdims=True) * (LRU_C * _sigmoid(-lam))
        small_ref[5:6, :] += jnp.sum(dga, axis=0, keepdims=True)
        small_ref[6:7, :] += jnp.sum(dgx, axis=0, keepdims=True)

        dcs = []
        for p in range(LRU_WIDTH // LANES):
            cols = slice(LANES * p, LANES * (p + 1))
            dgax = jnp.concatenate([dga[:, cols], dgx[:, cols]], axis=1).astype(BF16)
            dcs.append(_dot_nt(dgax, wbd_ref[p]))
            dwbd_ref[p] += _dot_tn(c[:, cols].astype(BF16), dgax)
        dc = dc + jnp.concatenate(dcs, axis=1)
        small_ref[4:5, :] += jnp.sum(dc, axis=0, keepdims=True)

        catd = jnp.concatenate([dc, dcnext_ref[...]], axis=0)
        du = dc * cw_ref[CONV_WIDTH - 1:CONV_WIDTH, :]
        for j in range(1, CONV_WIDTH):
            du = du + pltpu.roll(catd, tc + SUBLANES - j, 0)[:tc] * cw_ref[CONV_WIDTH - 1 - j:CONV_WIDTH - j, :]
        dcnext_ref[...] = dc[:SUBLANES]
        for k in range(CONV_WIDTH):
            small_ref[k:k + 1, :] += jnp.sum(dc * taps[k], axis=0, keepdims=True)
        dxl_ref[:, :LRU_WIDTH] = du.astype(BF16)
        dxl_ref[:, LRU_WIDTH:] = dg.astype(BF16)

    rev = lambda e, c: e * nc + (nc - 1 - c)
    chunk = lambda col: pl.BlockSpec((tc, LRU_WIDTH), lambda e, c: (rev(e, c), col))
    prev8 = pl.BlockSpec((SUBLANES, LRU_WIDTH), lambda e, c: (jnp.maximum(rev(e, c) * nb - 1, 0), 0))
    outs, got = _call(
        body, comm, grid=(t // seq, nc),
        in_specs=[chunk(0), chunk(1), prev8, chunk(0), prev8, chunk(0)] + _lru_param_specs(2),
        out_specs=[pl.BlockSpec((tc, 2 * LRU_WIDTH), lambda e, c: (rev(e, c), 0)),
                   pl.BlockSpec((SUBLANES, LRU_WIDTH), lambda e, c: (0, 0)),
                   pl.BlockSpec((LRU_WIDTH // LANES, LANES, 2 * LANES), lambda e, c: (0, 0, 0))],
        out_shape=[jax.ShapeDtypeStruct((t, 2 * LRU_WIDTH), BF16), jax.ShapeDtypeStruct((SUBLANES, LRU_WIDTH), F32),
                   jax.ShapeDtypeStruct((LRU_WIDTH // LANES, LANES, 2 * LANES), F32)],
        scratch_shapes=[pltpu.VMEM((SUBLANES, LRU_WIDTH), F32)] * 3,
        args=(xl, xl, xl, h, h, dy, conv_w, conv_b, wbd, ba, bx, lam), name="lru_bwd")
    return (*outs, got)


def _adam_math(w, g, m, v):
    m2 = ADAM_B1 * m + (1.0 - ADAM_B1) * g
    v2 = ADAM_B2 * v + (1.0 - ADAM_B2) * (g * g)
    m_hat = m2 / (1.0 - ADAM_B1 ** ADAM_STEP)
    v_hat = v2 / (1.0 - ADAM_B2 ** ADAM_STEP)
    return -ADAM_LR * (m_hat / (jnp.sqrt(v_hat) + ADAM_EPS) + ADAM_WD * w), m2, v2


def _adamw(w, g, m, v, name):
    rows, cols = w.shape
    tr = 256 if rows % 256 == 0 else rows

    def kern(w_ref, g_ref, m_ref, v_ref, d_ref, m2_ref, v2_ref):
        d_ref[...], m2_ref[...], v2_ref[...] = _adam_math(w_ref[...], g_ref[...], m_ref[...], v_ref[...])

    blk = pl.BlockSpec((tr, cols), lambda i: (i, 0))
    out = jax.ShapeDtypeStruct((rows, cols), F32)
    return pl.pallas_call(kern, grid=(rows // tr,), in_specs=[blk] * 4, out_specs=[blk] * 3, out_shape=[out] * 3,
                          compiler_params=_cp(("parallel",)), name=name)(w, g, m, v)


def _adamw_small(ws, gs, ms, vs):
    n = len(ws)

    def kern(*refs):
        for k in range(n):
            outs = _adam_math(refs[k][...], refs[n + k][...], refs[2 * n + k][...], refs[3 * n + k][...])
            for j in range(3):
                refs[(4 + j) * n + k][...] = outs[j]

    vm = pl.BlockSpec(memory_space=pltpu.VMEM)
    out = pl.pallas_call(kern, in_specs=[vm] * (4 * n), out_specs=[vm] * (3 * n),
                         out_shape=[jax.ShapeDtypeStruct(w.shape, F32) for w in ws] * 3, name="adamw_small")(*ws, *gs, *ms, *vs)
    return out[:n], out[n:2 * n], out[2 * n:]


def _comm_only(kind, arrays, name):
    return _call(lambda ins, outs, scr: None, (kind, arrays), grid=(1,), in_specs=[], out_specs=[], out_shape=[],
                 scratch_shapes=[], args=(), name=name)[1]


def _pair_add(g, got, core):
    _, rows, cols = g.shape
    half = rows // 2
    tr = min(256, half)
    nt = half // tr

    def kern(c_ref, g_ref, o_ref, out_ref):
        out_ref[...] = (g_ref[...] + o_ref[...]).astype(BF16)

    return pl.pallas_call(
        kern, grid_spec=pltpu.PrefetchScalarGridSpec(
            num_scalar_prefetch=1, grid=(N_CHIPS, nt),
            in_specs=[pl.BlockSpec((None, tr, cols), lambda j, i, c_ref: (j, c_ref[0] * nt + i, 0)),
                      pl.BlockSpec((None, tr, cols), lambda j, i, c_ref: (j, i, 0))],
            out_specs=pl.BlockSpec((None, tr, cols), lambda j, i, c_ref: (j, i, 0))),
        out_shape=jax.ShapeDtypeStruct((N_CHIPS, half, cols), BF16),
        compiler_params=_cp(("parallel", "parallel")), name="pair_add",
    )(core, g, got)


def _chip_add(part, got, place):
    _, half, cols = part.shape
    tr = min(256, half)
    nt = half // tr

    def kern(p_ref, part_ref, got_ref, out_ref):
        out_ref[...] = (part_ref[...].astype(F32) + got_ref[0].astype(F32) + got_ref[1].astype(F32)
                        + got_ref[2].astype(F32))

    return pl.pallas_call(
        kern, grid_spec=pltpu.PrefetchScalarGridSpec(
            num_scalar_prefetch=1, grid=(nt,),
            in_specs=[pl.BlockSpec((None, tr, cols), lambda i, p_ref: (p_ref[0], i, 0)),
                      pl.BlockSpec((3, tr, cols), lambda i, p_ref: (0, i, 0))],
            out_specs=pl.BlockSpec((tr, cols), lambda i, p_ref: (p_ref[1] * nt + i, 0))),
        out_shape=jax.ShapeDtypeStruct((2 * half, cols), F32),
        compiler_params=_cp(("parallel",)), name="chip_add",
    )(place, part, got)


def _finale(packed, fulls):
    rows, n = packed.shape[0], len(fulls)

    def kern(in_ref, *refs):
        ins, out_ref, outs = refs[:n], refs[n], refs[n + 1:2 * n + 1]
        slots, send_sems, recv_sems, join_send, join_recv = refs[2 * n + 1:]
        x, y, c = _place()
        mine = 4 * x + 2 * y + c
        copies = []
        for w in range(n):
            half = ins[w].shape[0] // 2
            rws = pl.ds(c * half, half)
            copies.append(pltpu.make_async_remote_copy(
                src_ref=ins[w].at[rws, :], dst_ref=outs[w].at[rws, :], send_sem=join_send.at[w],
                recv_sem=join_recv.at[w], device_id=(x, y, 1 - c), device_id_type=MESH))
        for k in range(1, N_DEV):
            peer = (x ^ (k >> 2), y ^ ((k >> 1) & 1), c ^ (k & 1))
            copies.append(pltpu.make_async_remote_copy(
                src_ref=in_ref, dst_ref=slots.at[mine], send_sem=send_sems.at[k - 1], recv_sem=recv_sems.at[k - 1],
                device_id=peer, device_id_type=MESH))
        for cp in copies:
            cp.start()
        slots[mine] = in_ref[...]
        for cp in copies:
            cp.wait()
        acc = slots[0]
        for sl in range(1, N_DEV):
            acc = acc + slots[sl]
        out_ref[...] = acc

    vm = pl.BlockSpec(memory_space=pltpu.VMEM)
    out = pl.pallas_call(
        kern, in_specs=[vm] + [ANY] * n, out_specs=[vm] + [ANY] * n,
        out_shape=[jax.ShapeDtypeStruct((rows, LANES), F32)] + [jax.ShapeDtypeStruct(f.shape, f.dtype) for f in fulls],
        input_output_aliases={w + 1: w + 1 for w in range(n)},
        scratch_shapes=[pltpu.VMEM((N_DEV, rows, LANES), F32), pltpu.SemaphoreType.DMA((N_DEV - 1,)),
                        pltpu.SemaphoreType.DMA((N_DEV - 1,)), pltpu.SemaphoreType.DMA((n,)), pltpu.SemaphoreType.DMA((n,))],
        name="finale",
    )(packed, *fulls)
    return out[0], list(out[1:])


SMALL = ["norm1_g", "conv_w", "conv_b", "lru_w_a", "lru_b_a", "lru_w_x", "lru_b_x", "lru_lambda", "lru_out_g", "sb_out_g",
         "norm2_g", "final_g"]
BIG = ["w_in", "w_out", "w_up", "w_down"]
WEIGHTS = ["norm1_g", "w_in", "conv_w", "conv_b", "lru_w_a", "lru_b_a", "lru_w_x", "lru_b_x", "lru_lambda", "lru_out_g",
           "sb_out_g", "w_out", "norm2_g", "w_up", "w_down", "final_g"]


def _pack(arrays):
    flat = []
    for a in arrays:
        a = a.reshape(-1).astype(F32)
        flat.append(jnp.pad(a, (0, (-a.shape[0]) % LANES)))
    v = jnp.concatenate(flat)
    v = jnp.pad(v, (0, (-v.shape[0]) % (LANES * SUBLANES)))
    return v.reshape(-1, LANES)


def _unpack(packed, shapes):
    v, out, off = packed.reshape(-1), [], 0
    for shp in shapes:
        size = math.prod(shp)
        out.append(v[off:off + size].reshape(shp))
        off += size + (-size) % LANES
    return out


def _blockdiag_pairs(w):
    w = w.reshape(4, 2, DH, DH)
    z = jnp.zeros((4, DH, DH), w.dtype)
    return jnp.concatenate([jnp.concatenate([w[:, 0], z], axis=2), jnp.concatenate([z, w[:, 1]], axis=2)], axis=1)


def _blockdiag_unpairs(wbd):
    return jnp.stack([wbd[:, :DH, :DH], wbd[:, DH:, DH:]], axis=1).reshape(8, DH, DH)


def _full_cols(g):
    return jnp.transpose(g, (1, 0, 2)).reshape(g.shape[1], N_CHIPS * g.shape[2])


def _local_step(x2, tgt, seq, norm1_g, w_in, conv_w, conv_b, w_a, b_a, w_x, b_x, lru_lambda, lru_out_g, sb_out_g, rest,
                norm2_g, final_g, place=None):
    alone = place is None
    wbd = jnp.concatenate([_blockdiag_pairs(w_a), _blockdiag_pairs(w_x)], axis=2).astype(BF16)
    ba, bx = b_a.reshape(1, LRU_WIDTH), b_x.reshape(1, LRU_WIDTH)
    gf = final_g.reshape(1, D_MODEL)

    xn, got = _norm1(x2, norm1_g, None if alone else ("gather", [w_in, conv_w]))
    w_in_f, conv_w_f = (w_in, conv_w) if alone else (_full_cols(got[0]), _full_cols(got[1]))
    xl, qkv = _inproj(xn, w_in_f)
    h, y_lru = _lru_fwd(xl, conv_w_f, conv_b, wbd, ba, bx, lru_lambda, seq)
    o, tot, kmin, got = _attn_fwd(qkv, seq, None if alone else ("gather", rest))
    w_out_f, w_up_f, w_down_f = rest if alone else (
        got[0].reshape(D_MODEL, D_MODEL), _full_cols(got[1]), got[2].reshape(D_FF, D_MODEL))
    h1, mix = _outproj(y_lru, o, x2, lru_out_g, sb_out_g, w_out_f)
    hn, up, u2, dh2, dh2b, loss_part, d_final = _mlp_loss(h1, norm2_g, w_up_f, w_down_f, tgt, gf)

    dpre = _mlp_bwd_pre(dh2b, w_down_f, up)
    g_w_down = _matmul(u2, dh2b, "tn", F32, "dw_down", 1024, 1024, 1024).reshape(N_CHIPS, D_FF // N_CHIPS, D_MODEL)
    g_w_up = _matmul(hn, dpre, "tn", F32, "dw_up", 1024, D_FF // N_CHIPS, 1024, split_cols=True)
    dh1, d_norm2, _ = _proj_bwd_norm([dpre], w_up_f, h1, norm2_g, dh2, "mlp_bwd_in")
    g_w_out = _matmul(mix, dh1, "tn", F32, "dw_out", 1024, 1024, 2048).reshape(N_CHIPS, D_MODEL // N_CHIPS, D_MODEL)
    late = [g_w_out, g_w_up, g_w_down]
    dy_lru, do, d_ga, d_gb, swapped = _outproj_bwd(dh1, w_out_f, y_lru, o, lru_out_g, sb_out_g,
                                                   None if alone else ("swap", late))
    parts = None if alone else [_pair_add(g, r, place[1:]) for g, r in zip(late, swapped)]
    dq, dk, dv = _attn_bwd(qkv, do, tot, kmin, seq)
    dxl, lru_small, d_wbd, got = _lru_bwd(xl, h, dy_lru, conv_w_f, conv_b, wbd, ba, bx, lru_lambda, seq,
                                          None if alone else ("exchange", parts))
    if not alone:
        late = [_chip_add(p, r, place) for p, r in zip(parts, got)]
    dproj = jnp.concatenate([dxl, dq, dk, dv], axis=1)
    g_w_in = _matmul(xn, dproj, "tn", F32, "dw_in", 1024, IN_COLS // N_CHIPS, 2048, split_cols=True)
    part = None if alone else _pair_add(g_w_in, _comm_only("swap", [g_w_in], "pair_swap")[0], place[1:])
    dx, d_norm1, got = _proj_bwd_norm([dxl, dq, dk, dv], w_in_f, x2, norm1_g, dh1, "inproj_bwd",
                                      None if alone else ("exchange", [part]))
    if not alone:
        g_w_in = _chip_add(part, got[0], place)
    small_parts = {
        "norm1_g": d_norm1, "conv_w": lru_small[:CONV_WIDTH], "conv_b": lru_small[4:5],
        "lru_w_a": _blockdiag_unpairs(d_wbd[:, :, :LANES]), "lru_b_a": lru_small[5:6],
        "lru_w_x": _blockdiag_unpairs(d_wbd[:, :, LANES:]), "lru_b_x": lru_small[6:7], "lru_lambda": lru_small[7:8],
        "lru_out_g": d_ga, "sb_out_g": d_gb, "norm2_g": d_norm2, "final_g": d_final,
    }
    return loss_part, dx, [g_w_in] + late, small_parts


def kernel(x, norm1_g, w_in, conv_w, conv_b, lru_w_a, lru_b_a, lru_w_x, lru_b_x, lru_lambda, lru_out_g, sb_out_g, w_out, norm2_g, w_up, w_down, final_g, loss_target, m_norm1_g, m_w_in, m_conv_w, m_conv_b, m_lru_w_a, m_lru_b_a, m_lru_w_x, m_lru_b_x, m_lru_lambda, m_lru_out_g, m_sb_out_g, m_w_out, m_norm2_g, m_w_up, m_w_down, m_final_g, v_norm1_g, v_w_in, v_conv_w, v_conv_b, v_lru_w_a, v_lru_b_a, v_lru_w_x, v_lru_b_x, v_lru_lambda, v_lru_out_g, v_sb_out_g, v_w_out, v_norm2_g, v_w_up, v_w_down, v_final_g):
    given = dict(locals())
    ne, seq, _ = x.shape
    t = ne * seq
    xi, yi, ci = _place()
    place = jnp.stack([2 * xi + yi, ci]).astype(jnp.int32)

    loss_part, dx, halves, small_parts = _local_step(
        x.reshape(t, D_MODEL), loss_target.reshape(t, D_MODEL), seq, norm1_g, w_in[0].astype(BF16), conv_w[0], conv_b,
        lru_w_a[0], lru_b_a, lru_w_x[0], lru_b_x, lru_lambda, lru_out_g, sb_out_g,
        [w_out[0].astype(BF16), w_up[0].astype(BF16), w_down[0].astype(BF16)], norm2_g, final_g, place)

    full_shapes = {n: ((CONV_WIDTH, LRU_WIDTH) if n == "conv_w" else given[n].shape) for n in SMALL}
    red, fulls = _finale(_pack([small_parts[n] for n in SMALL] + [loss_part]), halves)
    red_list = _unpack(red, [full_shapes[n] for n in SMALL] + [(1, LANES)])
    grads = dict(zip(SMALL, red_list[:-1]))
    loss = red_list[-1][0, 0]
    grads["conv_w"] = lax.dynamic_slice_in_dim(grads["conv_w"], place[0] * (LRU_WIDTH // N_CHIPS), LRU_WIDTH // N_CHIPS,
                                               axis=1).reshape(conv_w.shape)
    for n, full in zip(BIG, fulls):
        grads[n] = full.reshape(given[n].shape)

    delta, new_m, new_v = {}, {}, {}
    for n in BIG:
        shp = given[n].shape
        d, m2, v2 = _adamw(given[n][0], grads[n][0], given["m_" + n][0], given["v_" + n][0], "adamw_" + n)
        delta[n], new_m[n], new_v[n] = d.reshape(shp), m2.reshape(shp), v2.reshape(shp)
    as2d = lambda a: a.reshape(-1, a.shape[-1])
    ds, m2s, v2s = _adamw_small([as2d(given[n]) for n in SMALL], [as2d(grads[n]) for n in SMALL],
                                [as2d(given["m_" + n]) for n in SMALL], [as2d(given["v_" + n]) for n in SMALL])
    for n, dd, mm, vv in zip(SMALL, ds, m2s, v2s):
        shp = given[n].shape
        delta[n], new_m[n], new_v[n] = dd.reshape(shp), mm.reshape(shp), vv.reshape(shp)

    return (loss, dx.reshape(x.shape), *[grads[n] for n in WEIGHTS], *[delta[n] for n in WEIGHTS],
            *[new_m[n] for n in WEIGHTS], *[new_v[n] for n in WEIGHTS])
```

```python
import functools
import math

import jax
import jax.numpy as jnp
from jax import lax
from jax.experimental import pallas as pl
from jax.experimental.pallas import tpu as pltpu

F32, BF16 = jnp.float32, jnp.bfloat16
MESH = pl.DeviceIdType.MESH

D_MODEL = 1024
LRU_WIDTH = 512
SB_WIDTH = 512
DH = 64
IN_COLS = 2 * LRU_WIDTH + 3 * SB_WIDTH
D_FF = 4 * D_MODEL
CONV_WIDTH = 4
LRU_C = 8.0
EPS = 1e-6
N_CHIPS = 4
N_DEV = 8
LANES = 128
SUBLANES = 8
TQ = 512
TK = 128
ATT_SCALE = 1.0 / math.sqrt(DH)
SKIP_LOG = -105.0
BAND = 2
VMEM_LIMIT = 52 * 1024 * 1024
VMEM_LIMIT_BIG = 62 * 1024 * 1024

ADAM_LR, ADAM_B1, ADAM_B2, ADAM_EPS, ADAM_WD, ADAM_STEP = 0.001, 0.9, 0.999, 1e-08, 0.01, 10

_GELU_K = math.sqrt(2.0 / math.pi)
_GELU_C = 0.044715


def _cp(sem, vmem=VMEM_LIMIT):
    return pltpu.CompilerParams(dimension_semantics=sem, vmem_limit_bytes=vmem)


def _dot(a, b):
    return jnp.dot(a, b, preferred_element_type=F32)


def _dot_nt(a, b):
    return lax.dot_general(a, b, (((1,), (1,)), ((), ())), preferred_element_type=F32)


def _dot_tn(a, b):
    return lax.dot_general(a, b, (((0,), (0,)), ((), ())), preferred_element_type=F32)


def _rstd(x):
    return lax.rsqrt(jnp.mean(x * x, axis=-1, keepdims=True) + EPS)


def _rms_bwd(x, g, dy):
    r = _rstd(x)
    gd = g * dy
    dx = r * gd - x * (r * r * r) * jnp.mean(x * gd, axis=-1, keepdims=True)
    return dx, jnp.sum(dy * x * r, axis=0, keepdims=True)


def _sigmoid(x):
    return 0.5 * jnp.tanh(0.5 * x) + 0.5


def _softplus(x):
    return jnp.maximum(x, 0.0) + jnp.log(1.0 + jnp.exp(-jnp.abs(x)))


def _neg_expm1(x, ex):
    series = -x * (1.0 + x * (0.5 + x * (1.0 / 6.0)))
    return jnp.where(x > -2.0 ** -7, series, 1.0 - ex)


def _gelu(g):
    t = jnp.tanh(_GELU_K * (g + _GELU_C * g * g * g))
    return 0.5 * g * (1.0 + t), t


def _gelu_grad(g, t):
    return 0.5 * (1.0 + t) + 0.5 * g * (1.0 - t * t) * _GELU_K * (1.0 + 3.0 * _GELU_C * g * g)


def _rows(shape):
    return lax.broadcasted_iota(jnp.int32, shape, 0)


def _shift_down(x, s, fill):
    n, c = x.shape
    if s % SUBLANES == 0:
        return jnp.concatenate([jnp.broadcast_to(jnp.asarray(fill, x.dtype), (s, c)), x[:n - s]], axis=0)
    return jnp.where(_rows(x.shape) >= s, pltpu.roll(x, s, 0), fill)


def _shift_up(x, s, fill):
    n, c = x.shape
    if s % SUBLANES == 0:
        return jnp.concatenate([x[s:], jnp.broadcast_to(jnp.asarray(fill, x.dtype), (s, c))], axis=0)
    return jnp.where(_rows(x.shape) < n - s, pltpu.roll(x, n - s, 0), fill)


def _row_of(x, idx):
    return jnp.sum(jnp.where(_rows(x.shape) == idx, x, 0.0), axis=0, keepdims=True)


def _matmul(a, b, dims, out_dtype, name, tm, tn, tk, split_cols=False):
    if dims == "nn":
        (m, kk), n, dot = a.shape, b.shape[1], _dot
    elif dims == "nt":
        (m, kk), n, dot = a.shape, b.shape[0], _dot_nt
    else:
        (kk, m), n, dot = a.shape, b.shape[1], _dot_tn
    tm, tn, tk = min(tm, m), min(tn, n), min(tk, kk)
    assert m % tm == 0 and n % tn == 0 and kk % tk == 0, (name, m, n, kk)
    nk = kk // tk

    def kern(a_ref, b_ref, o_ref, acc_ref):
        k = pl.program_id(2)

        @pl.when(k == 0)
        def _():
            acc_ref[...] = jnp.zeros_like(acc_ref)

        acc_ref[...] += dot(a_ref[...].astype(BF16), b_ref[...].astype(BF16))

        @pl.when(k == nk - 1)
        def _():
            o_ref[...] = acc_ref[...].astype(o_ref.dtype)

    if split_cols:
        out_shape = jax.ShapeDtypeStruct((n // tn, m, tn), out_dtype)
        o_spec = pl.BlockSpec((None, tm, tn), lambda i, j, k: (j, i, 0))
    else:
        out_shape = jax.ShapeDtypeStruct((m, n), out_dtype)
        o_spec = pl.BlockSpec((tm, tn), lambda i, j, k: (i, j))
    if dims == "nn":
        a_spec = pl.BlockSpec((tm, tk), lambda i, j, k: (i, k))
        b_spec = pl.BlockSpec((tk, tn), lambda i, j, k: (k, j))
    elif dims == "nt":
        a_spec = pl.BlockSpec((tm, tk), lambda i, j, k: (i, k))
        b_spec = pl.BlockSpec((tn, tk), lambda i, j, k: (j, k))
    else:
        a_spec = pl.BlockSpec((tk, tm), lambda i, j, k: (k, i))
        b_spec = pl.BlockSpec((tk, tn), lambda i, j, k: (k, j))
    return pl.pallas_call(
        kern, grid=(m // tm, n // tn, nk), in_specs=[a_spec, b_spec], out_specs=o_spec, out_shape=out_shape,
        scratch_shapes=[pltpu.VMEM((tm, tn), F32)], compiler_params=_cp(("parallel", "parallel", "arbitrary")), name=name,
    )(a, b)


ANY = pl.BlockSpec(memory_space=pl.ANY)


def _place():
    return lax.axis_index("x"), lax.axis_index("y"), lax.axis_index("c")


def _other_chips(x, y):
    return [(1 - x, y), (x, 1 - y), (1 - x, 1 - y)]


def _gather_copies(ins, outs, send_sems, recv_sems, loc_sems):
    x, y, c = _place()
    mine = 2 * x + y
    copies = []
    for w in range(len(ins)):
        copies.append(pltpu.make_async_copy(ins[w], outs[w].at[mine], loc_sems.at[w]))
        for k, chip in enumerate(_other_chips(x, y)):
            copies.append(pltpu.make_async_remote_copy(
                src_ref=ins[w], dst_ref=outs[w].at[mine], send_sem=send_sems.at[3 * w + k],
                recv_sem=recv_sems.at[3 * w + k], device_id=(*chip, c), device_id_type=MESH))
    return copies


def _gather_shapes(shards):
    return ([jax.ShapeDtypeStruct((N_CHIPS,) + s.shape, s.dtype) for s in shards],
            [pltpu.SemaphoreType.DMA((3 * len(shards),)), pltpu.SemaphoreType.DMA((3 * len(shards),)),
             pltpu.SemaphoreType.DMA((len(shards),))])


def _exchange_copies(ins, outs, send_sems, recv_sems):
    x, y, c = _place()
    copies = []
    for w in range(len(ins)):
        for k, chip in enumerate(_other_chips(x, y)):
            copies.append(pltpu.make_async_remote_copy(
                src_ref=ins[w].at[2 * chip[0] + chip[1]], dst_ref=outs[w].at[k], send_sem=send_sems.at[3 * w + k],
                recv_sem=recv_sems.at[3 * w + k], device_id=(*chip, c), device_id_type=MESH))
    return copies


def _exchange_shapes(parts):
    return ([jax.ShapeDtypeStruct((3,) + p.shape[1:], p.dtype) for p in parts],
            [pltpu.SemaphoreType.DMA((3 * len(parts),)), pltpu.SemaphoreType.DMA((3 * len(parts),))])


def _swap_copies(ins, outs, send_sems, recv_sems):
    x, y, c = _place()
    copies = []
    for w in range(len(ins)):
        half = ins[w].shape[1] // 2
        copies.append(pltpu.make_async_remote_copy(
            src_ref=ins[w].at[:, pl.ds((1 - c) * half, half), :], dst_ref=outs[w], send_sem=send_sems.at[w],
            recv_sem=recv_sems.at[w], device_id=(x, y, 1 - c), device_id_type=MESH))
    return copies


def _swap_shapes(grads):
    return ([jax.ShapeDtypeStruct((g.shape[0], g.shape[1] // 2, g.shape[2]), g.dtype) for g in grads],
            [pltpu.SemaphoreType.DMA((len(grads),)), pltpu.SemaphoreType.DMA((len(grads),))])


def _gather2_copies(ins, outs, send_sems, recv_sems, loc_sems, fwd_send, fwd_recv):
    x, y, c = _place()
    mine = 2 * x + y
    copies = []
    for w in range(len(ins)):
        half = ins[w].shape[0] // 2
        rows = pl.ds(c * half, half)
        copies.append(pltpu.make_async_copy(ins[w], outs[w].at[mine], loc_sems.at[w]))
        for k, chip in enumerate(_other_chips(x, y)):
            copies.append(pltpu.make_async_remote_copy(
                src_ref=ins[w].at[rows, :], dst_ref=outs[w].at[mine, rows, :], send_sem=send_sems.at[3 * w + k],
                recv_sem=recv_sems.at[3 * w + k], device_id=(*chip, c), device_id_type=MESH))
    return copies


def _gather2_forward(ins, outs, send_sems, recv_sems, loc_sems, fwd_send, fwd_recv):
    x, y, c = _place()
    copies = []
    for w in range(len(ins)):
        half = ins[w].shape[0] // 2
        rows = pl.ds(c * half, half)
        for k, chip in enumerate(_other_chips(x, y)):
            slab = outs[w].at[2 * chip[0] + chip[1], rows, :]
            copies.append(pltpu.make_async_remote_copy(
                src_ref=slab, dst_ref=slab, send_sem=fwd_send.at[3 * w + k], recv_sem=fwd_recv.at[3 * w + k],
                device_id=(x, y, 1 - c), device_id_type=MESH))
    return copies


def _gather2_shapes(shards):
    n = len(shards)
    return ([jax.ShapeDtypeStruct((N_CHIPS,) + s.shape, s.dtype) for s in shards],
            [pltpu.SemaphoreType.DMA((3 * n,)), pltpu.SemaphoreType.DMA((3 * n,)), pltpu.SemaphoreType.DMA((n,)),
             pltpu.SemaphoreType.DMA((3 * n,)), pltpu.SemaphoreType.DMA((3 * n,))])


COMM = {"gather": (_gather_copies, _gather_shapes, None), "exchange": (_exchange_copies, _exchange_shapes, None),
        "swap": (_swap_copies, _swap_shapes, None), "gather2": (_gather2_copies, _gather2_shapes, _gather2_forward)}


def _call(body, comm, *, grid, in_specs, out_specs, out_shape, scratch_shapes, args, name):
    ni, no, ns = len(in_specs), len(out_specs), len(scratch_shapes)
    arrays = list(comm[1]) if comm else []
    nc = len(arrays)
    first_fn, shapes_fn, second_fn = COMM[comm[0]] if comm else (None, None, None)
    c_shapes, c_sems = shapes_fn(arrays) if comm else ([], [])

    def kern(*refs):
        ins, cin, outs = refs[:ni], refs[ni:ni + nc], refs[ni + nc:ni + nc + no]
        cout, scr, sems = refs[ni + nc + no:ni + 2 * nc + no], refs[ni + 2 * nc + no:ni + 2 * nc + no + ns], refs[ni + 2 * nc + no + ns:]
        ids = [pl.program_id(d) for d in range(len(grid))]
        if nc:
            @pl.when(functools.reduce(lambda a, b: a & b, [i == 0 for i in ids]))
            def _():
                for cp in first_fn(cin, cout, *sems):
                    cp.start()

        body(ins, outs, scr)
        if nc:
            @pl.when(functools.reduce(lambda a, b: a & b, [i == g - 1 for i, g in zip(ids, grid)]))
            def _():
                for cp in first_fn(cin, cout, *sems):
                    cp.wait()
                if second_fn is not None:
                    more = second_fn(cin, cout, *sems)
                    for cp in more:
                        cp.start()
                    for cp in more:
                        cp.wait()

    out = pl.pallas_call(
        kern, grid=grid, in_specs=list(in_specs) + [ANY] * nc, out_specs=list(out_specs) + [ANY] * nc,
        out_shape=list(out_shape) + c_shapes, scratch_shapes=list(scratch_shapes) + c_sems,
        compiler_params=_cp(("arbitrary",) * len(grid)), name=name,
    )(*args, *arrays)
    return list(out[:no]), list(out[no:])


def _resident(shape):
    return pl.BlockSpec(shape, lambda *_: (0,) * len(shape), pipeline_mode=pl.Buffered(1))


def _norm1(x, g1, comm):
    t = x.shape[0]
    tm = min(1024, t)

    def body(ins, outs, _):
        xv = ins[0][...]
        outs[0][...] = (xv * _rstd(xv) * ins[1][...]).astype(BF16)

    row = pl.BlockSpec((tm, D_MODEL), lambda i: (i, 0))
    (xn,), got = _call(body, comm, grid=(t // tm,), in_specs=[row, pl.BlockSpec((1, D_MODEL), lambda i: (0, 0))],
                       out_specs=[row], out_shape=[jax.ShapeDtypeStruct((t, D_MODEL), BF16)], scratch_shapes=[],
                       args=(x, g1), name="norm1")
    return xn, got


def _inproj(xn, w_in, comm=None):
    t = xn.shape[0]
    tm = min(512, t)

    def body(ins, outs, _):
        xn_v = ins[0][...]
        outs[0][...] = _dot(xn_v, ins[1][:, : 2 * LRU_WIDTH])
        outs[1][...] = _dot(xn_v, ins[1][:, 2 * LRU_WIDTH:]).astype(BF16)

    row = lambda c: pl.BlockSpec((tm, c), lambda i: (i, 0))
    (xl, qkv), got = _call(
        body, comm, grid=(t // tm,), in_specs=[row(D_MODEL), _resident((D_MODEL, IN_COLS))],
        out_specs=[row(2 * LRU_WIDTH), row(3 * SB_WIDTH)],
        out_shape=[jax.ShapeDtypeStruct((t, 2 * LRU_WIDTH), F32), jax.ShapeDtypeStruct((t, 3 * SB_WIDTH), BF16)],
        scratch_shapes=[], args=(xn, w_in), name="inproj")
    return xl, qkv, got


def _conv_taps(hist, u):
    cat = jnp.concatenate([hist, u], axis=0)
    return [pltpu.roll(cat, CONV_WIDTH - 1 - k, 0)[SUBLANES:] for k in range(CONV_WIDTH - 1)] + [u]


def _lru_gates(c, wbd_ref, ba, bx, sp):
    gas, gxs = [], []
    for p in range(LRU_WIDTH // LANES):
        gax = _dot(c[:, LANES * p: LANES * (p + 1)].astype(BF16), wbd_ref[p])
        gas.append(gax[:, :LANES])
        gxs.append(gax[:, LANES:])
    r = _sigmoid(jnp.concatenate(gas, axis=1) + ba)
    i = _sigmoid(jnp.concatenate(gxs, axis=1) + bx)
    la = (-LRU_C) * r * sp
    a = jnp.exp(la)
    e2 = _neg_expm1(2.0 * la, a * a)
    inv_mult = lax.rsqrt(jnp.maximum(e2, 1e-30))
    return r, i, a, e2 * inv_mult, inv_mult


def _scan_fwd(a, b):
    s = 1
    while s < a.shape[0]:
        b = b + a * _shift_down(b, s, 0.0)
        a = a * _shift_down(a, s, 1.0)
        s *= 2
    return a, b


def _scan_rev(a, b):
    s = 1
    while s < a.shape[0]:
        b = b + a * _shift_up(b, s, 0.0)
        a = a * _shift_up(a, s, 1.0)
        s *= 2
    return a, b


def _lru_param_specs(grid_rank):
    z2 = (lambda e, c: (0, 0)) if grid_rank == 2 else None
    return [
        pl.BlockSpec((CONV_WIDTH, LRU_WIDTH), z2), pl.BlockSpec((1, LRU_WIDTH), z2),
        pl.BlockSpec((LRU_WIDTH // LANES, LANES, 2 * LANES), lambda e, c: (0, 0, 0)),
        pl.BlockSpec((1, LRU_WIDTH), z2), pl.BlockSpec((1, LRU_WIDTH), z2), pl.BlockSpec((1, LRU_WIDTH), z2),
    ]


def _lru_fwd(xl, conv_w, conv_b, wbd, ba, bx, lam, seq):
    t = xl.shape[0]
    tc = min(512, seq)
    nc = seq // tc

    def kern(u_ref, g_ref, cw_ref, cb_ref, wbd_ref, ba_ref, bx_ref, lam_ref, h_ref, y_ref, hist_ref, hcar_ref):
        @pl.when(pl.program_id(1) == 0)
        def _():
            hist_ref[...] = jnp.zeros_like(hist_ref)
            hcar_ref[...] = jnp.zeros_like(hcar_ref)

        u = u_ref[...]
        taps = _conv_taps(hist_ref[...], u)
        hist_ref[...] = u_ref[tc - SUBLANES:, :]
        c = cb_ref[...]
        for k in range(CONV_WIDTH):
            c = c + taps[k] * cw_ref[k:k + 1, :]
        sp = _softplus(-lam_ref[...])
        _, i, a, mult, _ = _lru_gates(c, wbd_ref, ba_ref[...], bx_ref[...], sp)
        aa, bb = _scan_fwd(a, mult * i * c)
        h = bb + aa * hcar_ref[0:1, :]
        h_ref[...] = h
        hcar_ref[0:1, :] = h_ref[tc - 1:tc, :]
        y_ref[...] = h * _gelu(g_ref[...])[0]

    chunk = lambda col: pl.BlockSpec((tc, LRU_WIDTH), lambda e, c: (e * nc + c, col))
    out = jax.ShapeDtypeStruct((t, LRU_WIDTH), F32)
    return pl.pallas_call(
        kern, grid=(t // seq, nc), in_specs=[chunk(0), chunk(1)] + _lru_param_specs(2),
        out_specs=[chunk(0), chunk(0)], out_shape=[out, out],
        scratch_shapes=[pltpu.VMEM((SUBLANES, LRU_WIDTH), F32), pltpu.VMEM((SUBLANES, LRU_WIDTH), F32)],
        compiler_params=_cp(("arbitrary", "arbitrary")), name="lru_fwd",
    )(xl, xl, conv_w, conv_b, wbd, ba, bx, lam)


def _att_consts():
    row = lax.broadcasted_iota(jnp.int32, (TQ, 2 * TK), 0)
    key = lax.broadcasted_iota(jnp.int32, (TQ, 2 * TK), 1) & (TK - 1)
    return key < row


def _sum_matrix(kind):
    j = lax.broadcasted_iota(jnp.int32, (2 * TK, 2 * TK), 0) & (TK - 1)
    s = lax.broadcasted_iota(jnp.int32, (2 * TK, 2 * TK), 1)
    pick = {"after": j > s, "upto": j <= s, "before": j < s}[kind]
    return jnp.where((s >= TK) | pick, 1.0, 0.0).astype(BF16)


def _hi_lo(x):
    hi = x.astype(BF16)
    return hi, (x - hi.astype(F32)).astype(BF16)


def _pair_sums(x, m):
    hi, lo = _hi_lo(x)
    out = []
    for hd in range(2):
        cols = slice(hd * TK, (hd + 1) * TK)
        out.append(_dot(jnp.concatenate([hi[:, cols], lo[:, cols]], axis=1), m))
    return [o[:, :TK] for o in out], [o[:, TK:] for o in out]


def _att_logits(qb, kbd):
    z = _dot(qb, kbd)
    lg = jnp.log(1.0 + jnp.exp(-jnp.abs(z)))
    lb = jnp.minimum(z, 0.0) - lg
    return lb, lb - z


def _head_diag(x, rows_first):
    n = x.shape[0] if rows_first else x.shape[1]
    idx = lax.broadcasted_iota(jnp.int32, x.shape, 0 if rows_first else 1)
    return jnp.where(idx < n // 2, x, 0), jnp.where(idx >= n // 2, x, 0)


def _band_tiles():
    nd = TQ // TK
    return [(jr, TK * max(jr, 0), TK * min(jr + BAND + 1, nd)) for jr in range(nd - 1, -BAND - 1, -1)]


def _rows_update(st, new, lo, hi):
    def one(x, y):
        pieces = ([x[:lo]] if lo else []) + [y] + ([x[hi:]] if hi < x.shape[0] else [])
        return pieces[0] if len(pieces) == 1 else jnp.concatenate(pieces, axis=0)
    return tuple(one(x, y) for x, y in zip(st, new))


def _scaled_q(q_ref, q0):
    return (q_ref[pl.ds(q0, TQ), :].astype(F32) * ATT_SCALE).astype(BF16)


def _qkv_specs(seq):
    n = SB_WIDTH // LANES
    return [pl.BlockSpec((seq, LANES), lambda e, p, off=off: (e, off * n + p)) for off in range(3)]


def _attn_fwd(qkv, seq, comm=None):
    t = qkv.shape[0]
    ne, nq, nk = t // seq, seq // TQ, seq // TK

    def body(ins, outs, scr):
        (q_ref, k_ref, v_ref), (o_ref, tot_ref, kmin_ref), (kbd_scr, vbd_scr) = ins, outs, scr
        causal = _att_consts()
        after = _sum_matrix("after")

        def prep(j, _):
            k0 = pl.multiple_of(j * TK, TK)
            top, bot = _head_diag(k_ref[pl.ds(k0, TK), :].astype(F32).T, True)
            kbd_scr[j] = jnp.concatenate([top, bot], axis=1).astype(BF16)
            left, right = _head_diag(v_ref[pl.ds(k0, TK), :], False)
            vbd_scr[j] = jnp.concatenate([left, right], axis=0)
            return 0

        lax.fori_loop(0, nk, prep, 0)

        def block(j, qb, st, mask):
            c0, c1, oacc = st
            lb, l1 = _att_logits(qb, kbd_scr[j])
            if mask is not None:
                l1 = jnp.where(mask, l1, 0.0)
            (s0, s1), (r0, r1) = _pair_sums(l1, after)
            att = jnp.exp(lb + jnp.concatenate([s0 + c0, s1 + c1], axis=1))
            if mask is not None:
                att = jnp.where(mask, att, 0.0)
            return c0 + r0, c1 + r1, oacc + _dot(att.astype(BF16), vbd_scr[j])

        def general(qi, qb, st):
            for jj in reversed(range(TQ // TK)):
                lo = TK * jj
                new = block((TQ // TK) * qi + jj, qb[lo:], tuple(x[lo:] for x in st), causal[:TQ - lo])
                st = _rows_update(st, new, lo, TQ)

            npair = (TQ // TK // 2) * qi

            def more(its):
                return (its[0] < npair) & (jnp.max(jnp.maximum(its[1], its[2])) > SKIP_LOG)

            def kloop(its):
                j = 2 * (npair - its[0]) - 1
                return (its[0] + 1,) + block(j - 1, qb, block(j, qb, its[1:], None), None)

            done, c0, c1, oacc = lax.while_loop(more, kloop, (jnp.int32(0),) + st)
            return c0, c1, oacc, 2 * (npair - done)

        def short(qi, qb, st):
            for jr, lo, hi in _band_tiles():
                new = block((TQ // TK) * qi + jr, qb[lo:hi], tuple(x[lo:hi] for x in st),
                            causal[:hi - lo] if jr >= 0 else None)
                st = _rows_update(st, new, lo, hi)
            return st

        def qloop(qi, _):
            q0 = pl.multiple_of(qi * TQ, TQ)
            qb = _scaled_q(q_ref, q0)
            zero = jnp.zeros((TQ, TK), F32)
            st = (zero, zero, jnp.zeros((TQ, LANES), F32))

            def try_short():
                c0, c1, oacc = short(qi, qb, st)
                return lax.cond(jnp.max(jnp.maximum(c0, c1)) <= SKIP_LOG, lambda: (c0, c1, oacc, jnp.int32(-1)),
                                lambda: general(qi, qb, st))

            c0, c1, oacc, first = lax.cond(qi > 0, try_short, lambda: general(qi, qb, st))
            o_ref[pl.ds(q0, TQ), :] = oacc
            tot_ref[pl.ds(q0, TQ), :] = jnp.concatenate([c0, c1], axis=1)
            kmin_ref[pl.program_id(0), pl.program_id(1), qi] = first
            return 0

        lax.fori_loop(0, nq, qloop, 0)

    (o, tot, kmin), got = _call(
        body, comm, grid=(ne, SB_WIDTH // LANES), in_specs=_qkv_specs(seq),
        out_specs=[pl.BlockSpec((seq, LANES), lambda e, p: (e, p)), pl.BlockSpec((seq, 2 * TK), lambda e, p: (e, p)),
                   pl.BlockSpec(memory_space=pltpu.SMEM)],
        out_shape=[jax.ShapeDtypeStruct((t, SB_WIDTH), F32), jax.ShapeDtypeStruct((t, 2 * TK * SB_WIDTH // LANES), F32),
                   jax.ShapeDtypeStruct((ne, SB_WIDTH // LANES, nq), jnp.int32)],
        scratch_shapes=[pltpu.VMEM((nk, LANES, 2 * TK), BF16), pltpu.VMEM((nk, 2 * TK, LANES), BF16)],
        args=(qkv, qkv, qkv), name="attn_fwd")
    return o, tot, kmin, got


def _outproj(y_lru, o, x, ga, gb, w_out):
    t = x.shape[0]
    tm = min(512, t)

    def kern(y_ref, o_ref, x_ref, ga_ref, gb_ref, w_ref, h1_ref, mix_ref):
        yv, ov = y_ref[...], o_ref[...]
        mix = jnp.concatenate([yv * _rstd(yv) * ga_ref[...], ov * _rstd(ov) * gb_ref[...]], axis=1).astype(BF16)
        mix_ref[...] = mix
        h1_ref[...] = x_ref[...] + _dot(mix, w_ref[...])

    row = lambda c: pl.BlockSpec((tm, c), lambda i: (i, 0))
    vec = lambda c: pl.BlockSpec((1, c), lambda i: (0, 0))
    return pl.pallas_call(
        kern, grid=(t // tm,),
        in_specs=[row(LRU_WIDTH), row(SB_WIDTH), row(D_MODEL), vec(LRU_WIDTH), vec(SB_WIDTH),
                  pl.BlockSpec((D_MODEL, D_MODEL), lambda i: (0, 0))],
        out_specs=[row(D_MODEL), row(D_MODEL)],
        out_shape=[jax.ShapeDtypeStruct((t, D_MODEL), F32), jax.ShapeDtypeStruct((t, D_MODEL), BF16)],
        compiler_params=_cp(("parallel",)), name="outproj",
    )(y_lru, o, x, ga, gb, w_out)


def _mlp_loss(h1, g2, w_up, w_down, target, gf):
    t = h1.shape[0]
    tm, tf = min(512, t), 1024

    def kern(h1_ref, g_ref, wu_ref, wd_ref, t_ref, gf_ref, hn_ref, up_ref, u2_ref, dh_ref, dhb_ref, loss_ref, dg_ref):
        @pl.when(pl.program_id(0) == 0)
        def _():
            loss_ref[...] = jnp.zeros_like(loss_ref)
            dg_ref[...] = jnp.zeros_like(dg_ref)

        hv = h1_ref[...]
        hn = (hv * _rstd(hv) * g_ref[...]).astype(BF16)
        hn_ref[...] = hn
        h2 = hv
        for f in range(D_FF // tf):
            cols = slice(f * tf, (f + 1) * tf)
            up = jnp.maximum(_dot(hn, wu_ref[:, cols]), 0.0)
            u2 = (up * up).astype(BF16)
            up_ref[:, cols] = up.astype(BF16)
            u2_ref[:, cols] = u2
            h2 = h2 + _dot(u2, wd_ref[cols, :])

        g = gf_ref[...]
        err = h2 * _rstd(h2) * g - t_ref[...]
        lane = lax.broadcasted_iota(jnp.int32, (1, LANES), 1)
        loss_ref[...] += jnp.where(lane == 0, 0.5 * jnp.sum(err * err) / D_MODEL, 0.0)
        dx, dg = _rms_bwd(h2, g, err * (1.0 / D_MODEL))
        dh_ref[...] = dx
        dhb_ref[...] = dx.astype(BF16)
        dg_ref[...] += dg

    row = lambda c: pl.BlockSpec((tm, c), lambda i: (i, 0))
    vec = pl.BlockSpec((1, D_MODEL), lambda i: (0, 0))
    return pl.pallas_call(
        kern, grid=(t // tm,),
        in_specs=[row(D_MODEL), vec, _resident((D_MODEL, D_FF)), _resident((D_FF, D_MODEL)), row(D_MODEL), vec],
        out_specs=[row(D_MODEL), row(D_FF), row(D_FF), row(D_MODEL), row(D_MODEL), pl.BlockSpec((1, LANES), lambda i: (0, 0)), vec],
        out_shape=[jax.ShapeDtypeStruct((t, D_MODEL), BF16), jax.ShapeDtypeStruct((t, D_FF), BF16),
                   jax.ShapeDtypeStruct((t, D_FF), BF16), jax.ShapeDtypeStruct((t, D_MODEL), F32),
                   jax.ShapeDtypeStruct((t, D_MODEL), BF16), jax.ShapeDtypeStruct((1, LANES), F32),
                   jax.ShapeDtypeStruct((1, D_MODEL), F32)],
        compiler_params=_cp(("arbitrary",), VMEM_LIMIT_BIG), name="mlp_loss",
    )(h1, g2, w_up, w_down, target, gf)


def _mlp_bwd_pre(dh2, w_down, up):
    t = dh2.shape[0]
    tm, tf = min(512, t), 1024

    def kern(d_ref, w_ref, up_ref, o_ref):
        dv = d_ref[...]
        for f in range(D_FF // tf):
            cols = slice(f * tf, (f + 1) * tf)
            o_ref[:, cols] = (_dot_nt(dv, w_ref[cols, :]) * (2.0 * up_ref[:, cols].astype(F32))).astype(BF16)

    row = lambda c: pl.BlockSpec((tm, c), lambda i: (i, 0))
    return pl.pallas_call(
        kern, grid=(t // tm,), in_specs=[row(D_MODEL), _resident((D_FF, D_MODEL)), row(D_FF)],
        out_specs=row(D_FF), out_shape=jax.ShapeDtypeStruct((t, D_FF), BF16),
        compiler_params=_cp(("parallel",)), name="mlp_bwd_pre",
    )(dh2, w_down, up)


def _proj_bwd_norm(dys, w, x, g, resid, name, comm=None):
    t = x.shape[0]
    tm = min(512, t)
    widths = [dy.shape[1] for dy in dys]
    n = len(dys)

    def body(ins, outs, _):
        dy_refs, (w_ref, x_ref, g_ref, r_ref), (dx_ref, dg_ref) = ins[:n], ins[n:], outs

        @pl.when(pl.program_id(0) == 0)
        def _():
            dg_ref[...] = jnp.zeros_like(dg_ref)

        off, dxn = 0, None
        for dy_ref, wd in zip(dy_refs, widths):
            part = _dot_nt(dy_ref[...], w_ref[:, off:off + wd])
            dxn = part if dxn is None else dxn + part
            off += wd
        dx, dg = _rms_bwd(x_ref[...], g_ref[...], dxn)
        dx_ref[...] = r_ref[...] + dx
        dg_ref[...] += dg

    row = lambda c: pl.BlockSpec((tm, c), lambda i: (i, 0))
    vec = pl.BlockSpec((1, D_MODEL), lambda i: (0, 0))
    (dx, dg), got = _call(
        body, comm, grid=(t // tm,), in_specs=[row(wd) for wd in widths] + [_resident(w.shape), row(D_MODEL), vec, row(D_MODEL)],
        out_specs=[row(D_MODEL), vec],
        out_shape=[jax.ShapeDtypeStruct((t, D_MODEL), F32), jax.ShapeDtypeStruct((1, D_MODEL), F32)],
        scratch_shapes=[], args=(*dys, w, x, g, resid), name=name)
    return dx, dg, got


def _outproj_bwd(dh1, w_out, y_lru, o, ga, gb, comm=None):
    t = dh1.shape[0]
    tm = min(512, t)

    def body(ins, outs, _):
        (d_ref, w_ref, y_ref, o_ref, ga_ref, gb_ref), (dy_ref, do_ref, dga_ref, dgb_ref) = ins, outs

        @pl.when(pl.program_id(0) == 0)
        def _():
            dga_ref[...] = jnp.zeros_like(dga_ref)
            dgb_ref[...] = jnp.zeros_like(dgb_ref)

        dmix = _dot_nt(d_ref[...].astype(BF16), w_ref[...])
        dy, dga = _rms_bwd(y_ref[...], ga_ref[...], dmix[:, :LRU_WIDTH])
        do, dgb = _rms_bwd(o_ref[...], gb_ref[...], dmix[:, LRU_WIDTH:])
        dy_ref[...] = dy
        do_ref[...] = do
        dga_ref[...] += dga
        dgb_ref[...] += dgb

    row = lambda c: pl.BlockSpec((tm, c), lambda i: (i, 0))
    vec = pl.BlockSpec((1, LRU_WIDTH), lambda i: (0, 0))
    half = jax.ShapeDtypeStruct((t, LRU_WIDTH), F32)
    gsum = jax.ShapeDtypeStruct((1, LRU_WIDTH), F32)
    outs, got = _call(body, comm, grid=(t // tm,),
                      in_specs=[row(D_MODEL), _resident((D_MODEL, D_MODEL)), row(LRU_WIDTH), row(SB_WIDTH), vec, vec],
                      out_specs=[row(LRU_WIDTH), row(SB_WIDTH), vec, vec], out_shape=[half, half, gsum, gsum],
                      scratch_shapes=[], args=(dh1, w_out, y_lru, o, ga, gb), name="outproj_bwd")
    return (*outs, got)


def _attn_bwd(qkv, do, tot, kmin, seq):
    t = qkv.shape[0]
    ne, nq, nk = t // seq, seq // TQ, seq // TK

    def kern(q_ref, k_ref, v_ref, do_ref, tot_ref, kmin_ref, dq_ref, dk_ref, dv_ref,
             kbd_scr, vtbd_scr, kbd2_scr, dkt_scr, dvt_scr):
        causal = _att_consts()
        upto, before = _sum_matrix("upto"), _sum_matrix("before")

        def prep(j, _):
            k0 = pl.multiple_of(j * TK, TK)
            kb = k_ref[pl.ds(k0, TK), :]
            top, bot = _head_diag(kb.astype(F32).T, True)
            kbd_scr[j] = jnp.concatenate([top, bot], axis=1).astype(BF16)
            top, bot = _head_diag(v_ref[pl.ds(k0, TK), :].astype(F32).T, True)
            vtbd_scr[j] = jnp.concatenate([top, bot], axis=1).astype(BF16)
            left, right = _head_diag(kb, False)
            kbd2_scr[j] = jnp.concatenate([left, right], axis=0)
            dkt_scr[j] = jnp.zeros((LANES, 2 * TK), F32)
            dvt_scr[j] = jnp.zeros((LANES, 2 * TK), F32)
            return 0

        lax.fori_loop(0, nk, prep, 0)

        def block(j, qb, qt, dob, dot_, totb, st, mask):
            f0, f1, p0, p1, dqacc = st
            lb, l1 = _att_logits(qb, kbd_scr[j])
            if mask is not None:
                l1 = jnp.where(mask, l1, 0.0)
            (s0, s1), (r0, r1) = _pair_sums(l1, upto)
            att = jnp.exp(lb + (totb - jnp.concatenate([s0 + f0, s1 + f1], axis=1)))
            if mask is not None:
                att = jnp.where(mask, att, 0.0)
            pw = att * _dot(dob, vtbd_scr[j])
            (e0, e1), (t0, t1) = _pair_sums(pw, before)
            dz = pw - jnp.exp(lb) * (pw + jnp.concatenate([e0 + p0, e1 + p1], axis=1))
            if mask is not None:
                dz = jnp.where(mask, dz, 0.0)
            dzb = dz.astype(BF16)
            dkt_scr[j] += _dot(qt, dzb)
            dvt_scr[j] += _dot(dot_, att.astype(BF16))
            return f0 + r0, f1 + r1, p0 + t0, p1 + t1, dqacc + _dot(dzb, kbd2_scr[j])

        def qloop(qi, _):
            q0 = pl.multiple_of(qi * TQ, TQ)
            qb = _scaled_q(q_ref, q0)
            qt = qb.astype(F32).T.astype(BF16)
            do32 = do_ref[pl.ds(q0, TQ), :]
            dob, dot_ = do32.astype(BF16), do32.T.astype(BF16)
            totb = tot_ref[pl.ds(q0, TQ), :]
            zero = jnp.zeros((TQ, TK), F32)
            st = (zero, zero, zero, zero, jnp.zeros((TQ, LANES), F32))

            k0 = kmin_ref[pl.program_id(0), pl.program_id(1), qi]

            def tile(j, lo, hi, st, masked):
                new = block(j, qb[lo:hi], qt[:, lo:hi], dob[lo:hi], dot_[:, lo:hi], totb[lo:hi],
                            tuple(x[lo:hi] for x in st), causal[:hi - lo] if masked else None)
                return _rows_update(st, new, lo, hi)

            def general():
                def kloop(it, st):
                    j = k0 + 2 * it
                    return block(j + 1, qb, qt, dob, dot_, totb, block(j, qb, qt, dob, dot_, totb, st, None), None)

                out = lax.fori_loop(0, ((TQ // TK) * qi - k0) // 2, kloop, st)
                for jj in range(TQ // TK):
                    out = tile((TQ // TK) * qi + jj, TK * jj, TQ, out, True)
                return out

            def short():
                out = st
                for jr, lo, hi in reversed(_band_tiles()):
                    out = tile((TQ // TK) * qi + jr, lo, hi, out, jr >= 0)
                return out

            st = lax.cond(k0 < 0, short, general)
            dq_ref[pl.ds(q0, TQ), :] = (st[4] * ATT_SCALE).astype(BF16)
            return 0

        lax.fori_loop(0, nq, qloop, 0)

        def finish(j, _):
            k0 = pl.multiple_of(j * TK, TK)
            head0 = lax.broadcasted_iota(jnp.int32, (LANES, TK), 0) < DH
            for src, dst in ((dkt_scr, dk_ref), (dvt_scr, dv_ref)):
                acc = src[j]
                dst[pl.ds(k0, TK), :] = jnp.where(head0, acc[:, :TK], acc[:, TK:]).T.astype(BF16)
            return 0

        lax.fori_loop(0, nk, finish, 0)

    blk = pl.BlockSpec((seq, LANES), lambda e, p: (e, p))
    grad = jax.ShapeDtypeStruct((t, SB_WIDTH), BF16)
    return pl.pallas_call(
        kern, grid=(ne, SB_WIDTH // LANES),
        in_specs=_qkv_specs(seq) + [blk, pl.BlockSpec((seq, 2 * TK), lambda e, p: (e, p)),
                                    pl.BlockSpec(memory_space=pltpu.SMEM)],
        out_specs=[blk, blk, blk], out_shape=[grad, grad, grad],
        scratch_shapes=[pltpu.VMEM((nk, LANES, 2 * TK), BF16), pltpu.VMEM((nk, LANES, 2 * TK), BF16),
                        pltpu.VMEM((nk, 2 * TK, LANES), BF16), pltpu.VMEM((nk, LANES, 2 * TK), F32),
                        pltpu.VMEM((nk, LANES, 2 * TK), F32)],
        compiler_params=_cp(("parallel", "parallel")), name="attn_bwd",
    )(qkv, qkv, qkv, do, tot, kmin)


def _lru_bwd(xl, h, dy, conv_w, conv_b, wbd, ba, bx, lam, seq, comm=None):
    t = xl.shape[0]
    tc = min(512, seq)
    nc = seq // tc
    nb = tc // SUBLANES


    def body(ins, outs, scr):
        u_ref, g_ref, up_ref, h_ref, hp_ref, dy_ref, cw_ref, cb_ref, wbd_ref, ba_ref, bx_ref, lam_ref = ins
        (dxl_ref, small_ref, dwbd_ref), (lnext_ref, anext_ref, dcnext_ref) = outs, scr
        e, ci = pl.program_id(0), pl.program_id(1)
        first = ci == nc - 1

        @pl.when((e == 0) & (ci == 0))
        def _():
            small_ref[...] = jnp.zeros_like(small_ref)
            dwbd_ref[...] = jnp.zeros_like(dwbd_ref)

        @pl.when(ci == 0)
        def _():
            lnext_ref[...] = jnp.zeros_like(lnext_ref)
            anext_ref[...] = jnp.zeros_like(anext_ref)
            dcnext_ref[...] = jnp.zeros_like(dcnext_ref)

        u, g = u_ref[...], g_ref[...]
        keep = jnp.where(first, 0.0, 1.0)
        taps = _conv_taps(keep * up_ref[...], u)
        c = cb_ref[...]
        for k in range(CONV_WIDTH):
            c = c + taps[k] * cw_ref[k:k + 1, :]
        lam = lam_ref[...]
        sp = _softplus(-lam)
        r, i, a, mult, inv_mult = _lru_gates(c, wbd_ref, ba_ref[...], bx_ref[...], sp)
        gel, th = _gelu(g)
        dyv, hv = dy_ref[...], h_ref[...]
        dg = dyv * hv * _gelu_grad(g, th)

        aa, bb = _scan_rev(_shift_up(a, 1, anext_ref[0:1, :]), dyv * gel)
        lt = bb + aa * lnext_ref[0:1, :]
        lnext_ref[0:1, :] = _row_of(lt, 0)
        anext_ref[0:1, :] = _row_of(a, 0)

        hprev = _shift_down(hv, 1, keep * hp_ref[SUBLANES - 1:SUBLANES, :])
        da = lt * hprev
        dmult = lt * i * c
        di = lt * mult * c
        dc = lt * mult * i
        dla = da * a - dmult * (a * a) * inv_mult
        dga = dla * ((-LRU_C) * sp) * r * (1.0 - r)
        dgx = di * i * (1.0 - i)
        small_ref[7:8, :] += jnp.sum(dla * r, axis=0, keepdims=True) * (LRU_C * _sigmoid(-lam))
        small_ref[5:6, :] += jnp.sum(dga, axis=0, keepdims=True)
        small_ref[6:7, :] += jnp.sum(dgx, axis=0, keepdims=True)

        dcs = []
        for p in range(LRU_WIDTH // LANES):
            cols = slice(LANES * p, LANES * (p + 1))
            dgax = jnp.concatenate([dga[:, cols], dgx[:, cols]], axis=1).astype(BF16)
            dcs.append(_dot_nt(dgax, wbd_ref[p]))
            dwbd_ref[p] += _dot_tn(c[:, cols].astype(BF16), dgax)
        dc = dc + jnp.concatenate(dcs, axis=1)
        small_ref[4:5, :] += jnp.sum(dc, axis=0, keepdims=True)

        catd = jnp.concatenate([dc, dcnext_ref[...]], axis=0)
        du = dc * cw_ref[CONV_WIDTH - 1:CONV_WIDTH, :]
        for j in range(1, CONV_WIDTH):
            du = du + pltpu.roll(catd, tc + SUBLANES - j, 0)[:tc] * cw_ref[CONV_WIDTH - 1 - j:CONV_WIDTH - j, :]
        dcnext_ref[...] = dc[:SUBLANES]
        for k in range(CONV_WIDTH):
            small_ref[k:k + 1, :] += jnp.sum(dc * taps[k], axis=0, keepdims=True)
        dxl_ref[:, :LRU_WIDTH] = du.astype(BF16)
        dxl_ref[:, LRU_WIDTH:] = dg.astype(BF16)

    rev = lambda e, c: e * nc + (nc - 1 - c)
    chunk = lambda col: pl.BlockSpec((tc, LRU_WIDTH), lambda e, c: (rev(e, c), col))
    prev8 = pl.BlockSpec((SUBLANES, LRU_WIDTH), lambda e, c: (jnp.maximum(rev(e, c) * nb - 1, 0), 0))
    outs, got = _call(
        body, comm, grid=(t // seq, nc),
        in_specs=[chunk(0), chunk(1), prev8, chunk(0), prev8, chunk(0)] + _lru_param_specs(2),
        out_specs=[pl.BlockSpec((tc, 2 * LRU_WIDTH), lambda e, c: (rev(e, c), 0)),
                   pl.BlockSpec((SUBLANES, LRU_WIDTH), lambda e, c: (0, 0)),
                   pl.BlockSpec((LRU_WIDTH // LANES, LANES, 2 * LANES), lambda e, c: (0, 0, 0))],
        out_shape=[jax.ShapeDtypeStruct((t, 2 * LRU_WIDTH), BF16), jax.ShapeDtypeStruct((SUBLANES, LRU_WIDTH), F32),
                   jax.ShapeDtypeStruct((LRU_WIDTH // LANES, LANES, 2 * LANES), F32)],
        scratch_shapes=[pltpu.VMEM((SUBLANES, LRU_WIDTH), F32)] * 3,
        args=(xl, xl, xl, h, h, dy, conv_w, conv_b, wbd, ba, bx, lam), name="lru_bwd")
    return (*outs, got)


def _adam_math(w, g, m, v):
    m2 = ADAM_B1 * m + (1.0 - ADAM_B1) * g
    v2 = ADAM_B2 * v + (1.0 - ADAM_B2) * (g * g)
    m_hat = m2 / (1.0 - ADAM_B1 ** ADAM_STEP)
    v_hat = v2 / (1.0 - ADAM_B2 ** ADAM_STEP)
    return -ADAM_LR * (m_hat / (jnp.sqrt(v_hat) + ADAM_EPS) + ADAM_WD * w), m2, v2


def _adamw(w, g, m, v, name):
    rows, cols = w.shape
    tr = 256 if rows % 256 == 0 else rows

    def kern(w_ref, g_ref, m_ref, v_ref, d_ref, m2_ref, v2_ref):
        d_ref[...], m2_ref[...], v2_ref[...] = _adam_math(w_ref[...], g_ref[...], m_ref[...], v_ref[...])

    blk = pl.BlockSpec((tr, cols), lambda i: (i, 0))
    out = jax.ShapeDtypeStruct((rows, cols), F32)
    return pl.pallas_call(kern, grid=(rows // tr,), in_specs=[blk] * 4, out_specs=[blk] * 3, out_shape=[out] * 3,
                          compiler_params=_cp(("parallel",)), name=name)(w, g, m, v)


def _adamw_small(ws, gs, ms, vs):
    n = len(ws)

    def kern(*refs):
        for k in range(n):
            outs = _adam_math(refs[k][...], refs[n + k][...], refs[2 * n + k][...], refs[3 * n + k][...])
            for j in range(3):
                refs[(4 + j) * n + k][...] = outs[j]

    vm = pl.BlockSpec(memory_space=pltpu.VMEM)
    out = pl.pallas_call(kern, in_specs=[vm] * (4 * n), out_specs=[vm] * (3 * n),
                         out_shape=[jax.ShapeDtypeStruct(w.shape, F32) for w in ws] * 3, name="adamw_small")(*ws, *gs, *ms, *vs)
    return out[:n], out[n:2 * n], out[2 * n:]


def _comm_only(kind, arrays, name):
    return _call(lambda ins, outs, scr: None, (kind, arrays), grid=(1,), in_specs=[], out_specs=[], out_shape=[],
                 scratch_shapes=[], args=(), name=name)[1]


def _pair_add(g, got, core):
    _, rows, cols = g.shape
    half = rows // 2
    tr = min(256, half)
    nt = half // tr

    def kern(c_ref, g_ref, o_ref, out_ref):
        out_ref[...] = (g_ref[...] + o_ref[...]).astype(BF16)

    return pl.pallas_call(
        kern, grid_spec=pltpu.PrefetchScalarGridSpec(
            num_scalar_prefetch=1, grid=(N_CHIPS, nt),
            in_specs=[pl.BlockSpec((None, tr, cols), lambda j, i, c_ref: (j, c_ref[0] * nt + i, 0)),
                      pl.BlockSpec((None, tr, cols), lambda j, i, c_ref: (j, i, 0))],
            out_specs=pl.BlockSpec((None, tr, cols), lambda j, i, c_ref: (j, i, 0))),
        out_shape=jax.ShapeDtypeStruct((N_CHIPS, half, cols), BF16),
        compiler_params=_cp(("parallel", "parallel")), name="pair_add",
    )(core, g, got)


def _chip_add(part, got, place):
    _, half, cols = part.shape
    tr = min(256, half)
    nt = half // tr

    def kern(p_ref, part_ref, got_ref, out_ref):
        out_ref[...] = (part_ref[...].astype(F32) + got_ref[0].astype(F32) + got_ref[1].astype(F32)
                        + got_ref[2].astype(F32))

    return pl.pallas_call(
        kern, grid_spec=pltpu.PrefetchScalarGridSpec(
            num_scalar_prefetch=1, grid=(nt,),
            in_specs=[pl.BlockSpec((None, tr, cols), lambda i, p_ref: (p_ref[0], i, 0)),
                      pl.BlockSpec((3, tr, cols), lambda i, p_ref: (0, i, 0))],
            out_specs=pl.BlockSpec((tr, cols), lambda i, p_ref: (p_ref[1] * nt + i, 0))),
        out_shape=jax.ShapeDtypeStruct((2 * half, cols), F32),
        compiler_params=_cp(("parallel",)), name="chip_add",
    )(place, part, got)


def _finale(packed, fulls):
    rows, n = packed.shape[0], len(fulls)

    def kern(in_ref, *refs):
        ins, out_ref, outs = refs[:n], refs[n], refs[n + 1:2 * n + 1]
        slots, send_sems, recv_sems, join_send, join_recv = refs[2 * n + 1:]
        x, y, c = _place()
        mine = 4 * x + 2 * y + c
        copies = []
        for w in range(n):
            half = ins[w].shape[0] // 2
            rws = pl.ds(c * half, half)
            copies.append(pltpu.make_async_remote_copy(
                src_ref=ins[w].at[rws, :], dst_ref=outs[w].at[rws, :], send_sem=join_send.at[w],
                recv_sem=join_recv.at[w], device_id=(x, y, 1 - c), device_id_type=MESH))
        for k in range(1, N_DEV):
            peer = (x ^ (k >> 2), y ^ ((k >> 1) & 1), c ^ (k & 1))
            copies.append(pltpu.make_async_remote_copy(
                src_ref=in_ref, dst_ref=slots.at[mine], send_sem=send_sems.at[k - 1], recv_sem=recv_sems.at[k - 1],
                device_id=peer, device_id_type=MESH))
        for cp in copies:
            cp.start()
        slots[mine] = in_ref[...]
        for cp in copies:
            cp.wait()
        acc = slots[0]
        for sl in range(1, N_DEV):
            acc = acc + slots[sl]
        out_ref[...] = acc

    vm = pl.BlockSpec(memory_space=pltpu.VMEM)
    out = pl.pallas_call(
        kern, in_specs=[vm] + [ANY] * n, out_specs=[vm] + [ANY] * n,
        out_shape=[jax.ShapeDtypeStruct((rows, LANES), F32)] + [jax.ShapeDtypeStruct(f.shape, f.dtype) for f in fulls],
        input_output_aliases={w + 1: w + 1 for w in range(n)},
        scratch_shapes=[pltpu.VMEM((N_DEV, rows, LANES), F32), pltpu.SemaphoreType.DMA((N_DEV - 1,)),
                        pltpu.SemaphoreType.DMA((N_DEV - 1,)), pltpu.SemaphoreType.DMA((n,)), pltpu.SemaphoreType.DMA((n,))],
        name="finale",
    )(packed, *fulls)
    return out[0], list(out[1:])


SMALL = ["norm1_g", "conv_w", "conv_b", "lru_w_a", "lru_b_a", "lru_w_x", "lru_b_x", "lru_lambda", "lru_out_g", "sb_out_g",
         "norm2_g", "final_g"]
BIG = ["w_in", "w_out", "w_up", "w_down"]
WEIGHTS = ["norm1_g", "w_in", "conv_w", "conv_b", "lru_w_a", "lru_b_a", "lru_w_x", "lru_b_x", "lru_lambda", "lru_out_g",
           "sb_out_g", "w_out", "norm2_g", "w_up", "w_down", "final_g"]


def _pack(arrays):
    flat = []
    for a in arrays:
        a = a.reshape(-1).astype(F32)
        flat.append(jnp.pad(a, (0, (-a.shape[0]) % LANES)))
    v = jnp.concatenate(flat)
    v = jnp.pad(v, (0, (-v.shape[0]) % (LANES * SUBLANES)))
    return v.reshape(-1, LANES)


def _unpack(packed, shapes):
    v, out, off = packed.reshape(-1), [], 0
    for shp in shapes:
        size = math.prod(shp)
        out.append(v[off:off + size].reshape(shp))
        off += size + (-size) % LANES
    return out


def _blockdiag_pairs(w):
    w = w.reshape(4, 2, DH, DH)
    z = jnp.zeros((4, DH, DH), w.dtype)
    return jnp.concatenate([jnp.concatenate([w[:, 0], z], axis=2), jnp.concatenate([z, w[:, 1]], axis=2)], axis=1)


def _blockdiag_unpairs(wbd):
    return jnp.stack([wbd[:, :DH, :DH], wbd[:, DH:, DH:]], axis=1).reshape(8, DH, DH)


def _full_cols(g):
    return jnp.transpose(g, (1, 0, 2)).reshape(g.shape[1], N_CHIPS * g.shape[2])


def _local_step(x2, tgt, seq, norm1_g, w_in, conv_w, conv_b, w_a, b_a, w_x, b_x, lru_lambda, lru_out_g, sb_out_g, rest,
                norm2_g, final_g, place=None):
    alone = place is None
    wbd = jnp.concatenate([_blockdiag_pairs(w_a), _blockdiag_pairs(w_x)], axis=2).astype(BF16)
    ba, bx = b_a.reshape(1, LRU_WIDTH), b_x.reshape(1, LRU_WIDTH)
    gf = final_g.reshape(1, D_MODEL)

    xn, got = _norm1(x2, norm1_g, None if alone else ("gather2", [w_in]))
    w_in_f = w_in if alone else _full_cols(got[0])
    xl, qkv, got = _inproj(xn, w_in_f, None if alone else ("gather", [conv_w]))
    conv_w_f = conv_w if alone else _full_cols(got[0])
    h, y_lru = _lru_fwd(xl, conv_w_f, conv_b, wbd, ba, bx, lru_lambda, seq)
    o, tot, kmin, got = _attn_fwd(qkv, seq, None if alone else ("gather2", rest))
    w_out_f, w_up_f, w_down_f = rest if alone else (
        got[0].reshape(D_MODEL, D_MODEL), _full_cols(got[1]), got[2].reshape(D_FF, D_MODEL))
    h1, mix = _outproj(y_lru, o, x2, lru_out_g, sb_out_g, w_out_f)
    hn, up, u2, dh2, dh2b, loss_part, d_final = _mlp_loss(h1, norm2_g, w_up_f, w_down_f, tgt, gf)

    dpre = _mlp_bwd_pre(dh2b, w_down_f, up)
    g_w_down = _matmul(u2, dh2b, "tn", F32, "dw_down", 1024, 1024, 1024).reshape(N_CHIPS, D_FF // N_CHIPS, D_MODEL)
    g_w_up = _matmul(hn, dpre, "tn", F32, "dw_up", 1024, D_FF // N_CHIPS, 1024, split_cols=True)
    dh1, d_norm2, _ = _proj_bwd_norm([dpre], w_up_f, h1, norm2_g, dh2, "mlp_bwd_in")
    g_w_out = _matmul(mix, dh1, "tn", F32, "dw_out", 1024, 1024, 2048).reshape(N_CHIPS, D_MODEL // N_CHIPS, D_MODEL)
    late = [g_w_out, g_w_up, g_w_down]
    dy_lru, do, d_ga, d_gb, swapped = _outproj_bwd(dh1, w_out_f, y_lru, o, lru_out_g, sb_out_g,
                                                   None if alone else ("swap", late))
    parts = None if alone else [_pair_add(g, r, place[1:]) for g, r in zip(late, swapped)]
    dq, dk, dv = _attn_bwd(qkv, do, tot, kmin, seq)
    dxl, lru_small, d_wbd, got = _lru_bwd(xl, h, dy_lru, conv_w_f, conv_b, wbd, ba, bx, lru_lambda, seq,
                                          None if alone else ("exchange", parts))
    if not alone:
        late = [_chip_add(p, r, place) for p, r in zip(parts, got)]
    dproj = jnp.concatenate([dxl, dq, dk, dv], axis=1)
    g_w_in = _matmul(xn, dproj, "tn", F32, "dw_in", 1024, IN_COLS // N_CHIPS, 2048, split_cols=True)
    part = None if alone else _pair_add(g_w_in, _comm_only("swap", [g_w_in], "pair_swap")[0], place[1:])
    dx, d_norm1, got = _proj_bwd_norm([dxl, dq, dk, dv], w_in_f, x2, norm1_g, dh1, "inproj_bwd",
                                      None if alone else ("exchange", [part]))
    if not alone:
        g_w_in = _chip_add(part, got[0], place)
    small_parts = {
        "norm1_g": d_norm1, "conv_w": lru_small[:CONV_WIDTH], "conv_b": lru_small[4:5],
        "lru_w_a": _blockdiag_unpairs(d_wbd[:, :, :LANES]), "lru_b_a": lru_small[5:6],
        "lru_w_x": _blockdiag_unpairs(d_wbd[:, :, LANES:]), "lru_b_x": lru_small[6:7], "lru_lambda": lru_small[7:8],
        "lru_out_g": d_ga, "sb_out_g": d_gb, "norm2_g": d_norm2, "final_g": d_final,
    }
    return loss_part, dx, [g_w_in] + late, small_parts


def kernel(x, norm1_g, w_in, conv_w, conv_b, lru_w_a, lru_b_a, lru_w_x, lru_b_x, lru_lambda, lru_out_g, sb_out_g, w_out, norm2_g, w_up, w_down, final_g, loss_target, m_norm1_g, m_w_in, m_conv_w, m_conv_b, m_lru_w_a, m_lru_b_a, m_lru_w_x, m_lru_b_x, m_lru_lambda, m_lru_out_g, m_sb_out_g, m_w_out, m_norm2_g, m_w_up, m_w_down, m_final_g, v_norm1_g, v_w_in, v_conv_w, v_conv_b, v_lru_w_a, v_lru_b_a, v_lru_w_x, v_lru_b_x, v_lru_lambda, v_lru_out_g, v_sb_out_g, v_w_out, v_norm2_g, v_w_up, v_w_down, v_final_g):
    given = dict(locals())
    ne, seq, _ = x.shape
    t = ne * seq
    xi, yi, ci = _place()
    place = jnp.stack([2 * xi + yi, ci]).astype(jnp.int32)

    loss_part, dx, halves, small_parts = _local_step(
        x.reshape(t, D_MODEL), loss_target.reshape(t, D_MODEL), seq, norm1_g, w_in[0].astype(BF16), conv_w[0], conv_b,
        lru_w_a[0], lru_b_a, lru_w_x[0], lru_b_x, lru_lambda, lru_out_g, sb_out_g,
        [w_out[0].astype(BF16), w_up[0].astype(BF16), w_down[0].astype(BF16)], norm2_g, final_g, place)

    full_shapes = {n: ((CONV_WIDTH, LRU_WIDTH) if n == "conv_w" else given[n].shape) for n in SMALL}
    red, fulls = _finale(_pack([small_parts[n] for n in SMALL] + [loss_part]), halves)
    red_list = _unpack(red, [full_shapes[n] for n in SMALL] + [(1, LANES)])
    grads = dict(zip(SMALL, red_list[:-1]))
    loss = red_list[-1][0, 0]
    grads["conv_w"] = lax.dynamic_slice_in_dim(grads["conv_w"], place[0] * (LRU_WIDTH // N_CHIPS), LRU_WIDTH // N_CHIPS,
                                               axis=1).reshape(conv_w.shape)
    for n, full in zip(BIG, fulls):
        grads[n] = full.reshape(given[n].shape)

    delta, new_m, new_v = {}, {}, {}
    for n in BIG:
        shp = given[n].shape
        d, m2, v2 = _adamw(given[n][0], grads[n][0], given["m_" + n][0], given["v_" + n][0], "adamw_" + n)
        delta[n], new_m[n], new_v[n] = d.reshape(shp), m2.reshape(shp), v2.reshape(shp)
    as2d = lambda a: a.reshape(-1, a.shape[-1])
    ds, m2s, v2s = _adamw_small([as2d(given[n]) for n in SMALL], [as2d(grads[n]) for n in SMALL],
                                [as2d(given["m_" + n]) for n in SMALL], [as2d(given["v_" + n]) for n in SMALL])
    for n, dd, mm, vv in zip(SMALL, ds, m2s, v2s):
        shp = given[n].shape
        delta[n], new_m[n], new_v[n] = dd.reshape(shp), mm.reshape(shp), vv.reshape(shp)

    return (loss, dx.reshape(x.shape), *[grads[n] for n in WEIGHTS], *[delta[n] for n in WEIGHTS],
            *[new_m[n] for n in WEIGHTS], *[new_v[n] for n in WEIGHTS])
```

```python
import functools
import math

import jax
import jax.numpy as jnp
from jax import lax
from jax.experimental import pallas as pl
from jax.experimental.pallas import tpu as pltpu

F32, BF16 = jnp.float32, jnp.bfloat16
MESH = pl.DeviceIdType.MESH

D_MODEL = 1024
LRU_WIDTH = 512
SB_WIDTH = 512
DH = 64
IN_COLS = 2 * LRU_WIDTH + 3 * SB_WIDTH
D_FF = 4 * D_MODEL
CONV_WIDTH = 4
LRU_C = 8.0
EPS = 1e-6
N_CHIPS = 4
N_DEV = 8
LANES = 128
SUBLANES = 8
TQ = 512
TK = 128
ATT_SCALE = 1.0 / math.sqrt(DH)
SKIP_LOG = -105.0
BAND = 2
VMEM_LIMIT = 52 * 1024 * 1024
VMEM_LIMIT_BIG = 62 * 1024 * 1024

ADAM_LR, ADAM_B1, ADAM_B2, ADAM_EPS, ADAM_WD, ADAM_STEP = 0.001, 0.9, 0.999, 1e-08, 0.01, 10

_GELU_K = math.sqrt(2.0 / math.pi)
_GELU_C = 0.044715


def _cp(sem, vmem=VMEM_LIMIT):
    return pltpu.CompilerParams(dimension_semantics=sem, vmem_limit_bytes=vmem)


def _dot(a, b):
    return jnp.dot(a, b, preferred_element_type=F32)


def _dot_nt(a, b):
    return lax.dot_general(a, b, (((1,), (1,)), ((), ())), preferred_element_type=F32)


def _dot_tn(a, b):
    return lax.dot_general(a, b, (((0,), (0,)), ((), ())), preferred_element_type=F32)


def _rstd(x):
    return lax.rsqrt(jnp.mean(x * x, axis=-1, keepdims=True) + EPS)


def _rms_bwd(x, g, dy):
    r = _rstd(x)
    gd = g * dy
    dx = r * gd - x * (r * r * r) * jnp.mean(x * gd, axis=-1, keepdims=True)
    return dx, jnp.sum(dy * x * r, axis=0, keepdims=True)


def _sigmoid(x):
    return 0.5 * jnp.tanh(0.5 * x) + 0.5


def _softplus(x):
    return jnp.maximum(x, 0.0) + jnp.log(1.0 + jnp.exp(-jnp.abs(x)))


def _neg_expm1(x, ex):
    series = -x * (1.0 + x * (0.5 + x * (1.0 / 6.0)))
    return jnp.where(x > -2.0 ** -7, series, 1.0 - ex)


def _gelu(g):
    t = jnp.tanh(_GELU_K * (g + _GELU_C * g * g * g))
    return 0.5 * g * (1.0 + t), t


def _gelu_grad(g, t):
    return 0.5 * (1.0 + t) + 0.5 * g * (1.0 - t * t) * _GELU_K * (1.0 + 3.0 * _GELU_C * g * g)


def _rows(shape):
    return lax.broadcasted_iota(jnp.int32, shape, 0)


def _shift_down(x, s, fill):
    n, c = x.shape
    if s % SUBLANES == 0:
        return jnp.concatenate([jnp.broadcast_to(jnp.asarray(fill, x.dtype), (s, c)), x[:n - s]], axis=0)
    return jnp.where(_rows(x.shape) >= s, pltpu.roll(x, s, 0), fill)


def _shift_up(x, s, fill):
    n, c = x.shape
    if s % SUBLANES == 0:
        return jnp.concatenate([x[s:], jnp.broadcast_to(jnp.asarray(fill, x.dtype), (s, c))], axis=0)
    return jnp.where(_rows(x.shape) < n - s, pltpu.roll(x, n - s, 0), fill)


def _row_of(x, idx):
    return jnp.sum(jnp.where(_rows(x.shape) == idx, x, 0.0), axis=0, keepdims=True)


def _matmul(a, b, dims, out_dtype, name, tm, tn, tk, split_cols=False):
    if dims == "nn":
        (m, kk), n, dot = a.shape, b.shape[1], _dot
    elif dims == "nt":
        (m, kk), n, dot = a.shape, b.shape[0], _dot_nt
    else:
        (kk, m), n, dot = a.shape, b.shape[1], _dot_tn
    tm, tn, tk = min(tm, m), min(tn, n), min(tk, kk)
    assert m % tm == 0 and n % tn == 0 and kk % tk == 0, (name, m, n, kk)
    nk = kk // tk

    def kern(a_ref, b_ref, o_ref, acc_ref):
        k = pl.program_id(2)

        @pl.when(k == 0)
        def _():
            acc_ref[...] = jnp.zeros_like(acc_ref)

        acc_ref[...] += dot(a_ref[...].astype(BF16), b_ref[...].astype(BF16))

        @pl.when(k == nk - 1)
        def _():
            o_ref[...] = acc_ref[...].astype(o_ref.dtype)

    if split_cols:
        out_shape = jax.ShapeDtypeStruct((n // tn, m, tn), out_dtype)
        o_spec = pl.BlockSpec((None, tm, tn), lambda i, j, k: (j, i, 0))
    else:
        out_shape = jax.ShapeDtypeStruct((m, n), out_dtype)
        o_spec = pl.BlockSpec((tm, tn), lambda i, j, k: (i, j))
    if dims == "nn":
        a_spec = pl.BlockSpec((tm, tk), lambda i, j, k: (i, k))
        b_spec = pl.BlockSpec((tk, tn), lambda i, j, k: (k, j))
    elif dims == "nt":
        a_spec = pl.BlockSpec((tm, tk), lambda i, j, k: (i, k))
        b_spec = pl.BlockSpec((tn, tk), lambda i, j, k: (j, k))
    else:
        a_spec = pl.BlockSpec((tk, tm), lambda i, j, k: (k, i))
        b_spec = pl.BlockSpec((tk, tn), lambda i, j, k: (k, j))
    return pl.pallas_call(
        kern, grid=(m // tm, n // tn, nk), in_specs=[a_spec, b_spec], out_specs=o_spec, out_shape=out_shape,
        scratch_shapes=[pltpu.VMEM((tm, tn), F32)], compiler_params=_cp(("parallel", "parallel", "arbitrary")), name=name,
    )(a, b)


ANY = pl.BlockSpec(memory_space=pl.ANY)


def _place():
    return lax.axis_index("x"), lax.axis_index("y"), lax.axis_index("c")


def _other_chips(x, y):
    return [(1 - x, y), (x, 1 - y), (1 - x, 1 - y)]


def _own_slab(shard, gathered, send_sem, recv_sem):
    x, y, c = _place()
    return pltpu.make_async_remote_copy(src_ref=shard, dst_ref=gathered.at[2 * x + y], send_sem=send_sem, recv_sem=recv_sem,
                                        device_id=(x, y, 1 - c), device_id_type=MESH)


def _gather_copies(ins, outs, send_sems, recv_sems, own_send, own_recv):
    x, y, c = _place()
    mine = 2 * x + y
    copies = []
    for w in range(len(ins)):
        copies.append(_own_slab(ins[w], outs[w], own_send.at[w], own_recv.at[w]))
        for k, chip in enumerate(_other_chips(x, y)):
            copies.append(pltpu.make_async_remote_copy(
                src_ref=ins[w], dst_ref=outs[w].at[mine], send_sem=send_sems.at[3 * w + k],
                recv_sem=recv_sems.at[3 * w + k], device_id=(*chip, c), device_id_type=MESH))
    return copies


def _gather_shapes(shards):
    return ([jax.ShapeDtypeStruct((N_CHIPS,) + s.shape, s.dtype) for s in shards],
            [pltpu.SemaphoreType.DMA((3 * len(shards),)), pltpu.SemaphoreType.DMA((3 * len(shards),)),
             pltpu.SemaphoreType.DMA((len(shards),)), pltpu.SemaphoreType.DMA((len(shards),))])


def _exchange_copies(ins, outs, send_sems, recv_sems):
    x, y, c = _place()
    copies = []
    for w in range(len(ins)):
        for k, chip in enumerate(_other_chips(x, y)):
            copies.append(pltpu.make_async_remote_copy(
                src_ref=ins[w].at[2 * chip[0] + chip[1]], dst_ref=outs[w].at[k], send_sem=send_sems.at[3 * w + k],
                recv_sem=recv_sems.at[3 * w + k], device_id=(*chip, c), device_id_type=MESH))
    return copies


def _exchange_shapes(parts):
    return ([jax.ShapeDtypeStruct((3,) + p.shape[1:], p.dtype) for p in parts],
            [pltpu.SemaphoreType.DMA((3 * len(parts),)), pltpu.SemaphoreType.DMA((3 * len(parts),))])


def _swap_copies(ins, outs, send_sems, recv_sems):
    x, y, c = _place()
    copies = []
    for w in range(len(ins)):
        half = ins[w].shape[1] // 2
        copies.append(pltpu.make_async_remote_copy(
            src_ref=ins[w].at[:, pl.ds((1 - c) * half, half), :], dst_ref=outs[w], send_sem=send_sems.at[w],
            recv_sem=recv_sems.at[w], device_id=(x, y, 1 - c), device_id_type=MESH))
    return copies


def _swap_shapes(grads):
    return ([jax.ShapeDtypeStruct((g.shape[0], g.shape[1] // 2, g.shape[2]), g.dtype) for g in grads],
            [pltpu.SemaphoreType.DMA((len(grads),)), pltpu.SemaphoreType.DMA((len(grads),))])


def _gather2_copies(ins, outs, send_sems, recv_sems, own_send, own_recv, fwd_send, fwd_recv):
    x, y, c = _place()
    mine = 2 * x + y
    copies = []
    for w in range(len(ins)):
        half = ins[w].shape[0] // 2
        rows = pl.ds(c * half, half)
        copies.append(_own_slab(ins[w], outs[w], own_send.at[w], own_recv.at[w]))
        for k, chip in enumerate(_other_chips(x, y)):
            copies.append(pltpu.make_async_remote_copy(
                src_ref=ins[w].at[rows, :], dst_ref=outs[w].at[mine, rows, :], send_sem=send_sems.at[3 * w + k],
                recv_sem=recv_sems.at[3 * w + k], device_id=(*chip, c), device_id_type=MESH))
    return copies


def _gather2_forward(ins, outs, send_sems, recv_sems, own_send, own_recv, fwd_send, fwd_recv):
    x, y, c = _place()
    copies = []
    for w in range(len(ins)):
        half = ins[w].shape[0] // 2
        rows = pl.ds(c * half, half)
        for k, chip in enumerate(_other_chips(x, y)):
            slab = outs[w].at[2 * chip[0] + chip[1], rows, :]
            copies.append(pltpu.make_async_remote_copy(
                src_ref=slab, dst_ref=slab, send_sem=fwd_send.at[3 * w + k], recv_sem=fwd_recv.at[3 * w + k],
                device_id=(x, y, 1 - c), device_id_type=MESH))
    return copies


def _gather2_shapes(shards):
    n = len(shards)
    return ([jax.ShapeDtypeStruct((N_CHIPS,) + s.shape, s.dtype) for s in shards],
            [pltpu.SemaphoreType.DMA((3 * n,)), pltpu.SemaphoreType.DMA((3 * n,)), pltpu.SemaphoreType.DMA((n,)),
             pltpu.SemaphoreType.DMA((n,)), pltpu.SemaphoreType.DMA((3 * n,)), pltpu.SemaphoreType.DMA((3 * n,))])


COMM = {"gather": (_gather_copies, _gather_shapes, None), "exchange": (_exchange_copies, _exchange_shapes, None),
        "swap": (_swap_copies, _swap_shapes, None), "gather2": (_gather2_copies, _gather2_shapes, _gather2_forward)}


def _call(body, comm, *, grid, in_specs, out_specs, out_shape, scratch_shapes, args, name):
    ni, no, ns = len(in_specs), len(out_specs), len(scratch_shapes)
    arrays = list(comm[1]) if comm else []
    nc = len(arrays)
    first_fn, shapes_fn, second_fn = COMM[comm[0]] if comm else (None, None, None)
    c_shapes, c_sems = shapes_fn(arrays) if comm else ([], [])

    def kern(*refs):
        ins, cin, outs = refs[:ni], refs[ni:ni + nc], refs[ni + nc:ni + nc + no]
        cout, scr, sems = refs[ni + nc + no:ni + 2 * nc + no], refs[ni + 2 * nc + no:ni + 2 * nc + no + ns], refs[ni + 2 * nc + no + ns:]
        ids = [pl.program_id(d) for d in range(len(grid))]
        if nc:
            @pl.when(functools.reduce(lambda a, b: a & b, [i == 0 for i in ids]))
            def _():
                for cp in first_fn(cin, cout, *sems):
                    cp.start()

        body(ins, outs, scr)
        if nc:
            @pl.when(functools.reduce(lambda a, b: a & b, [i == g - 1 for i, g in zip(ids, grid)]))
            def _():
                for cp in first_fn(cin, cout, *sems):
                    cp.wait()
                if second_fn is not None:
                    more = second_fn(cin, cout, *sems)
                    for cp in more:
                        cp.start()
                    for cp in more:
                        cp.wait()

    out = pl.pallas_call(
        kern, grid=grid, in_specs=list(in_specs) + [ANY] * nc, out_specs=list(out_specs) + [ANY] * nc,
        out_shape=list(out_shape) + c_shapes, scratch_shapes=list(scratch_shapes) + c_sems,
        compiler_params=_cp(("arbitrary",) * len(grid)), name=name,
    )(*args, *arrays)
    return list(out[:no]), list(out[no:])


def _resident(shape):
    return pl.BlockSpec(shape, lambda *_: (0,) * len(shape), pipeline_mode=pl.Buffered(1))


def _norm1(x, g1, comm):
    t = x.shape[0]
    tm = min(1024, t)

    def body(ins, outs, _):
        xv = ins[0][...]
        outs[0][...] = (xv * _rstd(xv) * ins[1][...]).astype(BF16)

    row = pl.BlockSpec((tm, D_MODEL), lambda i: (i, 0))
    (xn,), got = _call(body, comm, grid=(t // tm,), in_specs=[row, pl.BlockSpec((1, D_MODEL), lambda i: (0, 0))],
                       out_specs=[row], out_shape=[jax.ShapeDtypeStruct((t, D_MODEL), BF16)], scratch_shapes=[],
                       args=(x, g1), name="norm1")
    return xn, got


def _inproj(xn, w_in, comm=None):
    t = xn.shape[0]
    tm = min(512, t)

    def body(ins, outs, _):
        xn_v = ins[0][...]
        outs[0][...] = _dot(xn_v, ins[1][:, : 2 * LRU_WIDTH])
        outs[1][...] = _dot(xn_v, ins[1][:, 2 * LRU_WIDTH:]).astype(BF16)

    row = lambda c: pl.BlockSpec((tm, c), lambda i: (i, 0))
    (xl, qkv), got = _call(
        body, comm, grid=(t // tm,), in_specs=[row(D_MODEL), _resident((D_MODEL, IN_COLS))],
        out_specs=[row(2 * LRU_WIDTH), row(3 * SB_WIDTH)],
        out_shape=[jax.ShapeDtypeStruct((t, 2 * LRU_WIDTH), F32), jax.ShapeDtypeStruct((t, 3 * SB_WIDTH), BF16)],
        scratch_shapes=[], args=(xn, w_in), name="inproj")
    return xl, qkv, got


def _conv_taps(hist, u):
    cat = jnp.concatenate([hist, u], axis=0)
    return [pltpu.roll(cat, CONV_WIDTH - 1 - k, 0)[SUBLANES:] for k in range(CONV_WIDTH - 1)] + [u]


def _lru_gates(c, wbd_ref, ba, bx, sp):
    gas, gxs = [], []
    for p in range(LRU_WIDTH // LANES):
        gax = _dot(c[:, LANES * p: LANES * (p + 1)].astype(BF16), wbd_ref[p])
        gas.append(gax[:, :LANES])
        gxs.append(gax[:, LANES:])
    r = _sigmoid(jnp.concatenate(gas, axis=1) + ba)
    i = _sigmoid(jnp.concatenate(gxs, axis=1) + bx)
    la = (-LRU_C) * r * sp
    a = jnp.exp(la)
    e2 = _neg_expm1(2.0 * la, a * a)
    inv_mult = lax.rsqrt(jnp.maximum(e2, 1e-30))
    return r, i, a, e2 * inv_mult, inv_mult


def _scan_fwd(a, b):
    s = 1
    while s < a.shape[0]:
        b = b + a * _shift_down(b, s, 0.0)
        a = a * _shift_down(a, s, 1.0)
        s *= 2
    return a, b


def _scan_rev(a, b):
    s = 1
    while s < a.shape[0]:
        b = b + a * _shift_up(b, s, 0.0)
        a = a * _shift_up(a, s, 1.0)
        s *= 2
    return a, b


def _lru_param_specs(grid_rank):
    z2 = (lambda e, c: (0, 0)) if grid_rank == 2 else None
    return [
        pl.BlockSpec((CONV_WIDTH, LRU_WIDTH), z2), pl.BlockSpec((1, LRU_WIDTH), z2),
        pl.BlockSpec((LRU_WIDTH // LANES, LANES, 2 * LANES), lambda e, c: (0, 0, 0)),
        pl.BlockSpec((1, LRU_WIDTH), z2), pl.BlockSpec((1, LRU_WIDTH), z2), pl.BlockSpec((1, LRU_WIDTH), z2),
    ]


def _lru_fwd(xl, conv_w, conv_b, wbd, ba, bx, lam, seq):
    t = xl.shape[0]
    tc = min(512, seq)
    nc = seq // tc

    def kern(u_ref, g_ref, cw_ref, cb_ref, wbd_ref, ba_ref, bx_ref, lam_ref, h_ref, y_ref, hist_ref, hcar_ref):
        @pl.when(pl.program_id(1) == 0)
        def _():
            hist_ref[...] = jnp.zeros_like(hist_ref)
            hcar_ref[...] = jnp.zeros_like(hcar_ref)

        u = u_ref[...]
        taps = _conv_taps(hist_ref[...], u)
        hist_ref[...] = u_ref[tc - SUBLANES:, :]
        c = cb_ref[...]
        for k in range(CONV_WIDTH):
            c = c + taps[k] * cw_ref[k:k + 1, :]
        sp = _softplus(-lam_ref[...])
        _, i, a, mult, _ = _lru_gates(c, wbd_ref, ba_ref[...], bx_ref[...], sp)
        aa, bb = _scan_fwd(a, mult * i * c)
        h = bb + aa * hcar_ref[0:1, :]
        h_ref[...] = h
        hcar_ref[0:1, :] = h_ref[tc - 1:tc, :]
        y_ref[...] = h * _gelu(g_ref[...])[0]

    chunk = lambda col: pl.BlockSpec((tc, LRU_WIDTH), lambda e, c: (e * nc + c, col))
    out = jax.ShapeDtypeStruct((t, LRU_WIDTH), F32)
    return pl.pallas_call(
        kern, grid=(t // seq, nc), in_specs=[chunk(0), chunk(1)] + _lru_param_specs(2),
        out_specs=[chunk(0), chunk(0)], out_shape=[out, out],
        scratch_shapes=[pltpu.VMEM((SUBLANES, LRU_WIDTH), F32), pltpu.VMEM((SUBLANES, LRU_WIDTH), F32)],
        compiler_params=_cp(("arbitrary", "arbitrary")), name="lru_fwd",
    )(xl, xl, conv_w, conv_b, wbd, ba, bx, lam)


def _att_consts():
    row = lax.broadcasted_iota(jnp.int32, (TQ, 2 * TK), 0)
    key = lax.broadcasted_iota(jnp.int32, (TQ, 2 * TK), 1) & (TK - 1)
    return key < row


def _sum_matrix(kind):
    j = lax.broadcasted_iota(jnp.int32, (2 * TK, 2 * TK), 0) & (TK - 1)
    s = lax.broadcasted_iota(jnp.int32, (2 * TK, 2 * TK), 1)
    pick = {"after": j > s, "upto": j <= s, "before": j < s}[kind]
    return jnp.where((s >= TK) | pick, 1.0, 0.0).astype(BF16)


def _hi_lo(x):
    hi = x.astype(BF16)
    return hi, (x - hi.astype(F32)).astype(BF16)


def _pair_sums(x, m):
    hi, lo = _hi_lo(x)
    out = []
    for hd in range(2):
        cols = slice(hd * TK, (hd + 1) * TK)
        out.append(_dot(jnp.concatenate([hi[:, cols], lo[:, cols]], axis=1), m))
    return [o[:, :TK] for o in out], [o[:, TK:] for o in out]


def _att_logits(qb, kbd):
    z = _dot(qb, kbd)
    lg = jnp.log(1.0 + jnp.exp(-jnp.abs(z)))
    lb = jnp.minimum(z, 0.0) - lg
    return lb, lb - z


def _head_diag(x, rows_first):
    n = x.shape[0] if rows_first else x.shape[1]
    idx = lax.broadcasted_iota(jnp.int32, x.shape, 0 if rows_first else 1)
    return jnp.where(idx < n // 2, x, 0), jnp.where(idx >= n // 2, x, 0)


def _band_tiles():
    nd = TQ // TK
    return [(jr, TK * max(jr, 0), TK * min(jr + BAND + 1, nd)) for jr in range(nd - 1, -BAND - 1, -1)]


def _rows_update(st, new, lo, hi):
    def one(x, y):
        pieces = ([x[:lo]] if lo else []) + [y] + ([x[hi:]] if hi < x.shape[0] else [])
        return pieces[0] if len(pieces) == 1 else jnp.concatenate(pieces, axis=0)
    return tuple(one(x, y) for x, y in zip(st, new))


def _scaled_q(q_ref, q0):
    return (q_ref[pl.ds(q0, TQ), :].astype(F32) * ATT_SCALE).astype(BF16)


def _qkv_specs(seq):
    n = SB_WIDTH // LANES
    return [pl.BlockSpec((seq, LANES), lambda e, p, off=off: (e, off * n + p)) for off in range(3)]


def _attn_fwd(qkv, seq, comm=None):
    t = qkv.shape[0]
    ne, nq, nk = t // seq, seq // TQ, seq // TK

    def body(ins, outs, scr):
        (q_ref, k_ref, v_ref), (o_ref, tot_ref, kmin_ref), (kbd_scr, vbd_scr) = ins, outs, scr
        causal = _att_consts()
        after = _sum_matrix("after")

        def prep(j, _):
            k0 = pl.multiple_of(j * TK, TK)
            top, bot = _head_diag(k_ref[pl.ds(k0, TK), :].astype(F32).T, True)
            kbd_scr[j] = jnp.concatenate([top, bot], axis=1).astype(BF16)
            left, right = _head_diag(v_ref[pl.ds(k0, TK), :], False)
            vbd_scr[j] = jnp.concatenate([left, right], axis=0)
            return 0

        lax.fori_loop(0, nk, prep, 0)

        def block(j, qb, st, mask):
            c0, c1, oacc = st
            lb, l1 = _att_logits(qb, kbd_scr[j])
            if mask is not None:
                l1 = jnp.where(mask, l1, 0.0)
            (s0, s1), (r0, r1) = _pair_sums(l1, after)
            att = jnp.exp(lb + jnp.concatenate([s0 + c0, s1 + c1], axis=1))
            if mask is not None:
                att = jnp.where(mask, att, 0.0)
            return c0 + r0, c1 + r1, oacc + _dot(att.astype(BF16), vbd_scr[j])

        def general(qi, qb, st):
            for jj in reversed(range(TQ // TK)):
                lo = TK * jj
                new = block((TQ // TK) * qi + jj, qb[lo:], tuple(x[lo:] for x in st), causal[:TQ - lo])
                st = _rows_update(st, new, lo, TQ)

            npair = (TQ // TK // 2) * qi

            def more(its):
                return (its[0] < npair) & (jnp.max(jnp.maximum(its[1], its[2])) > SKIP_LOG)

            def kloop(its):
                j = 2 * (npair - its[0]) - 1
                return (its[0] + 1,) + block(j - 1, qb, block(j, qb, its[1:], None), None)

            done, c0, c1, oacc = lax.while_loop(more, kloop, (jnp.int32(0),) + st)
            return c0, c1, oacc, 2 * (npair - done)

        def short(qi, qb, st):
            for jr, lo, hi in _band_tiles():
                new = block((TQ // TK) * qi + jr, qb[lo:hi], tuple(x[lo:hi] for x in st),
                            causal[:hi - lo] if jr >= 0 else None)
                st = _rows_update(st, new, lo, hi)
            return st

        def qloop(qi, _):
            q0 = pl.multiple_of(qi * TQ, TQ)
            qb = _scaled_q(q_ref, q0)
            zero = jnp.zeros((TQ, TK), F32)
            st = (zero, zero, jnp.zeros((TQ, LANES), F32))

            def try_short():
                c0, c1, oacc = short(qi, qb, st)
                return lax.cond(jnp.max(jnp.maximum(c0, c1)) <= SKIP_LOG, lambda: (c0, c1, oacc, jnp.int32(-1)),
                                lambda: general(qi, qb, st))

            c0, c1, oacc, first = lax.cond(qi > 0, try_short, lambda: general(qi, qb, st))
            o_ref[pl.ds(q0, TQ), :] = oacc
            tot_ref[pl.ds(q0, TQ), :] = jnp.concatenate([c0, c1], axis=1)
            kmin_ref[pl.program_id(0), pl.program_id(1), qi] = first
            return 0

        lax.fori_loop(0, nq, qloop, 0)

    (o, tot, kmin), got = _call(
        body, comm, grid=(ne, SB_WIDTH // LANES), in_specs=_qkv_specs(seq),
        out_specs=[pl.BlockSpec((seq, LANES), lambda e, p: (e, p)), pl.BlockSpec((seq, 2 * TK), lambda e, p: (e, p)),
                   pl.BlockSpec(memory_space=pltpu.SMEM)],
        out_shape=[jax.ShapeDtypeStruct((t, SB_WIDTH), F32), jax.ShapeDtypeStruct((t, 2 * TK * SB_WIDTH // LANES), F32),
                   jax.ShapeDtypeStruct((ne, SB_WIDTH // LANES, nq), jnp.int32)],
        scratch_shapes=[pltpu.VMEM((nk, LANES, 2 * TK), BF16), pltpu.VMEM((nk, 2 * TK, LANES), BF16)],
        args=(qkv, qkv, qkv), name="attn_fwd")
    return o, tot, kmin, got


def _outproj(y_lru, o, x, ga, gb, w_out):
    t = x.shape[0]
    tm = min(512, t)

    def kern(y_ref, o_ref, x_ref, ga_ref, gb_ref, w_ref, h1_ref, mix_ref):
        yv, ov = y_ref[...], o_ref[...]
        mix = jnp.concatenate([yv * _rstd(yv) * ga_ref[...], ov * _rstd(ov) * gb_ref[...]], axis=1).astype(BF16)
        mix_ref[...] = mix
        h1_ref[...] = x_ref[...] + _dot(mix, w_ref[...])

    row = lambda c: pl.BlockSpec((tm, c), lambda i: (i, 0))
    vec = lambda c: pl.BlockSpec((1, c), lambda i: (0, 0))
    return pl.pallas_call(
        kern, grid=(t // tm,),
        in_specs=[row(LRU_WIDTH), row(SB_WIDTH), row(D_MODEL), vec(LRU_WIDTH), vec(SB_WIDTH),
                  pl.BlockSpec((D_MODEL, D_MODEL), lambda i: (0, 0))],
        out_specs=[row(D_MODEL), row(D_MODEL)],
        out_shape=[jax.ShapeDtypeStruct((t, D_MODEL), F32), jax.ShapeDtypeStruct((t, D_MODEL), BF16)],
        compiler_params=_cp(("parallel",)), name="outproj",
    )(y_lru, o, x, ga, gb, w_out)


def _mlp_loss(h1, g2, w_up, w_down, target, gf):
    t = h1.shape[0]
    tm, tf = min(512, t), 1024

    def kern(h1_ref, g_ref, wu_ref, wd_ref, t_ref, gf_ref, hn_ref, up_ref, u2_ref, dh_ref, dhb_ref, loss_ref, dg_ref):
        @pl.when(pl.program_id(0) == 0)
        def _():
            loss_ref[...] = jnp.zeros_like(loss_ref)
            dg_ref[...] = jnp.zeros_like(dg_ref)

        hv = h1_ref[...]
        hn = (hv * _rstd(hv) * g_ref[...]).astype(BF16)
        hn_ref[...] = hn
        h2 = hv
        for f in range(D_FF // tf):
            cols = slice(f * tf, (f + 1) * tf)
            up = jnp.maximum(_dot(hn, wu_ref[:, cols]), 0.0)
            u2 = (up * up).astype(BF16)
            up_ref[:, cols] = up.astype(BF16)
            u2_ref[:, cols] = u2
            h2 = h2 + _dot(u2, wd_ref[cols, :])

        g = gf_ref[...]
        err = h2 * _rstd(h2) * g - t_ref[...]
        lane = lax.broadcasted_iota(jnp.int32, (1, LANES), 1)
        loss_ref[...] += jnp.where(lane == 0, 0.5 * jnp.sum(err * err) / D_MODEL, 0.0)
        dx, dg = _rms_bwd(h2, g, err * (1.0 / D_MODEL))
        dh_ref[...] = dx
        dhb_ref[...] = dx.astype(BF16)
        dg_ref[...] += dg

    row = lambda c: pl.BlockSpec((tm, c), lambda i: (i, 0))
    vec = pl.BlockSpec((1, D_MODEL), lambda i: (0, 0))
    return pl.pallas_call(
        kern, grid=(t // tm,),
        in_specs=[row(D_MODEL), vec, _resident((D_MODEL, D_FF)), _resident((D_FF, D_MODEL)), row(D_MODEL), vec],
        out_specs=[row(D_MODEL), row(D_FF), row(D_FF), row(D_MODEL), row(D_MODEL), pl.BlockSpec((1, LANES), lambda i: (0, 0)), vec],
        out_shape=[jax.ShapeDtypeStruct((t, D_MODEL), BF16), jax.ShapeDtypeStruct((t, D_FF), BF16),
                   jax.ShapeDtypeStruct((t, D_FF), BF16), jax.ShapeDtypeStruct((t, D_MODEL), F32),
                   jax.ShapeDtypeStruct((t, D_MODEL), BF16), jax.ShapeDtypeStruct((1, LANES), F32),
                   jax.ShapeDtypeStruct((1, D_MODEL), F32)],
        compiler_params=_cp(("arbitrary",), VMEM_LIMIT_BIG), name="mlp_loss",
    )(h1, g2, w_up, w_down, target, gf)


def _mlp_bwd_pre(dh2, w_down, up):
    t = dh2.shape[0]
    tm, tf = min(512, t), 1024

    def kern(d_ref, w_ref, up_ref, o_ref):
        dv = d_ref[...]
        for f in range(D_FF // tf):
            cols = slice(f * tf, (f + 1) * tf)
            o_ref[:, cols] = (_dot_nt(dv, w_ref[cols, :]) * (2.0 * up_ref[:, cols].astype(F32))).astype(BF16)

    row = lambda c: pl.BlockSpec((tm, c), lambda i: (i, 0))
    return pl.pallas_call(
        kern, grid=(t // tm,), in_specs=[row(D_MODEL), _resident((D_FF, D_MODEL)), row(D_FF)],
        out_specs=row(D_FF), out_shape=jax.ShapeDtypeStruct((t, D_FF), BF16),
        compiler_params=_cp(("parallel",)), name="mlp_bwd_pre",
    )(dh2, w_down, up)


def _proj_bwd_norm(dys, w, x, g, resid, name, comm=None):
    t = x.shape[0]
    tm = min(512, t)
    widths = [dy.shape[1] for dy in dys]
    n = len(dys)

    def body(ins, outs, _):
        dy_refs, (w_ref, x_ref, g_ref, r_ref), (dx_ref, dg_ref) = ins[:n], ins[n:], outs

        @pl.when(pl.program_id(0) == 0)
        def _():
            dg_ref[...] = jnp.zeros_like(dg_ref)

        off, dxn = 0, None
        for dy_ref, wd in zip(dy_refs, widths):
            part = _dot_nt(dy_ref[...], w_ref[:, off:off + wd])
            dxn = part if dxn is None else dxn + part
            off += wd
        dx, dg = _rms_bwd(x_ref[...], g_ref[...], dxn)
        dx_ref[...] = r_ref[...] + dx
        dg_ref[...] += dg

    row = lambda c: pl.BlockSpec((tm, c), lambda i: (i, 0))
    vec = pl.BlockSpec((1, D_MODEL), lambda i: (0, 0))
    (dx, dg), got = _call(
        body, comm, grid=(t // tm,), in_specs=[row(wd) for wd in widths] + [_resident(w.shape), row(D_MODEL), vec, row(D_MODEL)],
        out_specs=[row(D_MODEL), vec],
        out_shape=[jax.ShapeDtypeStruct((t, D_MODEL), F32), jax.ShapeDtypeStruct((1, D_MODEL), F32)],
        scratch_shapes=[], args=(*dys, w, x, g, resid), name=name)
    return dx, dg, got


def _outproj_bwd(dh1, w_out, y_lru, o, ga, gb, comm=None):
    t = dh1.shape[0]
    tm = min(512, t)

    def body(ins, outs, _):
        (d_ref, w_ref, y_ref, o_ref, ga_ref, gb_ref), (dy_ref, do_ref, dga_ref, dgb_ref) = ins, outs

        @pl.when(pl.program_id(0) == 0)
        def _():
            dga_ref[...] = jnp.zeros_like(dga_ref)
            dgb_ref[...] = jnp.zeros_like(dgb_ref)

        dmix = _dot_nt(d_ref[...].astype(BF16), w_ref[...])
        dy, dga = _rms_bwd(y_ref[...], ga_ref[...], dmix[:, :LRU_WIDTH])
        do, dgb = _rms_bwd(o_ref[...], gb_ref[...], dmix[:, LRU_WIDTH:])
        dy_ref[...] = dy
        do_ref[...] = do
        dga_ref[...] += dga
        dgb_ref[...] += dgb

    row = lambda c: pl.BlockSpec((tm, c), lambda i: (i, 0))
    vec = pl.BlockSpec((1, LRU_WIDTH), lambda i: (0, 0))
    half = jax.ShapeDtypeStruct((t, LRU_WIDTH), F32)
    gsum = jax.ShapeDtypeStruct((1, LRU_WIDTH), F32)
    outs, got = _call(body, comm, grid=(t // tm,),
                      in_specs=[row(D_MODEL), _resident((D_MODEL, D_MODEL)), row(LRU_WIDTH), row(SB_WIDTH), vec, vec],
                      out_specs=[row(LRU_WIDTH), row(SB_WIDTH), vec, vec], out_shape=[half, half, gsum, gsum],
                      scratch_shapes=[], args=(dh1, w_out, y_lru, o, ga, gb), name="outproj_bwd")
    return (*outs, got)


def _attn_bwd(qkv, do, tot, kmin, seq):
    t = qkv.shape[0]
    ne, nq, nk = t // seq, seq // TQ, seq // TK

    def kern(q_ref, k_ref, v_ref, do_ref, tot_ref, kmin_ref, dq_ref, dk_ref, dv_ref,
             kbd_scr, vtbd_scr, kbd2_scr, dkt_scr, dvt_scr):
        causal = _att_consts()
        upto, before = _sum_matrix("upto"), _sum_matrix("before")

        def prep(j, _):
            k0 = pl.multiple_of(j * TK, TK)
            kb = k_ref[pl.ds(k0, TK), :]
            top, bot = _head_diag(kb.astype(F32).T, True)
            kbd_scr[j] = jnp.concatenate([top, bot], axis=1).astype(BF16)
            top, bot = _head_diag(v_ref[pl.ds(k0, TK), :].astype(F32).T, True)
            vtbd_scr[j] = jnp.concatenate([top, bot], axis=1).astype(BF16)
            left, right = _head_diag(kb, False)
            kbd2_scr[j] = jnp.concatenate([left, right], axis=0)
            dkt_scr[j] = jnp.zeros((LANES, 2 * TK), F32)
            dvt_scr[j] = jnp.zeros((LANES, 2 * TK), F32)
            return 0

        lax.fori_loop(0, nk, prep, 0)

        def block(j, qb, qt, dob, dot_, totb, st, mask):
            f0, f1, p0, p1, dqacc = st
            lb, l1 = _att_logits(qb, kbd_scr[j])
            if mask is not None:
                l1 = jnp.where(mask, l1, 0.0)
            (s0, s1), (r0, r1) = _pair_sums(l1, upto)
            att = jnp.exp(lb + (totb - jnp.concatenate([s0 + f0, s1 + f1], axis=1)))
            if mask is not None:
                att = jnp.where(mask, att, 0.0)
            pw = att * _dot(dob, vtbd_scr[j])
            (e0, e1), (t0, t1) = _pair_sums(pw, before)
            dz = pw - jnp.exp(lb) * (pw + jnp.concatenate([e0 + p0, e1 + p1], axis=1))
            if mask is not None:
                dz = jnp.where(mask, dz, 0.0)
            dzb = dz.astype(BF16)
            dkt_scr[j] += _dot(qt, dzb)
            dvt_scr[j] += _dot(dot_, att.astype(BF16))
            return f0 + r0, f1 + r1, p0 + t0, p1 + t1, dqacc + _dot(dzb, kbd2_scr[j])

        def qloop(qi, _):
            q0 = pl.multiple_of(qi * TQ, TQ)
            qb = _scaled_q(q_ref, q0)
            qt = qb.astype(F32).T.astype(BF16)
            do32 = do_ref[pl.ds(q0, TQ), :]
            dob, dot_ = do32.astype(BF16), do32.T.astype(BF16)
            totb = tot_ref[pl.ds(q0, TQ), :]
            zero = jnp.zeros((TQ, TK), F32)
            st = (zero, zero, zero, zero, jnp.zeros((TQ, LANES), F32))

            k0 = kmin_ref[pl.program_id(0), pl.program_id(1), qi]

            def tile(j, lo, hi, st, masked):
                new = block(j, qb[lo:hi], qt[:, lo:hi], dob[lo:hi], dot_[:, lo:hi], totb[lo:hi],
                            tuple(x[lo:hi] for x in st), causal[:hi - lo] if masked else None)
                return _rows_update(st, new, lo, hi)

            def general():
                def kloop(it, st):
                    j = k0 + 2 * it
                    return block(j + 1, qb, qt, dob, dot_, totb, block(j, qb, qt, dob, dot_, totb, st, None), None)

                out = lax.fori_loop(0, ((TQ // TK) * qi - k0) // 2, kloop, st)
                for jj in range(TQ // TK):
                    out = tile((TQ // TK) * qi + jj, TK * jj, TQ, out, True)
                return out

            def short():
                out = st
                for jr, lo, hi in reversed(_band_tiles()):
                    out = tile((TQ // TK) * qi + jr, lo, hi, out, jr >= 0)
                return out

            st = lax.cond(k0 < 0, short, general)
            dq_ref[pl.ds(q0, TQ), :] = (st[4] * ATT_SCALE).astype(BF16)
            return 0

        lax.fori_loop(0, nq, qloop, 0)

        def finish(j, _):
            k0 = pl.multiple_of(j * TK, TK)
            head0 = lax.broadcasted_iota(jnp.int32, (LANES, TK), 0) < DH
            for src, dst in ((dkt_scr, dk_ref), (dvt_scr, dv_ref)):
                acc = src[j]
                dst[pl.ds(k0, TK), :] = jnp.where(head0, acc[:, :TK], acc[:, TK:]).T.astype(BF16)
            return 0

        lax.fori_loop(0, nk, finish, 0)

    blk = pl.BlockSpec((seq, LANES), lambda e, p: (e, p))
    grad = jax.ShapeDtypeStruct((t, SB_WIDTH), BF16)
    return pl.pallas_call(
        kern, grid=(ne, SB_WIDTH // LANES),
        in_specs=_qkv_specs(seq) + [blk, pl.BlockSpec((seq, 2 * TK), lambda e, p: (e, p)),
                                    pl.BlockSpec(memory_space=pltpu.SMEM)],
        out_specs=[blk, blk, blk], out_shape=[grad, grad, grad],
        scratch_shapes=[pltpu.VMEM((nk, LANES, 2 * TK), BF16), pltpu.VMEM((nk, LANES, 2 * TK), BF16),
                        pltpu.VMEM((nk, 2 * TK, LANES), BF16), pltpu.VMEM((nk, LANES, 2 * TK), F32),
                        pltpu.VMEM((nk, LANES, 2 * TK), F32)],
        compiler_params=_cp(("parallel", "parallel")), name="attn_bwd",
    )(qkv, qkv, qkv, do, tot, kmin)


def _lru_bwd(xl, h, dy, conv_w, conv_b, wbd, ba, bx, lam, seq, comm=None):
    t = xl.shape[0]
    tc = min(512, seq)
    nc = seq // tc
    nb = tc // SUBLANES


    def body(ins, outs, scr):
        u_ref, g_ref, up_ref, h_ref, hp_ref, dy_ref, cw_ref, cb_ref, wbd_ref, ba_ref, bx_ref, lam_ref = ins
        (dxl_ref, small_ref, dwbd_ref), (lnext_ref, anext_ref, dcnext_ref) = outs, scr
        e, ci = pl.program_id(0), pl.program_id(1)
        first = ci == nc - 1

        @pl.when((e == 0) & (ci == 0))
        def _():
            small_ref[...] = jnp.zeros_like(small_ref)
            dwbd_ref[...] = jnp.zeros_like(dwbd_ref)

        @pl.when(ci == 0)
        def _():
            lnext_ref[...] = jnp.zeros_like(lnext_ref)
            anext_ref[...] = jnp.zeros_like(anext_ref)
            dcnext_ref[...] = jnp.zeros_like(dcnext_ref)

        u, g = u_ref[...], g_ref[...]
        keep = jnp.where(first, 0.0, 1.0)
        taps = _conv_taps(keep * up_ref[...], u)
        c = cb_ref[...]
        for k in range(CONV_WIDTH):
            c = c + taps[k] * cw_ref[k:k + 1, :]
        lam = lam_ref[...]
        sp = _softplus(-lam)
        r, i, a, mult, inv_mult = _lru_gates(c, wbd_ref, ba_ref[...], bx_ref[...], sp)
        gel, th = _gelu(g)
        dyv, hv = dy_ref[...], h_ref[...]
        dg = dyv * hv * _gelu_grad(g, th)

        aa, bb = _scan_rev(_shift_up(a, 1, anext_ref[0:1, :]), dyv * gel)
        lt = bb + aa * lnext_ref[0:1, :]
        lnext_ref[0:1, :] = _row_of(lt, 0)
        anext_ref[0:1, :] = _row_of(a, 0)

        hprev = _shift_down(hv, 1, keep * hp_ref[SUBLANES - 1:SUBLANES, :])
        da = lt * hprev
        dmult = lt * i * c
        di = lt * mult * c
        dc = lt * mult * i
        dla = da * a - dmult * (a * a) * inv_mult
        dga = dla * ((-LRU_C) * sp) * r * (1.0 - r)
        dgx = di * i * (1.0 - i)
        small_ref[7:8, :] += jnp.sum(dla * r, axis=0, keepdims=True) * (LRU_C * _sigmoid(-lam))
        small_ref[5:6, :] += jnp.sum(dga, axis=0, keepdims=True)
        small_ref[6:7, :] += jnp.sum(dgx, axis=0, keepdims=True)

        dcs = []
        for p in range(LRU_WIDTH // LANES):
            cols = slice(LANES * p, LANES * (p + 1))
            dgax = jnp.concatenate([dga[:, cols], dgx[:, cols]], axis=1).astype(BF16)
            dcs.append(_dot_nt(dgax, wbd_ref[p]))
            dwbd_ref[p] += _dot_tn(c[:, cols].astype(BF16), dgax)
        dc = dc + jnp.concatenate(dcs, axis=1)
        small_ref[4:5, :] += jnp.sum(dc, axis=0, keepdims=True)

        catd = jnp.concatenate([dc, dcnext_ref[...]], axis=0)
        du = dc * cw_ref[CONV_WIDTH - 1:CONV_WIDTH, :]
        for j in range(1, CONV_WIDTH):
            du = du + pltpu.roll(catd, tc + SUBLANES - j, 0)[:tc] * cw_ref[CONV_WIDTH - 1 - j:CONV_WIDTH - j, :]
        dcnext_ref[...] = dc[:SUBLANES]
        for k in range(CONV_WIDTH):
            small_ref[k:k + 1, :] += jnp.sum(dc * taps[k], axis=0, keepdims=True)
        dxl_ref[:, :LRU_WIDTH] = du.astype(BF16)
        dxl_ref[:, LRU_WIDTH:] = dg.astype(BF16)

    rev = lambda e, c: e * nc + (nc - 1 - c)
    chunk = lambda col: pl.BlockSpec((tc, LRU_WIDTH), lambda e, c: (rev(e, c), col))
    prev8 = pl.BlockSpec((SUBLANES, LRU_WIDTH), lambda e, c: (jnp.maximum(rev(e, c) * nb - 1, 0), 0))
    outs, got = _call(
        body, comm, grid=(t // seq, nc),
        in_specs=[chunk(0), chunk(1), prev8, chunk(0), prev8, chunk(0)] + _lru_param_specs(2),
        out_specs=[pl.BlockSpec((tc, 2 * LRU_WIDTH), lambda e, c: (rev(e, c), 0)),
                   pl.BlockSpec((SUBLANES, LRU_WIDTH), lambda e, c: (0, 0)),
                   pl.BlockSpec((LRU_WIDTH // LANES, LANES, 2 * LANES), lambda e, c: (0, 0, 0))],
        out_shape=[jax.ShapeDtypeStruct((t, 2 * LRU_WIDTH), BF16), jax.ShapeDtypeStruct((SUBLANES, LRU_WIDTH), F32),
                   jax.ShapeDtypeStruct((LRU_WIDTH // LANES, LANES, 2 * LANES), F32)],
        scratch_shapes=[pltpu.VMEM((SUBLANES, LRU_WIDTH), F32)] * 3,
        args=(xl, xl, xl, h, h, dy, conv_w, conv_b, wbd, ba, bx, lam), name="lru_bwd")
    return (*outs, got)


def _adam_math(w, g, m, v):
    m2 = ADAM_B1 * m + (1.0 - ADAM_B1) * g
    v2 = ADAM_B2 * v + (1.0 - ADAM_B2) * (g * g)
    m_hat = m2 / (1.0 - ADAM_B1 ** ADAM_STEP)
    v_hat = v2 / (1.0 - ADAM_B2 ** ADAM_STEP)
    return -ADAM_LR * (m_hat / (jnp.sqrt(v_hat) + ADAM_EPS) + ADAM_WD * w), m2, v2


def _adamw(w, g, m, v, name):
    rows, cols = w.shape
    tr = 256 if rows % 256 == 0 else rows

    def kern(w_ref, g_ref, m_ref, v_ref, d_ref, m2_ref, v2_ref):
        d_ref[...], m2_ref[...], v2_ref[...] = _adam_math(w_ref[...], g_ref[...], m_ref[...], v_ref[...])

    blk = pl.BlockSpec((tr, cols), lambda i: (i, 0))
    out = jax.ShapeDtypeStruct((rows, cols), F32)
    return pl.pallas_call(kern, grid=(rows // tr,), in_specs=[blk] * 4, out_specs=[blk] * 3, out_shape=[out] * 3,
                          compiler_params=_cp(("parallel",)), name=name)(w, g, m, v)


def _adamw_small(ws, gs, ms, vs):
    n = len(ws)

    def kern(*refs):
        for k in range(n):
            outs = _adam_math(refs[k][...], refs[n + k][...], refs[2 * n + k][...], refs[3 * n + k][...])
            for j in range(3):
                refs[(4 + j) * n + k][...] = outs[j]

    vm = pl.BlockSpec(memory_space=pltpu.VMEM)
    out = pl.pallas_call(kern, in_specs=[vm] * (4 * n), out_specs=[vm] * (3 * n),
                         out_shape=[jax.ShapeDtypeStruct(w.shape, F32) for w in ws] * 3, name="adamw_small")(*ws, *gs, *ms, *vs)
    return out[:n], out[n:2 * n], out[2 * n:]


def _comm_only(kind, arrays, name):
    return _call(lambda ins, outs, scr: None, (kind, arrays), grid=(1,), in_specs=[], out_specs=[], out_shape=[],
                 scratch_shapes=[], args=(), name=name)[1]


def _pair_add(g, got, core):
    _, rows, cols = g.shape
    half = rows // 2
    tr = min(256, half)
    nt = half // tr

    def kern(c_ref, g_ref, o_ref, out_ref):
        out_ref[...] = (g_ref[...] + o_ref[...]).astype(BF16)

    return pl.pallas_call(
        kern, grid_spec=pltpu.PrefetchScalarGridSpec(
            num_scalar_prefetch=1, grid=(N_CHIPS, nt),
            in_specs=[pl.BlockSpec((None, tr, cols), lambda j, i, c_ref: (j, c_ref[0] * nt + i, 0)),
                      pl.BlockSpec((None, tr, cols), lambda j, i, c_ref: (j, i, 0))],
            out_specs=pl.BlockSpec((None, tr, cols), lambda j, i, c_ref: (j, i, 0))),
        out_shape=jax.ShapeDtypeStruct((N_CHIPS, half, cols), BF16),
        compiler_params=_cp(("parallel", "parallel")), name="pair_add",
    )(core, g, got)


def _chip_add(part, got, place):
    _, half, cols = part.shape
    tr = min(256, half)
    nt = half // tr

    def kern(p_ref, part_ref, got_ref, out_ref):
        out_ref[...] = (part_ref[...].astype(F32) + got_ref[0].astype(F32) + got_ref[1].astype(F32)
                        + got_ref[2].astype(F32))

    return pl.pallas_call(
        kern, grid_spec=pltpu.PrefetchScalarGridSpec(
            num_scalar_prefetch=1, grid=(nt,),
            in_specs=[pl.BlockSpec((None, tr, cols), lambda i, p_ref: (p_ref[0], i, 0)),
                      pl.BlockSpec((3, tr, cols), lambda i, p_ref: (0, i, 0))],
            out_specs=pl.BlockSpec((tr, cols), lambda i, p_ref: (p_ref[1] * nt + i, 0))),
        out_shape=jax.ShapeDtypeStruct((2 * half, cols), F32),
        compiler_params=_cp(("parallel",)), name="chip_add",
    )(place, part, got)


def _finale(packed, fulls):
    rows, n = packed.shape[0], len(fulls)

    def kern(in_ref, *refs):
        ins, out_ref, outs = refs[:n], refs[n], refs[n + 1:2 * n + 1]
        slots, send_sems, recv_sems, join_send, join_recv = refs[2 * n + 1:]
        x, y, c = _place()
        mine = 4 * x + 2 * y + c
        copies = []
        for w in range(n):
            half = ins[w].shape[0] // 2
            rws = pl.ds(c * half, half)
            copies.append(pltpu.make_async_remote_copy(
                src_ref=ins[w].at[rws, :], dst_ref=outs[w].at[rws, :], send_sem=join_send.at[w],
                recv_sem=join_recv.at[w], device_id=(x, y, 1 - c), device_id_type=MESH))
        for k in range(1, N_DEV):
            peer = (x ^ (k >> 2), y ^ ((k >> 1) & 1), c ^ (k & 1))
            copies.append(pltpu.make_async_remote_copy(
                src_ref=in_ref, dst_ref=slots.at[mine], send_sem=send_sems.at[k - 1], recv_sem=recv_sems.at[k - 1],
                device_id=peer, device_id_type=MESH))
        for cp in copies:
            cp.start()
        slots[mine] = in_ref[...]
        for cp in copies:
            cp.wait()
        acc = slots[0]
        for sl in range(1, N_DEV):
            acc = acc + slots[sl]
        out_ref[...] = acc

    vm = pl.BlockSpec(memory_space=pltpu.VMEM)
    out = pl.pallas_call(
        kern, in_specs=[vm] + [ANY] * n, out_specs=[vm] + [ANY] * n,
        out_shape=[jax.ShapeDtypeStruct((rows, LANES), F32)] + [jax.ShapeDtypeStruct(f.shape, f.dtype) for f in fulls],
        input_output_aliases={w + 1: w + 1 for w in range(n)},
        scratch_shapes=[pltpu.VMEM((N_DEV, rows, LANES), F32), pltpu.SemaphoreType.DMA((N_DEV - 1,)),
                        pltpu.SemaphoreType.DMA((N_DEV - 1,)), pltpu.SemaphoreType.DMA((n,)), pltpu.SemaphoreType.DMA((n,))],
        name="finale",
    )(packed, *fulls)
    return out[0], list(out[1:])


SMALL = ["norm1_g", "conv_w", "conv_b", "lru_w_a", "lru_b_a", "lru_w_x", "lru_b_x", "lru_lambda", "lru_out_g", "sb_out_g",
         "norm2_g", "final_g"]
BIG = ["w_in", "w_out", "w_up", "w_down"]
WEIGHTS = ["norm1_g", "w_in", "conv_w", "conv_b", "lru_w_a", "lru_b_a", "lru_w_x", "lru_b_x", "lru_lambda", "lru_out_g",
           "sb_out_g", "w_out", "norm2_g", "w_up", "w_down", "final_g"]


def _pack(arrays):
    flat = []
    for a in arrays:
        a = a.reshape(-1).astype(F32)
        flat.append(jnp.pad(a, (0, (-a.shape[0]) % LANES)))
    v = jnp.concatenate(flat)
    v = jnp.pad(v, (0, (-v.shape[0]) % (LANES * SUBLANES)))
    return v.reshape(-1, LANES)


def _unpack(packed, shapes):
    v, out, off = packed.reshape(-1), [], 0
    for shp in shapes:
        size = math.prod(shp)
        out.append(v[off:off + size].reshape(shp))
        off += size + (-size) % LANES
    return out


def _blockdiag_pairs(w):
    w = w.reshape(4, 2, DH, DH)
    z = jnp.zeros((4, DH, DH), w.dtype)
    return jnp.concatenate([jnp.concatenate([w[:, 0], z], axis=2), jnp.concatenate([z, w[:, 1]], axis=2)], axis=1)


def _blockdiag_unpairs(wbd):
    return jnp.stack([wbd[:, :DH, :DH], wbd[:, DH:, DH:]], axis=1).reshape(8, DH, DH)


def _full_cols(g):
    return jnp.transpose(g, (1, 0, 2)).reshape(g.shape[1], N_CHIPS * g.shape[2])


def _local_step(x2, tgt, seq, norm1_g, w_in, conv_w, conv_b, w_a, b_a, w_x, b_x, lru_lambda, lru_out_g, sb_out_g, rest,
                norm2_g, final_g, place=None):
    alone = place is None
    wbd = jnp.concatenate([_blockdiag_pairs(w_a), _blockdiag_pairs(w_x)], axis=2).astype(BF16)
    ba, bx = b_a.reshape(1, LRU_WIDTH), b_x.reshape(1, LRU_WIDTH)
    gf = final_g.reshape(1, D_MODEL)

    xn, got = _norm1(x2, norm1_g, None if alone else ("gather2", [w_in]))
    w_in_f = w_in if alone else _full_cols(got[0])
    xl, qkv, got = _inproj(xn, w_in_f, None if alone else ("gather", [conv_w]))
    conv_w_f = conv_w if alone else _full_cols(got[0])
    h, y_lru = _lru_fwd(xl, conv_w_f, conv_b, wbd, ba, bx, lru_lambda, seq)
    o, tot, kmin, got = _attn_fwd(qkv, seq, None if alone else ("gather2", rest))
    w_out_f, w_up_f, w_down_f = rest if alone else (
        got[0].reshape(D_MODEL, D_MODEL), _full_cols(got[1]), got[2].reshape(D_FF, D_MODEL))
    h1, mix = _outproj(y_lru, o, x2, lru_out_g, sb_out_g, w_out_f)
    hn, up, u2, dh2, dh2b, loss_part, d_final = _mlp_loss(h1, norm2_g, w_up_f, w_down_f, tgt, gf)

    dpre = _mlp_bwd_pre(dh2b, w_down_f, up)
    g_w_down = _matmul(u2, dh2b, "tn", F32, "dw_down", 1024, 1024, 1024).reshape(N_CHIPS, D_FF // N_CHIPS, D_MODEL)
    g_w_up = _matmul(hn, dpre, "tn", F32, "dw_up", 1024, D_FF // N_CHIPS, 1024, split_cols=True)
    dh1, d_norm2, _ = _proj_bwd_norm([dpre], w_up_f, h1, norm2_g, dh2, "mlp_bwd_in")
    g_w_out = _matmul(mix, dh1, "tn", F32, "dw_out", 1024, 1024, 2048).reshape(N_CHIPS, D_MODEL // N_CHIPS, D_MODEL)
    late = [g_w_out, g_w_up, g_w_down]
    dy_lru, do, d_ga, d_gb, swapped = _outproj_bwd(dh1, w_out_f, y_lru, o, lru_out_g, sb_out_g,
                                                   None if alone else ("swap", late))
    parts = None if alone else [_pair_add(g, r, place[1:]) for g, r in zip(late, swapped)]
    dq, dk, dv = _attn_bwd(qkv, do, tot, kmin, seq)
    dxl, lru_small, d_wbd, got = _lru_bwd(xl, h, dy_lru, conv_w_f, conv_b, wbd, ba, bx, lru_lambda, seq,
                                          None if alone else ("exchange", parts))
    if not alone:
        late = [_chip_add(p, r, place) for p, r in zip(parts, got)]
    dproj = jnp.concatenate([dxl, dq, dk, dv], axis=1)
    g_w_in = _matmul(xn, dproj, "tn", F32, "dw_in", 1024, IN_COLS // N_CHIPS, 2048, split_cols=True)
    part = None if alone else _pair_add(g_w_in, _comm_only("swap", [g_w_in], "pair_swap")[0], place[1:])
    dx, d_norm1, got = _proj_bwd_norm([dxl, dq, dk, dv], w_in_f, x2, norm1_g, dh1, "inproj_bwd",
                                      None if alone else ("exchange", [part]))
    if not alone:
        g_w_in = _chip_add(part, got[0], place)
    small_parts = {
        "norm1_g": d_norm1, "conv_w": lru_small[:CONV_WIDTH], "conv_b": lru_small[4:5],
        "lru_w_a": _blockdiag_unpairs(d_wbd[:, :, :LANES]), "lru_b_a": lru_small[5:6],
        "lru_w_x": _blockdiag_unpairs(d_wbd[:, :, LANES:]), "lru_b_x": lru_small[6:7], "lru_lambda": lru_small[7:8],
        "lru_out_g": d_ga, "sb_out_g": d_gb, "norm2_g": d_norm2, "final_g": d_final,
    }
    return loss_part, dx, [g_w_in] + late, small_parts


def kernel(x, norm1_g, w_in, conv_w, conv_b, lru_w_a, lru_b_a, lru_w_x, lru_b_x, lru_lambda, lru_out_g, sb_out_g, w_out, norm2_g, w_up, w_down, final_g, loss_target, m_norm1_g, m_w_in, m_conv_w, m_conv_b, m_lru_w_a, m_lru_b_a, m_lru_w_x, m_lru_b_x, m_lru_lambda, m_lru_out_g, m_sb_out_g, m_w_out, m_norm2_g, m_w_up, m_w_down, m_final_g, v_norm1_g, v_w_in, v_conv_w, v_conv_b, v_lru_w_a, v_lru_b_a, v_lru_w_x, v_lru_b_x, v_lru_lambda, v_lru_out_g, v_sb_out_g, v_w_out, v_norm2_g, v_w_up, v_w_down, v_final_g):
    given = dict(locals())
    ne, seq, _ = x.shape
    t = ne * seq
    xi, yi, ci = _place()
    place = jnp.stack([2 * xi + yi, ci]).astype(jnp.int32)

    loss_part, dx, halves, small_parts = _local_step(
        x.reshape(t, D_MODEL), loss_target.reshape(t, D_MODEL), seq, norm1_g, w_in[0].astype(BF16), conv_w[0], conv_b,
        lru_w_a[0], lru_b_a, lru_w_x[0], lru_b_x, lru_lambda, lru_out_g, sb_out_g,
        [w_out[0].astype(BF16), w_up[0].astype(BF16), w_down[0].astype(BF16)], norm2_g, final_g, place)

    full_shapes = {n: ((CONV_WIDTH, LRU_WIDTH) if n == "conv_w" else given[n].shape) for n in SMALL}
    red, fulls = _finale(_pack([small_parts[n] for n in SMALL] + [loss_part]), halves)
    red_list = _unpack(red, [full_shapes[n] for n in SMALL] + [(1, LANES)])
    grads = dict(zip(SMALL, red_list[:-1]))
    loss = red_list[-1][0, 0]
    grads["conv_w"] = lax.dynamic_slice_in_dim(grads["conv_w"], place[0] * (LRU_WIDTH // N_CHIPS), LRU_WIDTH // N_CHIPS,
                                               axis=1).reshape(conv_w.shape)
    for n, full in zip(BIG, fulls):
        grads[n] = full.reshape(given[n].shape)

    delta, new_m, new_v = {}, {}, {}
    for n in BIG:
        shp = given[n].shape
        d, m2, v2 = _adamw(given[n][0], grads[n][0], given["m_" + n][0], given["v_" + n][0], "adamw_" + n)
        delta[n], new_m[n], new_v[n] = d.reshape(shp), m2.reshape(shp), v2.reshape(shp)
    as2d = lambda a: a.reshape(-1, a.shape[-1])
    ds, m2s, v2s = _adamw_small([as2d(given[n]) for n in SMALL], [as2d(grads[n]) for n in SMALL],
                                [as2d(given["m_" + n]) for n in SMALL], [as2d(given["v_" + n]) for n in SMALL])
    for n, dd, mm, vv in zip(SMALL, ds, m2s, v2s):
        shp = given[n].shape
        delta[n], new_m[n], new_v[n] = dd.reshape(shp), mm.reshape(shp), vv.reshape(shp)

    return (loss, dx.reshape(x.shape), *[grads[n] for n in WEIGHTS], *[delta[n] for n in WEIGHTS],
            *[new_m[n] for n in WEIGHTS], *[new_v[n] for n in WEIGHTS])
```

```python
import functools
import math

import jax
import jax.numpy as jnp
from jax import lax
from jax.experimental import pallas as pl
from jax.experimental.pallas import tpu as pltpu

F32, BF16 = jnp.float32, jnp.bfloat16
MESH = pl.DeviceIdType.MESH

D_MODEL = 1024
LRU_WIDTH = 512
SB_WIDTH = 512
DH = 64
IN_COLS = 2 * LRU_WIDTH + 3 * SB_WIDTH
D_FF = 4 * D_MODEL
CONV_WIDTH = 4
LRU_C = 8.0
EPS = 1e-6
N_CHIPS = 4
N_DEV = 8
LANES = 128
SUBLANES = 8
TQ = 512
TK = 128
ATT_SCALE = 1.0 / math.sqrt(DH)
SKIP_LOG = -105.0
BAND = 2
VMEM_LIMIT = 52 * 1024 * 1024
VMEM_LIMIT_BIG = 62 * 1024 * 1024

ADAM_LR, ADAM_B1, ADAM_B2, ADAM_EPS, ADAM_WD, ADAM_STEP = 0.001, 0.9, 0.999, 1e-08, 0.01, 10

_GELU_K = math.sqrt(2.0 / math.pi)
_GELU_C = 0.044715


def _cp(sem, vmem=VMEM_LIMIT):
    return pltpu.CompilerParams(dimension_semantics=sem, vmem_limit_bytes=vmem)


def _dot(a, b):
    return jnp.dot(a, b, preferred_element_type=F32)


def _dot_nt(a, b):
    return lax.dot_general(a, b, (((1,), (1,)), ((), ())), preferred_element_type=F32)


def _dot_tn(a, b):
    return lax.dot_general(a, b, (((0,), (0,)), ((), ())), preferred_element_type=F32)


def _rstd(x):
    return lax.rsqrt(jnp.mean(x * x, axis=-1, keepdims=True) + EPS)


def _rms_bwd(x, g, dy):
    r = _rstd(x)
    gd = g * dy
    dx = r * gd - x * (r * r * r) * jnp.mean(x * gd, axis=-1, keepdims=True)
    return dx, jnp.sum(dy * x * r, axis=0, keepdims=True)


def _sigmoid(x):
    return 0.5 * jnp.tanh(0.5 * x) + 0.5


def _softplus(x):
    return jnp.maximum(x, 0.0) + jnp.log(1.0 + jnp.exp(-jnp.abs(x)))


def _neg_expm1(x, ex):
    series = -x * (1.0 + x * (0.5 + x * (1.0 / 6.0)))
    return jnp.where(x > -2.0 ** -7, series, 1.0 - ex)


def _gelu(g):
    t = jnp.tanh(_GELU_K * (g + _GELU_C * g * g * g))
    return 0.5 * g * (1.0 + t), t


def _gelu_grad(g, t):
    return 0.5 * (1.0 + t) + 0.5 * g * (1.0 - t * t) * _GELU_K * (1.0 + 3.0 * _GELU_C * g * g)


def _rows(shape):
    return lax.broadcasted_iota(jnp.int32, shape, 0)


def _shift_down(x, s, fill):
    n, c = x.shape
    if s % SUBLANES == 0:
        return jnp.concatenate([jnp.broadcast_to(jnp.asarray(fill, x.dtype), (s, c)), x[:n - s]], axis=0)
    return jnp.where(_rows(x.shape) >= s, pltpu.roll(x, s, 0), fill)


def _shift_up(x, s, fill):
    n, c = x.shape
    if s % SUBLANES == 0:
        return jnp.concatenate([x[s:], jnp.broadcast_to(jnp.asarray(fill, x.dtype), (s, c))], axis=0)
    return jnp.where(_rows(x.shape) < n - s, pltpu.roll(x, n - s, 0), fill)


def _row_of(x, idx):
    return jnp.sum(jnp.where(_rows(x.shape) == idx, x, 0.0), axis=0, keepdims=True)


def _matmul(a, b, dims, out_dtype, name, tm, tn, tk, split_cols=False):
    if dims == "nn":
        (m, kk), n, dot = a.shape, b.shape[1], _dot
    elif dims == "nt":
        (m, kk), n, dot = a.shape, b.shape[0], _dot_nt
    else:
        (kk, m), n, dot = a.shape, b.shape[1], _dot_tn
    tm, tn, tk = min(tm, m), min(tn, n), min(tk, kk)
    assert m % tm == 0 and n % tn == 0 and kk % tk == 0, (name, m, n, kk)
    nk = kk // tk

    def kern(a_ref, b_ref, o_ref, acc_ref):
        k = pl.program_id(2)

        @pl.when(k == 0)
        def _():
            acc_ref[...] = jnp.zeros_like(acc_ref)

        acc_ref[...] += dot(a_ref[...].astype(BF16), b_ref[...].astype(BF16))

        @pl.when(k == nk - 1)
        def _():
            o_ref[...] = acc_ref[...].astype(o_ref.dtype)

    if split_cols:
        out_shape = jax.ShapeDtypeStruct((n // tn, m, tn), out_dtype)
        o_spec = pl.BlockSpec((None, tm, tn), lambda i, j, k: (j, i, 0))
    else:
        out_shape = jax.ShapeDtypeStruct((m, n), out_dtype)
        o_spec = pl.BlockSpec((tm, tn), lambda i, j, k: (i, j))
    if dims == "nn":
        a_spec = pl.BlockSpec((tm, tk), lambda i, j, k: (i, k))
        b_spec = pl.BlockSpec((tk, tn), lambda i, j, k: (k, j))
    elif dims == "nt":
        a_spec = pl.BlockSpec((tm, tk), lambda i, j, k: (i, k))
        b_spec = pl.BlockSpec((tn, tk), lambda i, j, k: (j, k))
    else:
        a_spec = pl.BlockSpec((tk, tm), lambda i, j, k: (k, i))
        b_spec = pl.BlockSpec((tk, tn), lambda i, j, k: (k, j))
    return pl.pallas_call(
        kern, grid=(m // tm, n // tn, nk), in_specs=[a_spec, b_spec], out_specs=o_spec, out_shape=out_shape,
        scratch_shapes=[pltpu.VMEM((tm, tn), F32)], compiler_params=_cp(("parallel", "parallel", "arbitrary")), name=name,
    )(a, b)


ANY = pl.BlockSpec(memory_space=pl.ANY)


def _place():
    return lax.axis_index("x"), lax.axis_index("y"), lax.axis_index("c")


def _other_chips(x, y):
    return [(1 - x, y), (x, 1 - y), (1 - x, 1 - y)]


def _own_slab(shard, gathered, send_sem, recv_sem):
    x, y, c = _place()
    return pltpu.make_async_remote_copy(src_ref=shard, dst_ref=gathered.at[2 * x + y], send_sem=send_sem, recv_sem=recv_sem,
                                        device_id=(x, y, 1 - c), device_id_type=MESH)


def _gather_copies(ins, outs, send_sems, recv_sems, own_send, own_recv):
    x, y, c = _place()
    mine = 2 * x + y
    copies = []
    for w in range(len(ins)):
        copies.append(_own_slab(ins[w], outs[w], own_send.at[w], own_recv.at[w]))
        for k, chip in enumerate(_other_chips(x, y)):
            copies.append(pltpu.make_async_remote_copy(
                src_ref=ins[w], dst_ref=outs[w].at[mine], send_sem=send_sems.at[3 * w + k],
                recv_sem=recv_sems.at[3 * w + k], device_id=(*chip, c), device_id_type=MESH))
    return copies


def _gather_shapes(shards):
    return ([jax.ShapeDtypeStruct((N_CHIPS,) + s.shape, s.dtype) for s in shards],
            [pltpu.SemaphoreType.DMA((3 * len(shards),)), pltpu.SemaphoreType.DMA((3 * len(shards),)),
             pltpu.SemaphoreType.DMA((len(shards),)), pltpu.SemaphoreType.DMA((len(shards),))])


def _exchange_copies(ins, outs, send_sems, recv_sems):
    x, y, c = _place()
    copies = []
    for w in range(len(ins)):
        for k, chip in enumerate(_other_chips(x, y)):
            copies.append(pltpu.make_async_remote_copy(
                src_ref=ins[w].at[2 * chip[0] + chip[1]], dst_ref=outs[w].at[k], send_sem=send_sems.at[3 * w + k],
                recv_sem=recv_sems.at[3 * w + k], device_id=(*chip, c), device_id_type=MESH))
    return copies


def _exchange_shapes(parts):
    return ([jax.ShapeDtypeStruct((3,) + p.shape[1:], p.dtype) for p in parts],
            [pltpu.SemaphoreType.DMA((3 * len(parts),)), pltpu.SemaphoreType.DMA((3 * len(parts),))])


def _swap_copies(ins, outs, send_sems, recv_sems):
    x, y, c = _place()
    copies = []
    for w in range(len(ins)):
        half = ins[w].shape[1] // 2
        copies.append(pltpu.make_async_remote_copy(
            src_ref=ins[w].at[:, pl.ds((1 - c) * half, half), :], dst_ref=outs[w], send_sem=send_sems.at[w],
            recv_sem=recv_sems.at[w], device_id=(x, y, 1 - c), device_id_type=MESH))
    return copies


def _swap_shapes(grads):
    return ([jax.ShapeDtypeStruct((g.shape[0], g.shape[1] // 2, g.shape[2]), g.dtype) for g in grads],
            [pltpu.SemaphoreType.DMA((len(grads),)), pltpu.SemaphoreType.DMA((len(grads),))])


def _gather2_copies(ins, outs, send_sems, recv_sems, own_send, own_recv, fwd_send, fwd_recv):
    x, y, c = _place()
    mine = 2 * x + y
    copies = []
    for w in range(len(ins)):
        half = ins[w].shape[0] // 2
        rows = pl.ds(c * half, half)
        copies.append(_own_slab(ins[w], outs[w], own_send.at[w], own_recv.at[w]))
        for k, chip in enumerate(_other_chips(x, y)):
            copies.append(pltpu.make_async_remote_copy(
                src_ref=ins[w].at[rows, :], dst_ref=outs[w].at[mine, rows, :], send_sem=send_sems.at[3 * w + k],
                recv_sem=recv_sems.at[3 * w + k], device_id=(*chip, c), device_id_type=MESH))
    return copies


def _gather2_forward(ins, outs, send_sems, recv_sems, own_send, own_recv, fwd_send, fwd_recv):
    x, y, c = _place()
    copies = []
    for w in range(len(ins)):
        half = ins[w].shape[0] // 2
        rows = pl.ds(c * half, half)
        for k, chip in enumerate(_other_chips(x, y)):
            slab = outs[w].at[2 * chip[0] + chip[1], rows, :]
            copies.append(pltpu.make_async_remote_copy(
                src_ref=slab, dst_ref=slab, send_sem=fwd_send.at[3 * w + k], recv_sem=fwd_recv.at[3 * w + k],
                device_id=(x, y, 1 - c), device_id_type=MESH))
    return copies


def _gather2_shapes(shards):
    n = len(shards)
    return ([jax.ShapeDtypeStruct((N_CHIPS,) + s.shape, s.dtype) for s in shards],
            [pltpu.SemaphoreType.DMA((3 * n,)), pltpu.SemaphoreType.DMA((3 * n,)), pltpu.SemaphoreType.DMA((n,)),
             pltpu.SemaphoreType.DMA((n,)), pltpu.SemaphoreType.DMA((3 * n,)), pltpu.SemaphoreType.DMA((3 * n,))])


COMM = {"gather": (_gather_copies, _gather_shapes, None), "exchange": (_exchange_copies, _exchange_shapes, None),
        "swap": (_swap_copies, _swap_shapes, None), "gather2": (_gather2_copies, _gather2_shapes, _gather2_forward)}


def _call(body, comm, *, grid, in_specs, out_specs, out_shape, scratch_shapes, args, name):
    ni, no, ns = len(in_specs), len(out_specs), len(scratch_shapes)
    arrays = list(comm[1]) if comm else []
    nc = len(arrays)
    first_fn, shapes_fn, second_fn = COMM[comm[0]] if comm else (None, None, None)
    c_shapes, c_sems = shapes_fn(arrays) if comm else ([], [])

    def kern(*refs):
        ins, cin, outs = refs[:ni], refs[ni:ni + nc], refs[ni + nc:ni + nc + no]
        cout, scr, sems = refs[ni + nc + no:ni + 2 * nc + no], refs[ni + 2 * nc + no:ni + 2 * nc + no + ns], refs[ni + 2 * nc + no + ns:]
        ids = [pl.program_id(d) for d in range(len(grid))]
        if nc:
            @pl.when(functools.reduce(lambda a, b: a & b, [i == 0 for i in ids]))
            def _():
                for cp in first_fn(cin, cout, *sems):
                    cp.start()

        body(ins, outs, scr)
        if nc:
            @pl.when(functools.reduce(lambda a, b: a & b, [i == g - 1 for i, g in zip(ids, grid)]))
            def _():
                for cp in first_fn(cin, cout, *sems):
                    cp.wait()
                if second_fn is not None:
                    more = second_fn(cin, cout, *sems)
                    for cp in more:
                        cp.start()
                    for cp in more:
                        cp.wait()

    out = pl.pallas_call(
        kern, grid=grid, in_specs=list(in_specs) + [ANY] * nc, out_specs=list(out_specs) + [ANY] * nc,
        out_shape=list(out_shape) + c_shapes, scratch_shapes=list(scratch_shapes) + c_sems,
        compiler_params=_cp(("arbitrary",) * len(grid)), name=name,
    )(*args, *arrays)
    return list(out[:no]), list(out[no:])


def _resident(shape):
    return pl.BlockSpec(shape, lambda *_: (0,) * len(shape), pipeline_mode=pl.Buffered(1))


def _norm1(x, g1, comm):
    t = x.shape[0]
    tm = min(1024, t)

    def body(ins, outs, _):
        xv = ins[0][...]
        outs[0][...] = (xv * _rstd(xv) * ins[1][...]).astype(BF16)

    row = pl.BlockSpec((tm, D_MODEL), lambda i: (i, 0))
    (xn,), got = _call(body, comm, grid=(t // tm,), in_specs=[row, pl.BlockSpec((1, D_MODEL), lambda i: (0, 0))],
                       out_specs=[row], out_shape=[jax.ShapeDtypeStruct((t, D_MODEL), BF16)], scratch_shapes=[],
                       args=(x, g1), name="norm1")
    return xn, got


def _inproj(xn, w_in, comm=None):
    t = xn.shape[0]
    tm = min(512, t)

    def body(ins, outs, _):
        xn_v = ins[0][...]
        outs[0][...] = _dot(xn_v, ins[1][:, : 2 * LRU_WIDTH])
        outs[1][...] = _dot(xn_v, ins[1][:, 2 * LRU_WIDTH:]).astype(BF16)

    row = lambda c: pl.BlockSpec((tm, c), lambda i: (i, 0))
    (xl, qkv), got = _call(
        body, comm, grid=(t // tm,), in_specs=[row(D_MODEL), _resident((D_MODEL, IN_COLS))],
        out_specs=[row(2 * LRU_WIDTH), row(3 * SB_WIDTH)],
        out_shape=[jax.ShapeDtypeStruct((t, 2 * LRU_WIDTH), F32), jax.ShapeDtypeStruct((t, 3 * SB_WIDTH), BF16)],
        scratch_shapes=[], args=(xn, w_in), name="inproj")
    return xl, qkv, got


def _conv_taps(hist, u):
    cat = jnp.concatenate([hist, u], axis=0)
    return [pltpu.roll(cat, CONV_WIDTH - 1 - k, 0)[SUBLANES:] for k in range(CONV_WIDTH - 1)] + [u]


def _lru_gates(c, wbd_ref, ba, bx, sp):
    gas, gxs = [], []
    for p in range(LRU_WIDTH // LANES):
        gax = _dot(c[:, LANES * p: LANES * (p + 1)].astype(BF16), wbd_ref[p])
        gas.append(gax[:, :LANES])
        gxs.append(gax[:, LANES:])
    r = _sigmoid(jnp.concatenate(gas, axis=1) + ba)
    i = _sigmoid(jnp.concatenate(gxs, axis=1) + bx)
    la = (-LRU_C) * r * sp
    a = jnp.exp(la)
    e2 = _neg_expm1(2.0 * la, a * a)
    inv_mult = lax.rsqrt(jnp.maximum(e2, 1e-30))
    return r, i, a, e2 * inv_mult, inv_mult


def _scan_fwd(a, b):
    s = 1
    while s < a.shape[0]:
        b = b + a * _shift_down(b, s, 0.0)
        a = a * _shift_down(a, s, 1.0)
        s *= 2
    return a, b


def _scan_rev(a, b):
    s = 1
    while s < a.shape[0]:
        b = b + a * _shift_up(b, s, 0.0)
        a = a * _shift_up(a, s, 1.0)
        s *= 2
    return a, b


def _lru_param_specs(grid_rank):
    z2 = (lambda e, c: (0, 0)) if grid_rank == 2 else None
    return [
        pl.BlockSpec((CONV_WIDTH, LRU_WIDTH), z2), pl.BlockSpec((1, LRU_WIDTH), z2),
        pl.BlockSpec((LRU_WIDTH // LANES, LANES, 2 * LANES), lambda e, c: (0, 0, 0)),
        pl.BlockSpec((1, LRU_WIDTH), z2), pl.BlockSpec((1, LRU_WIDTH), z2), pl.BlockSpec((1, LRU_WIDTH), z2),
    ]


def _lru_fwd(xl, conv_w, conv_b, wbd, ba, bx, lam, seq):
    t = xl.shape[0]
    tc = min(512, seq)
    nc = seq // tc

    def kern(u_ref, g_ref, cw_ref, cb_ref, wbd_ref, ba_ref, bx_ref, lam_ref, h_ref, y_ref, hist_ref, hcar_ref):
        @pl.when(pl.program_id(1) == 0)
        def _():
            hist_ref[...] = jnp.zeros_like(hist_ref)
            hcar_ref[...] = jnp.zeros_like(hcar_ref)

        u = u_ref[...]
        taps = _conv_taps(hist_ref[...], u)
        hist_ref[...] = u_ref[tc - SUBLANES:, :]
        c = cb_ref[...]
        for k in range(CONV_WIDTH):
            c = c + taps[k] * cw_ref[k:k + 1, :]
        sp = _softplus(-lam_ref[...])
        _, i, a, mult, _ = _lru_gates(c, wbd_ref, ba_ref[...], bx_ref[...], sp)
        aa, bb = _scan_fwd(a, mult * i * c)
        h = bb + aa * hcar_ref[0:1, :]
        h_ref[...] = h
        hcar_ref[0:1, :] = h_ref[tc - 1:tc, :]
        y_ref[...] = h * _gelu(g_ref[...])[0]

    chunk = lambda col: pl.BlockSpec((tc, LRU_WIDTH), lambda e, c: (e * nc + c, col))
    out = jax.ShapeDtypeStruct((t, LRU_WIDTH), F32)
    return pl.pallas_call(
        kern, grid=(t // seq, nc), in_specs=[chunk(0), chunk(1)] + _lru_param_specs(2),
        out_specs=[chunk(0), chunk(0)], out_shape=[out, out],
        scratch_shapes=[pltpu.VMEM((SUBLANES, LRU_WIDTH), F32), pltpu.VMEM((SUBLANES, LRU_WIDTH), F32)],
        compiler_params=_cp(("arbitrary", "arbitrary")), name="lru_fwd",
    )(xl, xl, conv_w, conv_b, wbd, ba, bx, lam)


def _att_consts():
    row = lax.broadcasted_iota(jnp.int32, (TQ, 2 * TK), 0)
    key = lax.broadcasted_iota(jnp.int32, (TQ, 2 * TK), 1) & (TK - 1)
    return key < row


def _sum_matrix(kind):
    j = lax.broadcasted_iota(jnp.int32, (2 * TK, 2 * TK), 0) & (TK - 1)
    s = lax.broadcasted_iota(jnp.int32, (2 * TK, 2 * TK), 1)
    pick = {"after": j > s, "upto": j <= s, "before": j < s}[kind]
    return jnp.where((s >= TK) | pick, 1.0, 0.0).astype(BF16)


def _hi_lo(x):
    hi = x.astype(BF16)
    return hi, (x - hi.astype(F32)).astype(BF16)


def _pair_sums(x, m):
    hi, lo = _hi_lo(x)
    out = []
    for hd in range(2):
        cols = slice(hd * TK, (hd + 1) * TK)
        out.append(_dot(jnp.concatenate([hi[:, cols], lo[:, cols]], axis=1), m))
    return [o[:, :TK] for o in out], [o[:, TK:] for o in out]


def _att_logits(qb, kbd):
    z = _dot(qb, kbd)
    lg = jnp.log(1.0 + jnp.exp(-jnp.abs(z)))
    lb = jnp.minimum(z, 0.0) - lg
    return lb, lb - z


def _head_diag(x, rows_first):
    n = x.shape[0] if rows_first else x.shape[1]
    idx = lax.broadcasted_iota(jnp.int32, x.shape, 0 if rows_first else 1)
    return jnp.where(idx < n // 2, x, 0), jnp.where(idx >= n // 2, x, 0)


def _band_tiles():
    nd = TQ // TK
    return [(jr, TK * max(jr, 0), TK * min(jr + BAND + 1, nd)) for jr in range(nd - 1, -BAND - 1, -1)]


def _rows_update(st, new, lo, hi):
    def one(x, y):
        pieces = ([x[:lo]] if lo else []) + [y] + ([x[hi:]] if hi < x.shape[0] else [])
        return pieces[0] if len(pieces) == 1 else jnp.concatenate(pieces, axis=0)
    return tuple(one(x, y) for x, y in zip(st, new))


def _scaled_q(q_ref, q0):
    return (q_ref[pl.ds(q0, TQ), :].astype(F32) * ATT_SCALE).astype(BF16)


def _qkv_specs(seq):
    n = SB_WIDTH // LANES
    return [pl.BlockSpec((seq, LANES), lambda e, p, off=off: (e, off * n + p)) for off in range(3)]


def _attn_fwd(qkv, seq, comm=None):
    t = qkv.shape[0]
    ne, nq, nk = t // seq, seq // TQ, seq // TK

    def body(ins, outs, scr):
        (q_ref, k_ref, v_ref), (o_ref, tot_ref, kmin_ref), (kbd_scr, vbd_scr) = ins, outs, scr
        causal = _att_consts()
        after = _sum_matrix("after")

        def prep(j, _):
            k0 = pl.multiple_of(j * TK, TK)
            top, bot = _head_diag(k_ref[pl.ds(k0, TK), :].T, True)
            kbd_scr[j] = jnp.concatenate([top, bot], axis=1)
            left, right = _head_diag(v_ref[pl.ds(k0, TK), :], False)
            vbd_scr[j] = jnp.concatenate([left, right], axis=0)
            return 0

        lax.fori_loop(0, nk, prep, 0)

        def block(j, qb, st, mask):
            c0, c1, oacc = st
            lb, l1 = _att_logits(qb, kbd_scr[j])
            if mask is not None:
                l1 = jnp.where(mask, l1, 0.0)
            (s0, s1), (r0, r1) = _pair_sums(l1, after)
            att = jnp.exp(lb + jnp.concatenate([s0 + c0, s1 + c1], axis=1))
            if mask is not None:
                att = jnp.where(mask, att, 0.0)
            return c0 + r0, c1 + r1, oacc + _dot(att.astype(BF16), vbd_scr[j])

        def general(qi, qb, st):
            for jj in reversed(range(TQ // TK)):
                lo = TK * jj
                new = block((TQ // TK) * qi + jj, qb[lo:], tuple(x[lo:] for x in st), causal[:TQ - lo])
                st = _rows_update(st, new, lo, TQ)

            npair = (TQ // TK // 2) * qi

            def more(its):
                return (its[0] < npair) & (jnp.max(jnp.maximum(its[1], its[2])) > SKIP_LOG)

            def kloop(its):
                j = 2 * (npair - its[0]) - 1
                return (its[0] + 1,) + block(j - 1, qb, block(j, qb, its[1:], None), None)

            done, c0, c1, oacc = lax.while_loop(more, kloop, (jnp.int32(0),) + st)
            return c0, c1, oacc, 2 * (npair - done)

        def short(qi, qb, st):
            for jr, lo, hi in _band_tiles():
                new = block((TQ // TK) * qi + jr, qb[lo:hi], tuple(x[lo:hi] for x in st),
                            causal[:hi - lo] if jr >= 0 else None)
                st = _rows_update(st, new, lo, hi)
            return st

        def qloop(qi, _):
            q0 = pl.multiple_of(qi * TQ, TQ)
            qb = _scaled_q(q_ref, q0)
            zero = jnp.zeros((TQ, TK), F32)
            st = (zero, zero, jnp.zeros((TQ, LANES), F32))

            def try_short():
                c0, c1, oacc = short(qi, qb, st)
                return lax.cond(jnp.max(jnp.maximum(c0, c1)) <= SKIP_LOG, lambda: (c0, c1, oacc, jnp.int32(-1)),
                                lambda: general(qi, qb, st))

            c0, c1, oacc, first = lax.cond(qi > 0, try_short, lambda: general(qi, qb, st))
            o_ref[pl.ds(q0, TQ), :] = oacc
            tot_ref[pl.ds(q0, TQ), :] = jnp.concatenate([c0, c1], axis=1)
            kmin_ref[pl.program_id(0), pl.program_id(1), qi] = first
            return 0

        lax.fori_loop(0, nq, qloop, 0)

    (o, tot, kmin), got = _call(
        body, comm, grid=(ne, SB_WIDTH // LANES), in_specs=_qkv_specs(seq),
        out_specs=[pl.BlockSpec((seq, LANES), lambda e, p: (e, p)), pl.BlockSpec((seq, 2 * TK), lambda e, p: (e, p)),
                   pl.BlockSpec(memory_space=pltpu.SMEM)],
        out_shape=[jax.ShapeDtypeStruct((t, SB_WIDTH), F32), jax.ShapeDtypeStruct((t, 2 * TK * SB_WIDTH // LANES), F32),
                   jax.ShapeDtypeStruct((ne, SB_WIDTH // LANES, nq), jnp.int32)],
        scratch_shapes=[pltpu.VMEM((nk, LANES, 2 * TK), BF16), pltpu.VMEM((nk, 2 * TK, LANES), BF16)],
        args=(qkv, qkv, qkv), name="attn_fwd")
    return o, tot, kmin, got


def _outproj(y_lru, o, x, ga, gb, w_out):
    t = x.shape[0]
    tm = min(512, t)

    def kern(y_ref, o_ref, x_ref, ga_ref, gb_ref, w_ref, h1_ref, mix_ref):
        yv, ov = y_ref[...], o_ref[...]
        mix = jnp.concatenate([yv * _rstd(yv) * ga_ref[...], ov * _rstd(ov) * gb_ref[...]], axis=1).astype(BF16)
        mix_ref[...] = mix
        h1_ref[...] = x_ref[...] + _dot(mix, w_ref[...])

    row = lambda c: pl.BlockSpec((tm, c), lambda i: (i, 0))
    vec = lambda c: pl.BlockSpec((1, c), lambda i: (0, 0))
    return pl.pallas_call(
        kern, grid=(t // tm,),
        in_specs=[row(LRU_WIDTH), row(SB_WIDTH), row(D_MODEL), vec(LRU_WIDTH), vec(SB_WIDTH),
                  pl.BlockSpec((D_MODEL, D_MODEL), lambda i: (0, 0))],
        out_specs=[row(D_MODEL), row(D_MODEL)],
        out_shape=[jax.ShapeDtypeStruct((t, D_MODEL), F32), jax.ShapeDtypeStruct((t, D_MODEL), BF16)],
        compiler_params=_cp(("parallel",)), name="outproj",
    )(y_lru, o, x, ga, gb, w_out)


def _mlp_loss(h1, g2, w_up, w_down, target, gf):
    t = h1.shape[0]
    tm, tf = min(512, t), 1024

    def kern(h1_ref, g_ref, wu_ref, wd_ref, t_ref, gf_ref, hn_ref, up_ref, u2_ref, dh_ref, dhb_ref, loss_ref, dg_ref):
        @pl.when(pl.program_id(0) == 0)
        def _():
            loss_ref[...] = jnp.zeros_like(loss_ref)
            dg_ref[...] = jnp.zeros_like(dg_ref)

        hv = h1_ref[...]
        hn = (hv * _rstd(hv) * g_ref[...]).astype(BF16)
        hn_ref[...] = hn
        h2 = hv
        for f in range(D_FF // tf):
            cols = slice(f * tf, (f + 1) * tf)
            up = jnp.maximum(_dot(hn, wu_ref[:, cols]), 0.0)
            u2 = (up * up).astype(BF16)
            up_ref[:, cols] = up.astype(BF16)
            u2_ref[:, cols] = u2
            h2 = h2 + _dot(u2, wd_ref[cols, :])

        g = gf_ref[...]
        err = h2 * _rstd(h2) * g - t_ref[...]
        lane = lax.broadcasted_iota(jnp.int32, (1, LANES), 1)
        loss_ref[...] += jnp.where(lane == 0, 0.5 * jnp.sum(err * err) / D_MODEL, 0.0)
        dx, dg = _rms_bwd(h2, g, err * (1.0 / D_MODEL))
        dh_ref[...] = dx
        dhb_ref[...] = dx.astype(BF16)
        dg_ref[...] += dg

    row = lambda c: pl.BlockSpec((tm, c), lambda i: (i, 0))
    vec = pl.BlockSpec((1, D_MODEL), lambda i: (0, 0))
    return pl.pallas_call(
        kern, grid=(t // tm,),
        in_specs=[row(D_MODEL), vec, _resident((D_MODEL, D_FF)), _resident((D_FF, D_MODEL)), row(D_MODEL), vec],
        out_specs=[row(D_MODEL), row(D_FF), row(D_FF), row(D_MODEL), row(D_MODEL), pl.BlockSpec((1, LANES), lambda i: (0, 0)), vec],
        out_shape=[jax.ShapeDtypeStruct((t, D_MODEL), BF16), jax.ShapeDtypeStruct((t, D_FF), BF16),
                   jax.ShapeDtypeStruct((t, D_FF), BF16), jax.ShapeDtypeStruct((t, D_MODEL), F32),
                   jax.ShapeDtypeStruct((t, D_MODEL), BF16), jax.ShapeDtypeStruct((1, LANES), F32),
                   jax.ShapeDtypeStruct((1, D_MODEL), F32)],
        compiler_params=_cp(("arbitrary",), VMEM_LIMIT_BIG), name="mlp_loss",
    )(h1, g2, w_up, w_down, target, gf)


def _mlp_bwd_pre(dh2, w_down, up):
    t = dh2.shape[0]
    tm, tf = min(512, t), 1024

    def kern(d_ref, w_ref, up_ref, o_ref):
        dv = d_ref[...]
        for f in range(D_FF // tf):
            cols = slice(f * tf, (f + 1) * tf)
            o_ref[:, cols] = (_dot_nt(dv, w_ref[cols, :]) * (2.0 * up_ref[:, cols].astype(F32))).astype(BF16)

    row = lambda c: pl.BlockSpec((tm, c), lambda i: (i, 0))
    return pl.pallas_call(
        kern, grid=(t // tm,), in_specs=[row(D_MODEL), _resident((D_FF, D_MODEL)), row(D_FF)],
        out_specs=row(D_FF), out_shape=jax.ShapeDtypeStruct((t, D_FF), BF16),
        compiler_params=_cp(("parallel",)), name="mlp_bwd_pre",
    )(dh2, w_down, up)


def _proj_bwd_norm(dys, w, x, g, resid, name, comm=None, bf16_copy=False):
    t = x.shape[0]
    tm = min(512, t)
    widths = [dy.shape[1] for dy in dys]
    n = len(dys)

    def body(ins, outs, _):
        dy_refs, (w_ref, x_ref, g_ref, r_ref), (dx_ref, dg_ref) = ins[:n], ins[n:], outs[:2]

        @pl.when(pl.program_id(0) == 0)
        def _():
            dg_ref[...] = jnp.zeros_like(dg_ref)

        off, dxn = 0, None
        for dy_ref, wd in zip(dy_refs, widths):
            part = _dot_nt(dy_ref[...], w_ref[:, off:off + wd])
            dxn = part if dxn is None else dxn + part
            off += wd
        dx, dg = _rms_bwd(x_ref[...], g_ref[...], dxn)
        dx = r_ref[...] + dx
        dx_ref[...] = dx
        dg_ref[...] += dg
        if bf16_copy:
            outs[2][...] = dx.astype(BF16)

    row = lambda c: pl.BlockSpec((tm, c), lambda i: (i, 0))
    vec = pl.BlockSpec((1, D_MODEL), lambda i: (0, 0))
    outs, got = _call(
        body, comm, grid=(t // tm,), in_specs=[row(wd) for wd in widths] + [_resident(w.shape), row(D_MODEL), vec, row(D_MODEL)],
        out_specs=[row(D_MODEL), vec] + [row(D_MODEL)] * bf16_copy,
        out_shape=[jax.ShapeDtypeStruct((t, D_MODEL), F32), jax.ShapeDtypeStruct((1, D_MODEL), F32)]
        + [jax.ShapeDtypeStruct((t, D_MODEL), BF16)] * bf16_copy,
        scratch_shapes=[], args=(*dys, w, x, g, resid), name=name)
    return outs[0], (outs[2] if bf16_copy else None), outs[1], got


def _outproj_bwd(dh1, w_out, y_lru, o, ga, gb, comm=None):
    t = dh1.shape[0]
    tm = min(512, t)

    def body(ins, outs, _):
        (d_ref, w_ref, y_ref, o_ref, ga_ref, gb_ref), (dy_ref, do_ref, dga_ref, dgb_ref) = ins, outs

        @pl.when(pl.program_id(0) == 0)
        def _():
            dga_ref[...] = jnp.zeros_like(dga_ref)
            dgb_ref[...] = jnp.zeros_like(dgb_ref)

        dmix = _dot_nt(d_ref[...], w_ref[...])
        dy, dga = _rms_bwd(y_ref[...], ga_ref[...], dmix[:, :LRU_WIDTH])
        do, dgb = _rms_bwd(o_ref[...], gb_ref[...], dmix[:, LRU_WIDTH:])
        dy_ref[...] = dy
        do_ref[...] = do
        dga_ref[...] += dga
        dgb_ref[...] += dgb

    row = lambda c: pl.BlockSpec((tm, c), lambda i: (i, 0))
    vec = pl.BlockSpec((1, LRU_WIDTH), lambda i: (0, 0))
    half = jax.ShapeDtypeStruct((t, LRU_WIDTH), F32)
    gsum = jax.ShapeDtypeStruct((1, LRU_WIDTH), F32)
    outs, got = _call(body, comm, grid=(t // tm,),
                      in_specs=[row(D_MODEL), _resident((D_MODEL, D_MODEL)), row(LRU_WIDTH), row(SB_WIDTH), vec, vec],
                      out_specs=[row(LRU_WIDTH), row(SB_WIDTH), vec, vec], out_shape=[half, half, gsum, gsum],
                      scratch_shapes=[], args=(dh1, w_out, y_lru, o, ga, gb), name="outproj_bwd")
    return (*outs, got)


def _attn_bwd(qkv, do, tot, kmin, seq):
    t = qkv.shape[0]
    ne, nq, nk = t // seq, seq // TQ, seq // TK

    def kern(q_ref, k_ref, v_ref, do_ref, tot_ref, kmin_ref, dq_ref, dk_ref, dv_ref,
             kbd_scr, vtbd_scr, kbd2_scr, dkt_scr, dvt_scr):
        causal = _att_consts()
        upto, before = _sum_matrix("upto"), _sum_matrix("before")

        def prep(j, _):
            k0 = pl.multiple_of(j * TK, TK)
            kb = k_ref[pl.ds(k0, TK), :]
            top, bot = _head_diag(kb.T, True)
            kbd_scr[j] = jnp.concatenate([top, bot], axis=1)
            top, bot = _head_diag(v_ref[pl.ds(k0, TK), :].T, True)
            vtbd_scr[j] = jnp.concatenate([top, bot], axis=1)
            left, right = _head_diag(kb, False)
            kbd2_scr[j] = jnp.concatenate([left, right], axis=0)
            dkt_scr[j] = jnp.zeros((LANES, 2 * TK), F32)
            dvt_scr[j] = jnp.zeros((LANES, 2 * TK), F32)
            return 0

        lax.fori_loop(0, nk, prep, 0)

        def block(j, qb, qt, dob, dot_, totb, st, mask):
            f0, f1, p0, p1, dqacc = st
            lb, l1 = _att_logits(qb, kbd_scr[j])
            if mask is not None:
                l1 = jnp.where(mask, l1, 0.0)
            (s0, s1), (r0, r1) = _pair_sums(l1, upto)
            att = jnp.exp(lb + (totb - jnp.concatenate([s0 + f0, s1 + f1], axis=1)))
            if mask is not None:
                att = jnp.where(mask, att, 0.0)
            pw = att * _dot(dob, vtbd_scr[j])
            (e0, e1), (t0, t1) = _pair_sums(pw, before)
            dz = pw - jnp.exp(lb) * (pw + jnp.concatenate([e0 + p0, e1 + p1], axis=1))
            if mask is not None:
                dz = jnp.where(mask, dz, 0.0)
            dzb = dz.astype(BF16)
            dkt_scr[j] += _dot(qt, dzb)
            dvt_scr[j] += _dot(dot_, att.astype(BF16))
            return f0 + r0, f1 + r1, p0 + t0, p1 + t1, dqacc + _dot(dzb, kbd2_scr[j])

        def qloop(qi, _):
            q0 = pl.multiple_of(qi * TQ, TQ)
            qb = _scaled_q(q_ref, q0)
            qt = qb.T
            do32 = do_ref[pl.ds(q0, TQ), :]
            dob = do32.astype(BF16)
            dot_ = dob.T
            totb = tot_ref[pl.ds(q0, TQ), :]
            zero = jnp.zeros((TQ, TK), F32)
            st = (zero, zero, zero, zero, jnp.zeros((TQ, LANES), F32))

            k0 = kmin_ref[pl.program_id(0), pl.program_id(1), qi]

            def tile(j, lo, hi, st, masked):
                new = block(j, qb[lo:hi], qt[:, lo:hi], dob[lo:hi], dot_[:, lo:hi], totb[lo:hi],
                            tuple(x[lo:hi] for x in st), causal[:hi - lo] if masked else None)
                return _rows_update(st, new, lo, hi)

            def general():
                def kloop(it, st):
                    j = k0 + 2 * it
                    return block(j + 1, qb, qt, dob, dot_, totb, block(j, qb, qt, dob, dot_, totb, st, None), None)

                out = lax.fori_loop(0, ((TQ // TK) * qi - k0) // 2, kloop, st)
                for jj in range(TQ // TK):
                    out = tile((TQ // TK) * qi + jj, TK * jj, TQ, out, True)
                return out

            def short():
                out = st
                for jr, lo, hi in reversed(_band_tiles()):
                    out = tile((TQ // TK) * qi + jr, lo, hi, out, jr >= 0)
                return out

            st = lax.cond(k0 < 0, short, general)
            dq_ref[pl.ds(q0, TQ), :] = (st[4] * ATT_SCALE).astype(BF16)
            return 0

        lax.fori_loop(0, nq, qloop, 0)

        def finish(j, _):
            k0 = pl.multiple_of(j * TK, TK)
            head0 = lax.broadcasted_iota(jnp.int32, (LANES, TK), 0) < DH
            for src, dst in ((dkt_scr, dk_ref), (dvt_scr, dv_ref)):
                acc = src[j]
                dst[pl.ds(k0, TK), :] = jnp.where(head0, acc[:, :TK], acc[:, TK:]).astype(BF16).T
            return 0

        lax.fori_loop(0, nk, finish, 0)

    blk = pl.BlockSpec((seq, LANES), lambda e, p: (e, p))
    grad = jax.ShapeDtypeStruct((t, SB_WIDTH), BF16)
    return pl.pallas_call(
        kern, grid=(ne, SB_WIDTH // LANES),
        in_specs=_qkv_specs(seq) + [blk, pl.BlockSpec((seq, 2 * TK), lambda e, p: (e, p)),
                                    pl.BlockSpec(memory_space=pltpu.SMEM)],
        out_specs=[blk, blk, blk], out_shape=[grad, grad, grad],
        scratch_shapes=[pltpu.VMEM((nk, LANES, 2 * TK), BF16), pltpu.VMEM((nk, LANES, 2 * TK), BF16),
                        pltpu.VMEM((nk, 2 * TK, LANES), BF16), pltpu.VMEM((nk, LANES, 2 * TK), F32),
                        pltpu.VMEM((nk, LANES, 2 * TK), F32)],
        compiler_params=_cp(("parallel", "parallel")), name="attn_bwd",
    )(qkv, qkv, qkv, do, tot, kmin)


def _lru_bwd(xl, h, dy, conv_w, conv_b, wbd, ba, bx, lam, seq, comm=None):
    t = xl.shape[0]
    tc = min(512, seq)
    nc = seq // tc
    nb = tc // SUBLANES


    def body(ins, outs, scr):
        u_ref, g_ref, up_ref, h_ref, hp_ref, dy_ref, cw_ref, cb_ref, wbd_ref, ba_ref, bx_ref, lam_ref = ins
        (dxl_ref, small_ref, dwbd_ref), (lnext_ref, anext_ref, dcnext_ref) = outs, scr
        e, ci = pl.program_id(0), pl.program_id(1)
        first = ci == nc - 1

        @pl.when((e == 0) & (ci == 0))
        def _():
            small_ref[...] = jnp.zeros_like(small_ref)
            dwbd_ref[...] = jnp.zeros_like(dwbd_ref)

        @pl.when(ci == 0)
        def _():
            lnext_ref[...] = jnp.zeros_like(lnext_ref)
            anext_ref[...] = jnp.zeros_like(anext_ref)
            dcnext_ref[...] = jnp.zeros_like(dcnext_ref)

        u, g = u_ref[...], g_ref[...]
        keep = jnp.where(first, 0.0, 1.0)
        taps = _conv_taps(keep * up_ref[...], u)
        c = cb_ref[...]
        for k in range(CONV_WIDTH):
            c = c + taps[k] * cw_ref[k:k + 1, :]
        lam = lam_ref[...]
        sp = _softplus(-lam)
        r, i, a, mult, inv_mult = _lru_gates(c, wbd_ref, ba_ref[...], bx_ref[...], sp)
        gel, th = _gelu(g)
        dyv, hv = dy_ref[...], h_ref[...]
        dg = dyv * hv * _gelu_grad(g, th)

        aa, bb = _scan_rev(_shift_up(a, 1, anext_ref[0:1, :]), dyv * gel)
        lt = bb + aa * lnext_ref[0:1, :]
        lnext_ref[0:1, :] = _row_of(lt, 0)
        anext_ref[0:1, :] = _row_of(a, 0)

        hprev = _shift_down(hv, 1, keep * hp_ref[SUBLANES - 1:SUBLANES, :])
        da = lt * hprev
        dmult = lt * i * c
        di = lt * mult * c
        dc = lt * mult * i
        dla = da * a - dmult * (a * a) * inv_mult
        dga = dla * ((-LRU_C) * sp) * r * (1.0 - r)
        dgx = di * i * (1.0 - i)
        small_ref[7:8, :] += jnp.sum(dla * r, axis=0, keepdims=True) * (LRU_C * _sigmoid(-lam))
        small_ref[5:6, :] += jnp.sum(dga, axis=0, keepdims=True)
        small_ref[6:7, :] += jnp.sum(dgx, axis=0, keepdims=True)

        dcs = []
        for p in range(LRU_WIDTH // LANES):
            cols = slice(LANES * p, LANES * (p + 1))
            dgax = jnp.concatenate([dga[:, cols], dgx[:, cols]], axis=1).astype(BF16)
            dcs.append(_dot_nt(dgax, wbd_ref[p]))
            dwbd_ref[p] += _dot_tn(c[:, cols].astype(BF16), dgax)
        dc = dc + jnp.concatenate(dcs, axis=1)
        small_ref[4:5, :] += jnp.sum(dc, axis=0, keepdims=True)

        catd = jnp.concatenate([dc, dcnext_ref[...]], axis=0)
        du = dc * cw_ref[CONV_WIDTH - 1:CONV_WIDTH, :]
        for j in range(1, CONV_WIDTH):
            du = du + pltpu.roll(catd, tc + SUBLANES - j, 0)[:tc] * cw_ref[CONV_WIDTH - 1 - j:CONV_WIDTH - j, :]
        dcnext_ref[...] = dc[:SUBLANES]
        for k in range(CONV_WIDTH):
            small_ref[k:k + 1, :] += jnp.sum(dc * taps[k], axis=0, keepdims=True)
        dxl_ref[:, :LRU_WIDTH] = du.astype(BF16)
        dxl_ref[:, LRU_WIDTH:] = dg.astype(BF16)

    rev = lambda e, c: e * nc + (nc - 1 - c)
    chunk = lambda col: pl.BlockSpec((tc, LRU_WIDTH), lambda e, c: (rev(e, c), col))
    prev8 = pl.BlockSpec((SUBLANES, LRU_WIDTH), lambda e, c: (jnp.maximum(rev(e, c) * nb - 1, 0), 0))
    outs, got = _call(
        body, comm, grid=(t // seq, nc),
        in_specs=[chunk(0), chunk(1), prev8, chunk(0), prev8, chunk(0)] + _lru_param_specs(2),
        out_specs=[pl.BlockSpec((tc, 2 * LRU_WIDTH), lambda e, c: (rev(e, c), 0)),
                   pl.BlockSpec((SUBLANES, LRU_WIDTH), lambda e, c: (0, 0)),
                   pl.BlockSpec((LRU_WIDTH // LANES, LANES, 2 * LANES), lambda e, c: (0, 0, 0))],
        out_shape=[jax.ShapeDtypeStruct((t, 2 * LRU_WIDTH), BF16), jax.ShapeDtypeStruct((SUBLANES, LRU_WIDTH), F32),
                   jax.ShapeDtypeStruct((LRU_WIDTH // LANES, LANES, 2 * LANES), F32)],
        scratch_shapes=[pltpu.VMEM((SUBLANES, LRU_WIDTH), F32)] * 3,
        args=(xl, xl, xl, h, h, dy, conv_w, conv_b, wbd, ba, bx, lam), name="lru_bwd")
    return (*outs, got)


def _adam_math(w, g, m, v):
    m2 = ADAM_B1 * m + (1.0 - ADAM_B1) * g
    v2 = ADAM_B2 * v + (1.0 - ADAM_B2) * (g * g)
    m_hat = m2 / (1.0 - ADAM_B1 ** ADAM_STEP)
    v_hat = v2 / (1.0 - ADAM_B2 ** ADAM_STEP)
    return -ADAM_LR * (m_hat / (jnp.sqrt(v_hat) + ADAM_EPS) + ADAM_WD * w), m2, v2


def _adamw(w, g, m, v, name):
    rows, cols = w.shape
    tr = 256 if rows % 256 == 0 else rows

    def kern(w_ref, g_ref, m_ref, v_ref, d_ref, m2_ref, v2_ref):
        d_ref[...], m2_ref[...], v2_ref[...] = _adam_math(w_ref[...], g_ref[...], m_ref[...], v_ref[...])

    blk = pl.BlockSpec((tr, cols), lambda i: (i, 0))
    out = jax.ShapeDtypeStruct((rows, cols), F32)
    return pl.pallas_call(kern, grid=(rows // tr,), in_specs=[blk] * 4, out_specs=[blk] * 3, out_shape=[out] * 3,
                          compiler_params=_cp(("parallel",)), name=name)(w, g, m, v)


def _adamw_small(ws, gs, ms, vs):
    n = len(ws)

    def kern(*refs):
        for k in range(n):
            outs = _adam_math(refs[k][...], refs[n + k][...], refs[2 * n + k][...], refs[3 * n + k][...])
            for j in range(3):
                refs[(4 + j) * n + k][...] = outs[j]

    vm = pl.BlockSpec(memory_space=pltpu.VMEM)
    out = pl.pallas_call(kern, in_specs=[vm] * (4 * n), out_specs=[vm] * (3 * n),
                         out_shape=[jax.ShapeDtypeStruct(w.shape, F32) for w in ws] * 3, name="adamw_small")(*ws, *gs, *ms, *vs)
    return out[:n], out[n:2 * n], out[2 * n:]


def _comm_only(kind, arrays, name):
    return _call(lambda ins, outs, scr: None, (kind, arrays), grid=(1,), in_specs=[], out_specs=[], out_shape=[],
                 scratch_shapes=[], args=(), name=name)[1]


def _pair_add(g, got, core):
    _, rows, cols = g.shape
    half = rows // 2
    tr = min(256, half)
    nt = half // tr

    def kern(c_ref, g_ref, o_ref, out_ref):
        out_ref[...] = (g_ref[...] + o_ref[...]).astype(BF16)

    return pl.pallas_call(
        kern, grid_spec=pltpu.PrefetchScalarGridSpec(
            num_scalar_prefetch=1, grid=(N_CHIPS, nt),
            in_specs=[pl.BlockSpec((None, tr, cols), lambda j, i, c_ref: (j, c_ref[0] * nt + i, 0)),
                      pl.BlockSpec((None, tr, cols), lambda j, i, c_ref: (j, i, 0))],
            out_specs=pl.BlockSpec((None, tr, cols), lambda j, i, c_ref: (j, i, 0))),
        out_shape=jax.ShapeDtypeStruct((N_CHIPS, half, cols), BF16),
        compiler_params=_cp(("parallel", "parallel")), name="pair_add",
    )(core, g, got)


def _chip_add(part, got, place):
    _, half, cols = part.shape
    tr = min(256, half)
    nt = half // tr

    def kern(p_ref, part_ref, got_ref, out_ref):
        out_ref[...] = (part_ref[...].astype(F32) + got_ref[0].astype(F32) + got_ref[1].astype(F32)
                        + got_ref[2].astype(F32))

    return pl.pallas_call(
        kern, grid_spec=pltpu.PrefetchScalarGridSpec(
            num_scalar_prefetch=1, grid=(nt,),
            in_specs=[pl.BlockSpec((None, tr, cols), lambda i, p_ref: (p_ref[0], i, 0)),
                      pl.BlockSpec((3, tr, cols), lambda i, p_ref: (0, i, 0))],
            out_specs=pl.BlockSpec((tr, cols), lambda i, p_ref: (p_ref[1] * nt + i, 0))),
        out_shape=jax.ShapeDtypeStruct((2 * half, cols), F32),
        compiler_params=_cp(("parallel",)), name="chip_add",
    )(place, part, got)


def _finale(packed, fulls):
    rows, n = packed.shape[0], len(fulls)

    def kern(in_ref, *refs):
        ins, out_ref, outs = refs[:n], refs[n], refs[n + 1:2 * n + 1]
        slots, send_sems, recv_sems, join_send, join_recv = refs[2 * n + 1:]
        x, y, c = _place()
        mine = 4 * x + 2 * y + c
        copies = []
        for w in range(n):
            half = ins[w].shape[0] // 2
            rws = pl.ds(c * half, half)
            copies.append(pltpu.make_async_remote_copy(
                src_ref=ins[w].at[rws, :], dst_ref=outs[w].at[rws, :], send_sem=join_send.at[w],
                recv_sem=join_recv.at[w], device_id=(x, y, 1 - c), device_id_type=MESH))
        for k in range(1, N_DEV):
            peer = (x ^ (k >> 2), y ^ ((k >> 1) & 1), c ^ (k & 1))
            copies.append(pltpu.make_async_remote_copy(
                src_ref=in_ref, dst_ref=slots.at[mine], send_sem=send_sems.at[k - 1], recv_sem=recv_sems.at[k - 1],
                device_id=peer, device_id_type=MESH))
        for cp in copies:
            cp.start()
        slots[mine] = in_ref[...]
        for cp in copies:
            cp.wait()
        acc = slots[0]
        for sl in range(1, N_DEV):
            acc = acc + slots[sl]
        out_ref[...] = acc

    vm = pl.BlockSpec(memory_space=pltpu.VMEM)
    out = pl.pallas_call(
        kern, in_specs=[vm] + [ANY] * n, out_specs=[vm] + [ANY] * n,
        out_shape=[jax.ShapeDtypeStruct((rows, LANES), F32)] + [jax.ShapeDtypeStruct(f.shape, f.dtype) for f in fulls],
        input_output_aliases={w + 1: w + 1 for w in range(n)},
        scratch_shapes=[pltpu.VMEM((N_DEV, rows, LANES), F32), pltpu.SemaphoreType.DMA((N_DEV - 1,)),
                        pltpu.SemaphoreType.DMA((N_DEV - 1,)), pltpu.SemaphoreType.DMA((n,)), pltpu.SemaphoreType.DMA((n,))],
        name="finale",
    )(packed, *fulls)
    return out[0], list(out[1:])


SMALL = ["norm1_g", "conv_w", "conv_b", "lru_w_a", "lru_b_a", "lru_w_x", "lru_b_x", "lru_lambda", "lru_out_g", "sb_out_g",
         "norm2_g", "final_g"]
BIG = ["w_in", "w_out", "w_up", "w_down"]
WEIGHTS = ["norm1_g", "w_in", "conv_w", "conv_b", "lru_w_a", "lru_b_a", "lru_w_x", "lru_b_x", "lru_lambda", "lru_out_g",
           "sb_out_g", "w_out", "norm2_g", "w_up", "w_down", "final_g"]


def _pack(arrays):
    flat = []
    for a in arrays:
        a = a.reshape(-1).astype(F32)
        flat.append(jnp.pad(a, (0, (-a.shape[0]) % LANES)))
    v = jnp.concatenate(flat)
    v = jnp.pad(v, (0, (-v.shape[0]) % (LANES * SUBLANES)))
    return v.reshape(-1, LANES)


def _unpack(packed, shapes):
    v, out, off = packed.reshape(-1), [], 0
    for shp in shapes:
        size = math.prod(shp)
        out.append(v[off:off + size].reshape(shp))
        off += size + (-size) % LANES
    return out


def _blockdiag_pairs(w):
    w = w.reshape(4, 2, DH, DH)
    z = jnp.zeros((4, DH, DH), w.dtype)
    return jnp.concatenate([jnp.concatenate([w[:, 0], z], axis=2), jnp.concatenate([z, w[:, 1]], axis=2)], axis=1)


def _blockdiag_unpairs(wbd):
    return jnp.stack([wbd[:, :DH, :DH], wbd[:, DH:, DH:]], axis=1).reshape(8, DH, DH)


def _full_cols(g):
    return jnp.transpose(g, (1, 0, 2)).reshape(g.shape[1], N_CHIPS * g.shape[2])


def _local_step(x2, tgt, seq, norm1_g, w_in, conv_w, conv_b, w_a, b_a, w_x, b_x, lru_lambda, lru_out_g, sb_out_g, rest,
                norm2_g, final_g, place=None):
    alone = place is None
    wbd = jnp.concatenate([_blockdiag_pairs(w_a), _blockdiag_pairs(w_x)], axis=2).astype(BF16)
    ba, bx = b_a.reshape(1, LRU_WIDTH), b_x.reshape(1, LRU_WIDTH)
    gf = final_g.reshape(1, D_MODEL)

    xn, got = _norm1(x2, norm1_g, None if alone else ("gather2", [w_in]))
    w_in_f = w_in if alone else _full_cols(got[0])
    xl, qkv, got = _inproj(xn, w_in_f, None if alone else ("gather", [conv_w]))
    conv_w_f = conv_w if alone else _full_cols(got[0])
    h, y_lru = _lru_fwd(xl, conv_w_f, conv_b, wbd, ba, bx, lru_lambda, seq)
    o, tot, kmin, got = _attn_fwd(qkv, seq, None if alone else ("gather2", rest))
    w_out_f, w_up_f, w_down_f = rest if alone else (
        got[0].reshape(D_MODEL, D_MODEL), _full_cols(got[1]), got[2].reshape(D_FF, D_MODEL))
    h1, mix = _outproj(y_lru, o, x2, lru_out_g, sb_out_g, w_out_f)
    hn, up, u2, dh2, dh2b, loss_part, d_final = _mlp_loss(h1, norm2_g, w_up_f, w_down_f, tgt, gf)

    dpre = _mlp_bwd_pre(dh2b, w_down_f, up)
    g_w_down = _matmul(u2, dh2b, "tn", F32, "dw_down", 1024, 1024, 1024).reshape(N_CHIPS, D_FF // N_CHIPS, D_MODEL)
    g_w_up = _matmul(hn, dpre, "tn", F32, "dw_up", 1024, D_FF // N_CHIPS, 1024, split_cols=True)
    dh1, dh1b, d_norm2, _ = _proj_bwd_norm([dpre], w_up_f, h1, norm2_g, dh2, "mlp_bwd_in", bf16_copy=True)
    g_w_out = _matmul(mix, dh1b, "tn", F32, "dw_out", 1024, 1024, 2048).reshape(N_CHIPS, D_MODEL // N_CHIPS, D_MODEL)
    late = [g_w_out, g_w_up, g_w_down]
    dy_lru, do, d_ga, d_gb, swapped = _outproj_bwd(dh1b, w_out_f, y_lru, o, lru_out_g, sb_out_g,
                                                   None if alone else ("swap", late))
    parts = None if alone else [_pair_add(g, r, place[1:]) for g, r in zip(late, swapped)]
    dq, dk, dv = _attn_bwd(qkv, do, tot, kmin, seq)
    dxl, lru_small, d_wbd, got = _lru_bwd(xl, h, dy_lru, conv_w_f, conv_b, wbd, ba, bx, lru_lambda, seq,
                                          None if alone else ("exchange", parts))
    if not alone:
        late = [_chip_add(p, r, place) for p, r in zip(parts, got)]
    dproj = jnp.concatenate([dxl, dq, dk, dv], axis=1)
    g_w_in = _matmul(xn, dproj, "tn", F32, "dw_in", 1024, IN_COLS // N_CHIPS, 2048, split_cols=True)
    part = None if alone else _pair_add(g_w_in, _comm_only("swap", [g_w_in], "pair_swap")[0], place[1:])
    dx, _, d_norm1, got = _proj_bwd_norm([dxl, dq, dk, dv], w_in_f, x2, norm1_g, dh1, "inproj_bwd",
                                         None if alone else ("exchange", [part]))
    if not alone:
        g_w_in = _chip_add(part, got[0], place)
    small_parts = {
        "norm1_g": d_norm1, "conv_w": lru_small[:CONV_WIDTH], "conv_b": lru_small[4:5],
        "lru_w_a": _blockdiag_unpairs(d_wbd[:, :, :LANES]), "lru_b_a": lru_small[5:6],
        "lru_w_x": _blockdiag_unpairs(d_wbd[:, :, LANES:]), "lru_b_x": lru_small[6:7], "lru_lambda": lru_small[7:8],
        "lru_out_g": d_ga, "sb_out_g": d_gb, "norm2_g": d_norm2, "final_g": d_final,
    }
    return loss_part, dx, [g_w_in] + late, small_parts


def kernel(x, norm1_g, w_in, conv_w, conv_b, lru_w_a, lru_b_a, lru_w_x, lru_b_x, lru_lambda, lru_out_g, sb_out_g, w_out, norm2_g, w_up, w_down, final_g, loss_target, m_norm1_g, m_w_in, m_conv_w, m_conv_b, m_lru_w_a, m_lru_b_a, m_lru_w_x, m_lru_b_x, m_lru_lambda, m_lru_out_g, m_sb_out_g, m_w_out, m_norm2_g, m_w_up, m_w_down, m_final_g, v_norm1_g, v_w_in, v_conv_w, v_conv_b, v_lru_w_a, v_lru_b_a, v_lru_w_x, v_lru_b_x, v_lru_lambda, v_lru_out_g, v_sb_out_g, v_w_out, v_norm2_g, v_w_up, v_w_down, v_final_g):
    given = dict(locals())
    ne, seq, _ = x.shape
    t = ne * seq
    xi, yi, ci = _place()
    place = jnp.stack([2 * xi + yi, ci]).astype(jnp.int32)

    loss_part, dx, halves, small_parts = _local_step(
        x.reshape(t, D_MODEL), loss_target.reshape(t, D_MODEL), seq, norm1_g, w_in[0].astype(BF16), conv_w[0], conv_b,
        lru_w_a[0], lru_b_a, lru_w_x[0], lru_b_x, lru_lambda, lru_out_g, sb_out_g,
        [w_out[0].astype(BF16), w_up[0].astype(BF16), w_down[0].astype(BF16)], norm2_g, final_g, place)

    full_shapes = {n: ((CONV_WIDTH, LRU_WIDTH) if n == "conv_w" else given[n].shape) for n in SMALL}
    red, fulls = _finale(_pack([small_parts[n] for n in SMALL] + [loss_part]), halves)
    red_list = _unpack(red, [full_shapes[n] for n in SMALL] + [(1, LANES)])
    grads = dict(zip(SMALL, red_list[:-1]))
    loss = red_list[-1][0, 0]
    grads["conv_w"] = lax.dynamic_slice_in_dim(grads["conv_w"], place[0] * (LRU_WIDTH // N_CHIPS), LRU_WIDTH // N_CHIPS,
                                               axis=1).reshape(conv_w.shape)
    for n, full in zip(BIG, fulls):
        grads[n] = full.reshape(given[n].shape)

    delta, new_m, new_v = {}, {}, {}
    for n in BIG:
        shp = given[n].shape
        d, m2, v2 = _adamw(given[n][0], grads[n][0], given["m_" + n][0], given["v_" + n][0], "adamw_" + n)
        delta[n], new_m[n], new_v[n] = d.reshape(shp), m2.reshape(shp), v2.reshape(shp)
    as2d = lambda a: a.reshape(-1, a.shape[-1])
    ds, m2s, v2s = _adamw_small([as2d(given[n]) for n in SMALL], [as2d(grads[n]) for n in SMALL],
                                [as2d(given["m_" + n]) for n in SMALL], [as2d(given["v_" + n]) for n in SMALL])
    for n, dd, mm, vv in zip(SMALL, ds, m2s, v2s):
        shp = given[n].shape
        delta[n], new_m[n], new_v[n] = dd.reshape(shp), mm.reshape(shp), vv.reshape(shp)

    return (loss, dx.reshape(x.shape), *[grads[n] for n in WEIGHTS], *[delta[n] for n in WEIGHTS],
            *[new_m[n] for n in WEIGHTS], *[new_v[n] for n in WEIGHTS])
```

```python
import functools
import math

import jax
import jax.numpy as jnp
from jax import lax
from jax.experimental import pallas as pl
from jax.experimental.pallas import tpu as pltpu

F32, BF16 = jnp.float32, jnp.bfloat16
MESH = pl.DeviceIdType.MESH

D_MODEL = 1024
LRU_WIDTH = 512
SB_WIDTH = 512
DH = 64
IN_COLS = 2 * LRU_WIDTH + 3 * SB_WIDTH
D_FF = 4 * D_MODEL
CONV_WIDTH = 4
LRU_C = 8.0
EPS = 1e-6
N_CHIPS = 4
N_DEV = 8
LANES = 128
SUBLANES = 8
ROW_TILE = 512
TQ = 512
TK = 128
ATT_SCALE = 1.0 / math.sqrt(DH)
SKIP_LOG = -105.0
BAND = 2
VMEM_LIMIT = 52 * 1024 * 1024
VMEM_LIMIT_BIG = 62 * 1024 * 1024

ADAM_LR, ADAM_B1, ADAM_B2, ADAM_EPS, ADAM_WD, ADAM_STEP = 0.001, 0.9, 0.999, 1e-08, 0.01, 10

_GELU_K = math.sqrt(2.0 / math.pi)
_GELU_C = 0.044715


def _cp(sem, vmem=VMEM_LIMIT):
    return pltpu.CompilerParams(dimension_semantics=sem, vmem_limit_bytes=vmem)


def _dot(a, b):
    return jnp.dot(a, b, preferred_element_type=F32)


def _dot_nt(a, b):
    return lax.dot_general(a, b, (((1,), (1,)), ((), ())), preferred_element_type=F32)


def _dot_tn(a, b):
    return lax.dot_general(a, b, (((0,), (0,)), ((), ())), preferred_element_type=F32)


def _rstd(x):
    return lax.rsqrt(jnp.mean(x * x, axis=-1, keepdims=True) + EPS)


def _rms_bwd(x, g, dy):
    r = _rstd(x)
    gd = g * dy
    dx = r * gd - x * (r * r * r) * jnp.mean(x * gd, axis=-1, keepdims=True)
    return dx, jnp.sum(dy * x * r, axis=0, keepdims=True)


def _sigmoid(x):
    return 0.5 * jnp.tanh(0.5 * x) + 0.5


def _softplus(x):
    return jnp.maximum(x, 0.0) + jnp.log(1.0 + jnp.exp(-jnp.abs(x)))


def _neg_expm1(x, ex):
    series = -x * (1.0 + x * (0.5 + x * (1.0 / 6.0)))
    return jnp.where(x > -2.0 ** -7, series, 1.0 - ex)


def _gelu(g):
    t = jnp.tanh(_GELU_K * (g + _GELU_C * g * g * g))
    return 0.5 * g * (1.0 + t), t


def _gelu_grad(g, t):
    return 0.5 * (1.0 + t) + 0.5 * g * (1.0 - t * t) * _GELU_K * (1.0 + 3.0 * _GELU_C * g * g)


def _rows(shape):
    return lax.broadcasted_iota(jnp.int32, shape, 0)


def _shift_down(x, s, fill):
    n, c = x.shape
    if s % SUBLANES == 0:
        return jnp.concatenate([jnp.broadcast_to(jnp.asarray(fill, x.dtype), (s, c)), x[:n - s]], axis=0)
    return jnp.where(_rows(x.shape) >= s, pltpu.roll(x, s, 0), fill)


def _shift_up(x, s, fill):
    n, c = x.shape
    if s % SUBLANES == 0:
        return jnp.concatenate([x[s:], jnp.broadcast_to(jnp.asarray(fill, x.dtype), (s, c))], axis=0)
    return jnp.where(_rows(x.shape) < n - s, pltpu.roll(x, n - s, 0), fill)


def _row_of(x, idx):
    return jnp.sum(jnp.where(_rows(x.shape) == idx, x, 0.0), axis=0, keepdims=True)


def _matmul(a, b, dims, out_dtype, name, tm, tn, tk, split_cols=False):
    if dims == "nn":
        (m, kk), n, dot = a.shape, b.shape[1], _dot
    elif dims == "nt":
        (m, kk), n, dot = a.shape, b.shape[0], _dot_nt
    else:
        (kk, m), n, dot = a.shape, b.shape[1], _dot_tn
    tm, tn, tk = min(tm, m), min(tn, n), min(tk, kk)
    assert m % tm == 0 and n % tn == 0 and kk % tk == 0, (name, m, n, kk)
    nk = kk // tk

    def kern(a_ref, b_ref, o_ref, acc_ref):
        k = pl.program_id(2)

        @pl.when(k == 0)
        def _():
            acc_ref[...] = jnp.zeros_like(acc_ref)

        acc_ref[...] += dot(a_ref[...].astype(BF16), b_ref[...].astype(BF16))

        @pl.when(k == nk - 1)
        def _():
            o_ref[...] = acc_ref[...].astype(o_ref.dtype)

    if split_cols:
        out_shape = jax.ShapeDtypeStruct((n // tn, m, tn), out_dtype)
        o_spec = pl.BlockSpec((None, tm, tn), lambda i, j, k: (j, i, 0))
    else:
        out_shape = jax.ShapeDtypeStruct((m, n), out_dtype)
        o_spec = pl.BlockSpec((tm, tn), lambda i, j, k: (i, j))
    if dims == "nn":
        a_spec = pl.BlockSpec((tm, tk), lambda i, j, k: (i, k))
        b_spec = pl.BlockSpec((tk, tn), lambda i, j, k: (k, j))
    elif dims == "nt":
        a_spec = pl.BlockSpec((tm, tk), lambda i, j, k: (i, k))
        b_spec = pl.BlockSpec((tn, tk), lambda i, j, k: (j, k))
    else:
        a_spec = pl.BlockSpec((tk, tm), lambda i, j, k: (k, i))
        b_spec = pl.BlockSpec((tk, tn), lambda i, j, k: (k, j))
    return pl.pallas_call(
        kern, grid=(m // tm, n // tn, nk), in_specs=[a_spec, b_spec], out_specs=o_spec, out_shape=out_shape,
        scratch_shapes=[pltpu.VMEM((tm, tn), F32)], compiler_params=_cp(("parallel", "parallel", "arbitrary")), name=name,
    )(a, b)


ANY = pl.BlockSpec(memory_space=pl.ANY)


def _place():
    return lax.axis_index("x"), lax.axis_index("y"), lax.axis_index("c")


def _other_chips(x, y):
    return [(1 - x, y), (x, 1 - y), (1 - x, 1 - y)]


def _own_slab(shard, gathered, send_sem, recv_sem):
    x, y, c = _place()
    return pltpu.make_async_remote_copy(src_ref=shard, dst_ref=gathered.at[2 * x + y], send_sem=send_sem, recv_sem=recv_sem,
                                        device_id=(x, y, 1 - c), device_id_type=MESH)


def _gather_copies(ins, outs, send_sems, recv_sems, own_send, own_recv):
    x, y, c = _place()
    mine = 2 * x + y
    copies = []
    for w in range(len(ins)):
        copies.append(_own_slab(ins[w], outs[w], own_send.at[w], own_recv.at[w]))
        for k, chip in enumerate(_other_chips(x, y)):
            copies.append(pltpu.make_async_remote_copy(
                src_ref=ins[w], dst_ref=outs[w].at[mine], send_sem=send_sems.at[3 * w + k],
                recv_sem=recv_sems.at[3 * w + k], device_id=(*chip, c), device_id_type=MESH))
    return copies


def _gather_shapes(shards):
    return ([jax.ShapeDtypeStruct((N_CHIPS,) + s.shape, s.dtype) for s in shards],
            [pltpu.SemaphoreType.DMA((3 * len(shards),)), pltpu.SemaphoreType.DMA((3 * len(shards),)),
             pltpu.SemaphoreType.DMA((len(shards),)), pltpu.SemaphoreType.DMA((len(shards),))])


def _exchange_copies(ins, outs, send_sems, recv_sems):
    x, y, c = _place()
    copies = []
    for w in range(len(ins)):
        for k, chip in enumerate(_other_chips(x, y)):
            copies.append(pltpu.make_async_remote_copy(
                src_ref=ins[w].at[2 * chip[0] + chip[1]], dst_ref=outs[w].at[k], send_sem=send_sems.at[3 * w + k],
                recv_sem=recv_sems.at[3 * w + k], device_id=(*chip, c), device_id_type=MESH))
    return copies


def _exchange_shapes(parts):
    return ([jax.ShapeDtypeStruct((3,) + p.shape[1:], p.dtype) for p in parts],
            [pltpu.SemaphoreType.DMA((3 * len(parts),)), pltpu.SemaphoreType.DMA((3 * len(parts),))])


def _swap_copies(ins, outs, send_sems, recv_sems):
    x, y, c = _place()
    copies = []
    for w in range(len(ins)):
        half = ins[w].shape[1] // 2
        copies.append(pltpu.make_async_remote_copy(
            src_ref=ins[w].at[:, pl.ds((1 - c) * half, half), :], dst_ref=outs[w], send_sem=send_sems.at[w],
            recv_sem=recv_sems.at[w], device_id=(x, y, 1 - c), device_id_type=MESH))
    return copies


def _swap_shapes(grads):
    return ([jax.ShapeDtypeStruct((g.shape[0], g.shape[1] // 2, g.shape[2]), g.dtype) for g in grads],
            [pltpu.SemaphoreType.DMA((len(grads),)), pltpu.SemaphoreType.DMA((len(grads),))])


def _gather2_copies(ins, outs, send_sems, recv_sems, own_send, own_recv, fwd_send, fwd_recv):
    x, y, c = _place()
    mine = 2 * x + y
    copies = []
    for w in range(len(ins)):
        half = ins[w].shape[0] // 2
        rows = pl.ds(c * half, half)
        copies.append(_own_slab(ins[w], outs[w], own_send.at[w], own_recv.at[w]))
        for k, chip in enumerate(_other_chips(x, y)):
            copies.append(pltpu.make_async_remote_copy(
                src_ref=ins[w].at[rows, :], dst_ref=outs[w].at[mine, rows, :], send_sem=send_sems.at[3 * w + k],
                recv_sem=recv_sems.at[3 * w + k], device_id=(*chip, c), device_id_type=MESH))
    return copies


def _gather2_forward(ins, outs, send_sems, recv_sems, own_send, own_recv, fwd_send, fwd_recv):
    x, y, c = _place()
    copies = []
    for w in range(len(ins)):
        half = ins[w].shape[0] // 2
        rows = pl.ds(c * half, half)
        for k, chip in enumerate(_other_chips(x, y)):
            slab = outs[w].at[2 * chip[0] + chip[1], rows, :]
            copies.append(pltpu.make_async_remote_copy(
                src_ref=slab, dst_ref=slab, send_sem=fwd_send.at[3 * w + k], recv_sem=fwd_recv.at[3 * w + k],
                device_id=(x, y, 1 - c), device_id_type=MESH))
    return copies


def _gather2_shapes(shards):
    n = len(shards)
    return ([jax.ShapeDtypeStruct((N_CHIPS,) + s.shape, s.dtype) for s in shards],
            [pltpu.SemaphoreType.DMA((3 * n,)), pltpu.SemaphoreType.DMA((3 * n,)), pltpu.SemaphoreType.DMA((n,)),
             pltpu.SemaphoreType.DMA((n,)), pltpu.SemaphoreType.DMA((3 * n,)), pltpu.SemaphoreType.DMA((3 * n,))])


COMM = {"gather": (_gather_copies, _gather_shapes, None), "exchange": (_exchange_copies, _exchange_shapes, None),
        "swap": (_swap_copies, _swap_shapes, None), "gather2": (_gather2_copies, _gather2_shapes, _gather2_forward)}


def _call(body, comm, *, grid, in_specs, out_specs, out_shape, scratch_shapes, args, name):
    ni, no, ns = len(in_specs), len(out_specs), len(scratch_shapes)
    arrays = list(comm[1]) if comm else []
    nc = len(arrays)
    first_fn, shapes_fn, second_fn = COMM[comm[0]] if comm else (None, None, None)
    c_shapes, c_sems = shapes_fn(arrays) if comm else ([], [])

    def kern(*refs):
        ins, cin, outs = refs[:ni], refs[ni:ni + nc], refs[ni + nc:ni + nc + no]
        rest = refs[ni + nc + no:]
        cout, scr, sems = rest[:nc], rest[nc:nc + ns], rest[nc + ns:]
        ids = [pl.program_id(d) for d in range(len(grid))]
        if nc:
            @pl.when(functools.reduce(lambda a, b: a & b, [i == 0 for i in ids]))
            def _():
                for cp in first_fn(cin, cout, *sems):
                    cp.start()

        body(ins, outs, scr)
        if nc:
            @pl.when(functools.reduce(lambda a, b: a & b, [i == g - 1 for i, g in zip(ids, grid)]))
            def _():
                for cp in first_fn(cin, cout, *sems):
                    cp.wait()
                if second_fn is not None:
                    more = second_fn(cin, cout, *sems)
                    for cp in more:
                        cp.start()
                    for cp in more:
                        cp.wait()

    out = pl.pallas_call(
        kern, grid=grid, in_specs=list(in_specs) + [ANY] * nc, out_specs=list(out_specs) + [ANY] * nc,
        out_shape=list(out_shape) + c_shapes, scratch_shapes=list(scratch_shapes) + c_sems,
        compiler_params=_cp(("arbitrary",) * len(grid)), name=name,
    )(*args, *arrays)
    return list(out[:no]), list(out[no:])


def _resident(shape):
    return pl.BlockSpec(shape, lambda *_: (0,) * len(shape), pipeline_mode=pl.Buffered(1))


def _norm1(x, g1, comm):
    t = x.shape[0]
    tm = min(1024, t)

    def body(ins, outs, _):
        xv = ins[0][...]
        outs[0][...] = (xv * _rstd(xv) * ins[1][...]).astype(BF16)

    row = pl.BlockSpec((tm, D_MODEL), lambda i: (i, 0))
    (xn,), got = _call(body, comm, grid=(t // tm,), in_specs=[row, pl.BlockSpec((1, D_MODEL), lambda i: (0, 0))],
                       out_specs=[row], out_shape=[jax.ShapeDtypeStruct((t, D_MODEL), BF16)], scratch_shapes=[],
                       args=(x, g1), name="norm1")
    return xn, got


def _inproj(xn, w_in, comm=None):
    t = xn.shape[0]
    tm = min(ROW_TILE, t)

    def body(ins, outs, _):
        xn_v = ins[0][...]
        outs[0][...] = _dot(xn_v, ins[1][:, : 2 * LRU_WIDTH])
        outs[1][...] = _dot(xn_v, ins[1][:, 2 * LRU_WIDTH:]).astype(BF16)

    row = lambda c: pl.BlockSpec((tm, c), lambda i: (i, 0))
    (xl, qkv), got = _call(
        body, comm, grid=(t // tm,), in_specs=[row(D_MODEL), _resident((D_MODEL, IN_COLS))],
        out_specs=[row(2 * LRU_WIDTH), row(3 * SB_WIDTH)],
        out_shape=[jax.ShapeDtypeStruct((t, 2 * LRU_WIDTH), F32), jax.ShapeDtypeStruct((t, 3 * SB_WIDTH), BF16)],
        scratch_shapes=[], args=(xn, w_in), name="inproj")
    return xl, qkv, got


def _conv_taps(hist, u):
    cat = jnp.concatenate([hist, u], axis=0)
    return [pltpu.roll(cat, CONV_WIDTH - 1 - k, 0)[SUBLANES:] for k in range(CONV_WIDTH - 1)] + [u]


def _lru_gates(c, wbd_ref, ba, bx, sp):
    gas, gxs = [], []
    for p in range(LRU_WIDTH // LANES):
        gax = _dot(c[:, LANES * p: LANES * (p + 1)].astype(BF16), wbd_ref[p])
        gas.append(gax[:, :LANES])
        gxs.append(gax[:, LANES:])
    r = _sigmoid(jnp.concatenate(gas, axis=1) + ba)
    i = _sigmoid(jnp.concatenate(gxs, axis=1) + bx)
    la = (-LRU_C) * r * sp
    a = jnp.exp(la)
    e2 = _neg_expm1(2.0 * la, a * a)
    inv_mult = lax.rsqrt(jnp.maximum(e2, 1e-30))
    return r, i, a, e2 * inv_mult, inv_mult


def _scan_fwd(a, b):
    s = 1
    while s < a.shape[0]:
        b = b + a * _shift_down(b, s, 0.0)
        a = a * _shift_down(a, s, 1.0)
        s *= 2
    return a, b


def _scan_rev(a, b):
    s = 1
    while s < a.shape[0]:
        b = b + a * _shift_up(b, s, 0.0)
        a = a * _shift_up(a, s, 1.0)
        s *= 2
    return a, b


def _lru_param_specs(grid_rank):
    z2 = (lambda e, c: (0, 0)) if grid_rank == 2 else None
    return [
        pl.BlockSpec((CONV_WIDTH, LRU_WIDTH), z2), pl.BlockSpec((1, LRU_WIDTH), z2),
        pl.BlockSpec((LRU_WIDTH // LANES, LANES, 2 * LANES), lambda e, c: (0, 0, 0)),
        pl.BlockSpec((1, LRU_WIDTH), z2), pl.BlockSpec((1, LRU_WIDTH), z2), pl.BlockSpec((1, LRU_WIDTH), z2),
    ]


def _lru_fwd(xl, conv_w, conv_b, wbd, ba, bx, lam, seq):
    t = xl.shape[0]
    tc = min(512, seq)
    nc = seq // tc

    def kern(u_ref, g_ref, cw_ref, cb_ref, wbd_ref, ba_ref, bx_ref, lam_ref, h_ref, y_ref, hist_ref, hcar_ref):
        @pl.when(pl.program_id(1) == 0)
        def _():
            hist_ref[...] = jnp.zeros_like(hist_ref)
            hcar_ref[...] = jnp.zeros_like(hcar_ref)

        u = u_ref[...]
        taps = _conv_taps(hist_ref[...], u)
        hist_ref[...] = u_ref[tc - SUBLANES:, :]
        c = cb_ref[...]
        for k in range(CONV_WIDTH):
            c = c + taps[k] * cw_ref[k:k + 1, :]
        sp = _softplus(-lam_ref[...])
        _, i, a, mult, _ = _lru_gates(c, wbd_ref, ba_ref[...], bx_ref[...], sp)
        aa, bb = _scan_fwd(a, mult * i * c)
        h = bb + aa * hcar_ref[0:1, :]
        h_ref[...] = h
        hcar_ref[0:1, :] = h_ref[tc - 1:tc, :]
        y_ref[...] = h * _gelu(g_ref[...])[0]

    chunk = lambda col: pl.BlockSpec((tc, LRU_WIDTH), lambda e, c: (e * nc + c, col))
    out = jax.ShapeDtypeStruct((t, LRU_WIDTH), F32)
    return pl.pallas_call(
        kern, grid=(t // seq, nc), in_specs=[chunk(0), chunk(1)] + _lru_param_specs(2),
        out_specs=[chunk(0), chunk(0)], out_shape=[out, out],
        scratch_shapes=[pltpu.VMEM((SUBLANES, LRU_WIDTH), F32), pltpu.VMEM((SUBLANES, LRU_WIDTH), F32)],
        compiler_params=_cp(("arbitrary", "arbitrary")), name="lru_fwd",
    )(xl, xl, conv_w, conv_b, wbd, ba, bx, lam)


def _att_consts():
    row = lax.broadcasted_iota(jnp.int32, (TQ, 2 * TK), 0)
    key = lax.broadcasted_iota(jnp.int32, (TQ, 2 * TK), 1) & (TK - 1)
    return key < row


def _sum_matrix(kind):
    j = lax.broadcasted_iota(jnp.int32, (2 * TK, 2 * TK), 0) & (TK - 1)
    s = lax.broadcasted_iota(jnp.int32, (2 * TK, 2 * TK), 1)
    pick = {"after": j > s, "upto": j <= s, "before": j < s}[kind]
    return jnp.where((s >= TK) | pick, 1.0, 0.0).astype(BF16)


def _hi_lo(x):
    hi = x.astype(BF16)
    return hi, (x - hi.astype(F32)).astype(BF16)


def _pair_sums(x, m):
    hi, lo = _hi_lo(x)
    out = []
    for hd in range(2):
        cols = slice(hd * TK, (hd + 1) * TK)
        out.append(_dot(jnp.concatenate([hi[:, cols], lo[:, cols]], axis=1), m))
    return [o[:, :TK] for o in out], [o[:, TK:] for o in out]


def _att_logits(qb, kbd):
    z = _dot(qb, kbd)
    lg = jnp.log(1.0 + jnp.exp(-jnp.abs(z)))
    lb = jnp.minimum(z, 0.0) - lg
    return lb, lb - z


def _head_diag(x, rows_first):
    n = x.shape[0] if rows_first else x.shape[1]
    idx = lax.broadcasted_iota(jnp.int32, x.shape, 0 if rows_first else 1)
    return jnp.where(idx < n // 2, x, 0), jnp.where(idx >= n // 2, x, 0)


def _band_tiles():
    nd = TQ // TK
    return [(jr, TK * max(jr, 0), TK * min(jr + BAND + 1, nd)) for jr in range(nd - 1, -BAND - 1, -1)]


def _rows_update(st, new, lo, hi):
    def one(x, y):
        pieces = ([x[:lo]] if lo else []) + [y] + ([x[hi:]] if hi < x.shape[0] else [])
        return pieces[0] if len(pieces) == 1 else jnp.concatenate(pieces, axis=0)
    return tuple(one(x, y) for x, y in zip(st, new))


def _scaled_q(q_ref, q0):
    return (q_ref[pl.ds(q0, TQ), :].astype(F32) * ATT_SCALE).astype(BF16)


def _qkv_specs(seq):
    n = SB_WIDTH // LANES
    return [pl.BlockSpec((seq, LANES), lambda e, p, off=off: (e, off * n + p)) for off in range(3)]


def _attn_fwd(qkv, seq, comm=None):
    t = qkv.shape[0]
    ne, nq, nk = t // seq, seq // TQ, seq // TK

    def body(ins, outs, scr):
        (q_ref, k_ref, v_ref), (o_ref, tot_ref, kmin_ref), (kbd_scr, vbd_scr) = ins, outs, scr
        causal = _att_consts()
        after = _sum_matrix("after")

        def prep(j, _):
            k0 = pl.multiple_of(j * TK, TK)
            top, bot = _head_diag(k_ref[pl.ds(k0, TK), :].T, True)
            kbd_scr[j] = jnp.concatenate([top, bot], axis=1)
            left, right = _head_diag(v_ref[pl.ds(k0, TK), :], False)
            vbd_scr[j] = jnp.concatenate([left, right], axis=0)
            return 0

        lax.fori_loop(0, nk, prep, 0)

        def block(j, qb, st, mask):
            c0, c1, oacc = st
            lb, l1 = _att_logits(qb, kbd_scr[j])
            if mask is not None:
                l1 = jnp.where(mask, l1, 0.0)
            (s0, s1), (r0, r1) = _pair_sums(l1, after)
            att = jnp.exp(lb + jnp.concatenate([s0 + c0, s1 + c1], axis=1))
            if mask is not None:
                att = jnp.where(mask, att, 0.0)
            return c0 + r0, c1 + r1, oacc + _dot(att.astype(BF16), vbd_scr[j])

        def general(qi, qb, st):
            for jj in reversed(range(TQ // TK)):
                lo = TK * jj
                new = block((TQ // TK) * qi + jj, qb[lo:], tuple(x[lo:] for x in st), causal[:TQ - lo])
                st = _rows_update(st, new, lo, TQ)

            npair = (TQ // TK // 2) * qi

            def more(its):
                return (its[0] < npair) & (jnp.max(jnp.maximum(its[1], its[2])) > SKIP_LOG)

            def kloop(its):
                j = 2 * (npair - its[0]) - 1
                return (its[0] + 1,) + block(j - 1, qb, block(j, qb, its[1:], None), None)

            done, c0, c1, oacc = lax.while_loop(more, kloop, (jnp.int32(0),) + st)
            return c0, c1, oacc, 2 * (npair - done)

        def short(qi, qb, st):
            for jr, lo, hi in _band_tiles():
                new = block((TQ // TK) * qi + jr, qb[lo:hi], tuple(x[lo:hi] for x in st),
                            causal[:hi - lo] if jr >= 0 else None)
                st = _rows_update(st, new, lo, hi)
            return st

        def qloop(qi, _):
            q0 = pl.multiple_of(qi * TQ, TQ)
            qb = _scaled_q(q_ref, q0)
            zero = jnp.zeros((TQ, TK), F32)
            st = (zero, zero, jnp.zeros((TQ, LANES), F32))

            def try_short():
                c0, c1, oacc = short(qi, qb, st)
                return lax.cond(jnp.max(jnp.maximum(c0, c1)) <= SKIP_LOG, lambda: (c0, c1, oacc, jnp.int32(-1)),
                                lambda: general(qi, qb, st))

            c0, c1, oacc, first = lax.cond(qi > 0, try_short, lambda: general(qi, qb, st))
            o_ref[pl.ds(q0, TQ), :] = oacc
            tot_ref[pl.ds(q0, TQ), :] = jnp.concatenate([c0, c1], axis=1)
            kmin_ref[pl.program_id(0), pl.program_id(1), qi] = first
            return 0

        lax.fori_loop(0, nq, qloop, 0)

    (o, tot, kmin), got = _call(
        body, comm, grid=(ne, SB_WIDTH // LANES), in_specs=_qkv_specs(seq),
        out_specs=[pl.BlockSpec((seq, LANES), lambda e, p: (e, p)), pl.BlockSpec((seq, 2 * TK), lambda e, p: (e, p)),
                   pl.BlockSpec(memory_space=pltpu.SMEM)],
        out_shape=[jax.ShapeDtypeStruct((t, SB_WIDTH), F32), jax.ShapeDtypeStruct((t, 2 * TK * SB_WIDTH // LANES), F32),
                   jax.ShapeDtypeStruct((ne, SB_WIDTH // LANES, nq), jnp.int32)],
        scratch_shapes=[pltpu.VMEM((nk, LANES, 2 * TK), BF16), pltpu.VMEM((nk, 2 * TK, LANES), BF16)],
        args=(qkv, qkv, qkv), name="attn_fwd")
    return o, tot, kmin, got


def _outproj(y_lru, o, x, ga, gb, w_out):
    t = x.shape[0]
    tm = min(ROW_TILE, t)

    def kern(y_ref, o_ref, x_ref, ga_ref, gb_ref, w_ref, h1_ref, mix_ref):
        yv, ov = y_ref[...], o_ref[...]
        mix = jnp.concatenate([yv * _rstd(yv) * ga_ref[...], ov * _rstd(ov) * gb_ref[...]], axis=1).astype(BF16)
        mix_ref[...] = mix
        h1_ref[...] = x_ref[...] + _dot(mix, w_ref[...])

    row = lambda c: pl.BlockSpec((tm, c), lambda i: (i, 0))
    vec = lambda c: pl.BlockSpec((1, c), lambda i: (0, 0))
    return pl.pallas_call(
        kern, grid=(t // tm,),
        in_specs=[row(LRU_WIDTH), row(SB_WIDTH), row(D_MODEL), vec(LRU_WIDTH), vec(SB_WIDTH),
                  pl.BlockSpec((D_MODEL, D_MODEL), lambda i: (0, 0))],
        out_specs=[row(D_MODEL), row(D_MODEL)],
        out_shape=[jax.ShapeDtypeStruct((t, D_MODEL), F32), jax.ShapeDtypeStruct((t, D_MODEL), BF16)],
        compiler_params=_cp(("parallel",)), name="outproj",
    )(y_lru, o, x, ga, gb, w_out)


def _mlp_loss(h1, g2, w_up, w_down, target, gf):
    t = h1.shape[0]
    tm, tf = min(ROW_TILE, t), 1024

    def kern(h1_ref, g_ref, wu_ref, wd_ref, t_ref, gf_ref, hn_ref, up_ref, u2_ref, dh_ref, dhb_ref, loss_ref, dg_ref):
        @pl.when(pl.program_id(0) == 0)
        def _():
            loss_ref[...] = jnp.zeros_like(loss_ref)
            dg_ref[...] = jnp.zeros_like(dg_ref)

        hv = h1_ref[...]
        hn = (hv * _rstd(hv) * g_ref[...]).astype(BF16)
        hn_ref[...] = hn
        h2 = hv
        for f in range(D_FF // tf):
            cols = slice(f * tf, (f + 1) * tf)
            up = jnp.maximum(_dot(hn, wu_ref[:, cols]), 0.0)
            u2 = (up * up).astype(BF16)
            up_ref[:, cols] = up.astype(BF16)
            u2_ref[:, cols] = u2
            h2 = h2 + _dot(u2, wd_ref[cols, :])

        g = gf_ref[...]
        err = h2 * _rstd(h2) * g - t_ref[...]
        lane = lax.broadcasted_iota(jnp.int32, (1, LANES), 1)
        loss_ref[...] += jnp.where(lane == 0, 0.5 * jnp.sum(err * err) / D_MODEL, 0.0)
        dx, dg = _rms_bwd(h2, g, err * (1.0 / D_MODEL))
        dh_ref[...] = dx
        dhb_ref[...] = dx.astype(BF16)
        dg_ref[...] += dg

    row = lambda c: pl.BlockSpec((tm, c), lambda i: (i, 0))
    vec = pl.BlockSpec((1, D_MODEL), lambda i: (0, 0))
    return pl.pallas_call(
        kern, grid=(t // tm,),
        in_specs=[row(D_MODEL), vec, _resident((D_MODEL, D_FF)), _resident((D_FF, D_MODEL)), row(D_MODEL), vec],
        out_specs=[row(D_MODEL), row(D_FF), row(D_FF), row(D_MODEL), row(D_MODEL), pl.BlockSpec((1, LANES), lambda i: (0, 0)), vec],
        out_shape=[jax.ShapeDtypeStruct((t, D_MODEL), BF16), jax.ShapeDtypeStruct((t, D_FF), BF16),
                   jax.ShapeDtypeStruct((t, D_FF), BF16), jax.ShapeDtypeStruct((t, D_MODEL), F32),
                   jax.ShapeDtypeStruct((t, D_MODEL), BF16), jax.ShapeDtypeStruct((1, LANES), F32),
                   jax.ShapeDtypeStruct((1, D_MODEL), F32)],
        compiler_params=_cp(("arbitrary",), VMEM_LIMIT_BIG), name="mlp_loss",
    )(h1, g2, w_up, w_down, target, gf)


def _mlp_bwd_pre(dh2, w_down, up):
    t = dh2.shape[0]
    tm, tf = min(ROW_TILE, t), 1024

    def kern(d_ref, w_ref, up_ref, o_ref):
        dv = d_ref[...]
        for f in range(D_FF // tf):
            cols = slice(f * tf, (f + 1) * tf)
            o_ref[:, cols] = (_dot_nt(dv, w_ref[cols, :]) * (2.0 * up_ref[:, cols].astype(F32))).astype(BF16)

    row = lambda c: pl.BlockSpec((tm, c), lambda i: (i, 0))
    return pl.pallas_call(
        kern, grid=(t // tm,), in_specs=[row(D_MODEL), _resident((D_FF, D_MODEL)), row(D_FF)],
        out_specs=row(D_FF), out_shape=jax.ShapeDtypeStruct((t, D_FF), BF16),
        compiler_params=_cp(("parallel",)), name="mlp_bwd_pre",
    )(dh2, w_down, up)


def _proj_bwd_norm(dys, w, x, g, resid, name, comm=None, bf16_copy=False):
    t = x.shape[0]
    tm = min(ROW_TILE, t)
    widths = [dy.shape[1] for dy in dys]
    n = len(dys)

    def body(ins, outs, _):
        dy_refs, (w_ref, x_ref, g_ref, r_ref), (dx_ref, dg_ref) = ins[:n], ins[n:], outs[:2]

        @pl.when(pl.program_id(0) == 0)
        def _():
            dg_ref[...] = jnp.zeros_like(dg_ref)

        off, dxn = 0, None
        for dy_ref, wd in zip(dy_refs, widths):
            part = _dot_nt(dy_ref[...], w_ref[:, off:off + wd])
            dxn = part if dxn is None else dxn + part
            off += wd
        dx, dg = _rms_bwd(x_ref[...], g_ref[...], dxn)
        dx = r_ref[...] + dx
        dx_ref[...] = dx
        dg_ref[...] += dg
        if bf16_copy:
            outs[2][...] = dx.astype(BF16)

    row = lambda c: pl.BlockSpec((tm, c), lambda i: (i, 0))
    vec = pl.BlockSpec((1, D_MODEL), lambda i: (0, 0))
    outs, got = _call(
        body, comm, grid=(t // tm,), in_specs=[row(wd) for wd in widths] + [_resident(w.shape), row(D_MODEL), vec, row(D_MODEL)],
        out_specs=[row(D_MODEL), vec] + [row(D_MODEL)] * bf16_copy,
        out_shape=[jax.ShapeDtypeStruct((t, D_MODEL), F32), jax.ShapeDtypeStruct((1, D_MODEL), F32)]
        + [jax.ShapeDtypeStruct((t, D_MODEL), BF16)] * bf16_copy,
        scratch_shapes=[], args=(*dys, w, x, g, resid), name=name)
    return outs[0], (outs[2] if bf16_copy else None), outs[1], got


def _outproj_bwd(dh1, w_out, y_lru, o, ga, gb, comm=None):
    t = dh1.shape[0]
    tm = min(ROW_TILE, t)

    def body(ins, outs, _):
        (d_ref, w_ref, y_ref, o_ref, ga_ref, gb_ref), (dy_ref, do_ref, dga_ref, dgb_ref) = ins, outs

        @pl.when(pl.program_id(0) == 0)
        def _():
            dga_ref[...] = jnp.zeros_like(dga_ref)
            dgb_ref[...] = jnp.zeros_like(dgb_ref)

        dmix = _dot_nt(d_ref[...], w_ref[...])
        dy, dga = _rms_bwd(y_ref[...], ga_ref[...], dmix[:, :LRU_WIDTH])
        do, dgb = _rms_bwd(o_ref[...], gb_ref[...], dmix[:, LRU_WIDTH:])
        dy_ref[...] = dy
        do_ref[...] = do
        dga_ref[...] += dga
        dgb_ref[...] += dgb

    row = lambda c: pl.BlockSpec((tm, c), lambda i: (i, 0))
    vec = pl.BlockSpec((1, LRU_WIDTH), lambda i: (0, 0))
    half = jax.ShapeDtypeStruct((t, LRU_WIDTH), F32)
    gsum = jax.ShapeDtypeStruct((1, LRU_WIDTH), F32)
    outs, got = _call(body, comm, grid=(t // tm,),
                      in_specs=[row(D_MODEL), _resident((D_MODEL, D_MODEL)), row(LRU_WIDTH), row(SB_WIDTH), vec, vec],
                      out_specs=[row(LRU_WIDTH), row(SB_WIDTH), vec, vec], out_shape=[half, half, gsum, gsum],
                      scratch_shapes=[], args=(dh1, w_out, y_lru, o, ga, gb), name="outproj_bwd")
    return (*outs, got)


def _attn_bwd(qkv, do, tot, kmin, seq):
    t = qkv.shape[0]
    ne, nq, nk = t // seq, seq // TQ, seq // TK

    def kern(q_ref, k_ref, v_ref, do_ref, tot_ref, kmin_ref, dq_ref, dk_ref, dv_ref,
             kbd_scr, vtbd_scr, kbd2_scr, dkt_scr, dvt_scr):
        causal = _att_consts()
        upto, before = _sum_matrix("upto"), _sum_matrix("before")

        def prep(j, _):
            k0 = pl.multiple_of(j * TK, TK)
            kb = k_ref[pl.ds(k0, TK), :]
            top, bot = _head_diag(kb.T, True)
            kbd_scr[j] = jnp.concatenate([top, bot], axis=1)
            top, bot = _head_diag(v_ref[pl.ds(k0, TK), :].T, True)
            vtbd_scr[j] = jnp.concatenate([top, bot], axis=1)
            left, right = _head_diag(kb, False)
            kbd2_scr[j] = jnp.concatenate([left, right], axis=0)
            dkt_scr[j] = jnp.zeros((LANES, 2 * TK), F32)
            dvt_scr[j] = jnp.zeros((LANES, 2 * TK), F32)
            return 0

        lax.fori_loop(0, nk, prep, 0)

        def block(j, qb, qt, dob, dot_, totb, st, mask):
            f0, f1, p0, p1, dqacc = st
            lb, l1 = _att_logits(qb, kbd_scr[j])
            if mask is not None:
                l1 = jnp.where(mask, l1, 0.0)
            (s0, s1), (r0, r1) = _pair_sums(l1, upto)
            att = jnp.exp(lb + (totb - jnp.concatenate([s0 + f0, s1 + f1], axis=1)))
            if mask is not None:
                att = jnp.where(mask, att, 0.0)
            pw = att * _dot(dob, vtbd_scr[j])
            (e0, e1), (t0, t1) = _pair_sums(pw, before)
            dz = pw - jnp.exp(lb) * (pw + jnp.concatenate([e0 + p0, e1 + p1], axis=1))
            if mask is not None:
                dz = jnp.where(mask, dz, 0.0)
            dzb = dz.astype(BF16)
            dkt_scr[j] += _dot(qt, dzb)
            dvt_scr[j] += _dot(dot_, att.astype(BF16))
            return f0 + r0, f1 + r1, p0 + t0, p1 + t1, dqacc + _dot(dzb, kbd2_scr[j])

        def qloop(qi, _):
            q0 = pl.multiple_of(qi * TQ, TQ)
            qb = _scaled_q(q_ref, q0)
            qt = qb.T
            do32 = do_ref[pl.ds(q0, TQ), :]
            dob = do32.astype(BF16)
            dot_ = dob.T
            totb = tot_ref[pl.ds(q0, TQ), :]
            zero = jnp.zeros((TQ, TK), F32)
            st = (zero, zero, zero, zero, jnp.zeros((TQ, LANES), F32))

            k0 = kmin_ref[pl.program_id(0), pl.program_id(1), qi]

            def tile(j, lo, hi, st, masked):
                new = block(j, qb[lo:hi], qt[:, lo:hi], dob[lo:hi], dot_[:, lo:hi], totb[lo:hi],
                            tuple(x[lo:hi] for x in st), causal[:hi - lo] if masked else None)
                return _rows_update(st, new, lo, hi)

            def general():
                def kloop(it, st):
                    j = k0 + 2 * it
                    return block(j + 1, qb, qt, dob, dot_, totb, block(j, qb, qt, dob, dot_, totb, st, None), None)

                out = lax.fori_loop(0, ((TQ // TK) * qi - k0) // 2, kloop, st)
                for jj in range(TQ // TK):
                    out = tile((TQ // TK) * qi + jj, TK * jj, TQ, out, True)
                return out

            def short():
                out = st
                for jr, lo, hi in reversed(_band_tiles()):
                    out = tile((TQ // TK) * qi + jr, lo, hi, out, jr >= 0)
                return out

            st = lax.cond(k0 < 0, short, general)
            dq_ref[pl.ds(q0, TQ), :] = (st[4] * ATT_SCALE).astype(BF16)
            return 0

        lax.fori_loop(0, nq, qloop, 0)

        def finish(j, _):
            k0 = pl.multiple_of(j * TK, TK)
            head0 = lax.broadcasted_iota(jnp.int32, (LANES, TK), 0) < DH
            for src, dst in ((dkt_scr, dk_ref), (dvt_scr, dv_ref)):
                acc = src[j]
                dst[pl.ds(k0, TK), :] = jnp.where(head0, acc[:, :TK], acc[:, TK:]).astype(BF16).T
            return 0

        lax.fori_loop(0, nk, finish, 0)

    blk = pl.BlockSpec((seq, LANES), lambda e, p: (e, p))
    grad = jax.ShapeDtypeStruct((t, SB_WIDTH), BF16)
    return pl.pallas_call(
        kern, grid=(ne, SB_WIDTH // LANES),
        in_specs=_qkv_specs(seq) + [blk, pl.BlockSpec((seq, 2 * TK), lambda e, p: (e, p)),
                                    pl.BlockSpec(memory_space=pltpu.SMEM)],
        out_specs=[blk, blk, blk], out_shape=[grad, grad, grad],
        scratch_shapes=[pltpu.VMEM((nk, LANES, 2 * TK), BF16), pltpu.VMEM((nk, LANES, 2 * TK), BF16),
                        pltpu.VMEM((nk, 2 * TK, LANES), BF16), pltpu.VMEM((nk, LANES, 2 * TK), F32),
                        pltpu.VMEM((nk, LANES, 2 * TK), F32)],
        compiler_params=_cp(("parallel", "parallel")), name="attn_bwd",
    )(qkv, qkv, qkv, do, tot, kmin)


def _lru_bwd(xl, h, dy, conv_w, conv_b, wbd, ba, bx, lam, seq, comm=None):
    t = xl.shape[0]
    tc = min(512, seq)
    nc = seq // tc
    nb = tc // SUBLANES


    def body(ins, outs, scr):
        u_ref, g_ref, up_ref, h_ref, hp_ref, dy_ref, cw_ref, cb_ref, wbd_ref, ba_ref, bx_ref, lam_ref = ins
        (dxl_ref, small_ref, dwbd_ref), (lnext_ref, anext_ref, dcnext_ref) = outs, scr
        e, ci = pl.program_id(0), pl.program_id(1)
        first = ci == nc - 1

        @pl.when((e == 0) & (ci == 0))
        def _():
            small_ref[...] = jnp.zeros_like(small_ref)
            dwbd_ref[...] = jnp.zeros_like(dwbd_ref)

        @pl.when(ci == 0)
        def _():
            lnext_ref[...] = jnp.zeros_like(lnext_ref)
            anext_ref[...] = jnp.zeros_like(anext_ref)
            dcnext_ref[...] = jnp.zeros_like(dcnext_ref)

        u, g = u_ref[...], g_ref[...]
        keep = jnp.where(first, 0.0, 1.0)
        taps = _conv_taps(keep * up_ref[...], u)
        c = cb_ref[...]
        for k in range(CONV_WIDTH):
            c = c + taps[k] * cw_ref[k:k + 1, :]
        lam = lam_ref[...]
        sp = _softplus(-lam)
        r, i, a, mult, inv_mult = _lru_gates(c, wbd_ref, ba_ref[...], bx_ref[...], sp)
        gel, th = _gelu(g)
        dyv, hv = dy_ref[...], h_ref[...]
        dg = dyv * hv * _gelu_grad(g, th)

        aa, bb = _scan_rev(_shift_up(a, 1, anext_ref[0:1, :]), dyv * gel)
        lt = bb + aa * lnext_ref[0:1, :]
        lnext_ref[0:1, :] = _row_of(lt, 0)
        anext_ref[0:1, :] = _row_of(a, 0)

        hprev = _shift_down(hv, 1, keep * hp_ref[SUBLANES - 1:SUBLANES, :])
        da = lt * hprev
        dmult = lt * i * c
        di = lt * mult * c
        dc = lt * mult * i
        dla = da * a - dmult * (a * a) * inv_mult
        dga = dla * ((-LRU_C) * sp) * r * (1.0 - r)
        dgx = di * i * (1.0 - i)
        small_ref[7:8, :] += jnp.sum(dla * r, axis=0, keepdims=True) * (LRU_C * _sigmoid(-lam))
        small_ref[5:6, :] += jnp.sum(dga, axis=0, keepdims=True)
        small_ref[6:7, :] += jnp.sum(dgx, axis=0, keepdims=True)

        dcs = []
        for p in range(LRU_WIDTH // LANES):
            cols = slice(LANES * p, LANES * (p + 1))
            dgax = jnp.concatenate([dga[:, cols], dgx[:, cols]], axis=1).astype(BF16)
            dcs.append(_dot_nt(dgax, wbd_ref[p]))
            dwbd_ref[p] += _dot_tn(c[:, cols].astype(BF16), dgax)
        dc = dc + jnp.concatenate(dcs, axis=1)
        small_ref[4:5, :] += jnp.sum(dc, axis=0, keepdims=True)

        catd = jnp.concatenate([dc, dcnext_ref[...]], axis=0)
        du = dc * cw_ref[CONV_WIDTH - 1:CONV_WIDTH, :]
        for j in range(1, CONV_WIDTH):
            du = du + pltpu.roll(catd, tc + SUBLANES - j, 0)[:tc] * cw_ref[CONV_WIDTH - 1 - j:CONV_WIDTH - j, :]
        dcnext_ref[...] = dc[:SUBLANES]
        for k in range(CONV_WIDTH):
            small_ref[k:k + 1, :] += jnp.sum(dc * taps[k], axis=0, keepdims=True)
        dxl_ref[:, :LRU_WIDTH] = du.astype(BF16)
        dxl_ref[:, LRU_WIDTH:] = dg.astype(BF16)

    rev = lambda e, c: e * nc + (nc - 1 - c)
    chunk = lambda col: pl.BlockSpec((tc, LRU_WIDTH), lambda e, c: (rev(e, c), col))
    prev8 = pl.BlockSpec((SUBLANES, LRU_WIDTH), lambda e, c: (jnp.maximum(rev(e, c) * nb - 1, 0), 0))
    outs, got = _call(
        body, comm, grid=(t // seq, nc),
        in_specs=[chunk(0), chunk(1), prev8, chunk(0), prev8, chunk(0)] + _lru_param_specs(2),
        out_specs=[pl.BlockSpec((tc, 2 * LRU_WIDTH), lambda e, c: (rev(e, c), 0)),
                   pl.BlockSpec((SUBLANES, LRU_WIDTH), lambda e, c: (0, 0)),
                   pl.BlockSpec((LRU_WIDTH // LANES, LANES, 2 * LANES), lambda e, c: (0, 0, 0))],
        out_shape=[jax.ShapeDtypeStruct((t, 2 * LRU_WIDTH), BF16), jax.ShapeDtypeStruct((SUBLANES, LRU_WIDTH), F32),
                   jax.ShapeDtypeStruct((LRU_WIDTH // LANES, LANES, 2 * LANES), F32)],
        scratch_shapes=[pltpu.VMEM((SUBLANES, LRU_WIDTH), F32)] * 3,
        args=(xl, xl, xl, h, h, dy, conv_w, conv_b, wbd, ba, bx, lam), name="lru_bwd")
    return (*outs, got)


def _adam_math(w, g, m, v):
    m2 = ADAM_B1 * m + (1.0 - ADAM_B1) * g
    v2 = ADAM_B2 * v + (1.0 - ADAM_B2) * (g * g)
    m_hat = m2 / (1.0 - ADAM_B1 ** ADAM_STEP)
    v_hat = v2 / (1.0 - ADAM_B2 ** ADAM_STEP)
    return -ADAM_LR * (m_hat / (jnp.sqrt(v_hat) + ADAM_EPS) + ADAM_WD * w), m2, v2


def _adamw(w, g, m, v, name):
    rows, cols = w.shape
    tr = 256 if rows % 256 == 0 else rows

    def kern(w_ref, g_ref, m_ref, v_ref, d_ref, m2_ref, v2_ref):
        d_ref[...], m2_ref[...], v2_ref[...] = _adam_math(w_ref[...], g_ref[...], m_ref[...], v_ref[...])

    blk = pl.BlockSpec((tr, cols), lambda i: (i, 0))
    out = jax.ShapeDtypeStruct((rows, cols), F32)
    return pl.pallas_call(kern, grid=(rows // tr,), in_specs=[blk] * 4, out_specs=[blk] * 3, out_shape=[out] * 3,
                          compiler_params=_cp(("parallel",)), name=name)(w, g, m, v)


def _adamw_small(ws, gs, ms, vs):
    n = len(ws)

    def kern(*refs):
        for k in range(n):
            outs = _adam_math(refs[k][...], refs[n + k][...], refs[2 * n + k][...], refs[3 * n + k][...])
            for j in range(3):
                refs[(4 + j) * n + k][...] = outs[j]

    vm = pl.BlockSpec(memory_space=pltpu.VMEM)
    out = pl.pallas_call(kern, in_specs=[vm] * (4 * n), out_specs=[vm] * (3 * n),
                         out_shape=[jax.ShapeDtypeStruct(w.shape, F32) for w in ws] * 3, name="adamw_small")(*ws, *gs, *ms, *vs)
    return out[:n], out[n:2 * n], out[2 * n:]


def _comm_only(kind, arrays, name):
    return _call(lambda ins, outs, scr: None, (kind, arrays), grid=(1,), in_specs=[], out_specs=[], out_shape=[],
                 scratch_shapes=[], args=(), name=name)[1]


def _pair_add(g, got, core):
    _, rows, cols = g.shape
    half = rows // 2
    tr = min(256, half)
    nt = half // tr

    def kern(c_ref, g_ref, o_ref, out_ref):
        out_ref[...] = (g_ref[...] + o_ref[...]).astype(BF16)

    return pl.pallas_call(
        kern, grid_spec=pltpu.PrefetchScalarGridSpec(
            num_scalar_prefetch=1, grid=(N_CHIPS, nt),
            in_specs=[pl.BlockSpec((None, tr, cols), lambda j, i, c_ref: (j, c_ref[0] * nt + i, 0)),
                      pl.BlockSpec((None, tr, cols), lambda j, i, c_ref: (j, i, 0))],
            out_specs=pl.BlockSpec((None, tr, cols), lambda j, i, c_ref: (j, i, 0))),
        out_shape=jax.ShapeDtypeStruct((N_CHIPS, half, cols), BF16),
        compiler_params=_cp(("parallel", "parallel")), name="pair_add",
    )(core, g, got)


def _chip_add(part, got, place):
    _, half, cols = part.shape
    tr = min(256, half)
    nt = half // tr

    def kern(p_ref, part_ref, got_ref, out_ref):
        out_ref[...] = (part_ref[...].astype(F32) + got_ref[0].astype(F32) + got_ref[1].astype(F32)
                        + got_ref[2].astype(F32))

    return pl.pallas_call(
        kern, grid_spec=pltpu.PrefetchScalarGridSpec(
            num_scalar_prefetch=1, grid=(nt,),
            in_specs=[pl.BlockSpec((None, tr, cols), lambda i, p_ref: (p_ref[0], i, 0)),
                      pl.BlockSpec((3, tr, cols), lambda i, p_ref: (0, i, 0))],
            out_specs=pl.BlockSpec((tr, cols), lambda i, p_ref: (p_ref[1] * nt + i, 0))),
        out_shape=jax.ShapeDtypeStruct((2 * half, cols), F32),
        compiler_params=_cp(("parallel",)), name="chip_add",
    )(place, part, got)


def _finale(packed, fulls):
    rows, n = packed.shape[0], len(fulls)
    half = rows // 2
    assert half % SUBLANES == 0

    def kern(in_ref, *refs):
        ins, out_ref, outs = refs[:n], refs[n], refs[n + 1:2 * n + 1]
        pair_slot, chip_slots, pair_sems, chip_send, chip_recv, back_sems, join_send, join_recv = refs[2 * n + 1:]
        x, y, c = _place()
        chip, sibling = 2 * x + y, (x, y, 1 - c)
        mine = pl.ds(pl.multiple_of(c * half, SUBLANES), half)
        joins = []
        for w in range(n):
            rws = pl.ds(c * (ins[w].shape[0] // 2), ins[w].shape[0] // 2)
            joins.append(pltpu.make_async_remote_copy(
                src_ref=ins[w].at[rws, :], dst_ref=outs[w].at[rws, :], send_sem=join_send.at[w],
                recv_sem=join_recv.at[w], device_id=sibling, device_id_type=MESH))
        swap = pltpu.make_async_remote_copy(src_ref=in_ref, dst_ref=pair_slot, send_sem=pair_sems.at[0],
                                            recv_sem=pair_sems.at[1], device_id=sibling, device_id_type=MESH)
        for cp in joins + [swap]:
            cp.start()
        swap.wait()
        chip_slots[chip] = in_ref[mine, :] + pair_slot[mine, :]
        spread = [pltpu.make_async_remote_copy(
            src_ref=chip_slots.at[chip], dst_ref=chip_slots.at[chip], send_sem=chip_send.at[k], recv_sem=chip_recv.at[k],
            device_id=(*other, c), device_id_type=MESH) for k, other in enumerate(_other_chips(x, y))]
        for cp in spread:
            cp.start()
        for cp in spread:
            cp.wait()
        out_ref[mine, :] = chip_slots[0] + chip_slots[1] + chip_slots[2] + chip_slots[3]
        back = pltpu.make_async_remote_copy(src_ref=out_ref.at[mine, :], dst_ref=out_ref.at[mine, :], send_sem=back_sems.at[0],
                                            recv_sem=back_sems.at[1], device_id=sibling, device_id_type=MESH)
        back.start()
        for cp in [back] + joins:
            cp.wait()

    vm = pl.BlockSpec(memory_space=pltpu.VMEM)
    out = pl.pallas_call(
        kern, in_specs=[vm] + [ANY] * n, out_specs=[vm] + [ANY] * n,
        out_shape=[jax.ShapeDtypeStruct((rows, LANES), F32)] + [jax.ShapeDtypeStruct(f.shape, f.dtype) for f in fulls],
        input_output_aliases={w + 1: w + 1 for w in range(n)},
        scratch_shapes=[pltpu.VMEM((rows, LANES), F32), pltpu.VMEM((N_CHIPS, half, LANES), F32),
                        pltpu.SemaphoreType.DMA((2,)), pltpu.SemaphoreType.DMA((3,)), pltpu.SemaphoreType.DMA((3,)),
                        pltpu.SemaphoreType.DMA((2,)), pltpu.SemaphoreType.DMA((n,)), pltpu.SemaphoreType.DMA((n,))],
        name="finale",
    )(packed, *fulls)
    return out[0], list(out[1:])


SMALL = ["norm1_g", "conv_w", "conv_b", "lru_w_a", "lru_b_a", "lru_w_x", "lru_b_x", "lru_lambda", "lru_out_g", "sb_out_g",
         "norm2_g", "final_g"]
BIG = ["w_in", "w_out", "w_up", "w_down"]
WEIGHTS = ["norm1_g", "w_in", "conv_w", "conv_b", "lru_w_a", "lru_b_a", "lru_w_x", "lru_b_x", "lru_lambda", "lru_out_g",
           "sb_out_g", "w_out", "norm2_g", "w_up", "w_down", "final_g"]


def _pack(arrays):
    flat = []
    for a in arrays:
        a = a.reshape(-1).astype(F32)
        flat.append(jnp.pad(a, (0, (-a.shape[0]) % LANES)))
    v = jnp.concatenate(flat)
    v = jnp.pad(v, (0, (-v.shape[0]) % (LANES * 2 * SUBLANES)))
    return v.reshape(-1, LANES)


def _unpack(packed, shapes):
    v, out, off = packed.reshape(-1), [], 0
    for shp in shapes:
        size = math.prod(shp)
        out.append(v[off:off + size].reshape(shp))
        off += size + (-size) % LANES
    return out


def _blockdiag_pairs(w):
    w = w.reshape(4, 2, DH, DH)
    z = jnp.zeros((4, DH, DH), w.dtype)
    return jnp.concatenate([jnp.concatenate([w[:, 0], z], axis=2), jnp.concatenate([z, w[:, 1]], axis=2)], axis=1)


def _blockdiag_unpairs(wbd):
    return jnp.stack([wbd[:, :DH, :DH], wbd[:, DH:, DH:]], axis=1).reshape(8, DH, DH)


def _full_cols(g):
    return jnp.transpose(g, (1, 0, 2)).reshape(g.shape[1], N_CHIPS * g.shape[2])


def _local_step(x2, tgt, seq, norm1_g, w_in, conv_w, conv_b, w_a, b_a, w_x, b_x, lru_lambda, lru_out_g, sb_out_g, rest,
                norm2_g, final_g, place=None):
    alone = place is None
    wbd = jnp.concatenate([_blockdiag_pairs(w_a), _blockdiag_pairs(w_x)], axis=2).astype(BF16)
    ba, bx = b_a.reshape(1, LRU_WIDTH), b_x.reshape(1, LRU_WIDTH)
    gf = final_g.reshape(1, D_MODEL)

    xn, got = _norm1(x2, norm1_g, None if alone else ("gather2", [w_in]))
    w_in_f = w_in if alone else _full_cols(got[0])
    xl, qkv, got = _inproj(xn, w_in_f, None if alone else ("gather", [conv_w]))
    conv_w_f = conv_w if alone else _full_cols(got[0])
    h, y_lru = _lru_fwd(xl, conv_w_f, conv_b, wbd, ba, bx, lru_lambda, seq)
    o, tot, kmin, got = _attn_fwd(qkv, seq, None if alone else ("gather2", rest))
    w_out_f, w_up_f, w_down_f = rest if alone else (
        got[0].reshape(D_MODEL, D_MODEL), _full_cols(got[1]), got[2].reshape(D_FF, D_MODEL))
    h1, mix = _outproj(y_lru, o, x2, lru_out_g, sb_out_g, w_out_f)
    hn, up, u2, dh2, dh2b, loss_part, d_final = _mlp_loss(h1, norm2_g, w_up_f, w_down_f, tgt, gf)

    dpre = _mlp_bwd_pre(dh2b, w_down_f, up)
    g_w_down = _matmul(u2, dh2b, "tn", F32, "dw_down", 1024, 1024, 1024).reshape(N_CHIPS, D_FF // N_CHIPS, D_MODEL)
    g_w_up = _matmul(hn, dpre, "tn", F32, "dw_up", 1024, D_FF // N_CHIPS, 1024, split_cols=True)
    dh1, dh1b, d_norm2, _ = _proj_bwd_norm([dpre], w_up_f, h1, norm2_g, dh2, "mlp_bwd_in", bf16_copy=True)
    g_w_out = _matmul(mix, dh1b, "tn", F32, "dw_out", 1024, 1024, 2048).reshape(N_CHIPS, D_MODEL // N_CHIPS, D_MODEL)
    late = [g_w_out, g_w_up, g_w_down]
    dy_lru, do, d_ga, d_gb, swapped = _outproj_bwd(dh1b, w_out_f, y_lru, o, lru_out_g, sb_out_g,
                                                   None if alone else ("swap", late))
    parts = None if alone else [_pair_add(g, r, place[1:]) for g, r in zip(late, swapped)]
    dq, dk, dv = _attn_bwd(qkv, do, tot, kmin, seq)
    dxl, lru_small, d_wbd, got = _lru_bwd(xl, h, dy_lru, conv_w_f, conv_b, wbd, ba, bx, lru_lambda, seq,
                                          None if alone else ("exchange", parts))
    if not alone:
        late = [_chip_add(p, r, place) for p, r in zip(parts, got)]
    dproj = jnp.concatenate([dxl, dq, dk, dv], axis=1)
    g_w_in = _matmul(xn, dproj, "tn", F32, "dw_in", 1024, IN_COLS // N_CHIPS, 2048, split_cols=True)
    part = None if alone else _pair_add(g_w_in, _comm_only("swap", [g_w_in], "pair_swap")[0], place[1:])
    dx, _, d_norm1, got = _proj_bwd_norm([dxl, dq, dk, dv], w_in_f, x2, norm1_g, dh1, "inproj_bwd",
                                         None if alone else ("exchange", [part]))
    if not alone:
        g_w_in = _chip_add(part, got[0], place)
    small_parts = {
        "norm1_g": d_norm1, "conv_w": lru_small[:CONV_WIDTH], "conv_b": lru_small[4:5],
        "lru_w_a": _blockdiag_unpairs(d_wbd[:, :, :LANES]), "lru_b_a": lru_small[5:6],
        "lru_w_x": _blockdiag_unpairs(d_wbd[:, :, LANES:]), "lru_b_x": lru_small[6:7], "lru_lambda": lru_small[7:8],
        "lru_out_g": d_ga, "sb_out_g": d_gb, "norm2_g": d_norm2, "final_g": d_final,
    }
    return loss_part, dx, [g_w_in] + late, small_parts


def kernel(x, norm1_g, w_in, conv_w, conv_b, lru_w_a, lru_b_a, lru_w_x, lru_b_x, lru_lambda, lru_out_g, sb_out_g, w_out, norm2_g, w_up, w_down, final_g, loss_target, m_norm1_g, m_w_in, m_conv_w, m_conv_b, m_lru_w_a, m_lru_b_a, m_lru_w_x, m_lru_b_x, m_lru_lambda, m_lru_out_g, m_sb_out_g, m_w_out, m_norm2_g, m_w_up, m_w_down, m_final_g, v_norm1_g, v_w_in, v_conv_w, v_conv_b, v_lru_w_a, v_lru_b_a, v_lru_w_x, v_lru_b_x, v_lru_lambda, v_lru_out_g, v_sb_out_g, v_w_out, v_norm2_g, v_w_up, v_w_down, v_final_g):
    given = dict(locals())
    ne, seq, _ = x.shape
    t = ne * seq
    xi, yi, ci = _place()
    place = jnp.stack([2 * xi + yi, ci]).astype(jnp.int32)

    loss_part, dx, halves, small_parts = _local_step(
        x.reshape(t, D_MODEL), loss_target.reshape(t, D_MODEL), seq, norm1_g, w_in[0].astype(BF16), conv_w[0], conv_b,
        lru_w_a[0], lru_b_a, lru_w_x[0], lru_b_x, lru_lambda, lru_out_g, sb_out_g,
        [w_out[0].astype(BF16), w_up[0].astype(BF16), w_down[0].astype(BF16)], norm2_g, final_g, place)

    full_shapes = {n: ((CONV_WIDTH, LRU_WIDTH) if n == "conv_w" else given[n].shape) for n in SMALL}
    red, fulls = _finale(_pack([small_parts[n] for n in SMALL] + [loss_part]), halves)
    red_list = _unpack(red, [full_shapes[n] for n in SMALL] + [(1, LANES)])
    grads = dict(zip(SMALL, red_list[:-1]))
    loss = red_list[-1][0, 0]
    grads["conv_w"] = lax.dynamic_slice_in_dim(grads["conv_w"], place[0] * (LRU_WIDTH // N_CHIPS), LRU_WIDTH // N_CHIPS,
                                               axis=1).reshape(conv_w.shape)
    for n, full in zip(BIG, fulls):
        grads[n] = full.reshape(given[n].shape)

    delta, new_m, new_v = {}, {}, {}
    for n in BIG:
        shp = given[n].shape
        d, m2, v2 = _adamw(given[n][0], grads[n][0], given["m_" + n][0], given["v_" + n][0], "adamw_" + n)
        delta[n], new_m[n], new_v[n] = d.reshape(shp), m2.reshape(shp), v2.reshape(shp)
    as2d = lambda a: a.reshape(-1, a.shape[-1])
    ds, m2s, v2s = _adamw_small([as2d(given[n]) for n in SMALL], [as2d(grads[n]) for n in SMALL],
                                [as2d(given["m_" + n]) for n in SMALL], [as2d(given["v_" + n]) for n in SMALL])
    for n, dd, mm, vv in zip(SMALL, ds, m2s, v2s):
        shp = given[n].shape
        delta[n], new_m[n], new_v[n] = dd.reshape(shp), mm.reshape(shp), vv.reshape(shp)

    return (loss, dx.reshape(x.shape), *[grads[n] for n in WEIGHTS], *[delta[n] for n in WEIGHTS],
            *[new_m[n] for n in WEIGHTS], *[new_v[n] for n in WEIGHTS])
```

```python
import functools
import math

import jax
import jax.numpy as jnp
from jax import lax
from jax.experimental import pallas as pl
from jax.experimental.pallas import tpu as pltpu

F32, BF16 = jnp.float32, jnp.bfloat16
MESH = pl.DeviceIdType.MESH

D_MODEL = 1024
LRU_WIDTH = 512
SB_WIDTH = 512
DH = 64
IN_COLS = 2 * LRU_WIDTH + 3 * SB_WIDTH
D_FF = 4 * D_MODEL
CONV_WIDTH = 4
LRU_C = 8.0
EPS = 1e-6
N_CHIPS = 4
N_DEV = 8
LANES = 128
SUBLANES = 8
ROW_TILE = 512
TQ = 512
TK = 128
ATT_SCALE = 1.0 / math.sqrt(DH)
SKIP_LOG = -105.0
BAND = 2
VMEM_LIMIT = 52 * 1024 * 1024
VMEM_LIMIT_BIG = 62 * 1024 * 1024

ADAM_LR, ADAM_B1, ADAM_B2, ADAM_EPS, ADAM_WD, ADAM_STEP = 0.001, 0.9, 0.999, 1e-08, 0.01, 10

_GELU_K = math.sqrt(2.0 / math.pi)
_GELU_C = 0.044715


def _cp(sem, vmem=VMEM_LIMIT):
    return pltpu.CompilerParams(dimension_semantics=sem, vmem_limit_bytes=vmem)


def _dot(a, b):
    return jnp.dot(a, b, preferred_element_type=F32)


def _dot_nt(a, b):
    return lax.dot_general(a, b, (((1,), (1,)), ((), ())), preferred_element_type=F32)


def _dot_tn(a, b):
    return lax.dot_general(a, b, (((0,), (0,)), ((), ())), preferred_element_type=F32)


def _rstd(x):
    return lax.rsqrt(jnp.mean(x * x, axis=-1, keepdims=True) + EPS)


def _rms_bwd(x, g, dy):
    r = _rstd(x)
    gd = g * dy
    dx = r * gd - x * (r * r * r) * jnp.mean(x * gd, axis=-1, keepdims=True)
    return dx, jnp.sum(dy * x * r, axis=0, keepdims=True)


def _sigmoid(x):
    return 0.5 * jnp.tanh(0.5 * x) + 0.5


def _softplus(x):
    return jnp.maximum(x, 0.0) + jnp.log(1.0 + jnp.exp(-jnp.abs(x)))


def _neg_expm1(x, ex):
    series = -x * (1.0 + x * (0.5 + x * (1.0 / 6.0)))
    return jnp.where(x > -2.0 ** -7, series, 1.0 - ex)


def _gelu(g):
    t = jnp.tanh(_GELU_K * (g + _GELU_C * g * g * g))
    return 0.5 * g * (1.0 + t), t


def _gelu_grad(g, t):
    return 0.5 * (1.0 + t) + 0.5 * g * (1.0 - t * t) * _GELU_K * (1.0 + 3.0 * _GELU_C * g * g)


def _rows(shape):
    return lax.broadcasted_iota(jnp.int32, shape, 0)


def _shift_down(x, s, fill):
    n, c = x.shape
    if s % SUBLANES == 0:
        return jnp.concatenate([jnp.broadcast_to(jnp.asarray(fill, x.dtype), (s, c)), x[:n - s]], axis=0)
    return jnp.where(_rows(x.shape) >= s, pltpu.roll(x, s, 0), fill)


def _shift_up(x, s, fill):
    n, c = x.shape
    if s % SUBLANES == 0:
        return jnp.concatenate([x[s:], jnp.broadcast_to(jnp.asarray(fill, x.dtype), (s, c))], axis=0)
    return jnp.where(_rows(x.shape) < n - s, pltpu.roll(x, n - s, 0), fill)


def _row_of(x, idx):
    return jnp.sum(jnp.where(_rows(x.shape) == idx, x, 0.0), axis=0, keepdims=True)


def _matmul(a, b, dims, out_dtype, name, tm, tn, tk, split_cols=False):
    if dims == "nn":
        (m, kk), n, dot = a.shape, b.shape[1], _dot
    elif dims == "nt":
        (m, kk), n, dot = a.shape, b.shape[0], _dot_nt
    else:
        (kk, m), n, dot = a.shape, b.shape[1], _dot_tn
    tm, tn, tk = min(tm, m), min(tn, n), min(tk, kk)
    assert m % tm == 0 and n % tn == 0 and kk % tk == 0, (name, m, n, kk)
    nk = kk // tk

    def kern(a_ref, b_ref, o_ref, acc_ref):
        k = pl.program_id(2)

        @pl.when(k == 0)
        def _():
            acc_ref[...] = jnp.zeros_like(acc_ref)

        acc_ref[...] += dot(a_ref[...].astype(BF16), b_ref[...].astype(BF16))

        @pl.when(k == nk - 1)
        def _():
            o_ref[...] = acc_ref[...].astype(o_ref.dtype)

    if split_cols:
        out_shape = jax.ShapeDtypeStruct((n // tn, m, tn), out_dtype)
        o_spec = pl.BlockSpec((None, tm, tn), lambda i, j, k: (j, i, 0))
    else:
        out_shape = jax.ShapeDtypeStruct((m, n), out_dtype)
        o_spec = pl.BlockSpec((tm, tn), lambda i, j, k: (i, j))
    if dims == "nn":
        a_spec = pl.BlockSpec((tm, tk), lambda i, j, k: (i, k))
        b_spec = pl.BlockSpec((tk, tn), lambda i, j, k: (k, j))
    elif dims == "nt":
        a_spec = pl.BlockSpec((tm, tk), lambda i, j, k: (i, k))
        b_spec = pl.BlockSpec((tn, tk), lambda i, j, k: (j, k))
    else:
        a_spec = pl.BlockSpec((tk, tm), lambda i, j, k: (k, i))
        b_spec = pl.BlockSpec((tk, tn), lambda i, j, k: (k, j))
    return pl.pallas_call(
        kern, grid=(m // tm, n // tn, nk), in_specs=[a_spec, b_spec], out_specs=o_spec, out_shape=out_shape,
        scratch_shapes=[pltpu.VMEM((tm, tn), F32)], compiler_params=_cp(("parallel", "parallel", "arbitrary")), name=name,
    )(a, b)


ANY = pl.BlockSpec(memory_space=pl.ANY)


def _place():
    return lax.axis_index("x"), lax.axis_index("y"), lax.axis_index("c")


def _other_chips(x, y):
    return [(1 - x, y), (x, 1 - y), (1 - x, 1 - y)]


def _own_slab(shard, gathered, send_sem, recv_sem):
    x, y, c = _place()
    return pltpu.make_async_remote_copy(src_ref=shard, dst_ref=gathered.at[2 * x + y], send_sem=send_sem, recv_sem=recv_sem,
                                        device_id=(x, y, 1 - c), device_id_type=MESH)


def _gather_copies(ins, outs, send_sems, recv_sems, own_send, own_recv):
    x, y, c = _place()
    mine = 2 * x + y
    copies = []
    for w in range(len(ins)):
        copies.append(_own_slab(ins[w], outs[w], own_send.at[w], own_recv.at[w]))
        for k, chip in enumerate(_other_chips(x, y)):
            copies.append(pltpu.make_async_remote_copy(
                src_ref=ins[w], dst_ref=outs[w].at[mine], send_sem=send_sems.at[3 * w + k],
                recv_sem=recv_sems.at[3 * w + k], device_id=(*chip, c), device_id_type=MESH))
    return copies


def _gather_shapes(shards):
    return ([jax.ShapeDtypeStruct((N_CHIPS,) + s.shape, s.dtype) for s in shards],
            [pltpu.SemaphoreType.DMA((3 * len(shards),)), pltpu.SemaphoreType.DMA((3 * len(shards),)),
             pltpu.SemaphoreType.DMA((len(shards),)), pltpu.SemaphoreType.DMA((len(shards),))])


def _exchange_copies(ins, outs, send_sems, recv_sems):
    x, y, c = _place()
    copies = []
    for w in range(len(ins)):
        for k, chip in enumerate(_other_chips(x, y)):
            copies.append(pltpu.make_async_remote_copy(
                src_ref=ins[w].at[2 * chip[0] + chip[1]], dst_ref=outs[w].at[k], send_sem=send_sems.at[3 * w + k],
                recv_sem=recv_sems.at[3 * w + k], device_id=(*chip, c), device_id_type=MESH))
    return copies


def _exchange_shapes(parts):
    return ([jax.ShapeDtypeStruct((3,) + p.shape[1:], p.dtype) for p in parts],
            [pltpu.SemaphoreType.DMA((3 * len(parts),)), pltpu.SemaphoreType.DMA((3 * len(parts),))])


def _swap_copies(ins, outs, send_sems, recv_sems):
    x, y, c = _place()
    copies = []
    for w in range(len(ins)):
        half = ins[w].shape[1] // 2
        copies.append(pltpu.make_async_remote_copy(
            src_ref=ins[w].at[:, pl.ds((1 - c) * half, half), :], dst_ref=outs[w], send_sem=send_sems.at[w],
            recv_sem=recv_sems.at[w], device_id=(x, y, 1 - c), device_id_type=MESH))
    return copies


def _swap_shapes(grads):
    return ([jax.ShapeDtypeStruct((g.shape[0], g.shape[1] // 2, g.shape[2]), g.dtype) for g in grads],
            [pltpu.SemaphoreType.DMA((len(grads),)), pltpu.SemaphoreType.DMA((len(grads),))])


def _gather2_copies(ins, outs, send_sems, recv_sems, own_send, own_recv, fwd_send, fwd_recv):
    x, y, c = _place()
    mine = 2 * x + y
    copies = []
    for w in range(len(ins)):
        half = ins[w].shape[0] // 2
        rows = pl.ds(c * half, half)
        copies.append(_own_slab(ins[w], outs[w], own_send.at[w], own_recv.at[w]))
        for k, chip in enumerate(_other_chips(x, y)):
            copies.append(pltpu.make_async_remote_copy(
                src_ref=ins[w].at[rows, :], dst_ref=outs[w].at[mine, rows, :], send_sem=send_sems.at[3 * w + k],
                recv_sem=recv_sems.at[3 * w + k], device_id=(*chip, c), device_id_type=MESH))
    return copies


def _gather2_forward(ins, outs, send_sems, recv_sems, own_send, own_recv, fwd_send, fwd_recv):
    x, y, c = _place()
    copies = []
    for w in range(len(ins)):
        half = ins[w].shape[0] // 2
        rows = pl.ds(c * half, half)
        for k, chip in enumerate(_other_chips(x, y)):
            slab = outs[w].at[2 * chip[0] + chip[1], rows, :]
            copies.append(pltpu.make_async_remote_copy(
                src_ref=slab, dst_ref=slab, send_sem=fwd_send.at[3 * w + k], recv_sem=fwd_recv.at[3 * w + k],
                device_id=(x, y, 1 - c), device_id_type=MESH))
    return copies


def _gather2_shapes(shards):
    n = len(shards)
    return ([jax.ShapeDtypeStruct((N_CHIPS,) + s.shape, s.dtype) for s in shards],
            [pltpu.SemaphoreType.DMA((3 * n,)), pltpu.SemaphoreType.DMA((3 * n,)), pltpu.SemaphoreType.DMA((n,)),
             pltpu.SemaphoreType.DMA((n,)), pltpu.SemaphoreType.DMA((3 * n,)), pltpu.SemaphoreType.DMA((3 * n,))])


COMM = {"gather": (_gather_copies, _gather_shapes, None), "exchange": (_exchange_copies, _exchange_shapes, None),
        "swap": (_swap_copies, _swap_shapes, None), "gather2": (_gather2_copies, _gather2_shapes, _gather2_forward)}


def _call(body, comm, *, grid, in_specs, out_specs, out_shape, scratch_shapes, args, name):
    ni, no, ns = len(in_specs), len(out_specs), len(scratch_shapes)
    arrays = list(comm[1]) if comm else []
    nc = len(arrays)
    first_fn, shapes_fn, second_fn = COMM[comm[0]] if comm else (None, None, None)
    c_shapes, c_sems = shapes_fn(arrays) if comm else ([], [])

    def kern(*refs):
        ins, cin, outs = refs[:ni], refs[ni:ni + nc], refs[ni + nc:ni + nc + no]
        rest = refs[ni + nc + no:]
        cout, scr, sems = rest[:nc], rest[nc:nc + ns], rest[nc + ns:]
        ids = [pl.program_id(d) for d in range(len(grid))]
        if nc:
            @pl.when(functools.reduce(lambda a, b: a & b, [i == 0 for i in ids]))
            def _():
                for cp in first_fn(cin, cout, *sems):
                    cp.start()

        body(ins, outs, scr)
        if nc:
            @pl.when(functools.reduce(lambda a, b: a & b, [i == g - 1 for i, g in zip(ids, grid)]))
            def _():
                for cp in first_fn(cin, cout, *sems):
                    cp.wait()
                if second_fn is not None:
                    more = second_fn(cin, cout, *sems)
                    for cp in more:
                        cp.start()
                    for cp in more:
                        cp.wait()

    out = pl.pallas_call(
        kern, grid=grid, in_specs=list(in_specs) + [ANY] * nc, out_specs=list(out_specs) + [ANY] * nc,
        out_shape=list(out_shape) + c_shapes, scratch_shapes=list(scratch_shapes) + c_sems,
        compiler_params=_cp(("arbitrary",) * len(grid)), name=name,
    )(*args, *arrays)
    return list(out[:no]), list(out[no:])


def _resident(shape):
    return pl.BlockSpec(shape, lambda *_: (0,) * len(shape), pipeline_mode=pl.Buffered(1))


def _norm1(x, g1, comm):
    t = x.shape[0]
    tm = min(1024, t)

    def body(ins, outs, _):
        xv = ins[0][...]
        outs[0][...] = (xv * _rstd(xv) * ins[1][...]).astype(BF16)

    row = pl.BlockSpec((tm, D_MODEL), lambda i: (i, 0))
    (xn,), got = _call(body, comm, grid=(t // tm,), in_specs=[row, pl.BlockSpec((1, D_MODEL), lambda i: (0, 0))],
                       out_specs=[row], out_shape=[jax.ShapeDtypeStruct((t, D_MODEL), BF16)], scratch_shapes=[],
                       args=(x, g1), name="norm1")
    return xn, got


def _inproj(xn, w_in, comm=None):
    t = xn.shape[0]
    tm = min(ROW_TILE, t)

    def body(ins, outs, _):
        xn_v = ins[0][...]
        outs[0][...] = _dot(xn_v, ins[1][:, : 2 * LRU_WIDTH])
        outs[1][...] = _dot(xn_v, ins[1][:, 2 * LRU_WIDTH:]).astype(BF16)

    row = lambda c: pl.BlockSpec((tm, c), lambda i: (i, 0))
    (xl, qkv), got = _call(
        body, comm, grid=(t // tm,), in_specs=[row(D_MODEL), _resident((D_MODEL, IN_COLS))],
        out_specs=[row(2 * LRU_WIDTH), row(3 * SB_WIDTH)],
        out_shape=[jax.ShapeDtypeStruct((t, 2 * LRU_WIDTH), F32), jax.ShapeDtypeStruct((t, 3 * SB_WIDTH), BF16)],
        scratch_shapes=[], args=(xn, w_in), name="inproj")
    return xl, qkv, got


def _conv_taps(hist, u):
    cat = jnp.concatenate([hist, u], axis=0)
    return [pltpu.roll(cat, CONV_WIDTH - 1 - k, 0)[SUBLANES:] for k in range(CONV_WIDTH - 1)] + [u]


def _lru_gates(c, wbd_ref, ba, bx, sp):
    gas, gxs = [], []
    for p in range(LRU_WIDTH // LANES):
        gax = _dot(c[:, LANES * p: LANES * (p + 1)].astype(BF16), wbd_ref[p])
        gas.append(gax[:, :LANES])
        gxs.append(gax[:, LANES:])
    r = _sigmoid(jnp.concatenate(gas, axis=1) + ba)
    i = _sigmoid(jnp.concatenate(gxs, axis=1) + bx)
    la = (-LRU_C) * r * sp
    a = jnp.exp(la)
    e2 = _neg_expm1(2.0 * la, a * a)
    inv_mult = lax.rsqrt(jnp.maximum(e2, 1e-30))
    return r, i, a, e2 * inv_mult, inv_mult


def _scan_fwd(a, b):
    s = 1
    while s < a.shape[0]:
        b = b + a * _shift_down(b, s, 0.0)
        a = a * _shift_down(a, s, 1.0)
        s *= 2
    return a, b


def _scan_rev(a, b):
    s = 1
    while s < a.shape[0]:
        b = b + a * _shift_up(b, s, 0.0)
        a = a * _shift_up(a, s, 1.0)
        s *= 2
    return a, b


def _lru_param_specs(grid_rank):
    z2 = (lambda e, c: (0, 0)) if grid_rank == 2 else None
    return [
        pl.BlockSpec((CONV_WIDTH, LRU_WIDTH), z2), pl.BlockSpec((1, LRU_WIDTH), z2),
        pl.BlockSpec((LRU_WIDTH // LANES, LANES, 2 * LANES), lambda e, c: (0, 0, 0)),
        pl.BlockSpec((1, LRU_WIDTH), z2), pl.BlockSpec((1, LRU_WIDTH), z2), pl.BlockSpec((1, LRU_WIDTH), z2),
    ]


def _lru_fwd(xl, conv_w, conv_b, wbd, ba, bx, lam, seq):
    t = xl.shape[0]
    tc = min(512, seq)
    nc = seq // tc

    def kern(u_ref, g_ref, cw_ref, cb_ref, wbd_ref, ba_ref, bx_ref, lam_ref, h_ref, y_ref, hist_ref, hcar_ref):
        @pl.when(pl.program_id(1) == 0)
        def _():
            hist_ref[...] = jnp.zeros_like(hist_ref)
            hcar_ref[...] = jnp.zeros_like(hcar_ref)

        u = u_ref[...]
        taps = _conv_taps(hist_ref[...], u)
        hist_ref[...] = u_ref[tc - SUBLANES:, :]
        c = cb_ref[...]
        for k in range(CONV_WIDTH):
            c = c + taps[k] * cw_ref[k:k + 1, :]
        sp = _softplus(-lam_ref[...])
        _, i, a, mult, _ = _lru_gates(c, wbd_ref, ba_ref[...], bx_ref[...], sp)
        aa, bb = _scan_fwd(a, mult * i * c)
        h = bb + aa * hcar_ref[0:1, :]
        h_ref[...] = h
        hcar_ref[0:1, :] = h_ref[tc - 1:tc, :]
        y_ref[...] = h * _gelu(g_ref[...])[0]

    chunk = lambda col: pl.BlockSpec((tc, LRU_WIDTH), lambda e, c: (e * nc + c, col))
    out = jax.ShapeDtypeStruct((t, LRU_WIDTH), F32)
    return pl.pallas_call(
        kern, grid=(t // seq, nc), in_specs=[chunk(0), chunk(1)] + _lru_param_specs(2),
        out_specs=[chunk(0), chunk(0)], out_shape=[out, out],
        scratch_shapes=[pltpu.VMEM((SUBLANES, LRU_WIDTH), F32), pltpu.VMEM((SUBLANES, LRU_WIDTH), F32)],
        compiler_params=_cp(("arbitrary", "arbitrary")), name="lru_fwd",
    )(xl, xl, conv_w, conv_b, wbd, ba, bx, lam)


def _att_consts():
    row = lax.broadcasted_iota(jnp.int32, (TQ, 2 * TK), 0)
    key = lax.broadcasted_iota(jnp.int32, (TQ, 2 * TK), 1) & (TK - 1)
    return key < row


def _sum_matrix(kind):
    j = lax.broadcasted_iota(jnp.int32, (2 * TK, 2 * TK), 0) & (TK - 1)
    s = lax.broadcasted_iota(jnp.int32, (2 * TK, 2 * TK), 1)
    pick = {"after": j > s, "upto": j <= s, "before": j < s}[kind]
    return jnp.where((s >= TK) | pick, 1.0, 0.0).astype(BF16)


def _hi_lo(x):
    hi = x.astype(BF16)
    return hi, (x - hi.astype(F32)).astype(BF16)


def _pair_sums(x, m):
    hi, lo = _hi_lo(x)
    out = []
    for hd in range(2):
        cols = slice(hd * TK, (hd + 1) * TK)
        out.append(_dot(jnp.concatenate([hi[:, cols], lo[:, cols]], axis=1), m))
    return [o[:, :TK] for o in out], [o[:, TK:] for o in out]


def _att_logits(qb, kbd):
    z = _dot(qb, kbd)
    lg = jnp.log(1.0 + jnp.exp(-jnp.abs(z)))
    lb = jnp.minimum(z, 0.0) - lg
    return lb, lb - z


def _head_diag(x, rows_first):
    n = x.shape[0] if rows_first else x.shape[1]
    idx = lax.broadcasted_iota(jnp.int32, x.shape, 0 if rows_first else 1)
    return jnp.where(idx < n // 2, x, 0), jnp.where(idx >= n // 2, x, 0)


def _band_tiles():
    nd = TQ // TK
    return [(jr, TK * max(jr, 0), TK * min(jr + BAND + 1, nd)) for jr in range(nd - 1, -BAND - 1, -1)]


def _rows_update(st, new, lo, hi):
    def one(x, y):
        pieces = ([x[:lo]] if lo else []) + [y] + ([x[hi:]] if hi < x.shape[0] else [])
        return pieces[0] if len(pieces) == 1 else jnp.concatenate(pieces, axis=0)
    return tuple(one(x, y) for x, y in zip(st, new))


def _scaled_q(q_ref, q0):
    return (q_ref[pl.ds(q0, TQ), :].astype(F32) * ATT_SCALE).astype(BF16)


def _qkv_specs(seq):
    n = SB_WIDTH // LANES
    return [pl.BlockSpec((seq, LANES), lambda e, p, off=off: (e, off * n + p)) for off in range(3)]


def _attn_fwd(qkv, seq, comm=None):
    t = qkv.shape[0]
    ne, nq, nk = t // seq, seq // TQ, seq // TK

    def body(ins, outs, scr):
        (q_ref, k_ref, v_ref), (o_ref, tot_ref, kmin_ref), (kbd_scr, vbd_scr) = ins, outs, scr
        causal = _att_consts()
        after = _sum_matrix("after")

        def prep(j, _):
            k0 = pl.multiple_of(j * TK, TK)
            top, bot = _head_diag(k_ref[pl.ds(k0, TK), :].T, True)
            kbd_scr[j] = jnp.concatenate([top, bot], axis=1)
            left, right = _head_diag(v_ref[pl.ds(k0, TK), :], False)
            vbd_scr[j] = jnp.concatenate([left, right], axis=0)
            return 0

        lax.fori_loop(0, nk, prep, 0)

        def block(j, qb, st, mask):
            c0, c1, oacc = st
            lb, l1 = _att_logits(qb, kbd_scr[j])
            if mask is not None:
                l1 = jnp.where(mask, l1, 0.0)
            (s0, s1), (r0, r1) = _pair_sums(l1, after)
            att = jnp.exp(lb + jnp.concatenate([s0 + c0, s1 + c1], axis=1))
            if mask is not None:
                att = jnp.where(mask, att, 0.0)
            return c0 + r0, c1 + r1, oacc + _dot(att.astype(BF16), vbd_scr[j])

        def general(qi, qb, st):
            for jj in reversed(range(TQ // TK)):
                lo = TK * jj
                new = block((TQ // TK) * qi + jj, qb[lo:], tuple(x[lo:] for x in st), causal[:TQ - lo])
                st = _rows_update(st, new, lo, TQ)

            npair = (TQ // TK // 2) * qi

            def more(its):
                return (its[0] < npair) & (jnp.max(jnp.maximum(its[1], its[2])) > SKIP_LOG)

            def kloop(its):
                j = 2 * (npair - its[0]) - 1
                return (its[0] + 1,) + block(j - 1, qb, block(j, qb, its[1:], None), None)

            done, c0, c1, oacc = lax.while_loop(more, kloop, (jnp.int32(0),) + st)
            return c0, c1, oacc, 2 * (npair - done)

        def short(qi, qb, st):
            for jr, lo, hi in _band_tiles():
                new = block((TQ // TK) * qi + jr, qb[lo:hi], tuple(x[lo:hi] for x in st),
                            causal[:hi - lo] if jr >= 0 else None)
                st = _rows_update(st, new, lo, hi)
            return st

        def qloop(qi, _):
            q0 = pl.multiple_of(qi * TQ, TQ)
            qb = _scaled_q(q_ref, q0)
            zero = jnp.zeros((TQ, TK), F32)
            st = (zero, zero, jnp.zeros((TQ, LANES), F32))

            def try_short():
                c0, c1, oacc = short(qi, qb, st)
                return lax.cond(jnp.max(jnp.maximum(c0, c1)) <= SKIP_LOG, lambda: (c0, c1, oacc, jnp.int32(-1)),
                                lambda: general(qi, qb, st))

            c0, c1, oacc, first = lax.cond(qi > 0, try_short, lambda: general(qi, qb, st))
            o_ref[pl.ds(q0, TQ), :] = oacc
            tot_ref[pl.ds(q0, TQ), :] = jnp.concatenate([c0, c1], axis=1)
            kmin_ref[pl.program_id(0), pl.program_id(1), qi] = first
            return 0

        lax.fori_loop(0, nq, qloop, 0)

    (o, tot, kmin), got = _call(
        body, comm, grid=(ne, SB_WIDTH // LANES), in_specs=_qkv_specs(seq),
        out_specs=[pl.BlockSpec((seq, LANES), lambda e, p: (e, p)), pl.BlockSpec((seq, 2 * TK), lambda e, p: (e, p)),
                   pl.BlockSpec(memory_space=pltpu.SMEM)],
        out_shape=[jax.ShapeDtypeStruct((t, SB_WIDTH), F32), jax.ShapeDtypeStruct((t, 2 * TK * SB_WIDTH // LANES), F32),
                   jax.ShapeDtypeStruct((ne, SB_WIDTH // LANES, nq), jnp.int32)],
        scratch_shapes=[pltpu.VMEM((nk, LANES, 2 * TK), BF16), pltpu.VMEM((nk, 2 * TK, LANES), BF16)],
        args=(qkv, qkv, qkv), name="attn_fwd")
    return o, tot, kmin, got


def _outproj(y_lru, o, x, ga, gb, w_out):
    t = x.shape[0]
    tm = min(ROW_TILE, t)

    def kern(y_ref, o_ref, x_ref, ga_ref, gb_ref, w_ref, h1_ref, mix_ref):
        yv, ov = y_ref[...], o_ref[...]
        mix = jnp.concatenate([yv * _rstd(yv) * ga_ref[...], ov * _rstd(ov) * gb_ref[...]], axis=1).astype(BF16)
        mix_ref[...] = mix
        h1_ref[...] = x_ref[...] + _dot(mix, w_ref[...])

    row = lambda c: pl.BlockSpec((tm, c), lambda i: (i, 0))
    vec = lambda c: pl.BlockSpec((1, c), lambda i: (0, 0))
    return pl.pallas_call(
        kern, grid=(t // tm,),
        in_specs=[row(LRU_WIDTH), row(SB_WIDTH), row(D_MODEL), vec(LRU_WIDTH), vec(SB_WIDTH),
                  pl.BlockSpec((D_MODEL, D_MODEL), lambda i: (0, 0))],
        out_specs=[row(D_MODEL), row(D_MODEL)],
        out_shape=[jax.ShapeDtypeStruct((t, D_MODEL), F32), jax.ShapeDtypeStruct((t, D_MODEL), BF16)],
        compiler_params=_cp(("parallel",)), name="outproj",
    )(y_lru, o, x, ga, gb, w_out)


def _mlp_loss(h1, g2, w_up, w_down, target, gf):
    t = h1.shape[0]
    tm, tf = min(ROW_TILE, t), 1024

    def kern(h1_ref, g_ref, wu_ref, wd_ref, t_ref, gf_ref, hn_ref, up_ref, u2_ref, dh_ref, dhb_ref, loss_ref, dg_ref):
        @pl.when(pl.program_id(0) == 0)
        def _():
            loss_ref[...] = jnp.zeros_like(loss_ref)
            dg_ref[...] = jnp.zeros_like(dg_ref)

        hv = h1_ref[...]
        hn = (hv * _rstd(hv) * g_ref[...]).astype(BF16)
        hn_ref[...] = hn
        h2 = hv
        for f in range(D_FF // tf):
            cols = slice(f * tf, (f + 1) * tf)
            up = jnp.maximum(_dot(hn, wu_ref[:, cols]), 0.0)
            u2 = (up * up).astype(BF16)
            up_ref[:, cols] = up.astype(BF16)
            u2_ref[:, cols] = u2
            h2 = h2 + _dot(u2, wd_ref[cols, :])

        g = gf_ref[...]
        err = h2 * _rstd(h2) * g - t_ref[...]
        lane = lax.broadcasted_iota(jnp.int32, (1, LANES), 1)
        loss_ref[...] += jnp.where(lane == 0, 0.5 * jnp.sum(err * err) / D_MODEL, 0.0)
        dx, dg = _rms_bwd(h2, g, err * (1.0 / D_MODEL))
        dh_ref[...] = dx
        dhb_ref[...] = dx.astype(BF16)
        dg_ref[...] += dg

    row = lambda c: pl.BlockSpec((tm, c), lambda i: (i, 0))
    vec = pl.BlockSpec((1, D_MODEL), lambda i: (0, 0))
    return pl.pallas_call(
        kern, grid=(t // tm,),
        in_specs=[row(D_MODEL), vec, _resident((D_MODEL, D_FF)), _resident((D_FF, D_MODEL)), row(D_MODEL), vec],
        out_specs=[row(D_MODEL), row(D_FF), row(D_FF), row(D_MODEL), row(D_MODEL), pl.BlockSpec((1, LANES), lambda i: (0, 0)), vec],
        out_shape=[jax.ShapeDtypeStruct((t, D_MODEL), BF16), jax.ShapeDtypeStruct((t, D_FF), BF16),
                   jax.ShapeDtypeStruct((t, D_FF), BF16), jax.ShapeDtypeStruct((t, D_MODEL), F32),
                   jax.ShapeDtypeStruct((t, D_MODEL), BF16), jax.ShapeDtypeStruct((1, LANES), F32),
                   jax.ShapeDtypeStruct((1, D_MODEL), F32)],
        compiler_params=_cp(("arbitrary",), VMEM_LIMIT_BIG), name="mlp_loss",
    )(h1, g2, w_up, w_down, target, gf)


def _mlp_bwd_pre(dh2, w_down, up):
    t = dh2.shape[0]
    tm, tf = min(ROW_TILE, t), 1024

    def kern(d_ref, w_ref, up_ref, o_ref):
        dv = d_ref[...]
        for f in range(D_FF // tf):
            cols = slice(f * tf, (f + 1) * tf)
            o_ref[:, cols] = (_dot_nt(dv, w_ref[cols, :]) * (2.0 * up_ref[:, cols].astype(F32))).astype(BF16)

    row = lambda c: pl.BlockSpec((tm, c), lambda i: (i, 0))
    return pl.pallas_call(
        kern, grid=(t // tm,), in_specs=[row(D_MODEL), _resident((D_FF, D_MODEL)), row(D_FF)],
        out_specs=row(D_FF), out_shape=jax.ShapeDtypeStruct((t, D_FF), BF16),
        compiler_params=_cp(("parallel",)), name="mlp_bwd_pre",
    )(dh2, w_down, up)


def _proj_bwd_norm(dys, w, x, g, resid, name, comm=None, bf16_copy=False):
    t = x.shape[0]
    tm = min(ROW_TILE, t)
    widths = [dy.shape[1] for dy in dys]
    n = len(dys)

    def body(ins, outs, _):
        dy_refs, (w_ref, x_ref, g_ref, r_ref), (dx_ref, dg_ref) = ins[:n], ins[n:], outs[:2]

        @pl.when(pl.program_id(0) == 0)
        def _():
            dg_ref[...] = jnp.zeros_like(dg_ref)

        off, dxn = 0, None
        for dy_ref, wd in zip(dy_refs, widths):
            part = _dot_nt(dy_ref[...], w_ref[:, off:off + wd])
            dxn = part if dxn is None else dxn + part
            off += wd
        dx, dg = _rms_bwd(x_ref[...], g_ref[...], dxn)
        dx = r_ref[...] + dx
        dx_ref[...] = dx
        dg_ref[...] += dg
        if bf16_copy:
            outs[2][...] = dx.astype(BF16)

    row = lambda c: pl.BlockSpec((tm, c), lambda i: (i, 0))
    vec = pl.BlockSpec((1, D_MODEL), lambda i: (0, 0))
    outs, got = _call(
        body, comm, grid=(t // tm,), in_specs=[row(wd) for wd in widths] + [_resident(w.shape), row(D_MODEL), vec, row(D_MODEL)],
        out_specs=[row(D_MODEL), vec] + [row(D_MODEL)] * bf16_copy,
        out_shape=[jax.ShapeDtypeStruct((t, D_MODEL), F32), jax.ShapeDtypeStruct((1, D_MODEL), F32)]
        + [jax.ShapeDtypeStruct((t, D_MODEL), BF16)] * bf16_copy,
        scratch_shapes=[], args=(*dys, w, x, g, resid), name=name)
    return outs[0], (outs[2] if bf16_copy else None), outs[1], got


def _dw_in(xn, pieces):
    t = xn.shape[0]
    tk = min(2 * ROW_TILE, t)
    widths = [p.shape[1] for p in pieces]
    shard = IN_COLS // N_CHIPS

    def windows(j):
        out, off = [], 0
        for i, wd in enumerate(widths):
            a, b = max(j * shard, off), min((j + 1) * shard, off + wd)
            if a < b:
                assert (a - off) % LANES == 0 and (b - off) % LANES == 0
                out.append((i, a - off, b - off))
            off += wd
        return out

    def kern(x_ref, *refs):
        p_refs, o_ref = refs[:-1], refs[-1]

        @pl.when(pl.program_id(0) == 0)
        def _():
            o_ref[...] = jnp.zeros_like(o_ref)

        xv = x_ref[...]
        for j in range(N_CHIPS):
            cols = jnp.concatenate([p_refs[i][:, a:b] for i, a, b in windows(j)], axis=1)
            o_ref[j] += _dot_tn(xv, cols)

    row = lambda c: pl.BlockSpec((tk, c), lambda k: (k, 0))
    return pl.pallas_call(
        kern, grid=(t // tk,), in_specs=[row(D_MODEL)] + [row(wd) for wd in widths],
        out_specs=pl.BlockSpec((N_CHIPS, D_MODEL, shard), lambda k: (0, 0, 0)),
        out_shape=jax.ShapeDtypeStruct((N_CHIPS, D_MODEL, shard), F32),
        compiler_params=_cp(("arbitrary",)), name="dw_in",
    )(xn, *pieces)


def _outproj_bwd(dh1, w_out, y_lru, o, ga, gb, comm=None):
    t = dh1.shape[0]
    tm = min(ROW_TILE, t)

    def body(ins, outs, _):
        (d_ref, w_ref, y_ref, o_ref, ga_ref, gb_ref), (dy_ref, do_ref, dga_ref, dgb_ref) = ins, outs

        @pl.when(pl.program_id(0) == 0)
        def _():
            dga_ref[...] = jnp.zeros_like(dga_ref)
            dgb_ref[...] = jnp.zeros_like(dgb_ref)

        dmix = _dot_nt(d_ref[...], w_ref[...])
        dy, dga = _rms_bwd(y_ref[...], ga_ref[...], dmix[:, :LRU_WIDTH])
        do, dgb = _rms_bwd(o_ref[...], gb_ref[...], dmix[:, LRU_WIDTH:])
        dy_ref[...] = dy
        do_ref[...] = do
        dga_ref[...] += dga
        dgb_ref[...] += dgb

    row = lambda c: pl.BlockSpec((tm, c), lambda i: (i, 0))
    vec = pl.BlockSpec((1, LRU_WIDTH), lambda i: (0, 0))
    half = jax.ShapeDtypeStruct((t, LRU_WIDTH), F32)
    gsum = jax.ShapeDtypeStruct((1, LRU_WIDTH), F32)
    outs, got = _call(body, comm, grid=(t // tm,),
                      in_specs=[row(D_MODEL), _resident((D_MODEL, D_MODEL)), row(LRU_WIDTH), row(SB_WIDTH), vec, vec],
                      out_specs=[row(LRU_WIDTH), row(SB_WIDTH), vec, vec], out_shape=[half, half, gsum, gsum],
                      scratch_shapes=[], args=(dh1, w_out, y_lru, o, ga, gb), name="outproj_bwd")
    return (*outs, got)


def _attn_bwd(qkv, do, tot, kmin, seq):
    t = qkv.shape[0]
    ne, nq, nk = t // seq, seq // TQ, seq // TK

    def kern(q_ref, k_ref, v_ref, do_ref, tot_ref, kmin_ref, dq_ref, dk_ref, dv_ref,
             kbd_scr, vtbd_scr, kbd2_scr, dkt_scr, dvt_scr):
        causal = _att_consts()
        upto, before = _sum_matrix("upto"), _sum_matrix("before")

        def prep(j, _):
            k0 = pl.multiple_of(j * TK, TK)
            kb = k_ref[pl.ds(k0, TK), :]
            top, bot = _head_diag(kb.T, True)
            kbd_scr[j] = jnp.concatenate([top, bot], axis=1)
            top, bot = _head_diag(v_ref[pl.ds(k0, TK), :].T, True)
            vtbd_scr[j] = jnp.concatenate([top, bot], axis=1)
            left, right = _head_diag(kb, False)
            kbd2_scr[j] = jnp.concatenate([left, right], axis=0)
            dkt_scr[j] = jnp.zeros((LANES, 2 * TK), F32)
            dvt_scr[j] = jnp.zeros((LANES, 2 * TK), F32)
            return 0

        lax.fori_loop(0, nk, prep, 0)

        def block(j, qb, qt, dob, dot_, totb, st, mask):
            f0, f1, p0, p1, dqacc = st
            lb, l1 = _att_logits(qb, kbd_scr[j])
            if mask is not None:
                l1 = jnp.where(mask, l1, 0.0)
            (s0, s1), (r0, r1) = _pair_sums(l1, upto)
            att = jnp.exp(lb + (totb - jnp.concatenate([s0 + f0, s1 + f1], axis=1)))
            if mask is not None:
                att = jnp.where(mask, att, 0.0)
            pw = att * _dot(dob, vtbd_scr[j])
            (e0, e1), (t0, t1) = _pair_sums(pw, before)
            dz = pw - jnp.exp(lb) * (pw + jnp.concatenate([e0 + p0, e1 + p1], axis=1))
            if mask is not None:
                dz = jnp.where(mask, dz, 0.0)
            dzb = dz.astype(BF16)
            dkt_scr[j] += _dot(qt, dzb)
            dvt_scr[j] += _dot(dot_, att.astype(BF16))
            return f0 + r0, f1 + r1, p0 + t0, p1 + t1, dqacc + _dot(dzb, kbd2_scr[j])

        def qloop(qi, _):
            q0 = pl.multiple_of(qi * TQ, TQ)
            qb = _scaled_q(q_ref, q0)
            qt = qb.T
            do32 = do_ref[pl.ds(q0, TQ), :]
            dob = do32.astype(BF16)
            dot_ = dob.T
            totb = tot_ref[pl.ds(q0, TQ), :]
            zero = jnp.zeros((TQ, TK), F32)
            st = (zero, zero, zero, zero, jnp.zeros((TQ, LANES), F32))

            k0 = kmin_ref[pl.program_id(0), pl.program_id(1), qi]

            def tile(j, lo, hi, st, masked):
                new = block(j, qb[lo:hi], qt[:, lo:hi], dob[lo:hi], dot_[:, lo:hi], totb[lo:hi],
                            tuple(x[lo:hi] for x in st), causal[:hi - lo] if masked else None)
                return _rows_update(st, new, lo, hi)

            def general():
                def kloop(it, st):
                    j = k0 + 2 * it
                    return block(j + 1, qb, qt, dob, dot_, totb, block(j, qb, qt, dob, dot_, totb, st, None), None)

                out = lax.fori_loop(0, ((TQ // TK) * qi - k0) // 2, kloop, st)
                for jj in range(TQ // TK):
                    out = tile((TQ // TK) * qi + jj, TK * jj, TQ, out, True)
                return out

            def short():
                out = st
                for jr, lo, hi in reversed(_band_tiles()):
                    out = tile((TQ // TK) * qi + jr, lo, hi, out, jr >= 0)
                return out

            st = lax.cond(k0 < 0, short, general)
            dq_ref[pl.ds(q0, TQ), :] = (st[4] * ATT_SCALE).astype(BF16)
            return 0

        lax.fori_loop(0, nq, qloop, 0)

        def finish(j, _):
            k0 = pl.multiple_of(j * TK, TK)
            head0 = lax.broadcasted_iota(jnp.int32, (LANES, TK), 0) < DH
            for src, dst in ((dkt_scr, dk_ref), (dvt_scr, dv_ref)):
                acc = src[j]
                dst[pl.ds(k0, TK), :] = jnp.where(head0, acc[:, :TK], acc[:, TK:]).astype(BF16).T
            return 0

        lax.fori_loop(0, nk, finish, 0)

    blk = pl.BlockSpec((seq, LANES), lambda e, p: (e, p))
    grad = jax.ShapeDtypeStruct((t, SB_WIDTH), BF16)
    return pl.pallas_call(
        kern, grid=(ne, SB_WIDTH // LANES),
        in_specs=_qkv_specs(seq) + [blk, pl.BlockSpec((seq, 2 * TK), lambda e, p: (e, p)),
                                    pl.BlockSpec(memory_space=pltpu.SMEM)],
        out_specs=[blk, blk, blk], out_shape=[grad, grad, grad],
        scratch_shapes=[pltpu.VMEM((nk, LANES, 2 * TK), BF16), pltpu.VMEM((nk, LANES, 2 * TK), BF16),
                        pltpu.VMEM((nk, 2 * TK, LANES), BF16), pltpu.VMEM((nk, LANES, 2 * TK), F32),
                        pltpu.VMEM((nk, LANES, 2 * TK), F32)],
        compiler_params=_cp(("parallel", "parallel")), name="attn_bwd",
    )(qkv, qkv, qkv, do, tot, kmin)


def _lru_bwd(xl, h, dy, conv_w, conv_b, wbd, ba, bx, lam, seq, comm=None):
    t = xl.shape[0]
    tc = min(512, seq)
    nc = seq // tc
    nb = tc // SUBLANES


    def body(ins, outs, scr):
        u_ref, g_ref, up_ref, h_ref, hp_ref, dy_ref, cw_ref, cb_ref, wbd_ref, ba_ref, bx_ref, lam_ref = ins
        (dxl_ref, small_ref, dwbd_ref), (lnext_ref, anext_ref, dcnext_ref) = outs, scr
        e, ci = pl.program_id(0), pl.program_id(1)
        first = ci == nc - 1

        @pl.when((e == 0) & (ci == 0))
        def _():
            small_ref[...] = jnp.zeros_like(small_ref)
            dwbd_ref[...] = jnp.zeros_like(dwbd_ref)

        @pl.when(ci == 0)
        def _():
            lnext_ref[...] = jnp.zeros_like(lnext_ref)
            anext_ref[...] = jnp.zeros_like(anext_ref)
            dcnext_ref[...] = jnp.zeros_like(dcnext_ref)

        u, g = u_ref[...], g_ref[...]
        keep = jnp.where(first, 0.0, 1.0)
        taps = _conv_taps(keep * up_ref[...], u)
        c = cb_ref[...]
        for k in range(CONV_WIDTH):
            c = c + taps[k] * cw_ref[k:k + 1, :]
        lam = lam_ref[...]
        sp = _softplus(-lam)
        r, i, a, mult, inv_mult = _lru_gates(c, wbd_ref, ba_ref[...], bx_ref[...], sp)
        gel, th = _gelu(g)
        dyv, hv = dy_ref[...], h_ref[...]
        dg = dyv * hv * _gelu_grad(g, th)

        aa, bb = _scan_rev(_shift_up(a, 1, anext_ref[0:1, :]), dyv * gel)
        lt = bb + aa * lnext_ref[0:1, :]
        lnext_ref[0:1, :] = _row_of(lt, 0)
        anext_ref[0:1, :] = _row_of(a, 0)

        hprev = _shift_down(hv, 1, keep * hp_ref[SUBLANES - 1:SUBLANES, :])
        da = lt * hprev
        dmult = lt * i * c
        di = lt * mult * c
        dc = lt * mult * i
        dla = da * a - dmult * (a * a) * inv_mult
        dga = dla * ((-LRU_C) * sp) * r * (1.0 - r)
        dgx = di * i * (1.0 - i)
        small_ref[7:8, :] += jnp.sum(dla * r, axis=0, keepdims=True) * (LRU_C * _sigmoid(-lam))
        small_ref[5:6, :] += jnp.sum(dga, axis=0, keepdims=True)
        small_ref[6:7, :] += jnp.sum(dgx, axis=0, keepdims=True)

        dcs = []
        for p in range(LRU_WIDTH // LANES):
            cols = slice(LANES * p, LANES * (p + 1))
            dgax = jnp.concatenate([dga[:, cols], dgx[:, cols]], axis=1).astype(BF16)
            dcs.append(_dot_nt(dgax, wbd_ref[p]))
            dwbd_ref[p] += _dot_tn(c[:, cols].astype(BF16), dgax)
        dc = dc + jnp.concatenate(dcs, axis=1)
        small_ref[4:5, :] += jnp.sum(dc, axis=0, keepdims=True)

        catd = jnp.concatenate([dc, dcnext_ref[...]], axis=0)
        du = dc * cw_ref[CONV_WIDTH - 1:CONV_WIDTH, :]
        for j in range(1, CONV_WIDTH):
            du = du + pltpu.roll(catd, tc + SUBLANES - j, 0)[:tc] * cw_ref[CONV_WIDTH - 1 - j:CONV_WIDTH - j, :]
        dcnext_ref[...] = dc[:SUBLANES]
        for k in range(CONV_WIDTH):
            small_ref[k:k + 1, :] += jnp.sum(dc * taps[k], axis=0, keepdims=True)
        dxl_ref[:, :LRU_WIDTH] = du.astype(BF16)
        dxl_ref[:, LRU_WIDTH:] = dg.astype(BF16)

    rev = lambda e, c: e * nc + (nc - 1 - c)
    chunk = lambda col: pl.BlockSpec((tc, LRU_WIDTH), lambda e, c: (rev(e, c), col))
    prev8 = pl.BlockSpec((SUBLANES, LRU_WIDTH), lambda e, c: (jnp.maximum(rev(e, c) * nb - 1, 0), 0))
    outs, got = _call(
        body, comm, grid=(t // seq, nc),
        in_specs=[chunk(0), chunk(1), prev8, chunk(0), prev8, chunk(0)] + _lru_param_specs(2),
        out_specs=[pl.BlockSpec((tc, 2 * LRU_WIDTH), lambda e, c: (rev(e, c), 0)),
                   pl.BlockSpec((SUBLANES, LRU_WIDTH), lambda e, c: (0, 0)),
                   pl.BlockSpec((LRU_WIDTH // LANES, LANES, 2 * LANES), lambda e, c: (0, 0, 0))],
        out_shape=[jax.ShapeDtypeStruct((t, 2 * LRU_WIDTH), BF16), jax.ShapeDtypeStruct((SUBLANES, LRU_WIDTH), F32),
                   jax.ShapeDtypeStruct((LRU_WIDTH // LANES, LANES, 2 * LANES), F32)],
        scratch_shapes=[pltpu.VMEM((SUBLANES, LRU_WIDTH), F32)] * 3,
        args=(xl, xl, xl, h, h, dy, conv_w, conv_b, wbd, ba, bx, lam), name="lru_bwd")
    return (*outs, got)


def _adam_math(w, g, m, v):
    m2 = ADAM_B1 * m + (1.0 - ADAM_B1) * g
    v2 = ADAM_B2 * v + (1.0 - ADAM_B2) * (g * g)
    m_hat = m2 / (1.0 - ADAM_B1 ** ADAM_STEP)
    v_hat = v2 / (1.0 - ADAM_B2 ** ADAM_STEP)
    return -ADAM_LR * (m_hat / (jnp.sqrt(v_hat) + ADAM_EPS) + ADAM_WD * w), m2, v2


def _adamw(w, g, m, v, name):
    rows, cols = w.shape
    tr = 256 if rows % 256 == 0 else rows

    def kern(w_ref, g_ref, m_ref, v_ref, d_ref, m2_ref, v2_ref):
        d_ref[...], m2_ref[...], v2_ref[...] = _adam_math(w_ref[...], g_ref[...], m_ref[...], v_ref[...])

    blk = pl.BlockSpec((tr, cols), lambda i: (i, 0))
    out = jax.ShapeDtypeStruct((rows, cols), F32)
    return pl.pallas_call(kern, grid=(rows // tr,), in_specs=[blk] * 4, out_specs=[blk] * 3, out_shape=[out] * 3,
                          compiler_params=_cp(("parallel",)), name=name)(w, g, m, v)


def _adamw_small(ws, gs, ms, vs):
    n = len(ws)

    def kern(*refs):
        for k in range(n):
            outs = _adam_math(refs[k][...], refs[n + k][...], refs[2 * n + k][...], refs[3 * n + k][...])
            for j in range(3):
                refs[(4 + j) * n + k][...] = outs[j]

    vm = pl.BlockSpec(memory_space=pltpu.VMEM)
    out = pl.pallas_call(kern, in_specs=[vm] * (4 * n), out_specs=[vm] * (3 * n),
                         out_shape=[jax.ShapeDtypeStruct(w.shape, F32) for w in ws] * 3, name="adamw_small")(*ws, *gs, *ms, *vs)
    return out[:n], out[n:2 * n], out[2 * n:]


def _comm_only(kind, arrays, name):
    return _call(lambda ins, outs, scr: None, (kind, arrays), grid=(1,), in_specs=[], out_specs=[], out_shape=[],
                 scratch_shapes=[], args=(), name=name)[1]


def _pair_add(g, got, core):
    _, rows, cols = g.shape
    half = rows // 2
    tr = min(256, half)
    nt = half // tr

    def kern(c_ref, g_ref, o_ref, out_ref):
        out_ref[...] = (g_ref[...] + o_ref[...]).astype(BF16)

    return pl.pallas_call(
        kern, grid_spec=pltpu.PrefetchScalarGridSpec(
            num_scalar_prefetch=1, grid=(N_CHIPS, nt),
            in_specs=[pl.BlockSpec((None, tr, cols), lambda j, i, c_ref: (j, c_ref[0] * nt + i, 0)),
                      pl.BlockSpec((None, tr, cols), lambda j, i, c_ref: (j, i, 0))],
            out_specs=pl.BlockSpec((None, tr, cols), lambda j, i, c_ref: (j, i, 0))),
        out_shape=jax.ShapeDtypeStruct((N_CHIPS, half, cols), BF16),
        compiler_params=_cp(("parallel", "parallel")), name="pair_add",
    )(core, g, got)


def _chip_add(part, got, place):
    _, half, cols = part.shape
    tr = min(256, half)
    nt = half // tr

    def kern(p_ref, part_ref, got_ref, out_ref):
        out_ref[...] = (part_ref[...].astype(F32) + got_ref[0].astype(F32) + got_ref[1].astype(F32)
                        + got_ref[2].astype(F32))

    return pl.pallas_call(
        kern, grid_spec=pltpu.PrefetchScalarGridSpec(
            num_scalar_prefetch=1, grid=(nt,),
            in_specs=[pl.BlockSpec((None, tr, cols), lambda i, p_ref: (p_ref[0], i, 0)),
                      pl.BlockSpec((3, tr, cols), lambda i, p_ref: (0, i, 0))],
            out_specs=pl.BlockSpec((tr, cols), lambda i, p_ref: (p_ref[1] * nt + i, 0))),
        out_shape=jax.ShapeDtypeStruct((2 * half, cols), F32),
        compiler_params=_cp(("parallel",)), name="chip_add",
    )(place, part, got)


def _finale(packed, fulls):
    rows, n = packed.shape[0], len(fulls)
    half = rows // 2
    assert half % SUBLANES == 0

    def kern(in_ref, *refs):
        ins, out_ref, outs = refs[:n], refs[n], refs[n + 1:2 * n + 1]
        pair_slot, chip_slots, pair_sems, chip_send, chip_recv, back_sems, join_send, join_recv = refs[2 * n + 1:]
        x, y, c = _place()
        chip, sibling = 2 * x + y, (x, y, 1 - c)
        mine = pl.ds(pl.multiple_of(c * half, SUBLANES), half)
        joins = []
        for w in range(n):
            rws = pl.ds(c * (ins[w].shape[0] // 2), ins[w].shape[0] // 2)
            joins.append(pltpu.make_async_remote_copy(
                src_ref=ins[w].at[rws, :], dst_ref=outs[w].at[rws, :], send_sem=join_send.at[w],
                recv_sem=join_recv.at[w], device_id=sibling, device_id_type=MESH))
        swap = pltpu.make_async_remote_copy(src_ref=in_ref, dst_ref=pair_slot, send_sem=pair_sems.at[0],
                                            recv_sem=pair_sems.at[1], device_id=sibling, device_id_type=MESH)
        for cp in joins + [swap]:
            cp.start()
        swap.wait()
        chip_slots[chip] = in_ref[mine, :] + pair_slot[mine, :]
        spread = [pltpu.make_async_remote_copy(
            src_ref=chip_slots.at[chip], dst_ref=chip_slots.at[chip], send_sem=chip_send.at[k], recv_sem=chip_recv.at[k],
            device_id=(*other, c), device_id_type=MESH) for k, other in enumerate(_other_chips(x, y))]
        for cp in spread:
            cp.start()
        for cp in spread:
            cp.wait()
        out_ref[mine, :] = chip_slots[0] + chip_slots[1] + chip_slots[2] + chip_slots[3]
        back = pltpu.make_async_remote_copy(src_ref=out_ref.at[mine, :], dst_ref=out_ref.at[mine, :], send_sem=back_sems.at[0],
                                            recv_sem=back_sems.at[1], device_id=sibling, device_id_type=MESH)
        back.start()
        for cp in [back] + joins:
            cp.wait()

    vm = pl.BlockSpec(memory_space=pltpu.VMEM)
    out = pl.pallas_call(
        kern, in_specs=[vm] + [ANY] * n, out_specs=[vm] + [ANY] * n,
        out_shape=[jax.ShapeDtypeStruct((rows, LANES), F32)] + [jax.ShapeDtypeStruct(f.shape, f.dtype) for f in fulls],
        input_output_aliases={w + 1: w + 1 for w in range(n)},
        scratch_shapes=[pltpu.VMEM((rows, LANES), F32), pltpu.VMEM((N_CHIPS, half, LANES), F32),
                        pltpu.SemaphoreType.DMA((2,)), pltpu.SemaphoreType.DMA((3,)), pltpu.SemaphoreType.DMA((3,)),
                        pltpu.SemaphoreType.DMA((2,)), pltpu.SemaphoreType.DMA((n,)), pltpu.SemaphoreType.DMA((n,))],
        name="finale",
    )(packed, *fulls)
    return out[0], list(out[1:])


SMALL = ["norm1_g", "conv_w", "conv_b", "lru_w_a", "lru_b_a", "lru_w_x", "lru_b_x", "lru_lambda", "lru_out_g", "sb_out_g",
         "norm2_g", "final_g"]
BIG = ["w_in", "w_out", "w_up", "w_down"]
WEIGHTS = ["norm1_g", "w_in", "conv_w", "conv_b", "lru_w_a", "lru_b_a", "lru_w_x", "lru_b_x", "lru_lambda", "lru_out_g",
           "sb_out_g", "w_out", "norm2_g", "w_up", "w_down", "final_g"]


def _pack(arrays):
    flat = []
    for a in arrays:
        a = a.reshape(-1).astype(F32)
        flat.append(jnp.pad(a, (0, (-a.shape[0]) % LANES)))
    v = jnp.concatenate(flat)
    v = jnp.pad(v, (0, (-v.shape[0]) % (LANES * 2 * SUBLANES)))
    return v.reshape(-1, LANES)


def _unpack(packed, shapes):
    v, out, off = packed.reshape(-1), [], 0
    for shp in shapes:
        size = math.prod(shp)
        out.append(v[off:off + size].reshape(shp))
        off += size + (-size) % LANES
    return out


def _blockdiag_pairs(w):
    w = w.reshape(4, 2, DH, DH)
    z = jnp.zeros((4, DH, DH), w.dtype)
    return jnp.concatenate([jnp.concatenate([w[:, 0], z], axis=2), jnp.concatenate([z, w[:, 1]], axis=2)], axis=1)


def _blockdiag_unpairs(wbd):
    return jnp.stack([wbd[:, :DH, :DH], wbd[:, DH:, DH:]], axis=1).reshape(8, DH, DH)


def _full_cols(g):
    return jnp.transpose(g, (1, 0, 2)).reshape(g.shape[1], N_CHIPS * g.shape[2])


def _local_step(x2, tgt, seq, norm1_g, w_in, conv_w, conv_b, w_a, b_a, w_x, b_x, lru_lambda, lru_out_g, sb_out_g, rest,
                norm2_g, final_g, place=None):
    alone = place is None
    wbd = jnp.concatenate([_blockdiag_pairs(w_a), _blockdiag_pairs(w_x)], axis=2).astype(BF16)
    ba, bx = b_a.reshape(1, LRU_WIDTH), b_x.reshape(1, LRU_WIDTH)
    gf = final_g.reshape(1, D_MODEL)

    xn, got = _norm1(x2, norm1_g, None if alone else ("gather2", [w_in]))
    w_in_f = w_in if alone else _full_cols(got[0])
    xl, qkv, got = _inproj(xn, w_in_f, None if alone else ("gather", [conv_w]))
    conv_w_f = conv_w if alone else _full_cols(got[0])
    h, y_lru = _lru_fwd(xl, conv_w_f, conv_b, wbd, ba, bx, lru_lambda, seq)
    o, tot, kmin, got = _attn_fwd(qkv, seq, None if alone else ("gather2", rest))
    w_out_f, w_up_f, w_down_f = rest if alone else (
        got[0].reshape(D_MODEL, D_MODEL), _full_cols(got[1]), got[2].reshape(D_FF, D_MODEL))
    h1, mix = _outproj(y_lru, o, x2, lru_out_g, sb_out_g, w_out_f)
    hn, up, u2, dh2, dh2b, loss_part, d_final = _mlp_loss(h1, norm2_g, w_up_f, w_down_f, tgt, gf)

    dpre = _mlp_bwd_pre(dh2b, w_down_f, up)
    g_w_down = _matmul(u2, dh2b, "tn", F32, "dw_down", 1024, 1024, 1024).reshape(N_CHIPS, D_FF // N_CHIPS, D_MODEL)
    g_w_up = _matmul(hn, dpre, "tn", F32, "dw_up", 1024, D_FF // N_CHIPS, 1024, split_cols=True)
    dh1, dh1b, d_norm2, _ = _proj_bwd_norm([dpre], w_up_f, h1, norm2_g, dh2, "mlp_bwd_in", bf16_copy=True)
    g_w_out = _matmul(mix, dh1b, "tn", F32, "dw_out", 1024, 1024, 2048).reshape(N_CHIPS, D_MODEL // N_CHIPS, D_MODEL)
    late = [g_w_out, g_w_up, g_w_down]
    dy_lru, do, d_ga, d_gb, swapped = _outproj_bwd(dh1b, w_out_f, y_lru, o, lru_out_g, sb_out_g,
                                                   None if alone else ("swap", late))
    parts = None if alone else [_pair_add(g, r, place[1:]) for g, r in zip(late, swapped)]
    dq, dk, dv = _attn_bwd(qkv, do, tot, kmin, seq)
    dxl, lru_small, d_wbd, got = _lru_bwd(xl, h, dy_lru, conv_w_f, conv_b, wbd, ba, bx, lru_lambda, seq,
                                          None if alone else ("exchange", parts))
    if not alone:
        late = [_chip_add(p, r, place) for p, r in zip(parts, got)]
    g_w_in = _dw_in(xn, [dxl, dq, dk, dv])
    part = None if alone else _pair_add(g_w_in, _comm_only("swap", [g_w_in], "pair_swap")[0], place[1:])
    dx, _, d_norm1, got = _proj_bwd_norm([dxl, dq, dk, dv], w_in_f, x2, norm1_g, dh1, "inproj_bwd",
                                         None if alone else ("exchange", [part]))
    if not alone:
        g_w_in = _chip_add(part, got[0], place)
    small_parts = {
        "norm1_g": d_norm1, "conv_w": lru_small[:CONV_WIDTH], "conv_b": lru_small[4:5],
        "lru_w_a": _blockdiag_unpairs(d_wbd[:, :, :LANES]), "lru_b_a": lru_small[5:6],
        "lru_w_x": _blockdiag_unpairs(d_wbd[:, :, LANES:]), "lru_b_x": lru_small[6:7], "lru_lambda": lru_small[7:8],
        "lru_out_g": d_ga, "sb_out_g": d_gb, "norm2_g": d_norm2, "final_g": d_final,
    }
    return loss_part, dx, [g_w_in] + late, small_parts


def kernel(x, norm1_g, w_in, conv_w, conv_b, lru_w_a, lru_b_a, lru_w_x, lru_b_x, lru_lambda, lru_out_g, sb_out_g, w_out, norm2_g, w_up, w_down, final_g, loss_target, m_norm1_g, m_w_in, m_conv_w, m_conv_b, m_lru_w_a, m_lru_b_a, m_lru_w_x, m_lru_b_x, m_lru_lambda, m_lru_out_g, m_sb_out_g, m_w_out, m_norm2_g, m_w_up, m_w_down, m_final_g, v_norm1_g, v_w_in, v_conv_w, v_conv_b, v_lru_w_a, v_lru_b_a, v_lru_w_x, v_lru_b_x, v_lru_lambda, v_lru_out_g, v_sb_out_g, v_w_out, v_norm2_g, v_w_up, v_w_down, v_final_g):
    given = dict(locals())
    ne, seq, _ = x.shape
    t = ne * seq
    xi, yi, ci = _place()
    place = jnp.stack([2 * xi + yi, ci]).astype(jnp.int32)

    loss_part, dx, halves, small_parts = _local_step(
        x.reshape(t, D_MODEL), loss_target.reshape(t, D_MODEL), seq, norm1_g, w_in[0].astype(BF16), conv_w[0], conv_b,
        lru_w_a[0], lru_b_a, lru_w_x[0], lru_b_x, lru_lambda, lru_out_g, sb_out_g,
        [w_out[0].astype(BF16), w_up[0].astype(BF16), w_down[0].astype(BF16)], norm2_g, final_g, place)

    full_shapes = {n: ((CONV_WIDTH, LRU_WIDTH) if n == "conv_w" else given[n].shape) for n in SMALL}
    red, fulls = _finale(_pack([small_parts[n] for n in SMALL] + [loss_part]), halves)
    red_list = _unpack(red, [full_shapes[n] for n in SMALL] + [(1, LANES)])
    grads = dict(zip(SMALL, red_list[:-1]))
    loss = red_list[-1][0, 0]
    grads["conv_w"] = lax.dynamic_slice_in_dim(grads["conv_w"], place[0] * (LRU_WIDTH // N_CHIPS), LRU_WIDTH // N_CHIPS,
                                               axis=1).reshape(conv_w.shape)
    for n, full in zip(BIG, fulls):
        grads[n] = full.reshape(given[n].shape)

    delta, new_m, new_v = {}, {}, {}
    for n in BIG:
        shp = given[n].shape
        d, m2, v2 = _adamw(given[n][0], grads[n][0], given["m_" + n][0], given["v_" + n][0], "adamw_" + n)
        delta[n], new_m[n], new_v[n] = d.reshape(shp), m2.reshape(shp), v2.reshape(shp)
    as2d = lambda a: a.reshape(-1, a.shape[-1])
    ds, m2s, v2s = _adamw_small([as2d(given[n]) for n in SMALL], [as2d(grads[n]) for n in SMALL],
                                [as2d(given["m_" + n]) for n in SMALL], [as2d(given["v_" + n]) for n in SMALL])
    for n, dd, mm, vv in zip(SMALL, ds, m2s, v2s):
        shp = given[n].shape
        delta[n], new_m[n], new_v[n] = dd.reshape(shp), mm.reshape(shp), vv.reshape(shp)

    return (loss, dx.reshape(x.shape), *[grads[n] for n in WEIGHTS], *[delta[n] for n in WEIGHTS],
            *[new_m[n] for n in WEIGHTS], *[new_v[n] for n in WEIGHTS])
```

```python
import functools
import math

import jax
import jax.numpy as jnp
from jax import lax
from jax.experimental import pallas as pl
from jax.experimental.pallas import tpu as pltpu

F32, BF16 = jnp.float32, jnp.bfloat16
MESH = pl.DeviceIdType.MESH

D_MODEL = 1024
LRU_WIDTH = 512
SB_WIDTH = 512
DH = 64
IN_COLS = 2 * LRU_WIDTH + 3 * SB_WIDTH
D_FF = 4 * D_MODEL
CONV_WIDTH = 4
LRU_C = 8.0
EPS = 1e-6
N_CHIPS = 4
N_DEV = 8
LANES = 128
SUBLANES = 8
ROW_TILE = 512
GRAD_TILE = 1024
TQ = 512
TK = 128
ATT_SCALE = 1.0 / math.sqrt(DH)
SKIP_LOG = -105.0
BAND = 2
VMEM_LIMIT = 52 * 1024 * 1024
VMEM_LIMIT_BIG = 62 * 1024 * 1024

ADAM_LR, ADAM_B1, ADAM_B2, ADAM_EPS, ADAM_WD, ADAM_STEP = 0.001, 0.9, 0.999, 1e-08, 0.01, 10

_GELU_K = math.sqrt(2.0 / math.pi)
_GELU_C = 0.044715


def _cp(sem, vmem=VMEM_LIMIT):
    return pltpu.CompilerParams(dimension_semantics=sem, vmem_limit_bytes=vmem)


def _dot(a, b):
    return jnp.dot(a, b, preferred_element_type=F32)


def _dot_nt(a, b):
    return lax.dot_general(a, b, (((1,), (1,)), ((), ())), preferred_element_type=F32)


def _dot_tn(a, b):
    return lax.dot_general(a, b, (((0,), (0,)), ((), ())), preferred_element_type=F32)


def _rstd(x):
    return lax.rsqrt(jnp.mean(x * x, axis=-1, keepdims=True) + EPS)


def _rms_bwd(x, g, dy):
    r = _rstd(x)
    gd = g * dy
    dx = r * gd - x * (r * r * r) * jnp.mean(x * gd, axis=-1, keepdims=True)
    return dx, jnp.sum(dy * x * r, axis=0, keepdims=True)


def _sigmoid(x):
    return 0.5 * jnp.tanh(0.5 * x) + 0.5


def _softplus(x):
    return jnp.maximum(x, 0.0) + jnp.log(1.0 + jnp.exp(-jnp.abs(x)))


def _neg_expm1(x, ex):
    series = -x * (1.0 + x * (0.5 + x * (1.0 / 6.0)))
    return jnp.where(x > -2.0 ** -7, series, 1.0 - ex)


def _gelu(g):
    t = jnp.tanh(_GELU_K * (g + _GELU_C * g * g * g))
    return 0.5 * g * (1.0 + t), t


def _gelu_grad(g, t):
    return 0.5 * (1.0 + t) + 0.5 * g * (1.0 - t * t) * _GELU_K * (1.0 + 3.0 * _GELU_C * g * g)


def _rows(shape):
    return lax.broadcasted_iota(jnp.int32, shape, 0)


def _shift_down(x, s, fill):
    n, c = x.shape
    if s % SUBLANES == 0:
        return jnp.concatenate([jnp.broadcast_to(jnp.asarray(fill, x.dtype), (s, c)), x[:n - s]], axis=0)
    return jnp.where(_rows(x.shape) >= s, pltpu.roll(x, s, 0), fill)


def _shift_up(x, s, fill):
    n, c = x.shape
    if s % SUBLANES == 0:
        return jnp.concatenate([x[s:], jnp.broadcast_to(jnp.asarray(fill, x.dtype), (s, c))], axis=0)
    return jnp.where(_rows(x.shape) < n - s, pltpu.roll(x, n - s, 0), fill)


def _row_of(x, idx):
    return jnp.sum(jnp.where(_rows(x.shape) == idx, x, 0.0), axis=0, keepdims=True)


def _weight_grad(a, b, name, tk, split_cols=False):
    (kk, m), n = a.shape, b.shape[1]
    tm, tn, tk = min(GRAD_TILE, m), min(GRAD_TILE, n), min(tk, kk)
    assert m % tm == 0 and n % tn == 0 and kk % tk == 0, (name, m, n, kk)
    nk = kk // tk

    def kern(a_ref, b_ref, o_ref, acc_ref):
        k = pl.program_id(2)

        @pl.when(k == 0)
        def _():
            acc_ref[...] = jnp.zeros_like(acc_ref)

        acc_ref[...] += _dot_tn(a_ref[...], b_ref[...])

        @pl.when(k == nk - 1)
        def _():
            o_ref[...] = acc_ref[...]

    if split_cols:
        out_shape = jax.ShapeDtypeStruct((n // tn, m, tn), F32)
        o_spec = pl.BlockSpec((None, tm, tn), lambda i, j, k: (j, i, 0))
    else:
        out_shape = jax.ShapeDtypeStruct((m, n), F32)
        o_spec = pl.BlockSpec((tm, tn), lambda i, j, k: (i, j))
    return pl.pallas_call(
        kern, grid=(m // tm, n // tn, nk),
        in_specs=[pl.BlockSpec((tk, tm), lambda i, j, k: (k, i)), pl.BlockSpec((tk, tn), lambda i, j, k: (k, j))],
        out_specs=o_spec, out_shape=out_shape, scratch_shapes=[pltpu.VMEM((tm, tn), F32)],
        compiler_params=_cp(("parallel", "parallel", "arbitrary")), name=name,
    )(a, b)


ANY = pl.BlockSpec(memory_space=pl.ANY)


def _place():
    return lax.axis_index("x"), lax.axis_index("y"), lax.axis_index("c")


def _other_chips(x, y):
    return [(1 - x, y), (x, 1 - y), (1 - x, 1 - y)]


def _own_slab(shard, gathered, send_sem, recv_sem):
    x, y, c = _place()
    return pltpu.make_async_remote_copy(src_ref=shard, dst_ref=gathered.at[2 * x + y], send_sem=send_sem, recv_sem=recv_sem,
                                        device_id=(x, y, 1 - c), device_id_type=MESH)


def _gather_copies(ins, outs, send_sems, recv_sems, own_send, own_recv):
    x, y, c = _place()
    mine = 2 * x + y
    copies = []
    for w in range(len(ins)):
        copies.append(_own_slab(ins[w], outs[w], own_send.at[w], own_recv.at[w]))
        for k, chip in enumerate(_other_chips(x, y)):
            copies.append(pltpu.make_async_remote_copy(
                src_ref=ins[w], dst_ref=outs[w].at[mine], send_sem=send_sems.at[3 * w + k],
                recv_sem=recv_sems.at[3 * w + k], device_id=(*chip, c), device_id_type=MESH))
    return copies


def _gather_shapes(shards):
    return ([jax.ShapeDtypeStruct((N_CHIPS,) + s.shape, s.dtype) for s in shards],
            [pltpu.SemaphoreType.DMA((3 * len(shards),)), pltpu.SemaphoreType.DMA((3 * len(shards),)),
             pltpu.SemaphoreType.DMA((len(shards),)), pltpu.SemaphoreType.DMA((len(shards),))])


def _exchange_copies(ins, outs, send_sems, recv_sems):
    x, y, c = _place()
    copies = []
    for w in range(len(ins)):
        for k, chip in enumerate(_other_chips(x, y)):
            copies.append(pltpu.make_async_remote_copy(
                src_ref=ins[w].at[2 * chip[0] + chip[1]], dst_ref=outs[w].at[k], send_sem=send_sems.at[3 * w + k],
                recv_sem=recv_sems.at[3 * w + k], device_id=(*chip, c), device_id_type=MESH))
    return copies


def _exchange_shapes(parts):
    return ([jax.ShapeDtypeStruct((3,) + p.shape[1:], p.dtype) for p in parts],
            [pltpu.SemaphoreType.DMA((3 * len(parts),)), pltpu.SemaphoreType.DMA((3 * len(parts),))])


def _exchange8_copies(ins, outs, send_sems, recv_sems):
    x, y, c = _place()
    copies = []
    for w in range(len(ins)):
        half = ins[w].shape[1] // 2
        for k in range(1, N_DEV):
            px, py, pc = x ^ (k >> 2), y ^ ((k >> 1) & 1), c ^ (k & 1)
            copies.append(pltpu.make_async_remote_copy(
                src_ref=ins[w].at[2 * px + py, pl.ds(pc * half, half), :], dst_ref=outs[w].at[k - 1],
                send_sem=send_sems.at[(N_DEV - 1) * w + k - 1], recv_sem=recv_sems.at[(N_DEV - 1) * w + k - 1],
                device_id=(px, py, pc), device_id_type=MESH))
    return copies


def _exchange8_shapes(parts):
    n = (N_DEV - 1) * len(parts)
    return ([jax.ShapeDtypeStruct((N_DEV - 1, p.shape[1] // 2, p.shape[2]), p.dtype) for p in parts],
            [pltpu.SemaphoreType.DMA((n,)), pltpu.SemaphoreType.DMA((n,))])


def _swap_copies(ins, outs, send_sems, recv_sems):
    x, y, c = _place()
    copies = []
    for w in range(len(ins)):
        half = ins[w].shape[1] // 2
        copies.append(pltpu.make_async_remote_copy(
            src_ref=ins[w].at[:, pl.ds((1 - c) * half, half), :], dst_ref=outs[w], send_sem=send_sems.at[w],
            recv_sem=recv_sems.at[w], device_id=(x, y, 1 - c), device_id_type=MESH))
    return copies


def _swap_shapes(grads):
    return ([jax.ShapeDtypeStruct((g.shape[0], g.shape[1] // 2, g.shape[2]), g.dtype) for g in grads],
            [pltpu.SemaphoreType.DMA((len(grads),)), pltpu.SemaphoreType.DMA((len(grads),))])


def _gather2_copies(ins, outs, send_sems, recv_sems, own_send, own_recv, fwd_send, fwd_recv):
    x, y, c = _place()
    mine = 2 * x + y
    copies = []
    for w in range(len(ins)):
        half = ins[w].shape[0] // 2
        rows = pl.ds(c * half, half)
        copies.append(_own_slab(ins[w], outs[w], own_send.at[w], own_recv.at[w]))
        for k, chip in enumerate(_other_chips(x, y)):
            copies.append(pltpu.make_async_remote_copy(
                src_ref=ins[w].at[rows, :], dst_ref=outs[w].at[mine, rows, :], send_sem=send_sems.at[3 * w + k],
                recv_sem=recv_sems.at[3 * w + k], device_id=(*chip, c), device_id_type=MESH))
    return copies


def _gather2_forward(ins, outs, send_sems, recv_sems, own_send, own_recv, fwd_send, fwd_recv):
    x, y, c = _place()
    copies = []
    for w in range(len(ins)):
        half = ins[w].shape[0] // 2
        rows = pl.ds(c * half, half)
        for k, chip in enumerate(_other_chips(x, y)):
            slab = outs[w].at[2 * chip[0] + chip[1], rows, :]
            copies.append(pltpu.make_async_remote_copy(
                src_ref=slab, dst_ref=slab, send_sem=fwd_send.at[3 * w + k], recv_sem=fwd_recv.at[3 * w + k],
                device_id=(x, y, 1 - c), device_id_type=MESH))
    return copies


def _gather2_shapes(shards):
    n = len(shards)
    return ([jax.ShapeDtypeStruct((N_CHIPS,) + s.shape, s.dtype) for s in shards],
            [pltpu.SemaphoreType.DMA((3 * n,)), pltpu.SemaphoreType.DMA((3 * n,)), pltpu.SemaphoreType.DMA((n,)),
             pltpu.SemaphoreType.DMA((n,)), pltpu.SemaphoreType.DMA((3 * n,)), pltpu.SemaphoreType.DMA((3 * n,))])


COMM = {"gather": (_gather_copies, _gather_shapes, None), "exchange": (_exchange_copies, _exchange_shapes, None),
        "swap": (_swap_copies, _swap_shapes, None), "gather2": (_gather2_copies, _gather2_shapes, _gather2_forward),
        "exchange8": (_exchange8_copies, _exchange8_shapes, None)}


def _call(body, comm, *, grid, in_specs, out_specs, out_shape, scratch_shapes, args, name):
    ni, no, ns = len(in_specs), len(out_specs), len(scratch_shapes)
    arrays = list(comm[1]) if comm else []
    nc = len(arrays)
    first_fn, shapes_fn, second_fn = COMM[comm[0]] if comm else (None, None, None)
    c_shapes, c_sems = shapes_fn(arrays) if comm else ([], [])

    def kern(*refs):
        ins, cin, outs = refs[:ni], refs[ni:ni + nc], refs[ni + nc:ni + nc + no]
        rest = refs[ni + nc + no:]
        cout, scr, sems = rest[:nc], rest[nc:nc + ns], rest[nc + ns:]
        ids = [pl.program_id(d) for d in range(len(grid))]
        if nc:
            @pl.when(functools.reduce(lambda a, b: a & b, [i == 0 for i in ids]))
            def _():
                for cp in first_fn(cin, cout, *sems):
                    cp.start()

        body(ins, outs, scr)
        if nc:
            @pl.when(functools.reduce(lambda a, b: a & b, [i == g - 1 for i, g in zip(ids, grid)]))
            def _():
                for cp in first_fn(cin, cout, *sems):
                    cp.wait()
                if second_fn is not None:
                    more = second_fn(cin, cout, *sems)
                    for cp in more:
                        cp.start()
                    for cp in more:
                        cp.wait()

    out = pl.pallas_call(
        kern, grid=grid, in_specs=list(in_specs) + [ANY] * nc, out_specs=list(out_specs) + [ANY] * nc,
        out_shape=list(out_shape) + c_shapes, scratch_shapes=list(scratch_shapes) + c_sems,
        compiler_params=_cp(("arbitrary",) * len(grid)), name=name,
    )(*args, *arrays)
    return list(out[:no]), list(out[no:])


def _resident(shape):
    return pl.BlockSpec(shape, lambda *_: (0,) * len(shape), pipeline_mode=pl.Buffered(1))


def _norm1(x, g1, comm):
    t = x.shape[0]
    tm = min(1024, t)

    def body(ins, outs, _):
        xv = ins[0][...]
        outs[0][...] = (xv * _rstd(xv) * ins[1][...]).astype(BF16)

    row = pl.BlockSpec((tm, D_MODEL), lambda i: (i, 0))
    (xn,), got = _call(body, comm, grid=(t // tm,), in_specs=[row, pl.BlockSpec((1, D_MODEL), lambda i: (0, 0))],
                       out_specs=[row], out_shape=[jax.ShapeDtypeStruct((t, D_MODEL), BF16)], scratch_shapes=[],
                       args=(x, g1), name="norm1")
    return xn, got


def _inproj(xn, w_in, comm=None):
    t = xn.shape[0]
    tm = min(ROW_TILE, t)

    def body(ins, outs, _):
        xn_v = ins[0][...]
        outs[0][...] = _dot(xn_v, ins[1][:, : 2 * LRU_WIDTH])
        outs[1][...] = _dot(xn_v, ins[1][:, 2 * LRU_WIDTH:]).astype(BF16)

    row = lambda c: pl.BlockSpec((tm, c), lambda i: (i, 0))
    (xl, qkv), got = _call(
        body, comm, grid=(t // tm,), in_specs=[row(D_MODEL), _resident((D_MODEL, IN_COLS))],
        out_specs=[row(2 * LRU_WIDTH), row(3 * SB_WIDTH)],
        out_shape=[jax.ShapeDtypeStruct((t, 2 * LRU_WIDTH), F32), jax.ShapeDtypeStruct((t, 3 * SB_WIDTH), BF16)],
        scratch_shapes=[], args=(xn, w_in), name="inproj")
    return xl, qkv, got


def _conv_taps(hist, u):
    cat = jnp.concatenate([hist, u], axis=0)
    return [pltpu.roll(cat, CONV_WIDTH - 1 - k, 0)[SUBLANES:] for k in range(CONV_WIDTH - 1)] + [u]


def _lru_gates(c, wbd_ref, ba, bx, sp):
    gas, gxs = [], []
    for p in range(LRU_WIDTH // LANES):
        gax = _dot(c[:, LANES * p: LANES * (p + 1)].astype(BF16), wbd_ref[p])
        gas.append(gax[:, :LANES])
        gxs.append(gax[:, LANES:])
    r = _sigmoid(jnp.concatenate(gas, axis=1) + ba)
    i = _sigmoid(jnp.concatenate(gxs, axis=1) + bx)
    la = (-LRU_C) * r * sp
    a = jnp.exp(la)
    e2 = _neg_expm1(2.0 * la, a * a)
    inv_mult = lax.rsqrt(jnp.maximum(e2, 1e-30))
    return r, i, a, e2 * inv_mult, inv_mult


def _scan_fwd(a, b):
    s = 1
    while s < a.shape[0]:
        b = b + a * _shift_down(b, s, 0.0)
        a = a * _shift_down(a, s, 1.0)
        s *= 2
    return a, b


def _scan_rev(a, b):
    s = 1
    while s < a.shape[0]:
        b = b + a * _shift_up(b, s, 0.0)
        a = a * _shift_up(a, s, 1.0)
        s *= 2
    return a, b


def _lru_param_specs(grid_rank):
    z2 = (lambda e, c: (0, 0)) if grid_rank == 2 else None
    return [
        pl.BlockSpec((CONV_WIDTH, LRU_WIDTH), z2), pl.BlockSpec((1, LRU_WIDTH), z2),
        pl.BlockSpec((LRU_WIDTH // LANES, LANES, 2 * LANES), lambda e, c: (0, 0, 0)),
        pl.BlockSpec((1, LRU_WIDTH), z2), pl.BlockSpec((1, LRU_WIDTH), z2), pl.BlockSpec((1, LRU_WIDTH), z2),
    ]


def _lru_fwd(xl, conv_w, conv_b, wbd, ba, bx, lam, seq):
    t = xl.shape[0]
    tc = min(512, seq)
    nc = seq // tc

    def kern(u_ref, g_ref, cw_ref, cb_ref, wbd_ref, ba_ref, bx_ref, lam_ref, h_ref, y_ref, hist_ref, hcar_ref):
        @pl.when(pl.program_id(1) == 0)
        def _():
            hist_ref[...] = jnp.zeros_like(hist_ref)
            hcar_ref[...] = jnp.zeros_like(hcar_ref)

        u = u_ref[...]
        taps = _conv_taps(hist_ref[...], u)
        hist_ref[...] = u_ref[tc - SUBLANES:, :]
        c = cb_ref[...]
        for k in range(CONV_WIDTH):
            c = c + taps[k] * cw_ref[k:k + 1, :]
        sp = _softplus(-lam_ref[...])
        _, i, a, mult, _ = _lru_gates(c, wbd_ref, ba_ref[...], bx_ref[...], sp)
        aa, bb = _scan_fwd(a, mult * i * c)
        h = bb + aa * hcar_ref[0:1, :]
        h_ref[...] = h
        hcar_ref[0:1, :] = h_ref[tc - 1:tc, :]
        y_ref[...] = h * _gelu(g_ref[...])[0]

    chunk = lambda col: pl.BlockSpec((tc, LRU_WIDTH), lambda e, c: (e * nc + c, col))
    out = jax.ShapeDtypeStruct((t, LRU_WIDTH), F32)
    return pl.pallas_call(
        kern, grid=(t // seq, nc), in_specs=[chunk(0), chunk(1)] + _lru_param_specs(2),
        out_specs=[chunk(0), chunk(0)], out_shape=[out, out],
        scratch_shapes=[pltpu.VMEM((SUBLANES, LRU_WIDTH), F32), pltpu.VMEM((SUBLANES, LRU_WIDTH), F32)],
        compiler_params=_cp(("arbitrary", "arbitrary")), name="lru_fwd",
    )(xl, xl, conv_w, conv_b, wbd, ba, bx, lam)


def _att_consts():
    row = lax.broadcasted_iota(jnp.int32, (TQ, 2 * TK), 0)
    key = lax.broadcasted_iota(jnp.int32, (TQ, 2 * TK), 1) & (TK - 1)
    return key < row


def _sum_matrix(kind):
    j = lax.broadcasted_iota(jnp.int32, (2 * TK, 2 * TK), 0) & (TK - 1)
    s = lax.broadcasted_iota(jnp.int32, (2 * TK, 2 * TK), 1)
    pick = {"after": j > s, "upto": j <= s, "before": j < s}[kind]
    return jnp.where((s >= TK) | pick, 1.0, 0.0).astype(BF16)


def _hi_lo(x):
    hi = x.astype(BF16)
    return hi, (x - hi.astype(F32)).astype(BF16)


def _pair_sums(x, m):
    hi, lo = _hi_lo(x)
    out = []
    for hd in range(2):
        cols = slice(hd * TK, (hd + 1) * TK)
        out.append(_dot(jnp.concatenate([hi[:, cols], lo[:, cols]], axis=1), m))
    return [o[:, :TK] for o in out], [o[:, TK:] for o in out]


def _att_logits(qb, kbd):
    z = _dot(qb, kbd)
    lg = jnp.log(1.0 + jnp.exp(-jnp.abs(z)))
    lb = jnp.minimum(z, 0.0) - lg
    return lb, lb - z


def _head_diag(x, rows_first):
    n = x.shape[0] if rows_first else x.shape[1]
    idx = lax.broadcasted_iota(jnp.int32, x.shape, 0 if rows_first else 1)
    return jnp.where(idx < n // 2, x, 0), jnp.where(idx >= n // 2, x, 0)


def _band_tiles():
    nd = TQ // TK
    return [(jr, TK * max(jr, 0), TK * min(jr + BAND + 1, nd)) for jr in range(nd - 1, -BAND - 1, -1)]


def _rows_update(st, new, lo, hi):
    def one(x, y):
        pieces = ([x[:lo]] if lo else []) + [y] + ([x[hi:]] if hi < x.shape[0] else [])
        return pieces[0] if len(pieces) == 1 else jnp.concatenate(pieces, axis=0)
    return tuple(one(x, y) for x, y in zip(st, new))


def _scaled_q(q_ref, q0):
    return (q_ref[pl.ds(q0, TQ), :].astype(F32) * ATT_SCALE).astype(BF16)


def _qkv_specs(seq):
    n = SB_WIDTH // LANES
    return [pl.BlockSpec((seq, LANES), lambda e, p, off=off: (e, off * n + p)) for off in range(3)]


def _attn_fwd(qkv, seq, comm=None):
    t = qkv.shape[0]
    ne, nq, nk = t // seq, seq // TQ, seq // TK

    def body(ins, outs, scr):
        (q_ref, k_ref, v_ref), (o_ref, tot_ref, kmin_ref), (kbd_scr, vbd_scr) = ins, outs, scr
        causal = _att_consts()
        after = _sum_matrix("after")

        def prep(j, _):
            k0 = pl.multiple_of(j * TK, TK)
            top, bot = _head_diag(k_ref[pl.ds(k0, TK), :].T, True)
            kbd_scr[j] = jnp.concatenate([top, bot], axis=1)
            left, right = _head_diag(v_ref[pl.ds(k0, TK), :], False)
            vbd_scr[j] = jnp.concatenate([left, right], axis=0)
            return 0

        lax.fori_loop(0, nk, prep, 0)

        def block(j, qb, st, mask):
            c0, c1, oacc = st
            lb, l1 = _att_logits(qb, kbd_scr[j])
            if mask is not None:
                l1 = jnp.where(mask, l1, 0.0)
            (s0, s1), (r0, r1) = _pair_sums(l1, after)
            att = jnp.exp(lb + jnp.concatenate([s0 + c0, s1 + c1], axis=1))
            if mask is not None:
                att = jnp.where(mask, att, 0.0)
            return c0 + r0, c1 + r1, oacc + _dot(att.astype(BF16), vbd_scr[j])

        def general(qi, qb, st):
            for jj in reversed(range(TQ // TK)):
                lo = TK * jj
                new = block((TQ // TK) * qi + jj, qb[lo:], tuple(x[lo:] for x in st), causal[:TQ - lo])
                st = _rows_update(st, new, lo, TQ)

            npair = (TQ // TK // 2) * qi

            def more(its):
                return (its[0] < npair) & (jnp.max(jnp.maximum(its[1], its[2])) > SKIP_LOG)

            def kloop(its):
                j = 2 * (npair - its[0]) - 1
                return (its[0] + 1,) + block(j - 1, qb, block(j, qb, its[1:], None), None)

            done, c0, c1, oacc = lax.while_loop(more, kloop, (jnp.int32(0),) + st)
            return c0, c1, oacc, 2 * (npair - done)

        def short(qi, qb, st):
            for jr, lo, hi in _band_tiles():
                new = block((TQ // TK) * qi + jr, qb[lo:hi], tuple(x[lo:hi] for x in st),
                            causal[:hi - lo] if jr >= 0 else None)
                st = _rows_update(st, new, lo, hi)
            return st

        def qloop(qi, _):
            q0 = pl.multiple_of(qi * TQ, TQ)
            qb = _scaled_q(q_ref, q0)
            zero = jnp.zeros((TQ, TK), F32)
            st = (zero, zero, jnp.zeros((TQ, LANES), F32))

            def try_short():
                c0, c1, oacc = short(qi, qb, st)
                return lax.cond(jnp.max(jnp.maximum(c0, c1)) <= SKIP_LOG, lambda: (c0, c1, oacc, jnp.int32(-1)),
                                lambda: general(qi, qb, st))

            c0, c1, oacc, first = lax.cond(qi > 0, try_short, lambda: general(qi, qb, st))
            o_ref[pl.ds(q0, TQ), :] = oacc
            tot_ref[pl.ds(q0, TQ), :] = jnp.concatenate([c0, c1], axis=1)
            kmin_ref[pl.program_id(0), pl.program_id(1), qi] = first
            return 0

        lax.fori_loop(0, nq, qloop, 0)

    (o, tot, kmin), got = _call(
        body, comm, grid=(ne, SB_WIDTH // LANES), in_specs=_qkv_specs(seq),
        out_specs=[pl.BlockSpec((seq, LANES), lambda e, p: (e, p)), pl.BlockSpec((seq, 2 * TK), lambda e, p: (e, p)),
                   pl.BlockSpec(memory_space=pltpu.SMEM)],
        out_shape=[jax.ShapeDtypeStruct((t, SB_WIDTH), F32), jax.ShapeDtypeStruct((t, 2 * TK * SB_WIDTH // LANES), F32),
                   jax.ShapeDtypeStruct((ne, SB_WIDTH // LANES, nq), jnp.int32)],
        scratch_shapes=[pltpu.VMEM((nk, LANES, 2 * TK), BF16), pltpu.VMEM((nk, 2 * TK, LANES), BF16)],
        args=(qkv, qkv, qkv), name="attn_fwd")
    return o, tot, kmin, got


def _outproj(y_lru, o, x, ga, gb, w_out):
    t = x.shape[0]
    tm = min(ROW_TILE, t)

    def kern(y_ref, o_ref, x_ref, ga_ref, gb_ref, w_ref, h1_ref, mix_ref):
        yv, ov = y_ref[...], o_ref[...]
        mix = jnp.concatenate([yv * _rstd(yv) * ga_ref[...], ov * _rstd(ov) * gb_ref[...]], axis=1).astype(BF16)
        mix_ref[...] = mix
        h1_ref[...] = x_ref[...] + _dot(mix, w_ref[...])

    row = lambda c: pl.BlockSpec((tm, c), lambda i: (i, 0))
    vec = lambda c: pl.BlockSpec((1, c), lambda i: (0, 0))
    return pl.pallas_call(
        kern, grid=(t // tm,),
        in_specs=[row(LRU_WIDTH), row(SB_WIDTH), row(D_MODEL), vec(LRU_WIDTH), vec(SB_WIDTH),
                  pl.BlockSpec((D_MODEL, D_MODEL), lambda i: (0, 0))],
        out_specs=[row(D_MODEL), row(D_MODEL)],
        out_shape=[jax.ShapeDtypeStruct((t, D_MODEL), F32), jax.ShapeDtypeStruct((t, D_MODEL), BF16)],
        compiler_params=_cp(("parallel",)), name="outproj",
    )(y_lru, o, x, ga, gb, w_out)


def _mlp_loss(h1, g2, w_up, w_down, target, gf):
    t = h1.shape[0]
    tm, tf = min(ROW_TILE, t), 1024

    def kern(h1_ref, g_ref, wu_ref, wd_ref, t_ref, gf_ref, hn_ref, up_ref, u2_ref, dh_ref, dhb_ref, loss_ref, dg_ref):
        @pl.when(pl.program_id(0) == 0)
        def _():
            loss_ref[...] = jnp.zeros_like(loss_ref)
            dg_ref[...] = jnp.zeros_like(dg_ref)

        hv = h1_ref[...]
        hn = (hv * _rstd(hv) * g_ref[...]).astype(BF16)
        hn_ref[...] = hn
        h2 = hv
        for f in range(D_FF // tf):
            cols = slice(f * tf, (f + 1) * tf)
            up = jnp.maximum(_dot(hn, wu_ref[:, cols]), 0.0)
            u2 = (up * up).astype(BF16)
            up_ref[:, cols] = up.astype(BF16)
            u2_ref[:, cols] = u2
            h2 = h2 + _dot(u2, wd_ref[cols, :])

        g = gf_ref[...]
        err = h2 * _rstd(h2) * g - t_ref[...]
        lane = lax.broadcasted_iota(jnp.int32, (1, LANES), 1)
        loss_ref[...] += jnp.where(lane == 0, 0.5 * jnp.sum(err * err) / D_MODEL, 0.0)
        dx, dg = _rms_bwd(h2, g, err * (1.0 / D_MODEL))
        dh_ref[...] = dx
        dhb_ref[...] = dx.astype(BF16)
        dg_ref[...] += dg

    row = lambda c: pl.BlockSpec((tm, c), lambda i: (i, 0))
    vec = pl.BlockSpec((1, D_MODEL), lambda i: (0, 0))
    return pl.pallas_call(
        kern, grid=(t // tm,),
        in_specs=[row(D_MODEL), vec, _resident((D_MODEL, D_FF)), _resident((D_FF, D_MODEL)), row(D_MODEL), vec],
        out_specs=[row(D_MODEL), row(D_FF), row(D_FF), row(D_MODEL), row(D_MODEL), pl.BlockSpec((1, LANES), lambda i: (0, 0)), vec],
        out_shape=[jax.ShapeDtypeStruct((t, D_MODEL), BF16), jax.ShapeDtypeStruct((t, D_FF), BF16),
                   jax.ShapeDtypeStruct((t, D_FF), BF16), jax.ShapeDtypeStruct((t, D_MODEL), F32),
                   jax.ShapeDtypeStruct((t, D_MODEL), BF16), jax.ShapeDtypeStruct((1, LANES), F32),
                   jax.ShapeDtypeStruct((1, D_MODEL), F32)],
        compiler_params=_cp(("arbitrary",), VMEM_LIMIT_BIG), name="mlp_loss",
    )(h1, g2, w_up, w_down, target, gf)


def _mlp_bwd_pre(dh2, w_down, up):
    t = dh2.shape[0]
    tm, tf = min(ROW_TILE, t), 1024

    def kern(d_ref, w_ref, up_ref, o_ref):
        dv = d_ref[...]
        for f in range(D_FF // tf):
            cols = slice(f * tf, (f + 1) * tf)
            o_ref[:, cols] = (_dot_nt(dv, w_ref[cols, :]) * (2.0 * up_ref[:, cols].astype(F32))).astype(BF16)

    row = lambda c: pl.BlockSpec((tm, c), lambda i: (i, 0))
    return pl.pallas_call(
        kern, grid=(t // tm,), in_specs=[row(D_MODEL), _resident((D_FF, D_MODEL)), row(D_FF)],
        out_specs=row(D_FF), out_shape=jax.ShapeDtypeStruct((t, D_FF), BF16),
        compiler_params=_cp(("parallel",)), name="mlp_bwd_pre",
    )(dh2, w_down, up)


def _proj_bwd_norm(dys, w, x, g, resid, name, comm=None, bf16_copy=False):
    t = x.shape[0]
    tm = min(ROW_TILE, t)
    widths = [dy.shape[1] for dy in dys]
    n = len(dys)

    def body(ins, outs, _):
        dy_refs, (w_ref, x_ref, g_ref, r_ref), (dx_ref, dg_ref) = ins[:n], ins[n:], outs[:2]

        @pl.when(pl.program_id(0) == 0)
        def _():
            dg_ref[...] = jnp.zeros_like(dg_ref)

        off, dxn = 0, None
        for dy_ref, wd in zip(dy_refs, widths):
            part = _dot_nt(dy_ref[...], w_ref[:, off:off + wd])
            dxn = part if dxn is None else dxn + part
            off += wd
        dx, dg = _rms_bwd(x_ref[...], g_ref[...], dxn)
        dx = r_ref[...] + dx
        dx_ref[...] = dx
        dg_ref[...] += dg
        if bf16_copy:
            outs[2][...] = dx.astype(BF16)

    row = lambda c: pl.BlockSpec((tm, c), lambda i: (i, 0))
    vec = pl.BlockSpec((1, D_MODEL), lambda i: (0, 0))
    outs, got = _call(
        body, comm, grid=(t // tm,), in_specs=[row(wd) for wd in widths] + [_resident(w.shape), row(D_MODEL), vec, row(D_MODEL)],
        out_specs=[row(D_MODEL), vec] + [row(D_MODEL)] * bf16_copy,
        out_shape=[jax.ShapeDtypeStruct((t, D_MODEL), F32), jax.ShapeDtypeStruct((1, D_MODEL), F32)]
        + [jax.ShapeDtypeStruct((t, D_MODEL), BF16)] * bf16_copy,
        scratch_shapes=[], args=(*dys, w, x, g, resid), name=name)
    return outs[0], (outs[2] if bf16_copy else None), outs[1], got


def _dw_in(xn, pieces):
    t = xn.shape[0]
    tk = min(2 * ROW_TILE, t)
    widths = [p.shape[1] for p in pieces]
    shard = IN_COLS // N_CHIPS

    def windows(j):
        out, off = [], 0
        for i, wd in enumerate(widths):
            a, b = max(j * shard, off), min((j + 1) * shard, off + wd)
            if a < b:
                assert (a - off) % LANES == 0 and (b - off) % LANES == 0
                out.append((i, a - off, b - off))
            off += wd
        return out

    def kern(x_ref, *refs):
        p_refs, o_ref, ob_ref = refs[:-2], refs[-2], refs[-1]

        @pl.when(pl.program_id(0) == 0)
        def _():
            o_ref[...] = jnp.zeros_like(o_ref)

        xv = x_ref[...]
        for j in range(N_CHIPS):
            cols = jnp.concatenate([p_refs[i][:, a:b] for i, a, b in windows(j)], axis=1)
            o_ref[j] += _dot_tn(xv, cols)

        @pl.when(pl.program_id(0) == t // tk - 1)
        def _():
            ob_ref[...] = o_ref[...].astype(BF16)

    row = lambda c: pl.BlockSpec((tk, c), lambda k: (k, 0))
    whole = pl.BlockSpec((N_CHIPS, D_MODEL, shard), lambda k: (0, 0, 0))
    return pl.pallas_call(
        kern, grid=(t // tk,), in_specs=[row(D_MODEL)] + [row(wd) for wd in widths], out_specs=[whole, whole],
        out_shape=[jax.ShapeDtypeStruct((N_CHIPS, D_MODEL, shard), F32), jax.ShapeDtypeStruct((N_CHIPS, D_MODEL, shard), BF16)],
        compiler_params=_cp(("arbitrary",)), name="dw_in",
    )(xn, *pieces)


def _outproj_bwd(dh1, w_out, y_lru, o, ga, gb, comm=None):
    t = dh1.shape[0]
    tm = min(ROW_TILE, t)

    def body(ins, outs, _):
        (d_ref, w_ref, y_ref, o_ref, ga_ref, gb_ref), (dy_ref, do_ref, dga_ref, dgb_ref) = ins, outs

        @pl.when(pl.program_id(0) == 0)
        def _():
            dga_ref[...] = jnp.zeros_like(dga_ref)
            dgb_ref[...] = jnp.zeros_like(dgb_ref)

        dmix = _dot_nt(d_ref[...], w_ref[...])
        dy, dga = _rms_bwd(y_ref[...], ga_ref[...], dmix[:, :LRU_WIDTH])
        do, dgb = _rms_bwd(o_ref[...], gb_ref[...], dmix[:, LRU_WIDTH:])
        dy_ref[...] = dy
        do_ref[...] = do
        dga_ref[...] += dga
        dgb_ref[...] += dgb

    row = lambda c: pl.BlockSpec((tm, c), lambda i: (i, 0))
    vec = pl.BlockSpec((1, LRU_WIDTH), lambda i: (0, 0))
    half = jax.ShapeDtypeStruct((t, LRU_WIDTH), F32)
    gsum = jax.ShapeDtypeStruct((1, LRU_WIDTH), F32)
    outs, got = _call(body, comm, grid=(t // tm,),
                      in_specs=[row(D_MODEL), _resident((D_MODEL, D_MODEL)), row(LRU_WIDTH), row(SB_WIDTH), vec, vec],
                      out_specs=[row(LRU_WIDTH), row(SB_WIDTH), vec, vec], out_shape=[half, half, gsum, gsum],
                      scratch_shapes=[], args=(dh1, w_out, y_lru, o, ga, gb), name="outproj_bwd")
    return (*outs, got)


def _attn_bwd(qkv, do, tot, kmin, seq):
    t = qkv.shape[0]
    ne, nq, nk = t // seq, seq // TQ, seq // TK

    def kern(q_ref, k_ref, v_ref, do_ref, tot_ref, kmin_ref, dq_ref, dk_ref, dv_ref,
             kbd_scr, vtbd_scr, kbd2_scr, dkt_scr, dvt_scr):
        causal = _att_consts()
        upto, before = _sum_matrix("upto"), _sum_matrix("before")

        def prep(j, _):
            k0 = pl.multiple_of(j * TK, TK)
            kb = k_ref[pl.ds(k0, TK), :]
            top, bot = _head_diag(kb.T, True)
            kbd_scr[j] = jnp.concatenate([top, bot], axis=1)
            top, bot = _head_diag(v_ref[pl.ds(k0, TK), :].T, True)
            vtbd_scr[j] = jnp.concatenate([top, bot], axis=1)
            left, right = _head_diag(kb, False)
            kbd2_scr[j] = jnp.concatenate([left, right], axis=0)
            dkt_scr[j] = jnp.zeros((LANES, 2 * TK), F32)
            dvt_scr[j] = jnp.zeros((LANES, 2 * TK), F32)
            return 0

        lax.fori_loop(0, nk, prep, 0)

        def block(j, qb, qt, dob, dot_, totb, st, mask):
            f0, f1, p0, p1, dqacc = st
            lb, l1 = _att_logits(qb, kbd_scr[j])
            if mask is not None:
                l1 = jnp.where(mask, l1, 0.0)
            (s0, s1), (r0, r1) = _pair_sums(l1, upto)
            att = jnp.exp(lb + (totb - jnp.concatenate([s0 + f0, s1 + f1], axis=1)))
            if mask is not None:
                att = jnp.where(mask, att, 0.0)
            pw = att * _dot(dob, vtbd_scr[j])
            (e0, e1), (t0, t1) = _pair_sums(pw, before)
            dz = pw - jnp.exp(lb) * (pw + jnp.concatenate([e0 + p0, e1 + p1], axis=1))
            if mask is not None:
                dz = jnp.where(mask, dz, 0.0)
            dzb = dz.astype(BF16)
            dkt_scr[j] += _dot(qt, dzb)
            dvt_scr[j] += _dot(dot_, att.astype(BF16))
            return f0 + r0, f1 + r1, p0 + t0, p1 + t1, dqacc + _dot(dzb, kbd2_scr[j])

        def qloop(qi, _):
            q0 = pl.multiple_of(qi * TQ, TQ)
            qb = _scaled_q(q_ref, q0)
            qt = qb.T
            do32 = do_ref[pl.ds(q0, TQ), :]
            dob = do32.astype(BF16)
            dot_ = dob.T
            totb = tot_ref[pl.ds(q0, TQ), :]
            zero = jnp.zeros((TQ, TK), F32)
            st = (zero, zero, zero, zero, jnp.zeros((TQ, LANES), F32))

            k0 = kmin_ref[pl.program_id(0), pl.program_id(1), qi]

            def tile(j, lo, hi, st, masked):
                new = block(j, qb[lo:hi], qt[:, lo:hi], dob[lo:hi], dot_[:, lo:hi], totb[lo:hi],
                            tuple(x[lo:hi] for x in st), causal[:hi - lo] if masked else None)
                return _rows_update(st, new, lo, hi)

            def general():
                def kloop(it, st):
                    j = k0 + 2 * it
                    return block(j + 1, qb, qt, dob, dot_, totb, block(j, qb, qt, dob, dot_, totb, st, None), None)

                out = lax.fori_loop(0, ((TQ // TK) * qi - k0) // 2, kloop, st)
                for jj in range(TQ // TK):
                    out = tile((TQ // TK) * qi + jj, TK * jj, TQ, out, True)
                return out

            def short():
                out = st
                for jr, lo, hi in reversed(_band_tiles()):
                    out = tile((TQ // TK) * qi + jr, lo, hi, out, jr >= 0)
                return out

            st = lax.cond(k0 < 0, short, general)
            dq_ref[pl.ds(q0, TQ), :] = (st[4] * ATT_SCALE).astype(BF16)
            return 0

        lax.fori_loop(0, nq, qloop, 0)

        def finish(j, _):
            k0 = pl.multiple_of(j * TK, TK)
            head0 = lax.broadcasted_iota(jnp.int32, (LANES, TK), 0) < DH
            for src, dst in ((dkt_scr, dk_ref), (dvt_scr, dv_ref)):
                acc = src[j]
                dst[pl.ds(k0, TK), :] = jnp.where(head0, acc[:, :TK], acc[:, TK:]).astype(BF16).T
            return 0

        lax.fori_loop(0, nk, finish, 0)

    blk = pl.BlockSpec((seq, LANES), lambda e, p: (e, p))
    grad = jax.ShapeDtypeStruct((t, SB_WIDTH), BF16)
    return pl.pallas_call(
        kern, grid=(ne, SB_WIDTH // LANES),
        in_specs=_qkv_specs(seq) + [blk, pl.BlockSpec((seq, 2 * TK), lambda e, p: (e, p)),
                                    pl.BlockSpec(memory_space=pltpu.SMEM)],
        out_specs=[blk, blk, blk], out_shape=[grad, grad, grad],
        scratch_shapes=[pltpu.VMEM((nk, LANES, 2 * TK), BF16), pltpu.VMEM((nk, LANES, 2 * TK), BF16),
                        pltpu.VMEM((nk, 2 * TK, LANES), BF16), pltpu.VMEM((nk, LANES, 2 * TK), F32),
                        pltpu.VMEM((nk, LANES, 2 * TK), F32)],
        compiler_params=_cp(("parallel", "parallel")), name="attn_bwd",
    )(qkv, qkv, qkv, do, tot, kmin)


def _lru_bwd(xl, h, dy, conv_w, conv_b, wbd, ba, bx, lam, seq, comm=None):
    t = xl.shape[0]
    tc = min(512, seq)
    nc = seq // tc
    nb = tc // SUBLANES


    def body(ins, outs, scr):
        u_ref, g_ref, up_ref, h_ref, hp_ref, dy_ref, cw_ref, cb_ref, wbd_ref, ba_ref, bx_ref, lam_ref = ins
        (dxl_ref, small_ref, dwbd_ref), (lnext_ref, anext_ref, dcnext_ref) = outs, scr
        e, ci = pl.program_id(0), pl.program_id(1)
        first = ci == nc - 1

        @pl.when((e == 0) & (ci == 0))
        def _():
            small_ref[...] = jnp.zeros_like(small_ref)
            dwbd_ref[...] = jnp.zeros_like(dwbd_ref)

        @pl.when(ci == 0)
        def _():
            lnext_ref[...] = jnp.zeros_like(lnext_ref)
            anext_ref[...] = jnp.zeros_like(anext_ref)
            dcnext_ref[...] = jnp.zeros_like(dcnext_ref)

        u, g = u_ref[...], g_ref[...]
        keep = jnp.where(first, 0.0, 1.0)
        taps = _conv_taps(keep * up_ref[...], u)
        c = cb_ref[...]
        for k in range(CONV_WIDTH):
            c = c + taps[k] * cw_ref[k:k + 1, :]
        lam = lam_ref[...]
        sp = _softplus(-lam)
        r, i, a, mult, inv_mult = _lru_gates(c, wbd_ref, ba_ref[...], bx_ref[...], sp)
        gel, th = _gelu(g)
        dyv, hv = dy_ref[...], h_ref[...]
        dg = dyv * hv * _gelu_grad(g, th)

        aa, bb = _scan_rev(_shift_up(a, 1, anext_ref[0:1, :]), dyv * gel)
        lt = bb + aa * lnext_ref[0:1, :]
        lnext_ref[0:1, :] = _row_of(lt, 0)
        anext_ref[0:1, :] = _row_of(a, 0)

        hprev = _shift_down(hv, 1, keep * hp_ref[SUBLANES - 1:SUBLANES, :])
        da = lt * hprev
        dmult = lt * i * c
        di = lt * mult * c
        dc = lt * mult * i
        dla = da * a - dmult * (a * a) * inv_mult
        dga = dla * ((-LRU_C) * sp) * r * (1.0 - r)
        dgx = di * i * (1.0 - i)
        small_ref[7:8, :] += jnp.sum(dla * r, axis=0, keepdims=True) * (LRU_C * _sigmoid(-lam))
        small_ref[5:6, :] += jnp.sum(dga, axis=0, keepdims=True)
        small_ref[6:7, :] += jnp.sum(dgx, axis=0, keepdims=True)

        dcs = []
        for p in range(LRU_WIDTH // LANES):
            cols = slice(LANES * p, LANES * (p + 1))
            dgax = jnp.concatenate([dga[:, cols], dgx[:, cols]], axis=1).astype(BF16)
            dcs.append(_dot_nt(dgax, wbd_ref[p]))
            dwbd_ref[p] += _dot_tn(c[:, cols].astype(BF16), dgax)
        dc = dc + jnp.concatenate(dcs, axis=1)
        small_ref[4:5, :] += jnp.sum(dc, axis=0, keepdims=True)

        catd = jnp.concatenate([dc, dcnext_ref[...]], axis=0)
        du = dc * cw_ref[CONV_WIDTH - 1:CONV_WIDTH, :]
        for j in range(1, CONV_WIDTH):
            du = du + pltpu.roll(catd, tc + SUBLANES - j, 0)[:tc] * cw_ref[CONV_WIDTH - 1 - j:CONV_WIDTH - j, :]
        dcnext_ref[...] = dc[:SUBLANES]
        for k in range(CONV_WIDTH):
            small_ref[k:k + 1, :] += jnp.sum(dc * taps[k], axis=0, keepdims=True)
        dxl_ref[:, :LRU_WIDTH] = du.astype(BF16)
        dxl_ref[:, LRU_WIDTH:] = dg.astype(BF16)

    rev = lambda e, c: e * nc + (nc - 1 - c)
    chunk = lambda col: pl.BlockSpec((tc, LRU_WIDTH), lambda e, c: (rev(e, c), col))
    prev8 = pl.BlockSpec((SUBLANES, LRU_WIDTH), lambda e, c: (jnp.maximum(rev(e, c) * nb - 1, 0), 0))
    outs, got = _call(
        body, comm, grid=(t // seq, nc),
        in_specs=[chunk(0), chunk(1), prev8, chunk(0), prev8, chunk(0)] + _lru_param_specs(2),
        out_specs=[pl.BlockSpec((tc, 2 * LRU_WIDTH), lambda e, c: (rev(e, c), 0)),
                   pl.BlockSpec((SUBLANES, LRU_WIDTH), lambda e, c: (0, 0)),
                   pl.BlockSpec((LRU_WIDTH // LANES, LANES, 2 * LANES), lambda e, c: (0, 0, 0))],
        out_shape=[jax.ShapeDtypeStruct((t, 2 * LRU_WIDTH), BF16), jax.ShapeDtypeStruct((SUBLANES, LRU_WIDTH), F32),
                   jax.ShapeDtypeStruct((LRU_WIDTH // LANES, LANES, 2 * LANES), F32)],
        scratch_shapes=[pltpu.VMEM((SUBLANES, LRU_WIDTH), F32)] * 3,
        args=(xl, xl, xl, h, h, dy, conv_w, conv_b, wbd, ba, bx, lam), name="lru_bwd")
    return (*outs, got)


def _adam_math(w, g, m, v):
    m2 = ADAM_B1 * m + (1.0 - ADAM_B1) * g
    v2 = ADAM_B2 * v + (1.0 - ADAM_B2) * (g * g)
    m_hat = m2 / (1.0 - ADAM_B1 ** ADAM_STEP)
    v_hat = v2 / (1.0 - ADAM_B2 ** ADAM_STEP)
    return -ADAM_LR * (m_hat / (jnp.sqrt(v_hat) + ADAM_EPS) + ADAM_WD * w), m2, v2


def _adamw(w, g, m, v, name):
    rows, cols = w.shape
    tr = 256 if rows % 256 == 0 else rows

    def kern(w_ref, g_ref, m_ref, v_ref, d_ref, m2_ref, v2_ref):
        d_ref[...], m2_ref[...], v2_ref[...] = _adam_math(w_ref[...], g_ref[...], m_ref[...], v_ref[...])

    blk = pl.BlockSpec((tr, cols), lambda i: (i, 0))
    out = jax.ShapeDtypeStruct((rows, cols), F32)
    return pl.pallas_call(kern, grid=(rows // tr,), in_specs=[blk] * 4, out_specs=[blk] * 3, out_shape=[out] * 3,
                          compiler_params=_cp(("parallel",)), name=name)(w, g, m, v)


def _adamw_small(ws, gs, ms, vs):
    n = len(ws)

    def kern(*refs):
        for k in range(n):
            outs = _adam_math(refs[k][...], refs[n + k][...], refs[2 * n + k][...], refs[3 * n + k][...])
            for j in range(3):
                refs[(4 + j) * n + k][...] = outs[j]

    vm = pl.BlockSpec(memory_space=pltpu.VMEM)
    out = pl.pallas_call(kern, in_specs=[vm] * (4 * n), out_specs=[vm] * (3 * n),
                         out_shape=[jax.ShapeDtypeStruct(w.shape, F32) for w in ws] * 3, name="adamw_small")(*ws, *gs, *ms, *vs)
    return out[:n], out[n:2 * n], out[2 * n:]


def _pair_add(g, got, core):
    _, rows, cols = g.shape
    half = rows // 2
    tr = min(256, half)
    nt = half // tr

    def kern(c_ref, g_ref, o_ref, out_ref):
        out_ref[...] = (g_ref[...] + o_ref[...]).astype(BF16)

    return pl.pallas_call(
        kern, grid_spec=pltpu.PrefetchScalarGridSpec(
            num_scalar_prefetch=1, grid=(N_CHIPS, nt),
            in_specs=[pl.BlockSpec((None, tr, cols), lambda j, i, c_ref: (j, c_ref[0] * nt + i, 0)),
                      pl.BlockSpec((None, tr, cols), lambda j, i, c_ref: (j, i, 0))],
            out_specs=pl.BlockSpec((None, tr, cols), lambda j, i, c_ref: (j, i, 0))),
        out_shape=jax.ShapeDtypeStruct((N_CHIPS, half, cols), BF16),
        compiler_params=_cp(("parallel", "parallel")), name="pair_add",
    )(core, g, got)


def _chip_add(part, got, place):
    _, half, cols = part.shape
    tr = min(256, half)
    nt = half // tr

    def kern(p_ref, part_ref, got_ref, out_ref):
        out_ref[...] = (part_ref[...].astype(F32) + got_ref[0].astype(F32) + got_ref[1].astype(F32)
                        + got_ref[2].astype(F32))

    return pl.pallas_call(
        kern, grid_spec=pltpu.PrefetchScalarGridSpec(
            num_scalar_prefetch=1, grid=(nt,),
            in_specs=[pl.BlockSpec((None, tr, cols), lambda i, p_ref: (p_ref[0], i, 0)),
                      pl.BlockSpec((3, tr, cols), lambda i, p_ref: (0, i, 0))],
            out_specs=pl.BlockSpec((tr, cols), lambda i, p_ref: (p_ref[1] * nt + i, 0))),
        out_shape=jax.ShapeDtypeStruct((2 * half, cols), F32),
        compiler_params=_cp(("parallel",)), name="chip_add",
    )(place, part, got)


def _chip_add8(g, got, place):
    _, rows, cols = g.shape
    half = rows // 2
    tr = min(256, half)
    nt = half // tr

    def kern(p_ref, g_ref, got_ref, out_ref):
        acc = g_ref[...]
        for k in range(N_DEV - 1):
            acc = acc + got_ref[k].astype(F32)
        out_ref[...] = acc

    return pl.pallas_call(
        kern, grid_spec=pltpu.PrefetchScalarGridSpec(
            num_scalar_prefetch=1, grid=(nt,),
            in_specs=[pl.BlockSpec((None, tr, cols), lambda i, p_ref: (p_ref[0], p_ref[1] * nt + i, 0)),
                      pl.BlockSpec((N_DEV - 1, tr, cols), lambda i, p_ref: (0, i, 0))],
            out_specs=pl.BlockSpec((tr, cols), lambda i, p_ref: (p_ref[1] * nt + i, 0))),
        out_shape=jax.ShapeDtypeStruct((rows, cols), F32),
        compiler_params=_cp(("parallel",)), name="chip_add8",
    )(place, g, got)


def _finale(packed, fulls):
    rows, n = packed.shape[0], len(fulls)
    half = rows // 2
    assert half % SUBLANES == 0

    def kern(in_ref, *refs):
        ins, out_ref, outs = refs[:n], refs[n], refs[n + 1:2 * n + 1]
        pair_slot, chip_slots, pair_sems, chip_send, chip_recv, back_sems, join_send, join_recv = refs[2 * n + 1:]
        x, y, c = _place()
        chip, sibling = 2 * x + y, (x, y, 1 - c)
        mine = pl.ds(pl.multiple_of(c * half, SUBLANES), half)
        joins = []
        for w in range(n):
            rws = pl.ds(c * (ins[w].shape[0] // 2), ins[w].shape[0] // 2)
            joins.append(pltpu.make_async_remote_copy(
                src_ref=ins[w].at[rws, :], dst_ref=outs[w].at[rws, :], send_sem=join_send.at[w],
                recv_sem=join_recv.at[w], device_id=sibling, device_id_type=MESH))
        swap = pltpu.make_async_remote_copy(src_ref=in_ref, dst_ref=pair_slot, send_sem=pair_sems.at[0],
                                            recv_sem=pair_sems.at[1], device_id=sibling, device_id_type=MESH)
        for cp in joins + [swap]:
            cp.start()
        swap.wait()
        chip_slots[chip] = in_ref[mine, :] + pair_slot[mine, :]
        spread = [pltpu.make_async_remote_copy(
            src_ref=chip_slots.at[chip], dst_ref=chip_slots.at[chip], send_sem=chip_send.at[k], recv_sem=chip_recv.at[k],
            device_id=(*other, c), device_id_type=MESH) for k, other in enumerate(_other_chips(x, y))]
        for cp in spread:
            cp.start()
        for cp in spread:
            cp.wait()
        out_ref[mine, :] = chip_slots[0] + chip_slots[1] + chip_slots[2] + chip_slots[3]
        back = pltpu.make_async_remote_copy(src_ref=out_ref.at[mine, :], dst_ref=out_ref.at[mine, :], send_sem=back_sems.at[0],
                                            recv_sem=back_sems.at[1], device_id=sibling, device_id_type=MESH)
        back.start()
        for cp in [back] + joins:
            cp.wait()

    vm = pl.BlockSpec(memory_space=pltpu.VMEM)
    out = pl.pallas_call(
        kern, in_specs=[vm] + [ANY] * n, out_specs=[vm] + [ANY] * n,
        out_shape=[jax.ShapeDtypeStruct((rows, LANES), F32)] + [jax.ShapeDtypeStruct(f.shape, f.dtype) for f in fulls],
        input_output_aliases={w + 1: w + 1 for w in range(n)},
        scratch_shapes=[pltpu.VMEM((rows, LANES), F32), pltpu.VMEM((N_CHIPS, half, LANES), F32),
                        pltpu.SemaphoreType.DMA((2,)), pltpu.SemaphoreType.DMA((3,)), pltpu.SemaphoreType.DMA((3,)),
                        pltpu.SemaphoreType.DMA((2,)), pltpu.SemaphoreType.DMA((n,)), pltpu.SemaphoreType.DMA((n,))],
        name="finale",
    )(packed, *fulls)
    return out[0], list(out[1:])


SMALL = ["norm1_g", "conv_w", "conv_b", "lru_w_a", "lru_b_a", "lru_w_x", "lru_b_x", "lru_lambda", "lru_out_g", "sb_out_g",
         "norm2_g", "final_g"]
BIG = ["w_in", "w_out", "w_up", "w_down"]
WEIGHTS = ["norm1_g", "w_in", "conv_w", "conv_b", "lru_w_a", "lru_b_a", "lru_w_x", "lru_b_x", "lru_lambda", "lru_out_g",
           "sb_out_g", "w_out", "norm2_g", "w_up", "w_down", "final_g"]


def _pack(arrays):
    flat = []
    for a in arrays:
        a = a.reshape(-1).astype(F32)
        flat.append(jnp.pad(a, (0, (-a.shape[0]) % LANES)))
    v = jnp.concatenate(flat)
    v = jnp.pad(v, (0, (-v.shape[0]) % (LANES * 2 * SUBLANES)))
    return v.reshape(-1, LANES)


def _unpack(packed, shapes):
    v, out, off = packed.reshape(-1), [], 0
    for shp in shapes:
        size = math.prod(shp)
        out.append(v[off:off + size].reshape(shp))
        off += size + (-size) % LANES
    return out


def _blockdiag_pairs(w):
    w = w.reshape(4, 2, DH, DH)
    z = jnp.zeros((4, DH, DH), w.dtype)
    return jnp.concatenate([jnp.concatenate([w[:, 0], z], axis=2), jnp.concatenate([z, w[:, 1]], axis=2)], axis=1)


def _blockdiag_unpairs(wbd):
    return jnp.stack([wbd[:, :DH, :DH], wbd[:, DH:, DH:]], axis=1).reshape(8, DH, DH)


def _full_cols(g):
    return jnp.transpose(g, (1, 0, 2)).reshape(g.shape[1], N_CHIPS * g.shape[2])


def _local_step(x2, tgt, seq, norm1_g, w_in, conv_w, conv_b, w_a, b_a, w_x, b_x, lru_lambda, lru_out_g, sb_out_g, rest,
                norm2_g, final_g, place=None):
    alone = place is None
    wbd = jnp.concatenate([_blockdiag_pairs(w_a), _blockdiag_pairs(w_x)], axis=2).astype(BF16)
    ba, bx = b_a.reshape(1, LRU_WIDTH), b_x.reshape(1, LRU_WIDTH)
    gf = final_g.reshape(1, D_MODEL)

    xn, got = _norm1(x2, norm1_g, None if alone else ("gather2", [w_in]))
    w_in_f = w_in if alone else _full_cols(got[0])
    xl, qkv, got = _inproj(xn, w_in_f, None if alone else ("gather", [conv_w]))
    conv_w_f = conv_w if alone else _full_cols(got[0])
    h, y_lru = _lru_fwd(xl, conv_w_f, conv_b, wbd, ba, bx, lru_lambda, seq)
    o, tot, kmin, got = _attn_fwd(qkv, seq, None if alone else ("gather2", rest))
    w_out_f, w_up_f, w_down_f = rest if alone else (
        got[0].reshape(D_MODEL, D_MODEL), _full_cols(got[1]), got[2].reshape(D_FF, D_MODEL))
    h1, mix = _outproj(y_lru, o, x2, lru_out_g, sb_out_g, w_out_f)
    hn, up, u2, dh2, dh2b, loss_part, d_final = _mlp_loss(h1, norm2_g, w_up_f, w_down_f, tgt, gf)

    dpre = _mlp_bwd_pre(dh2b, w_down_f, up)
    g_w_down = _weight_grad(u2, dh2b, "dw_down", 2 * ROW_TILE).reshape(N_CHIPS, D_FF // N_CHIPS, D_MODEL)
    g_w_up = _weight_grad(hn, dpre, "dw_up", 2 * ROW_TILE, split_cols=True)
    dh1, dh1b, d_norm2, _ = _proj_bwd_norm([dpre], w_up_f, h1, norm2_g, dh2, "mlp_bwd_in", bf16_copy=True)
    g_w_out = _weight_grad(mix, dh1b, "dw_out", 4 * ROW_TILE).reshape(N_CHIPS, D_MODEL // N_CHIPS, D_MODEL)
    late = [g_w_out, g_w_up, g_w_down]
    dy_lru, do, d_ga, d_gb, swapped = _outproj_bwd(dh1b, w_out_f, y_lru, o, lru_out_g, sb_out_g,
                                                   None if alone else ("swap", late))
    parts = None if alone else [_pair_add(g, r, place[1:]) for g, r in zip(late, swapped)]
    dq, dk, dv = _attn_bwd(qkv, do, tot, kmin, seq)
    dxl, lru_small, d_wbd, got = _lru_bwd(xl, h, dy_lru, conv_w_f, conv_b, wbd, ba, bx, lru_lambda, seq,
                                          None if alone else ("exchange", parts))
    if not alone:
        late = [_chip_add(p, r, place) for p, r in zip(parts, got)]
    g_w_in, g_w_in_b = _dw_in(xn, [dxl, dq, dk, dv])
    dx, _, d_norm1, got = _proj_bwd_norm([dxl, dq, dk, dv], w_in_f, x2, norm1_g, dh1, "inproj_bwd",
                                         None if alone else ("exchange8", [g_w_in_b]))
    if not alone:
        g_w_in = _chip_add8(g_w_in, got[0], place)
    small_parts = {
        "norm1_g": d_norm1, "conv_w": lru_small[:CONV_WIDTH], "conv_b": lru_small[4:5],
        "lru_w_a": _blockdiag_unpairs(d_wbd[:, :, :LANES]), "lru_b_a": lru_small[5:6],
        "lru_w_x": _blockdiag_unpairs(d_wbd[:, :, LANES:]), "lru_b_x": lru_small[6:7], "lru_lambda": lru_small[7:8],
        "lru_out_g": d_ga, "sb_out_g": d_gb, "norm2_g": d_norm2, "final_g": d_final,
    }
    return loss_part, dx, [g_w_in] + late, small_parts


def kernel(x, norm1_g, w_in, conv_w, conv_b, lru_w_a, lru_b_a, lru_w_x, lru_b_x, lru_lambda, lru_out_g, sb_out_g, w_out, norm2_g, w_up, w_down, final_g, loss_target, m_norm1_g, m_w_in, m_conv_w, m_conv_b, m_lru_w_a, m_lru_b_a, m_lru_w_x, m_lru_b_x, m_lru_lambda, m_lru_out_g, m_sb_out_g, m_w_out, m_norm2_g, m_w_up, m_w_down, m_final_g, v_norm1_g, v_w_in, v_conv_w, v_conv_b, v_lru_w_a, v_lru_b_a, v_lru_w_x, v_lru_b_x, v_lru_lambda, v_lru_out_g, v_sb_out_g, v_w_out, v_norm2_g, v_w_up, v_w_down, v_final_g):
    given = dict(locals())
    ne, seq, _ = x.shape
    t = ne * seq
    xi, yi, ci = _place()
    place = jnp.stack([2 * xi + yi, ci]).astype(jnp.int32)

    loss_part, dx, halves, small_parts = _local_step(
        x.reshape(t, D_MODEL), loss_target.reshape(t, D_MODEL), seq, norm1_g, w_in[0].astype(BF16), conv_w[0], conv_b,
        lru_w_a[0], lru_b_a, lru_w_x[0], lru_b_x, lru_lambda, lru_out_g, sb_out_g,
        [w_out[0].astype(BF16), w_up[0].astype(BF16), w_down[0].astype(BF16)], norm2_g, final_g, place)

    full_shapes = {n: ((CONV_WIDTH, LRU_WIDTH) if n == "conv_w" else given[n].shape) for n in SMALL}
    red, fulls = _finale(_pack([small_parts[n] for n in SMALL] + [loss_part]), halves)
    red_list = _unpack(red, [full_shapes[n] for n in SMALL] + [(1, LANES)])
    grads = dict(zip(SMALL, red_list[:-1]))
    loss = red_list[-1][0, 0]
    grads["conv_w"] = lax.dynamic_slice_in_dim(grads["conv_w"], place[0] * (LRU_WIDTH // N_CHIPS), LRU_WIDTH // N_CHIPS,
                                               axis=1).reshape(conv_w.shape)
    for n, full in zip(BIG, fulls):
        grads[n] = full.reshape(given[n].shape)

    delta, new_m, new_v = {}, {}, {}
    for n in BIG:
        shp = given[n].shape
        d, m2, v2 = _adamw(given[n][0], grads[n][0], given["m_" + n][0], given["v_" + n][0], "adamw_" + n)
        delta[n], new_m[n], new_v[n] = d.reshape(shp), m2.reshape(shp), v2.reshape(shp)
    as2d = lambda a: a.reshape(-1, a.shape[-1])
    ds, m2s, v2s = _adamw_small([as2d(given[n]) for n in SMALL], [as2d(grads[n]) for n in SMALL],
                                [as2d(given["m_" + n]) for n in SMALL], [as2d(given["v_" + n]) for n in SMALL])
    for n, dd, mm, vv in zip(SMALL, ds, m2s, v2s):
        shp = given[n].shape
        delta[n], new_m[n], new_v[n] = dd.reshape(shp), mm.reshape(shp), vv.reshape(shp)

    return (loss, dx.reshape(x.shape), *[grads[n] for n in WEIGHTS], *[delta[n] for n in WEIGHTS],
            *[new_m[n] for n in WEIGHTS], *[new_v[n] for n in WEIGHTS])
```

```python
import functools
import math

import jax
import jax.numpy as jnp
from jax import lax
from jax.experimental import pallas as pl
from jax.experimental.pallas import tpu as pltpu

F32, BF16 = jnp.float32, jnp.bfloat16
MESH = pl.DeviceIdType.MESH

D_MODEL = 1024
LRU_WIDTH = 512
SB_WIDTH = 512
DH = 64
IN_COLS = 2 * LRU_WIDTH + 3 * SB_WIDTH
D_FF = 4 * D_MODEL
CONV_WIDTH = 4
LRU_C = 8.0
EPS = 1e-6
N_CHIPS = 4
N_DEV = 8
LANES = 128
SUBLANES = 8
ROW_TILE = 512
GRAD_TILE = 1024
TQ = 512
TK = 128
ATT_SCALE = 1.0 / math.sqrt(DH)
SKIP_LOG = -105.0
BAND = 2
VMEM_LIMIT = 52 * 1024 * 1024
VMEM_LIMIT_BIG = 62 * 1024 * 1024

ADAM_LR, ADAM_B1, ADAM_B2, ADAM_EPS, ADAM_WD, ADAM_STEP = 0.001, 0.9, 0.999, 1e-08, 0.01, 10

_GELU_K = math.sqrt(2.0 / math.pi)
_GELU_C = 0.044715


def _cp(sem, vmem=VMEM_LIMIT):
    return pltpu.CompilerParams(dimension_semantics=sem, vmem_limit_bytes=vmem)


def _dot(a, b):
    return jnp.dot(a, b, preferred_element_type=F32)


def _dot_nt(a, b):
    return lax.dot_general(a, b, (((1,), (1,)), ((), ())), preferred_element_type=F32)


def _dot_tn(a, b):
    return lax.dot_general(a, b, (((0,), (0,)), ((), ())), preferred_element_type=F32)


def _rstd(x):
    return lax.rsqrt(jnp.mean(x * x, axis=-1, keepdims=True) + EPS)


def _rms_bwd(x, g, dy):
    r = _rstd(x)
    gd = g * dy
    dx = r * gd - x * (r * r * r) * jnp.mean(x * gd, axis=-1, keepdims=True)
    return dx, jnp.sum(dy * x * r, axis=0, keepdims=True)


def _sigmoid(x):
    return 0.5 * jnp.tanh(0.5 * x) + 0.5


def _softplus(x):
    return jnp.maximum(x, 0.0) + jnp.log(1.0 + jnp.exp(-jnp.abs(x)))


def _neg_expm1(x, ex):
    series = -x * (1.0 + x * (0.5 + x * (1.0 / 6.0)))
    return jnp.where(x > -2.0 ** -7, series, 1.0 - ex)


def _gelu(g):
    t = jnp.tanh(_GELU_K * (g + _GELU_C * g * g * g))
    return 0.5 * g * (1.0 + t), t


def _gelu_grad(g, t):
    return 0.5 * (1.0 + t) + 0.5 * g * (1.0 - t * t) * _GELU_K * (1.0 + 3.0 * _GELU_C * g * g)


def _rows(shape):
    return lax.broadcasted_iota(jnp.int32, shape, 0)


def _shift_down(x, s, fill):
    n, c = x.shape
    if s % SUBLANES == 0:
        return jnp.concatenate([jnp.broadcast_to(jnp.asarray(fill, x.dtype), (s, c)), x[:n - s]], axis=0)
    return jnp.where(_rows(x.shape) >= s, pltpu.roll(x, s, 0), fill)


def _shift_up(x, s, fill):
    n, c = x.shape
    if s % SUBLANES == 0:
        return jnp.concatenate([x[s:], jnp.broadcast_to(jnp.asarray(fill, x.dtype), (s, c))], axis=0)
    return jnp.where(_rows(x.shape) < n - s, pltpu.roll(x, n - s, 0), fill)


def _row_of(x, idx):
    return jnp.sum(jnp.where(_rows(x.shape) == idx, x, 0.0), axis=0, keepdims=True)


def _weight_grad(a, b, name, tk, split_cols=False):
    (kk, m), n = a.shape, b.shape[1]
    tm, tn, tk = min(GRAD_TILE, m), min(GRAD_TILE, n), min(tk, kk)
    assert m % tm == 0 and n % tn == 0 and kk % tk == 0, (name, m, n, kk)
    nk = kk // tk

    def kern(a_ref, b_ref, o_ref, acc_ref):
        k = pl.program_id(2)

        @pl.when(k == 0)
        def _():
            acc_ref[...] = jnp.zeros_like(acc_ref)

        acc_ref[...] += _dot_tn(a_ref[...], b_ref[...])

        @pl.when(k == nk - 1)
        def _():
            o_ref[...] = acc_ref[...]

    if split_cols:
        out_shape = jax.ShapeDtypeStruct((n // tn, m, tn), F32)
        o_spec = pl.BlockSpec((None, tm, tn), lambda i, j, k: (j, i, 0))
    else:
        out_shape = jax.ShapeDtypeStruct((m, n), F32)
        o_spec = pl.BlockSpec((tm, tn), lambda i, j, k: (i, j))
    return pl.pallas_call(
        kern, grid=(m // tm, n // tn, nk),
        in_specs=[pl.BlockSpec((tk, tm), lambda i, j, k: (k, i)), pl.BlockSpec((tk, tn), lambda i, j, k: (k, j))],
        out_specs=o_spec, out_shape=out_shape, scratch_shapes=[pltpu.VMEM((tm, tn), F32)],
        compiler_params=_cp(("parallel", "parallel", "arbitrary")), name=name,
    )(a, b)


ANY = pl.BlockSpec(memory_space=pl.ANY)


def _place():
    return lax.axis_index("x"), lax.axis_index("y"), lax.axis_index("c")


def _other_chips(x, y):
    return [(1 - x, y), (x, 1 - y), (1 - x, 1 - y)]


def _own_slab(shard, gathered, send_sem, recv_sem):
    x, y, c = _place()
    return pltpu.make_async_remote_copy(src_ref=shard, dst_ref=gathered.at[2 * x + y], send_sem=send_sem, recv_sem=recv_sem,
                                        device_id=(x, y, 1 - c), device_id_type=MESH)


def _gather_copies(ins, outs, send_sems, recv_sems, own_send, own_recv):
    x, y, c = _place()
    mine = 2 * x + y
    copies = []
    for w in range(len(ins)):
        copies.append(_own_slab(ins[w], outs[w], own_send.at[w], own_recv.at[w]))
        for k, chip in enumerate(_other_chips(x, y)):
            copies.append(pltpu.make_async_remote_copy(
                src_ref=ins[w], dst_ref=outs[w].at[mine], send_sem=send_sems.at[3 * w + k],
                recv_sem=recv_sems.at[3 * w + k], device_id=(*chip, c), device_id_type=MESH))
    return copies


def _gather_shapes(shards):
    return ([jax.ShapeDtypeStruct((N_CHIPS,) + s.shape, s.dtype) for s in shards],
            [pltpu.SemaphoreType.DMA((3 * len(shards),)), pltpu.SemaphoreType.DMA((3 * len(shards),)),
             pltpu.SemaphoreType.DMA((len(shards),)), pltpu.SemaphoreType.DMA((len(shards),))])


def _exchange_copies(ins, outs, send_sems, recv_sems):
    x, y, c = _place()
    copies = []
    for w in range(len(ins)):
        for k, chip in enumerate(_other_chips(x, y)):
            copies.append(pltpu.make_async_remote_copy(
                src_ref=ins[w].at[2 * chip[0] + chip[1]], dst_ref=outs[w].at[k], send_sem=send_sems.at[3 * w + k],
                recv_sem=recv_sems.at[3 * w + k], device_id=(*chip, c), device_id_type=MESH))
    return copies


def _exchange_shapes(parts):
    return ([jax.ShapeDtypeStruct((3,) + p.shape[1:], p.dtype) for p in parts],
            [pltpu.SemaphoreType.DMA((3 * len(parts),)), pltpu.SemaphoreType.DMA((3 * len(parts),))])


def _exchange8_copies(ins, outs, send_sems, recv_sems):
    x, y, c = _place()
    copies = []
    for w in range(len(ins)):
        half = ins[w].shape[1] // 2
        for k in range(1, N_DEV):
            px, py, pc = x ^ (k >> 2), y ^ ((k >> 1) & 1), c ^ (k & 1)
            copies.append(pltpu.make_async_remote_copy(
                src_ref=ins[w].at[2 * px + py, pl.ds(pc * half, half), :], dst_ref=outs[w].at[k - 1],
                send_sem=send_sems.at[(N_DEV - 1) * w + k - 1], recv_sem=recv_sems.at[(N_DEV - 1) * w + k - 1],
                device_id=(px, py, pc), device_id_type=MESH))
    return copies


def _exchange8_shapes(parts):
    n = (N_DEV - 1) * len(parts)
    return ([jax.ShapeDtypeStruct((N_DEV - 1, p.shape[1] // 2, p.shape[2]), p.dtype) for p in parts],
            [pltpu.SemaphoreType.DMA((n,)), pltpu.SemaphoreType.DMA((n,))])


def _swap_copies(ins, outs, send_sems, recv_sems):
    x, y, c = _place()
    copies = []
    for w in range(len(ins)):
        half = ins[w].shape[1] // 2
        copies.append(pltpu.make_async_remote_copy(
            src_ref=ins[w].at[:, pl.ds((1 - c) * half, half), :], dst_ref=outs[w], send_sem=send_sems.at[w],
            recv_sem=recv_sems.at[w], device_id=(x, y, 1 - c), device_id_type=MESH))
    return copies


def _swap_shapes(grads):
    return ([jax.ShapeDtypeStruct((g.shape[0], g.shape[1] // 2, g.shape[2]), g.dtype) for g in grads],
            [pltpu.SemaphoreType.DMA((len(grads),)), pltpu.SemaphoreType.DMA((len(grads),))])


def _gather2_copies(ins, outs, send_sems, recv_sems, own_send, own_recv, fwd_send, fwd_recv):
    x, y, c = _place()
    mine = 2 * x + y
    copies = []
    for w in range(len(ins)):
        half = ins[w].shape[0] // 2
        rows = pl.ds(c * half, half)
        copies.append(_own_slab(ins[w], outs[w], own_send.at[w], own_recv.at[w]))
        for k, chip in enumerate(_other_chips(x, y)):
            copies.append(pltpu.make_async_remote_copy(
                src_ref=ins[w].at[rows, :], dst_ref=outs[w].at[mine, rows, :], send_sem=send_sems.at[3 * w + k],
                recv_sem=recv_sems.at[3 * w + k], device_id=(*chip, c), device_id_type=MESH))
    return copies


def _gather2_forward(ins, outs, send_sems, recv_sems, own_send, own_recv, fwd_send, fwd_recv):
    x, y, c = _place()
    copies = []
    for w in range(len(ins)):
        half = ins[w].shape[0] // 2
        rows = pl.ds(c * half, half)
        for k, chip in enumerate(_other_chips(x, y)):
            slab = outs[w].at[2 * chip[0] + chip[1], rows, :]
            copies.append(pltpu.make_async_remote_copy(
                src_ref=slab, dst_ref=slab, send_sem=fwd_send.at[3 * w + k], recv_sem=fwd_recv.at[3 * w + k],
                device_id=(x, y, 1 - c), device_id_type=MESH))
    return copies


def _gather2_shapes(shards):
    n = len(shards)
    return ([jax.ShapeDtypeStruct((N_CHIPS,) + s.shape, s.dtype) for s in shards],
            [pltpu.SemaphoreType.DMA((3 * n,)), pltpu.SemaphoreType.DMA((3 * n,)), pltpu.SemaphoreType.DMA((n,)),
             pltpu.SemaphoreType.DMA((n,)), pltpu.SemaphoreType.DMA((3 * n,)), pltpu.SemaphoreType.DMA((3 * n,))])


COMM = {"gather": (_gather_copies, _gather_shapes, None), "exchange": (_exchange_copies, _exchange_shapes, None),
        "swap": (_swap_copies, _swap_shapes, None), "gather2": (_gather2_copies, _gather2_shapes, _gather2_forward),
        "exchange8": (_exchange8_copies, _exchange8_shapes, None)}


def _call(body, comm, *, grid, in_specs, out_specs, out_shape, scratch_shapes, args, name):
    ni, no, ns = len(in_specs), len(out_specs), len(scratch_shapes)
    arrays = list(comm[1]) if comm else []
    nc = len(arrays)
    first_fn, shapes_fn, second_fn = COMM[comm[0]] if comm else (None, None, None)
    c_shapes, c_sems = shapes_fn(arrays) if comm else ([], [])

    def kern(*refs):
        ins, cin, outs = refs[:ni], refs[ni:ni + nc], refs[ni + nc:ni + nc + no]
        rest = refs[ni + nc + no:]
        cout, scr, sems = rest[:nc], rest[nc:nc + ns], rest[nc + ns:]
        ids = [pl.program_id(d) for d in range(len(grid))]
        if nc:
            @pl.when(functools.reduce(lambda a, b: a & b, [i == 0 for i in ids]))
            def _():
                for cp in first_fn(cin, cout, *sems):
                    cp.start()

        body(ins, outs, scr)
        if nc:
            @pl.when(functools.reduce(lambda a, b: a & b, [i == g - 1 for i, g in zip(ids, grid)]))
            def _():
                for cp in first_fn(cin, cout, *sems):
                    cp.wait()
                if second_fn is not None:
                    more = second_fn(cin, cout, *sems)
                    for cp in more:
                        cp.start()
                    for cp in more:
                        cp.wait()

    out = pl.pallas_call(
        kern, grid=grid, in_specs=list(in_specs) + [ANY] * nc, out_specs=list(out_specs) + [ANY] * nc,
        out_shape=list(out_shape) + c_shapes, scratch_shapes=list(scratch_shapes) + c_sems,
        compiler_params=_cp(("arbitrary",) * len(grid)), name=name,
    )(*args, *arrays)
    return list(out[:no]), list(out[no:])


def _resident(shape):
    return pl.BlockSpec(shape, lambda *_: (0,) * len(shape), pipeline_mode=pl.Buffered(1))


def _norm1(x, g1, comm):
    t = x.shape[0]
    tm = min(1024, t)

    def body(ins, outs, _):
        xv = ins[0][...]
        outs[0][...] = (xv * _rstd(xv) * ins[1][...]).astype(BF16)

    row = pl.BlockSpec((tm, D_MODEL), lambda i: (i, 0))
    (xn,), got = _call(body, comm, grid=(t // tm,), in_specs=[row, pl.BlockSpec((1, D_MODEL), lambda i: (0, 0))],
                       out_specs=[row], out_shape=[jax.ShapeDtypeStruct((t, D_MODEL), BF16)], scratch_shapes=[],
                       args=(x, g1), name="norm1")
    return xn, got


def _inproj(xn, w_in, comm=None):
    t = xn.shape[0]
    tm = min(ROW_TILE, t)

    def body(ins, outs, _):
        xn_v = ins[0][...]
        outs[0][...] = _dot(xn_v, ins[1][:, : 2 * LRU_WIDTH])
        outs[1][...] = _dot(xn_v, ins[1][:, 2 * LRU_WIDTH:]).astype(BF16)

    row = lambda c: pl.BlockSpec((tm, c), lambda i: (i, 0))
    (xl, qkv), got = _call(
        body, comm, grid=(t // tm,), in_specs=[row(D_MODEL), _resident((D_MODEL, IN_COLS))],
        out_specs=[row(2 * LRU_WIDTH), row(3 * SB_WIDTH)],
        out_shape=[jax.ShapeDtypeStruct((t, 2 * LRU_WIDTH), F32), jax.ShapeDtypeStruct((t, 3 * SB_WIDTH), BF16)],
        scratch_shapes=[], args=(xn, w_in), name="inproj")
    return xl, qkv, got


def _conv_taps(hist, u):
    cat = jnp.concatenate([hist, u], axis=0)
    return [pltpu.roll(cat, CONV_WIDTH - 1 - k, 0)[SUBLANES:] for k in range(CONV_WIDTH - 1)] + [u]


def _lru_gates(c, wbd_ref, ba, bx, sp):
    gas, gxs = [], []
    for p in range(LRU_WIDTH // LANES):
        gax = _dot(c[:, LANES * p: LANES * (p + 1)].astype(BF16), wbd_ref[p])
        gas.append(gax[:, :LANES])
        gxs.append(gax[:, LANES:])
    r = _sigmoid(jnp.concatenate(gas, axis=1) + ba)
    i = _sigmoid(jnp.concatenate(gxs, axis=1) + bx)
    la = (-LRU_C) * r * sp
    a = jnp.exp(la)
    e2 = _neg_expm1(2.0 * la, a * a)
    inv_mult = lax.rsqrt(jnp.maximum(e2, 1e-30))
    return r, i, a, e2 * inv_mult, inv_mult


def _scan_fwd(a, b):
    s = 1
    while s < a.shape[0]:
        b = b + a * _shift_down(b, s, 0.0)
        a = a * _shift_down(a, s, 1.0)
        s *= 2
    return a, b


def _scan_rev(a, b):
    s = 1
    while s < a.shape[0]:
        b = b + a * _shift_up(b, s, 0.0)
        a = a * _shift_up(a, s, 1.0)
        s *= 2
    return a, b


def _lru_param_specs(grid_rank):
    z2 = (lambda e, c: (0, 0)) if grid_rank == 2 else None
    return [
        pl.BlockSpec((CONV_WIDTH, LRU_WIDTH), z2), pl.BlockSpec((1, LRU_WIDTH), z2),
        pl.BlockSpec((LRU_WIDTH // LANES, LANES, 2 * LANES), lambda e, c: (0, 0, 0)),
        pl.BlockSpec((1, LRU_WIDTH), z2), pl.BlockSpec((1, LRU_WIDTH), z2), pl.BlockSpec((1, LRU_WIDTH), z2),
    ]


def _lru_fwd(xl, conv_w, conv_b, wbd, ba, bx, lam, seq):
    t = xl.shape[0]
    tc = min(512, seq)
    nc = seq // tc

    def kern(u_ref, g_ref, cw_ref, cb_ref, wbd_ref, ba_ref, bx_ref, lam_ref, h_ref, y_ref, hist_ref, hcar_ref):
        @pl.when(pl.program_id(1) == 0)
        def _():
            hist_ref[...] = jnp.zeros_like(hist_ref)
            hcar_ref[...] = jnp.zeros_like(hcar_ref)

        u = u_ref[...]
        taps = _conv_taps(hist_ref[...], u)
        hist_ref[...] = u_ref[tc - SUBLANES:, :]
        c = cb_ref[...]
        for k in range(CONV_WIDTH):
            c = c + taps[k] * cw_ref[k:k + 1, :]
        sp = _softplus(-lam_ref[...])
        _, i, a, mult, _ = _lru_gates(c, wbd_ref, ba_ref[...], bx_ref[...], sp)
        aa, bb = _scan_fwd(a, mult * i * c)
        h = bb + aa * hcar_ref[0:1, :]
        h_ref[...] = h
        hcar_ref[0:1, :] = h_ref[tc - 1:tc, :]
        y_ref[...] = h * _gelu(g_ref[...])[0]

    chunk = lambda col: pl.BlockSpec((tc, LRU_WIDTH), lambda e, c: (e * nc + c, col))
    out = jax.ShapeDtypeStruct((t, LRU_WIDTH), F32)
    return pl.pallas_call(
        kern, grid=(t // seq, nc), in_specs=[chunk(0), chunk(1)] + _lru_param_specs(2),
        out_specs=[chunk(0), chunk(0)], out_shape=[out, out],
        scratch_shapes=[pltpu.VMEM((SUBLANES, LRU_WIDTH), F32), pltpu.VMEM((SUBLANES, LRU_WIDTH), F32)],
        compiler_params=_cp(("arbitrary", "arbitrary")), name="lru_fwd",
    )(xl, xl, conv_w, conv_b, wbd, ba, bx, lam)


def _att_consts():
    row = lax.broadcasted_iota(jnp.int32, (TQ, 2 * TK), 0)
    key = lax.broadcasted_iota(jnp.int32, (TQ, 2 * TK), 1) & (TK - 1)
    return key < row


def _sum_matrix(kind):
    j = lax.broadcasted_iota(jnp.int32, (2 * TK, 2 * TK), 0) & (TK - 1)
    s = lax.broadcasted_iota(jnp.int32, (2 * TK, 2 * TK), 1)
    pick = {"after": j > s, "upto": j <= s, "before": j < s}[kind]
    return jnp.where((s >= TK) | pick, 1.0, 0.0).astype(BF16)


def _hi_lo(x):
    hi = x.astype(BF16)
    return hi, (x - hi.astype(F32)).astype(BF16)


def _pair_sums(x, m):
    hi, lo = _hi_lo(x)
    out = []
    for hd in range(2):
        cols = slice(hd * TK, (hd + 1) * TK)
        out.append(_dot(jnp.concatenate([hi[:, cols], lo[:, cols]], axis=1), m))
    return [o[:, :TK] for o in out], [o[:, TK:] for o in out]


def _att_logits(qb, kbd):
    z = _dot(qb, kbd)
    lg = jnp.log(1.0 + jnp.exp(-jnp.abs(z)))
    lb = jnp.minimum(z, 0.0) - lg
    return lb, lb - z


def _head_diag(x, rows_first):
    n = x.shape[0] if rows_first else x.shape[1]
    idx = lax.broadcasted_iota(jnp.int32, x.shape, 0 if rows_first else 1)
    return jnp.where(idx < n // 2, x, 0), jnp.where(idx >= n // 2, x, 0)


def _band_tiles():
    nd = TQ // TK
    return [(jr, TK * max(jr, 0), TK * min(jr + BAND + 1, nd)) for jr in range(nd - 1, -BAND - 1, -1)]


def _rows_update(st, new, lo, hi):
    def one(x, y):
        pieces = ([x[:lo]] if lo else []) + [y] + ([x[hi:]] if hi < x.shape[0] else [])
        return pieces[0] if len(pieces) == 1 else jnp.concatenate(pieces, axis=0)
    return tuple(one(x, y) for x, y in zip(st, new))


def _scaled_q(q_ref, q0):
    return (q_ref[pl.ds(q0, TQ), :].astype(F32) * ATT_SCALE).astype(BF16)


def _qkv_specs(seq):
    n = SB_WIDTH // LANES
    return [pl.BlockSpec((seq, LANES), lambda e, p, off=off: (e, off * n + p)) for off in range(3)]


def _attn_fwd(qkv, seq, comm=None):
    t = qkv.shape[0]
    ne, nq, nk = t // seq, seq // TQ, seq // TK

    def body(ins, outs, scr):
        (q_ref, k_ref, v_ref), (o_ref, tot_ref, kmin_ref), (kbd_scr, vbd_scr) = ins, outs, scr
        causal = _att_consts()
        after = _sum_matrix("after")

        def prep(j, _):
            k0 = pl.multiple_of(j * TK, TK)
            top, bot = _head_diag(k_ref[pl.ds(k0, TK), :].T, True)
            kbd_scr[j] = jnp.concatenate([top, bot], axis=1)
            left, right = _head_diag(v_ref[pl.ds(k0, TK), :], False)
            vbd_scr[j] = jnp.concatenate([left, right], axis=0)
            return 0

        lax.fori_loop(0, nk, prep, 0)

        def block(j, qb, st, mask):
            c0, c1, oacc = st
            lb, l1 = _att_logits(qb, kbd_scr[j])
            if mask is not None:
                l1 = jnp.where(mask, l1, 0.0)
            (s0, s1), (r0, r1) = _pair_sums(l1, after)
            att = jnp.exp(lb + jnp.concatenate([s0 + c0, s1 + c1], axis=1))
            if mask is not None:
                att = jnp.where(mask, att, 0.0)
            return c0 + r0, c1 + r1, oacc + _dot(att.astype(BF16), vbd_scr[j])

        def general(qi, qb, st):
            for jj in reversed(range(TQ // TK)):
                lo = TK * jj
                new = block((TQ // TK) * qi + jj, qb[lo:], tuple(x[lo:] for x in st), causal[:TQ - lo])
                st = _rows_update(st, new, lo, TQ)

            npair = (TQ // TK // 2) * qi

            def more(its):
                return (its[0] < npair) & (jnp.max(jnp.maximum(its[1], its[2])) > SKIP_LOG)

            def kloop(its):
                j = 2 * (npair - its[0]) - 1
                return (its[0] + 1,) + block(j - 1, qb, block(j, qb, its[1:], None), None)

            done, c0, c1, oacc = lax.while_loop(more, kloop, (jnp.int32(0),) + st)
            return c0, c1, oacc, 2 * (npair - done)

        def short(qi, qb, st):
            for jr, lo, hi in _band_tiles():
                new = block((TQ // TK) * qi + jr, qb[lo:hi], tuple(x[lo:hi] for x in st),
                            causal[:hi - lo] if jr >= 0 else None)
                st = _rows_update(st, new, lo, hi)
            return st

        def qloop(qi, _):
            q0 = pl.multiple_of(qi * TQ, TQ)
            qb = _scaled_q(q_ref, q0)
            zero = jnp.zeros((TQ, TK), F32)
            st = (zero, zero, jnp.zeros((TQ, LANES), F32))

            def try_short():
                c0, c1, oacc = short(qi, qb, st)
                return lax.cond(jnp.max(jnp.maximum(c0, c1)) <= SKIP_LOG, lambda: (c0, c1, oacc, jnp.int32(-1)),
                                lambda: general(qi, qb, st))

            c0, c1, oacc, first = lax.cond(qi > 0, try_short, lambda: general(qi, qb, st))
            o_ref[pl.ds(q0, TQ), :] = oacc
            tot_ref[pl.ds(q0, TQ), :] = jnp.concatenate([c0, c1], axis=1)
            kmin_ref[pl.program_id(0), pl.program_id(1), qi] = first
            return 0

        lax.fori_loop(0, nq, qloop, 0)

    (o, tot, kmin), got = _call(
        body, comm, grid=(ne, SB_WIDTH // LANES), in_specs=_qkv_specs(seq),
        out_specs=[pl.BlockSpec((seq, LANES), lambda e, p: (e, p)), pl.BlockSpec((seq, 2 * TK), lambda e, p: (e, p)),
                   pl.BlockSpec(memory_space=pltpu.SMEM)],
        out_shape=[jax.ShapeDtypeStruct((t, SB_WIDTH), F32), jax.ShapeDtypeStruct((t, 2 * TK * SB_WIDTH // LANES), F32),
                   jax.ShapeDtypeStruct((ne, SB_WIDTH // LANES, nq), jnp.int32)],
        scratch_shapes=[pltpu.VMEM((nk, LANES, 2 * TK), BF16), pltpu.VMEM((nk, 2 * TK, LANES), BF16)],
        args=(qkv, qkv, qkv), name="attn_fwd")
    return o, tot, kmin, got


def _outproj(y_lru, o, x, ga, gb, w_out):
    t = x.shape[0]
    tm = min(ROW_TILE, t)

    def kern(y_ref, o_ref, x_ref, ga_ref, gb_ref, w_ref, h1_ref, mix_ref):
        yv, ov = y_ref[...], o_ref[...]
        mix = jnp.concatenate([yv * _rstd(yv) * ga_ref[...], ov * _rstd(ov) * gb_ref[...]], axis=1).astype(BF16)
        mix_ref[...] = mix
        h1_ref[...] = x_ref[...] + _dot(mix, w_ref[...])

    row = lambda c: pl.BlockSpec((tm, c), lambda i: (i, 0))
    vec = lambda c: pl.BlockSpec((1, c), lambda i: (0, 0))
    return pl.pallas_call(
        kern, grid=(t // tm,),
        in_specs=[row(LRU_WIDTH), row(SB_WIDTH), row(D_MODEL), vec(LRU_WIDTH), vec(SB_WIDTH),
                  pl.BlockSpec((D_MODEL, D_MODEL), lambda i: (0, 0))],
        out_specs=[row(D_MODEL), row(D_MODEL)],
        out_shape=[jax.ShapeDtypeStruct((t, D_MODEL), F32), jax.ShapeDtypeStruct((t, D_MODEL), BF16)],
        compiler_params=_cp(("parallel",)), name="outproj",
    )(y_lru, o, x, ga, gb, w_out)


def _mlp_loss(h1, g2, w_up, w_down, target, gf):
    t = h1.shape[0]
    tm, tf = min(ROW_TILE, t), 1024

    def kern(h1_ref, g_ref, wu_ref, wd_ref, t_ref, gf_ref, hn_ref, up_ref, u2_ref, dh_ref, dhb_ref, loss_ref, dg_ref):
        @pl.when(pl.program_id(0) == 0)
        def _():
            loss_ref[...] = jnp.zeros_like(loss_ref)
            dg_ref[...] = jnp.zeros_like(dg_ref)

        hv = h1_ref[...]
        hn = (hv * _rstd(hv) * g_ref[...]).astype(BF16)
        hn_ref[...] = hn
        h2 = hv
        for f in range(D_FF // tf):
            cols = slice(f * tf, (f + 1) * tf)
            up = jnp.maximum(_dot(hn, wu_ref[:, cols]), 0.0)
            u2 = (up * up).astype(BF16)
            up_ref[:, cols] = up.astype(BF16)
            u2_ref[:, cols] = u2
            h2 = h2 + _dot(u2, wd_ref[cols, :])

        g = gf_ref[...]
        err = h2 * _rstd(h2) * g - t_ref[...]
        lane = lax.broadcasted_iota(jnp.int32, (1, LANES), 1)
        loss_ref[...] += jnp.where(lane == 0, 0.5 * jnp.sum(err * err) / D_MODEL, 0.0)
        dx, dg = _rms_bwd(h2, g, err * (1.0 / D_MODEL))
        dh_ref[...] = dx
        dhb_ref[...] = dx.astype(BF16)
        dg_ref[...] += dg

    row = lambda c: pl.BlockSpec((tm, c), lambda i: (i, 0))
    vec = pl.BlockSpec((1, D_MODEL), lambda i: (0, 0))
    return pl.pallas_call(
        kern, grid=(t // tm,),
        in_specs=[row(D_MODEL), vec, _resident((D_MODEL, D_FF)), _resident((D_FF, D_MODEL)), row(D_MODEL), vec],
        out_specs=[row(D_MODEL), row(D_FF), row(D_FF), row(D_MODEL), row(D_MODEL), pl.BlockSpec((1, LANES), lambda i: (0, 0)), vec],
        out_shape=[jax.ShapeDtypeStruct((t, D_MODEL), BF16), jax.ShapeDtypeStruct((t, D_FF), BF16),
                   jax.ShapeDtypeStruct((t, D_FF), BF16), jax.ShapeDtypeStruct((t, D_MODEL), F32),
                   jax.ShapeDtypeStruct((t, D_MODEL), BF16), jax.ShapeDtypeStruct((1, LANES), F32),
                   jax.ShapeDtypeStruct((1, D_MODEL), F32)],
        compiler_params=_cp(("arbitrary",), VMEM_LIMIT_BIG), name="mlp_loss",
    )(h1, g2, w_up, w_down, target, gf)


def _mlp_bwd_pre(dh2, w_down, up):
    t = dh2.shape[0]
    tm, tf = min(ROW_TILE, t), 1024

    def kern(d_ref, w_ref, up_ref, o_ref):
        dv = d_ref[...]
        for f in range(D_FF // tf):
            cols = slice(f * tf, (f + 1) * tf)
            o_ref[:, cols] = (_dot_nt(dv, w_ref[cols, :]) * (2.0 * up_ref[:, cols].astype(F32))).astype(BF16)

    row = lambda c: pl.BlockSpec((tm, c), lambda i: (i, 0))
    return pl.pallas_call(
        kern, grid=(t // tm,), in_specs=[row(D_MODEL), _resident((D_FF, D_MODEL)), row(D_FF)],
        out_specs=row(D_FF), out_shape=jax.ShapeDtypeStruct((t, D_FF), BF16),
        compiler_params=_cp(("parallel",)), name="mlp_bwd_pre",
    )(dh2, w_down, up)


def _proj_bwd_norm(dys, w, x, g, resid, name, comm=None, bf16_copy=False):
    t = x.shape[0]
    tm = min(ROW_TILE, t)
    widths = [dy.shape[1] for dy in dys]
    n = len(dys)

    def body(ins, outs, _):
        dy_refs, (w_ref, x_ref, g_ref, r_ref), (dx_ref, dg_ref) = ins[:n], ins[n:], outs[:2]

        @pl.when(pl.program_id(0) == 0)
        def _():
            dg_ref[...] = jnp.zeros_like(dg_ref)

        off, dxn = 0, None
        for dy_ref, wd in zip(dy_refs, widths):
            part = _dot_nt(dy_ref[...], w_ref[:, off:off + wd])
            dxn = part if dxn is None else dxn + part
            off += wd
        dx, dg = _rms_bwd(x_ref[...], g_ref[...], dxn)
        dx = r_ref[...] + dx
        dx_ref[...] = dx
        dg_ref[...] += dg
        if bf16_copy:
            outs[2][...] = dx.astype(BF16)

    row = lambda c: pl.BlockSpec((tm, c), lambda i: (i, 0))
    vec = pl.BlockSpec((1, D_MODEL), lambda i: (0, 0))
    outs, got = _call(
        body, comm, grid=(t // tm,), in_specs=[row(wd) for wd in widths] + [_resident(w.shape), row(D_MODEL), vec, row(D_MODEL)],
        out_specs=[row(D_MODEL), vec] + [row(D_MODEL)] * bf16_copy,
        out_shape=[jax.ShapeDtypeStruct((t, D_MODEL), F32), jax.ShapeDtypeStruct((1, D_MODEL), F32)]
        + [jax.ShapeDtypeStruct((t, D_MODEL), BF16)] * bf16_copy,
        scratch_shapes=[], args=(*dys, w, x, g, resid), name=name)
    return outs[0], (outs[2] if bf16_copy else None), outs[1], got


def _dw_in(xn, pieces):
    t = xn.shape[0]
    tk = min(2 * ROW_TILE, t)
    widths = [p.shape[1] for p in pieces]
    shard = IN_COLS // N_CHIPS

    def windows(j):
        out, off = [], 0
        for i, wd in enumerate(widths):
            a, b = max(j * shard, off), min((j + 1) * shard, off + wd)
            if a < b:
                assert (a - off) % LANES == 0 and (b - off) % LANES == 0
                out.append((i, a - off, b - off))
            off += wd
        return out

    def kern(x_ref, *refs):
        p_refs, o_ref, ob_ref = refs[:-2], refs[-2], refs[-1]

        @pl.when(pl.program_id(0) == 0)
        def _():
            o_ref[...] = jnp.zeros_like(o_ref)

        xv = x_ref[...]
        for j in range(N_CHIPS):
            cols = jnp.concatenate([p_refs[i][:, a:b] for i, a, b in windows(j)], axis=1)
            o_ref[j] += _dot_tn(xv, cols)

        @pl.when(pl.program_id(0) == t // tk - 1)
        def _():
            ob_ref[...] = o_ref[...].astype(BF16)

    row = lambda c: pl.BlockSpec((tk, c), lambda k: (k, 0))
    whole = pl.BlockSpec((N_CHIPS, D_MODEL, shard), lambda k: (0, 0, 0))
    return pl.pallas_call(
        kern, grid=(t // tk,), in_specs=[row(D_MODEL)] + [row(wd) for wd in widths], out_specs=[whole, whole],
        out_shape=[jax.ShapeDtypeStruct((N_CHIPS, D_MODEL, shard), F32), jax.ShapeDtypeStruct((N_CHIPS, D_MODEL, shard), BF16)],
        compiler_params=_cp(("arbitrary",)), name="dw_in",
    )(xn, *pieces)


def _outproj_bwd(dh1, w_out, y_lru, o, ga, gb, comm=None):
    t = dh1.shape[0]
    tm = min(ROW_TILE, t)

    def body(ins, outs, _):
        (d_ref, w_ref, y_ref, o_ref, ga_ref, gb_ref), (dy_ref, do_ref, dga_ref, dgb_ref) = ins, outs

        @pl.when(pl.program_id(0) == 0)
        def _():
            dga_ref[...] = jnp.zeros_like(dga_ref)
            dgb_ref[...] = jnp.zeros_like(dgb_ref)

        dmix = _dot_nt(d_ref[...], w_ref[...])
        dy, dga = _rms_bwd(y_ref[...], ga_ref[...], dmix[:, :LRU_WIDTH])
        do, dgb = _rms_bwd(o_ref[...], gb_ref[...], dmix[:, LRU_WIDTH:])
        dy_ref[...] = dy
        do_ref[...] = do
        dga_ref[...] += dga
        dgb_ref[...] += dgb

    row = lambda c: pl.BlockSpec((tm, c), lambda i: (i, 0))
    vec = pl.BlockSpec((1, LRU_WIDTH), lambda i: (0, 0))
    half = jax.ShapeDtypeStruct((t, LRU_WIDTH), F32)
    gsum = jax.ShapeDtypeStruct((1, LRU_WIDTH), F32)
    outs, got = _call(body, comm, grid=(t // tm,),
                      in_specs=[row(D_MODEL), _resident((D_MODEL, D_MODEL)), row(LRU_WIDTH), row(SB_WIDTH), vec, vec],
                      out_specs=[row(LRU_WIDTH), row(SB_WIDTH), vec, vec], out_shape=[half, half, gsum, gsum],
                      scratch_shapes=[], args=(dh1, w_out, y_lru, o, ga, gb), name="outproj_bwd")
    return (*outs, got)


def _attn_bwd(qkv, do, tot, kmin, seq):
    t = qkv.shape[0]
    ne, nq, nk = t // seq, seq // TQ, seq // TK

    def kern(q_ref, k_ref, v_ref, do_ref, tot_ref, kmin_ref, dq_ref, dk_ref, dv_ref,
             kbd_scr, vtbd_scr, kbd2_scr, dkt_scr, dvt_scr):
        causal = _att_consts()
        upto, before = _sum_matrix("upto"), _sum_matrix("before")

        def prep(j, _):
            k0 = pl.multiple_of(j * TK, TK)
            kb = k_ref[pl.ds(k0, TK), :]
            top, bot = _head_diag(kb.T, True)
            kbd_scr[j] = jnp.concatenate([top, bot], axis=1)
            top, bot = _head_diag(v_ref[pl.ds(k0, TK), :].T, True)
            vtbd_scr[j] = jnp.concatenate([top, bot], axis=1)
            left, right = _head_diag(kb, False)
            kbd2_scr[j] = jnp.concatenate([left, right], axis=0)
            dkt_scr[j] = jnp.zeros((LANES, 2 * TK), F32)
            dvt_scr[j] = jnp.zeros((LANES, 2 * TK), F32)
            return 0

        lax.fori_loop(0, nk, prep, 0)

        def block(j, qb, qt, dob, dot_, totb, st, mask):
            f0, f1, p0, p1, dqacc = st
            lb, l1 = _att_logits(qb, kbd_scr[j])
            if mask is not None:
                l1 = jnp.where(mask, l1, 0.0)
            (s0, s1), (r0, r1) = _pair_sums(l1, upto)
            att = jnp.exp(lb + (totb - jnp.concatenate([s0 + f0, s1 + f1], axis=1)))
            if mask is not None:
                att = jnp.where(mask, att, 0.0)
            pw = att * _dot(dob, vtbd_scr[j])
            (e0, e1), (t0, t1) = _pair_sums(pw, before)
            dz = pw - jnp.exp(lb) * (pw + jnp.concatenate([e0 + p0, e1 + p1], axis=1))
            if mask is not None:
                dz = jnp.where(mask, dz, 0.0)
            dzb = dz.astype(BF16)
            dkt_scr[j] += _dot(qt, dzb)
            dvt_scr[j] += _dot(dot_, att.astype(BF16))
            return f0 + r0, f1 + r1, p0 + t0, p1 + t1, dqacc + _dot(dzb, kbd2_scr[j])

        def qloop(qi, _):
            q0 = pl.multiple_of(qi * TQ, TQ)
            qb = _scaled_q(q_ref, q0)
            qt = qb.T
            do32 = do_ref[pl.ds(q0, TQ), :]
            dob = do32.astype(BF16)
            dot_ = dob.T
            totb = tot_ref[pl.ds(q0, TQ), :]
            zero = jnp.zeros((TQ, TK), F32)
            st = (zero, zero, zero, zero, jnp.zeros((TQ, LANES), F32))

            k0 = kmin_ref[pl.program_id(0), pl.program_id(1), qi]

            def tile(j, lo, hi, st, masked):
                new = block(j, qb[lo:hi], qt[:, lo:hi], dob[lo:hi], dot_[:, lo:hi], totb[lo:hi],
                            tuple(x[lo:hi] for x in st), causal[:hi - lo] if masked else None)
                return _rows_update(st, new, lo, hi)

            def general():
                def kloop(it, st):
                    j = k0 + 2 * it
                    return block(j + 1, qb, qt, dob, dot_, totb, block(j, qb, qt, dob, dot_, totb, st, None), None)

                out = lax.fori_loop(0, ((TQ // TK) * qi - k0) // 2, kloop, st)
                for jj in range(TQ // TK):
                    out = tile((TQ // TK) * qi + jj, TK * jj, TQ, out, True)
                return out

            def short():
                out = st
                for jr, lo, hi in reversed(_band_tiles()):
                    out = tile((TQ // TK) * qi + jr, lo, hi, out, jr >= 0)
                return out

            st = lax.cond(k0 < 0, short, general)
            dq_ref[pl.ds(q0, TQ), :] = (st[4] * ATT_SCALE).astype(BF16)
            return 0

        lax.fori_loop(0, nq, qloop, 0)

        def finish(j, _):
            k0 = pl.multiple_of(j * TK, TK)
            head0 = lax.broadcasted_iota(jnp.int32, (LANES, TK), 0) < DH
            for src, dst in ((dkt_scr, dk_ref), (dvt_scr, dv_ref)):
                acc = src[j]
                dst[pl.ds(k0, TK), :] = jnp.where(head0, acc[:, :TK], acc[:, TK:]).astype(BF16).T
            return 0

        lax.fori_loop(0, nk, finish, 0)

    blk = pl.BlockSpec((seq, LANES), lambda e, p: (e, p))
    grad = jax.ShapeDtypeStruct((t, SB_WIDTH), BF16)
    return pl.pallas_call(
        kern, grid=(ne, SB_WIDTH // LANES),
        in_specs=_qkv_specs(seq) + [blk, pl.BlockSpec((seq, 2 * TK), lambda e, p: (e, p)),
                                    pl.BlockSpec(memory_space=pltpu.SMEM)],
        out_specs=[blk, blk, blk], out_shape=[grad, grad, grad],
        scratch_shapes=[pltpu.VMEM((nk, LANES, 2 * TK), BF16), pltpu.VMEM((nk, LANES, 2 * TK), BF16),
                        pltpu.VMEM((nk, 2 * TK, LANES), BF16), pltpu.VMEM((nk, LANES, 2 * TK), F32),
                        pltpu.VMEM((nk, LANES, 2 * TK), F32)],
        compiler_params=_cp(("parallel", "parallel")), name="attn_bwd",
    )(qkv, qkv, qkv, do, tot, kmin)


def _lru_bwd(xl, h, dy, conv_w, conv_b, wbd, ba, bx, lam, seq, comm=None):
    t = xl.shape[0]
    tc = min(512, seq)
    nc = seq // tc
    nb = tc // SUBLANES


    def body(ins, outs, scr):
        u_ref, g_ref, up_ref, h_ref, hp_ref, dy_ref, cw_ref, cb_ref, wbd_ref, ba_ref, bx_ref, lam_ref = ins
        (dxl_ref, small_ref, dwbd_ref), (lnext_ref, anext_ref, dcnext_ref) = outs, scr
        e, ci = pl.program_id(0), pl.program_id(1)
        first = ci == nc - 1

        @pl.when((e == 0) & (ci == 0))
        def _():
            small_ref[...] = jnp.zeros_like(small_ref)
            dwbd_ref[...] = jnp.zeros_like(dwbd_ref)

        @pl.when(ci == 0)
        def _():
            lnext_ref[...] = jnp.zeros_like(lnext_ref)
            anext_ref[...] = jnp.zeros_like(anext_ref)
            dcnext_ref[...] = jnp.zeros_like(dcnext_ref)

        u, g = u_ref[...], g_ref[...]
        keep = jnp.where(first, 0.0, 1.0)
        taps = _conv_taps(keep * up_ref[...], u)
        c = cb_ref[...]
        for k in range(CONV_WIDTH):
            c = c + taps[k] * cw_ref[k:k + 1, :]
        lam = lam_ref[...]
        sp = _softplus(-lam)
        r, i, a, mult, inv_mult = _lru_gates(c, wbd_ref, ba_ref[...], bx_ref[...], sp)
        gel, th = _gelu(g)
        dyv, hv = dy_ref[...], h_ref[...]
        dg = dyv * hv * _gelu_grad(g, th)

        aa, bb = _scan_rev(_shift_up(a, 1, anext_ref[0:1, :]), dyv * gel)
        lt = bb + aa * lnext_ref[0:1, :]
        lnext_ref[0:1, :] = _row_of(lt, 0)
        anext_ref[0:1, :] = _row_of(a, 0)

        hprev = _shift_down(hv, 1, keep * hp_ref[SUBLANES - 1:SUBLANES, :])
        da = lt * hprev
        dmult = lt * i * c
        di = lt * mult * c
        dc = lt * mult * i
        dla = da * a - dmult * (a * a) * inv_mult
        dga = dla * ((-LRU_C) * sp) * r * (1.0 - r)
        dgx = di * i * (1.0 - i)
        small_ref[7:8, :] += jnp.sum(dla * r, axis=0, keepdims=True) * (LRU_C * _sigmoid(-lam))
        small_ref[5:6, :] += jnp.sum(dga, axis=0, keepdims=True)
        small_ref[6:7, :] += jnp.sum(dgx, axis=0, keepdims=True)

        dcs = []
        for p in range(LRU_WIDTH // LANES):
            cols = slice(LANES * p, LANES * (p + 1))
            dgax = jnp.concatenate([dga[:, cols], dgx[:, cols]], axis=1).astype(BF16)
            dcs.append(_dot_nt(dgax, wbd_ref[p]))
            dwbd_ref[p] += _dot_tn(c[:, cols].astype(BF16), dgax)
        dc = dc + jnp.concatenate(dcs, axis=1)
        small_ref[4:5, :] += jnp.sum(dc, axis=0, keepdims=True)

        catd = jnp.concatenate([dc, dcnext_ref[...]], axis=0)
        du = dc * cw_ref[CONV_WIDTH - 1:CONV_WIDTH, :]
        for j in range(1, CONV_WIDTH):
            du = du + pltpu.roll(catd, tc + SUBLANES - j, 0)[:tc] * cw_ref[CONV_WIDTH - 1 - j:CONV_WIDTH - j, :]
        dcnext_ref[...] = dc[:SUBLANES]
        for k in range(CONV_WIDTH):
            small_ref[k:k + 1, :] += jnp.sum(dc * taps[k], axis=0, keepdims=True)
        dxl_ref[:, :LRU_WIDTH] = du.astype(BF16)
        dxl_ref[:, LRU_WIDTH:] = dg.astype(BF16)

    rev = lambda e, c: e * nc + (nc - 1 - c)
    chunk = lambda col: pl.BlockSpec((tc, LRU_WIDTH), lambda e, c: (rev(e, c), col))
    prev8 = pl.BlockSpec((SUBLANES, LRU_WIDTH), lambda e, c: (jnp.maximum(rev(e, c) * nb - 1, 0), 0))
    outs, got = _call(
        body, comm, grid=(t // seq, nc),
        in_specs=[chunk(0), chunk(1), prev8, chunk(0), prev8, chunk(0)] + _lru_param_specs(2),
        out_specs=[pl.BlockSpec((tc, 2 * LRU_WIDTH), lambda e, c: (rev(e, c), 0)),
                   pl.BlockSpec((SUBLANES, LRU_WIDTH), lambda e, c: (0, 0)),
                   pl.BlockSpec((LRU_WIDTH // LANES, LANES, 2 * LANES), lambda e, c: (0, 0, 0))],
        out_shape=[jax.ShapeDtypeStruct((t, 2 * LRU_WIDTH), BF16), jax.ShapeDtypeStruct((SUBLANES, LRU_WIDTH), F32),
                   jax.ShapeDtypeStruct((LRU_WIDTH // LANES, LANES, 2 * LANES), F32)],
        scratch_shapes=[pltpu.VMEM((SUBLANES, LRU_WIDTH), F32)] * 3,
        args=(xl, xl, xl, h, h, dy, conv_w, conv_b, wbd, ba, bx, lam), name="lru_bwd")
    return (*outs, got)


def _adam_math(w, g, m, v):
    m2 = ADAM_B1 * m + (1.0 - ADAM_B1) * g
    v2 = ADAM_B2 * v + (1.0 - ADAM_B2) * (g * g)
    m_hat = m2 / (1.0 - ADAM_B1 ** ADAM_STEP)
    v_hat = v2 / (1.0 - ADAM_B2 ** ADAM_STEP)
    return -ADAM_LR * (m_hat / (jnp.sqrt(v_hat) + ADAM_EPS) + ADAM_WD * w), m2, v2


def _adamw(w, g, m, v, name):
    rows, cols = w.shape
    tr = ROW_TILE if rows % ROW_TILE == 0 else rows

    def kern(w_ref, g_ref, m_ref, v_ref, d_ref, m2_ref, v2_ref):
        d_ref[...], m2_ref[...], v2_ref[...] = _adam_math(w_ref[...], g_ref[...], m_ref[...], v_ref[...])

    blk = pl.BlockSpec((tr, cols), lambda i: (i, 0))
    out = jax.ShapeDtypeStruct((rows, cols), F32)
    return pl.pallas_call(kern, grid=(rows // tr,), in_specs=[blk] * 4, out_specs=[blk] * 3, out_shape=[out] * 3,
                          compiler_params=_cp(("parallel",)), name=name)(w, g, m, v)


def _adamw_small(ws, gs, ms, vs):
    n = len(ws)

    def kern(*refs):
        for k in range(n):
            outs = _adam_math(refs[k][...], refs[n + k][...], refs[2 * n + k][...], refs[3 * n + k][...])
            for j in range(3):
                refs[(4 + j) * n + k][...] = outs[j]

    vm = pl.BlockSpec(memory_space=pltpu.VMEM)
    out = pl.pallas_call(kern, in_specs=[vm] * (4 * n), out_specs=[vm] * (3 * n),
                         out_shape=[jax.ShapeDtypeStruct(w.shape, F32) for w in ws] * 3, name="adamw_small")(*ws, *gs, *ms, *vs)
    return out[:n], out[n:2 * n], out[2 * n:]


def _pair_add(g, got, core):
    _, rows, cols = g.shape
    half = rows // 2
    tr = min(ROW_TILE, half)
    nt = half // tr

    def kern(c_ref, g_ref, o_ref, out_ref):
        out_ref[...] = (g_ref[...] + o_ref[...]).astype(BF16)

    return pl.pallas_call(
        kern, grid_spec=pltpu.PrefetchScalarGridSpec(
            num_scalar_prefetch=1, grid=(N_CHIPS, nt),
            in_specs=[pl.BlockSpec((None, tr, cols), lambda j, i, c_ref: (j, c_ref[0] * nt + i, 0)),
                      pl.BlockSpec((None, tr, cols), lambda j, i, c_ref: (j, i, 0))],
            out_specs=pl.BlockSpec((None, tr, cols), lambda j, i, c_ref: (j, i, 0))),
        out_shape=jax.ShapeDtypeStruct((N_CHIPS, half, cols), BF16),
        compiler_params=_cp(("parallel", "parallel")), name="pair_add",
    )(core, g, got)


def _chip_add(part, got, place):
    _, half, cols = part.shape
    tr = min(ROW_TILE, half)
    nt = half // tr

    def kern(p_ref, part_ref, got_ref, out_ref):
        out_ref[...] = (part_ref[...].astype(F32) + got_ref[0].astype(F32) + got_ref[1].astype(F32)
                        + got_ref[2].astype(F32))

    return pl.pallas_call(
        kern, grid_spec=pltpu.PrefetchScalarGridSpec(
            num_scalar_prefetch=1, grid=(nt,),
            in_specs=[pl.BlockSpec((None, tr, cols), lambda i, p_ref: (p_ref[0], i, 0)),
                      pl.BlockSpec((3, tr, cols), lambda i, p_ref: (0, i, 0))],
            out_specs=pl.BlockSpec((tr, cols), lambda i, p_ref: (p_ref[1] * nt + i, 0))),
        out_shape=jax.ShapeDtypeStruct((2 * half, cols), F32),
        compiler_params=_cp(("parallel",)), name="chip_add",
    )(place, part, got)


def _chip_add8(g, got, place):
    _, rows, cols = g.shape
    half = rows // 2
    tr = min(ROW_TILE, half)
    nt = half // tr

    def kern(p_ref, g_ref, got_ref, out_ref):
        acc = g_ref[...]
        for k in range(N_DEV - 1):
            acc = acc + got_ref[k].astype(F32)
        out_ref[...] = acc

    return pl.pallas_call(
        kern, grid_spec=pltpu.PrefetchScalarGridSpec(
            num_scalar_prefetch=1, grid=(nt,),
            in_specs=[pl.BlockSpec((None, tr, cols), lambda i, p_ref: (p_ref[0], p_ref[1] * nt + i, 0)),
                      pl.BlockSpec((N_DEV - 1, tr, cols), lambda i, p_ref: (0, i, 0))],
            out_specs=pl.BlockSpec((tr, cols), lambda i, p_ref: (p_ref[1] * nt + i, 0))),
        out_shape=jax.ShapeDtypeStruct((rows, cols), F32),
        compiler_params=_cp(("parallel",)), name="chip_add8",
    )(place, g, got)


def _finale(packed, fulls):
    rows, n = packed.shape[0], len(fulls)
    half = rows // 2
    assert half % SUBLANES == 0

    def kern(in_ref, *refs):
        ins, out_ref, outs = refs[:n], refs[n], refs[n + 1:2 * n + 1]
        pair_slot, chip_slots, pair_sems, chip_send, chip_recv, back_sems, join_send, join_recv = refs[2 * n + 1:]
        x, y, c = _place()
        chip, sibling = 2 * x + y, (x, y, 1 - c)
        mine = pl.ds(pl.multiple_of(c * half, SUBLANES), half)
        joins = []
        for w in range(n):
            rws = pl.ds(c * (ins[w].shape[0] // 2), ins[w].shape[0] // 2)
            joins.append(pltpu.make_async_remote_copy(
                src_ref=ins[w].at[rws, :], dst_ref=outs[w].at[rws, :], send_sem=join_send.at[w],
                recv_sem=join_recv.at[w], device_id=sibling, device_id_type=MESH))
        swap = pltpu.make_async_remote_copy(src_ref=in_ref, dst_ref=pair_slot, send_sem=pair_sems.at[0],
                                            recv_sem=pair_sems.at[1], device_id=sibling, device_id_type=MESH)
        for cp in joins + [swap]:
            cp.start()
        swap.wait()
        chip_slots[chip] = in_ref[mine, :] + pair_slot[mine, :]
        spread = [pltpu.make_async_remote_copy(
            src_ref=chip_slots.at[chip], dst_ref=chip_slots.at[chip], send_sem=chip_send.at[k], recv_sem=chip_recv.at[k],
            device_id=(*other, c), device_id_type=MESH) for k, other in enumerate(_other_chips(x, y))]
        for cp in spread:
            cp.start()
        for cp in spread:
            cp.wait()
        out_ref[mine, :] = chip_slots[0] + chip_slots[1] + chip_slots[2] + chip_slots[3]
        back = pltpu.make_async_remote_copy(src_ref=out_ref.at[mine, :], dst_ref=out_ref.at[mine, :], send_sem=back_sems.at[0],
                                            recv_sem=back_sems.at[1], device_id=sibling, device_id_type=MESH)
        back.start()
        for cp in [back] + joins:
            cp.wait()

    vm = pl.BlockSpec(memory_space=pltpu.VMEM)
    out = pl.pallas_call(
        kern, in_specs=[vm] + [ANY] * n, out_specs=[vm] + [ANY] * n,
        out_shape=[jax.ShapeDtypeStruct((rows, LANES), F32)] + [jax.ShapeDtypeStruct(f.shape, f.dtype) for f in fulls],
        input_output_aliases={w + 1: w + 1 for w in range(n)},
        scratch_shapes=[pltpu.VMEM((rows, LANES), F32), pltpu.VMEM((N_CHIPS, half, LANES), F32),
                        pltpu.SemaphoreType.DMA((2,)), pltpu.SemaphoreType.DMA((3,)), pltpu.SemaphoreType.DMA((3,)),
                        pltpu.SemaphoreType.DMA((2,)), pltpu.SemaphoreType.DMA((n,)), pltpu.SemaphoreType.DMA((n,))],
        name="finale",
    )(packed, *fulls)
    return out[0], list(out[1:])


SMALL = ["norm1_g", "conv_w", "conv_b", "lru_w_a", "lru_b_a", "lru_w_x", "lru_b_x", "lru_lambda", "lru_out_g", "sb_out_g",
         "norm2_g", "final_g"]
BIG = ["w_in", "w_out", "w_up", "w_down"]
WEIGHTS = ["norm1_g", "w_in", "conv_w", "conv_b", "lru_w_a", "lru_b_a", "lru_w_x", "lru_b_x", "lru_lambda", "lru_out_g",
           "sb_out_g", "w_out", "norm2_g", "w_up", "w_down", "final_g"]


def _pack(arrays):
    flat = []
    for a in arrays:
        a = a.reshape(-1).astype(F32)
        flat.append(jnp.pad(a, (0, (-a.shape[0]) % LANES)))
    v = jnp.concatenate(flat)
    v = jnp.pad(v, (0, (-v.shape[0]) % (LANES * 2 * SUBLANES)))
    return v.reshape(-1, LANES)


def _unpack(packed, shapes):
    v, out, off = packed.reshape(-1), [], 0
    for shp in shapes:
        size = math.prod(shp)
        out.append(v[off:off + size].reshape(shp))
        off += size + (-size) % LANES
    return out


def _blockdiag_pairs(w):
    w = w.reshape(4, 2, DH, DH)
    z = jnp.zeros((4, DH, DH), w.dtype)
    return jnp.concatenate([jnp.concatenate([w[:, 0], z], axis=2), jnp.concatenate([z, w[:, 1]], axis=2)], axis=1)


def _blockdiag_unpairs(wbd):
    return jnp.stack([wbd[:, :DH, :DH], wbd[:, DH:, DH:]], axis=1).reshape(8, DH, DH)


def _full_cols(g):
    return jnp.transpose(g, (1, 0, 2)).reshape(g.shape[1], N_CHIPS * g.shape[2])


def _local_step(x2, tgt, seq, norm1_g, w_in, conv_w, conv_b, w_a, b_a, w_x, b_x, lru_lambda, lru_out_g, sb_out_g, rest,
                norm2_g, final_g, place):
    wbd = jnp.concatenate([_blockdiag_pairs(w_a), _blockdiag_pairs(w_x)], axis=2).astype(BF16)
    ba, bx = b_a.reshape(1, LRU_WIDTH), b_x.reshape(1, LRU_WIDTH)
    gf = final_g.reshape(1, D_MODEL)

    xn, got = _norm1(x2, norm1_g, ("gather2", [w_in]))
    w_in_f = _full_cols(got[0])
    xl, qkv, got = _inproj(xn, w_in_f, ("gather", [conv_w]))
    conv_w_f = _full_cols(got[0])
    h, y_lru = _lru_fwd(xl, conv_w_f, conv_b, wbd, ba, bx, lru_lambda, seq)
    o, tot, kmin, got = _attn_fwd(qkv, seq, ("gather2", rest))
    w_out_f, w_up_f, w_down_f = got[0].reshape(D_MODEL, D_MODEL), _full_cols(got[1]), got[2].reshape(D_FF, D_MODEL)
    h1, mix = _outproj(y_lru, o, x2, lru_out_g, sb_out_g, w_out_f)
    hn, up, u2, dh2, dh2b, loss_part, d_final = _mlp_loss(h1, norm2_g, w_up_f, w_down_f, tgt, gf)

    dpre = _mlp_bwd_pre(dh2b, w_down_f, up)
    g_w_down = _weight_grad(u2, dh2b, "dw_down", 2 * ROW_TILE).reshape(N_CHIPS, D_FF // N_CHIPS, D_MODEL)
    g_w_up = _weight_grad(hn, dpre, "dw_up", 2 * ROW_TILE, split_cols=True)
    dh1, dh1b, d_norm2, _ = _proj_bwd_norm([dpre], w_up_f, h1, norm2_g, dh2, "mlp_bwd_in", bf16_copy=True)
    g_w_out = _weight_grad(mix, dh1b, "dw_out", 4 * ROW_TILE).reshape(N_CHIPS, D_MODEL // N_CHIPS, D_MODEL)
    late = [g_w_out, g_w_up, g_w_down]
    dy_lru, do, d_ga, d_gb, swapped = _outproj_bwd(dh1b, w_out_f, y_lru, o, lru_out_g, sb_out_g, ("swap", late))
    parts = [_pair_add(g, r, place[1:]) for g, r in zip(late, swapped)]
    dq, dk, dv = _attn_bwd(qkv, do, tot, kmin, seq)
    dxl, lru_small, d_wbd, got = _lru_bwd(xl, h, dy_lru, conv_w_f, conv_b, wbd, ba, bx, lru_lambda, seq,
                                          ("exchange", parts))
    late = [_chip_add(p, r, place) for p, r in zip(parts, got)]
    g_w_in, g_w_in_b = _dw_in(xn, [dxl, dq, dk, dv])
    dx, _, d_norm1, got = _proj_bwd_norm([dxl, dq, dk, dv], w_in_f, x2, norm1_g, dh1, "inproj_bwd",
                                         ("exchange8", [g_w_in_b]))
    g_w_in = _chip_add8(g_w_in, got[0], place)
    small_parts = {
        "norm1_g": d_norm1, "conv_w": lru_small[:CONV_WIDTH], "conv_b": lru_small[4:5],
        "lru_w_a": _blockdiag_unpairs(d_wbd[:, :, :LANES]), "lru_b_a": lru_small[5:6],
        "lru_w_x": _blockdiag_unpairs(d_wbd[:, :, LANES:]), "lru_b_x": lru_small[6:7], "lru_lambda": lru_small[7:8],
        "lru_out_g": d_ga, "sb_out_g": d_gb, "norm2_g": d_norm2, "final_g": d_final,
    }
    return loss_part, dx, [g_w_in] + late, small_parts


def kernel(x, norm1_g, w_in, conv_w, conv_b, lru_w_a, lru_b_a, lru_w_x, lru_b_x, lru_lambda, lru_out_g, sb_out_g, w_out, norm2_g, w_up, w_down, final_g, loss_target, m_norm1_g, m_w_in, m_conv_w, m_conv_b, m_lru_w_a, m_lru_b_a, m_lru_w_x, m_lru_b_x, m_lru_lambda, m_lru_out_g, m_sb_out_g, m_w_out, m_norm2_g, m_w_up, m_w_down, m_final_g, v_norm1_g, v_w_in, v_conv_w, v_conv_b, v_lru_w_a, v_lru_b_a, v_lru_w_x, v_lru_b_x, v_lru_lambda, v_lru_out_g, v_sb_out_g, v_w_out, v_norm2_g, v_w_up, v_w_down, v_final_g):
    given = dict(locals())
    ne, seq, _ = x.shape
    t = ne * seq
    xi, yi, ci = _place()
    place = jnp.stack([2 * xi + yi, ci]).astype(jnp.int32)

    loss_part, dx, halves, small_parts = _local_step(
        x.reshape(t, D_MODEL), loss_target.reshape(t, D_MODEL), seq, norm1_g, w_in[0].astype(BF16), conv_w[0], conv_b,
        lru_w_a[0], lru_b_a, lru_w_x[0], lru_b_x, lru_lambda, lru_out_g, sb_out_g,
        [w_out[0].astype(BF16), w_up[0].astype(BF16), w_down[0].astype(BF16)], norm2_g, final_g, place)

    full_shapes = {n: ((CONV_WIDTH, LRU_WIDTH) if n == "conv_w" else given[n].shape) for n in SMALL}
    red, fulls = _finale(_pack([small_parts[n] for n in SMALL] + [loss_part]), halves)
    red_list = _unpack(red, [full_shapes[n] for n in SMALL] + [(1, LANES)])
    grads = dict(zip(SMALL, red_list[:-1]))
    loss = red_list[-1][0, 0]
    grads["conv_w"] = lax.dynamic_slice_in_dim(grads["conv_w"], place[0] * (LRU_WIDTH // N_CHIPS), LRU_WIDTH // N_CHIPS,
                                               axis=1).reshape(conv_w.shape)
    for n, full in zip(BIG, fulls):
        grads[n] = full.reshape(given[n].shape)

    delta, new_m, new_v = {}, {}, {}
    for n in BIG:
        shp = given[n].shape
        d, m2, v2 = _adamw(given[n][0], grads[n][0], given["m_" + n][0], given["v_" + n][0], "adamw_" + n)
        delta[n], new_m[n], new_v[n] = d.reshape(shp), m2.reshape(shp), v2.reshape(shp)
    as2d = lambda a: a.reshape(-1, a.shape[-1])
    ds, m2s, v2s = _adamw_small([as2d(given[n]) for n in SMALL], [as2d(grads[n]) for n in SMALL],
                                [as2d(given["m_" + n]) for n in SMALL], [as2d(given["v_" + n]) for n in SMALL])
    for n, dd, mm, vv in zip(SMALL, ds, m2s, v2s):
        shp = given[n].shape
        delta[n], new_m[n], new_v[n] = dd.reshape(shp), mm.reshape(shp), vv.reshape(shp)

    return (loss, dx.reshape(x.shape), *[grads[n] for n in WEIGHTS], *[delta[n] for n in WEIGHTS],
            *[new_m[n] for n in WEIGHTS], *[new_v[n] for n in WEIGHTS])
```

```python
import functools
import math

import jax
import jax.numpy as jnp
from jax import lax
from jax.experimental import pallas as pl
from jax.experimental.pallas import tpu as pltpu

F32, BF16 = jnp.float32, jnp.bfloat16
MESH = pl.DeviceIdType.MESH

D_MODEL = 1024
LRU_WIDTH = 512
SB_WIDTH = 512
DH = 64
IN_COLS = 2 * LRU_WIDTH + 3 * SB_WIDTH
D_FF = 4 * D_MODEL
CONV_WIDTH = 4
LRU_C = 8.0
EPS = 1e-6
N_CHIPS = 4
N_DEV = 8
LANES = 128
SUBLANES = 8
ROW_TILE = 512
GRAD_TILE = 1024
TQ = 512
TK = 128
ATT_SCALE = 1.0 / math.sqrt(DH)
SKIP_LOG = -105.0
BAND = 2
VMEM_LIMIT = 52 * 1024 * 1024
VMEM_LIMIT_BIG = 62 * 1024 * 1024

ADAM_LR, ADAM_B1, ADAM_B2, ADAM_EPS, ADAM_WD, ADAM_STEP = 0.001, 0.9, 0.999, 1e-08, 0.01, 10

_GELU_K = math.sqrt(2.0 / math.pi)
_GELU_C = 0.044715


def _cp(sem, vmem=VMEM_LIMIT):
    return pltpu.CompilerParams(dimension_semantics=sem, vmem_limit_bytes=vmem)


def _dot(a, b):
    return jnp.dot(a, b, preferred_element_type=F32)


def _dot_nt(a, b):
    return lax.dot_general(a, b, (((1,), (1,)), ((), ())), preferred_element_type=F32)


def _dot_tn(a, b):
    return lax.dot_general(a, b, (((0,), (0,)), ((), ())), preferred_element_type=F32)


def _rstd(x):
    return lax.rsqrt(jnp.mean(x * x, axis=-1, keepdims=True) + EPS)


def _rms_bwd(x, g, dy):
    r = _rstd(x)
    gd = g * dy
    dx = r * gd - x * (r * r * r) * jnp.mean(x * gd, axis=-1, keepdims=True)
    return dx, jnp.sum(dy * x * r, axis=0, keepdims=True)


def _sigmoid(x):
    return 0.5 * jnp.tanh(0.5 * x) + 0.5


def _softplus(x):
    return jnp.maximum(x, 0.0) + jnp.log(1.0 + jnp.exp(-jnp.abs(x)))


def _neg_expm1(x, ex):
    series = -x * (1.0 + x * (0.5 + x * (1.0 / 6.0)))
    return jnp.where(x > -2.0 ** -7, series, 1.0 - ex)


def _gelu(g):
    t = jnp.tanh(_GELU_K * (g + _GELU_C * g * g * g))
    return 0.5 * g * (1.0 + t), t


def _gelu_grad(g, t):
    return 0.5 * (1.0 + t) + 0.5 * g * (1.0 - t * t) * _GELU_K * (1.0 + 3.0 * _GELU_C * g * g)


def _rows(shape):
    return lax.broadcasted_iota(jnp.int32, shape, 0)


def _shift_down(x, s, fill):
    n, c = x.shape
    if s % SUBLANES == 0:
        return jnp.concatenate([jnp.broadcast_to(jnp.asarray(fill, x.dtype), (s, c)), x[:n - s]], axis=0)
    return jnp.where(_rows(x.shape) >= s, pltpu.roll(x, s, 0), fill)


def _shift_up(x, s, fill):
    n, c = x.shape
    if s % SUBLANES == 0:
        return jnp.concatenate([x[s:], jnp.broadcast_to(jnp.asarray(fill, x.dtype), (s, c))], axis=0)
    return jnp.where(_rows(x.shape) < n - s, pltpu.roll(x, n - s, 0), fill)


def _row_of(x, idx):
    return jnp.sum(jnp.where(_rows(x.shape) == idx, x, 0.0), axis=0, keepdims=True)


def _weight_grad(a, b, name, tk, split_cols=False):
    (kk, m), n = a.shape, b.shape[1]
    tm, tn, tk = min(GRAD_TILE, m), min(GRAD_TILE, n), min(tk, kk)
    assert m % tm == 0 and n % tn == 0 and kk % tk == 0, (name, m, n, kk)
    nk = kk // tk

    def kern(a_ref, b_ref, o_ref, acc_ref):
        k = pl.program_id(2)

        @pl.when(k == 0)
        def _():
            acc_ref[...] = jnp.zeros_like(acc_ref)

        acc_ref[...] += _dot_tn(a_ref[...], b_ref[...])

        @pl.when(k == nk - 1)
        def _():
            o_ref[...] = acc_ref[...]

    if split_cols:
        out_shape = jax.ShapeDtypeStruct((n // tn, m, tn), F32)
        o_spec = pl.BlockSpec((None, tm, tn), lambda i, j, k: (j, i, 0))
    else:
        out_shape = jax.ShapeDtypeStruct((m, n), F32)
        o_spec = pl.BlockSpec((tm, tn), lambda i, j, k: (i, j))
    return pl.pallas_call(
        kern, grid=(m // tm, n // tn, nk),
        in_specs=[pl.BlockSpec((tk, tm), lambda i, j, k: (k, i)), pl.BlockSpec((tk, tn), lambda i, j, k: (k, j))],
        out_specs=o_spec, out_shape=out_shape, scratch_shapes=[pltpu.VMEM((tm, tn), F32)],
        compiler_params=_cp(("parallel", "parallel", "arbitrary")), name=name,
    )(a, b)


ANY = pl.BlockSpec(memory_space=pl.ANY)


def _place():
    return lax.axis_index("x"), lax.axis_index("y"), lax.axis_index("c")


def _other_chips(x, y):
    return [(1 - x, y), (x, 1 - y), (1 - x, 1 - y)]


def _own_slab(shard, gathered, send_sem, recv_sem):
    x, y, c = _place()
    return pltpu.make_async_remote_copy(src_ref=shard, dst_ref=gathered.at[2 * x + y], send_sem=send_sem, recv_sem=recv_sem,
                                        device_id=(x, y, 1 - c), device_id_type=MESH)


def _gather_copies(ins, outs, send_sems, recv_sems, own_send, own_recv):
    x, y, c = _place()
    mine = 2 * x + y
    copies = []
    for w in range(len(ins)):
        copies.append(_own_slab(ins[w], outs[w], own_send.at[w], own_recv.at[w]))
        for k, chip in enumerate(_other_chips(x, y)):
            copies.append(pltpu.make_async_remote_copy(
                src_ref=ins[w], dst_ref=outs[w].at[mine], send_sem=send_sems.at[3 * w + k],
                recv_sem=recv_sems.at[3 * w + k], device_id=(*chip, c), device_id_type=MESH))
    return copies


def _gather_shapes(shards):
    return ([jax.ShapeDtypeStruct((N_CHIPS,) + s.shape, s.dtype) for s in shards],
            [pltpu.SemaphoreType.DMA((3 * len(shards),)), pltpu.SemaphoreType.DMA((3 * len(shards),)),
             pltpu.SemaphoreType.DMA((len(shards),)), pltpu.SemaphoreType.DMA((len(shards),))])


def _exchange_copies(ins, outs, send_sems, recv_sems):
    x, y, c = _place()
    copies = []
    for w in range(len(ins)):
        for k, chip in enumerate(_other_chips(x, y)):
            copies.append(pltpu.make_async_remote_copy(
                src_ref=ins[w].at[2 * chip[0] + chip[1]], dst_ref=outs[w].at[k], send_sem=send_sems.at[3 * w + k],
                recv_sem=recv_sems.at[3 * w + k], device_id=(*chip, c), device_id_type=MESH))
    return copies


def _exchange_shapes(parts):
    return ([jax.ShapeDtypeStruct((3,) + p.shape[1:], p.dtype) for p in parts],
            [pltpu.SemaphoreType.DMA((3 * len(parts),)), pltpu.SemaphoreType.DMA((3 * len(parts),))])


def _exchange8_copies(ins, outs, send_sems, recv_sems):
    x, y, c = _place()
    copies = []
    for w in range(len(ins)):
        half = ins[w].shape[1] // 2
        for k in range(1, N_DEV):
            px, py, pc = x ^ (k >> 2), y ^ ((k >> 1) & 1), c ^ (k & 1)
            copies.append(pltpu.make_async_remote_copy(
                src_ref=ins[w].at[2 * px + py, pl.ds(pc * half, half), :], dst_ref=outs[w].at[k - 1],
                send_sem=send_sems.at[(N_DEV - 1) * w + k - 1], recv_sem=recv_sems.at[(N_DEV - 1) * w + k - 1],
                device_id=(px, py, pc), device_id_type=MESH))
    return copies


def _exchange8_shapes(parts):
    n = (N_DEV - 1) * len(parts)
    return ([jax.ShapeDtypeStruct((N_DEV - 1, p.shape[1] // 2, p.shape[2]), p.dtype) for p in parts],
            [pltpu.SemaphoreType.DMA((n,)), pltpu.SemaphoreType.DMA((n,))])


def _swap_copies(ins, outs, send_sems, recv_sems):
    x, y, c = _place()
    copies = []
    for w in range(len(ins)):
        half = ins[w].shape[1] // 2
        copies.append(pltpu.make_async_remote_copy(
            src_ref=ins[w].at[:, pl.ds((1 - c) * half, half), :], dst_ref=outs[w], send_sem=send_sems.at[w],
            recv_sem=recv_sems.at[w], device_id=(x, y, 1 - c), device_id_type=MESH))
    return copies


def _swap_shapes(grads):
    return ([jax.ShapeDtypeStruct((g.shape[0], g.shape[1] // 2, g.shape[2]), g.dtype) for g in grads],
            [pltpu.SemaphoreType.DMA((len(grads),)), pltpu.SemaphoreType.DMA((len(grads),))])


def _gather2_copies(ins, outs, send_sems, recv_sems, own_send, own_recv, fwd_send, fwd_recv):
    x, y, c = _place()
    mine = 2 * x + y
    copies = []
    for w in range(len(ins)):
        half = ins[w].shape[0] // 2
        rows = pl.ds(c * half, half)
        copies.append(_own_slab(ins[w], outs[w], own_send.at[w], own_recv.at[w]))
        for k, chip in enumerate(_other_chips(x, y)):
            copies.append(pltpu.make_async_remote_copy(
                src_ref=ins[w].at[rows, :], dst_ref=outs[w].at[mine, rows, :], send_sem=send_sems.at[3 * w + k],
                recv_sem=recv_sems.at[3 * w + k], device_id=(*chip, c), device_id_type=MESH))
    return copies


def _gather2_forward(ins, outs, send_sems, recv_sems, own_send, own_recv, fwd_send, fwd_recv):
    x, y, c = _place()
    copies = []
    for w in range(len(ins)):
        half = ins[w].shape[0] // 2
        rows = pl.ds(c * half, half)
        for k, chip in enumerate(_other_chips(x, y)):
            slab = outs[w].at[2 * chip[0] + chip[1], rows, :]
            copies.append(pltpu.make_async_remote_copy(
                src_ref=slab, dst_ref=slab, send_sem=fwd_send.at[3 * w + k], recv_sem=fwd_recv.at[3 * w + k],
                device_id=(x, y, 1 - c), device_id_type=MESH))
    return copies


def _gather2_shapes(shards):
    n = len(shards)
    return ([jax.ShapeDtypeStruct((N_CHIPS,) + s.shape, s.dtype) for s in shards],
            [pltpu.SemaphoreType.DMA((3 * n,)), pltpu.SemaphoreType.DMA((3 * n,)), pltpu.SemaphoreType.DMA((n,)),
             pltpu.SemaphoreType.DMA((n,)), pltpu.SemaphoreType.DMA((3 * n,)), pltpu.SemaphoreType.DMA((3 * n,))])


COMM = {"gather": (_gather_copies, _gather_shapes, None), "exchange": (_exchange_copies, _exchange_shapes, None),
        "swap": (_swap_copies, _swap_shapes, None), "gather2": (_gather2_copies, _gather2_shapes, _gather2_forward),
        "exchange8": (_exchange8_copies, _exchange8_shapes, None)}


def _call(body, comm, *, grid, in_specs, out_specs, out_shape, scratch_shapes, args, name):
    ni, no, ns = len(in_specs), len(out_specs), len(scratch_shapes)
    arrays = list(comm[1]) if comm else []
    nc = len(arrays)
    first_fn, shapes_fn, second_fn = COMM[comm[0]] if comm else (None, None, None)
    c_shapes, c_sems = shapes_fn(arrays) if comm else ([], [])

    def kern(*refs):
        ins, cin, outs = refs[:ni], refs[ni:ni + nc], refs[ni + nc:ni + nc + no]
        rest = refs[ni + nc + no:]
        cout, scr, sems = rest[:nc], rest[nc:nc + ns], rest[nc + ns:]
        ids = [pl.program_id(d) for d in range(len(grid))]
        if nc:
            @pl.when(functools.reduce(lambda a, b: a & b, [i == 0 for i in ids]))
            def _():
                for cp in first_fn(cin, cout, *sems):
                    cp.start()

        body(ins, outs, scr)
        if nc:
            @pl.when(functools.reduce(lambda a, b: a & b, [i == g - 1 for i, g in zip(ids, grid)]))
            def _():
                for cp in first_fn(cin, cout, *sems):
                    cp.wait()
                if second_fn is not None:
                    more = second_fn(cin, cout, *sems)
                    for cp in more:
                        cp.start()
                    for cp in more:
                        cp.wait()

    out = pl.pallas_call(
        kern, grid=grid, in_specs=list(in_specs) + [ANY] * nc, out_specs=list(out_specs) + [ANY] * nc,
        out_shape=list(out_shape) + c_shapes, scratch_shapes=list(scratch_shapes) + c_sems,
        compiler_params=_cp(("arbitrary",) * len(grid)), name=name,
    )(*args, *arrays)
    return list(out[:no]), list(out[no:])


def _resident(shape):
    return pl.BlockSpec(shape, lambda *_: (0,) * len(shape), pipeline_mode=pl.Buffered(1))


def _norm1(x, g1, comm):
    t = x.shape[0]
    tm = min(1024, t)

    def body(ins, outs, _):
        xv = ins[0][...]
        outs[0][...] = (xv * _rstd(xv) * ins[1][...]).astype(BF16)

    row = pl.BlockSpec((tm, D_MODEL), lambda i: (i, 0))
    (xn,), got = _call(body, comm, grid=(t // tm,), in_specs=[row, pl.BlockSpec((1, D_MODEL), lambda i: (0, 0))],
                       out_specs=[row], out_shape=[jax.ShapeDtypeStruct((t, D_MODEL), BF16)], scratch_shapes=[],
                       args=(x, g1), name="norm1")
    return xn, got


def _inproj(xn, w_in, comm=None):
    t = xn.shape[0]
    tm = min(ROW_TILE, t)

    def body(ins, outs, _):
        xn_v = ins[0][...]
        outs[0][...] = _dot(xn_v, ins[1][:, : 2 * LRU_WIDTH])
        outs[1][...] = _dot(xn_v, ins[1][:, 2 * LRU_WIDTH:]).astype(BF16)

    row = lambda c: pl.BlockSpec((tm, c), lambda i: (i, 0))
    (xl, qkv), got = _call(
        body, comm, grid=(t // tm,), in_specs=[row(D_MODEL), _resident((D_MODEL, IN_COLS))],
        out_specs=[row(2 * LRU_WIDTH), row(3 * SB_WIDTH)],
        out_shape=[jax.ShapeDtypeStruct((t, 2 * LRU_WIDTH), F32), jax.ShapeDtypeStruct((t, 3 * SB_WIDTH), BF16)],
        scratch_shapes=[], args=(xn, w_in), name="inproj")
    return xl, qkv, got


def _conv_taps(hist, u):
    cat = jnp.concatenate([hist, u], axis=0)
    return [pltpu.roll(cat, CONV_WIDTH - 1 - k, 0)[SUBLANES:] for k in range(CONV_WIDTH - 1)] + [u]


def _lru_gates(c, wbd_ref, ba, bx, sp):
    gas, gxs = [], []
    for p in range(LRU_WIDTH // LANES):
        gax = _dot(c[:, LANES * p: LANES * (p + 1)].astype(BF16), wbd_ref[p])
        gas.append(gax[:, :LANES])
        gxs.append(gax[:, LANES:])
    r = _sigmoid(jnp.concatenate(gas, axis=1) + ba)
    i = _sigmoid(jnp.concatenate(gxs, axis=1) + bx)
    la = (-LRU_C) * r * sp
    a = jnp.exp(la)
    e2 = _neg_expm1(2.0 * la, a * a)
    inv_mult = lax.rsqrt(jnp.maximum(e2, 1e-30))
    return r, i, a, e2 * inv_mult, inv_mult


def _scan_fwd(a, b):
    s = 1
    while s < a.shape[0]:
        b = b + a * _shift_down(b, s, 0.0)
        a = a * _shift_down(a, s, 1.0)
        s *= 2
    return a, b


def _scan_rev(a, b):
    s = 1
    while s < a.shape[0]:
        b = b + a * _shift_up(b, s, 0.0)
        a = a * _shift_up(a, s, 1.0)
        s *= 2
    return a, b


def _lru_param_specs(grid_rank):
    z2 = (lambda e, c: (0, 0)) if grid_rank == 2 else None
    return [
        pl.BlockSpec((CONV_WIDTH, LRU_WIDTH), z2), pl.BlockSpec((1, LRU_WIDTH), z2),
        pl.BlockSpec((LRU_WIDTH // LANES, LANES, 2 * LANES), lambda e, c: (0, 0, 0)),
        pl.BlockSpec((1, LRU_WIDTH), z2), pl.BlockSpec((1, LRU_WIDTH), z2), pl.BlockSpec((1, LRU_WIDTH), z2),
    ]


def _lru_fwd(xl, conv_w, conv_b, wbd, ba, bx, lam, seq):
    t = xl.shape[0]
    tc = min(512, seq)
    nc = seq // tc

    def kern(u_ref, g_ref, cw_ref, cb_ref, wbd_ref, ba_ref, bx_ref, lam_ref, h_ref, y_ref, hist_ref, hcar_ref):
        @pl.when(pl.program_id(1) == 0)
        def _():
            hist_ref[...] = jnp.zeros_like(hist_ref)
            hcar_ref[...] = jnp.zeros_like(hcar_ref)

        u = u_ref[...]
        taps = _conv_taps(hist_ref[...], u)
        hist_ref[...] = u_ref[tc - SUBLANES:, :]
        c = cb_ref[...]
        for k in range(CONV_WIDTH):
            c = c + taps[k] * cw_ref[k:k + 1, :]
        sp = _softplus(-lam_ref[...])
        _, i, a, mult, _ = _lru_gates(c, wbd_ref, ba_ref[...], bx_ref[...], sp)
        aa, bb = _scan_fwd(a, mult * i * c)
        h = bb + aa * hcar_ref[0:1, :]
        h_ref[...] = h
        hcar_ref[0:1, :] = h_ref[tc - 1:tc, :]
        y_ref[...] = h * _gelu(g_ref[...])[0]

    chunk = lambda col: pl.BlockSpec((tc, LRU_WIDTH), lambda e, c: (e * nc + c, col))
    out = jax.ShapeDtypeStruct((t, LRU_WIDTH), F32)
    return pl.pallas_call(
        kern, grid=(t // seq, nc), in_specs=[chunk(0), chunk(1)] + _lru_param_specs(2),
        out_specs=[chunk(0), chunk(0)], out_shape=[out, out],
        scratch_shapes=[pltpu.VMEM((SUBLANES, LRU_WIDTH), F32), pltpu.VMEM((SUBLANES, LRU_WIDTH), F32)],
        compiler_params=_cp(("arbitrary", "arbitrary")), name="lru_fwd",
    )(xl, xl, conv_w, conv_b, wbd, ba, bx, lam)


def _att_consts():
    row = lax.broadcasted_iota(jnp.int32, (TQ, 2 * TK), 0)
    key = lax.broadcasted_iota(jnp.int32, (TQ, 2 * TK), 1) & (TK - 1)
    return key < row


def _sum_matrix(kind):
    j = lax.broadcasted_iota(jnp.int32, (2 * TK, 2 * TK), 0) & (TK - 1)
    s = lax.broadcasted_iota(jnp.int32, (2 * TK, 2 * TK), 1)
    pick = {"after": j > s, "upto": j <= s, "before": j < s}[kind]
    return jnp.where((s >= TK) | pick, 1.0, 0.0).astype(BF16)


def _hi_lo(x):
    hi = x.astype(BF16)
    return hi, (x - hi.astype(F32)).astype(BF16)


def _pair_sums(x, m):
    hi, lo = _hi_lo(x)
    out = []
    for hd in range(2):
        cols = slice(hd * TK, (hd + 1) * TK)
        out.append(_dot(jnp.concatenate([hi[:, cols], lo[:, cols]], axis=1), m))
    return [o[:, :TK] for o in out], [o[:, TK:] for o in out]


def _att_logits(qb, kbd):
    z = _dot(qb, kbd)
    lg = jnp.log(1.0 + jnp.exp(-jnp.abs(z)))
    lb = jnp.minimum(z, 0.0) - lg
    return lb, lb - z


def _head_diag(x, rows_first):
    n = x.shape[0] if rows_first else x.shape[1]
    idx = lax.broadcasted_iota(jnp.int32, x.shape, 0 if rows_first else 1)
    return jnp.where(idx < n // 2, x, 0), jnp.where(idx >= n // 2, x, 0)


def _band_tiles():
    nd = TQ // TK
    return [(jr, TK * max(jr, 0), TK * min(jr + BAND + 1, nd)) for jr in range(nd - 1, -BAND - 1, -1)]


def _rows_update(st, new, lo, hi):
    def one(x, y):
        pieces = ([x[:lo]] if lo else []) + [y] + ([x[hi:]] if hi < x.shape[0] else [])
        return pieces[0] if len(pieces) == 1 else jnp.concatenate(pieces, axis=0)
    return tuple(one(x, y) for x, y in zip(st, new))


def _scaled_q(q_ref, q0):
    return (q_ref[pl.ds(q0, TQ), :].astype(F32) * ATT_SCALE).astype(BF16)


def _qkv_specs(seq):
    n = SB_WIDTH // LANES
    return [pl.BlockSpec((seq, LANES), lambda e, p, off=off: (e, off * n + p)) for off in range(3)]


def _attn_fwd(qkv, seq, comm=None):
    t = qkv.shape[0]
    ne, nq, nk = t // seq, seq // TQ, seq // TK

    def body(ins, outs, scr):
        (q_ref, k_ref, v_ref), (o_ref, tot_ref, kmin_ref), (kbd_scr, vbd_scr) = ins, outs, scr
        causal = _att_consts()
        after = _sum_matrix("after")

        def prep(j, _):
            k0 = pl.multiple_of(j * TK, TK)
            top, bot = _head_diag(k_ref[pl.ds(k0, TK), :].T, True)
            kbd_scr[j] = jnp.concatenate([top, bot], axis=1)
            left, right = _head_diag(v_ref[pl.ds(k0, TK), :], False)
            vbd_scr[j] = jnp.concatenate([left, right], axis=0)
            return 0

        lax.fori_loop(0, nk, prep, 0)

        def block(j, qb, st, mask):
            c0, c1, oacc = st
            lb, l1 = _att_logits(qb, kbd_scr[j])
            if mask is not None:
                l1 = jnp.where(mask, l1, 0.0)
            (s0, s1), (r0, r1) = _pair_sums(l1, after)
            att = jnp.exp(lb + jnp.concatenate([s0 + c0, s1 + c1], axis=1))
            if mask is not None:
                att = jnp.where(mask, att, 0.0)
            return c0 + r0, c1 + r1, oacc + _dot(att.astype(BF16), vbd_scr[j])

        def general(qi, qb, st):
            for jj in reversed(range(TQ // TK)):
                lo = TK * jj
                new = block((TQ // TK) * qi + jj, qb[lo:], tuple(x[lo:] for x in st), causal[:TQ - lo])
                st = _rows_update(st, new, lo, TQ)

            npair = (TQ // TK // 2) * qi

            def more(its):
                return (its[0] < npair) & (jnp.max(jnp.maximum(its[1], its[2])) > SKIP_LOG)

            def kloop(its):
                j = 2 * (npair - its[0]) - 1
                return (its[0] + 1,) + block(j - 1, qb, block(j, qb, its[1:], None), None)

            done, c0, c1, oacc = lax.while_loop(more, kloop, (jnp.int32(0),) + st)
            return c0, c1, oacc, 2 * (npair - done)

        def short(qi, qb, st):
            for jr, lo, hi in _band_tiles():
                new = block((TQ // TK) * qi + jr, qb[lo:hi], tuple(x[lo:hi] for x in st),
                            causal[:hi - lo] if jr >= 0 else None)
                st = _rows_update(st, new, lo, hi)
            return st

        def qloop(qi, _):
            q0 = pl.multiple_of(qi * TQ, TQ)
            qb = _scaled_q(q_ref, q0)
            zero = jnp.zeros((TQ, TK), F32)
            st = (zero, zero, jnp.zeros((TQ, LANES), F32))

            def try_short():
                c0, c1, oacc = short(qi, qb, st)
                return lax.cond(jnp.max(jnp.maximum(c0, c1)) <= SKIP_LOG, lambda: (c0, c1, oacc, jnp.int32(-1)),
                                lambda: general(qi, qb, st))

            c0, c1, oacc, first = lax.cond(qi > 0, try_short, lambda: general(qi, qb, st))
            o_ref[pl.ds(q0, TQ), :] = oacc
            tot_ref[pl.ds(q0, TQ), :] = jnp.concatenate([c0, c1], axis=1)
            kmin_ref[pl.program_id(0), pl.program_id(1), qi] = first
            return 0

        lax.fori_loop(0, nq, qloop, 0)

    (o, tot, kmin), got = _call(
        body, comm, grid=(ne, SB_WIDTH // LANES), in_specs=_qkv_specs(seq),
        out_specs=[pl.BlockSpec((seq, LANES), lambda e, p: (e, p)), pl.BlockSpec((seq, 2 * TK), lambda e, p: (e, p)),
                   pl.BlockSpec(memory_space=pltpu.SMEM)],
        out_shape=[jax.ShapeDtypeStruct((t, SB_WIDTH), F32), jax.ShapeDtypeStruct((t, 2 * TK * SB_WIDTH // LANES), F32),
                   jax.ShapeDtypeStruct((ne, SB_WIDTH // LANES, nq), jnp.int32)],
        scratch_shapes=[pltpu.VMEM((nk, LANES, 2 * TK), BF16), pltpu.VMEM((nk, 2 * TK, LANES), BF16)],
        args=(qkv, qkv, qkv), name="attn_fwd")
    return o, tot, kmin, got


def _outproj(y_lru, o, x, ga, gb, w_out):
    t = x.shape[0]
    tm = min(ROW_TILE, t)

    def kern(y_ref, o_ref, x_ref, ga_ref, gb_ref, w_ref, h1_ref, mix_ref):
        yv, ov = y_ref[...], o_ref[...]
        mix = jnp.concatenate([yv * _rstd(yv) * ga_ref[...], ov * _rstd(ov) * gb_ref[...]], axis=1).astype(BF16)
        mix_ref[...] = mix
        h1_ref[...] = x_ref[...] + _dot(mix, w_ref[...])

    row = lambda c: pl.BlockSpec((tm, c), lambda i: (i, 0))
    vec = lambda c: pl.BlockSpec((1, c), lambda i: (0, 0))
    return pl.pallas_call(
        kern, grid=(t // tm,),
        in_specs=[row(LRU_WIDTH), row(SB_WIDTH), row(D_MODEL), vec(LRU_WIDTH), vec(SB_WIDTH),
                  pl.BlockSpec((D_MODEL, D_MODEL), lambda i: (0, 0))],
        out_specs=[row(D_MODEL), row(D_MODEL)],
        out_shape=[jax.ShapeDtypeStruct((t, D_MODEL), F32), jax.ShapeDtypeStruct((t, D_MODEL), BF16)],
        compiler_params=_cp(("parallel",)), name="outproj",
    )(y_lru, o, x, ga, gb, w_out)


def _mlp_loss(h1, g2, w_up, w_down, target, gf):
    t = h1.shape[0]
    tm, tf = min(ROW_TILE, t), 1024

    def kern(h1_ref, g_ref, wu_ref, wd_ref, t_ref, gf_ref, hn_ref, up_ref, u2_ref, dh_ref, dhb_ref, loss_ref, dg_ref):
        @pl.when(pl.program_id(0) == 0)
        def _():
            loss_ref[...] = jnp.zeros_like(loss_ref)
            dg_ref[...] = jnp.zeros_like(dg_ref)

        hv = h1_ref[...]
        hn = (hv * _rstd(hv) * g_ref[...]).astype(BF16)
        hn_ref[...] = hn
        h2 = hv
        for f in range(D_FF // tf):
            cols = slice(f * tf, (f + 1) * tf)
            up = jnp.maximum(_dot(hn, wu_ref[:, cols]), 0.0)
            u2 = (up * up).astype(BF16)
            up_ref[:, cols] = up.astype(BF16)
            u2_ref[:, cols] = u2
            h2 = h2 + _dot(u2, wd_ref[cols, :])

        g = gf_ref[...]
        err = h2 * _rstd(h2) * g - t_ref[...]
        lane = lax.broadcasted_iota(jnp.int32, (1, LANES), 1)
        loss_ref[...] += jnp.where(lane == 0, 0.5 * jnp.sum(err * err) / D_MODEL, 0.0)
        dx, dg = _rms_bwd(h2, g, err * (1.0 / D_MODEL))
        dh_ref[...] = dx
        dhb_ref[...] = dx.astype(BF16)
        dg_ref[...] += dg

    row = lambda c: pl.BlockSpec((tm, c), lambda i: (i, 0))
    vec = pl.BlockSpec((1, D_MODEL), lambda i: (0, 0))
    return pl.pallas_call(
        kern, grid=(t // tm,),
        in_specs=[row(D_MODEL), vec, _resident((D_MODEL, D_FF)), _resident((D_FF, D_MODEL)), row(D_MODEL), vec],
        out_specs=[row(D_MODEL), row(D_FF), row(D_FF), row(D_MODEL), row(D_MODEL), pl.BlockSpec((1, LANES), lambda i: (0, 0)), vec],
        out_shape=[jax.ShapeDtypeStruct((t, D_MODEL), BF16), jax.ShapeDtypeStruct((t, D_FF), BF16),
                   jax.ShapeDtypeStruct((t, D_FF), BF16), jax.ShapeDtypeStruct((t, D_MODEL), F32),
                   jax.ShapeDtypeStruct((t, D_MODEL), BF16), jax.ShapeDtypeStruct((1, LANES), F32),
                   jax.ShapeDtypeStruct((1, D_MODEL), F32)],
        compiler_params=_cp(("arbitrary",), VMEM_LIMIT_BIG), name="mlp_loss",
    )(h1, g2, w_up, w_down, target, gf)


def _mlp_bwd_pre(dh2, w_down, up):
    t = dh2.shape[0]
    tm, tf = min(ROW_TILE, t), 1024

    def kern(d_ref, w_ref, up_ref, o_ref):
        dv = d_ref[...]
        for f in range(D_FF // tf):
            cols = slice(f * tf, (f + 1) * tf)
            o_ref[:, cols] = (_dot_nt(dv, w_ref[cols, :]) * (2.0 * up_ref[:, cols].astype(F32))).astype(BF16)

    row = lambda c: pl.BlockSpec((tm, c), lambda i: (i, 0))
    return pl.pallas_call(
        kern, grid=(t // tm,), in_specs=[row(D_MODEL), _resident((D_FF, D_MODEL)), row(D_FF)],
        out_specs=row(D_FF), out_shape=jax.ShapeDtypeStruct((t, D_FF), BF16),
        compiler_params=_cp(("parallel",)), name="mlp_bwd_pre",
    )(dh2, w_down, up)


def _proj_bwd_norm(dys, w, x, g, resid, name, comm=None, bf16_copy=False):
    t = x.shape[0]
    tm = min(ROW_TILE, t)
    widths = [dy.shape[1] for dy in dys]
    n = len(dys)

    def body(ins, outs, _):
        dy_refs, (w_ref, x_ref, g_ref, r_ref), (dx_ref, dg_ref) = ins[:n], ins[n:], outs[:2]

        @pl.when(pl.program_id(0) == 0)
        def _():
            dg_ref[...] = jnp.zeros_like(dg_ref)

        off, dxn = 0, None
        for dy_ref, wd in zip(dy_refs, widths):
            part = _dot_nt(dy_ref[...], w_ref[:, off:off + wd])
            dxn = part if dxn is None else dxn + part
            off += wd
        dx, dg = _rms_bwd(x_ref[...], g_ref[...], dxn)
        dx = r_ref[...] + dx
        dx_ref[...] = dx
        dg_ref[...] += dg
        if bf16_copy:
            outs[2][...] = dx.astype(BF16)

    row = lambda c: pl.BlockSpec((tm, c), lambda i: (i, 0))
    vec = pl.BlockSpec((1, D_MODEL), lambda i: (0, 0))
    outs, got = _call(
        body, comm, grid=(t // tm,), in_specs=[row(wd) for wd in widths] + [_resident(w.shape), row(D_MODEL), vec, row(D_MODEL)],
        out_specs=[row(D_MODEL), vec] + [row(D_MODEL)] * bf16_copy,
        out_shape=[jax.ShapeDtypeStruct((t, D_MODEL), F32), jax.ShapeDtypeStruct((1, D_MODEL), F32)]
        + [jax.ShapeDtypeStruct((t, D_MODEL), BF16)] * bf16_copy,
        scratch_shapes=[], args=(*dys, w, x, g, resid), name=name)
    return outs[0], (outs[2] if bf16_copy else None), outs[1], got


def _dw_in(xn, pieces):
    t = xn.shape[0]
    tk = min(2 * ROW_TILE, t)
    widths = [p.shape[1] for p in pieces]
    shard = IN_COLS // N_CHIPS

    def windows(j):
        out, off = [], 0
        for i, wd in enumerate(widths):
            a, b = max(j * shard, off), min((j + 1) * shard, off + wd)
            if a < b:
                assert (a - off) % LANES == 0 and (b - off) % LANES == 0
                out.append((i, a - off, b - off))
            off += wd
        return out

    def kern(x_ref, *refs):
        p_refs, o_ref, ob_ref = refs[:-2], refs[-2], refs[-1]

        @pl.when(pl.program_id(0) == 0)
        def _():
            o_ref[...] = jnp.zeros_like(o_ref)

        xv = x_ref[...]
        for j in range(N_CHIPS):
            cols = jnp.concatenate([p_refs[i][:, a:b] for i, a, b in windows(j)], axis=1)
            o_ref[j] += _dot_tn(xv, cols)

        @pl.when(pl.program_id(0) == t // tk - 1)
        def _():
            ob_ref[...] = o_ref[...].astype(BF16)

    row = lambda c: pl.BlockSpec((tk, c), lambda k: (k, 0))
    whole = pl.BlockSpec((N_CHIPS, D_MODEL, shard), lambda k: (0, 0, 0))
    return pl.pallas_call(
        kern, grid=(t // tk,), in_specs=[row(D_MODEL)] + [row(wd) for wd in widths], out_specs=[whole, whole],
        out_shape=[jax.ShapeDtypeStruct((N_CHIPS, D_MODEL, shard), F32), jax.ShapeDtypeStruct((N_CHIPS, D_MODEL, shard), BF16)],
        compiler_params=_cp(("arbitrary",)), name="dw_in",
    )(xn, *pieces)


def _outproj_bwd(dh1, w_out, y_lru, o, ga, gb, comm=None):
    t = dh1.shape[0]
    tm = min(ROW_TILE, t)

    def body(ins, outs, _):
        (d_ref, w_ref, y_ref, o_ref, ga_ref, gb_ref), (dy_ref, do_ref, dga_ref, dgb_ref) = ins, outs

        @pl.when(pl.program_id(0) == 0)
        def _():
            dga_ref[...] = jnp.zeros_like(dga_ref)
            dgb_ref[...] = jnp.zeros_like(dgb_ref)

        dmix = _dot_nt(d_ref[...], w_ref[...])
        dy, dga = _rms_bwd(y_ref[...], ga_ref[...], dmix[:, :LRU_WIDTH])
        do, dgb = _rms_bwd(o_ref[...], gb_ref[...], dmix[:, LRU_WIDTH:])
        dy_ref[...] = dy
        do_ref[...] = do
        dga_ref[...] += dga
        dgb_ref[...] += dgb

    row = lambda c: pl.BlockSpec((tm, c), lambda i: (i, 0))
    vec = pl.BlockSpec((1, LRU_WIDTH), lambda i: (0, 0))
    half = jax.ShapeDtypeStruct((t, LRU_WIDTH), F32)
    gsum = jax.ShapeDtypeStruct((1, LRU_WIDTH), F32)
    outs, got = _call(body, comm, grid=(t // tm,),
                      in_specs=[row(D_MODEL), _resident((D_MODEL, D_MODEL)), row(LRU_WIDTH), row(SB_WIDTH), vec, vec],
                      out_specs=[row(LRU_WIDTH), row(SB_WIDTH), vec, vec], out_shape=[half, half, gsum, gsum],
                      scratch_shapes=[], args=(dh1, w_out, y_lru, o, ga, gb), name="outproj_bwd")
    return (*outs, got)


def _attn_bwd(qkv, do, tot, kmin, seq):
    t = qkv.shape[0]
    ne, nq, nk = t // seq, seq // TQ, seq // TK

    def kern(q_ref, k_ref, v_ref, do_ref, tot_ref, kmin_ref, dq_ref, dk_ref, dv_ref,
             kbd_scr, vtbd_scr, kbd2_scr, dkt_scr, dvt_scr):
        causal = _att_consts()
        upto, before = _sum_matrix("upto"), _sum_matrix("before")

        def prep(j, _):
            k0 = pl.multiple_of(j * TK, TK)
            kb = k_ref[pl.ds(k0, TK), :]
            top, bot = _head_diag(kb.T, True)
            kbd_scr[j] = jnp.concatenate([top, bot], axis=1)
            top, bot = _head_diag(v_ref[pl.ds(k0, TK), :].T, True)
            vtbd_scr[j] = jnp.concatenate([top, bot], axis=1)
            left, right = _head_diag(kb, False)
            kbd2_scr[j] = jnp.concatenate([left, right], axis=0)
            dkt_scr[j] = jnp.zeros((LANES, 2 * TK), F32)
            dvt_scr[j] = jnp.zeros((LANES, 2 * TK), F32)
            return 0

        lax.fori_loop(0, nk, prep, 0)

        def block(j, qb, qt, dob, dot_, totb, st, mask):
            f0, f1, p0, p1, dqacc = st
            lb, l1 = _att_logits(qb, kbd_scr[j])
            if mask is not None:
                l1 = jnp.where(mask, l1, 0.0)
            (s0, s1), (r0, r1) = _pair_sums(l1, upto)
            att = jnp.exp(lb + (totb - jnp.concatenate([s0 + f0, s1 + f1], axis=1)))
            if mask is not None:
                att = jnp.where(mask, att, 0.0)
            pw = att * _dot(dob, vtbd_scr[j])
            (e0, e1), (t0, t1) = _pair_sums(pw, before)
            dz = pw - jnp.exp(lb) * (pw + jnp.concatenate([e0 + p0, e1 + p1], axis=1))
            if mask is not None:
                dz = jnp.where(mask, dz, 0.0)
            dzb = dz.astype(BF16)
            dkt_scr[j] += _dot(qt, dzb)
            dvt_scr[j] += _dot(dot_, att.astype(BF16))
            return f0 + r0, f1 + r1, p0 + t0, p1 + t1, dqacc + _dot(dzb, kbd2_scr[j])

        def qloop(qi, _):
            q0 = pl.multiple_of(qi * TQ, TQ)
            qb = _scaled_q(q_ref, q0)
            qt = qb.T
            do32 = do_ref[pl.ds(q0, TQ), :]
            dob = do32.astype(BF16)
            dot_ = dob.T
            totb = tot_ref[pl.ds(q0, TQ), :]
            zero = jnp.zeros((TQ, TK), F32)
            st = (zero, zero, zero, zero, jnp.zeros((TQ, LANES), F32))

            k0 = kmin_ref[pl.program_id(0), pl.program_id(1), qi]

            def tile(j, lo, hi, st, masked):
                new = block(j, qb[lo:hi], qt[:, lo:hi], dob[lo:hi], dot_[:, lo:hi], totb[lo:hi],
                            tuple(x[lo:hi] for x in st), causal[:hi - lo] if masked else None)
                return _rows_update(st, new, lo, hi)

            def general():
                def kloop(it, st):
                    j = k0 + 2 * it
                    return block(j + 1, qb, qt, dob, dot_, totb, block(j, qb, qt, dob, dot_, totb, st, None), None)

                out = lax.fori_loop(0, ((TQ // TK) * qi - k0) // 2, kloop, st)
                for jj in range(TQ // TK):
                    out = tile((TQ // TK) * qi + jj, TK * jj, TQ, out, True)
                return out

            def short():
                out = st
                for jr, lo, hi in reversed(_band_tiles()):
                    out = tile((TQ // TK) * qi + jr, lo, hi, out, jr >= 0)
                return out

            st = lax.cond(k0 < 0, short, general)
            dq_ref[pl.ds(q0, TQ), :] = (st[4] * ATT_SCALE).astype(BF16)
            return 0

        lax.fori_loop(0, nq, qloop, 0)

        def finish(j, _):
            k0 = pl.multiple_of(j * TK, TK)
            head0 = lax.broadcasted_iota(jnp.int32, (LANES, TK), 0) < DH
            for src, dst in ((dkt_scr, dk_ref), (dvt_scr, dv_ref)):
                acc = src[j]
                dst[pl.ds(k0, TK), :] = jnp.where(head0, acc[:, :TK], acc[:, TK:]).astype(BF16).T
            return 0

        lax.fori_loop(0, nk, finish, 0)

    blk = pl.BlockSpec((seq, LANES), lambda e, p: (e, p))
    grad = jax.ShapeDtypeStruct((t, SB_WIDTH), BF16)
    return pl.pallas_call(
        kern, grid=(ne, SB_WIDTH // LANES),
        in_specs=_qkv_specs(seq) + [blk, pl.BlockSpec((seq, 2 * TK), lambda e, p: (e, p)),
                                    pl.BlockSpec(memory_space=pltpu.SMEM)],
        out_specs=[blk, blk, blk], out_shape=[grad, grad, grad],
        scratch_shapes=[pltpu.VMEM((nk, LANES, 2 * TK), BF16), pltpu.VMEM((nk, LANES, 2 * TK), BF16),
                        pltpu.VMEM((nk, 2 * TK, LANES), BF16), pltpu.VMEM((nk, LANES, 2 * TK), F32),
                        pltpu.VMEM((nk, LANES, 2 * TK), F32)],
        compiler_params=_cp(("parallel", "parallel")), name="attn_bwd",
    )(qkv, qkv, qkv, do, tot, kmin)


def _lru_bwd(xl, h, dy, conv_w, conv_b, wbd, ba, bx, lam, seq, comm=None):
    t = xl.shape[0]
    tc = min(512, seq)
    nc = seq // tc
    nb = tc // SUBLANES


    def body(ins, outs, scr):
        u_ref, g_ref, up_ref, h_ref, hp_ref, dy_ref, cw_ref, cb_ref, wbd_ref, ba_ref, bx_ref, lam_ref = ins
        (dxl_ref, small_ref, dwbd_ref), (lnext_ref, anext_ref, dcnext_ref) = outs, scr
        e, ci = pl.program_id(0), pl.program_id(1)
        first = ci == nc - 1

        @pl.when((e == 0) & (ci == 0))
        def _():
            small_ref[...] = jnp.zeros_like(small_ref)
            dwbd_ref[...] = jnp.zeros_like(dwbd_ref)

        @pl.when(ci == 0)
        def _():
            lnext_ref[...] = jnp.zeros_like(lnext_ref)
            anext_ref[...] = jnp.zeros_like(anext_ref)
            dcnext_ref[...] = jnp.zeros_like(dcnext_ref)

        u, g = u_ref[...], g_ref[...]
        keep = jnp.where(first, 0.0, 1.0)
        taps = _conv_taps(keep * up_ref[...], u)
        c = cb_ref[...]
        for k in range(CONV_WIDTH):
            c = c + taps[k] * cw_ref[k:k + 1, :]
        lam = lam_ref[...]
        sp = _softplus(-lam)
        r, i, a, mult, inv_mult = _lru_gates(c, wbd_ref, ba_ref[...], bx_ref[...], sp)
        gel, th = _gelu(g)
        dyv, hv = dy_ref[...], h_ref[...]
        dg = dyv * hv * _gelu_grad(g, th)

        aa, bb = _scan_rev(_shift_up(a, 1, anext_ref[0:1, :]), dyv * gel)
        lt = bb + aa * lnext_ref[0:1, :]
        lnext_ref[0:1, :] = _row_of(lt, 0)
        anext_ref[0:1, :] = _row_of(a, 0)

        hprev = _shift_down(hv, 1, keep * hp_ref[SUBLANES - 1:SUBLANES, :])
        da = lt * hprev
        dmult = lt * i * c
        di = lt * mult * c
        dc = lt * mult * i
        dla = da * a - dmult * (a * a) * inv_mult
        dga = dla * ((-LRU_C) * sp) * r * (1.0 - r)
        dgx = di * i * (1.0 - i)
        small_ref[7:8, :] += jnp.sum(dla * r, axis=0, keepdims=True) * (LRU_C * _sigmoid(-lam))
        small_ref[5:6, :] += jnp.sum(dga, axis=0, keepdims=True)
        small_ref[6:7, :] += jnp.sum(dgx, axis=0, keepdims=True)

        dcs = []
        for p in range(LRU_WIDTH // LANES):
            cols = slice(LANES * p, LANES * (p + 1))
            dgax = jnp.concatenate([dga[:, cols], dgx[:, cols]], axis=1).astype(BF16)
            dcs.append(_dot_nt(dgax, wbd_ref[p]))
            dwbd_ref[p] += _dot_tn(c[:, cols].astype(BF16), dgax)
        dc = dc + jnp.concatenate(dcs, axis=1)
        small_ref[4:5, :] += jnp.sum(dc, axis=0, keepdims=True)

        catd = jnp.concatenate([dc, dcnext_ref[...]], axis=0)
        du = dc * cw_ref[CONV_WIDTH - 1:CONV_WIDTH, :]
        for j in range(1, CONV_WIDTH):
            du = du + pltpu.roll(catd, tc + SUBLANES - j, 0)[:tc] * cw_ref[CONV_WIDTH - 1 - j:CONV_WIDTH - j, :]
        dcnext_ref[...] = dc[:SUBLANES]
        for k in range(CONV_WIDTH):
            small_ref[k:k + 1, :] += jnp.sum(dc * taps[k], axis=0, keepdims=True)
        dxl_ref[:, :LRU_WIDTH] = du.astype(BF16)
        dxl_ref[:, LRU_WIDTH:] = dg.astype(BF16)

    rev = lambda e, c: e * nc + (nc - 1 - c)
    chunk = lambda col: pl.BlockSpec((tc, LRU_WIDTH), lambda e, c: (rev(e, c), col))
    prev8 = pl.BlockSpec((SUBLANES, LRU_WIDTH), lambda e, c: (jnp.maximum(rev(e, c) * nb - 1, 0), 0))
    outs, got = _call(
        body, comm, grid=(t // seq, nc),
        in_specs=[chunk(0), chunk(1), prev8, chunk(0), prev8, chunk(0)] + _lru_param_specs(2),
        out_specs=[pl.BlockSpec((tc, 2 * LRU_WIDTH), lambda e, c: (rev(e, c), 0)),
                   pl.BlockSpec((SUBLANES, LRU_WIDTH), lambda e, c: (0, 0)),
                   pl.BlockSpec((LRU_WIDTH // LANES, LANES, 2 * LANES), lambda e, c: (0, 0, 0))],
        out_shape=[jax.ShapeDtypeStruct((t, 2 * LRU_WIDTH), BF16), jax.ShapeDtypeStruct((SUBLANES, LRU_WIDTH), F32),
                   jax.ShapeDtypeStruct((LRU_WIDTH // LANES, LANES, 2 * LANES), F32)],
        scratch_shapes=[pltpu.VMEM((SUBLANES, LRU_WIDTH), F32)] * 3,
        args=(xl, xl, xl, h, h, dy, conv_w, conv_b, wbd, ba, bx, lam), name="lru_bwd")
    return (*outs, got)


def _adam_math(w, g, m, v):
    m2 = ADAM_B1 * m + (1.0 - ADAM_B1) * g
    v2 = ADAM_B2 * v + (1.0 - ADAM_B2) * (g * g)
    m_hat = m2 / (1.0 - ADAM_B1 ** ADAM_STEP)
    v_hat = v2 / (1.0 - ADAM_B2 ** ADAM_STEP)
    return -ADAM_LR * (m_hat / (jnp.sqrt(v_hat) + ADAM_EPS) + ADAM_WD * w), m2, v2


def _adamw(w, g, m, v, name):
    rows, cols = w.shape
    tr = ROW_TILE if rows % ROW_TILE == 0 else rows

    def kern(w_ref, g_ref, m_ref, v_ref, d_ref, m2_ref, v2_ref):
        d_ref[...], m2_ref[...], v2_ref[...] = _adam_math(w_ref[...], g_ref[...], m_ref[...], v_ref[...])

    blk = pl.BlockSpec((tr, cols), lambda i: (i, 0))
    out = jax.ShapeDtypeStruct((rows, cols), F32)
    return pl.pallas_call(kern, grid=(rows // tr,), in_specs=[blk] * 4, out_specs=[blk] * 3, out_shape=[out] * 3,
                          compiler_params=_cp(("parallel",)), name=name)(w, g, m, v)


def _adamw_small(ws, gs, ms, vs):
    n = len(ws)

    def kern(*refs):
        for k in range(n):
            outs = _adam_math(refs[k][...], refs[n + k][...], refs[2 * n + k][...], refs[3 * n + k][...])
            for j in range(3):
                refs[(4 + j) * n + k][...] = outs[j]

    vm = pl.BlockSpec(memory_space=pltpu.VMEM)
    out = pl.pallas_call(kern, in_specs=[vm] * (4 * n), out_specs=[vm] * (3 * n),
                         out_shape=[jax.ShapeDtypeStruct(w.shape, F32) for w in ws] * 3, name="adamw_small")(*ws, *gs, *ms, *vs)
    return out[:n], out[n:2 * n], out[2 * n:]


def _pair_add_all(gs, gots, core):
    n = len(gs)
    halves = [g.shape[1] // 2 for g in gs]

    def kern(c_ref, *refs):
        for w in range(n):
            refs[2 * n + w][...] = (refs[w][...] + refs[n + w][...]).astype(BF16)

    own = [pl.BlockSpec((None, h, g.shape[2]), lambda j, c_ref: (j, c_ref[0], 0)) for g, h in zip(gs, halves)]
    slab = [pl.BlockSpec((None, h, g.shape[2]), lambda j, c_ref: (j, 0, 0)) for g, h in zip(gs, halves)]
    return pl.pallas_call(
        kern, grid_spec=pltpu.PrefetchScalarGridSpec(num_scalar_prefetch=1, grid=(N_CHIPS,), in_specs=own + slab,
                                                     out_specs=slab),
        out_shape=[jax.ShapeDtypeStruct((N_CHIPS, h, g.shape[2]), BF16) for g, h in zip(gs, halves)],
        compiler_params=_cp(("parallel",)), name="pair_add_all",
    )(core, *gs, *gots)


def _chip_add(part, got, place):
    _, half, cols = part.shape
    tr = min(ROW_TILE, half)
    nt = half // tr

    def kern(p_ref, part_ref, got_ref, out_ref):
        out_ref[...] = (part_ref[...].astype(F32) + got_ref[0].astype(F32) + got_ref[1].astype(F32)
                        + got_ref[2].astype(F32))

    return pl.pallas_call(
        kern, grid_spec=pltpu.PrefetchScalarGridSpec(
            num_scalar_prefetch=1, grid=(nt,),
            in_specs=[pl.BlockSpec((None, tr, cols), lambda i, p_ref: (p_ref[0], i, 0)),
                      pl.BlockSpec((3, tr, cols), lambda i, p_ref: (0, i, 0))],
            out_specs=pl.BlockSpec((tr, cols), lambda i, p_ref: (p_ref[1] * nt + i, 0))),
        out_shape=jax.ShapeDtypeStruct((2 * half, cols), F32),
        compiler_params=_cp(("parallel",)), name="chip_add",
    )(place, part, got)


def _chip_add8(g, got, place):
    _, rows, cols = g.shape
    half = rows // 2
    tr = min(ROW_TILE, half)
    nt = half // tr

    def kern(p_ref, g_ref, got_ref, out_ref):
        acc = g_ref[...]
        for k in range(N_DEV - 1):
            acc = acc + got_ref[k].astype(F32)
        out_ref[...] = acc

    return pl.pallas_call(
        kern, grid_spec=pltpu.PrefetchScalarGridSpec(
            num_scalar_prefetch=1, grid=(nt,),
            in_specs=[pl.BlockSpec((None, tr, cols), lambda i, p_ref: (p_ref[0], p_ref[1] * nt + i, 0)),
                      pl.BlockSpec((N_DEV - 1, tr, cols), lambda i, p_ref: (0, i, 0))],
            out_specs=pl.BlockSpec((tr, cols), lambda i, p_ref: (p_ref[1] * nt + i, 0))),
        out_shape=jax.ShapeDtypeStruct((rows, cols), F32),
        compiler_params=_cp(("parallel",)), name="chip_add8",
    )(place, g, got)


def _finale(packed, fulls):
    rows, n = packed.shape[0], len(fulls)
    half = rows // 2
    assert half % SUBLANES == 0

    def kern(in_ref, *refs):
        ins, out_ref, outs = refs[:n], refs[n], refs[n + 1:2 * n + 1]
        pair_slot, chip_slots, pair_sems, chip_send, chip_recv, back_sems, join_send, join_recv = refs[2 * n + 1:]
        x, y, c = _place()
        chip, sibling = 2 * x + y, (x, y, 1 - c)
        mine = pl.ds(pl.multiple_of(c * half, SUBLANES), half)
        joins = []
        for w in range(n):
            rws = pl.ds(c * (ins[w].shape[0] // 2), ins[w].shape[0] // 2)
            joins.append(pltpu.make_async_remote_copy(
                src_ref=ins[w].at[rws, :], dst_ref=outs[w].at[rws, :], send_sem=join_send.at[w],
                recv_sem=join_recv.at[w], device_id=sibling, device_id_type=MESH))
        swap = pltpu.make_async_remote_copy(src_ref=in_ref, dst_ref=pair_slot, send_sem=pair_sems.at[0],
                                            recv_sem=pair_sems.at[1], device_id=sibling, device_id_type=MESH)
        for cp in joins + [swap]:
            cp.start()
        swap.wait()
        chip_slots[chip] = in_ref[mine, :] + pair_slot[mine, :]
        spread = [pltpu.make_async_remote_copy(
            src_ref=chip_slots.at[chip], dst_ref=chip_slots.at[chip], send_sem=chip_send.at[k], recv_sem=chip_recv.at[k],
            device_id=(*other, c), device_id_type=MESH) for k, other in enumerate(_other_chips(x, y))]
        for cp in spread:
            cp.start()
        for cp in spread:
            cp.wait()
        out_ref[mine, :] = chip_slots[0] + chip_slots[1] + chip_slots[2] + chip_slots[3]
        back = pltpu.make_async_remote_copy(src_ref=out_ref.at[mine, :], dst_ref=out_ref.at[mine, :], send_sem=back_sems.at[0],
                                            recv_sem=back_sems.at[1], device_id=sibling, device_id_type=MESH)
        back.start()
        for cp in [back] + joins:
            cp.wait()

    vm = pl.BlockSpec(memory_space=pltpu.VMEM)
    out = pl.pallas_call(
        kern, in_specs=[vm] + [ANY] * n, out_specs=[vm] + [ANY] * n,
        out_shape=[jax.ShapeDtypeStruct((rows, LANES), F32)] + [jax.ShapeDtypeStruct(f.shape, f.dtype) for f in fulls],
        input_output_aliases={w + 1: w + 1 for w in range(n)},
        scratch_shapes=[pltpu.VMEM((rows, LANES), F32), pltpu.VMEM((N_CHIPS, half, LANES), F32),
                        pltpu.SemaphoreType.DMA((2,)), pltpu.SemaphoreType.DMA((3,)), pltpu.SemaphoreType.DMA((3,)),
                        pltpu.SemaphoreType.DMA((2,)), pltpu.SemaphoreType.DMA((n,)), pltpu.SemaphoreType.DMA((n,))],
        name="finale",
    )(packed, *fulls)
    return out[0], list(out[1:])


SMALL = ["norm1_g", "conv_w", "conv_b", "lru_w_a", "lru_b_a", "lru_w_x", "lru_b_x", "lru_lambda", "lru_out_g", "sb_out_g",
         "norm2_g", "final_g"]
BIG = ["w_in", "w_out", "w_up", "w_down"]
WEIGHTS = ["norm1_g", "w_in", "conv_w", "conv_b", "lru_w_a", "lru_b_a", "lru_w_x", "lru_b_x", "lru_lambda", "lru_out_g",
           "sb_out_g", "w_out", "norm2_g", "w_up", "w_down", "final_g"]


def _pack(arrays):
    flat = []
    for a in arrays:
        a = a.reshape(-1).astype(F32)
        flat.append(jnp.pad(a, (0, (-a.shape[0]) % LANES)))
    v = jnp.concatenate(flat)
    v = jnp.pad(v, (0, (-v.shape[0]) % (LANES * 2 * SUBLANES)))
    return v.reshape(-1, LANES)


def _unpack(packed, shapes):
    v, out, off = packed.reshape(-1), [], 0
    for shp in shapes:
        size = math.prod(shp)
        out.append(v[off:off + size].reshape(shp))
        off += size + (-size) % LANES
    return out


def _blockdiag_pairs(w):
    w = w.reshape(4, 2, DH, DH)
    z = jnp.zeros((4, DH, DH), w.dtype)
    return jnp.concatenate([jnp.concatenate([w[:, 0], z], axis=2), jnp.concatenate([z, w[:, 1]], axis=2)], axis=1)


def _blockdiag_unpairs(wbd):
    return jnp.stack([wbd[:, :DH, :DH], wbd[:, DH:, DH:]], axis=1).reshape(8, DH, DH)


def _full_cols(g):
    return jnp.transpose(g, (1, 0, 2)).reshape(g.shape[1], N_CHIPS * g.shape[2])


def _local_step(x2, tgt, seq, norm1_g, w_in, conv_w, conv_b, w_a, b_a, w_x, b_x, lru_lambda, lru_out_g, sb_out_g, rest,
                norm2_g, final_g, place):
    wbd = jnp.concatenate([_blockdiag_pairs(w_a), _blockdiag_pairs(w_x)], axis=2).astype(BF16)
    ba, bx = b_a.reshape(1, LRU_WIDTH), b_x.reshape(1, LRU_WIDTH)
    gf = final_g.reshape(1, D_MODEL)

    xn, got = _norm1(x2, norm1_g, ("gather2", [w_in]))
    w_in_f = _full_cols(got[0])
    xl, qkv, got = _inproj(xn, w_in_f, ("gather", [conv_w]))
    conv_w_f = _full_cols(got[0])
    h, y_lru = _lru_fwd(xl, conv_w_f, conv_b, wbd, ba, bx, lru_lambda, seq)
    o, tot, kmin, got = _attn_fwd(qkv, seq, ("gather2", rest))
    w_out_f, w_up_f, w_down_f = got[0].reshape(D_MODEL, D_MODEL), _full_cols(got[1]), got[2].reshape(D_FF, D_MODEL)
    h1, mix = _outproj(y_lru, o, x2, lru_out_g, sb_out_g, w_out_f)
    hn, up, u2, dh2, dh2b, loss_part, d_final = _mlp_loss(h1, norm2_g, w_up_f, w_down_f, tgt, gf)

    dpre = _mlp_bwd_pre(dh2b, w_down_f, up)
    g_w_down = _weight_grad(u2, dh2b, "dw_down", 2 * ROW_TILE).reshape(N_CHIPS, D_FF // N_CHIPS, D_MODEL)
    g_w_up = _weight_grad(hn, dpre, "dw_up", 2 * ROW_TILE, split_cols=True)
    dh1, dh1b, d_norm2, _ = _proj_bwd_norm([dpre], w_up_f, h1, norm2_g, dh2, "mlp_bwd_in", bf16_copy=True)
    g_w_out = _weight_grad(mix, dh1b, "dw_out", 4 * ROW_TILE).reshape(N_CHIPS, D_MODEL // N_CHIPS, D_MODEL)
    late = [g_w_out, g_w_up, g_w_down]
    dy_lru, do, d_ga, d_gb, swapped = _outproj_bwd(dh1b, w_out_f, y_lru, o, lru_out_g, sb_out_g, ("swap", late))
    parts = _pair_add_all(late, swapped, place[1:])
    dq, dk, dv = _attn_bwd(qkv, do, tot, kmin, seq)
    dxl, lru_small, d_wbd, got = _lru_bwd(xl, h, dy_lru, conv_w_f, conv_b, wbd, ba, bx, lru_lambda, seq,
                                          ("exchange", parts))
    late = [_chip_add(p, r, place) for p, r in zip(parts, got)]
    g_w_in, g_w_in_b = _dw_in(xn, [dxl, dq, dk, dv])
    dx, _, d_norm1, got = _proj_bwd_norm([dxl, dq, dk, dv], w_in_f, x2, norm1_g, dh1, "inproj_bwd",
                                         ("exchange8", [g_w_in_b]))
    g_w_in = _chip_add8(g_w_in, got[0], place)
    small_parts = {
        "norm1_g": d_norm1, "conv_w": lru_small[:CONV_WIDTH], "conv_b": lru_small[4:5],
        "lru_w_a": _blockdiag_unpairs(d_wbd[:, :, :LANES]), "lru_b_a": lru_small[5:6],
        "lru_w_x": _blockdiag_unpairs(d_wbd[:, :, LANES:]), "lru_b_x": lru_small[6:7], "lru_lambda": lru_small[7:8],
        "lru_out_g": d_ga, "sb_out_g": d_gb, "norm2_g": d_norm2, "final_g": d_final,
    }
    return loss_part, dx, [g_w_in] + late, small_parts


def kernel(x, norm1_g, w_in, conv_w, conv_b, lru_w_a, lru_b_a, lru_w_x, lru_b_x, lru_lambda, lru_out_g, sb_out_g, w_out, norm2_g, w_up, w_down, final_g, loss_target, m_norm1_g, m_w_in, m_conv_w, m_conv_b, m_lru_w_a, m_lru_b_a, m_lru_w_x, m_lru_b_x, m_lru_lambda, m_lru_out_g, m_sb_out_g, m_w_out, m_norm2_g, m_w_up, m_w_down, m_final_g, v_norm1_g, v_w_in, v_conv_w, v_conv_b, v_lru_w_a, v_lru_b_a, v_lru_w_x, v_lru_b_x, v_lru_lambda, v_lru_out_g, v_sb_out_g, v_w_out, v_norm2_g, v_w_up, v_w_down, v_final_g):
    given = dict(locals())
    ne, seq, _ = x.shape
    t = ne * seq
    xi, yi, ci = _place()
    place = jnp.stack([2 * xi + yi, ci]).astype(jnp.int32)

    loss_part, dx, halves, small_parts = _local_step(
        x.reshape(t, D_MODEL), loss_target.reshape(t, D_MODEL), seq, norm1_g, w_in[0].astype(BF16), conv_w[0], conv_b,
        lru_w_a[0], lru_b_a, lru_w_x[0], lru_b_x, lru_lambda, lru_out_g, sb_out_g,
        [w_out[0].astype(BF16), w_up[0].astype(BF16), w_down[0].astype(BF16)], norm2_g, final_g, place)

    full_shapes = {n: ((CONV_WIDTH, LRU_WIDTH) if n == "conv_w" else given[n].shape) for n in SMALL}
    red, fulls = _finale(_pack([small_parts[n] for n in SMALL] + [loss_part]), halves)
    red_list = _unpack(red, [full_shapes[n] for n in SMALL] + [(1, LANES)])
    grads = dict(zip(SMALL, red_list[:-1]))
    loss = red_list[-1][0, 0]
    grads["conv_w"] = lax.dynamic_slice_in_dim(grads["conv_w"], place[0] * (LRU_WIDTH // N_CHIPS), LRU_WIDTH // N_CHIPS,
                                               axis=1).reshape(conv_w.shape)
    for n, full in zip(BIG, fulls):
        grads[n] = full.reshape(given[n].shape)

    delta, new_m, new_v = {}, {}, {}
    for n in BIG:
        shp = given[n].shape
        d, m2, v2 = _adamw(given[n][0], grads[n][0], given["m_" + n][0], given["v_" + n][0], "adamw_" + n)
        delta[n], new_m[n], new_v[n] = d.reshape(shp), m2.reshape(shp), v2.reshape(shp)
    as2d = lambda a: a.reshape(-1, a.shape[-1])
    ds, m2s, v2s = _adamw_small([as2d(given[n]) for n in SMALL], [as2d(grads[n]) for n in SMALL],
                                [as2d(given["m_" + n]) for n in SMALL], [as2d(given["v_" + n]) for n in SMALL])
    for n, dd, mm, vv in zip(SMALL, ds, m2s, v2s):
        shp = given[n].shape
        delta[n], new_m[n], new_v[n] = dd.reshape(shp), mm.reshape(shp), vv.reshape(shp)

    return (loss, dx.reshape(x.shape), *[grads[n] for n in WEIGHTS], *[delta[n] for n in WEIGHTS],
            *[new_m[n] for n in WEIGHTS], *[new_v[n] for n in WEIGHTS])
```
